```python
import math
import jax, jax.numpy as jnp
from jax import lax
import numpy as np

D_MODEL = 4096
BATCH = 2
SEQ = 4096
DEPTH = 2

MLA_HEADS = 16
MLA_NOPE = 128
MLA_ROPE = 64
MLA_V = 128
Q_LORA = 1024
KV_LORA = 512
MLA_OUT = MLA_HEADS * MLA_V
CONV_CH = 1024
CONV_W = 3
MOBA_HEADS = 8
MOBA_HD = 128
MOBA_OUT = MOBA_HEADS * MOBA_HD
MOBA_BLOCK = 256
MOBA_TOPK = 3
MOBA_QCHUNK = 32
D_MIX = MLA_OUT + CONV_CH + MOBA_OUT
ROPE_THETA = 10000.0
ATTN_QBLOCK = 128
EPS = 1e-6
N_MOD = 6
D_IN = Q_LORA + KV_LORA + MLA_ROPE + 3 * CONV_CH + 3 * MOBA_OUT
IN_SPLITS = (Q_LORA, Q_LORA + KV_LORA, Q_LORA + KV_LORA + MLA_ROPE,
             Q_LORA + KV_LORA + MLA_ROPE + 3 * CONV_CH)
N_EXPERTS = 16
N_GROUPS = 4
EXPERTS_PER_GROUP = N_EXPERTS // N_GROUPS
TOP_K = 2
D_FF = 1024
MOE_BLOCK = 256

kernel_name = "hybrid_mla_conv_moba_grouped_moe"


def rms_norm(x, g):
    xf = x.astype(jnp.float32)
    y = xf * lax.rsqrt(jnp.mean(xf * xf, axis=-1, keepdims=True) + EPS)
    return (y * g.astype(jnp.float32)).astype(x.dtype)


def rope_tables(seq, dim):
    inv = 1.0 / (ROPE_THETA ** (jnp.arange(0, dim, 2, dtype=jnp.float32) / dim))
    ang = jnp.arange(seq, dtype=jnp.float32)[:, None] * inv[None, :]
    return jnp.cos(ang), jnp.sin(ang)


def apply_rope(x, cos, sin):
    x1, x2 = jnp.split(x, 2, axis=-1)
    c = cos[:, None, :].astype(x.dtype)
    s = sin[:, None, :].astype(x.dtype)
    return jnp.concatenate([x1 * c - x2 * s, x1 * s + x2 * c], axis=-1)


def causal_attention(q, k, v, scale):
    S = q.shape[1]
    outs = []
    for start in range(0, S, ATTN_QBLOCK):
        end = min(start + ATTN_QBLOCK, S)
        s = jnp.einsum('bqhd,bkhd->bhqk', q[:, start:end], k[:, :end]).astype(jnp.float32) * scale
        mask = jnp.arange(end)[None, :] <= (start + jnp.arange(end - start))[:, None]
        p = jax.nn.softmax(jnp.where(mask, s, -jnp.inf), axis=-1).astype(v.dtype)
        outs.append(jnp.einsum('bhqk,bkhd->bqhd', p, v[:, :end]))
    return jnp.concatenate(outs, axis=1)


def mla_mixer(c_q, c_kv, k_pe, q_norm_g, kv_norm_g, w_uq, w_ukv, cos, sin):
    B, S, _ = c_q.shape
    q = (rms_norm(c_q, q_norm_g) @ w_uq).reshape(B, S, MLA_HEADS, MLA_NOPE + MLA_ROPE)
    q = jnp.concatenate([q[..., :MLA_NOPE], apply_rope(q[..., MLA_NOPE:], cos, sin)], axis=-1)
    kv = (rms_norm(c_kv, kv_norm_g) @ w_ukv).reshape(B, S, MLA_HEADS, MLA_NOPE + MLA_V)
    k_rot = apply_rope(k_pe[:, :, None, :], cos, sin)
    k = jnp.concatenate([kv[..., :MLA_NOPE],
                         jnp.broadcast_to(k_rot, (B, S, MLA_HEADS, MLA_ROPE))], axis=-1)
    v = kv[..., MLA_NOPE:]
    out = causal_attention(q, k, v, (MLA_NOPE + MLA_ROPE) ** -0.5)
    return out.reshape(B, S, MLA_OUT)


def conv_mixer(bch, conv_w):
    b_gate, c_gate, h = jnp.split(bch, 3, axis=-1)
    u = c_gate * h
    y = lax.conv_general_dilated(u, conv_w[:, None, :], window_strides=(1,),
                                 padding=[(CONV_W - 1, 0)],
                                 dimension_numbers=('NWC', 'WIO', 'NWC'),
                                 feature_group_count=CONV_CH)
    return b_gate * y


def gather_blocks(blocks, idx):
    return jax.vmap(jax.vmap(lambda bl, ix: bl[ix]))(blocks, idx)


def moba_mixer(qkv, cos, sin):
    B, S, _ = qkv.shape
    q, k, v = [t.reshape(B, S, MOBA_HEADS, MOBA_HD) for t in jnp.split(qkv, 3, axis=-1)]
    q = apply_rope(q, cos, sin)
    k = apply_rope(k, cos, sin)
    n_blk = -(-S // MOBA_BLOCK)
    s_pad = n_blk * MOBA_BLOCK
    pad = ((0, 0), (0, s_pad - S), (0, 0), (0, 0))
    q, k, v = [jnp.pad(t, pad).transpose(0, 2, 1, 3) for t in (q, k, v)]
    kb = k.reshape(B, MOBA_HEADS, n_blk, MOBA_BLOCK, MOBA_HD)
    vb = v.reshape(B, MOBA_HEADS, n_blk, MOBA_BLOCK, MOBA_HD)
    k_mean = jnp.mean(kb, axis=3)
    n_sel = min(MOBA_TOPK, n_blk)
    scale = MOBA_HD ** -0.5
    sel_len = n_sel * MOBA_BLOCK

    def chunk(ci):
        start = ci * MOBA_QCHUNK
        own = start // MOBA_BLOCK
        qc = lax.dynamic_slice_in_dim(q, start, MOBA_QCHUNK, axis=2)
        k_own = lax.dynamic_slice_in_dim(k, own * MOBA_BLOCK, MOBA_BLOCK, axis=2)
        v_own = lax.dynamic_slice_in_dim(v, own * MOBA_BLOCK, MOBA_BLOCK, axis=2)
        qpos = start + jnp.arange(MOBA_QCHUNK)
        kpos = own * MOBA_BLOCK + jnp.arange(MOBA_BLOCK)
        gsc = jnp.einsum('bhqd,bhnd->bhqn', qc, k_mean).astype(jnp.float32)
        gsc = jnp.where(jnp.arange(n_blk) < own, gsc, -jnp.inf)
        _, sel = lax.top_k(gsc, n_sel)
        valid = sel < own
        k_sel = gather_blocks(kb, sel)
        v_sel = gather_blocks(vb, sel)
        s_sel = jnp.einsum('bhqd,bhqjpd->bhqjp', qc, k_sel).astype(jnp.float32) * scale
        s_sel = jnp.where(valid[..., None], s_sel, -jnp.inf)
        s_own = jnp.einsum('bhqd,bhpd->bhqp', qc, k_own).astype(jnp.float32) * scale
        s_own = jnp.where(kpos[None, :] <= qpos[:, None], s_own, -jnp.inf)
        s_all = jnp.concatenate([s_sel.reshape(B, MOBA_HEADS, MOBA_QCHUNK, sel_len), s_own], axis=-1)
        p = jax.nn.softmax(s_all, axis=-1).astype(v.dtype)
        p_sel = p[..., :sel_len].reshape(B, MOBA_HEADS, MOBA_QCHUNK, n_sel, MOBA_BLOCK)
        return (jnp.einsum('bhqjp,bhqjpd->bhqd', p_sel, v_sel)
                + jnp.einsum('bhqp,bhpd->bhqd', p[..., sel_len:], v_own))

    outs = lax.map(chunk, jnp.arange(s_pad // MOBA_QCHUNK))
    out = outs.transpose(1, 0, 3, 2, 4).reshape(B, s_pad, MOBA_OUT)
    return out[:, :S]


def token_mixer(h, w_in, q_norm_g, kv_norm_g, w_uq, w_ukv, conv_w, group_norm_g, w_out,
                rope_mla, rope_moba):
    proj = h @ w_in
    c_q, c_kv, k_pe, bch, qkv = jnp.split(proj, IN_SPLITS, axis=-1)
    y_a = mla_mixer(c_q, c_kv, k_pe, q_norm_g, kv_norm_g, w_uq, w_ukv, *rope_mla)
    y_b = conv_mixer(bch, conv_w)
    y_c = moba_mixer(qkv, *rope_moba)
    g_a, g_b, g_c = jnp.split(group_norm_g, (MLA_OUT, MLA_OUT + CONV_CH))
    y = jnp.concatenate([rms_norm(y_a, g_a), rms_norm(y_b, g_b), rms_norm(y_c, g_c)], axis=-1)
    return y @ w_out


def moe_ffn(h, w_router, router_bias, w_gate, w_up, w_down):
    B, S, D = h.shape
    n_tok = B * S
    t = h.reshape(n_tok, D)
    scores = jax.nn.sigmoid((t @ w_router).astype(jnp.float32))
    biased = scores + router_bias.astype(jnp.float32)[None, :]
    grp_score = lax.top_k(biased.reshape(n_tok, N_GROUPS, EXPERTS_PER_GROUP), TOP_K)[0].sum(-1)
    grp = jnp.argmax(grp_score, axis=-1)
    in_grp = (jnp.arange(N_EXPERTS) // EXPERTS_PER_GROUP)[None, :] == grp[:, None]
    _, e_idx = lax.top_k(jnp.where(in_grp, biased, -jnp.inf), TOP_K)
    gate = jnp.take_along_axis(scores, e_idx, axis=-1)
    gate = gate / jnp.sum(gate, axis=-1, keepdims=True)
    n_asg = n_tok * TOP_K
    flat_e = e_idx.reshape(n_asg)
    order = jnp.argsort(flat_e)
    e_sorted = flat_e[order]
    sizes = jnp.bincount(flat_e, length=N_EXPERTS)
    padded = (sizes + MOE_BLOCK - 1) // MOE_BLOCK * MOE_BLOCK
    pad_end = jnp.cumsum(padded)
    pad_start = pad_end - padded
    grp_start = jnp.cumsum(sizes) - sizes
    dest = pad_start[e_sorted] + jnp.arange(n_asg) - grp_start[e_sorted]
    n_blocks = -(-n_asg // MOE_BLOCK) + N_EXPERTS
    n_slots = n_blocks * MOE_BLOCK
    slot_tok = jnp.zeros((n_slots,), jnp.int32).at[dest].set((order // TOP_K).astype(jnp.int32))
    slot_w = jnp.zeros((n_slots,), t.dtype).at[dest].set(gate.reshape(n_asg)[order].astype(t.dtype))
    block_exp = jnp.minimum(jnp.searchsorted(pad_end, jnp.arange(n_blocks) * MOE_BLOCK, side='right'),
                            N_EXPERTS - 1)

    def expert_block(args):
        toks, e = args
        xb = t[toks]
        a = jax.nn.silu(xb @ w_gate[e]) * (xb @ w_up[e])
        return a @ w_down[e]

    y = lax.map(expert_block, (slot_tok.reshape(n_blocks, MOE_BLOCK), block_exp))
    y = y.reshape(n_slots, D) * slot_w[:, None]
    return jnp.zeros_like(t).at[slot_tok].add(y).reshape(B, S, D)


def setup_inputs(seed: int = 0) -> dict:
    key = jax.random.key(seed)
    ks = jax.random.split(key, 20)
    nrm = lambda k, shape, s: jax.random.normal(k, shape, jnp.float32) * s
    gain = lambda k, shape: 1.0 + 0.02 * jax.random.normal(k, shape, jnp.float32)
    return {
        "x": nrm(ks[0], (BATCH, SEQ, D_MODEL), 1.0),
        "c": nrm(ks[1], (BATCH, D_MODEL), 1.0),
        "w_mod": nrm(ks[2], (D_MODEL, N_MOD * D_MODEL), 0.5 * D_MODEL ** -0.5),
        "mod_table": nrm(ks[3], (DEPTH, N_MOD, D_MODEL), 0.1),
        "mix_norm_g": gain(ks[4], (DEPTH, D_MODEL)),
        "w_in": nrm(ks[5], (DEPTH, D_MODEL, D_IN), D_MODEL ** -0.5),
        "q_norm_g": gain(ks[6], (DEPTH, Q_LORA)),
        "kv_norm_g": gain(ks[7], (DEPTH, KV_LORA)),
        "w_uq": nrm(ks[8], (DEPTH, Q_LORA, MLA_HEADS * (MLA_NOPE + MLA_ROPE)), Q_LORA ** -0.5),
        "w_ukv": nrm(ks[9], (DEPTH, KV_LORA, MLA_HEADS * (MLA_NOPE + MLA_V)), KV_LORA ** -0.5),
        "conv_w": nrm(ks[10], (DEPTH, CONV_W, CONV_CH), CONV_W ** -0.5),
        "group_norm_g": gain(ks[11], (DEPTH, D_MIX)),
        "w_out": nrm(ks[12], (DEPTH, D_MIX, D_MODEL), D_MIX ** -0.5),
        "ffn_norm_g": gain(ks[13], (DEPTH, D_MODEL)),
        "w_router": nrm(ks[14], (D_MODEL, N_EXPERTS), D_MODEL ** -0.5),
        "router_bias": nrm(ks[15], (N_EXPERTS,), 0.01),
        "w_gate": nrm(ks[16], (DEPTH, N_EXPERTS, D_MODEL, D_FF), D_MODEL ** -0.5),
        "w_up": nrm(ks[17], (DEPTH, N_EXPERTS, D_MODEL, D_FF), D_MODEL ** -0.5),
        "w_down": nrm(ks[18], (DEPTH, N_EXPERTS, D_FF, D_MODEL), D_FF ** -0.5),
        "final_norm_g": gain(ks[19], (D_MODEL,)),
    }


def reference(x, c, w_mod, mod_table, mix_norm_g, w_in, q_norm_g, kv_norm_g, w_uq, w_ukv,
              conv_w, group_norm_g, w_out, ffn_norm_g, w_router, router_bias, w_gate, w_up,
              w_down, final_norm_g):
    B, S, D = x.shape
    rope_mla = rope_tables(S, MLA_ROPE)
    rope_moba = rope_tables(S, MOBA_HD)
    mod_shared = (jax.nn.silu(c) @ w_mod).reshape(B, N_MOD, D)
    for l in range(DEPTH):
        mod = mod_shared + mod_table[l][None]
        sh1, sc1, g1, sh2, sc2, g2 = [mod[:, i][:, None, :] for i in range(N_MOD)]
        h = rms_norm(x, mix_norm_g[l]) * (1 + sc1) + sh1
        x = x + g1 * token_mixer(h, w_in[l], q_norm_g[l], kv_norm_g[l], w_uq[l], w_ukv[l],
                                 conv_w[l], group_norm_g[l], w_out[l], rope_mla, rope_moba)
        h = rms_norm(x, ffn_norm_g[l]) * (1 + sc2) + sh2
        x = x + g2 * moe_ffn(h, w_router, router_bias, w_gate[l], w_up[l], w_down[l])
    return rms_norm(x, final_norm_g)
```

```python
import functools

import jax
import jax.numpy as jnp
from jax import lax
from jax.experimental import pallas as pl
from jax.experimental.pallas import tpu as pltpu

MLA_HEADS = 16
MLA_NOPE = 128
MLA_ROPE = 64
MLA_V = 128
Q_LORA = 1024
KV_LORA = 512
MLA_DK_PAD = 256
CONV_CH = 1024
CONV_W = 3
MOBA_HEADS = 8
MOBA_HD = 128
MOBA_BLOCK = 256
MOBA_TOPK = 3
ROPE_THETA = 10000.0
EPS = 1e-6
N_MOD = 6
N_EXPERTS = 16
N_GROUPS = 4
TOP_K = 2
MOE_BLOCK = 256
FF_SPLIT = 2

V7X_VMEM_LIMIT_BYTES = 56 * 1024 * 1024
LANE = 128
SUBLANE = 8

F32 = jnp.float32
BF16 = jnp.bfloat16
NEG_INF = float("-inf")


def _params(*sem):
    return pltpu.CompilerParams(dimension_semantics=sem, vmem_limit_bytes=V7X_VMEM_LIMIT_BYTES)


def _dot(a, b):
    return jnp.dot(a, b, preferred_element_type=F32)


def _dot_nt(a, b, precision=None):
    return lax.dot_general(a, b, (((1,), (1,)), ((), ())), preferred_element_type=F32,
                           precision=precision)


def _tile(dim, want):
    return want if dim % want == 0 else dim


def _mod_kernel(c_ref, w_ref, o_ref):
    c = c_ref[...]
    a = (c * jax.nn.sigmoid(c)).astype(BF16)
    o_ref[...] = _dot(a, w_ref[...].astype(BF16))


def mod_matmul(c_pad, w_mod):
    rows, d = c_pad.shape
    n = w_mod.shape[1]
    tn = _tile(n, 512)
    return pl.pallas_call(
        _mod_kernel,
        grid=(n // tn,),
        in_specs=[pl.BlockSpec((rows, d), lambda j: (0, 0)),
                  pl.BlockSpec((d, tn), lambda j: (0, j))],
        out_specs=pl.BlockSpec((rows, tn), lambda j: (0, j)),
        out_shape=jax.ShapeDtypeStruct((rows, n), F32),
        compiler_params=_params("arbitrary"),
        name="mod_matmul",
    )(c_pad, w_mod)


def _rms(x, g):
    return x * lax.rsqrt(jnp.mean(x * x, axis=-1, keepdims=True) + EPS) * g


def _norm_mod_kernel(x_ref, g_ref, sc_ref, sh_ref, o_ref):
    y = _rms(x_ref[...], g_ref[...])
    o_ref[...] = (y * sc_ref[0] + sh_ref[0]).astype(o_ref.dtype)


def _norm_kernel(x_ref, g_ref, o_ref):
    o_ref[...] = _rms(x_ref[...], g_ref[...]).astype(o_ref.dtype)


def _norm_router_kernel(x_ref, g_ref, sc_ref, sh_ref, whi_ref, wlo_ref, o_ref, lg_ref):
    h = _rms(x_ref[...], g_ref[...]) * sc_ref[0] + sh_ref[0]
    hi = h.astype(BF16)
    o_ref[...] = hi
    lo = (h - hi.astype(F32)).astype(BF16)
    lg_ref[...] = _dot(hi, whi_ref[...]) + _dot(hi, wlo_ref[...]) + _dot(lo, whi_ref[...])


def norm_mod(x2, g, sc1p, sh, seq, out_dtype=BF16, router=None):
    t, d = x2.shape
    tm = _tile(seq, 256)
    per_b = seq // tm
    row = pl.BlockSpec((tm, d), lambda i: (i, 0))
    vec = pl.BlockSpec((1, d), lambda i: (0, 0))
    mod = pl.BlockSpec((1, 1, d), lambda i: (i // per_b, 0, 0))
    g2 = g.reshape(1, d)
    if sc1p is None:
        return pl.pallas_call(
            _norm_kernel, grid=(t // tm,), in_specs=[row, vec], out_specs=row,
            out_shape=jax.ShapeDtypeStruct((t, d), out_dtype),
            compiler_params=_params("arbitrary"), name="final_norm")(x2, g2)
    if router is None:
        return pl.pallas_call(
            _norm_mod_kernel, grid=(t // tm,), in_specs=[row, vec, mod, mod], out_specs=row,
            out_shape=jax.ShapeDtypeStruct((t, d), out_dtype),
            compiler_params=_params("arbitrary"), name="norm_mod")(x2, g2, sc1p, sh)
    whi, wlo = router
    ne = whi.shape[1]
    wspec = pl.BlockSpec((d, ne), lambda i: (0, 0))
    return pl.pallas_call(
        _norm_router_kernel, grid=(t // tm,), in_specs=[row, vec, mod, mod, wspec, wspec],
        out_specs=[row, pl.BlockSpec((tm, ne), lambda i: (i, 0))],
        out_shape=[jax.ShapeDtypeStruct((t, d), out_dtype), jax.ShapeDtypeStruct((t, ne), F32)],
        compiler_params=_params("arbitrary"), name="norm_router")(x2, g2, sc1p, sh, whi, wlo)


def _mm_kernel(a_ref, w_ref, o_ref, acc_ref, *, nk):
    k = pl.program_id(2)

    @pl.when(k == 0)
    def _():
        acc_ref[...] = jnp.zeros_like(acc_ref)

    acc_ref[...] += _dot(a_ref[...], w_ref[...])

    @pl.when(k == nk - 1)
    def _():
        o_ref[...] = acc_ref[...].astype(o_ref.dtype)


def matmul(a, w, out_dtype, tm=1024, tn=1024, tk=1024):
    m, kd = a.shape
    n = w.shape[1]
    tm, tn, tk = _tile(m, tm), _tile(n, tn), _tile(kd, tk)
    nk = kd // tk
    return pl.pallas_call(
        functools.partial(_mm_kernel, nk=nk),
        grid=(m // tm, n // tn, nk),
        in_specs=[pl.BlockSpec((tm, tk), lambda i, j, k: (i, k)),
                  pl.BlockSpec((tk, tn), lambda i, j, k: (k, j))],
        out_specs=pl.BlockSpec((tm, tn), lambda i, j, k: (i, j)),
        out_shape=jax.ShapeDtypeStruct((m, n), out_dtype),
        scratch_shapes=[pltpu.VMEM((tm, tn), F32)],
        compiler_params=_params("parallel", "parallel", "arbitrary"),
        name="matmul",
    )(a, w)


def _mm_res_kernel(a_ref, w_ref, x_ref, gate_ref, o_ref, acc_ref, *, nk):
    k = pl.program_id(2)

    @pl.when(k == 0)
    def _():
        acc_ref[...] = jnp.zeros_like(acc_ref)

    acc_ref[...] += _dot(a_ref[...], w_ref[...])

    @pl.when(k == nk - 1)
    def _():
        o_ref[...] = x_ref[...] + gate_ref[0] * acc_ref[...]


def matmul_residual(a, w, x2, gate, seq, tm=1024, tn=1024, tk=1024):
    m, kd = a.shape
    n = w.shape[1]
    tm, tn, tk = _tile(seq, tm), _tile(n, tn), _tile(kd, tk)
    nk = kd // tk
    per_b = seq // tm
    return pl.pallas_call(
        functools.partial(_mm_res_kernel, nk=nk),
        grid=(m // tm, n // tn, nk),
        in_specs=[pl.BlockSpec((tm, tk), lambda i, j, k: (i, k)),
                  pl.BlockSpec((tk, tn), lambda i, j, k: (k, j)),
                  pl.BlockSpec((tm, tn), lambda i, j, k: (i, j)),
                  pl.BlockSpec((1, 1, tn), lambda i, j, k: (i // per_b, 0, j))],
        out_specs=pl.BlockSpec((tm, tn), lambda i, j, k: (i, j)),
        out_shape=jax.ShapeDtypeStruct((m, n), F32),
        scratch_shapes=[pltpu.VMEM((tm, tn), F32)],
        compiler_params=_params("parallel", "parallel", "arbitrary"),
        name="matmul_residual",
    )(a, w, x2, gate)


def _rope_pair(r2, cos_ref, sin_ref):
    return r2 * cos_ref[...] + pltpu.roll(r2, MLA_ROPE, 1) * sin_ref[...]


def _qup_kernel(cq_ref, g_ref, w_ref, cos_ref, sin_ref, o_ref, *, scale):
    an = _rms(cq_ref[...], g_ref[...]).astype(BF16)
    res = _dot(an, w_ref[0])
    roped = _rope_pair(res[:, MLA_NOPE:], cos_ref, sin_ref)
    o_ref[0] = (jnp.concatenate([res[:, :MLA_NOPE], roped], axis=1) * scale).astype(o_ref.dtype)


def mla_q_up(proj_a, g, w_q, cos_p, sin_p, seq, scale):
    t = proj_a.shape[0]
    heads = w_q.shape[0]
    tm = _tile(seq, 512)
    per_b = seq // tm
    return pl.pallas_call(
        functools.partial(_qup_kernel, scale=scale),
        grid=(t // tm, heads),
        in_specs=[pl.BlockSpec((tm, Q_LORA), lambda i, h: (i, 0)),
                  pl.BlockSpec((1, Q_LORA), lambda i, h: (0, 0)),
                  pl.BlockSpec((1, Q_LORA, MLA_DK_PAD), lambda i, h: (h, 0, 0)),
                  pl.BlockSpec((tm, LANE), lambda i, h: (i % per_b, 0)),
                  pl.BlockSpec((tm, LANE), lambda i, h: (i % per_b, 0))],
        out_specs=pl.BlockSpec((1, tm, MLA_DK_PAD), lambda i, h: (h, i, 0)),
        out_shape=jax.ShapeDtypeStruct((heads, t, MLA_DK_PAD), BF16),
        compiler_params=_params("parallel", "arbitrary"),
        name="mla_q_up",
    )(proj_a, g.reshape(1, Q_LORA), w_q, cos_p, sin_p)


def _kvup_kernel(ckv_ref, g_ref, w_ref, kpe_ref, cos_ref, sin_ref, k_ref, v_ref):
    an = _rms(ckv_ref[...], g_ref[...]).astype(BF16)
    res = _dot(an, w_ref[...])
    k_rot = _rope_pair(kpe_ref[...], cos_ref, sin_ref)
    k_ref[0] = jnp.concatenate([res[:, :MLA_NOPE], k_rot], axis=1).astype(k_ref.dtype)
    v_ref[0] = res[:, MLA_NOPE:].astype(v_ref.dtype)


def mla_kv_up(proj_a, g, w_kv, cos_p, sin_p, seq):
    t = proj_a.shape[0]
    heads = w_kv.shape[1] // (MLA_NOPE + MLA_V)
    tm = _tile(seq, 512)
    per_b = seq // tm
    hw = MLA_NOPE + MLA_V
    return pl.pallas_call(
        _kvup_kernel,
        grid=(t // tm, heads),
        in_specs=[pl.BlockSpec((tm, KV_LORA), lambda i, h: (i, Q_LORA // KV_LORA)),
                  pl.BlockSpec((1, KV_LORA), lambda i, h: (0, 0)),
                  pl.BlockSpec((KV_LORA, hw), lambda i, h: (0, h)),
                  pl.BlockSpec((tm, LANE), lambda i, h: (i, (Q_LORA + KV_LORA) // LANE)),
                  pl.BlockSpec((tm, LANE), lambda i, h: (i % per_b, 0)),
                  pl.BlockSpec((tm, LANE), lambda i, h: (i % per_b, 0))],
        out_specs=[pl.BlockSpec((1, tm, MLA_DK_PAD), lambda i, h: (h, i, 0)),
                   pl.BlockSpec((1, tm, MLA_V), lambda i, h: (h, i, 0))],
        out_shape=[jax.ShapeDtypeStruct((heads, t, MLA_DK_PAD), BF16),
                   jax.ShapeDtypeStruct((heads, t, MLA_V), BF16)],
        compiler_params=_params("parallel", "arbitrary"),
        name="mla_kv_up",
    )(proj_a, g.reshape(1, KV_LORA), w_kv, proj_a, cos_p, sin_p)


def _softmax_step(s, v, m_ref, l_ref, acc_ref):
    m_prev = m_ref[...]
    m_new = jnp.maximum(m_prev, jnp.max(s, axis=1, keepdims=True))
    alpha = jnp.exp(m_prev - m_new)
    p = jnp.exp(s - m_new)
    l_ref[...] = alpha * l_ref[...] + jnp.sum(p, axis=1, keepdims=True)
    acc_ref[...] = alpha * acc_ref[...] + _dot(p.astype(v.dtype), v)
    m_ref[...] = m_new


def _softmax_first(s, v, m_ref, l_ref, acc_ref):
    m = jnp.max(s, axis=1, keepdims=True)
    p = jnp.exp(s - m)
    m_ref[...] = m
    l_ref[...] = jnp.sum(p, axis=1, keepdims=True)
    acc_ref[...] = _dot(p.astype(v.dtype), v)


def _causal_mask(s):
    row = lax.broadcasted_iota(jnp.int32, s.shape, 0)
    col = lax.broadcasted_iota(jnp.int32, s.shape, 1)
    return jnp.where(col <= row, s, NEG_INF)


def _mla_attn_kernel(q_ref, k_ref, v_ref, o_ref, m_ref, l_ref, acc_ref, *, tq):
    i = pl.program_id(2)
    q = q_ref[0]
    start = pl.multiple_of(i * tq, tq)
    s = _causal_mask(_dot_nt(q, k_ref[0, pl.ds(start, tq), :]))
    _softmax_first(s, v_ref[0, pl.ds(start, tq), :], m_ref, l_ref, acc_ref)

    def body(j, carry):
        off = pl.multiple_of(j * tq, tq)
        s = _dot_nt(q, k_ref[0, pl.ds(off, tq), :])
        _softmax_step(s, v_ref[0, pl.ds(off, tq), :], m_ref, l_ref, acc_ref)
        return carry

    lax.fori_loop(0, i, body, 0)
    o_ref[...] = (acc_ref[...] / l_ref[...]).astype(o_ref.dtype)


def mla_attention(q, k, v, batch, seq, out_dtype):
    heads, t, dk = q.shape
    dv = v.shape[2]
    tq = _tile(seq, 512)
    nq = seq // tq
    return pl.pallas_call(
        functools.partial(_mla_attn_kernel, tq=tq),
        grid=(batch, heads, nq),
        in_specs=[pl.BlockSpec((1, tq, dk), lambda b, h, i: (h, b * nq + i, 0)),
                  pl.BlockSpec((1, seq, dk), lambda b, h, i: (h, b, 0)),
                  pl.BlockSpec((1, seq, dv), lambda b, h, i: (h, b, 0))],
        out_specs=pl.BlockSpec((tq, dv), lambda b, h, i: (b * nq + i, h)),
        out_shape=jax.ShapeDtypeStruct((t, heads * dv), out_dtype),
        scratch_shapes=[pltpu.VMEM((tq, 1), F32), pltpu.VMEM((tq, 1), F32),
                        pltpu.VMEM((tq, dv), F32)],
        compiler_params=_params("parallel", "parallel", "arbitrary"),
        name="mla_attention",
    )(q, k, v)


def _conv_kernel(b_ref, c_ref, h_ref, w_ref, o_ref, carry_ref, *, per_b):
    i = pl.program_id(0)

    @pl.when(i % per_b == 0)
    def _():
        carry_ref[...] = jnp.zeros_like(carry_ref)

    tm = o_ref.shape[0]
    w0, w1, w2 = w_ref[0:1, :], w_ref[1:2, :], w_ref[2:3, :]
    u = c_ref[...] * h_ref[...]
    y = w0 * pltpu.roll(u, 2, 0) + w1 * pltpu.roll(u, 1, 0) + w2 * u
    o_ref[...] = (b_ref[...] * y).astype(o_ref.dtype)
    u8 = u[0:SUBLANE, :]
    tail = carry_ref[...]
    r8 = lax.broadcasted_iota(jnp.int32, u8.shape, 0)
    p1 = jnp.where(r8 < 1, pltpu.roll(tail, 1, 0), pltpu.roll(u8, 1, 0))
    p2 = jnp.where(r8 < 2, pltpu.roll(tail, 2, 0), pltpu.roll(u8, 2, 0))
    y8 = w0 * p2 + w1 * p1 + w2 * u8
    o_ref[0:SUBLANE, :] = (b_ref[0:SUBLANE, :] * y8).astype(o_ref.dtype)
    carry_ref[...] = u[tm - SUBLANE:tm, :]


def conv_mixer(bch, conv_w, seq, out_dtype):
    t = bch.shape[0]
    ch = conv_w.shape[1]
    tm = _tile(seq, 256)
    per_b = seq // tm
    return pl.pallas_call(
        functools.partial(_conv_kernel, per_b=per_b),
        grid=(t // tm,),
        in_specs=[pl.BlockSpec((tm, ch), lambda i: (i, 0)),
                  pl.BlockSpec((tm, ch), lambda i: (i, 1)),
                  pl.BlockSpec((tm, ch), lambda i: (i, 2)),
                  pl.BlockSpec((CONV_W, ch), lambda i: (0, 0))],
        out_specs=pl.BlockSpec((tm, ch), lambda i: (i, 0)),
        out_shape=jax.ShapeDtypeStruct((t, ch), out_dtype),
        scratch_shapes=[pltpu.VMEM((SUBLANE, ch), F32)],
        compiler_params=_params("arbitrary"),
        name="conv_mixer",
    )(bch, bch, bch, conv_w)


def _moba_prep_kernel(q_ref, k_ref, v_ref, cos_ref, sin_ref, qo_ref, ko_ref, vo_ref, km_ref):
    heads = qo_ref.shape[0]
    cos, sin = cos_ref[...], sin_ref[...]
    means = []
    for h in range(heads):
        sl = slice(h * MOBA_HD, (h + 1) * MOBA_HD)
        qh = q_ref[:, sl]
        kh = k_ref[:, sl]
        qo_ref[h] = (qh * cos + pltpu.roll(qh, MOBA_HD // 2, 1) * sin).astype(qo_ref.dtype)
        kr = kh * cos + pltpu.roll(kh, MOBA_HD // 2, 1) * sin
        ko_ref[h] = kr.astype(ko_ref.dtype)
        vo_ref[h] = v_ref[:, sl].astype(vo_ref.dtype)
        means.append(jnp.mean(kr, axis=0, keepdims=True))
    km_ref[0] = jnp.concatenate(means, axis=0)


def moba_prep(qkv, cos_f, sin_s, seq):
    t = qkv.shape[0]
    width = qkv.shape[1] // 3
    heads = width // MOBA_HD
    tm = MOBA_BLOCK
    per_b = seq // tm
    hspec = pl.BlockSpec((heads, tm, MOBA_HD), lambda i: (0, i, 0))
    hshape = jax.ShapeDtypeStruct((heads, t, MOBA_HD), BF16)
    return pl.pallas_call(
        _moba_prep_kernel,
        grid=(t // tm,),
        in_specs=[pl.BlockSpec((tm, width), lambda i: (i, 0)),
                  pl.BlockSpec((tm, width), lambda i: (i, 1)),
                  pl.BlockSpec((tm, width), lambda i: (i, 2)),
                  pl.BlockSpec((tm, MOBA_HD), lambda i: (i % per_b, 0)),
                  pl.BlockSpec((tm, MOBA_HD), lambda i: (i % per_b, 0))],
        out_specs=[hspec, hspec, hspec, pl.BlockSpec((1, heads, MOBA_HD), lambda i: (i, 0, 0))],
        out_shape=[hshape, hshape, hshape,
                   jax.ShapeDtypeStruct((t // tm, heads, MOBA_HD), F32)],
        compiler_params=_params("arbitrary"),
        name="moba_prep",
    )(qkv, qkv, qkv, cos_f, sin_s)


def _moba_attn_kernel(q_ref, k_ref, v_ref, km_ref, o_ref, m_ref, l_ref, acc_ref, sel_ref, *, scale):
    own = pl.program_id(2)
    qf = q_ref[0].astype(F32)
    n_blk = km_ref.shape[2]
    g = _dot_nt(qf, km_ref[0, 0], precision=lax.Precision.HIGHEST)
    blk = lax.broadcasted_iota(jnp.int32, g.shape, 1)
    past = blk < own
    g = jnp.where(past, g, NEG_INF)
    sel = jnp.zeros(g.shape, F32)
    for _ in range(min(MOBA_TOPK, n_blk)):
        mx = jnp.max(g, axis=1, keepdims=True)
        first = jnp.min(jnp.where(g == mx, blk, n_blk), axis=1, keepdims=True)
        hit = blk == first
        sel = jnp.where(hit, 1.0, sel)
        g = jnp.where(hit, NEG_INF, g)
    sel_ref[...] = jnp.where(past, sel, 0.0)

    q = (qf * scale).astype(k_ref.dtype)
    start = pl.multiple_of(own * MOBA_BLOCK, MOBA_BLOCK)
    s = _causal_mask(_dot_nt(q, k_ref[0, pl.ds(start, MOBA_BLOCK), :]))
    _softmax_first(s, v_ref[0, pl.ds(start, MOBA_BLOCK), :], m_ref, l_ref, acc_ref)

    def body(n, carry):
        off = pl.multiple_of(n * MOBA_BLOCK, MOBA_BLOCK)
        blk_n = lax.broadcasted_iota(jnp.int32, sel_ref.shape, 1)
        chosen = jnp.sum(jnp.where(blk_n == n, sel_ref[...], 0.0), axis=1, keepdims=True) > 0.5
        s = _dot_nt(q, k_ref[0, pl.ds(off, MOBA_BLOCK), :])
        s = jnp.where(chosen, s, NEG_INF)
        _softmax_step(s, v_ref[0, pl.ds(off, MOBA_BLOCK), :], m_ref, l_ref, acc_ref)
        return carry

    lax.fori_loop(0, own, body, 0)
    o_ref[...] = (acc_ref[...] / l_ref[...]).astype(o_ref.dtype)


def moba_attention(q, k, v, k_mean, batch, seq, out_dtype):
    heads, t, hd = q.shape
    n_blk = seq // MOBA_BLOCK
    return pl.pallas_call(
        functools.partial(_moba_attn_kernel, scale=hd ** -0.5),
        grid=(batch, heads, n_blk),
        in_specs=[pl.BlockSpec((1, MOBA_BLOCK, hd), lambda b, h, i: (h, b * n_blk + i, 0)),
                  pl.BlockSpec((1, seq, hd), lambda b, h, i: (h, b, 0)),
                  pl.BlockSpec((1, seq, hd), lambda b, h, i: (h, b, 0)),
                  pl.BlockSpec((1, 1, n_blk, hd), lambda b, h, i: (b, h, 0, 0))],
        out_specs=pl.BlockSpec((MOBA_BLOCK, hd), lambda b, h, i: (b * n_blk + i, h)),
        out_shape=jax.ShapeDtypeStruct((t, heads * hd), out_dtype),
        scratch_shapes=[pltpu.VMEM((MOBA_BLOCK, 1), F32), pltpu.VMEM((MOBA_BLOCK, 1), F32),
                        pltpu.VMEM((MOBA_BLOCK, hd), F32), pltpu.VMEM((MOBA_BLOCK, n_blk), F32)],
        compiler_params=_params("parallel", "parallel", "arbitrary"),
        name="moba_attention",
    )(q, k, v, k_mean)


def _group_norm_kernel(a_ref, b_ref, c_ref, ga_ref, gb_ref, gc_ref, o_ref):
    wa, wb = a_ref.shape[1], b_ref.shape[1]
    o_ref[:, 0:wa] = _rms(a_ref[...].astype(F32), ga_ref[...]).astype(o_ref.dtype)
    o_ref[:, wa:wa + wb] = _rms(b_ref[...].astype(F32), gb_ref[...]).astype(o_ref.dtype)
    o_ref[:, wa + wb:] = _rms(c_ref[...].astype(F32), gc_ref[...]).astype(o_ref.dtype)


def group_norm(y_a, y_b, y_c, g):
    t = y_a.shape[0]
    wa, wb, wc = y_a.shape[1], y_b.shape[1], y_c.shape[1]
    tm = _tile(t, 256)
    ga, gb, gc = g[:wa].reshape(1, wa), g[wa:wa + wb].reshape(1, wb), g[wa + wb:].reshape(1, wc)
    rows = lambda w: pl.BlockSpec((tm, w), lambda i: (i, 0))
    vec = lambda w: pl.BlockSpec((1, w), lambda i: (0, 0))
    return pl.pallas_call(
        _group_norm_kernel,
        grid=(t // tm,),
        in_specs=[rows(wa), rows(wb), rows(wc), vec(wa), vec(wb), vec(wc)],
        out_specs=rows(wa + wb + wc),
        out_shape=jax.ShapeDtypeStruct((t, wa + wb + wc), BF16),
        compiler_params=_params("arbitrary"),
        name="group_norm",
    )(y_a, y_b, y_c, ga, gb, gc)


def _ffn_kernel(bexp_ref, nused_ref, x_ref, wg_ref, wu_ref, wd_ref, o_ref):
    b = pl.program_id(1)

    @pl.when(b < nused_ref[0])
    def _():
        x = x_ref[...]
        gate = _dot(x, wg_ref[0])
        up = _dot(x, wu_ref[0])
        a = (gate * jax.nn.sigmoid(gate) * up).astype(x.dtype)
        o_ref[0] = _dot(a, wd_ref[0]).astype(o_ref.dtype)

    @pl.when(b >= nused_ref[0])
    def _():
        o_ref[...] = jnp.zeros_like(o_ref)


def expert_ffn(xs, block_exp, n_used, w_gate, w_up, w_down):
    n_slots, d = xs.shape
    ff = w_gate.shape[2]
    fh = ff // FF_SPLIT
    n_blocks = n_slots // MOE_BLOCK
    grid_spec = pltpu.PrefetchScalarGridSpec(
        num_scalar_prefetch=2,
        grid=(FF_SPLIT, n_blocks),
        in_specs=[pl.BlockSpec((MOE_BLOCK, d), lambda j, b, be, nu: (b, 0)),
                  pl.BlockSpec((1, d, fh), lambda j, b, be, nu: (be[b], 0, j)),
                  pl.BlockSpec((1, d, fh), lambda j, b, be, nu: (be[b], 0, j)),
                  pl.BlockSpec((1, fh, d), lambda j, b, be, nu: (be[b], j, 0))],
        out_specs=pl.BlockSpec((1, MOE_BLOCK, d), lambda j, b, be, nu: (j, b, 0)),
    )
    return pl.pallas_call(
        _ffn_kernel,
        grid_spec=grid_spec,
        out_shape=jax.ShapeDtypeStruct((FF_SPLIT, n_slots, d), BF16),
        compiler_params=_params("arbitrary", "arbitrary"),
        name="expert_ffn",
    )(block_exp, n_used, xs, w_gate, w_up, w_down)


def _combine_kernel(x_ref, gate_ref, y_ref, w_ref, o_ref):
    w = w_ref[...]
    acc = jnp.zeros(x_ref.shape, F32)
    for kk in range(y_ref.shape[1]):
        yk = y_ref[0, kk].astype(F32)
        for j in range(1, y_ref.shape[0]):
            yk = yk + y_ref[j, kk].astype(F32)
        acc = acc + w[:, kk:kk + 1] * yk
    o_ref[...] = x_ref[...] + gate_ref[0] * acc


def moe_combine(x2, gate, yg, w_tok, seq):
    t, d = x2.shape
    nj, nk = yg.shape[0], yg.shape[1]
    tm = _tile(seq, 256)
    per_b = seq // tm
    return pl.pallas_call(
        _combine_kernel,
        grid=(t // tm,),
        in_specs=[pl.BlockSpec((tm, d), lambda i: (i, 0)),
                  pl.BlockSpec((1, 1, d), lambda i: (i // per_b, 0, 0)),
                  pl.BlockSpec((nj, nk, tm, d), lambda i: (0, 0, i, 0)),
                  pl.BlockSpec((tm, nk), lambda i: (i, 0))],
        out_specs=pl.BlockSpec((tm, d), lambda i: (i, 0)),
        out_shape=jax.ShapeDtypeStruct((t, d), F32),
        compiler_params=_params("arbitrary"),
        name="moe_combine",
    )(x2, gate, yg, w_tok)


def _route(logits, router_bias):
    n_tok = logits.shape[0]
    per_grp = N_EXPERTS // N_GROUPS
    scores = jax.nn.sigmoid(logits)
    biased = scores + router_bias.astype(F32)[None, :]
    grp_score = lax.top_k(biased.reshape(n_tok, N_GROUPS, per_grp), TOP_K)[0].sum(-1)
    grp = jnp.argmax(grp_score, axis=-1)
    in_grp = (jnp.arange(N_EXPERTS) // per_grp)[None, :] == grp[:, None]
    _, e_idx = lax.top_k(jnp.where(in_grp, biased, -jnp.inf), TOP_K)
    gate = jnp.take_along_axis(scores, e_idx, axis=-1)
    gate = gate / jnp.sum(gate, axis=-1, keepdims=True)
    n_asg = n_tok * TOP_K
    flat_e = e_idx.reshape(n_asg)
    onehot = (flat_e[:, None] == jnp.arange(N_EXPERTS)[None, :]).astype(jnp.int32)
    running = jnp.cumsum(onehot, axis=0)
    rank = jnp.take_along_axis(running, flat_e[:, None], axis=1)[:, 0] - 1
    sizes = running[-1]
    padded = (sizes + MOE_BLOCK - 1) // MOE_BLOCK * MOE_BLOCK
    pad_end = jnp.cumsum(padded)
    pad_start = pad_end - padded
    dest = (pad_start[flat_e] + rank).astype(jnp.int32)
    n_blocks = -(-n_asg // MOE_BLOCK) + N_EXPERTS
    n_slots = n_blocks * MOE_BLOCK
    slot_tok = jnp.zeros((n_slots,), jnp.int32).at[dest].set(
        (jnp.arange(n_asg, dtype=jnp.int32) // TOP_K))
    block_exp = jnp.minimum(
        jnp.searchsorted(pad_end, jnp.arange(n_blocks) * MOE_BLOCK, side='right'),
        N_EXPERTS - 1).astype(jnp.int32)
    n_used = (pad_end[-1] // MOE_BLOCK).astype(jnp.int32).reshape(1)
    return gate, dest.reshape(n_tok, TOP_K), slot_tok, block_exp, n_used


def _rot_cols(w):
    half = w.shape[-1] // 2
    return jnp.concatenate([-w[..., half:], w[..., :half]], axis=-1)


def _prep_layer(w_in, w_uq, w_ukv, w_out, w_gate, w_up, w_down):
    d = w_in.shape[0]
    a_end = Q_LORA + KV_LORA + MLA_ROPE
    b_end = a_end + 3 * CONV_CH
    w_kpe = w_in[:, Q_LORA + KV_LORA:a_end]
    w_a = jnp.concatenate([w_in[:, :a_end], _rot_cols(w_kpe)], axis=1).astype(BF16)
    w_b = w_in[:, a_end:b_end].astype(BF16)
    w_c = w_in[:, b_end:].astype(BF16)
    wq = w_uq.reshape(Q_LORA, MLA_HEADS, MLA_NOPE + MLA_ROPE)
    wq_rope = wq[..., MLA_NOPE:]
    wq = jnp.concatenate([wq, _rot_cols(wq_rope)], axis=-1).transpose(1, 0, 2).astype(BF16)
    return dict(w_a=w_a, w_b=w_b, w_c=w_c, w_q=wq, w_kv=w_ukv.astype(BF16),
                w_out=w_out.astype(BF16), w_gate=w_gate.astype(BF16), w_up=w_up.astype(BF16),
                w_down=w_down.astype(BF16))


def _rope_tables(seq, dim):
    inv = 1.0 / (ROPE_THETA ** (jnp.arange(0, dim, 2, dtype=F32) / dim))
    ang = jnp.arange(seq, dtype=F32)[:, None] * inv[None, :]
    return jnp.cos(ang), jnp.sin(ang)


def kernel(x, c, w_mod, mod_table, mix_norm_g, w_in, q_norm_g, kv_norm_g, w_uq, w_ukv, conv_w,
           group_norm_g, w_out, ffn_norm_g, w_router, router_bias, w_gate, w_up, w_down,
           final_norm_g):
    batch, seq, d = x.shape
    depth = w_in.shape[0]
    t = batch * seq
    x2 = x.reshape(t, d)

    cos_a, sin_a = _rope_tables(seq, MLA_ROPE)
    zeros_a = jnp.zeros_like(cos_a)
    cos_p = jnp.concatenate([cos_a, cos_a, zeros_a, zeros_a], axis=1)
    sin_p = jnp.concatenate([sin_a, sin_a, zeros_a, zeros_a], axis=1)
    cos_b, sin_b = _rope_tables(seq, MOBA_HD)
    cos_f = jnp.concatenate([cos_b, cos_b], axis=1)
    sin_s = jnp.concatenate([-sin_b, sin_b], axis=1)

    c_pad = jnp.zeros((SUBLANE, d), F32).at[:batch].set(c)
    mod_shared = mod_matmul(c_pad, w_mod)[:batch].reshape(batch, N_MOD, d)

    wr_hi = w_router.astype(BF16)
    wr_lo = (w_router - wr_hi.astype(F32)).astype(BF16)
    mla_scale = (MLA_NOPE + MLA_ROPE) ** -0.5

    for l in range(depth):
        p = _prep_layer(w_in[l], w_uq[l], w_ukv[l], w_out[l], w_gate[l], w_up[l], w_down[l])
        mod = mod_shared + mod_table[l][None]
        sh1, sc1, g1, sh2, sc2, g2 = [mod[:, i][:, None, :] for i in range(N_MOD)]

        h = norm_mod(x2, mix_norm_g[l], 1.0 + sc1, sh1, seq)
        proj_a = matmul(h, p["w_a"], F32, tk=512)
        bch = matmul(h, p["w_b"], F32)
        qkv = matmul(h, p["w_c"], F32)

        q_a = mla_q_up(proj_a, q_norm_g[l], p["w_q"], cos_p, sin_p, seq, mla_scale)
        k_a, v_a = mla_kv_up(proj_a, kv_norm_g[l], p["w_kv"], cos_p, sin_p, seq)
        y_a = mla_attention(q_a, k_a, v_a, batch, seq, BF16)

        y_b = conv_mixer(bch, conv_w[l], seq, F32)

        q_c, k_c, v_c, k_mean = moba_prep(qkv, cos_f, sin_s, seq)
        n_blk = seq // MOBA_BLOCK
        k_mean = k_mean.reshape(batch, n_blk, MOBA_HEADS, MOBA_HD).transpose(0, 2, 1, 3)
        y_c = moba_attention(q_c, k_c, v_c, k_mean, batch, seq, BF16)

        y = group_norm(y_a, y_b, y_c, group_norm_g[l])
        x2 = matmul_residual(y, p["w_out"], x2, g1, seq)

        h2, logits = norm_mod(x2, ffn_norm_g[l], 1.0 + sc2, sh2, seq, router=(wr_hi, wr_lo))
        gate, pos, slot_tok, block_exp, n_used = _route(logits, router_bias)
        xs = jnp.take(h2, slot_tok, axis=0)
        yp = expert_ffn(xs, block_exp, n_used, p["w_gate"], p["w_up"], p["w_down"])
        yg = jnp.take(yp, pos.T, axis=1)
        x2 = moe_combine(x2, g2, yg, gate, seq)

    out = norm_mod(x2, final_norm_g, None, None, seq, out_dtype=F32)
    return out.reshape(batch, seq, d)
```

```python
import functools

import jax
import jax.numpy as jnp
from jax import lax
from jax.experimental import pallas as pl
from jax.experimental.pallas import tpu as pltpu

MLA_HEADS = 16
MLA_NOPE = 128
MLA_ROPE = 64
MLA_V = 128
Q_LORA = 1024
KV_LORA = 512
MLA_DK_PAD = 256
CONV_CH = 1024
CONV_W = 3
MOBA_HEADS = 8
MOBA_HD = 128
MOBA_BLOCK = 256
MOBA_TOPK = 3
ROPE_THETA = 10000.0
EPS = 1e-6
N_MOD = 6
N_EXPERTS = 16
N_GROUPS = 4
TOP_K = 2
MOE_BLOCK = 256
FF_SPLIT = 2
ATTN_TILE = 512
ONES_ROWS = 16

V7X_VMEM_LIMIT_BYTES = 56 * 1024 * 1024
LANE = 128
SUBLANE = 8

F32 = jnp.float32
BF16 = jnp.bfloat16
NEG_INF = float("-inf")


def _params(*sem):
    return pltpu.CompilerParams(dimension_semantics=sem, vmem_limit_bytes=V7X_VMEM_LIMIT_BYTES)


def _dot(a, b, precision=None):
    return jnp.dot(a, b, preferred_element_type=F32, precision=precision)


def _dot_nt(a, b):
    return lax.dot_general(a, b, (((1,), (1,)), ((), ())), preferred_element_type=F32)


def _tile(dim, want):
    return want if dim % want == 0 else dim


def _mod_kernel(c_ref, w_ref, o_ref):
    c = c_ref[...]
    a = (c * jax.nn.sigmoid(c)).astype(BF16)
    o_ref[...] = _dot(a, w_ref[...].astype(BF16))


def mod_matmul(c_pad, w_mod):
    rows, d = c_pad.shape
    n = w_mod.shape[1]
    tn = _tile(n, 512)
    return pl.pallas_call(
        _mod_kernel,
        grid=(n // tn,),
        in_specs=[pl.BlockSpec((rows, d), lambda j: (0, 0)),
                  pl.BlockSpec((d, tn), lambda j: (0, j))],
        out_specs=pl.BlockSpec((rows, tn), lambda j: (0, j)),
        out_shape=jax.ShapeDtypeStruct((rows, n), F32),
        compiler_params=_params("arbitrary"),
        name="mod_matmul",
    )(c_pad, w_mod)


def _rms(x, g):
    return x * lax.rsqrt(jnp.mean(x * x, axis=-1, keepdims=True) + EPS) * g


def _norm_mod_kernel(x_ref, g_ref, sc_ref, sh_ref, o_ref):
    y = _rms(x_ref[...], g_ref[...])
    o_ref[...] = (y * sc_ref[0] + sh_ref[0]).astype(o_ref.dtype)


def _norm_kernel(x_ref, g_ref, o_ref):
    o_ref[...] = _rms(x_ref[...], g_ref[...]).astype(o_ref.dtype)


def _top2_sum(a, b, c, d):
    hi1, lo1 = jnp.maximum(a, b), jnp.minimum(a, b)
    hi2, lo2 = jnp.maximum(c, d), jnp.minimum(c, d)
    return jnp.maximum(hi1, hi2) + jnp.maximum(jnp.minimum(hi1, hi2), jnp.maximum(lo1, lo2))


def _norm_router_kernel(x_ref, g_ref, sc_ref, sh_ref, whi_ref, wlo_ref, bias_ref,
                        o_ref, e_ref, gate_ref):
    h = _rms(x_ref[...], g_ref[...]) * sc_ref[0] + sh_ref[0]
    hi = h.astype(BF16)
    o_ref[...] = hi
    lo = (h - hi.astype(F32)).astype(BF16)
    whi, wlo = whi_ref[...], wlo_ref[...]
    logits = _dot_nt(whi, hi) + _dot_nt(wlo, hi) + _dot_nt(whi, lo)
    scores = jax.nn.sigmoid(logits)
    biased = scores + bias_ref[...]
    per_grp = N_EXPERTS // N_GROUPS
    b_rows = [biased[e:e + 1, :] for e in range(N_EXPERTS)]
    s_rows = [scores[e:e + 1, :] for e in range(N_EXPERTS)]
    grp_scores = [_top2_sum(*b_rows[g * per_grp:(g + 1) * per_grp]) for g in range(N_GROUPS)]
    best, grp = grp_scores[0], jnp.zeros(grp_scores[0].shape, jnp.int32)
    for g in range(1, N_GROUPS):
        better = grp_scores[g] > best
        grp = jnp.where(better, g, grp)
        best = jnp.where(better, grp_scores[g], best)
    b4, s4 = b_rows[:per_grp], s_rows[:per_grp]
    for g in range(1, N_GROUPS):
        in_g = grp == g
        b4 = [jnp.where(in_g, b_rows[g * per_grp + j], b4[j]) for j in range(per_grp)]
        s4 = [jnp.where(in_g, s_rows[g * per_grp + j], s4[j]) for j in range(per_grp)]
    v1, i1 = b4[0], jnp.zeros(grp.shape, jnp.int32)
    for j in range(1, per_grp):
        better = b4[j] > v1
        i1 = jnp.where(better, j, i1)
        v1 = jnp.where(better, b4[j], v1)
    v2, i2 = jnp.full(v1.shape, NEG_INF, F32), jnp.zeros(grp.shape, jnp.int32)
    for j in range(per_grp):
        better = (i1 != j) & (b4[j] > v2)
        i2 = jnp.where(better, j, i2)
        v2 = jnp.where(better, b4[j], v2)
    g1, g2 = s4[0], s4[0]
    for j in range(1, per_grp):
        g1 = jnp.where(i1 == j, s4[j], g1)
        g2 = jnp.where(i2 == j, s4[j], g2)
    total = g1 + g2
    e_ref[...] = jnp.concatenate([grp * per_grp + i1, grp * per_grp + i2], axis=0)
    gate_ref[...] = jnp.concatenate([g1 / total, g2 / total], axis=0)


def norm_mod(x2, g, sc1p, sh, seq, out_dtype=BF16, router=None):
    t, d = x2.shape
    tm = _tile(seq, 256)
    per_b = seq // tm
    row = pl.BlockSpec((tm, d), lambda i: (i, 0))
    vec = pl.BlockSpec((1, d), lambda i: (0, 0))
    mod = pl.BlockSpec((1, 1, d), lambda i: (i // per_b, 0, 0))
    g2 = g.reshape(1, d)
    if sc1p is None:
        return pl.pallas_call(
            _norm_kernel, grid=(t // tm,), in_specs=[row, vec], out_specs=row,
            out_shape=jax.ShapeDtypeStruct((t, d), out_dtype),
            compiler_params=_params("arbitrary"), name="final_norm")(x2, g2)
    if router is None:
        return pl.pallas_call(
            _norm_mod_kernel, grid=(t // tm,), in_specs=[row, vec, mod, mod], out_specs=row,
            out_shape=jax.ShapeDtypeStruct((t, d), out_dtype),
            compiler_params=_params("arbitrary"), name="norm_mod")(x2, g2, sc1p, sh)
    whi_t, wlo_t, bias = router
    ne = whi_t.shape[0]
    wspec = pl.BlockSpec((ne, d), lambda i: (0, 0))
    kspec = pl.BlockSpec((TOP_K, tm), lambda i: (0, i))
    return pl.pallas_call(
        _norm_router_kernel, grid=(t // tm,),
        in_specs=[row, vec, mod, mod, wspec, wspec, pl.BlockSpec((ne, 1), lambda i: (0, 0))],
        out_specs=[row, kspec, kspec],
        out_shape=[jax.ShapeDtypeStruct((t, d), out_dtype),
                   jax.ShapeDtypeStruct((TOP_K, t), jnp.int32),
                   jax.ShapeDtypeStruct((TOP_K, t), F32)],
        compiler_params=_params("arbitrary"), name="norm_router")(x2, g2, sc1p, sh, whi_t, wlo_t, bias)


def _mm_kernel(a_ref, w_ref, o_ref, acc_ref, *, nk):
    k = pl.program_id(2)

    @pl.when(k == 0)
    def _():
        acc_ref[...] = jnp.zeros_like(acc_ref)

    acc_ref[...] += _dot(a_ref[...], w_ref[...])

    @pl.when(k == nk - 1)
    def _():
        o_ref[...] = acc_ref[...].astype(o_ref.dtype)


def matmul(a, w, out_dtype, tm=1024, tn=1024, tk=1024):
    m, kd = a.shape
    n = w.shape[1]
    tm, tn, tk = _tile(m, tm), _tile(n, tn), _tile(kd, tk)
    nk = kd // tk
    return pl.pallas_call(
        functools.partial(_mm_kernel, nk=nk),
        grid=(m // tm, n // tn, nk),
        in_specs=[pl.BlockSpec((tm, tk), lambda i, j, k: (i, k)),
                  pl.BlockSpec((tk, tn), lambda i, j, k: (k, j))],
        out_specs=pl.BlockSpec((tm, tn), lambda i, j, k: (i, j)),
        out_shape=jax.ShapeDtypeStruct((m, n), out_dtype),
        scratch_shapes=[pltpu.VMEM((tm, tn), F32)],
        compiler_params=_params("parallel", "parallel", "arbitrary"),
        name="matmul",
    )(a, w)


def _mm_res_kernel(a_ref, w_ref, x_ref, gate_ref, o_ref, acc_ref, *, nk):
    k = pl.program_id(2)

    @pl.when(k == 0)
    def _():
        acc_ref[...] = jnp.zeros_like(acc_ref)

    acc_ref[...] += _dot(a_ref[...], w_ref[...])

    @pl.when(k == nk - 1)
    def _():
        o_ref[...] = x_ref[...] + gate_ref[0] * acc_ref[...]


def matmul_residual(a, w, x2, gate, seq, tm=1024, tn=1024, tk=1024):
    m, kd = a.shape
    n = w.shape[1]
    tm, tn, tk = _tile(seq, tm), _tile(n, tn), _tile(kd, tk)
    nk = kd // tk
    per_b = seq // tm
    return pl.pallas_call(
        functools.partial(_mm_res_kernel, nk=nk),
        grid=(m // tm, n // tn, nk),
        in_specs=[pl.BlockSpec((tm, tk), lambda i, j, k: (i, k)),
                  pl.BlockSpec((tk, tn), lambda i, j, k: (k, j)),
                  pl.BlockSpec((tm, tn), lambda i, j, k: (i, j)),
                  pl.BlockSpec((1, 1, tn), lambda i, j, k: (i // per_b, 0, j))],
        out_specs=pl.BlockSpec((tm, tn), lambda i, j, k: (i, j)),
        out_shape=jax.ShapeDtypeStruct((m, n), F32),
        scratch_shapes=[pltpu.VMEM((tm, tn), F32)],
        compiler_params=_params("parallel", "parallel", "arbitrary"),
        name="matmul_residual",
    )(a, w, x2, gate)


def _rope_pair(r2, cos_ref, sin_ref):
    return r2 * cos_ref[...] + pltpu.roll(r2, MLA_ROPE, 1) * sin_ref[...]


def _with_ones_rows(vt):
    return jnp.concatenate([vt, jnp.ones((ONES_ROWS, vt.shape[1]), vt.dtype)], axis=0)


def _qup_kernel(cq_ref, g_ref, w_ref, cos_ref, sin_ref, o_ref, *, scale):
    an = _rms(cq_ref[...], g_ref[...]).astype(BF16)
    res = _dot(an, w_ref[0])
    roped = _rope_pair(res[:, MLA_NOPE:], cos_ref, sin_ref)
    q = jnp.concatenate([res[:, :MLA_NOPE], roped], axis=1) * scale
    o_ref[0, 0] = q.T.astype(o_ref.dtype)


def mla_q_up(proj_a, g, w_q, cos_p, sin_p, seq, scale):
    t = proj_a.shape[0]
    heads = w_q.shape[0]
    tm = _tile(seq, ATTN_TILE)
    per_b = seq // tm
    return pl.pallas_call(
        functools.partial(_qup_kernel, scale=scale),
        grid=(t // tm, heads),
        in_specs=[pl.BlockSpec((tm, Q_LORA), lambda i, h: (i, 0)),
                  pl.BlockSpec((1, Q_LORA), lambda i, h: (0, 0)),
                  pl.BlockSpec((1, Q_LORA, MLA_DK_PAD), lambda i, h: (h, 0, 0)),
                  pl.BlockSpec((tm, LANE), lambda i, h: (i % per_b, 0)),
                  pl.BlockSpec((tm, LANE), lambda i, h: (i % per_b, 0))],
        out_specs=pl.BlockSpec((1, 1, MLA_DK_PAD, tm), lambda i, h: (h, i, 0, 0)),
        out_shape=jax.ShapeDtypeStruct((heads, t // tm, MLA_DK_PAD, tm), BF16),
        compiler_params=_params("parallel", "arbitrary"),
        name="mla_q_up",
    )(proj_a, g.reshape(1, Q_LORA), w_q, cos_p, sin_p)


def _kvup_kernel(ckv_ref, g_ref, w_ref, kpe_ref, cos_ref, sin_ref, k_ref, vt_ref):
    an = _rms(ckv_ref[...], g_ref[...]).astype(BF16)
    res = _dot(an, w_ref[...])
    k_rot = _rope_pair(kpe_ref[...], cos_ref, sin_ref)
    k_ref[0] = jnp.concatenate([res[:, :MLA_NOPE], k_rot], axis=1).astype(k_ref.dtype)
    vt_ref[0, 0] = _with_ones_rows(res[:, MLA_NOPE:].T).astype(vt_ref.dtype)


def mla_kv_up(proj_a, g, w_kv, cos_p, sin_p, seq):
    t = proj_a.shape[0]
    heads = w_kv.shape[1] // (MLA_NOPE + MLA_V)
    tm = _tile(seq, ATTN_TILE)
    per_b = seq // tm
    hw = MLA_NOPE + MLA_V
    return pl.pallas_call(
        _kvup_kernel,
        grid=(t // tm, heads),
        in_specs=[pl.BlockSpec((tm, KV_LORA), lambda i, h: (i, Q_LORA // KV_LORA)),
                  pl.BlockSpec((1, KV_LORA), lambda i, h: (0, 0)),
                  pl.BlockSpec((KV_LORA, hw), lambda i, h: (0, h)),
                  pl.BlockSpec((tm, LANE), lambda i, h: (i, (Q_LORA + KV_LORA) // LANE)),
                  pl.BlockSpec((tm, LANE), lambda i, h: (i % per_b, 0)),
                  pl.BlockSpec((tm, LANE), lambda i, h: (i % per_b, 0))],
        out_specs=[pl.BlockSpec((1, tm, MLA_DK_PAD), lambda i, h: (h, i, 0)),
                   pl.BlockSpec((1, 1, MLA_V + ONES_ROWS, tm), lambda i, h: (h, i, 0, 0))],
        out_shape=[jax.ShapeDtypeStruct((heads, t, MLA_DK_PAD), BF16),
                   jax.ShapeDtypeStruct((heads, t // tm, MLA_V + ONES_ROWS, tm), BF16)],
        compiler_params=_params("parallel", "arbitrary"),
        name="mla_kv_up",
    )(proj_a, g.reshape(1, KV_LORA), w_kv, proj_a, cos_p, sin_p)


def _softmax_first(s, vt, m_ref, acc_ref):
    m = jnp.max(s, axis=0, keepdims=True)
    p = jnp.exp(s - m).astype(vt.dtype)
    m_ref[...] = m
    acc_ref[...] = _dot(vt, p)


def _softmax_step(s, vt, m_ref, acc_ref):
    m_prev = m_ref[...]
    m_new = jnp.maximum(m_prev, jnp.max(s, axis=0, keepdims=True))
    alpha = jnp.exp(m_prev - m_new)
    p = jnp.exp(s - m_new).astype(vt.dtype)
    acc_ref[...] = alpha * acc_ref[...] + _dot(vt, p)
    m_ref[...] = m_new


def _softmax_finish(acc_ref, o_ref, dv):
    acc = acc_ref[...]
    o_ref[...] = (acc[:dv, :] / acc[dv:dv + 1, :]).T.astype(o_ref.dtype)


def _causal_mask_t(s):
    key = lax.broadcasted_iota(jnp.int32, s.shape, 0)
    qry = lax.broadcasted_iota(jnp.int32, s.shape, 1)
    return jnp.where(key <= qry, s, NEG_INF)


def _mla_attn_kernel(qt_ref, k_ref, vt_ref, o_ref, m_ref, acc_ref, *, tq, dv):
    i = pl.program_id(2)
    qt = qt_ref[0, 0]

    def scores(j):
        off = pl.multiple_of(j * tq, tq)
        return _dot(k_ref[0, pl.ds(off, tq), :], qt)

    _softmax_first(_causal_mask_t(scores(i)), vt_ref[0, i], m_ref, acc_ref)

    def body(j, carry):
        _softmax_step(scores(j), vt_ref[0, j], m_ref, acc_ref)
        return carry

    lax.fori_loop(0, i, body, 0)
    _softmax_finish(acc_ref, o_ref, dv)


def mla_attention(qt, k, vt, batch, seq, out_dtype):
    heads, _, dk, tq = qt.shape
    t = k.shape[1]
    dvx = vt.shape[2]
    dv = dvx - ONES_ROWS
    nq = seq // tq
    return pl.pallas_call(
        functools.partial(_mla_attn_kernel, tq=tq, dv=dv),
        grid=(batch, heads, nq),
        in_specs=[pl.BlockSpec((1, 1, dk, tq), lambda b, h, i: (h, b * nq + i, 0, 0)),
                  pl.BlockSpec((1, seq, dk), lambda b, h, i: (h, b, 0)),
                  pl.BlockSpec((1, nq, dvx, tq), lambda b, h, i: (h, b, 0, 0))],
        out_specs=pl.BlockSpec((tq, dv), lambda b, h, i: (b * nq + i, h)),
        out_shape=jax.ShapeDtypeStruct((t, heads * dv), out_dtype),
        scratch_shapes=[pltpu.VMEM((1, tq), F32), pltpu.VMEM((dvx, tq), F32)],
        compiler_params=_params("parallel", "parallel", "arbitrary"),
        name="mla_attention",
    )(qt, k, vt)


def _conv_kernel(b_ref, c_ref, h_ref, w_ref, o_ref, carry_ref, *, per_b):
    i = pl.program_id(0)

    @pl.when(i % per_b == 0)
    def _():
        carry_ref[...] = jnp.zeros_like(carry_ref)

    tm = o_ref.shape[0]
    w0, w1, w2 = w_ref[0:1, :], w_ref[1:2, :], w_ref[2:3, :]
    u = c_ref[...] * h_ref[...]
    y = w0 * pltpu.roll(u, 2, 0) + w1 * pltpu.roll(u, 1, 0) + w2 * u
    o_ref[...] = (b_ref[...] * y).astype(o_ref.dtype)
    u8 = u[0:SUBLANE, :]
    tail = carry_ref[...]
    r8 = lax.broadcasted_iota(jnp.int32, u8.shape, 0)
    p1 = jnp.where(r8 < 1, pltpu.roll(tail, 1, 0), pltpu.roll(u8, 1, 0))
    p2 = jnp.where(r8 < 2, pltpu.roll(tail, 2, 0), pltpu.roll(u8, 2, 0))
    y8 = w0 * p2 + w1 * p1 + w2 * u8
    o_ref[0:SUBLANE, :] = (b_ref[0:SUBLANE, :] * y8).astype(o_ref.dtype)
    carry_ref[...] = u[tm - SUBLANE:tm, :]


def conv_mixer(bch, conv_w, seq, out_dtype):
    t = bch.shape[0]
    ch = conv_w.shape[1]
    tm = _tile(seq, 256)
    per_b = seq // tm
    return pl.pallas_call(
        functools.partial(_conv_kernel, per_b=per_b),
        grid=(t // tm,),
        in_specs=[pl.BlockSpec((tm, ch), lambda i: (i, 0)),
                  pl.BlockSpec((tm, ch), lambda i: (i, 1)),
                  pl.BlockSpec((tm, ch), lambda i: (i, 2)),
                  pl.BlockSpec((CONV_W, ch), lambda i: (0, 0))],
        out_specs=pl.BlockSpec((tm, ch), lambda i: (i, 0)),
        out_shape=jax.ShapeDtypeStruct((t, ch), out_dtype),
        scratch_shapes=[pltpu.VMEM((SUBLANE, ch), F32)],
        compiler_params=_params("arbitrary"),
        name="conv_mixer",
    )(bch, bch, bch, conv_w)


def _moba_prep_kernel(q_ref, k_ref, v_ref, cos_ref, sin_ref, qo_ref, ko_ref, vo_ref, km_ref):
    heads = ko_ref.shape[0]
    cos, sin = cos_ref[...], sin_ref[...]
    n_sub = km_ref.shape[0]
    means = [[] for _ in range(n_sub)]
    for h in range(heads):
        sl = slice(h * MOBA_HD, (h + 1) * MOBA_HD)
        qh = q_ref[:, sl]
        kh = k_ref[:, sl]
        qr = qh * cos + pltpu.roll(qh, MOBA_HD // 2, 1) * sin
        qo_ref[h, 0] = qr.T.astype(qo_ref.dtype)
        kr = kh * cos + pltpu.roll(kh, MOBA_HD // 2, 1) * sin
        ko_ref[h] = kr.astype(ko_ref.dtype)
        vo_ref[h, 0] = _with_ones_rows(v_ref[:, sl].T).astype(vo_ref.dtype)
        for s in range(n_sub):
            means[s].append(jnp.mean(kr[s * MOBA_BLOCK:(s + 1) * MOBA_BLOCK], axis=0, keepdims=True))
    for s in range(n_sub):
        km_ref[s] = jnp.concatenate(means[s], axis=0)


def moba_prep(qkv, cos_f, sin_s, seq):
    t = qkv.shape[0]
    width = qkv.shape[1] // 3
    heads = width // MOBA_HD
    tm = 2 * MOBA_BLOCK
    assert seq % tm == 0
    per_b = seq // tm
    nt = t // tm
    return pl.pallas_call(
        _moba_prep_kernel,
        grid=(nt,),
        in_specs=[pl.BlockSpec((tm, width), lambda i: (i, 0)),
                  pl.BlockSpec((tm, width), lambda i: (i, 1)),
                  pl.BlockSpec((tm, width), lambda i: (i, 2)),
                  pl.BlockSpec((tm, MOBA_HD), lambda i: (i % per_b, 0)),
                  pl.BlockSpec((tm, MOBA_HD), lambda i: (i % per_b, 0))],
        out_specs=[pl.BlockSpec((heads, 1, MOBA_HD, tm), lambda i: (0, i, 0, 0)),
                   pl.BlockSpec((heads, tm, MOBA_HD), lambda i: (0, i, 0)),
                   pl.BlockSpec((heads, 1, MOBA_HD + ONES_ROWS, tm), lambda i: (0, i, 0, 0)),
                   pl.BlockSpec((2, heads, MOBA_HD), lambda i: (i, 0, 0))],
        out_shape=[jax.ShapeDtypeStruct((heads, nt, MOBA_HD, tm), BF16),
                   jax.ShapeDtypeStruct((heads, t, MOBA_HD), BF16),
                   jax.ShapeDtypeStruct((heads, nt, MOBA_HD + ONES_ROWS, tm), BF16),
                   jax.ShapeDtypeStruct((2 * nt, heads, MOBA_HD), F32)],
        compiler_params=_params("arbitrary"),
        name="moba_prep",
    )(qkv, qkv, qkv, cos_f, sin_s)


def _moba_attn_kernel(qt_ref, k_ref, vt_ref, km_ref, o_ref, m_ref, acc_ref, sel_ref, *, scale, dv):
    i = pl.program_id(2)
    qtf = qt_ref[0, 0].astype(F32)
    n_blk = km_ref.shape[2]
    tq = qtf.shape[1]
    blk_w = MOBA_BLOCK
    g = _dot(km_ref[0, 0], qtf, precision=lax.Precision.HIGHEST)
    blk = lax.broadcasted_iota(jnp.int32, g.shape, 0)
    own = 2 * i + (lax.broadcasted_iota(jnp.int32, g.shape, 1) >= blk_w).astype(jnp.int32)
    past = blk < own
    g = jnp.where(past, g, NEG_INF)
    sel = jnp.zeros(g.shape, F32)
    for _ in range(min(MOBA_TOPK, n_blk)):
        mx = jnp.max(g, axis=0, keepdims=True)
        first = jnp.min(jnp.where(g == mx, blk, n_blk), axis=0, keepdims=True)
        hit = blk == first
        sel = jnp.where(hit, 1.0, sel)
        g = jnp.where(hit, NEG_INF, g)
    sel_ref[...] = jnp.where(past, sel, 0.0)

    qt = (qtf * scale).astype(k_ref.dtype)

    def scores(p):
        off = pl.multiple_of(p * tq, tq)
        return _dot(k_ref[0, pl.ds(off, tq), :], qt)

    def chosen(n):
        return sel_ref[pl.ds(n, 1), :] > 0.5

    s = scores(i)
    key = lax.broadcasted_iota(jnp.int32, (blk_w, tq), 0)
    qry = lax.broadcasted_iota(jnp.int32, (blk_w, tq), 1)
    chosen_lim = jnp.where(sel_ref[pl.ds(2 * i, 1), :] > 0.5, blk_w, -1)
    top_ok = key <= jnp.where(qry < blk_w, qry, chosen_lim)
    bot_ok = key <= qry - blk_w
    s = jnp.concatenate([jnp.where(top_ok, s[:blk_w], NEG_INF),
                         jnp.where(bot_ok, s[blk_w:], NEG_INF)], axis=0)
    _softmax_first(s, vt_ref[0, i], m_ref, acc_ref)

    def body(p, carry):
        s = scores(p)
        s = jnp.concatenate([jnp.where(chosen(2 * p), s[:blk_w], NEG_INF),
                             jnp.where(chosen(2 * p + 1), s[blk_w:], NEG_INF)], axis=0)
        _softmax_step(s, vt_ref[0, p], m_ref, acc_ref)
        return carry

    lax.fori_loop(0, i, body, 0)
    _softmax_finish(acc_ref, o_ref, dv)


def moba_attention(qt, k, vt, k_mean, batch, seq, out_dtype):
    heads, _, hd, tq = qt.shape
    t = k.shape[1]
    dvx = vt.shape[2]
    dv = dvx - ONES_ROWS
    n_blk = seq // MOBA_BLOCK
    nq = seq // tq
    return pl.pallas_call(
        functools.partial(_moba_attn_kernel, scale=hd ** -0.5, dv=dv),
        grid=(batch, heads, nq),
        in_specs=[pl.BlockSpec((1, 1, hd, tq), lambda b, h, i: (h, b * nq + i, 0, 0)),
                  pl.BlockSpec((1, seq, hd), lambda b, h, i: (h, b, 0)),
                  pl.BlockSpec((1, nq, dvx, tq), lambda b, h, i: (h, b, 0, 0)),
                  pl.BlockSpec((1, 1, n_blk, hd), lambda b, h, i: (b, h, 0, 0))],
        out_specs=pl.BlockSpec((tq, dv), lambda b, h, i: (b * nq + i, h)),
        out_shape=jax.ShapeDtypeStruct((t, heads * dv), out_dtype),
        scratch_shapes=[pltpu.VMEM((1, tq), F32), pltpu.VMEM((dvx, tq), F32),
                        pltpu.VMEM((n_blk, tq), F32)],
        compiler_params=_params("parallel", "parallel", "arbitrary"),
        name="moba_attention",
    )(qt, k, vt, k_mean)


def _group_norm_kernel(a_ref, b_ref, c_ref, ga_ref, gb_ref, gc_ref, o_ref):
    wa, wb = a_ref.shape[1], b_ref.shape[1]
    o_ref[:, 0:wa] = _rms(a_ref[...].astype(F32), ga_ref[...]).astype(o_ref.dtype)
    o_ref[:, wa:wa + wb] = _rms(b_ref[...].astype(F32), gb_ref[...]).astype(o_ref.dtype)
    o_ref[:, wa + wb:] = _rms(c_ref[...].astype(F32), gc_ref[...]).astype(o_ref.dtype)


def group_norm(y_a, y_b, y_c, g):
    t = y_a.shape[0]
    wa, wb, wc = y_a.shape[1], y_b.shape[1], y_c.shape[1]
    tm = _tile(t, 256)
    ga, gb, gc = g[:wa].reshape(1, wa), g[wa:wa + wb].reshape(1, wb), g[wa + wb:].reshape(1, wc)
    rows = lambda w: pl.BlockSpec((tm, w), lambda i: (i, 0))
    vec = lambda w: pl.BlockSpec((1, w), lambda i: (0, 0))
    return pl.pallas_call(
        _group_norm_kernel,
        grid=(t // tm,),
        in_specs=[rows(wa), rows(wb), rows(wc), vec(wa), vec(wb), vec(wc)],
        out_specs=rows(wa + wb + wc),
        out_shape=jax.ShapeDtypeStruct((t, wa + wb + wc), BF16),
        compiler_params=_params("arbitrary"),
        name="group_norm",
    )(y_a, y_b, y_c, ga, gb, gc)


def _ffn_kernel(bexp_ref, nused_ref, x_ref, wg_ref, wu_ref, wd_ref, o_ref):
    b = pl.program_id(1)

    @pl.when(b < nused_ref[0])
    def _():
        x = x_ref[...]
        gate = _dot(x, wg_ref[0])
        up = _dot(x, wu_ref[0])
        a = (gate * jax.nn.sigmoid(gate) * up).astype(x.dtype)
        o_ref[0] = _dot(a, wd_ref[0]).astype(o_ref.dtype)

    @pl.when(b >= nused_ref[0])
    def _():
        o_ref[...] = jnp.zeros_like(o_ref)


def expert_ffn(xs, block_exp, n_used, w_gate, w_up, w_down):
    n_slots, d = xs.shape
    ff = w_gate.shape[2]
    fh = ff // FF_SPLIT
    n_blocks = n_slots // MOE_BLOCK
    grid_spec = pltpu.PrefetchScalarGridSpec(
        num_scalar_prefetch=2,
        grid=(FF_SPLIT, n_blocks),
        in_specs=[pl.BlockSpec((MOE_BLOCK, d), lambda j, b, be, nu: (b, 0)),
                  pl.BlockSpec((1, d, fh), lambda j, b, be, nu: (be[b], 0, j)),
                  pl.BlockSpec((1, d, fh), lambda j, b, be, nu: (be[b], 0, j)),
                  pl.BlockSpec((1, fh, d), lambda j, b, be, nu: (be[b], j, 0))],
        out_specs=pl.BlockSpec((1, MOE_BLOCK, d), lambda j, b, be, nu: (j, b, 0)),
    )
    return pl.pallas_call(
        _ffn_kernel,
        grid_spec=grid_spec,
        out_shape=jax.ShapeDtypeStruct((FF_SPLIT, n_slots, d), BF16),
        compiler_params=_params("arbitrary", "arbitrary"),
        name="expert_ffn",
    )(block_exp, n_used, xs, w_gate, w_up, w_down)


def _combine_kernel(x_ref, gate_ref, y_ref, w_ref, o_ref):
    w = w_ref[...]
    acc = jnp.zeros(x_ref.shape, F32)
    for kk in range(y_ref.shape[1]):
        yk = y_ref[0, kk].astype(F32)
        for j in range(1, y_ref.shape[0]):
            yk = yk + y_ref[j, kk].astype(F32)
        acc = acc + w[:, kk:kk + 1] * yk
    o_ref[...] = x_ref[...] + gate_ref[0] * acc


def moe_combine(x2, gate, yg, w_tok, seq):
    t, d = x2.shape
    nj, nk = yg.shape[0], yg.shape[1]
    tm = _tile(seq, 256)
    per_b = seq // tm
    return pl.pallas_call(
        _combine_kernel,
        grid=(t // tm,),
        in_specs=[pl.BlockSpec((tm, d), lambda i: (i, 0)),
                  pl.BlockSpec((1, 1, d), lambda i: (i // per_b, 0, 0)),
                  pl.BlockSpec((nj, nk, tm, d), lambda i: (0, 0, i, 0)),
                  pl.BlockSpec((tm, nk), lambda i: (i, 0))],
        out_specs=pl.BlockSpec((tm, d), lambda i: (i, 0)),
        out_shape=jax.ShapeDtypeStruct((t, d), F32),
        compiler_params=_params("arbitrary"),
        name="moe_combine",
    )(x2, gate, yg, w_tok)


def _group_by_expert(e_idx):
    n_tok = e_idx.shape[1]
    n_asg = n_tok * TOP_K
    flat_e = e_idx.reshape(n_asg)
    onehot = (flat_e[:, None] == jnp.arange(N_EXPERTS)[None, :]).astype(jnp.int32)
    running = jnp.cumsum(onehot, axis=0)
    rank = jnp.sum(running * onehot, axis=1) - 1
    sizes = running[-1]
    padded = (sizes + MOE_BLOCK - 1) // MOE_BLOCK * MOE_BLOCK
    pad_end = jnp.cumsum(padded)
    pad_start = pad_end - padded
    dest = (jnp.sum(pad_start[None, :] * onehot, axis=1) + rank).astype(jnp.int32)
    n_blocks = -(-n_asg // MOE_BLOCK) + N_EXPERTS
    n_slots = n_blocks * MOE_BLOCK
    slot_tok = jnp.zeros((n_slots,), jnp.int32).at[dest].set(
        jnp.arange(n_asg, dtype=jnp.int32) % n_tok, mode="promise_in_bounds")
    block_start = jnp.arange(n_blocks, dtype=jnp.int32) * MOE_BLOCK
    block_exp = jnp.minimum(jnp.sum(block_start[:, None] >= pad_end[None, :], axis=1),
                            N_EXPERTS - 1).astype(jnp.int32)
    n_used = (pad_end[-1] // MOE_BLOCK).astype(jnp.int32).reshape(1)
    return dest.reshape(TOP_K, n_tok), slot_tok, block_exp, n_used


def _rot_cols(w):
    half = w.shape[-1] // 2
    return jnp.concatenate([-w[..., half:], w[..., :half]], axis=-1)


def _prep_layer(w_in, w_uq, w_ukv, w_out, w_gate, w_up, w_down):
    a_end = Q_LORA + KV_LORA + MLA_ROPE
    b_end = a_end + 3 * CONV_CH
    w_kpe = w_in[:, Q_LORA + KV_LORA:a_end]
    w_a = jnp.concatenate([w_in[:, :a_end], _rot_cols(w_kpe)], axis=1).astype(BF16)
    w_b = w_in[:, a_end:b_end].astype(BF16)
    w_c = w_in[:, b_end:].astype(BF16)
    wq = w_uq.reshape(Q_LORA, MLA_HEADS, MLA_NOPE + MLA_ROPE)
    wq_rope = wq[..., MLA_NOPE:]
    wq = jnp.concatenate([wq, _rot_cols(wq_rope)], axis=-1).transpose(1, 0, 2).astype(BF16)
    return dict(w_a=w_a, w_b=w_b, w_c=w_c, w_q=wq, w_kv=w_ukv.astype(BF16),
                w_out=w_out.astype(BF16), w_gate=w_gate.astype(BF16), w_up=w_up.astype(BF16),
                w_down=w_down.astype(BF16))


def _rope_tables(seq, dim):
    inv = 1.0 / (ROPE_THETA ** (jnp.arange(0, dim, 2, dtype=F32) / dim))
    ang = jnp.arange(seq, dtype=F32)[:, None] * inv[None, :]
    return jnp.cos(ang), jnp.sin(ang)


def _take_rows(a, idx):
    return a.at[idx].get(mode="promise_in_bounds")


def kernel(x, c, w_mod, mod_table, mix_norm_g, w_in, q_norm_g, kv_norm_g, w_uq, w_ukv, conv_w,
           group_norm_g, w_out, ffn_norm_g, w_router, router_bias, w_gate, w_up, w_down,
           final_norm_g):
    batch, seq, d = x.shape
    depth = w_in.shape[0]
    t = batch * seq
    x2 = x.reshape(t, d)

    cos_a, sin_a = _rope_tables(seq, MLA_ROPE)
    zeros_a = jnp.zeros_like(cos_a)
    cos_p = jnp.concatenate([cos_a, cos_a, zeros_a, zeros_a], axis=1)
    sin_p = jnp.concatenate([sin_a, sin_a, zeros_a, zeros_a], axis=1)
    cos_b, sin_b = _rope_tables(seq, MOBA_HD)
    cos_f = jnp.concatenate([cos_b, cos_b], axis=1)
    sin_s = jnp.concatenate([-sin_b, sin_b], axis=1)

    c_pad = jnp.zeros((SUBLANE, d), F32).at[:batch].set(c)
    mod_shared = mod_matmul(c_pad, w_mod)[:batch].reshape(batch, N_MOD, d)

    wr_t = w_router.T
    wr_hi = wr_t.astype(BF16)
    wr_lo = (wr_t - wr_hi.astype(F32)).astype(BF16)
    router = (wr_hi, wr_lo, router_bias.astype(F32).reshape(N_EXPERTS, 1))
    mla_scale = (MLA_NOPE + MLA_ROPE) ** -0.5

    for l in range(depth):
        p = _prep_layer(w_in[l], w_uq[l], w_ukv[l], w_out[l], w_gate[l], w_up[l], w_down[l])
        mod = mod_shared + mod_table[l][None]
        sh1, sc1, g1, sh2, sc2, g2 = [mod[:, i][:, None, :] for i in range(N_MOD)]

        h = norm_mod(x2, mix_norm_g[l], 1.0 + sc1, sh1, seq)
        proj_a = matmul(h, p["w_a"], F32, tk=512)
        bch = matmul(h, p["w_b"], F32)
        qkv = matmul(h, p["w_c"], F32)

        q_a = mla_q_up(proj_a, q_norm_g[l], p["w_q"], cos_p, sin_p, seq, mla_scale)
        k_a, v_a = mla_kv_up(proj_a, kv_norm_g[l], p["w_kv"], cos_p, sin_p, seq)
        y_a = mla_attention(q_a, k_a, v_a, batch, seq, BF16)

        y_b = conv_mixer(bch, conv_w[l], seq, F32)

        q_c, k_c, v_c, k_mean = moba_prep(qkv, cos_f, sin_s, seq)
        n_blk = seq // MOBA_BLOCK
        k_mean = k_mean.reshape(batch, n_blk, MOBA_HEADS, MOBA_HD).transpose(0, 2, 1, 3)
        y_c = moba_attention(q_c, k_c, v_c, k_mean, batch, seq, BF16)

        y = group_norm(y_a, y_b, y_c, group_norm_g[l])
        x2 = matmul_residual(y, p["w_out"], x2, g1, seq)

        h2, e_idx, gates = norm_mod(x2, ffn_norm_g[l], 1.0 + sc2, sh2, seq, router=router)
        pos, slot_tok, block_exp, n_used = _group_by_expert(e_idx)
        xs = _take_rows(h2, slot_tok)
        yp = expert_ffn(xs, block_exp, n_used, p["w_gate"], p["w_up"], p["w_down"])
        n_slots = yp.shape[1]
        idx = pos[None] + (jnp.arange(FF_SPLIT, dtype=jnp.int32) * n_slots)[:, None, None]
        yg = _take_rows(yp.reshape(FF_SPLIT * n_slots, d), idx.reshape(-1))
        x2 = moe_combine(x2, g2, yg.reshape(FF_SPLIT, TOP_K, t, d), gates.T, seq)

    out = norm_mod(x2, final_norm_g, None, None, seq, out_dtype=F32)
    return out.reshape(batch, seq, d)
```

```python
import functools

import jax
import jax.numpy as jnp
from jax import lax
from jax.experimental import pallas as pl
from jax.experimental.pallas import tpu as pltpu

MLA_HEADS = 16
MLA_NOPE = 128
MLA_ROPE = 64
MLA_V = 128
Q_LORA = 1024
KV_LORA = 512
MLA_DK_PAD = 256
CONV_CH = 1024
CONV_W = 3
MOBA_HEADS = 8
MOBA_HD = 128
MOBA_BLOCK = 256
MOBA_TOPK = 3
ROPE_THETA = 10000.0
EPS = 1e-6
N_MOD = 6
N_EXPERTS = 16
N_GROUPS = 4
TOP_K = 2
MOE_BLOCK = 256
FF_SPLIT = 2
UP_HEADS_PER_STEP = 4
ATTN_TILE = 512
ONES_ROWS = 16

V7X_VMEM_LIMIT_BYTES = 56 * 1024 * 1024
LANE = 128
SUBLANE = 8

F32 = jnp.float32
BF16 = jnp.bfloat16
NEG_INF = float("-inf")


def _params(*sem):
    return pltpu.CompilerParams(dimension_semantics=sem, vmem_limit_bytes=V7X_VMEM_LIMIT_BYTES)


def _dot(a, b, precision=None):
    return jnp.dot(a, b, preferred_element_type=F32, precision=precision)


def _dot_nt(a, b):
    return lax.dot_general(a, b, (((1,), (1,)), ((), ())), preferred_element_type=F32)


def _tile(dim, want):
    return want if dim % want == 0 else dim


def _mod_kernel(c_ref, w_ref, o_ref):
    @pl.when(pl.program_id(0) == 0)
    def _():
        o_ref[...] = jnp.zeros_like(o_ref)

    c = c_ref[...]
    a = (c * jax.nn.sigmoid(c)).astype(BF16)
    o_ref[...] += _dot(a, w_ref[...].astype(BF16))


def mod_matmul(c_pad, w_mod):
    rows, d = c_pad.shape
    n = w_mod.shape[1]
    tk = _tile(d, LANE)
    return pl.pallas_call(
        _mod_kernel,
        grid=(d // tk,),
        in_specs=[pl.BlockSpec((rows, tk), lambda k: (0, k)),
                  pl.BlockSpec((tk, n), lambda k: (k, 0))],
        out_specs=pl.BlockSpec((rows, n), lambda k: (0, 0)),
        out_shape=jax.ShapeDtypeStruct((rows, n), F32),
        compiler_params=_params("arbitrary"),
        name="mod_matmul",
    )(c_pad, w_mod)


def _rms(x, g):
    return x * lax.rsqrt(jnp.mean(x * x, axis=-1, keepdims=True) + EPS) * g


def _norm_mod_kernel(x_ref, g_ref, sc_ref, sh_ref, o_ref):
    y = _rms(x_ref[...], g_ref[...])
    o_ref[...] = (y * sc_ref[0] + sh_ref[0]).astype(o_ref.dtype)


def _norm_kernel(x_ref, g_ref, o_ref):
    o_ref[...] = _rms(x_ref[...], g_ref[...]).astype(o_ref.dtype)


def _top2_sum(a, b, c, d):
    hi1, lo1 = jnp.maximum(a, b), jnp.minimum(a, b)
    hi2, lo2 = jnp.maximum(c, d), jnp.minimum(c, d)
    return jnp.maximum(hi1, hi2) + jnp.maximum(jnp.minimum(hi1, hi2), jnp.maximum(lo1, lo2))


def _norm_router_kernel(x_ref, g_ref, sc_ref, sh_ref, whi_ref, wlo_ref, bias_ref,
                        o_ref, e_ref, gate_ref):
    h = _rms(x_ref[...], g_ref[...]) * sc_ref[0] + sh_ref[0]
    hi = h.astype(BF16)
    o_ref[...] = hi
    lo = (h - hi.astype(F32)).astype(BF16)
    whi, wlo = whi_ref[...], wlo_ref[...]
    logits = _dot_nt(whi, hi) + _dot_nt(wlo, hi) + _dot_nt(whi, lo)
    scores = jax.nn.sigmoid(logits)
    biased = scores + bias_ref[...]
    per_grp = N_EXPERTS // N_GROUPS
    b_rows = [biased[e:e + 1, :] for e in range(N_EXPERTS)]
    s_rows = [scores[e:e + 1, :] for e in range(N_EXPERTS)]
    grp_scores = [_top2_sum(*b_rows[g * per_grp:(g + 1) * per_grp]) for g in range(N_GROUPS)]
    best, grp = grp_scores[0], jnp.zeros(grp_scores[0].shape, jnp.int32)
    for g in range(1, N_GROUPS):
        better = grp_scores[g] > best
        grp = jnp.where(better, g, grp)
        best = jnp.where(better, grp_scores[g], best)
    b4, s4 = b_rows[:per_grp], s_rows[:per_grp]
    for g in range(1, N_GROUPS):
        in_g = grp == g
        b4 = [jnp.where(in_g, b_rows[g * per_grp + j], b4[j]) for j in range(per_grp)]
        s4 = [jnp.where(in_g, s_rows[g * per_grp + j], s4[j]) for j in range(per_grp)]
    v1, i1 = b4[0], jnp.zeros(grp.shape, jnp.int32)
    for j in range(1, per_grp):
        better = b4[j] > v1
        i1 = jnp.where(better, j, i1)
        v1 = jnp.where(better, b4[j], v1)
    v2, i2 = jnp.full(v1.shape, NEG_INF, F32), jnp.zeros(grp.shape, jnp.int32)
    for j in range(per_grp):
        better = (i1 != j) & (b4[j] > v2)
        i2 = jnp.where(better, j, i2)
        v2 = jnp.where(better, b4[j], v2)
    g1, g2 = s4[0], s4[0]
    for j in range(1, per_grp):
        g1 = jnp.where(i1 == j, s4[j], g1)
        g2 = jnp.where(i2 == j, s4[j], g2)
    total = g1 + g2
    e_ref[...] = jnp.concatenate([grp * per_grp + i1, grp * per_grp + i2], axis=0)
    gate_ref[...] = jnp.concatenate([g1 / total, g2 / total], axis=0)


def norm_mod(x2, g, sc1p, sh, seq, out_dtype=BF16, router=None):
    t, d = x2.shape
    tm = _tile(seq, 256)
    per_b = seq // tm
    row = pl.BlockSpec((tm, d), lambda i: (i, 0))
    vec = pl.BlockSpec((1, d), lambda i: (0, 0))
    mod = pl.BlockSpec((1, 1, d), lambda i: (i // per_b, 0, 0))
    g2 = g.reshape(1, d)
    if sc1p is None:
        return pl.pallas_call(
            _norm_kernel, grid=(t // tm,), in_specs=[row, vec], out_specs=row,
            out_shape=jax.ShapeDtypeStruct((t, d), out_dtype),
            compiler_params=_params("arbitrary"), name="final_norm")(x2, g2)
    if router is None:
        return pl.pallas_call(
            _norm_mod_kernel, grid=(t // tm,), in_specs=[row, vec, mod, mod], out_specs=row,
            out_shape=jax.ShapeDtypeStruct((t, d), out_dtype),
            compiler_params=_params("arbitrary"), name="norm_mod")(x2, g2, sc1p, sh)
    whi_t, wlo_t, bias = router
    ne = whi_t.shape[0]
    wspec = pl.BlockSpec((ne, d), lambda i: (0, 0))
    kspec = pl.BlockSpec((TOP_K, tm), lambda i: (0, i))
    return pl.pallas_call(
        _norm_router_kernel, grid=(t // tm,),
        in_specs=[row, vec, mod, mod, wspec, wspec, pl.BlockSpec((ne, 1), lambda i: (0, 0))],
        out_specs=[row, kspec, kspec],
        out_shape=[jax.ShapeDtypeStruct((t, d), out_dtype),
                   jax.ShapeDtypeStruct((TOP_K, t), jnp.int32),
                   jax.ShapeDtypeStruct((TOP_K, t), F32)],
        compiler_params=_params("arbitrary"), name="norm_router")(x2, g2, sc1p, sh, whi_t, wlo_t, bias)


def _mm_kernel(a_ref, w_ref, o_ref):
    o_ref[...] = _dot(a_ref[...], w_ref[...]).astype(o_ref.dtype)


def matmul(a, w, out_dtype, tm=1024, tn=1024):
    m, kd = a.shape
    n = w.shape[1]
    tm, tn = _tile(m, tm), _tile(n, tn)
    return pl.pallas_call(
        _mm_kernel,
        grid=(m // tm, n // tn),
        in_specs=[pl.BlockSpec((tm, kd), lambda i, j: (i, 0)),
                  pl.BlockSpec((kd, tn), lambda i, j: (0, j))],
        out_specs=pl.BlockSpec((tm, tn), lambda i, j: (i, j)),
        out_shape=jax.ShapeDtypeStruct((m, n), out_dtype),
        compiler_params=_params("parallel", "parallel"),
        name="matmul",
    )(a, w)


def _mm_res_kernel(a_ref, w_ref, x_ref, gate_ref, o_ref):
    o_ref[...] = x_ref[...] + gate_ref[0] * _dot(a_ref[...], w_ref[...])


def matmul_residual(a, w, x2, gate, seq, tm=1024, tn=1024):
    m, kd = a.shape
    n = w.shape[1]
    tm, tn = _tile(seq, tm), _tile(n, tn)
    per_b = seq // tm
    return pl.pallas_call(
        _mm_res_kernel,
        grid=(m // tm, n // tn),
        in_specs=[pl.BlockSpec((tm, kd), lambda i, j: (i, 0)),
                  pl.BlockSpec((kd, tn), lambda i, j: (0, j)),
                  pl.BlockSpec((tm, tn), lambda i, j: (i, j)),
                  pl.BlockSpec((1, 1, tn), lambda i, j: (i // per_b, 0, j))],
        out_specs=pl.BlockSpec((tm, tn), lambda i, j: (i, j)),
        out_shape=jax.ShapeDtypeStruct((m, n), F32),
        compiler_params=_params("parallel", "parallel"),
        name="matmul_residual",
    )(a, w, x2, gate)


def _rope_pair(r2, cos_ref, sin_ref):
    return r2 * cos_ref[...] + pltpu.roll(r2, MLA_ROPE, 1) * sin_ref[...]


def _with_ones_rows(vt):
    return jnp.concatenate([vt, jnp.ones((ONES_ROWS, vt.shape[1]), vt.dtype)], axis=0)


def _qup_kernel(cq_ref, g_ref, w_ref, cos_ref, sin_ref, o_ref, an_ref, *, scale):
    @pl.when(pl.program_id(1) == 0)
    def _():
        an_ref[...] = _rms(cq_ref[...], g_ref[...]).astype(BF16)

    an = an_ref[...]
    for hh in range(w_ref.shape[0]):
        res = _dot(an, w_ref[hh])
        roped = _rope_pair(res[:, MLA_NOPE:], cos_ref, sin_ref)
        q = jnp.concatenate([res[:, :MLA_NOPE], roped], axis=1) * scale
        o_ref[hh, 0] = q.T.astype(o_ref.dtype)


def mla_q_up(proj_a, g, w_q, cos_p, sin_p, seq, scale):
    t = proj_a.shape[0]
    heads = w_q.shape[0]
    hb = UP_HEADS_PER_STEP
    tm = _tile(seq, ATTN_TILE)
    per_b = seq // tm
    return pl.pallas_call(
        functools.partial(_qup_kernel, scale=scale),
        grid=(t // tm, heads // hb),
        in_specs=[pl.BlockSpec((tm, Q_LORA), lambda i, h: (i, 0)),
                  pl.BlockSpec((1, Q_LORA), lambda i, h: (0, 0)),
                  pl.BlockSpec((hb, Q_LORA, MLA_DK_PAD), lambda i, h: (h, 0, 0)),
                  pl.BlockSpec((tm, LANE), lambda i, h: (i % per_b, 0)),
                  pl.BlockSpec((tm, LANE), lambda i, h: (i % per_b, 0))],
        out_specs=pl.BlockSpec((hb, 1, MLA_DK_PAD, tm), lambda i, h: (h, i, 0, 0)),
        out_shape=jax.ShapeDtypeStruct((heads, t // tm, MLA_DK_PAD, tm), BF16),
        scratch_shapes=[pltpu.VMEM((tm, Q_LORA), BF16)],
        compiler_params=_params("parallel", "arbitrary"),
        name="mla_q_up",
    )(proj_a, g.reshape(1, Q_LORA), w_q, cos_p, sin_p)


def _kvup_kernel(ckv_ref, g_ref, w_ref, kpe_ref, cos_ref, sin_ref, k_ref, vt_ref, an_ref):
    @pl.when(pl.program_id(1) == 0)
    def _():
        an_ref[...] = _rms(ckv_ref[...], g_ref[...]).astype(BF16)

    an = an_ref[...]
    k_rot = _rope_pair(kpe_ref[...], cos_ref, sin_ref)
    hw = MLA_NOPE + MLA_V
    for hh in range(k_ref.shape[0]):
        res = _dot(an, w_ref[:, hh * hw:(hh + 1) * hw])
        k_ref[hh] = jnp.concatenate([res[:, :MLA_NOPE], k_rot], axis=1).astype(k_ref.dtype)
        vt_ref[hh, 0] = _with_ones_rows(res[:, MLA_NOPE:].T).astype(vt_ref.dtype)


def mla_kv_up(proj_a, g, w_kv, cos_p, sin_p, seq):
    t = proj_a.shape[0]
    hw = MLA_NOPE + MLA_V
    heads = w_kv.shape[1] // hw
    hb = UP_HEADS_PER_STEP
    tm = _tile(seq, ATTN_TILE)
    per_b = seq // tm
    return pl.pallas_call(
        _kvup_kernel,
        grid=(t // tm, heads // hb),
        in_specs=[pl.BlockSpec((tm, KV_LORA), lambda i, h: (i, Q_LORA // KV_LORA)),
                  pl.BlockSpec((1, KV_LORA), lambda i, h: (0, 0)),
                  pl.BlockSpec((KV_LORA, hb * hw), lambda i, h: (0, h)),
                  pl.BlockSpec((tm, LANE), lambda i, h: (i, (Q_LORA + KV_LORA) // LANE)),
                  pl.BlockSpec((tm, LANE), lambda i, h: (i % per_b, 0)),
                  pl.BlockSpec((tm, LANE), lambda i, h: (i % per_b, 0))],
        out_specs=[pl.BlockSpec((hb, tm, MLA_DK_PAD), lambda i, h: (h, i, 0)),
                   pl.BlockSpec((hb, 1, MLA_V + ONES_ROWS, tm), lambda i, h: (h, i, 0, 0))],
        out_shape=[jax.ShapeDtypeStruct((heads, t, MLA_DK_PAD), BF16),
                   jax.ShapeDtypeStruct((heads, t // tm, MLA_V + ONES_ROWS, tm), BF16)],
        scratch_shapes=[pltpu.VMEM((tm, KV_LORA), BF16)],
        compiler_params=_params("parallel", "arbitrary"),
        name="mla_kv_up",
    )(proj_a, g.reshape(1, KV_LORA), w_kv, proj_a, cos_p, sin_p)


def _softmax_first(s, vt, m_ref, acc_ref):
    m = jnp.max(s, axis=0, keepdims=True)
    p = jnp.exp(s - m).astype(vt.dtype)
    m_ref[...] = m
    acc_ref[...] = _dot(vt, p)


def _softmax_step(s, vt, m_ref, acc_ref):
    m_prev = m_ref[...]
    m_new = jnp.maximum(m_prev, jnp.max(s, axis=0, keepdims=True))
    alpha = jnp.exp(m_prev - m_new)
    p = jnp.exp(s - m_new).astype(vt.dtype)
    acc_ref[...] = alpha * acc_ref[...] + _dot(vt, p)
    m_ref[...] = m_new


def _softmax_finish(acc_ref, o_ref, dv):
    acc = acc_ref[...]
    o_ref[...] = (acc[:dv, :] / acc[dv:dv + 1, :]).T.astype(o_ref.dtype)


def _causal_mask_t(s):
    key = lax.broadcasted_iota(jnp.int32, s.shape, 0)
    qry = lax.broadcasted_iota(jnp.int32, s.shape, 1)
    return jnp.where(key <= qry, s, NEG_INF)


def _mla_attn_kernel(qt_ref, k_ref, vt_ref, o_ref, m_ref, acc_ref, *, tq, dv):
    i = pl.program_id(2)
    qt = qt_ref[0, 0]

    def scores(j):
        off = pl.multiple_of(j * tq, tq)
        return _dot(k_ref[0, pl.ds(off, tq), :], qt)

    _softmax_first(_causal_mask_t(scores(i)), vt_ref[0, i], m_ref, acc_ref)

    def body(j, carry):
        _softmax_step(scores(j), vt_ref[0, j], m_ref, acc_ref)
        return carry

    lax.fori_loop(0, i, body, 0)
    _softmax_finish(acc_ref, o_ref, dv)


def mla_attention(qt, k, vt, batch, seq, out_dtype):
    heads, _, dk, tq = qt.shape
    t = k.shape[1]
    dvx = vt.shape[2]
    dv = dvx - ONES_ROWS
    nq = seq // tq
    return pl.pallas_call(
        functools.partial(_mla_attn_kernel, tq=tq, dv=dv),
        grid=(batch, heads, nq),
        in_specs=[pl.BlockSpec((1, 1, dk, tq), lambda b, h, i: (h, b * nq + i, 0, 0)),
                  pl.BlockSpec((1, seq, dk), lambda b, h, i: (h, b, 0)),
                  pl.BlockSpec((1, nq, dvx, tq), lambda b, h, i: (h, b, 0, 0))],
        out_specs=pl.BlockSpec((tq, dv), lambda b, h, i: (b * nq + i, h)),
        out_shape=jax.ShapeDtypeStruct((t, heads * dv), out_dtype),
        scratch_shapes=[pltpu.VMEM((1, tq), F32), pltpu.VMEM((dvx, tq), F32)],
        compiler_params=_params("parallel", "parallel", "arbitrary"),
        name="mla_attention",
    )(qt, k, vt)


def _conv_kernel(b_ref, c_ref, h_ref, w_ref, o_ref, carry_ref, *, per_b):
    i = pl.program_id(0)

    @pl.when(i % per_b == 0)
    def _():
        carry_ref[...] = jnp.zeros_like(carry_ref)

    tm = o_ref.shape[0]
    w0, w1, w2 = w_ref[0:1, :], w_ref[1:2, :], w_ref[2:3, :]
    u = c_ref[...] * h_ref[...]
    y = w0 * pltpu.roll(u, 2, 0) + w1 * pltpu.roll(u, 1, 0) + w2 * u
    o_ref[...] = (b_ref[...] * y).astype(o_ref.dtype)
    u8 = u[0:SUBLANE, :]
    tail = carry_ref[...]
    r8 = lax.broadcasted_iota(jnp.int32, u8.shape, 0)
    p1 = jnp.where(r8 < 1, pltpu.roll(tail, 1, 0), pltpu.roll(u8, 1, 0))
    p2 = jnp.where(r8 < 2, pltpu.roll(tail, 2, 0), pltpu.roll(u8, 2, 0))
    y8 = w0 * p2 + w1 * p1 + w2 * u8
    o_ref[0:SUBLANE, :] = (b_ref[0:SUBLANE, :] * y8).astype(o_ref.dtype)
    carry_ref[...] = u[tm - SUBLANE:tm, :]


def conv_mixer(bch, conv_w, seq, out_dtype):
    t = bch.shape[0]
    ch = conv_w.shape[1]
    tm = _tile(seq, 256)
    per_b = seq // tm
    return pl.pallas_call(
        functools.partial(_conv_kernel, per_b=per_b),
        grid=(t // tm,),
        in_specs=[pl.BlockSpec((tm, ch), lambda i: (i, 0)),
                  pl.BlockSpec((tm, ch), lambda i: (i, 1)),
                  pl.BlockSpec((tm, ch), lambda i: (i, 2)),
                  pl.BlockSpec((CONV_W, ch), lambda i: (0, 0))],
        out_specs=pl.BlockSpec((tm, ch), lambda i: (i, 0)),
        out_shape=jax.ShapeDtypeStruct((t, ch), out_dtype),
        scratch_shapes=[pltpu.VMEM((SUBLANE, ch), F32)],
        compiler_params=_params("arbitrary"),
        name="conv_mixer",
    )(bch, bch, bch, conv_w)


def _moba_prep_kernel(q_ref, k_ref, v_ref, cos_ref, sin_ref, qo_ref, ko_ref, vo_ref, km_ref):
    heads = ko_ref.shape[0]
    cos, sin = cos_ref[...], sin_ref[...]
    n_sub = km_ref.shape[0]
    means = [[] for _ in range(n_sub)]
    for h in range(heads):
        sl = slice(h * MOBA_HD, (h + 1) * MOBA_HD)
        qh = q_ref[:, sl]
        kh = k_ref[:, sl]
        qr = qh * cos + pltpu.roll(qh, MOBA_HD // 2, 1) * sin
        qo_ref[h, 0] = qr.T.astype(qo_ref.dtype)
        kr = kh * cos + pltpu.roll(kh, MOBA_HD // 2, 1) * sin
        ko_ref[h] = kr.astype(ko_ref.dtype)
        vo_ref[h, 0] = _with_ones_rows(v_ref[:, sl].T).astype(vo_ref.dtype)
        for s in range(n_sub):
            means[s].append(jnp.mean(kr[s * MOBA_BLOCK:(s + 1) * MOBA_BLOCK], axis=0, keepdims=True))
    for s in range(n_sub):
        km_ref[s] = jnp.concatenate(means[s], axis=0)


def moba_prep(qkv, cos_f, sin_s, seq):
    t = qkv.shape[0]
    width = qkv.shape[1] // 3
    heads = width // MOBA_HD
    tm = 2 * MOBA_BLOCK
    assert seq % tm == 0
    per_b = seq // tm
    nt = t // tm
    return pl.pallas_call(
        _moba_prep_kernel,
        grid=(nt,),
        in_specs=[pl.BlockSpec((tm, width), lambda i: (i, 0)),
                  pl.BlockSpec((tm, width), lambda i: (i, 1)),
                  pl.BlockSpec((tm, width), lambda i: (i, 2)),
                  pl.BlockSpec((tm, MOBA_HD), lambda i: (i % per_b, 0)),
                  pl.BlockSpec((tm, MOBA_HD), lambda i: (i % per_b, 0))],
        out_specs=[pl.BlockSpec((heads, 1, MOBA_HD, tm), lambda i: (0, i, 0, 0)),
                   pl.BlockSpec((heads, tm, MOBA_HD), lambda i: (0, i, 0)),
                   pl.BlockSpec((heads, 1, MOBA_HD + ONES_ROWS, tm), lambda i: (0, i, 0, 0)),
                   pl.BlockSpec((2, heads, MOBA_HD), lambda i: (i, 0, 0))],
        out_shape=[jax.ShapeDtypeStruct((heads, nt, MOBA_HD, tm), BF16),
                   jax.ShapeDtypeStruct((heads, t, MOBA_HD), BF16),
                   jax.ShapeDtypeStruct((heads, nt, MOBA_HD + ONES_ROWS, tm), BF16),
                   jax.ShapeDtypeStruct((2 * nt, heads, MOBA_HD), F32)],
        compiler_params=_params("arbitrary"),
        name="moba_prep",
    )(qkv, qkv, qkv, cos_f, sin_s)


def _moba_attn_kernel(qt_ref, k_ref, vt_ref, km_ref, o_ref, m_ref, acc_ref, sel_ref, *, scale, dv):
    i = pl.program_id(2)
    qtf = qt_ref[0, 0].astype(F32)
    n_blk = km_ref.shape[2]
    tq = qtf.shape[1]
    blk_w = MOBA_BLOCK
    g = _dot(km_ref[0, 0], qtf, precision=lax.Precision.HIGHEST)
    blk = lax.broadcasted_iota(jnp.int32, g.shape, 0)
    own = 2 * i + (lax.broadcasted_iota(jnp.int32, g.shape, 1) >= blk_w).astype(jnp.int32)
    past = blk < own
    g = jnp.where(past, g, NEG_INF)
    sel = jnp.zeros(g.shape, F32)
    for _ in range(min(MOBA_TOPK, n_blk)):
        mx = jnp.max(g, axis=0, keepdims=True)
        first = jnp.min(jnp.where(g == mx, blk, n_blk), axis=0, keepdims=True)
        hit = blk == first
        sel = jnp.where(hit, 1.0, sel)
        g = jnp.where(hit, NEG_INF, g)
    sel_ref[...] = jnp.where(past, sel, 0.0)

    qt = (qtf * scale).astype(k_ref.dtype)

    def scores(p):
        off = pl.multiple_of(p * tq, tq)
        return _dot(k_ref[0, pl.ds(off, tq), :], qt)

    def chosen(n):
        return sel_ref[pl.ds(n, 1), :] > 0.5

    s = scores(i)
    key = lax.broadcasted_iota(jnp.int32, (blk_w, tq), 0)
    qry = lax.broadcasted_iota(jnp.int32, (blk_w, tq), 1)
    chosen_lim = jnp.where(sel_ref[pl.ds(2 * i, 1), :] > 0.5, blk_w, -1)
    top_ok = key <= jnp.where(qry < blk_w, qry, chosen_lim)
    bot_ok = key <= qry - blk_w
    s = jnp.concatenate([jnp.where(top_ok, s[:blk_w], NEG_INF),
                         jnp.where(bot_ok, s[blk_w:], NEG_INF)], axis=0)
    _softmax_first(s, vt_ref[0, i], m_ref, acc_ref)

    def body(p, carry):
        s = scores(p)
        s = jnp.concatenate([jnp.where(chosen(2 * p), s[:blk_w], NEG_INF),
                             jnp.where(chosen(2 * p + 1), s[blk_w:], NEG_INF)], axis=0)
        _softmax_step(s, vt_ref[0, p], m_ref, acc_ref)
        return carry

    lax.fori_loop(0, i, body, 0)
    _softmax_finish(acc_ref, o_ref, dv)


def moba_attention(qt, k, vt, k_mean, batch, seq, out_dtype):
    heads, _, hd, tq = qt.shape
    t = k.shape[1]
    dvx = vt.shape[2]
    dv = dvx - ONES_ROWS
    n_blk = seq // MOBA_BLOCK
    nq = seq // tq
    return pl.pallas_call(
        functools.partial(_moba_attn_kernel, scale=hd ** -0.5, dv=dv),
        grid=(batch, heads, nq),
        in_specs=[pl.BlockSpec((1, 1, hd, tq), lambda b, h, i: (h, b * nq + i, 0, 0)),
                  pl.BlockSpec((1, seq, hd), lambda b, h, i: (h, b, 0)),
                  pl.BlockSpec((1, nq, dvx, tq), lambda b, h, i: (h, b, 0, 0)),
                  pl.BlockSpec((1, 1, n_blk, hd), lambda b, h, i: (b, h, 0, 0))],
        out_specs=pl.BlockSpec((tq, dv), lambda b, h, i: (b * nq + i, h)),
        out_shape=jax.ShapeDtypeStruct((t, heads * dv), out_dtype),
        scratch_shapes=[pltpu.VMEM((1, tq), F32), pltpu.VMEM((dvx, tq), F32),
                        pltpu.VMEM((n_blk, tq), F32)],
        compiler_params=_params("parallel", "parallel", "arbitrary"),
        name="moba_attention",
    )(qt, k, vt, k_mean)


def _group_norm_kernel(a_ref, b_ref, c_ref, ga_ref, gb_ref, gc_ref, o_ref):
    wa, wb = a_ref.shape[1], b_ref.shape[1]
    o_ref[:, 0:wa] = _rms(a_ref[...].astype(F32), ga_ref[...]).astype(o_ref.dtype)
    o_ref[:, wa:wa + wb] = _rms(b_ref[...].astype(F32), gb_ref[...]).astype(o_ref.dtype)
    o_ref[:, wa + wb:] = _rms(c_ref[...].astype(F32), gc_ref[...]).astype(o_ref.dtype)


def group_norm(y_a, y_b, y_c, g):
    t = y_a.shape[0]
    wa, wb, wc = y_a.shape[1], y_b.shape[1], y_c.shape[1]
    tm = _tile(t, 256)
    ga, gb, gc = g[:wa].reshape(1, wa), g[wa:wa + wb].reshape(1, wb), g[wa + wb:].reshape(1, wc)
    rows = lambda w: pl.BlockSpec((tm, w), lambda i: (i, 0))
    vec = lambda w: pl.BlockSpec((1, w), lambda i: (0, 0))
    return pl.pallas_call(
        _group_norm_kernel,
        grid=(t // tm,),
        in_specs=[rows(wa), rows(wb), rows(wc), vec(wa), vec(wb), vec(wc)],
        out_specs=rows(wa + wb + wc),
        out_shape=jax.ShapeDtypeStruct((t, wa + wb + wc), BF16),
        compiler_params=_params("arbitrary"),
        name="group_norm",
    )(y_a, y_b, y_c, ga, gb, gc)


def _ffn_up_kernel(bexp_ref, first_ref, nused_ref, x_ref, wg_ref, wu_ref, a_ref, wg_bf, wu_bf):
    b = pl.program_id(1)
    used = b < nused_ref[0]

    @pl.when(used & (first_ref[b] == 1))
    def _():
        wg_bf[...] = wg_ref[0, 0].astype(BF16)
        wu_bf[...] = wu_ref[0, 0].astype(BF16)

    @pl.when(used)
    def _():
        x = x_ref[...]
        gate = _dot(x, wg_bf[...])
        up = _dot(x, wu_bf[...])
        a_ref[...] = (gate * jax.nn.sigmoid(gate) * up).astype(a_ref.dtype)

    @pl.when(jnp.logical_not(used))
    def _():
        a_ref[...] = jnp.zeros_like(a_ref)


def _ffn_down_kernel(bexp_ref, first_ref, nused_ref, a_ref, wd_ref, o_ref, wd_bf):
    b = pl.program_id(1)
    used = b < nused_ref[0]

    @pl.when(used & (first_ref[b] == 1))
    def _():
        wd_bf[...] = wd_ref[0, 0].astype(BF16)

    @pl.when(used)
    def _():
        o_ref[...] = _dot(a_ref[...], wd_bf[...]).astype(o_ref.dtype)

    @pl.when(jnp.logical_not(used))
    def _():
        o_ref[...] = jnp.zeros_like(o_ref)


def expert_ffn(xs, block_exp, block_first, n_used, w_gate, w_up, w_down, layer):
    n_slots, d = xs.shape
    ff = w_gate.shape[3]
    fh, dh = ff // FF_SPLIT, d // FF_SPLIT
    n_blocks = n_slots // MOE_BLOCK
    up_spec = pltpu.PrefetchScalarGridSpec(
        num_scalar_prefetch=3,
        grid=(FF_SPLIT, n_blocks),
        in_specs=[pl.BlockSpec((MOE_BLOCK, d), lambda j, b, be, fi, nu: (b, 0)),
                  pl.BlockSpec((1, 1, d, fh), lambda j, b, be, fi, nu: (layer, be[b], 0, j)),
                  pl.BlockSpec((1, 1, d, fh), lambda j, b, be, fi, nu: (layer, be[b], 0, j))],
        out_specs=pl.BlockSpec((MOE_BLOCK, fh), lambda j, b, be, fi, nu: (b, j)),
        scratch_shapes=[pltpu.VMEM((d, fh), BF16), pltpu.VMEM((d, fh), BF16)],
    )
    act = pl.pallas_call(
        _ffn_up_kernel,
        grid_spec=up_spec,
        out_shape=jax.ShapeDtypeStruct((n_slots, ff), BF16),
        compiler_params=_params("arbitrary", "arbitrary"),
        name="expert_ffn_up",
    )(block_exp, block_first, n_used, xs, w_gate, w_up)
    down_spec = pltpu.PrefetchScalarGridSpec(
        num_scalar_prefetch=3,
        grid=(FF_SPLIT, n_blocks),
        in_specs=[pl.BlockSpec((MOE_BLOCK, ff), lambda j, b, be, fi, nu: (b, 0)),
                  pl.BlockSpec((1, 1, ff, dh), lambda j, b, be, fi, nu: (layer, be[b], 0, j))],
        out_specs=pl.BlockSpec((MOE_BLOCK, dh), lambda j, b, be, fi, nu: (b, j)),
        scratch_shapes=[pltpu.VMEM((ff, dh), BF16)],
    )
    return pl.pallas_call(
        _ffn_down_kernel,
        grid_spec=down_spec,
        out_shape=jax.ShapeDtypeStruct((n_slots, d), BF16),
        compiler_params=_params("arbitrary", "arbitrary"),
        name="expert_ffn_down",
    )(block_exp, block_first, n_used, act, w_down)


def _combine_kernel(x_ref, gate_ref, y_ref, w_ref, o_ref):
    w = w_ref[...]
    acc = w[:, 0:1] * y_ref[0].astype(F32)
    for kk in range(1, y_ref.shape[0]):
        acc = acc + w[:, kk:kk + 1] * y_ref[kk].astype(F32)
    o_ref[...] = x_ref[...] + gate_ref[0] * acc


def moe_combine(x2, gate, yg, w_tok, seq):
    t, d = x2.shape
    nk = yg.shape[0]
    tm = _tile(seq, 256)
    per_b = seq // tm
    return pl.pallas_call(
        _combine_kernel,
        grid=(t // tm,),
        in_specs=[pl.BlockSpec((tm, d), lambda i: (i, 0)),
                  pl.BlockSpec((1, 1, d), lambda i: (i // per_b, 0, 0)),
                  pl.BlockSpec((nk, tm, d), lambda i: (0, i, 0)),
                  pl.BlockSpec((tm, nk), lambda i: (i, 0))],
        out_specs=pl.BlockSpec((tm, d), lambda i: (i, 0)),
        out_shape=jax.ShapeDtypeStruct((t, d), F32),
        compiler_params=_params("arbitrary"),
        name="moe_combine",
    )(x2, gate, yg, w_tok)


def _group_by_expert(e_idx):
    n_tok = e_idx.shape[1]
    n_asg = n_tok * TOP_K
    flat_e = e_idx.reshape(n_asg)
    onehot = (flat_e[:, None] == jnp.arange(N_EXPERTS)[None, :]).astype(jnp.int32)
    running = jnp.cumsum(onehot, axis=0)
    rank = jnp.sum(running * onehot, axis=1) - 1
    sizes = running[-1]
    padded = (sizes + MOE_BLOCK - 1) // MOE_BLOCK * MOE_BLOCK
    pad_end = jnp.cumsum(padded)
    pad_start = pad_end - padded
    dest = (jnp.sum(pad_start[None, :] * onehot, axis=1) + rank).astype(jnp.int32)
    n_blocks = -(-n_asg // MOE_BLOCK) + N_EXPERTS
    n_slots = n_blocks * MOE_BLOCK
    slot_tok = jnp.zeros((n_slots,), jnp.int32).at[dest].set(
        jnp.arange(n_asg, dtype=jnp.int32) % n_tok, mode="promise_in_bounds")
    block_start = jnp.arange(n_blocks, dtype=jnp.int32) * MOE_BLOCK
    block_exp = jnp.minimum(jnp.sum(block_start[:, None] >= pad_end[None, :], axis=1),
                            N_EXPERTS - 1).astype(jnp.int32)
    n_used = (pad_end[-1] // MOE_BLOCK).astype(jnp.int32)
    block_exp = jnp.where(jnp.arange(n_blocks) < n_used, block_exp, block_exp[n_used - 1])
    prev_exp = jnp.concatenate([jnp.full((1,), -1, jnp.int32), block_exp[:-1]])
    block_first = (block_exp != prev_exp).astype(jnp.int32)
    return dest.reshape(TOP_K, n_tok), slot_tok, block_exp, block_first, n_used.reshape(1)


def _rot_cols(w):
    half = w.shape[-1] // 2
    return jnp.concatenate([-w[..., half:], w[..., :half]], axis=-1)


def _prep_layer(w_in, w_uq, w_ukv, w_out):
    a_end = Q_LORA + KV_LORA + MLA_ROPE
    b_end = a_end + 3 * CONV_CH
    w_kpe = w_in[:, Q_LORA + KV_LORA:a_end]
    w_a = jnp.concatenate([w_in[:, :a_end], _rot_cols(w_kpe)], axis=1).astype(BF16)
    w_b = w_in[:, a_end:b_end].astype(BF16)
    w_c = w_in[:, b_end:].astype(BF16)
    wq = w_uq.reshape(Q_LORA, MLA_HEADS, MLA_NOPE + MLA_ROPE)
    wq_rope = wq[..., MLA_NOPE:]
    wq = jnp.concatenate([wq, _rot_cols(wq_rope)], axis=-1).transpose(1, 0, 2).astype(BF16)
    return dict(w_a=w_a, w_b=w_b, w_c=w_c, w_q=wq, w_kv=w_ukv.astype(BF16),
                w_out=w_out.astype(BF16))


def _rope_tables(seq, dim):
    inv = 1.0 / (ROPE_THETA ** (jnp.arange(0, dim, 2, dtype=F32) / dim))
    ang = jnp.arange(seq, dtype=F32)[:, None] * inv[None, :]
    return jnp.cos(ang), jnp.sin(ang)


def _take_rows(a, idx):
    return a.at[idx].get(mode="promise_in_bounds")


def kernel(x, c, w_mod, mod_table, mix_norm_g, w_in, q_norm_g, kv_norm_g, w_uq, w_ukv, conv_w,
           group_norm_g, w_out, ffn_norm_g, w_router, router_bias, w_gate, w_up, w_down,
           final_norm_g):
    batch, seq, d = x.shape
    depth = w_in.shape[0]
    t = batch * seq
    x2 = x.reshape(t, d)

    cos_a, sin_a = _rope_tables(seq, MLA_ROPE)
    zeros_a = jnp.zeros_like(cos_a)
    cos_p = jnp.concatenate([cos_a, cos_a, zeros_a, zeros_a], axis=1)
    sin_p = jnp.concatenate([sin_a, sin_a, zeros_a, zeros_a], axis=1)
    cos_b, sin_b = _rope_tables(seq, MOBA_HD)
    cos_f = jnp.concatenate([cos_b, cos_b], axis=1)
    sin_s = jnp.concatenate([-sin_b, sin_b], axis=1)

    c_pad = jnp.zeros((SUBLANE, d), F32).at[:batch].set(c)
    mod_shared = mod_matmul(c_pad, w_mod)[:batch].reshape(batch, N_MOD, d)

    wr_t = w_router.T
    wr_hi = wr_t.astype(BF16)
    wr_lo = (wr_t - wr_hi.astype(F32)).astype(BF16)
    router = (wr_hi, wr_lo, router_bias.astype(F32).reshape(N_EXPERTS, 1))
    mla_scale = (MLA_NOPE + MLA_ROPE) ** -0.5

    for l in range(depth):
        p = _prep_layer(w_in[l], w_uq[l], w_ukv[l], w_out[l])
        mod = mod_shared + mod_table[l][None]
        sh1, sc1, g1, sh2, sc2, g2 = [mod[:, i][:, None, :] for i in range(N_MOD)]

        h = norm_mod(x2, mix_norm_g[l], 1.0 + sc1, sh1, seq)
        proj_a = matmul(h, p["w_a"], F32, tm=512)
        bch = matmul(h, p["w_b"], F32)
        qkv = matmul(h, p["w_c"], F32)

        q_a = mla_q_up(proj_a, q_norm_g[l], p["w_q"], cos_p, sin_p, seq, mla_scale)
        k_a, v_a = mla_kv_up(proj_a, kv_norm_g[l], p["w_kv"], cos_p, sin_p, seq)
        y_a = mla_attention(q_a, k_a, v_a, batch, seq, BF16)

        y_b = conv_mixer(bch, conv_w[l], seq, F32)

        q_c, k_c, v_c, k_mean = moba_prep(qkv, cos_f, sin_s, seq)
        n_blk = seq // MOBA_BLOCK
        k_mean = k_mean.reshape(batch, n_blk, MOBA_HEADS, MOBA_HD).transpose(0, 2, 1, 3)
        y_c = moba_attention(q_c, k_c, v_c, k_mean, batch, seq, BF16)

        y = group_norm(y_a, y_b, y_c, group_norm_g[l])
        x2 = matmul_residual(y, p["w_out"], x2, g1, seq)

        h2, e_idx, gates = norm_mod(x2, ffn_norm_g[l], 1.0 + sc2, sh2, seq, router=router)
        pos, slot_tok, block_exp, block_first, n_used = _group_by_expert(e_idx)
        xs = _take_rows(h2, slot_tok)
        ys = expert_ffn(xs, block_exp, block_first, n_used, w_gate, w_up, w_down, l)
        yg = _take_rows(ys, pos.reshape(-1))
        x2 = moe_combine(x2, g2, yg.reshape(TOP_K, t, d), gates.T, seq)

    out = norm_mod(x2, final_norm_g, None, None, seq, out_dtype=F32)
    return out.reshape(batch, seq, d)
```

```python
import functools

import jax
import jax.numpy as jnp
from jax import lax
from jax.experimental import pallas as pl
from jax.experimental.pallas import tpu as pltpu

MLA_HEADS = 16
MLA_NOPE = 128
MLA_ROPE = 64
MLA_V = 128
Q_LORA = 1024
KV_LORA = 512
MLA_DK_PAD = 256
CONV_CH = 1024
CONV_W = 3
MOBA_HEADS = 8
MOBA_HD = 128
MOBA_BLOCK = 256
MOBA_TOPK = 3
ROPE_THETA = 10000.0
EPS = 1e-6
N_MOD = 6
N_EXPERTS = 16
N_GROUPS = 4
TOP_K = 2
MOE_BLOCK = 256
FF_SPLIT = 2
UP_HEADS_PER_STEP = 4
ATTN_HEADS_PER_STEP = 4
ATTN_TILE = 512
ONES_ROWS = 16

V7X_VMEM_LIMIT_BYTES = 56 * 1024 * 1024
LANE = 128
SUBLANE = 8

F32 = jnp.float32
BF16 = jnp.bfloat16
NEG_INF = float("-inf")


def _params(*sem):
    return pltpu.CompilerParams(dimension_semantics=sem, vmem_limit_bytes=V7X_VMEM_LIMIT_BYTES)


def _dot(a, b, precision=None):
    return jnp.dot(a, b, preferred_element_type=F32, precision=precision)


def _dot_nt(a, b):
    return lax.dot_general(a, b, (((1,), (1,)), ((), ())), preferred_element_type=F32)


def _tile(dim, want):
    return want if dim % want == 0 else dim


def _mod_kernel(c_ref, w0_ref, w1_ref, o_ref):
    @pl.when(pl.program_id(0) == 0)
    def _():
        o_ref[...] = jnp.zeros_like(o_ref)

    c = c_ref[...]
    a = (c * jax.nn.sigmoid(c)).astype(BF16)
    half = w0_ref.shape[1]
    o_ref[:, :half] += _dot(a, w0_ref[...].astype(BF16))
    o_ref[:, half:] += _dot(a, w1_ref[...].astype(BF16))


def mod_matmul(c_pad, w_mod):
    rows, d = c_pad.shape
    n = w_mod.shape[1]
    tk = _tile(d, LANE)
    return pl.pallas_call(
        _mod_kernel,
        grid=(d // tk,),
        in_specs=[pl.BlockSpec((rows, tk), lambda k: (0, k)),
                  pl.BlockSpec((tk, n // 2), lambda k: (k, 0)),
                  pl.BlockSpec((tk, n // 2), lambda k: (k, 1))],
        out_specs=pl.BlockSpec((rows, n), lambda k: (0, 0)),
        out_shape=jax.ShapeDtypeStruct((rows, n), F32),
        compiler_params=_params("arbitrary"),
        name="mod_matmul",
    )(c_pad, w_mod, w_mod)


def _rms(x, g):
    return x * lax.rsqrt(jnp.mean(x * x, axis=-1, keepdims=True) + EPS) * g


def _norm_mod_kernel(x_ref, g_ref, sc_ref, sh_ref, o_ref):
    y = _rms(x_ref[...], g_ref[...])
    o_ref[...] = (y * sc_ref[0] + sh_ref[0]).astype(o_ref.dtype)


def _norm_kernel(x_ref, g_ref, o_ref):
    o_ref[...] = _rms(x_ref[...], g_ref[...]).astype(o_ref.dtype)


def _top2_sum(a, b, c, d):
    hi1, lo1 = jnp.maximum(a, b), jnp.minimum(a, b)
    hi2, lo2 = jnp.maximum(c, d), jnp.minimum(c, d)
    return jnp.maximum(hi1, hi2) + jnp.maximum(jnp.minimum(hi1, hi2), jnp.maximum(lo1, lo2))


def _norm_router_kernel(x_ref, g_ref, sc_ref, sh_ref, whi_ref, wlo_ref, bias_ref,
                        o_ref, e_ref, gate_ref):
    h = _rms(x_ref[...], g_ref[...]) * sc_ref[0] + sh_ref[0]
    hi = h.astype(BF16)
    o_ref[...] = hi
    lo = (h - hi.astype(F32)).astype(BF16)
    whi, wlo = whi_ref[...], wlo_ref[...]
    logits = _dot_nt(whi, hi) + _dot_nt(wlo, hi) + _dot_nt(whi, lo)
    scores = jax.nn.sigmoid(logits)
    biased = scores + bias_ref[...]
    per_grp = N_EXPERTS // N_GROUPS
    b_rows = [biased[e:e + 1, :] for e in range(N_EXPERTS)]
    s_rows = [scores[e:e + 1, :] for e in range(N_EXPERTS)]
    grp_scores = [_top2_sum(*b_rows[g * per_grp:(g + 1) * per_grp]) for g in range(N_GROUPS)]
    best, grp = grp_scores[0], jnp.zeros(grp_scores[0].shape, jnp.int32)
    for g in range(1, N_GROUPS):
        better = grp_scores[g] > best
        grp = jnp.where(better, g, grp)
        best = jnp.where(better, grp_scores[g], best)
    b4, s4 = b_rows[:per_grp], s_rows[:per_grp]
    for g in range(1, N_GROUPS):
        in_g = grp == g
        b4 = [jnp.where(in_g, b_rows[g * per_grp + j], b4[j]) for j in range(per_grp)]
        s4 = [jnp.where(in_g, s_rows[g * per_grp + j], s4[j]) for j in range(per_grp)]
    v1, i1 = b4[0], jnp.zeros(grp.shape, jnp.int32)
    for j in range(1, per_grp):
        better = b4[j] > v1
        i1 = jnp.where(better, j, i1)
        v1 = jnp.where(better, b4[j], v1)
    v2, i2 = jnp.full(v1.shape, NEG_INF, F32), jnp.zeros(grp.shape, jnp.int32)
    for j in range(per_grp):
        better = (i1 != j) & (b4[j] > v2)
        i2 = jnp.where(better, j, i2)
        v2 = jnp.where(better, b4[j], v2)
    g1, g2 = s4[0], s4[0]
    for j in range(1, per_grp):
        g1 = jnp.where(i1 == j, s4[j], g1)
        g2 = jnp.where(i2 == j, s4[j], g2)
    total = g1 + g2
    e_ref[...] = jnp.concatenate([grp * per_grp + i1, grp * per_grp + i2], axis=0)
    gate_ref[...] = jnp.concatenate([g1 / total, g2 / total], axis=0)


def norm_mod(x2, g, sc1p, sh, seq, out_dtype=BF16, router=None):
    t, d = x2.shape
    tm = _tile(seq, 256)
    per_b = seq // tm
    row = pl.BlockSpec((tm, d), lambda i: (i, 0))
    vec = pl.BlockSpec((1, d), lambda i: (0, 0))
    mod = pl.BlockSpec((1, 1, d), lambda i: (i // per_b, 0, 0))
    g2 = g.reshape(1, d)
    if sc1p is None:
        return pl.pallas_call(
            _norm_kernel, grid=(t // tm,), in_specs=[row, vec], out_specs=row,
            out_shape=jax.ShapeDtypeStruct((t, d), out_dtype),
            compiler_params=_params("arbitrary"), name="final_norm")(x2, g2)
    if router is None:
        return pl.pallas_call(
            _norm_mod_kernel, grid=(t // tm,), in_specs=[row, vec, mod, mod], out_specs=row,
            out_shape=jax.ShapeDtypeStruct((t, d), out_dtype),
            compiler_params=_params("arbitrary"), name="norm_mod")(x2, g2, sc1p, sh)
    whi_t, wlo_t, bias = router
    ne = whi_t.shape[0]
    wspec = pl.BlockSpec((ne, d), lambda i: (0, 0))
    kspec = pl.BlockSpec((TOP_K, tm), lambda i: (0, i))
    return pl.pallas_call(
        _norm_router_kernel, grid=(t // tm,),
        in_specs=[row, vec, mod, mod, wspec, wspec, pl.BlockSpec((ne, 1), lambda i: (0, 0))],
        out_specs=[row, kspec, kspec],
        out_shape=[jax.ShapeDtypeStruct((t, d), out_dtype),
                   jax.ShapeDtypeStruct((TOP_K, t), jnp.int32),
                   jax.ShapeDtypeStruct((TOP_K, t), F32)],
        compiler_params=_params("arbitrary"), name="norm_router")(x2, g2, sc1p, sh, whi_t, wlo_t, bias)


def _mm_kernel(a_ref, w_ref, o_ref):
    o_ref[...] = _dot(a_ref[...], w_ref[...]).astype(o_ref.dtype)


def matmul(a, w, out_dtype, tm=1024, tn=1024):
    m, kd = a.shape
    n = w.shape[1]
    tm, tn = _tile(m, tm), _tile(n, tn)
    return pl.pallas_call(
        _mm_kernel,
        grid=(m // tm, n // tn),
        in_specs=[pl.BlockSpec((tm, kd), lambda i, j: (i, 0)),
                  pl.BlockSpec((kd, tn), lambda i, j: (0, j))],
        out_specs=pl.BlockSpec((tm, tn), lambda i, j: (i, j)),
        out_shape=jax.ShapeDtypeStruct((m, n), out_dtype),
        compiler_params=_params("parallel", "parallel"),
        name="matmul",
    )(a, w)


def _mm_res_kernel(a_ref, w_ref, x_ref, gate_ref, o_ref):
    o_ref[...] = x_ref[...] + gate_ref[0] * _dot(a_ref[...], w_ref[...])


def matmul_residual(a, w, x2, gate, seq, tm=1024, tn=1024):
    m, kd = a.shape
    n = w.shape[1]
    tm, tn = _tile(seq, tm), _tile(n, tn)
    per_b = seq // tm
    return pl.pallas_call(
        _mm_res_kernel,
        grid=(m // tm, n // tn),
        in_specs=[pl.BlockSpec((tm, kd), lambda i, j: (i, 0)),
                  pl.BlockSpec((kd, tn), lambda i, j: (0, j)),
                  pl.BlockSpec((tm, tn), lambda i, j: (i, j)),
                  pl.BlockSpec((1, 1, tn), lambda i, j: (i // per_b, 0, j))],
        out_specs=pl.BlockSpec((tm, tn), lambda i, j: (i, j)),
        out_shape=jax.ShapeDtypeStruct((m, n), F32),
        compiler_params=_params("parallel", "parallel"),
        name="matmul_residual",
    )(a, w, x2, gate)


def _rope_pair(r2, cos_ref, sin_ref):
    return r2 * cos_ref[...] + pltpu.roll(r2, MLA_ROPE, 1) * sin_ref[...]


def _with_ones_rows(vt):
    return jnp.concatenate([vt, jnp.ones((ONES_ROWS, vt.shape[1]), vt.dtype)], axis=0)


def _qup_kernel(cq_ref, g_ref, w_ref, cos_ref, sin_ref, o_ref, an_ref, *, scale):
    @pl.when(pl.program_id(1) == 0)
    def _():
        an_ref[...] = _rms(cq_ref[...], g_ref[...]).astype(BF16)

    an = an_ref[...]
    for hh in range(w_ref.shape[0]):
        res = _dot(an, w_ref[hh])
        roped = _rope_pair(res[:, MLA_NOPE:], cos_ref, sin_ref)
        q = jnp.concatenate([res[:, :MLA_NOPE], roped], axis=1) * scale
        o_ref[hh, 0] = q.T.astype(o_ref.dtype)


def mla_q_up(proj_a, g, w_q, cos_p, sin_p, seq, scale):
    t = proj_a.shape[0]
    heads = w_q.shape[0]
    hb = UP_HEADS_PER_STEP
    tm = _tile(seq, ATTN_TILE)
    per_b = seq // tm
    return pl.pallas_call(
        functools.partial(_qup_kernel, scale=scale),
        grid=(t // tm, heads // hb),
        in_specs=[pl.BlockSpec((tm, Q_LORA), lambda i, h: (i, 0)),
                  pl.BlockSpec((1, Q_LORA), lambda i, h: (0, 0)),
                  pl.BlockSpec((hb, Q_LORA, MLA_DK_PAD), lambda i, h: (h, 0, 0)),
                  pl.BlockSpec((tm, LANE), lambda i, h: (i % per_b, 0)),
                  pl.BlockSpec((tm, LANE), lambda i, h: (i % per_b, 0))],
        out_specs=pl.BlockSpec((hb, 1, MLA_DK_PAD, tm), lambda i, h: (h, i, 0, 0)),
        out_shape=jax.ShapeDtypeStruct((heads, t // tm, MLA_DK_PAD, tm), BF16),
        scratch_shapes=[pltpu.VMEM((tm, Q_LORA), BF16)],
        compiler_params=_params("parallel", "arbitrary"),
        name="mla_q_up",
    )(proj_a, g.reshape(1, Q_LORA), w_q, cos_p, sin_p)


def _kvup_kernel(ckv_ref, g_ref, w_ref, kpe_ref, cos_ref, sin_ref, k_ref, vt_ref, an_ref):
    @pl.when(pl.program_id(1) == 0)
    def _():
        an_ref[...] = _rms(ckv_ref[...], g_ref[...]).astype(BF16)

    an = an_ref[...]
    k_rot = _rope_pair(kpe_ref[...], cos_ref, sin_ref)
    hw = MLA_NOPE + MLA_V
    for hh in range(k_ref.shape[0]):
        res = _dot(an, w_ref[:, hh * hw:(hh + 1) * hw])
        k_ref[hh] = jnp.concatenate([res[:, :MLA_NOPE], k_rot], axis=1).astype(k_ref.dtype)
        vt_ref[hh, 0] = _with_ones_rows(res[:, MLA_NOPE:].T).astype(vt_ref.dtype)


def mla_kv_up(proj_a, g, w_kv, cos_p, sin_p, seq):
    t = proj_a.shape[0]
    hw = MLA_NOPE + MLA_V
    heads = w_kv.shape[1] // hw
    hb = UP_HEADS_PER_STEP
    tm = _tile(seq, ATTN_TILE)
    per_b = seq // tm
    return pl.pallas_call(
        _kvup_kernel,
        grid=(t // tm, heads // hb),
        in_specs=[pl.BlockSpec((tm, KV_LORA), lambda i, h: (i, Q_LORA // KV_LORA)),
                  pl.BlockSpec((1, KV_LORA), lambda i, h: (0, 0)),
                  pl.BlockSpec((KV_LORA, hb * hw), lambda i, h: (0, h)),
                  pl.BlockSpec((tm, LANE), lambda i, h: (i, (Q_LORA + KV_LORA) // LANE)),
                  pl.BlockSpec((tm, LANE), lambda i, h: (i % per_b, 0)),
                  pl.BlockSpec((tm, LANE), lambda i, h: (i % per_b, 0))],
        out_specs=[pl.BlockSpec((hb, tm, MLA_DK_PAD), lambda i, h: (h, i, 0)),
                   pl.BlockSpec((hb, 1, MLA_V + ONES_ROWS, tm), lambda i, h: (h, i, 0, 0))],
        out_shape=[jax.ShapeDtypeStruct((heads, t, MLA_DK_PAD), BF16),
                   jax.ShapeDtypeStruct((heads, t // tm, MLA_V + ONES_ROWS, tm), BF16)],
        scratch_shapes=[pltpu.VMEM((tm, KV_LORA), BF16)],
        compiler_params=_params("parallel", "arbitrary"),
        name="mla_kv_up",
    )(proj_a, g.reshape(1, KV_LORA), w_kv, proj_a, cos_p, sin_p)


def _softmax_first(s, vt, m_ref, acc_ref):
    m = jnp.max(s, axis=0, keepdims=True)
    p = jnp.exp(s - m).astype(vt.dtype)
    m_ref[...] = m
    acc_ref[...] = _dot(vt, p)


def _softmax_step(s, vt, m_ref, acc_ref):
    m_prev = m_ref[...]
    m_new = jnp.maximum(m_prev, jnp.max(s, axis=0, keepdims=True))
    alpha = jnp.exp(m_prev - m_new)
    p = jnp.exp(s - m_new).astype(vt.dtype)
    acc_ref[...] = alpha * acc_ref[...] + _dot(vt, p)
    m_ref[...] = m_new


def _softmax_finish(acc_ref, dv):
    acc = acc_ref[...]
    return (acc[:dv, :] / acc[dv:dv + 1, :]).T


def _causal_mask_t(s):
    key = lax.broadcasted_iota(jnp.int32, s.shape, 0)
    qry = lax.broadcasted_iota(jnp.int32, s.shape, 1)
    return jnp.where(key <= qry, s, NEG_INF)


def _mla_attn_kernel(qt_ref, k_ref, vt_ref, o_ref, m_ref, acc_ref, *, tq, dv):
    i = pl.program_id(2)
    n_h = qt_ref.shape[0]
    qts = [qt_ref[h, 0] for h in range(n_h)]

    def scores(h, j):
        off = pl.multiple_of(j * tq, tq)
        return _dot(k_ref[h, pl.ds(off, tq), :], qts[h])

    for h in range(n_h):
        _softmax_first(_causal_mask_t(scores(h, i)), vt_ref[h, i], m_ref.at[h], acc_ref.at[h])

    def body(j, carry):
        for h in range(n_h):
            _softmax_step(scores(h, j), vt_ref[h, j], m_ref.at[h], acc_ref.at[h])
        return carry

    lax.fori_loop(0, i, body, 0)
    for h in range(n_h):
        o_ref[:, h * dv:(h + 1) * dv] = _softmax_finish(acc_ref.at[h], dv).astype(o_ref.dtype)


def mla_attention(qt, k, vt, batch, seq, out_dtype):
    heads, _, dk, tq = qt.shape
    t = k.shape[1]
    dvx = vt.shape[2]
    dv = dvx - ONES_ROWS
    nq = seq // tq
    hb = ATTN_HEADS_PER_STEP
    return pl.pallas_call(
        functools.partial(_mla_attn_kernel, tq=tq, dv=dv),
        grid=(batch, heads // hb, nq),
        in_specs=[pl.BlockSpec((hb, 1, dk, tq), lambda b, h, i: (h, b * nq + i, 0, 0)),
                  pl.BlockSpec((hb, seq, dk), lambda b, h, i: (h, b, 0)),
                  pl.BlockSpec((hb, nq, dvx, tq), lambda b, h, i: (h, b, 0, 0))],
        out_specs=pl.BlockSpec((tq, hb * dv), lambda b, h, i: (b * nq + i, h)),
        out_shape=jax.ShapeDtypeStruct((t, heads * dv), out_dtype),
        scratch_shapes=[pltpu.VMEM((hb, 1, tq), F32), pltpu.VMEM((hb, dvx, tq), F32)],
        compiler_params=_params("parallel", "parallel", "arbitrary"),
        name="mla_attention",
    )(qt, k, vt)


def _conv_kernel(b_ref, c_ref, h_ref, w_ref, o_ref, carry_ref, *, per_b):
    i = pl.program_id(0)

    @pl.when(i % per_b == 0)
    def _():
        carry_ref[...] = jnp.zeros_like(carry_ref)

    tm = o_ref.shape[0]
    w0, w1, w2 = w_ref[0:1, :], w_ref[1:2, :], w_ref[2:3, :]
    u = c_ref[...] * h_ref[...]
    y = w0 * pltpu.roll(u, 2, 0) + w1 * pltpu.roll(u, 1, 0) + w2 * u
    o_ref[...] = (b_ref[...] * y).astype(o_ref.dtype)
    u8 = u[0:SUBLANE, :]
    tail = carry_ref[...]
    r8 = lax.broadcasted_iota(jnp.int32, u8.shape, 0)
    p1 = jnp.where(r8 < 1, pltpu.roll(tail, 1, 0), pltpu.roll(u8, 1, 0))
    p2 = jnp.where(r8 < 2, pltpu.roll(tail, 2, 0), pltpu.roll(u8, 2, 0))
    y8 = w0 * p2 + w1 * p1 + w2 * u8
    o_ref[0:SUBLANE, :] = (b_ref[0:SUBLANE, :] * y8).astype(o_ref.dtype)
    carry_ref[...] = u[tm - SUBLANE:tm, :]


def conv_mixer(bch, conv_w, seq, out_dtype):
    t = bch.shape[0]
    ch = conv_w.shape[1]
    tm = _tile(seq, 256)
    per_b = seq // tm
    return pl.pallas_call(
        functools.partial(_conv_kernel, per_b=per_b),
        grid=(t // tm,),
        in_specs=[pl.BlockSpec((tm, ch), lambda i: (i, 0)),
                  pl.BlockSpec((tm, ch), lambda i: (i, 1)),
                  pl.BlockSpec((tm, ch), lambda i: (i, 2)),
                  pl.BlockSpec((CONV_W, ch), lambda i: (0, 0))],
        out_specs=pl.BlockSpec((tm, ch), lambda i: (i, 0)),
        out_shape=jax.ShapeDtypeStruct((t, ch), out_dtype),
        scratch_shapes=[pltpu.VMEM((SUBLANE, ch), F32)],
        compiler_params=_params("arbitrary"),
        name="conv_mixer",
    )(bch, bch, bch, conv_w)


def _moba_prep_kernel(q_ref, k_ref, v_ref, cos_ref, sin_ref, qo_ref, ko_ref, vo_ref, km_ref):
    heads = ko_ref.shape[0]
    cos, sin = cos_ref[...], sin_ref[...]
    n_sub = km_ref.shape[0]
    means = [[] for _ in range(n_sub)]
    for h in range(heads):
        sl = slice(h * MOBA_HD, (h + 1) * MOBA_HD)
        qh = q_ref[:, sl]
        kh = k_ref[:, sl]
        qr = qh * cos + pltpu.roll(qh, MOBA_HD // 2, 1) * sin
        qo_ref[h, 0] = qr.T.astype(qo_ref.dtype)
        kr = kh * cos + pltpu.roll(kh, MOBA_HD // 2, 1) * sin
        ko_ref[h] = kr.astype(ko_ref.dtype)
        vo_ref[h, 0] = _with_ones_rows(v_ref[:, sl].T).astype(vo_ref.dtype)
        for s in range(n_sub):
            means[s].append(jnp.mean(kr[s * MOBA_BLOCK:(s + 1) * MOBA_BLOCK], axis=0, keepdims=True))
    for s in range(n_sub):
        km_ref[s] = jnp.concatenate(means[s], axis=0)


def moba_prep(qkv, cos_f, sin_s, seq):
    t = qkv.shape[0]
    width = qkv.shape[1] // 3
    heads = width // MOBA_HD
    tm = 2 * MOBA_BLOCK
    assert seq % tm == 0
    per_b = seq // tm
    nt = t // tm
    return pl.pallas_call(
        _moba_prep_kernel,
        grid=(nt,),
        in_specs=[pl.BlockSpec((tm, width), lambda i: (i, 0)),
                  pl.BlockSpec((tm, width), lambda i: (i, 1)),
                  pl.BlockSpec((tm, width), lambda i: (i, 2)),
                  pl.BlockSpec((tm, MOBA_HD), lambda i: (i % per_b, 0)),
                  pl.BlockSpec((tm, MOBA_HD), lambda i: (i % per_b, 0))],
        out_specs=[pl.BlockSpec((heads, 1, MOBA_HD, tm), lambda i: (0, i, 0, 0)),
                   pl.BlockSpec((heads, tm, MOBA_HD), lambda i: (0, i, 0)),
                   pl.BlockSpec((heads, 1, MOBA_HD + ONES_ROWS, tm), lambda i: (0, i, 0, 0)),
                   pl.BlockSpec((2, heads, MOBA_HD), lambda i: (i, 0, 0))],
        out_shape=[jax.ShapeDtypeStruct((heads, nt, MOBA_HD, tm), BF16),
                   jax.ShapeDtypeStruct((heads, t, MOBA_HD), BF16),
                   jax.ShapeDtypeStruct((heads, nt, MOBA_HD + ONES_ROWS, tm), BF16),
                   jax.ShapeDtypeStruct((2 * nt, heads, MOBA_HD), F32)],
        compiler_params=_params("arbitrary"),
        name="moba_prep",
    )(qkv, qkv, qkv, cos_f, sin_s)


def _moba_attn_kernel(qt_ref, k_ref, vt_ref, km_ref, o_ref, m_ref, acc_ref, sel_ref, *, scale, dv):
    i = pl.program_id(2)
    n_h = qt_ref.shape[0]
    n_blk = km_ref.shape[2]
    tq = qt_ref.shape[3]
    blk_w = MOBA_BLOCK
    blk = lax.broadcasted_iota(jnp.int32, (n_blk, tq), 0)
    own = 2 * i + (lax.broadcasted_iota(jnp.int32, (n_blk, tq), 1) >= blk_w).astype(jnp.int32)
    past = blk < own
    qts = []
    for h in range(n_h):
        qtf = qt_ref[h, 0].astype(F32)
        g = _dot(km_ref[0, h], qtf, precision=lax.Precision.HIGHEST)
        g = jnp.where(past, g, NEG_INF)
        sel = jnp.zeros(g.shape, F32)
        for _ in range(min(MOBA_TOPK, n_blk)):
            mx = jnp.max(g, axis=0, keepdims=True)
            first = jnp.min(jnp.where(g == mx, blk, n_blk), axis=0, keepdims=True)
            hit = blk == first
            sel = jnp.where(hit, 1.0, sel)
            g = jnp.where(hit, NEG_INF, g)
        sel_ref[h] = jnp.where(past, sel, 0.0)
        qts.append((qtf * scale).astype(k_ref.dtype))

    def scores(h, p):
        off = pl.multiple_of(p * tq, tq)
        return _dot(k_ref[h, pl.ds(off, tq), :], qts[h])

    def chosen(h, n):
        return sel_ref[h, pl.ds(n, 1), :] > 0.5

    key = lax.broadcasted_iota(jnp.int32, (blk_w, tq), 0)
    qry = lax.broadcasted_iota(jnp.int32, (blk_w, tq), 1)
    bot_ok = key <= qry - blk_w
    for h in range(n_h):
        s = scores(h, i)
        chosen_lim = jnp.where(chosen(h, 2 * i), blk_w, -1)
        top_ok = key <= jnp.where(qry < blk_w, qry, chosen_lim)
        s = jnp.concatenate([jnp.where(top_ok, s[:blk_w], NEG_INF),
                             jnp.where(bot_ok, s[blk_w:], NEG_INF)], axis=0)
        _softmax_first(s, vt_ref[h, i], m_ref.at[h], acc_ref.at[h])

    def body(p, carry):
        for h in range(n_h):
            s = scores(h, p)
            s = jnp.concatenate([jnp.where(chosen(h, 2 * p), s[:blk_w], NEG_INF),
                                 jnp.where(chosen(h, 2 * p + 1), s[blk_w:], NEG_INF)], axis=0)
            _softmax_step(s, vt_ref[h, p], m_ref.at[h], acc_ref.at[h])
        return carry

    lax.fori_loop(0, i, body, 0)
    for h in range(n_h):
        o_ref[:, h * dv:(h + 1) * dv] = _softmax_finish(acc_ref.at[h], dv).astype(o_ref.dtype)


def moba_attention(qt, k, vt, k_mean, batch, seq, out_dtype):
    heads, _, hd, tq = qt.shape
    t = k.shape[1]
    dvx = vt.shape[2]
    dv = dvx - ONES_ROWS
    n_blk = seq // MOBA_BLOCK
    nq = seq // tq
    hb = ATTN_HEADS_PER_STEP
    return pl.pallas_call(
        functools.partial(_moba_attn_kernel, scale=hd ** -0.5, dv=dv),
        grid=(batch, heads // hb, nq),
        in_specs=[pl.BlockSpec((hb, 1, hd, tq), lambda b, h, i: (h, b * nq + i, 0, 0)),
                  pl.BlockSpec((hb, seq, hd), lambda b, h, i: (h, b, 0)),
                  pl.BlockSpec((hb, nq, dvx, tq), lambda b, h, i: (h, b, 0, 0)),
                  pl.BlockSpec((1, hb, n_blk, hd), lambda b, h, i: (b, h, 0, 0))],
        out_specs=pl.BlockSpec((tq, hb * dv), lambda b, h, i: (b * nq + i, h)),
        out_shape=jax.ShapeDtypeStruct((t, heads * dv), out_dtype),
        scratch_shapes=[pltpu.VMEM((hb, 1, tq), F32), pltpu.VMEM((hb, dvx, tq), F32),
                        pltpu.VMEM((hb, n_blk, tq), F32)],
        compiler_params=_params("parallel", "parallel", "arbitrary"),
        name="moba_attention",
    )(qt, k, vt, k_mean)


def _group_norm_kernel(a_ref, b_ref, c_ref, ga_ref, gb_ref, gc_ref, o_ref):
    wa, wb = a_ref.shape[1], b_ref.shape[1]
    o_ref[:, 0:wa] = _rms(a_ref[...].astype(F32), ga_ref[...]).astype(o_ref.dtype)
    o_ref[:, wa:wa + wb] = _rms(b_ref[...].astype(F32), gb_ref[...]).astype(o_ref.dtype)
    o_ref[:, wa + wb:] = _rms(c_ref[...].astype(F32), gc_ref[...]).astype(o_ref.dtype)


def group_norm(y_a, y_b, y_c, g):
    t = y_a.shape[0]
    wa, wb, wc = y_a.shape[1], y_b.shape[1], y_c.shape[1]
    tm = _tile(t, 256)
    ga, gb, gc = g[:wa].reshape(1, wa), g[wa:wa + wb].reshape(1, wb), g[wa + wb:].reshape(1, wc)
    rows = lambda w: pl.BlockSpec((tm, w), lambda i: (i, 0))
    vec = lambda w: pl.BlockSpec((1, w), lambda i: (0, 0))
    return pl.pallas_call(
        _group_norm_kernel,
        grid=(t // tm,),
        in_specs=[rows(wa), rows(wb), rows(wc), vec(wa), vec(wb), vec(wc)],
        out_specs=rows(wa + wb + wc),
        out_shape=jax.ShapeDtypeStruct((t, wa + wb + wc), BF16),
        compiler_params=_params("arbitrary"),
        name="group_norm",
    )(y_a, y_b, y_c, ga, gb, gc)


def _ffn_up_kernel(bexp_ref, first_ref, nused_ref, x_ref, wg_ref, wu_ref, a_ref, wg_bf, wu_bf):
    b = pl.program_id(1)
    used = b < nused_ref[0]

    @pl.when(used & (first_ref[b] == 1))
    def _():
        wg_bf[...] = wg_ref[0, 0].astype(BF16)
        wu_bf[...] = wu_ref[0, 0].astype(BF16)

    @pl.when(used)
    def _():
        x = x_ref[...]
        gate = _dot(x, wg_bf[...])
        up = _dot(x, wu_bf[...])
        a_ref[...] = (gate * jax.nn.sigmoid(gate) * up).astype(a_ref.dtype)

    @pl.when(jnp.logical_not(used))
    def _():
        a_ref[...] = jnp.zeros_like(a_ref)


def _ffn_down_kernel(bexp_ref, first_ref, nused_ref, a_ref, wd_ref, o_ref, wd_bf):
    b = pl.program_id(1)
    used = b < nused_ref[0]

    @pl.when(used & (first_ref[b] == 1))
    def _():
        wd_bf[...] = wd_ref[0, 0].astype(BF16)

    @pl.when(used)
    def _():
        o_ref[...] = _dot(a_ref[...], wd_bf[...]).astype(o_ref.dtype)

    @pl.when(jnp.logical_not(used))
    def _():
        o_ref[...] = jnp.zeros_like(o_ref)


def expert_ffn(xs, block_exp, block_first, n_used, w_gate, w_up, w_down, layer):
    n_slots, d = xs.shape
    ff = w_gate.shape[3]
    fh, dh = ff // FF_SPLIT, d // FF_SPLIT
    n_blocks = n_slots // MOE_BLOCK
    up_spec = pltpu.PrefetchScalarGridSpec(
        num_scalar_prefetch=3,
        grid=(FF_SPLIT, n_blocks),
        in_specs=[pl.BlockSpec((MOE_BLOCK, d), lambda j, b, be, fi, nu: (b, 0)),
                  pl.BlockSpec((1, 1, d, fh), lambda j, b, be, fi, nu: (layer, be[b], 0, j)),
                  pl.BlockSpec((1, 1, d, fh), lambda j, b, be, fi, nu: (layer, be[b], 0, j))],
        out_specs=pl.BlockSpec((MOE_BLOCK, fh), lambda j, b, be, fi, nu: (b, j)),
        scratch_shapes=[pltpu.VMEM((d, fh), BF16), pltpu.VMEM((d, fh), BF16)],
    )
    act = pl.pallas_call(
        _ffn_up_kernel,
        grid_spec=up_spec,
        out_shape=jax.ShapeDtypeStruct((n_slots, ff), BF16),
        compiler_params=_params("arbitrary", "arbitrary"),
        name="expert_ffn_up",
    )(block_exp, block_first, n_used, xs, w_gate, w_up)
    down_spec = pltpu.PrefetchScalarGridSpec(
        num_scalar_prefetch=3,
        grid=(FF_SPLIT, n_blocks),
        in_specs=[pl.BlockSpec((MOE_BLOCK, ff), lambda j, b, be, fi, nu: (b, 0)),
                  pl.BlockSpec((1, 1, ff, dh), lambda j, b, be, fi, nu: (layer, be[b], 0, j))],
        out_specs=pl.BlockSpec((MOE_BLOCK, dh), lambda j, b, be, fi, nu: (b, j)),
        scratch_shapes=[pltpu.VMEM((ff, dh), BF16)],
    )
    return pl.pallas_call(
        _ffn_down_kernel,
        grid_spec=down_spec,
        out_shape=jax.ShapeDtypeStruct((n_slots, d), BF16),
        compiler_params=_params("arbitrary", "arbitrary"),
        name="expert_ffn_down",
    )(block_exp, block_first, n_used, act, w_down)


def _combine_kernel(x_ref, gate_ref, y_ref, w_ref, o_ref):
    w = w_ref[...]
    acc = w[:, 0:1] * y_ref[0].astype(F32)
    for kk in range(1, y_ref.shape[0]):
        acc = acc + w[:, kk:kk + 1] * y_ref[kk].astype(F32)
    o_ref[...] = x_ref[...] + gate_ref[0] * acc


def moe_combine(x2, gate, yg, w_tok, seq):
    t, d = x2.shape
    nk = yg.shape[0]
    tm = _tile(seq, 256)
    per_b = seq // tm
    return pl.pallas_call(
        _combine_kernel,
        grid=(t // tm,),
        in_specs=[pl.BlockSpec((tm, d), lambda i: (i, 0)),
                  pl.BlockSpec((1, 1, d), lambda i: (i // per_b, 0, 0)),
                  pl.BlockSpec((nk, tm, d), lambda i: (0, i, 0)),
                  pl.BlockSpec((tm, nk), lambda i: (i, 0))],
        out_specs=pl.BlockSpec((tm, d), lambda i: (i, 0)),
        out_shape=jax.ShapeDtypeStruct((t, d), F32),
        compiler_params=_params("arbitrary"),
        name="moe_combine",
    )(x2, gate, yg, w_tok)


def _group_by_expert(e_idx):
    n_tok = e_idx.shape[1]
    n_asg = n_tok * TOP_K
    flat_e = e_idx.reshape(n_asg)
    onehot = (flat_e[:, None] == jnp.arange(N_EXPERTS)[None, :]).astype(jnp.int32)
    running = jnp.cumsum(onehot, axis=0)
    rank = jnp.sum(running * onehot, axis=1) - 1
    sizes = running[-1]
    padded = (sizes + MOE_BLOCK - 1) // MOE_BLOCK * MOE_BLOCK
    pad_end = jnp.cumsum(padded)
    pad_start = pad_end - padded
    dest = (jnp.sum(pad_start[None, :] * onehot, axis=1) + rank).astype(jnp.int32)
    n_blocks = -(-n_asg // MOE_BLOCK) + N_EXPERTS
    n_slots = n_blocks * MOE_BLOCK
    slot_tok = jnp.zeros((n_slots,), jnp.int32).at[dest].set(
        jnp.arange(n_asg, dtype=jnp.int32) % n_tok, mode="promise_in_bounds")
    block_start = jnp.arange(n_blocks, dtype=jnp.int32) * MOE_BLOCK
    block_exp = jnp.minimum(jnp.sum(block_start[:, None] >= pad_end[None, :], axis=1),
                            N_EXPERTS - 1).astype(jnp.int32)
    n_used = (pad_end[-1] // MOE_BLOCK).astype(jnp.int32)
    block_exp = jnp.where(jnp.arange(n_blocks) < n_used, block_exp, block_exp[n_used - 1])
    prev_exp = jnp.concatenate([jnp.full((1,), -1, jnp.int32), block_exp[:-1]])
    block_first = (block_exp != prev_exp).astype(jnp.int32)
    return dest.reshape(TOP_K, n_tok), slot_tok, block_exp, block_first, n_used.reshape(1)


def _rot_cols(w):
    half = w.shape[-1] // 2
    return jnp.concatenate([-w[..., half:], w[..., :half]], axis=-1)


def _prep_layer(w_in, w_uq, w_ukv, w_out):
    a_end = Q_LORA + KV_LORA + MLA_ROPE
    b_end = a_end + 3 * CONV_CH
    w_kpe = w_in[:, Q_LORA + KV_LORA:a_end]
    w_a = jnp.concatenate([w_in[:, :a_end], _rot_cols(w_kpe)], axis=1).astype(BF16)
    w_b = w_in[:, a_end:b_end].astype(BF16)
    w_c = w_in[:, b_end:].astype(BF16)
    wq = w_uq.reshape(Q_LORA, MLA_HEADS, MLA_NOPE + MLA_ROPE)
    wq_rope = wq[..., MLA_NOPE:]
    wq = jnp.concatenate([wq, _rot_cols(wq_rope)], axis=-1).transpose(1, 0, 2).astype(BF16)
    return dict(w_a=w_a, w_b=w_b, w_c=w_c, w_q=wq, w_kv=w_ukv.astype(BF16),
                w_out=w_out.astype(BF16))


def _rope_tables(seq, dim):
    inv = 1.0 / (ROPE_THETA ** (jnp.arange(0, dim, 2, dtype=F32) / dim))
    ang = jnp.arange(seq, dtype=F32)[:, None] * inv[None, :]
    return jnp.cos(ang), jnp.sin(ang)


def _take_rows(a, idx):
    return a.at[idx].get(mode="promise_in_bounds")


def kernel(x, c, w_mod, mod_table, mix_norm_g, w_in, q_norm_g, kv_norm_g, w_uq, w_ukv, conv_w,
           group_norm_g, w_out, ffn_norm_g, w_router, router_bias, w_gate, w_up, w_down,
           final_norm_g):
    batch, seq, d = x.shape
    depth = w_in.shape[0]
    t = batch * seq
    x2 = x.reshape(t, d)

    cos_a, sin_a = _rope_tables(seq, MLA_ROPE)
    zeros_a = jnp.zeros_like(cos_a)
    cos_p = jnp.concatenate([cos_a, cos_a, zeros_a, zeros_a], axis=1)
    sin_p = jnp.concatenate([sin_a, sin_a, zeros_a, zeros_a], axis=1)
    cos_b, sin_b = _rope_tables(seq, MOBA_HD)
    cos_f = jnp.concatenate([cos_b, cos_b], axis=1)
    sin_s = jnp.concatenate([-sin_b, sin_b], axis=1)

    c_pad = jnp.zeros((SUBLANE, d), F32).at[:batch].set(c)
    mod_shared = mod_matmul(c_pad, w_mod)[:batch].reshape(batch, N_MOD, d)

    wr_t = w_router.T
    wr_hi = wr_t.astype(BF16)
    wr_lo = (wr_t - wr_hi.astype(F32)).astype(BF16)
    router = (wr_hi, wr_lo, router_bias.astype(F32).reshape(N_EXPERTS, 1))
    mla_scale = (MLA_NOPE + MLA_ROPE) ** -0.5

    for l in range(depth):
        p = _prep_layer(w_in[l], w_uq[l], w_ukv[l], w_out[l])
        mod = mod_shared + mod_table[l][None]
        sh1, sc1, g1, sh2, sc2, g2 = [mod[:, i][:, None, :] for i in range(N_MOD)]

        h = norm_mod(x2, mix_norm_g[l], 1.0 + sc1, sh1, seq)
        proj_a = matmul(h, p["w_a"], F32, tm=512)
        bch = matmul(h, p["w_b"], F32)
        qkv = matmul(h, p["w_c"], F32)

        q_a = mla_q_up(proj_a, q_norm_g[l], p["w_q"], cos_p, sin_p, seq, mla_scale)
        k_a, v_a = mla_kv_up(proj_a, kv_norm_g[l], p["w_kv"], cos_p, sin_p, seq)
        y_a = mla_attention(q_a, k_a, v_a, batch, seq, BF16)

        y_b = conv_mixer(bch, conv_w[l], seq, F32)

        q_c, k_c, v_c, k_mean = moba_prep(qkv, cos_f, sin_s, seq)
        n_blk = seq // MOBA_BLOCK
        k_mean = k_mean.reshape(batch, n_blk, MOBA_HEADS, MOBA_HD).transpose(0, 2, 1, 3)
        y_c = moba_attention(q_c, k_c, v_c, k_mean, batch, seq, BF16)

        y = group_norm(y_a, y_b, y_c, group_norm_g[l])
        x2 = matmul_residual(y, p["w_out"], x2, g1, seq)

        h2, e_idx, gates = norm_mod(x2, ffn_norm_g[l], 1.0 + sc2, sh2, seq, router=router)
        pos, slot_tok, block_exp, block_first, n_used = _group_by_expert(e_idx)
        xs = _take_rows(h2, slot_tok)
        ys = expert_ffn(xs, block_exp, block_first, n_used, w_gate, w_up, w_down, l)
        yg = _take_rows(ys, pos.reshape(-1))
        x2 = moe_combine(x2, g2, yg.reshape(TOP_K, t, d), gates.T, seq)

    out = norm_mod(x2, final_norm_g, None, None, seq, out_dtype=F32)
    return out.reshape(batch, seq, d)
```

```python
import functools

import jax
import jax.numpy as jnp
from jax import lax
from jax.experimental import pallas as pl
from jax.experimental.pallas import tpu as pltpu

MLA_HEADS = 16
MLA_NOPE = 128
MLA_ROPE = 64
MLA_V = 128
Q_LORA = 1024
KV_LORA = 512
MLA_DK_PAD = 256
CONV_CH = 1024
CONV_W = 3
MOBA_HEADS = 8
MOBA_HD = 128
MOBA_BLOCK = 256
MOBA_TOPK = 3
ROPE_THETA = 10000.0
EPS = 1e-6
N_MOD = 6
N_EXPERTS = 16
N_GROUPS = 4
TOP_K = 2
MOE_BLOCK = 512
GATHER_ROWS = 256
FF_SPLIT = 2
UP_HEADS_PER_STEP = 4
ATTN_HEADS_PER_STEP = 4
ATTN_TILE = 512
ONES_ROWS = 16

V7X_VMEM_LIMIT_BYTES = 56 * 1024 * 1024
LANE = 128
SUBLANE = 8

F32 = jnp.float32
BF16 = jnp.bfloat16
NEG_INF = float("-inf")


def _params(*sem):
    return pltpu.CompilerParams(dimension_semantics=sem, vmem_limit_bytes=V7X_VMEM_LIMIT_BYTES)


def _dot(a, b, precision=None):
    return jnp.dot(a, b, preferred_element_type=F32, precision=precision)


def _dot_nt(a, b):
    return lax.dot_general(a, b, (((1,), (1,)), ((), ())), preferred_element_type=F32)


def _tile(dim, want):
    return want if dim % want == 0 else dim


def _mod_kernel(c_ref, w0_ref, w1_ref, o_ref):
    @pl.when(pl.program_id(0) == 0)
    def _():
        o_ref[...] = jnp.zeros_like(o_ref)

    c = c_ref[...]
    a = (c * jax.nn.sigmoid(c)).astype(BF16)
    half = w0_ref.shape[1]
    o_ref[:, :half] += _dot(a, w0_ref[...].astype(BF16))
    o_ref[:, half:] += _dot(a, w1_ref[...].astype(BF16))


def mod_matmul(c_pad, w_mod):
    rows, d = c_pad.shape
    n = w_mod.shape[1]
    tk = _tile(d, LANE)
    return pl.pallas_call(
        _mod_kernel,
        grid=(d // tk,),
        in_specs=[pl.BlockSpec((rows, tk), lambda k: (0, k)),
                  pl.BlockSpec((tk, n // 2), lambda k: (k, 0)),
                  pl.BlockSpec((tk, n // 2), lambda k: (k, 1))],
        out_specs=pl.BlockSpec((rows, n), lambda k: (0, 0)),
        out_shape=jax.ShapeDtypeStruct((rows, n), F32),
        compiler_params=_params("arbitrary"),
        name="mod_matmul",
    )(c_pad, w_mod, w_mod)


def _rms(x, g):
    return x * lax.rsqrt(jnp.mean(x * x, axis=-1, keepdims=True) + EPS) * g


def _norm_mod_kernel(x_ref, g_ref, sc_ref, sh_ref, o_ref):
    y = _rms(x_ref[...], g_ref[...])
    o_ref[...] = (y * sc_ref[0] + sh_ref[0]).astype(o_ref.dtype)


def _norm_kernel(x_ref, g_ref, o_ref):
    o_ref[...] = _rms(x_ref[...], g_ref[...]).astype(o_ref.dtype)


def _top2_sum(a, b, c, d):
    hi1, lo1 = jnp.maximum(a, b), jnp.minimum(a, b)
    hi2, lo2 = jnp.maximum(c, d), jnp.minimum(c, d)
    return jnp.maximum(hi1, hi2) + jnp.maximum(jnp.minimum(hi1, hi2), jnp.maximum(lo1, lo2))


def _norm_router_kernel(x_ref, g_ref, sc_ref, sh_ref, whi_ref, wlo_ref, bias_ref,
                        o_ref, e_ref, gate_ref):
    h = _rms(x_ref[...], g_ref[...]) * sc_ref[0] + sh_ref[0]
    hi = h.astype(BF16)
    o_ref[...] = h.astype(o_ref.dtype)
    lo = (h - hi.astype(F32)).astype(BF16)
    whi, wlo = whi_ref[...], wlo_ref[...]
    logits = _dot_nt(whi, hi) + _dot_nt(wlo, hi) + _dot_nt(whi, lo)
    scores = jax.nn.sigmoid(logits)
    biased = scores + bias_ref[...]
    per_grp = N_EXPERTS // N_GROUPS
    b_rows = [biased[e:e + 1, :] for e in range(N_EXPERTS)]
    s_rows = [scores[e:e + 1, :] for e in range(N_EXPERTS)]
    grp_scores = [_top2_sum(*b_rows[g * per_grp:(g + 1) * per_grp]) for g in range(N_GROUPS)]
    best, grp = grp_scores[0], jnp.zeros(grp_scores[0].shape, jnp.int32)
    for g in range(1, N_GROUPS):
        better = grp_scores[g] > best
        grp = jnp.where(better, g, grp)
        best = jnp.where(better, grp_scores[g], best)
    b4, s4 = b_rows[:per_grp], s_rows[:per_grp]
    for g in range(1, N_GROUPS):
        in_g = grp == g
        b4 = [jnp.where(in_g, b_rows[g * per_grp + j], b4[j]) for j in range(per_grp)]
        s4 = [jnp.where(in_g, s_rows[g * per_grp + j], s4[j]) for j in range(per_grp)]
    v1, i1 = b4[0], jnp.zeros(grp.shape, jnp.int32)
    for j in range(1, per_grp):
        better = b4[j] > v1
        i1 = jnp.where(better, j, i1)
        v1 = jnp.where(better, b4[j], v1)
    v2, i2 = jnp.full(v1.shape, NEG_INF, F32), jnp.zeros(grp.shape, jnp.int32)
    for j in range(per_grp):
        better = (i1 != j) & (b4[j] > v2)
        i2 = jnp.where(better, j, i2)
        v2 = jnp.where(better, b4[j], v2)
    g1, g2 = s4[0], s4[0]
    for j in range(1, per_grp):
        g1 = jnp.where(i1 == j, s4[j], g1)
        g2 = jnp.where(i2 == j, s4[j], g2)
    total = g1 + g2
    e_ref[...] = jnp.concatenate([grp * per_grp + i1, grp * per_grp + i2], axis=0)
    gate_ref[...] = jnp.concatenate([g1 / total, g2 / total], axis=0)


def norm_mod(x2, g, sc1p, sh, seq, out_dtype=BF16, router=None):
    t, d = x2.shape
    tm = _tile(seq, 256)
    per_b = seq // tm
    row = pl.BlockSpec((tm, d), lambda i: (i, 0))
    vec = pl.BlockSpec((1, d), lambda i: (0, 0))
    mod = pl.BlockSpec((1, 1, d), lambda i: (i // per_b, 0, 0))
    g2 = g.reshape(1, d)
    if sc1p is None:
        return pl.pallas_call(
            _norm_kernel, grid=(t // tm,), in_specs=[row, vec], out_specs=row,
            out_shape=jax.ShapeDtypeStruct((t, d), out_dtype),
            compiler_params=_params("arbitrary"), name="final_norm")(x2, g2)
    if router is None:
        return pl.pallas_call(
            _norm_mod_kernel, grid=(t // tm,), in_specs=[row, vec, mod, mod], out_specs=row,
            out_shape=jax.ShapeDtypeStruct((t, d), out_dtype),
            compiler_params=_params("arbitrary"), name="norm_mod")(x2, g2, sc1p, sh)
    whi_t, wlo_t, bias = router
    ne = whi_t.shape[0]
    wspec = pl.BlockSpec((ne, d), lambda i: (0, 0))
    kspec = pl.BlockSpec((TOP_K, tm), lambda i: (0, i))
    return pl.pallas_call(
        _norm_router_kernel, grid=(t // tm,),
        in_specs=[row, vec, mod, mod, wspec, wspec, pl.BlockSpec((ne, 1), lambda i: (0, 0))],
        out_specs=[row, kspec, kspec],
        out_shape=[jax.ShapeDtypeStruct((t, d), out_dtype),
                   jax.ShapeDtypeStruct((TOP_K, t), jnp.int32),
                   jax.ShapeDtypeStruct((TOP_K, t), F32)],
        compiler_params=_params("arbitrary"), name="norm_router")(x2, g2, sc1p, sh, whi_t, wlo_t, bias)


def _mm_kernel(a_ref, w_ref, o_ref):
    o_ref[...] = _dot(a_ref[...], w_ref[...]).astype(o_ref.dtype)


def matmul(a, w, out_dtype, tm=1024, tn=1024):
    m, kd = a.shape
    n = w.shape[1]
    tm, tn = _tile(m, tm), _tile(n, tn)
    return pl.pallas_call(
        _mm_kernel,
        grid=(m // tm, n // tn),
        in_specs=[pl.BlockSpec((tm, kd), lambda i, j: (i, 0)),
                  pl.BlockSpec((kd, tn), lambda i, j: (0, j))],
        out_specs=pl.BlockSpec((tm, tn), lambda i, j: (i, j)),
        out_shape=jax.ShapeDtypeStruct((m, n), out_dtype),
        compiler_params=_params("parallel", "parallel"),
        name="matmul",
    )(a, w)


def _mm_res_kernel(a_ref, w_ref, x_ref, gate_ref, o_ref):
    o_ref[...] = x_ref[...] + gate_ref[0] * _dot(a_ref[...], w_ref[...])


def matmul_residual(a, w, x2, gate, seq, tm=1024, tn=1024):
    m, kd = a.shape
    n = w.shape[1]
    tm, tn = _tile(seq, tm), _tile(n, tn)
    per_b = seq // tm
    return pl.pallas_call(
        _mm_res_kernel,
        grid=(m // tm, n // tn),
        in_specs=[pl.BlockSpec((tm, kd), lambda i, j: (i, 0)),
                  pl.BlockSpec((kd, tn), lambda i, j: (0, j)),
                  pl.BlockSpec((tm, tn), lambda i, j: (i, j)),
                  pl.BlockSpec((1, 1, tn), lambda i, j: (i // per_b, 0, j))],
        out_specs=pl.BlockSpec((tm, tn), lambda i, j: (i, j)),
        out_shape=jax.ShapeDtypeStruct((m, n), F32),
        compiler_params=_params("parallel", "parallel"),
        name="matmul_residual",
    )(a, w, x2, gate)


def _rope_pair(r2, cos_ref, sin_ref):
    return r2 * cos_ref[...] + pltpu.roll(r2, MLA_ROPE, 1) * sin_ref[...]


def _with_ones_rows(vt):
    return jnp.concatenate([vt, jnp.ones((ONES_ROWS, vt.shape[1]), vt.dtype)], axis=0)


def _qup_kernel(cq_ref, g_ref, w_ref, cos_ref, sin_ref, o_ref, an_ref, *, scale):
    @pl.when(pl.program_id(1) == 0)
    def _():
        an_ref[...] = _rms(cq_ref[...], g_ref[...]).astype(BF16)

    an = an_ref[...]
    for hh in range(w_ref.shape[0]):
        res = _dot(an, w_ref[hh])
        roped = _rope_pair(res[:, MLA_NOPE:], cos_ref, sin_ref)
        q = jnp.concatenate([res[:, :MLA_NOPE], roped], axis=1) * scale
        o_ref[hh, 0] = q.T.astype(o_ref.dtype)


def mla_q_up(proj_a, g, w_q, cos_p, sin_p, seq, scale):
    t = proj_a.shape[0]
    heads = w_q.shape[0]
    hb = UP_HEADS_PER_STEP
    tm = _tile(seq, ATTN_TILE)
    per_b = seq // tm
    return pl.pallas_call(
        functools.partial(_qup_kernel, scale=scale),
        grid=(t // tm, heads // hb),
        in_specs=[pl.BlockSpec((tm, Q_LORA), lambda i, h: (i, 0)),
                  pl.BlockSpec((1, Q_LORA), lambda i, h: (0, 0)),
                  pl.BlockSpec((hb, Q_LORA, MLA_DK_PAD), lambda i, h: (h, 0, 0)),
                  pl.BlockSpec((tm, LANE), lambda i, h: (i % per_b, 0)),
                  pl.BlockSpec((tm, LANE), lambda i, h: (i % per_b, 0))],
        out_specs=pl.BlockSpec((hb, 1, MLA_DK_PAD, tm), lambda i, h: (h, i, 0, 0)),
        out_shape=jax.ShapeDtypeStruct((heads, t // tm, MLA_DK_PAD, tm), BF16),
        scratch_shapes=[pltpu.VMEM((tm, Q_LORA), BF16)],
        compiler_params=_params("parallel", "arbitrary"),
        name="mla_q_up",
    )(proj_a, g.reshape(1, Q_LORA), w_q, cos_p, sin_p)


def _kvup_kernel(ckv_ref, g_ref, w_ref, kpe_ref, cos_ref, sin_ref, k_ref, vt_ref, an_ref):
    @pl.when(pl.program_id(1) == 0)
    def _():
        an_ref[...] = _rms(ckv_ref[...], g_ref[...]).astype(BF16)

    an = an_ref[...]
    k_rot = _rope_pair(kpe_ref[...], cos_ref, sin_ref)
    hw = MLA_NOPE + MLA_V
    for hh in range(k_ref.shape[0]):
        res = _dot(an, w_ref[:, hh * hw:(hh + 1) * hw])
        k_ref[hh] = jnp.concatenate([res[:, :MLA_NOPE], k_rot], axis=1).astype(k_ref.dtype)
        vt_ref[hh, 0] = _with_ones_rows(res[:, MLA_NOPE:].T).astype(vt_ref.dtype)


def mla_kv_up(proj_a, g, w_kv, cos_p, sin_p, seq):
    t = proj_a.shape[0]
    hw = MLA_NOPE + MLA_V
    heads = w_kv.shape[1] // hw
    hb = UP_HEADS_PER_STEP
    tm = _tile(seq, ATTN_TILE)
    per_b = seq // tm
    return pl.pallas_call(
        _kvup_kernel,
        grid=(t // tm, heads // hb),
        in_specs=[pl.BlockSpec((tm, KV_LORA), lambda i, h: (i, Q_LORA // KV_LORA)),
                  pl.BlockSpec((1, KV_LORA), lambda i, h: (0, 0)),
                  pl.BlockSpec((KV_LORA, hb * hw), lambda i, h: (0, h)),
                  pl.BlockSpec((tm, LANE), lambda i, h: (i, (Q_LORA + KV_LORA) // LANE)),
                  pl.BlockSpec((tm, LANE), lambda i, h: (i % per_b, 0)),
                  pl.BlockSpec((tm, LANE), lambda i, h: (i % per_b, 0))],
        out_specs=[pl.BlockSpec((hb, tm, MLA_DK_PAD), lambda i, h: (h, i, 0)),
                   pl.BlockSpec((hb, 1, MLA_V + ONES_ROWS, tm), lambda i, h: (h, i, 0, 0))],
        out_shape=[jax.ShapeDtypeStruct((heads, t, MLA_DK_PAD), BF16),
                   jax.ShapeDtypeStruct((heads, t // tm, MLA_V + ONES_ROWS, tm), BF16)],
        scratch_shapes=[pltpu.VMEM((tm, KV_LORA), BF16)],
        compiler_params=_params("parallel", "arbitrary"),
        name="mla_kv_up",
    )(proj_a, g.reshape(1, KV_LORA), w_kv, proj_a, cos_p, sin_p)


def _softmax_first(s, vt, m_ref, acc_ref):
    m = jnp.max(s, axis=0, keepdims=True)
    p = jnp.exp(s - m).astype(vt.dtype)
    m_ref[...] = m
    acc_ref[...] = _dot(vt, p)


def _softmax_step(s, vt, m_ref, acc_ref):
    m_prev = m_ref[...]
    m_new = jnp.maximum(m_prev, jnp.max(s, axis=0, keepdims=True))
    alpha = jnp.exp(m_prev - m_new)
    p = jnp.exp(s - m_new).astype(vt.dtype)
    acc_ref[...] = alpha * acc_ref[...] + _dot(vt, p)
    m_ref[...] = m_new


def _softmax_finish(acc_ref, dv):
    acc = acc_ref[...]
    return (acc[:dv, :] / acc[dv:dv + 1, :]).T


def _causal_mask_t(s):
    key = lax.broadcasted_iota(jnp.int32, s.shape, 0)
    qry = lax.broadcasted_iota(jnp.int32, s.shape, 1)
    return jnp.where(key <= qry, s, NEG_INF)


def _mla_attn_kernel(qt_ref, k_ref, vt_ref, o_ref, m_ref, acc_ref, *, tq, dv):
    i = pl.program_id(2)
    n_h = qt_ref.shape[0]
    qts = [qt_ref[h, 0] for h in range(n_h)]

    def scores(h, j):
        off = pl.multiple_of(j * tq, tq)
        return _dot(k_ref[h, pl.ds(off, tq), :], qts[h])

    for h in range(n_h):
        _softmax_first(_causal_mask_t(scores(h, i)), vt_ref[h, i], m_ref.at[h], acc_ref.at[h])

    def body(j, carry):
        for h in range(n_h):
            _softmax_step(scores(h, j), vt_ref[h, j], m_ref.at[h], acc_ref.at[h])
        return carry

    lax.fori_loop(0, i, body, 0)
    for h in range(n_h):
        o_ref[:, h * dv:(h + 1) * dv] = _softmax_finish(acc_ref.at[h], dv).astype(o_ref.dtype)


def mla_attention(qt, k, vt, batch, seq, out_dtype):
    heads, _, dk, tq = qt.shape
    t = k.shape[1]
    dvx = vt.shape[2]
    dv = dvx - ONES_ROWS
    nq = seq // tq
    hb = ATTN_HEADS_PER_STEP
    return pl.pallas_call(
        functools.partial(_mla_attn_kernel, tq=tq, dv=dv),
        grid=(batch, heads // hb, nq),
        in_specs=[pl.BlockSpec((hb, 1, dk, tq), lambda b, h, i: (h, b * nq + i, 0, 0)),
                  pl.BlockSpec((hb, seq, dk), lambda b, h, i: (h, b, 0)),
                  pl.BlockSpec((hb, nq, dvx, tq), lambda b, h, i: (h, b, 0, 0))],
        out_specs=pl.BlockSpec((tq, hb * dv), lambda b, h, i: (b * nq + i, h)),
        out_shape=jax.ShapeDtypeStruct((t, heads * dv), out_dtype),
        scratch_shapes=[pltpu.VMEM((hb, 1, tq), F32), pltpu.VMEM((hb, dvx, tq), F32)],
        compiler_params=_params("parallel", "parallel", "arbitrary"),
        name="mla_attention",
    )(qt, k, vt)


def _conv_kernel(b_ref, c_ref, h_ref, w_ref, o_ref, carry_ref, *, per_b):
    i = pl.program_id(0)

    @pl.when(i % per_b == 0)
    def _():
        carry_ref[...] = jnp.zeros_like(carry_ref)

    tm = o_ref.shape[0]
    w0, w1, w2 = w_ref[0:1, :], w_ref[1:2, :], w_ref[2:3, :]
    u = c_ref[...] * h_ref[...]
    y = w0 * pltpu.roll(u, 2, 0) + w1 * pltpu.roll(u, 1, 0) + w2 * u
    o_ref[...] = (b_ref[...] * y).astype(o_ref.dtype)
    u8 = u[0:SUBLANE, :]
    tail = carry_ref[...]
    r8 = lax.broadcasted_iota(jnp.int32, u8.shape, 0)
    p1 = jnp.where(r8 < 1, pltpu.roll(tail, 1, 0), pltpu.roll(u8, 1, 0))
    p2 = jnp.where(r8 < 2, pltpu.roll(tail, 2, 0), pltpu.roll(u8, 2, 0))
    y8 = w0 * p2 + w1 * p1 + w2 * u8
    o_ref[0:SUBLANE, :] = (b_ref[0:SUBLANE, :] * y8).astype(o_ref.dtype)
    carry_ref[...] = u[tm - SUBLANE:tm, :]


def conv_mixer(bch, conv_w, seq, out_dtype):
    t = bch.shape[0]
    ch = conv_w.shape[1]
    tm = _tile(seq, 256)
    per_b = seq // tm
    return pl.pallas_call(
        functools.partial(_conv_kernel, per_b=per_b),
        grid=(t // tm,),
        in_specs=[pl.BlockSpec((tm, ch), lambda i: (i, 0)),
                  pl.BlockSpec((tm, ch), lambda i: (i, 1)),
                  pl.BlockSpec((tm, ch), lambda i: (i, 2)),
                  pl.BlockSpec((CONV_W, ch), lambda i: (0, 0))],
        out_specs=pl.BlockSpec((tm, ch), lambda i: (i, 0)),
        out_shape=jax.ShapeDtypeStruct((t, ch), out_dtype),
        scratch_shapes=[pltpu.VMEM((SUBLANE, ch), F32)],
        compiler_params=_params("arbitrary"),
        name="conv_mixer",
    )(bch, bch, bch, conv_w)


def _moba_prep_kernel(q_ref, k_ref, v_ref, cos_ref, sin_ref, qo_ref, ko_ref, vo_ref, km_ref):
    heads = ko_ref.shape[0]
    cos, sin = cos_ref[...], sin_ref[...]
    n_sub = km_ref.shape[0]
    means = [[] for _ in range(n_sub)]
    for h in range(heads):
        sl = slice(h * MOBA_HD, (h + 1) * MOBA_HD)
        qh = q_ref[:, sl]
        kh = k_ref[:, sl]
        qr = qh * cos + pltpu.roll(qh, MOBA_HD // 2, 1) * sin
        qo_ref[h, 0] = qr.T.astype(qo_ref.dtype)
        kr = kh * cos + pltpu.roll(kh, MOBA_HD // 2, 1) * sin
        ko_ref[h] = kr.astype(ko_ref.dtype)
        vo_ref[h, 0] = _with_ones_rows(v_ref[:, sl].T).astype(vo_ref.dtype)
        for s in range(n_sub):
            means[s].append(jnp.mean(kr[s * MOBA_BLOCK:(s + 1) * MOBA_BLOCK], axis=0, keepdims=True))
    for s in range(n_sub):
        km_ref[s] = jnp.concatenate(means[s], axis=0)


def moba_prep(qkv, cos_f, sin_s, seq):
    t = qkv.shape[0]
    width = qkv.shape[1] // 3
    heads = width // MOBA_HD
    tm = 2 * MOBA_BLOCK
    assert seq % tm == 0
    per_b = seq // tm
    nt = t // tm
    return pl.pallas_call(
        _moba_prep_kernel,
        grid=(nt,),
        in_specs=[pl.BlockSpec((tm, width), lambda i: (i, 0)),
                  pl.BlockSpec((tm, width), lambda i: (i, 1)),
                  pl.BlockSpec((tm, width), lambda i: (i, 2)),
                  pl.BlockSpec((tm, MOBA_HD), lambda i: (i % per_b, 0)),
                  pl.BlockSpec((tm, MOBA_HD), lambda i: (i % per_b, 0))],
        out_specs=[pl.BlockSpec((heads, 1, MOBA_HD, tm), lambda i: (0, i, 0, 0)),
                   pl.BlockSpec((heads, tm, MOBA_HD), lambda i: (0, i, 0)),
                   pl.BlockSpec((heads, 1, MOBA_HD + ONES_ROWS, tm), lambda i: (0, i, 0, 0)),
                   pl.BlockSpec((2, heads, MOBA_HD), lambda i: (i, 0, 0))],
        out_shape=[jax.ShapeDtypeStruct((heads, nt, MOBA_HD, tm), BF16),
                   jax.ShapeDtypeStruct((heads, t, MOBA_HD), BF16),
                   jax.ShapeDtypeStruct((heads, nt, MOBA_HD + ONES_ROWS, tm), BF16),
                   jax.ShapeDtypeStruct((2 * nt, heads, MOBA_HD), F32)],
        compiler_params=_params("arbitrary"),
        name="moba_prep",
    )(qkv, qkv, qkv, cos_f, sin_s)


def _moba_attn_kernel(qt_ref, k_ref, vt_ref, km_ref, o_ref, m_ref, acc_ref, sel_ref, *, scale, dv):
    i = pl.program_id(2)
    n_h = qt_ref.shape[0]
    n_blk = km_ref.shape[2]
    tq = qt_ref.shape[3]
    blk_w = MOBA_BLOCK
    blk = lax.broadcasted_iota(jnp.int32, (n_blk, tq), 0)
    own = 2 * i + (lax.broadcasted_iota(jnp.int32, (n_blk, tq), 1) >= blk_w).astype(jnp.int32)
    past = blk < own
    qts = []
    for h in range(n_h):
        qtf = qt_ref[h, 0].astype(F32)
        g = _dot(km_ref[0, h], qtf, precision=lax.Precision.HIGHEST)
        g = jnp.where(past, g, NEG_INF)
        sel = jnp.zeros(g.shape, F32)
        for _ in range(min(MOBA_TOPK, n_blk)):
            mx = jnp.max(g, axis=0, keepdims=True)
            first = jnp.min(jnp.where(g == mx, blk, n_blk), axis=0, keepdims=True)
            hit = blk == first
            sel = jnp.where(hit, 1.0, sel)
            g = jnp.where(hit, NEG_INF, g)
        sel_ref[h] = jnp.where(past, sel, 0.0)
        qts.append((qtf * scale).astype(k_ref.dtype))

    def scores(h, p):
        off = pl.multiple_of(p * tq, tq)
        return _dot(k_ref[h, pl.ds(off, tq), :], qts[h])

    def chosen(h, n):
        return sel_ref[h, pl.ds(n, 1), :] > 0.5

    key = lax.broadcasted_iota(jnp.int32, (blk_w, tq), 0)
    qry = lax.broadcasted_iota(jnp.int32, (blk_w, tq), 1)
    bot_ok = key <= qry - blk_w
    for h in range(n_h):
        s = scores(h, i)
        chosen_lim = jnp.where(chosen(h, 2 * i), blk_w, -1)
        top_ok = key <= jnp.where(qry < blk_w, qry, chosen_lim)
        s = jnp.concatenate([jnp.where(top_ok, s[:blk_w], NEG_INF),
                             jnp.where(bot_ok, s[blk_w:], NEG_INF)], axis=0)
        _softmax_first(s, vt_ref[h, i], m_ref.at[h], acc_ref.at[h])

    def body(p, carry):
        for h in range(n_h):
            s = scores(h, p)
            s = jnp.concatenate([jnp.where(chosen(h, 2 * p), s[:blk_w], NEG_INF),
                                 jnp.where(chosen(h, 2 * p + 1), s[blk_w:], NEG_INF)], axis=0)
            _softmax_step(s, vt_ref[h, p], m_ref.at[h], acc_ref.at[h])
        return carry

    lax.fori_loop(0, i, body, 0)
    for h in range(n_h):
        o_ref[:, h * dv:(h + 1) * dv] = _softmax_finish(acc_ref.at[h], dv).astype(o_ref.dtype)


def moba_attention(qt, k, vt, k_mean, batch, seq, out_dtype):
    heads, _, hd, tq = qt.shape
    t = k.shape[1]
    dvx = vt.shape[2]
    dv = dvx - ONES_ROWS
    n_blk = seq // MOBA_BLOCK
    nq = seq // tq
    hb = ATTN_HEADS_PER_STEP
    return pl.pallas_call(
        functools.partial(_moba_attn_kernel, scale=hd ** -0.5, dv=dv),
        grid=(batch, heads // hb, nq),
        in_specs=[pl.BlockSpec((hb, 1, hd, tq), lambda b, h, i: (h, b * nq + i, 0, 0)),
                  pl.BlockSpec((hb, seq, hd), lambda b, h, i: (h, b, 0)),
                  pl.BlockSpec((hb, nq, dvx, tq), lambda b, h, i: (h, b, 0, 0)),
                  pl.BlockSpec((1, hb, n_blk, hd), lambda b, h, i: (b, h, 0, 0))],
        out_specs=pl.BlockSpec((tq, hb * dv), lambda b, h, i: (b * nq + i, h)),
        out_shape=jax.ShapeDtypeStruct((t, heads * dv), out_dtype),
        scratch_shapes=[pltpu.VMEM((hb, 1, tq), F32), pltpu.VMEM((hb, dvx, tq), F32),
                        pltpu.VMEM((hb, n_blk, tq), F32)],
        compiler_params=_params("parallel", "parallel", "arbitrary"),
        name="moba_attention",
    )(qt, k, vt, k_mean)


def _group_norm_kernel(a_ref, b_ref, c_ref, ga_ref, gb_ref, gc_ref, o_ref):
    wa, wb = a_ref.shape[1], b_ref.shape[1]
    o_ref[:, 0:wa] = _rms(a_ref[...].astype(F32), ga_ref[...]).astype(o_ref.dtype)
    o_ref[:, wa:wa + wb] = _rms(b_ref[...].astype(F32), gb_ref[...]).astype(o_ref.dtype)
    o_ref[:, wa + wb:] = _rms(c_ref[...].astype(F32), gc_ref[...]).astype(o_ref.dtype)


def group_norm(y_a, y_b, y_c, g):
    t = y_a.shape[0]
    wa, wb, wc = y_a.shape[1], y_b.shape[1], y_c.shape[1]
    tm = _tile(t, 256)
    ga, gb, gc = g[:wa].reshape(1, wa), g[wa:wa + wb].reshape(1, wb), g[wa + wb:].reshape(1, wc)
    rows = lambda w: pl.BlockSpec((tm, w), lambda i: (i, 0))
    vec = lambda w: pl.BlockSpec((1, w), lambda i: (0, 0))
    return pl.pallas_call(
        _group_norm_kernel,
        grid=(t // tm,),
        in_specs=[rows(wa), rows(wb), rows(wc), vec(wa), vec(wb), vec(wc)],
        out_specs=rows(wa + wb + wc),
        out_shape=jax.ShapeDtypeStruct((t, wa + wb + wc), BF16),
        compiler_params=_params("arbitrary"),
        name="group_norm",
    )(y_a, y_b, y_c, ga, gb, gc)


def _ffn_up_kernel(bexp_ref, first_ref, nused_ref, x_ref, wg_ref, wu_ref, a_ref, wg_bf, wu_bf):
    b = pl.program_id(1)
    used = b < nused_ref[0]

    @pl.when(used & (first_ref[b] == 1))
    def _():
        wg_bf[...] = wg_ref[0, 0].astype(BF16)
        wu_bf[...] = wu_ref[0, 0].astype(BF16)

    @pl.when(used)
    def _():
        x = x_ref[...]
        gate = _dot(x, wg_bf[...])
        up = _dot(x, wu_bf[...])
        a_ref[...] = (gate * jax.nn.sigmoid(gate) * up).astype(a_ref.dtype)

    @pl.when(jnp.logical_not(used))
    def _():
        a_ref[...] = jnp.zeros_like(a_ref)


def _ffn_down_kernel(bexp_ref, first_ref, nused_ref, a_ref, wd_ref, o_ref, wd_bf):
    b = pl.program_id(1)
    used = b < nused_ref[0]

    @pl.when(used & (first_ref[b] == 1))
    def _():
        wd_bf[...] = wd_ref[0, 0].astype(BF16)

    @pl.when(used)
    def _():
        o_ref[...] = _dot(a_ref[...], wd_bf[...]).astype(o_ref.dtype)

    @pl.when(jnp.logical_not(used))
    def _():
        o_ref[...] = jnp.zeros_like(o_ref)


def expert_ffn(xs, block_exp, block_first, n_used, w_gate, w_up, w_down, layer):
    n_slots, d = xs.shape
    ff = w_gate.shape[3]
    fh, dh = ff // FF_SPLIT, d // FF_SPLIT
    n_blocks = n_slots // MOE_BLOCK
    up_spec = pltpu.PrefetchScalarGridSpec(
        num_scalar_prefetch=3,
        grid=(FF_SPLIT, n_blocks),
        in_specs=[pl.BlockSpec((MOE_BLOCK, d), lambda j, b, be, fi, nu: (b, 0)),
                  pl.BlockSpec((1, 1, d, fh), lambda j, b, be, fi, nu: (layer, be[b], 0, j)),
                  pl.BlockSpec((1, 1, d, fh), lambda j, b, be, fi, nu: (layer, be[b], 0, j))],
        out_specs=pl.BlockSpec((MOE_BLOCK, fh), lambda j, b, be, fi, nu: (b, j)),
        scratch_shapes=[pltpu.VMEM((d, fh), BF16), pltpu.VMEM((d, fh), BF16)],
    )
    act = pl.pallas_call(
        _ffn_up_kernel,
        grid_spec=up_spec,
        out_shape=jax.ShapeDtypeStruct((n_slots, ff), BF16),
        compiler_params=_params("arbitrary", "arbitrary"),
        name="expert_ffn_up",
    )(block_exp, block_first, n_used, xs, w_gate, w_up)
    down_spec = pltpu.PrefetchScalarGridSpec(
        num_scalar_prefetch=3,
        grid=(FF_SPLIT, n_blocks),
        in_specs=[pl.BlockSpec((MOE_BLOCK, ff), lambda j, b, be, fi, nu: (b, 0)),
                  pl.BlockSpec((1, 1, ff, dh), lambda j, b, be, fi, nu: (layer, be[b], 0, j))],
        out_specs=pl.BlockSpec((MOE_BLOCK, dh), lambda j, b, be, fi, nu: (b, j)),
        scratch_shapes=[pltpu.VMEM((ff, dh), BF16)],
    )
    return pl.pallas_call(
        _ffn_down_kernel,
        grid_spec=down_spec,
        out_shape=jax.ShapeDtypeStruct((n_slots, d), F32),
        compiler_params=_params("arbitrary", "arbitrary"),
        name="expert_ffn_down",
    )(block_exp, block_first, n_used, act, w_down)


def _start_row_copies(idx_ref, base, n_rows, src_ref, dst_ref, sem):
    def issue(r, carry):
        row = idx_ref[base + r]
        pltpu.make_async_copy(src_ref.at[pl.ds(row, 1)], dst_ref.at[pl.ds(r, 1)], sem).start()
        return carry

    lax.fori_loop(0, n_rows, issue, 0, unroll=8)


def _wait_row_copies(n_rows, src_ref, dst_ref, sem):
    pltpu.make_async_copy(src_ref.at[pl.ds(0, n_rows)], dst_ref, sem).wait()


def _gather_cast_kernel(idx_ref, nused_ref, src_ref, o_ref, buf, sem):
    b = pl.program_id(0)
    rows = o_ref.shape[0]
    n_used = nused_ref[0]

    @pl.when((b == 0) & (n_used > 0))
    def _():
        _start_row_copies(idx_ref, 0, rows, src_ref, buf.at[0], sem.at[0])

    @pl.when(b + 1 < n_used)
    def _():
        nxt = (b + 1) % 2
        _start_row_copies(idx_ref, (b + 1) * rows, rows, src_ref, buf.at[nxt], sem.at[nxt])

    @pl.when(b < n_used)
    def _():
        cur = b % 2
        _wait_row_copies(rows, src_ref, buf.at[cur], sem.at[cur])
        o_ref[...] = buf[cur].astype(o_ref.dtype)

    @pl.when(b >= n_used)
    def _():
        o_ref[...] = jnp.zeros_like(o_ref)


def gather_rows_cast(src, idx, n_used_tiles, out_dtype):
    d = src.shape[1]
    m = idx.shape[0]
    rows = GATHER_ROWS
    grid_spec = pltpu.PrefetchScalarGridSpec(
        num_scalar_prefetch=2,
        grid=(m // rows,),
        in_specs=[pl.BlockSpec(memory_space=pl.ANY)],
        out_specs=pl.BlockSpec((rows, d), lambda b, ix, nu: (b, 0)),
        scratch_shapes=[pltpu.VMEM((2, rows, d), src.dtype), pltpu.SemaphoreType.DMA((2,))],
    )
    return pl.pallas_call(
        _gather_cast_kernel,
        grid_spec=grid_spec,
        out_shape=jax.ShapeDtypeStruct((m, d), out_dtype),
        compiler_params=_params("arbitrary"),
        name="gather_rows_cast",
    )(idx, n_used_tiles, src)


def _combine_kernel(pos_ref, x_ref, gate_ref, w_ref, ys_ref, o_ref, ybuf, sem, *, n_tok):
    i = pl.program_id(0)
    n_tiles = pl.num_programs(0)
    tm = x_ref.shape[0]

    def start_tile(tile, slot):
        for kk in range(TOP_K):
            _start_row_copies(pos_ref, kk * n_tok + tile * tm, tm, ys_ref, ybuf.at[slot, kk],
                              sem.at[slot])

    @pl.when(i == 0)
    def _():
        start_tile(0, 0)

    @pl.when(i + 1 < n_tiles)
    def _():
        start_tile(i + 1, (i + 1) % 2)

    cur = i % 2
    for kk in range(TOP_K):
        _wait_row_copies(tm, ys_ref, ybuf.at[cur, kk], sem.at[cur])
    w = w_ref[...]
    acc = w[:, 0:1] * ybuf[cur, 0]
    for kk in range(1, TOP_K):
        acc = acc + w[:, kk:kk + 1] * ybuf[cur, kk]
    o_ref[...] = x_ref[...] + gate_ref[0] * acc


def moe_combine(x2, gate, ys, pos, w_tok, seq):
    t, d = x2.shape
    tm = _tile(seq, GATHER_ROWS)
    per_b = seq // tm
    grid_spec = pltpu.PrefetchScalarGridSpec(
        num_scalar_prefetch=1,
        grid=(t // tm,),
        in_specs=[pl.BlockSpec((tm, d), lambda i, ps: (i, 0)),
                  pl.BlockSpec((1, 1, d), lambda i, ps: (i // per_b, 0, 0)),
                  pl.BlockSpec((tm, TOP_K), lambda i, ps: (i, 0)),
                  pl.BlockSpec(memory_space=pl.ANY)],
        out_specs=pl.BlockSpec((tm, d), lambda i, ps: (i, 0)),
        scratch_shapes=[pltpu.VMEM((2, TOP_K, tm, d), ys.dtype), pltpu.SemaphoreType.DMA((2,))],
    )
    return pl.pallas_call(
        functools.partial(_combine_kernel, n_tok=t),
        grid_spec=grid_spec,
        out_shape=jax.ShapeDtypeStruct((t, d), F32),
        compiler_params=_params("arbitrary"),
        name="moe_combine",
    )(pos, x2, gate, w_tok, ys)


def _group_by_expert(e_idx):
    n_tok = e_idx.shape[1]
    n_asg = n_tok * TOP_K
    flat_e = e_idx.reshape(n_asg)
    onehot = (flat_e[:, None] == jnp.arange(N_EXPERTS)[None, :]).astype(jnp.int32)
    running = jnp.cumsum(onehot, axis=0)
    rank = jnp.sum(running * onehot, axis=1) - 1
    sizes = running[-1]
    padded = (sizes + MOE_BLOCK - 1) // MOE_BLOCK * MOE_BLOCK
    pad_end = jnp.cumsum(padded)
    pad_start = pad_end - padded
    dest = (jnp.sum(pad_start[None, :] * onehot, axis=1) + rank).astype(jnp.int32)
    n_blocks = -(-n_asg // MOE_BLOCK) + N_EXPERTS
    n_slots = n_blocks * MOE_BLOCK
    slot_tok = jnp.zeros((n_slots,), jnp.int32).at[dest].set(
        jnp.arange(n_asg, dtype=jnp.int32) % n_tok, mode="promise_in_bounds")
    block_start = jnp.arange(n_blocks, dtype=jnp.int32) * MOE_BLOCK
    block_exp = jnp.minimum(jnp.sum(block_start[:, None] >= pad_end[None, :], axis=1),
                            N_EXPERTS - 1).astype(jnp.int32)
    n_used = (pad_end[-1] // MOE_BLOCK).astype(jnp.int32)
    block_exp = jnp.where(jnp.arange(n_blocks) < n_used, block_exp, block_exp[n_used - 1])
    prev_exp = jnp.concatenate([jnp.full((1,), -1, jnp.int32), block_exp[:-1]])
    block_first = (block_exp != prev_exp).astype(jnp.int32)
    return dest.reshape(TOP_K, n_tok), slot_tok, block_exp, block_first, n_used.reshape(1)


def _rot_cols(w):
    half = w.shape[-1] // 2
    return jnp.concatenate([-w[..., half:], w[..., :half]], axis=-1)


def _prep_layer(w_in, w_uq, w_ukv, w_out):
    a_end = Q_LORA + KV_LORA + MLA_ROPE
    b_end = a_end + 3 * CONV_CH
    w_kpe = w_in[:, Q_LORA + KV_LORA:a_end]
    w_a = jnp.concatenate([w_in[:, :a_end], _rot_cols(w_kpe)], axis=1).astype(BF16)
    w_b = w_in[:, a_end:b_end].astype(BF16)
    w_c = w_in[:, b_end:].astype(BF16)
    wq = w_uq.reshape(Q_LORA, MLA_HEADS, MLA_NOPE + MLA_ROPE)
    wq_rope = wq[..., MLA_NOPE:]
    wq = jnp.concatenate([wq, _rot_cols(wq_rope)], axis=-1).transpose(1, 0, 2).astype(BF16)
    return dict(w_a=w_a, w_b=w_b, w_c=w_c, w_q=wq, w_kv=w_ukv.astype(BF16),
                w_out=w_out.astype(BF16))


def _rope_tables(seq, dim):
    inv = 1.0 / (ROPE_THETA ** (jnp.arange(0, dim, 2, dtype=F32) / dim))
    ang = jnp.arange(seq, dtype=F32)[:, None] * inv[None, :]
    return jnp.cos(ang), jnp.sin(ang)


def kernel(x, c, w_mod, mod_table, mix_norm_g, w_in, q_norm_g, kv_norm_g, w_uq, w_ukv, conv_w,
           group_norm_g, w_out, ffn_norm_g, w_router, router_bias, w_gate, w_up, w_down,
           final_norm_g):
    batch, seq, d = x.shape
    depth = w_in.shape[0]
    t = batch * seq
    x2 = x.reshape(t, d)

    cos_a, sin_a = _rope_tables(seq, MLA_ROPE)
    zeros_a = jnp.zeros_like(cos_a)
    cos_p = jnp.concatenate([cos_a, cos_a, zeros_a, zeros_a], axis=1)
    sin_p = jnp.concatenate([sin_a, sin_a, zeros_a, zeros_a], axis=1)
    cos_b, sin_b = _rope_tables(seq, MOBA_HD)
    cos_f = jnp.concatenate([cos_b, cos_b], axis=1)
    sin_s = jnp.concatenate([-sin_b, sin_b], axis=1)

    c_pad = jnp.zeros((SUBLANE, d), F32).at[:batch].set(c)
    mod_shared = mod_matmul(c_pad, w_mod)[:batch].reshape(batch, N_MOD, d)

    wr_t = w_router.T
    wr_hi = wr_t.astype(BF16)
    wr_lo = (wr_t - wr_hi.astype(F32)).astype(BF16)
    router = (wr_hi, wr_lo, router_bias.astype(F32).reshape(N_EXPERTS, 1))
    mla_scale = (MLA_NOPE + MLA_ROPE) ** -0.5

    for l in range(depth):
        p = _prep_layer(w_in[l], w_uq[l], w_ukv[l], w_out[l])
        mod = mod_shared + mod_table[l][None]
        sh1, sc1, g1, sh2, sc2, g2 = [mod[:, i][:, None, :] for i in range(N_MOD)]

        h = norm_mod(x2, mix_norm_g[l], 1.0 + sc1, sh1, seq)
        proj_a = matmul(h, p["w_a"], F32, tm=512)
        bch = matmul(h, p["w_b"], F32)
        qkv = matmul(h, p["w_c"], F32)

        q_a = mla_q_up(proj_a, q_norm_g[l], p["w_q"], cos_p, sin_p, seq, mla_scale)
        k_a, v_a = mla_kv_up(proj_a, kv_norm_g[l], p["w_kv"], cos_p, sin_p, seq)
        y_a = mla_attention(q_a, k_a, v_a, batch, seq, BF16)

        y_b = conv_mixer(bch, conv_w[l], seq, F32)

        q_c, k_c, v_c, k_mean = moba_prep(qkv, cos_f, sin_s, seq)
        n_blk = seq // MOBA_BLOCK
        k_mean = k_mean.reshape(batch, n_blk, MOBA_HEADS, MOBA_HD).transpose(0, 2, 1, 3)
        y_c = moba_attention(q_c, k_c, v_c, k_mean, batch, seq, BF16)

        y = group_norm(y_a, y_b, y_c, group_norm_g[l])
        x2 = matmul_residual(y, p["w_out"], x2, g1, seq)

        h2, e_idx, gates = norm_mod(x2, ffn_norm_g[l], 1.0 + sc2, sh2, seq, out_dtype=F32,
                                    router=router)
        pos, slot_tok, block_exp, block_first, n_used = _group_by_expert(e_idx)
        xs = gather_rows_cast(h2, slot_tok, n_used * (MOE_BLOCK // GATHER_ROWS), BF16)
        ys = expert_ffn(xs, block_exp, block_first, n_used, w_gate, w_up, w_down, l)
        x2 = moe_combine(x2, g2, ys, pos.reshape(-1), gates.T, seq)

    out = norm_mod(x2, final_norm_g, None, None, seq, out_dtype=F32)
    return out.reshape(batch, seq, d)
```

```python
import functools

import jax
import jax.numpy as jnp
from jax import lax
from jax.experimental import pallas as pl
from jax.experimental.pallas import tpu as pltpu

MLA_HEADS = 16
MLA_NOPE = 128
MLA_ROPE = 64
MLA_V = 128
Q_LORA = 1024
KV_LORA = 512
MLA_DK_PAD = 256
CONV_CH = 1024
CONV_W = 3
MOBA_HEADS = 8
MOBA_HD = 128
MOBA_BLOCK = 256
MOBA_TOPK = 3
ROPE_THETA = 10000.0
EPS = 1e-6
N_MOD = 6
N_EXPERTS = 16
N_GROUPS = 4
TOP_K = 2
MOE_BLOCK = 512
GATHER_ROWS = 256
FF_SPLIT = 2
UP_HEADS_PER_STEP = 4
ATTN_HEADS_PER_STEP = 4
ATTN_TILE = 512
ONES_ROWS = 16

V7X_VMEM_LIMIT_BYTES = 56 * 1024 * 1024
LANE = 128
SUBLANE = 8

F32 = jnp.float32
BF16 = jnp.bfloat16
PACKED = jnp.uint32
NEG_INF = float("-inf")


def _params(*sem):
    return pltpu.CompilerParams(dimension_semantics=sem, vmem_limit_bytes=V7X_VMEM_LIMIT_BYTES)


def _dot(a, b, precision=None):
    return jnp.dot(a, b, preferred_element_type=F32, precision=precision)


def _dot_nt(a, b):
    return lax.dot_general(a, b, (((1,), (1,)), ((), ())), preferred_element_type=F32)


def _tile(dim, want):
    return want if dim % want == 0 else dim


def _mod_kernel(c_ref, w0_ref, w1_ref, o_ref):
    @pl.when(pl.program_id(0) == 0)
    def _():
        o_ref[...] = jnp.zeros_like(o_ref)

    c = c_ref[...]
    a = (c * jax.nn.sigmoid(c)).astype(BF16)
    half = w0_ref.shape[1]
    o_ref[:, :half] += _dot(a, w0_ref[...].astype(BF16))
    o_ref[:, half:] += _dot(a, w1_ref[...].astype(BF16))


def mod_matmul(c_pad, w_mod):
    rows, d = c_pad.shape
    n = w_mod.shape[1]
    tk = _tile(d, LANE)
    return pl.pallas_call(
        _mod_kernel,
        grid=(d // tk,),
        in_specs=[pl.BlockSpec((rows, tk), lambda k: (0, k)),
                  pl.BlockSpec((tk, n // 2), lambda k: (k, 0)),
                  pl.BlockSpec((tk, n // 2), lambda k: (k, 1))],
        out_specs=pl.BlockSpec((rows, n), lambda k: (0, 0)),
        out_shape=jax.ShapeDtypeStruct((rows, n), F32),
        compiler_params=_params("arbitrary"),
        name="mod_matmul",
    )(c_pad, w_mod, w_mod)


def _rms(x, g):
    return x * lax.rsqrt(jnp.mean(x * x, axis=-1, keepdims=True) + EPS) * g


def _norm_mod_kernel(x_ref, g_ref, sc_ref, sh_ref, o_ref):
    y = _rms(x_ref[...], g_ref[...])
    o_ref[...] = (y * sc_ref[0] + sh_ref[0]).astype(o_ref.dtype)


def _norm_kernel(x_ref, g_ref, o_ref):
    o_ref[...] = _rms(x_ref[...], g_ref[...]).astype(o_ref.dtype)


def _top2_sum(a, b, c, d):
    hi1, lo1 = jnp.maximum(a, b), jnp.minimum(a, b)
    hi2, lo2 = jnp.maximum(c, d), jnp.minimum(c, d)
    return jnp.maximum(hi1, hi2) + jnp.maximum(jnp.minimum(hi1, hi2), jnp.maximum(lo1, lo2))


def _norm_router_kernel(x_ref, g_ref, sc_ref, sh_ref, whi_ref, wlo_ref, bias_ref,
                        o_ref, e_ref, gate_ref):
    h = _rms(x_ref[...], g_ref[...]) * sc_ref[0] + sh_ref[0]
    hi = h.astype(BF16)
    o_ref[...] = h.astype(o_ref.dtype)
    lo = (h - hi.astype(F32)).astype(BF16)
    whi, wlo = whi_ref[...], wlo_ref[...]
    logits = _dot_nt(whi, hi) + _dot_nt(wlo, hi) + _dot_nt(whi, lo)
    scores = jax.nn.sigmoid(logits)
    biased = scores + bias_ref[...]
    per_grp = N_EXPERTS // N_GROUPS
    b_rows = [biased[e:e + 1, :] for e in range(N_EXPERTS)]
    s_rows = [scores[e:e + 1, :] for e in range(N_EXPERTS)]
    grp_scores = [_top2_sum(*b_rows[g * per_grp:(g + 1) * per_grp]) for g in range(N_GROUPS)]
    best, grp = grp_scores[0], jnp.zeros(grp_scores[0].shape, jnp.int32)
    for g in range(1, N_GROUPS):
        better = grp_scores[g] > best
        grp = jnp.where(better, g, grp)
        best = jnp.where(better, grp_scores[g], best)
    b4, s4 = b_rows[:per_grp], s_rows[:per_grp]
    for g in range(1, N_GROUPS):
        in_g = grp == g
        b4 = [jnp.where(in_g, b_rows[g * per_grp + j], b4[j]) for j in range(per_grp)]
        s4 = [jnp.where(in_g, s_rows[g * per_grp + j], s4[j]) for j in range(per_grp)]
    v1, i1 = b4[0], jnp.zeros(grp.shape, jnp.int32)
    for j in range(1, per_grp):
        better = b4[j] > v1
        i1 = jnp.where(better, j, i1)
        v1 = jnp.where(better, b4[j], v1)
    v2, i2 = jnp.full(v1.shape, NEG_INF, F32), jnp.zeros(grp.shape, jnp.int32)
    for j in range(per_grp):
        better = (i1 != j) & (b4[j] > v2)
        i2 = jnp.where(better, j, i2)
        v2 = jnp.where(better, b4[j], v2)
    g1, g2 = s4[0], s4[0]
    for j in range(1, per_grp):
        g1 = jnp.where(i1 == j, s4[j], g1)
        g2 = jnp.where(i2 == j, s4[j], g2)
    total = g1 + g2
    e_ref[...] = jnp.concatenate([grp * per_grp + i1, grp * per_grp + i2], axis=0)
    gate_ref[...] = jnp.concatenate([g1 / total, g2 / total], axis=0)


def norm_mod(x2, g, sc1p, sh, seq, out_dtype=BF16, router=None):
    t, d = x2.shape
    tm = _tile(seq, 256)
    per_b = seq // tm
    row = pl.BlockSpec((tm, d), lambda i: (i, 0))
    vec = pl.BlockSpec((1, d), lambda i: (0, 0))
    mod = pl.BlockSpec((1, 1, d), lambda i: (i // per_b, 0, 0))
    g2 = g.reshape(1, d)
    if sc1p is None:
        return pl.pallas_call(
            _norm_kernel, grid=(t // tm,), in_specs=[row, vec], out_specs=row,
            out_shape=jax.ShapeDtypeStruct((t, d), out_dtype),
            compiler_params=_params("arbitrary"), name="final_norm")(x2, g2)
    if router is None:
        return pl.pallas_call(
            _norm_mod_kernel, grid=(t // tm,), in_specs=[row, vec, mod, mod], out_specs=row,
            out_shape=jax.ShapeDtypeStruct((t, d), out_dtype),
            compiler_params=_params("arbitrary"), name="norm_mod")(x2, g2, sc1p, sh)
    whi_t, wlo_t, bias = router
    ne = whi_t.shape[0]
    wspec = pl.BlockSpec((ne, d), lambda i: (0, 0))
    kspec = pl.BlockSpec((TOP_K, tm), lambda i: (0, i))
    return pl.pallas_call(
        _norm_router_kernel, grid=(t // tm,),
        in_specs=[row, vec, mod, mod, wspec, wspec, pl.BlockSpec((ne, 1), lambda i: (0, 0))],
        out_specs=[row, kspec, kspec],
        out_shape=[jax.ShapeDtypeStruct((t, d), out_dtype),
                   jax.ShapeDtypeStruct((TOP_K, t), jnp.int32),
                   jax.ShapeDtypeStruct((TOP_K, t), F32)],
        compiler_params=_params("arbitrary"), name="norm_router")(x2, g2, sc1p, sh, whi_t, wlo_t, bias)


def _mm_kernel(a_ref, w_ref, o_ref):
    o_ref[...] = _dot(a_ref[...], w_ref[...]).astype(o_ref.dtype)


def matmul(a, w, out_dtype, tm=1024, tn=1024):
    m, kd = a.shape
    n = w.shape[1]
    tm, tn = _tile(m, tm), _tile(n, tn)
    return pl.pallas_call(
        _mm_kernel,
        grid=(m // tm, n // tn),
        in_specs=[pl.BlockSpec((tm, kd), lambda i, j: (i, 0)),
                  pl.BlockSpec((kd, tn), lambda i, j: (0, j))],
        out_specs=pl.BlockSpec((tm, tn), lambda i, j: (i, j)),
        out_shape=jax.ShapeDtypeStruct((m, n), out_dtype),
        compiler_params=_params("parallel", "parallel"),
        name="matmul",
    )(a, w)


def _mm_res_kernel(a_ref, w_ref, x_ref, gate_ref, o_ref):
    o_ref[...] = x_ref[...] + gate_ref[0] * _dot(a_ref[...], w_ref[...])


def matmul_residual(a, w, x2, gate, seq, tm=1024, tn=1024):
    m, kd = a.shape
    n = w.shape[1]
    tm, tn = _tile(seq, tm), _tile(n, tn)
    per_b = seq // tm
    return pl.pallas_call(
        _mm_res_kernel,
        grid=(m // tm, n // tn),
        in_specs=[pl.BlockSpec((tm, kd), lambda i, j: (i, 0)),
                  pl.BlockSpec((kd, tn), lambda i, j: (0, j)),
                  pl.BlockSpec((tm, tn), lambda i, j: (i, j)),
                  pl.BlockSpec((1, 1, tn), lambda i, j: (i // per_b, 0, j))],
        out_specs=pl.BlockSpec((tm, tn), lambda i, j: (i, j)),
        out_shape=jax.ShapeDtypeStruct((m, n), F32),
        compiler_params=_params("parallel", "parallel"),
        name="matmul_residual",
    )(a, w, x2, gate)


def _rope_pair(r2, cos_ref, sin_ref):
    return r2 * cos_ref[...] + pltpu.roll(r2, MLA_ROPE, 1) * sin_ref[...]


def _with_ones_rows(vt):
    return jnp.concatenate([vt, jnp.ones((ONES_ROWS, vt.shape[1]), vt.dtype)], axis=0)


def _qup_kernel(cq_ref, g_ref, w_ref, cos_ref, sin_ref, o_ref, an_ref, *, scale):
    @pl.when(pl.program_id(1) == 0)
    def _():
        an_ref[...] = _rms(cq_ref[...], g_ref[...]).astype(BF16)

    an = an_ref[...]
    for hh in range(w_ref.shape[0]):
        res = _dot(an, w_ref[hh])
        roped = _rope_pair(res[:, MLA_NOPE:], cos_ref, sin_ref)
        q = jnp.concatenate([res[:, :MLA_NOPE], roped], axis=1) * scale
        o_ref[hh, 0] = q.T.astype(o_ref.dtype)


def mla_q_up(proj_a, g, w_q, cos_p, sin_p, seq, scale):
    t = proj_a.shape[0]
    heads = w_q.shape[0]
    hb = UP_HEADS_PER_STEP
    tm = _tile(seq, ATTN_TILE)
    per_b = seq // tm
    return pl.pallas_call(
        functools.partial(_qup_kernel, scale=scale),
        grid=(t // tm, heads // hb),
        in_specs=[pl.BlockSpec((tm, Q_LORA), lambda i, h: (i, 0)),
                  pl.BlockSpec((1, Q_LORA), lambda i, h: (0, 0)),
                  pl.BlockSpec((hb, Q_LORA, MLA_DK_PAD), lambda i, h: (h, 0, 0)),
                  pl.BlockSpec((tm, LANE), lambda i, h: (i % per_b, 0)),
                  pl.BlockSpec((tm, LANE), lambda i, h: (i % per_b, 0))],
        out_specs=pl.BlockSpec((hb, 1, MLA_DK_PAD, tm), lambda i, h: (h, i, 0, 0)),
        out_shape=jax.ShapeDtypeStruct((heads, t // tm, MLA_DK_PAD, tm), BF16),
        scratch_shapes=[pltpu.VMEM((tm, Q_LORA), BF16)],
        compiler_params=_params("parallel", "arbitrary"),
        name="mla_q_up",
    )(proj_a, g.reshape(1, Q_LORA), w_q, cos_p, sin_p)


def _kvup_kernel(ckv_ref, g_ref, w_ref, kpe_ref, cos_ref, sin_ref, k_ref, vt_ref, an_ref):
    @pl.when(pl.program_id(1) == 0)
    def _():
        an_ref[...] = _rms(ckv_ref[...], g_ref[...]).astype(BF16)

    an = an_ref[...]
    k_rot = _rope_pair(kpe_ref[...], cos_ref, sin_ref)
    hw = MLA_NOPE + MLA_V
    for hh in range(k_ref.shape[0]):
        res = _dot(an, w_ref[:, hh * hw:(hh + 1) * hw])
        k_ref[hh] = jnp.concatenate([res[:, :MLA_NOPE], k_rot], axis=1).astype(k_ref.dtype)
        vt_ref[hh, 0] = _with_ones_rows(res[:, MLA_NOPE:].T).astype(vt_ref.dtype)


def mla_kv_up(proj_a, g, w_kv, cos_p, sin_p, seq):
    t = proj_a.shape[0]
    hw = MLA_NOPE + MLA_V
    heads = w_kv.shape[1] // hw
    hb = UP_HEADS_PER_STEP
    tm = _tile(seq, ATTN_TILE)
    per_b = seq // tm
    return pl.pallas_call(
        _kvup_kernel,
        grid=(t // tm, heads // hb),
        in_specs=[pl.BlockSpec((tm, KV_LORA), lambda i, h: (i, Q_LORA // KV_LORA)),
                  pl.BlockSpec((1, KV_LORA), lambda i, h: (0, 0)),
                  pl.BlockSpec((KV_LORA, hb * hw), lambda i, h: (0, h)),
                  pl.BlockSpec((tm, LANE), lambda i, h: (i, (Q_LORA + KV_LORA) // LANE)),
                  pl.BlockSpec((tm, LANE), lambda i, h: (i % per_b, 0)),
                  pl.BlockSpec((tm, LANE), lambda i, h: (i % per_b, 0))],
        out_specs=[pl.BlockSpec((hb, tm, MLA_DK_PAD), lambda i, h: (h, i, 0)),
                   pl.BlockSpec((hb, 1, MLA_V + ONES_ROWS, tm), lambda i, h: (h, i, 0, 0))],
        out_shape=[jax.ShapeDtypeStruct((heads, t, MLA_DK_PAD), BF16),
                   jax.ShapeDtypeStruct((heads, t // tm, MLA_V + ONES_ROWS, tm), BF16)],
        scratch_shapes=[pltpu.VMEM((tm, KV_LORA), BF16)],
        compiler_params=_params("parallel", "arbitrary"),
        name="mla_kv_up",
    )(proj_a, g.reshape(1, KV_LORA), w_kv, proj_a, cos_p, sin_p)


def _softmax_first(s, vt, m_ref, acc_ref):
    m = jnp.max(s, axis=0, keepdims=True)
    p = jnp.exp(s - m).astype(vt.dtype)
    m_ref[...] = m
    acc_ref[...] = _dot(vt, p)


def _softmax_step(s, vt, m_ref, acc_ref):
    m_prev = m_ref[...]
    m_new = jnp.maximum(m_prev, jnp.max(s, axis=0, keepdims=True))
    alpha = jnp.exp(m_prev - m_new)
    p = jnp.exp(s - m_new).astype(vt.dtype)
    acc_ref[...] = alpha * acc_ref[...] + _dot(vt, p)
    m_ref[...] = m_new


def _softmax_finish(acc_ref, dv):
    acc = acc_ref[...]
    return (acc[:dv, :] / acc[dv:dv + 1, :]).T


def _causal_mask_t(s):
    key = lax.broadcasted_iota(jnp.int32, s.shape, 0)
    qry = lax.broadcasted_iota(jnp.int32, s.shape, 1)
    return jnp.where(key <= qry, s, NEG_INF)


def _mla_attn_kernel(qt_ref, k_ref, vt_ref, o_ref, m_ref, acc_ref, *, tq, dv):
    i = pl.program_id(2)
    n_h = qt_ref.shape[0]
    qts = [qt_ref[h, 0] for h in range(n_h)]

    def scores(h, j):
        off = pl.multiple_of(j * tq, tq)
        return _dot(k_ref[h, pl.ds(off, tq), :], qts[h])

    for h in range(n_h):
        _softmax_first(_causal_mask_t(scores(h, i)), vt_ref[h, i], m_ref.at[h], acc_ref.at[h])

    def body(j, carry):
        for h in range(n_h):
            _softmax_step(scores(h, j), vt_ref[h, j], m_ref.at[h], acc_ref.at[h])
        return carry

    lax.fori_loop(0, i, body, 0)
    for h in range(n_h):
        o_ref[:, h * dv:(h + 1) * dv] = _softmax_finish(acc_ref.at[h], dv).astype(o_ref.dtype)


def mla_attention(qt, k, vt, batch, seq, out_dtype):
    heads, _, dk, tq = qt.shape
    t = k.shape[1]
    dvx = vt.shape[2]
    dv = dvx - ONES_ROWS
    nq = seq // tq
    hb = ATTN_HEADS_PER_STEP
    return pl.pallas_call(
        functools.partial(_mla_attn_kernel, tq=tq, dv=dv),
        grid=(batch, heads // hb, nq),
        in_specs=[pl.BlockSpec((hb, 1, dk, tq), lambda b, h, i: (h, b * nq + i, 0, 0)),
                  pl.BlockSpec((hb, seq, dk), lambda b, h, i: (h, b, 0)),
                  pl.BlockSpec((hb, nq, dvx, tq), lambda b, h, i: (h, b, 0, 0))],
        out_specs=pl.BlockSpec((tq, hb * dv), lambda b, h, i: (b * nq + i, h)),
        out_shape=jax.ShapeDtypeStruct((t, heads * dv), out_dtype),
        scratch_shapes=[pltpu.VMEM((hb, 1, tq), F32), pltpu.VMEM((hb, dvx, tq), F32)],
        compiler_params=_params("parallel", "parallel", "arbitrary"),
        name="mla_attention",
    )(qt, k, vt)


def _conv_kernel(b_ref, c_ref, h_ref, w_ref, o_ref, carry_ref, *, per_b):
    i = pl.program_id(0)

    @pl.when(i % per_b == 0)
    def _():
        carry_ref[...] = jnp.zeros_like(carry_ref)

    tm = o_ref.shape[0]
    w0, w1, w2 = w_ref[0:1, :], w_ref[1:2, :], w_ref[2:3, :]
    u = c_ref[...] * h_ref[...]
    y = w0 * pltpu.roll(u, 2, 0) + w1 * pltpu.roll(u, 1, 0) + w2 * u
    o_ref[...] = (b_ref[...] * y).astype(o_ref.dtype)
    u8 = u[0:SUBLANE, :]
    tail = carry_ref[...]
    r8 = lax.broadcasted_iota(jnp.int32, u8.shape, 0)
    p1 = jnp.where(r8 < 1, pltpu.roll(tail, 1, 0), pltpu.roll(u8, 1, 0))
    p2 = jnp.where(r8 < 2, pltpu.roll(tail, 2, 0), pltpu.roll(u8, 2, 0))
    y8 = w0 * p2 + w1 * p1 + w2 * u8
    o_ref[0:SUBLANE, :] = (b_ref[0:SUBLANE, :] * y8).astype(o_ref.dtype)
    carry_ref[...] = u[tm - SUBLANE:tm, :]


def conv_mixer(bch, conv_w, seq, out_dtype):
    t = bch.shape[0]
    ch = conv_w.shape[1]
    tm = _tile(seq, 256)
    per_b = seq // tm
    return pl.pallas_call(
        functools.partial(_conv_kernel, per_b=per_b),
        grid=(t // tm,),
        in_specs=[pl.BlockSpec((tm, ch), lambda i: (i, 0)),
                  pl.BlockSpec((tm, ch), lambda i: (i, 1)),
                  pl.BlockSpec((tm, ch), lambda i: (i, 2)),
                  pl.BlockSpec((CONV_W, ch), lambda i: (0, 0))],
        out_specs=pl.BlockSpec((tm, ch), lambda i: (i, 0)),
        out_shape=jax.ShapeDtypeStruct((t, ch), out_dtype),
        scratch_shapes=[pltpu.VMEM((SUBLANE, ch), F32)],
        compiler_params=_params("arbitrary"),
        name="conv_mixer",
    )(bch, bch, bch, conv_w)


def _moba_prep_kernel(q_ref, k_ref, v_ref, cos_ref, sin_ref, qo_ref, ko_ref, vo_ref, km_ref):
    heads = ko_ref.shape[0]
    cos, sin = cos_ref[...], sin_ref[...]
    n_sub = km_ref.shape[0]
    means = [[] for _ in range(n_sub)]
    for h in range(heads):
        sl = slice(h * MOBA_HD, (h + 1) * MOBA_HD)
        qh = q_ref[:, sl]
        kh = k_ref[:, sl]
        qr = qh * cos + pltpu.roll(qh, MOBA_HD // 2, 1) * sin
        qo_ref[h, 0] = qr.T.astype(qo_ref.dtype)
        kr = kh * cos + pltpu.roll(kh, MOBA_HD // 2, 1) * sin
        ko_ref[h] = kr.astype(ko_ref.dtype)
        vo_ref[h, 0] = _with_ones_rows(v_ref[:, sl].T).astype(vo_ref.dtype)
        for s in range(n_sub):
            means[s].append(jnp.mean(kr[s * MOBA_BLOCK:(s + 1) * MOBA_BLOCK], axis=0, keepdims=True))
    for s in range(n_sub):
        km_ref[s] = jnp.concatenate(means[s], axis=0)


def moba_prep(qkv, cos_f, sin_s, seq):
    t = qkv.shape[0]
    width = qkv.shape[1] // 3
    heads = width // MOBA_HD
    tm = 2 * MOBA_BLOCK
    assert seq % tm == 0
    per_b = seq // tm
    nt = t // tm
    return pl.pallas_call(
        _moba_prep_kernel,
        grid=(nt,),
        in_specs=[pl.BlockSpec((tm, width), lambda i: (i, 0)),
                  pl.BlockSpec((tm, width), lambda i: (i, 1)),
                  pl.BlockSpec((tm, width), lambda i: (i, 2)),
                  pl.BlockSpec((tm, MOBA_HD), lambda i: (i % per_b, 0)),
                  pl.BlockSpec((tm, MOBA_HD), lambda i: (i % per_b, 0))],
        out_specs=[pl.BlockSpec((heads, 1, MOBA_HD, tm), lambda i: (0, i, 0, 0)),
                   pl.BlockSpec((heads, tm, MOBA_HD), lambda i: (0, i, 0)),
                   pl.BlockSpec((heads, 1, MOBA_HD + ONES_ROWS, tm), lambda i: (0, i, 0, 0)),
                   pl.BlockSpec((2, heads, MOBA_HD), lambda i: (i, 0, 0))],
        out_shape=[jax.ShapeDtypeStruct((heads, nt, MOBA_HD, tm), BF16),
                   jax.ShapeDtypeStruct((heads, t, MOBA_HD), BF16),
                   jax.ShapeDtypeStruct((heads, nt, MOBA_HD + ONES_ROWS, tm), BF16),
                   jax.ShapeDtypeStruct((2 * nt, heads, MOBA_HD), F32)],
        compiler_params=_params("arbitrary"),
        name="moba_prep",
    )(qkv, qkv, qkv, cos_f, sin_s)


def _moba_attn_kernel(qt_ref, k_ref, vt_ref, km_ref, o_ref, m_ref, acc_ref, sel_ref, *, scale, dv):
    i = pl.program_id(2)
    n_h = qt_ref.shape[0]
    n_blk = km_ref.shape[2]
    tq = qt_ref.shape[3]
    blk_w = MOBA_BLOCK
    blk = lax.broadcasted_iota(jnp.int32, (n_blk, tq), 0)
    own = 2 * i + (lax.broadcasted_iota(jnp.int32, (n_blk, tq), 1) >= blk_w).astype(jnp.int32)
    past = blk < own
    qts = []
    for h in range(n_h):
        qtf = qt_ref[h, 0].astype(F32)
        g = _dot(km_ref[0, h], qtf, precision=lax.Precision.HIGHEST)
        g = jnp.where(past, g, NEG_INF)
        sel = jnp.zeros(g.shape, F32)
        for _ in range(min(MOBA_TOPK, n_blk)):
            mx = jnp.max(g, axis=0, keepdims=True)
            first = jnp.min(jnp.where(g == mx, blk, n_blk), axis=0, keepdims=True)
            hit = blk == first
            sel = jnp.where(hit, 1.0, sel)
            g = jnp.where(hit, NEG_INF, g)
        sel_ref[h] = jnp.where(past, sel, 0.0)
        qts.append((qtf * scale).astype(k_ref.dtype))

    def scores(h, p):
        off = pl.multiple_of(p * tq, tq)
        return _dot(k_ref[h, pl.ds(off, tq), :], qts[h])

    def chosen(h, n):
        return sel_ref[h, pl.ds(n, 1), :] > 0.5

    key = lax.broadcasted_iota(jnp.int32, (blk_w, tq), 0)
    qry = lax.broadcasted_iota(jnp.int32, (blk_w, tq), 1)
    bot_ok = key <= qry - blk_w
    for h in range(n_h):
        s = scores(h, i)
        chosen_lim = jnp.where(chosen(h, 2 * i), blk_w, -1)
        top_ok = key <= jnp.where(qry < blk_w, qry, chosen_lim)
        s = jnp.concatenate([jnp.where(top_ok, s[:blk_w], NEG_INF),
                             jnp.where(bot_ok, s[blk_w:], NEG_INF)], axis=0)
        _softmax_first(s, vt_ref[h, i], m_ref.at[h], acc_ref.at[h])

    def body(p, carry):
        for h in range(n_h):
            s = scores(h, p)
            s = jnp.concatenate([jnp.where(chosen(h, 2 * p), s[:blk_w], NEG_INF),
                                 jnp.where(chosen(h, 2 * p + 1), s[blk_w:], NEG_INF)], axis=0)
            _softmax_step(s, vt_ref[h, p], m_ref.at[h], acc_ref.at[h])
        return carry

    lax.fori_loop(0, i, body, 0)
    for h in range(n_h):
        o_ref[:, h * dv:(h + 1) * dv] = _softmax_finish(acc_ref.at[h], dv).astype(o_ref.dtype)


def moba_attention(qt, k, vt, k_mean, batch, seq, out_dtype):
    heads, _, hd, tq = qt.shape
    t = k.shape[1]
    dvx = vt.shape[2]
    dv = dvx - ONES_ROWS
    n_blk = seq // MOBA_BLOCK
    nq = seq // tq
    hb = ATTN_HEADS_PER_STEP
    return pl.pallas_call(
        functools.partial(_moba_attn_kernel, scale=hd ** -0.5, dv=dv),
        grid=(batch, heads // hb, nq),
        in_specs=[pl.BlockSpec((hb, 1, hd, tq), lambda b, h, i: (h, b * nq + i, 0, 0)),
                  pl.BlockSpec((hb, seq, hd), lambda b, h, i: (h, b, 0)),
                  pl.BlockSpec((hb, nq, dvx, tq), lambda b, h, i: (h, b, 0, 0)),
                  pl.BlockSpec((1, hb, n_blk, hd), lambda b, h, i: (b, h, 0, 0))],
        out_specs=pl.BlockSpec((tq, hb * dv), lambda b, h, i: (b * nq + i, h)),
        out_shape=jax.ShapeDtypeStruct((t, heads * dv), out_dtype),
        scratch_shapes=[pltpu.VMEM((hb, 1, tq), F32), pltpu.VMEM((hb, dvx, tq), F32),
                        pltpu.VMEM((hb, n_blk, tq), F32)],
        compiler_params=_params("parallel", "parallel", "arbitrary"),
        name="moba_attention",
    )(qt, k, vt, k_mean)


def _group_norm_kernel(a_ref, b_ref, c_ref, ga_ref, gb_ref, gc_ref, o_ref):
    wa, wb = a_ref.shape[1], b_ref.shape[1]
    o_ref[:, 0:wa] = _rms(a_ref[...].astype(F32), ga_ref[...]).astype(o_ref.dtype)
    o_ref[:, wa:wa + wb] = _rms(b_ref[...].astype(F32), gb_ref[...]).astype(o_ref.dtype)
    o_ref[:, wa + wb:] = _rms(c_ref[...].astype(F32), gc_ref[...]).astype(o_ref.dtype)


def group_norm(y_a, y_b, y_c, g):
    t = y_a.shape[0]
    wa, wb, wc = y_a.shape[1], y_b.shape[1], y_c.shape[1]
    tm = _tile(t, 256)
    ga, gb, gc = g[:wa].reshape(1, wa), g[wa:wa + wb].reshape(1, wb), g[wa + wb:].reshape(1, wc)
    rows = lambda w: pl.BlockSpec((tm, w), lambda i: (i, 0))
    vec = lambda w: pl.BlockSpec((1, w), lambda i: (0, 0))
    return pl.pallas_call(
        _group_norm_kernel,
        grid=(t // tm,),
        in_specs=[rows(wa), rows(wb), rows(wc), vec(wa), vec(wb), vec(wc)],
        out_specs=rows(wa + wb + wc),
        out_shape=jax.ShapeDtypeStruct((t, wa + wb + wc), BF16),
        compiler_params=_params("arbitrary"),
        name="group_norm",
    )(y_a, y_b, y_c, ga, gb, gc)


def _ffn_up_kernel(bexp_ref, first_ref, nused_ref, x_ref, wg_ref, wu_ref, a_ref, wg_bf, wu_bf):
    b = pl.program_id(1)
    used = b < nused_ref[0]

    @pl.when(used & (first_ref[b] == 1))
    def _():
        wg_bf[...] = wg_ref[0, 0].astype(BF16)
        wu_bf[...] = wu_ref[0, 0].astype(BF16)

    @pl.when(used)
    def _():
        x = x_ref[...]
        gate = _dot(x, wg_bf[...])
        up = _dot(x, wu_bf[...])
        a_ref[...] = (gate * jax.nn.sigmoid(gate) * up).astype(a_ref.dtype)

    @pl.when(jnp.logical_not(used))
    def _():
        a_ref[...] = jnp.zeros_like(a_ref)


def _pack_bf16_pair(lo, hi):
    lo_bits = lax.bitcast_convert_type(lo.astype(BF16).astype(F32), PACKED) >> 16
    hi_bits = lax.bitcast_convert_type(hi.astype(BF16).astype(F32), PACKED) & jnp.uint32(0xFFFF0000)
    return hi_bits | lo_bits


def _unpack_bf16_pair(packed):
    lo = lax.bitcast_convert_type(packed << 16, F32)
    hi = lax.bitcast_convert_type(packed & jnp.uint32(0xFFFF0000), F32)
    return lo, hi


def _ffn_down_kernel(bexp_ref, first_ref, nused_ref, a_ref, wd_ref, o_ref, wd_bf):
    b = pl.program_id(1)
    used = b < nused_ref[0]

    @pl.when(used & (first_ref[b] == 1))
    def _():
        wd_bf[...] = wd_ref[0, 0].astype(BF16)

    @pl.when(used)
    def _():
        y = _dot(a_ref[...], wd_bf[...])
        half = y.shape[0] // 2
        o_ref[...] = _pack_bf16_pair(y[:half], y[half:])

    @pl.when(jnp.logical_not(used))
    def _():
        o_ref[...] = jnp.zeros_like(o_ref)


def expert_ffn(xs, block_exp, block_first, n_used, w_gate, w_up, w_down, layer):
    n_slots, d = xs.shape
    ff = w_gate.shape[3]
    fh, dh = ff // FF_SPLIT, d // FF_SPLIT
    n_blocks = n_slots // MOE_BLOCK
    up_spec = pltpu.PrefetchScalarGridSpec(
        num_scalar_prefetch=3,
        grid=(FF_SPLIT, n_blocks),
        in_specs=[pl.BlockSpec((MOE_BLOCK, d), lambda j, b, be, fi, nu: (b, 0)),
                  pl.BlockSpec((1, 1, d, fh), lambda j, b, be, fi, nu: (layer, be[b], 0, j)),
                  pl.BlockSpec((1, 1, d, fh), lambda j, b, be, fi, nu: (layer, be[b], 0, j))],
        out_specs=pl.BlockSpec((MOE_BLOCK, fh), lambda j, b, be, fi, nu: (b, j)),
        scratch_shapes=[pltpu.VMEM((d, fh), BF16), pltpu.VMEM((d, fh), BF16)],
    )
    act = pl.pallas_call(
        _ffn_up_kernel,
        grid_spec=up_spec,
        out_shape=jax.ShapeDtypeStruct((n_slots, ff), BF16),
        compiler_params=_params("arbitrary", "arbitrary"),
        name="expert_ffn_up",
    )(block_exp, block_first, n_used, xs, w_gate, w_up)
    down_spec = pltpu.PrefetchScalarGridSpec(
        num_scalar_prefetch=3,
        grid=(FF_SPLIT, n_blocks),
        in_specs=[pl.BlockSpec((MOE_BLOCK, ff), lambda j, b, be, fi, nu: (b, 0)),
                  pl.BlockSpec((1, 1, ff, dh), lambda j, b, be, fi, nu: (layer, be[b], 0, j))],
        out_specs=pl.BlockSpec((MOE_BLOCK // 2, dh), lambda j, b, be, fi, nu: (b, j)),
        scratch_shapes=[pltpu.VMEM((ff, dh), BF16)],
    )
    return pl.pallas_call(
        _ffn_down_kernel,
        grid_spec=down_spec,
        out_shape=jax.ShapeDtypeStruct((n_slots // 2, d), PACKED),
        compiler_params=_params("arbitrary", "arbitrary"),
        name="expert_ffn_down",
    )(block_exp, block_first, n_used, act, w_down)


def _start_row_copies(idx_ref, base, n_rows, src_ref, dst_ref, sem):
    def issue(r, carry):
        row = idx_ref[base + r]
        pltpu.make_async_copy(src_ref.at[pl.ds(row, 1)], dst_ref.at[pl.ds(r, 1)], sem).start()
        return carry

    lax.fori_loop(0, n_rows, issue, 0, unroll=8)


def _wait_row_copies(n_rows, src_ref, dst_ref, sem):
    pltpu.make_async_copy(src_ref.at[pl.ds(0, n_rows)], dst_ref, sem).wait()


def _gather_cast_kernel(idx_ref, nused_ref, src_ref, o_ref, buf, sem):
    b = pl.program_id(0)
    rows = o_ref.shape[0]
    n_used = nused_ref[0]

    @pl.when((b == 0) & (n_used > 0))
    def _():
        _start_row_copies(idx_ref, 0, rows, src_ref, buf.at[0], sem.at[0])

    @pl.when(b + 1 < n_used)
    def _():
        nxt = (b + 1) % 2
        _start_row_copies(idx_ref, (b + 1) * rows, rows, src_ref, buf.at[nxt], sem.at[nxt])

    @pl.when(b < n_used)
    def _():
        cur = b % 2
        _wait_row_copies(rows, src_ref, buf.at[cur], sem.at[cur])
        o_ref[...] = buf[cur].astype(o_ref.dtype)

    @pl.when(b >= n_used)
    def _():
        o_ref[...] = jnp.zeros_like(o_ref)


def gather_rows_cast(src, idx, n_used_tiles, out_dtype):
    d = src.shape[1]
    m = idx.shape[0]
    rows = GATHER_ROWS
    grid_spec = pltpu.PrefetchScalarGridSpec(
        num_scalar_prefetch=2,
        grid=(m // rows,),
        in_specs=[pl.BlockSpec(memory_space=pl.ANY)],
        out_specs=pl.BlockSpec((rows, d), lambda b, ix, nu: (b, 0)),
        scratch_shapes=[pltpu.VMEM((2, rows, d), src.dtype), pltpu.SemaphoreType.DMA((2,))],
    )
    return pl.pallas_call(
        _gather_cast_kernel,
        grid_spec=grid_spec,
        out_shape=jax.ShapeDtypeStruct((m, d), out_dtype),
        compiler_params=_params("arbitrary"),
        name="gather_rows_cast",
    )(idx, n_used_tiles, src)


def _combine_kernel(pos_ref, x_ref, gate_ref, w_ref, half_ref, ys_ref, o_ref, ybuf, sem, *, n_tok):
    i = pl.program_id(0)
    n_tiles = pl.num_programs(0)
    tm = x_ref.shape[0]

    def start_tile(tile, slot):
        for kk in range(TOP_K):
            _start_row_copies(pos_ref, kk * n_tok + tile * tm, tm, ys_ref, ybuf.at[slot, kk],
                              sem.at[slot])

    @pl.when(i == 0)
    def _():
        start_tile(0, 0)

    @pl.when(i + 1 < n_tiles)
    def _():
        start_tile(i + 1, (i + 1) % 2)

    cur = i % 2
    for kk in range(TOP_K):
        _wait_row_copies(tm, ys_ref, ybuf.at[cur, kk], sem.at[cur])
    w = w_ref[...]
    upper = half_ref[...] > 0.5
    acc = None
    for kk in range(TOP_K):
        lo, hi = _unpack_bf16_pair(ybuf[cur, kk])
        term = w[:, kk:kk + 1] * jnp.where(upper[:, kk:kk + 1], hi, lo)
        acc = term if acc is None else acc + term
    o_ref[...] = x_ref[...] + gate_ref[0] * acc


def moe_combine(x2, gate, ys, pos, w_tok, half_tok, seq):
    t, d = x2.shape
    tm = _tile(seq, GATHER_ROWS)
    per_b = seq // tm
    grid_spec = pltpu.PrefetchScalarGridSpec(
        num_scalar_prefetch=1,
        grid=(t // tm,),
        in_specs=[pl.BlockSpec((tm, d), lambda i, ps: (i, 0)),
                  pl.BlockSpec((1, 1, d), lambda i, ps: (i // per_b, 0, 0)),
                  pl.BlockSpec((tm, TOP_K), lambda i, ps: (i, 0)),
                  pl.BlockSpec((tm, TOP_K), lambda i, ps: (i, 0)),
                  pl.BlockSpec(memory_space=pl.ANY)],
        out_specs=pl.BlockSpec((tm, d), lambda i, ps: (i, 0)),
        scratch_shapes=[pltpu.VMEM((2, TOP_K, tm, d), ys.dtype), pltpu.SemaphoreType.DMA((2,))],
    )
    return pl.pallas_call(
        functools.partial(_combine_kernel, n_tok=t),
        grid_spec=grid_spec,
        out_shape=jax.ShapeDtypeStruct((t, d), F32),
        compiler_params=_params("arbitrary"),
        name="moe_combine",
    )(pos, x2, gate, w_tok, half_tok, ys)


def _group_by_expert(e_idx):
    n_tok = e_idx.shape[1]
    n_asg = n_tok * TOP_K
    flat_e = e_idx.reshape(n_asg)
    onehot = (flat_e[:, None] == jnp.arange(N_EXPERTS)[None, :]).astype(jnp.int32)
    running = jnp.cumsum(onehot, axis=0)
    rank = jnp.sum(running * onehot, axis=1) - 1
    sizes = running[-1]
    padded = (sizes + MOE_BLOCK - 1) // MOE_BLOCK * MOE_BLOCK
    pad_end = jnp.cumsum(padded)
    pad_start = pad_end - padded
    dest = (jnp.sum(pad_start[None, :] * onehot, axis=1) + rank).astype(jnp.int32)
    n_blocks = -(-n_asg // MOE_BLOCK) + N_EXPERTS
    n_slots = n_blocks * MOE_BLOCK
    slot_tok = (jnp.arange(n_slots, dtype=jnp.int32) % n_tok).at[dest].set(
        jnp.arange(n_asg, dtype=jnp.int32) % n_tok, mode="promise_in_bounds")
    block_start = jnp.arange(n_blocks, dtype=jnp.int32) * MOE_BLOCK
    block_exp = jnp.minimum(jnp.sum(block_start[:, None] >= pad_end[None, :], axis=1),
                            N_EXPERTS - 1).astype(jnp.int32)
    n_used = (pad_end[-1] // MOE_BLOCK).astype(jnp.int32)
    block_exp = jnp.where(jnp.arange(n_blocks) < n_used, block_exp, block_exp[n_used - 1])
    prev_exp = jnp.concatenate([jnp.full((1,), -1, jnp.int32), block_exp[:-1]])
    block_first = (block_exp != prev_exp).astype(jnp.int32)
    return dest.reshape(TOP_K, n_tok), slot_tok, block_exp, block_first, n_used.reshape(1)


def _rot_cols(w):
    half = w.shape[-1] // 2
    return jnp.concatenate([-w[..., half:], w[..., :half]], axis=-1)


def _prep_layer(w_in, w_uq, w_ukv, w_out):
    a_end = Q_LORA + KV_LORA + MLA_ROPE
    b_end = a_end + 3 * CONV_CH
    w_kpe = w_in[:, Q_LORA + KV_LORA:a_end]
    w_a = jnp.concatenate([w_in[:, :a_end], _rot_cols(w_kpe)], axis=1).astype(BF16)
    w_b = w_in[:, a_end:b_end].astype(BF16)
    w_c = w_in[:, b_end:].astype(BF16)
    wq = w_uq.reshape(Q_LORA, MLA_HEADS, MLA_NOPE + MLA_ROPE)
    wq_rope = wq[..., MLA_NOPE:]
    wq = jnp.concatenate([wq, _rot_cols(wq_rope)], axis=-1).transpose(1, 0, 2).astype(BF16)
    return dict(w_a=w_a, w_b=w_b, w_c=w_c, w_q=wq, w_kv=w_ukv.astype(BF16),
                w_out=w_out.astype(BF16))


def _rope_tables(seq, dim):
    inv = 1.0 / (ROPE_THETA ** (jnp.arange(0, dim, 2, dtype=F32) / dim))
    ang = jnp.arange(seq, dtype=F32)[:, None] * inv[None, :]
    return jnp.cos(ang), jnp.sin(ang)


def kernel(x, c, w_mod, mod_table, mix_norm_g, w_in, q_norm_g, kv_norm_g, w_uq, w_ukv, conv_w,
           group_norm_g, w_out, ffn_norm_g, w_router, router_bias, w_gate, w_up, w_down,
           final_norm_g):
    batch, seq, d = x.shape
    depth = w_in.shape[0]
    t = batch * seq
    x2 = x.reshape(t, d)

    cos_a, sin_a = _rope_tables(seq, MLA_ROPE)
    zeros_a = jnp.zeros_like(cos_a)
    cos_p = jnp.concatenate([cos_a, cos_a, zeros_a, zeros_a], axis=1)
    sin_p = jnp.concatenate([sin_a, sin_a, zeros_a, zeros_a], axis=1)
    cos_b, sin_b = _rope_tables(seq, MOBA_HD)
    cos_f = jnp.concatenate([cos_b, cos_b], axis=1)
    sin_s = jnp.concatenate([-sin_b, sin_b], axis=1)

    c_pad = jnp.zeros((SUBLANE, d), F32).at[:batch].set(c)
    mod_shared = mod_matmul(c_pad, w_mod)[:batch].reshape(batch, N_MOD, d)

    wr_t = w_router.T
    wr_hi = wr_t.astype(BF16)
    wr_lo = (wr_t - wr_hi.astype(F32)).astype(BF16)
    router = (wr_hi, wr_lo, router_bias.astype(F32).reshape(N_EXPERTS, 1))
    mla_scale = (MLA_NOPE + MLA_ROPE) ** -0.5

    for l in range(depth):
        p = _prep_layer(w_in[l], w_uq[l], w_ukv[l], w_out[l])
        mod = mod_shared + mod_table[l][None]
        sh1, sc1, g1, sh2, sc2, g2 = [mod[:, i][:, None, :] for i in range(N_MOD)]

        h = norm_mod(x2, mix_norm_g[l], 1.0 + sc1, sh1, seq)
        proj_a = matmul(h, p["w_a"], F32, tm=512)
        bch = matmul(h, p["w_b"], F32)
        qkv = matmul(h, p["w_c"], F32)

        q_a = mla_q_up(proj_a, q_norm_g[l], p["w_q"], cos_p, sin_p, seq, mla_scale)
        k_a, v_a = mla_kv_up(proj_a, kv_norm_g[l], p["w_kv"], cos_p, sin_p, seq)
        y_a = mla_attention(q_a, k_a, v_a, batch, seq, BF16)

        y_b = conv_mixer(bch, conv_w[l], seq, F32)

        q_c, k_c, v_c, k_mean = moba_prep(qkv, cos_f, sin_s, seq)
        n_blk = seq // MOBA_BLOCK
        k_mean = k_mean.reshape(batch, n_blk, MOBA_HEADS, MOBA_HD).transpose(0, 2, 1, 3)
        y_c = moba_attention(q_c, k_c, v_c, k_mean, batch, seq, BF16)

        y = group_norm(y_a, y_b, y_c, group_norm_g[l])
        x2 = matmul_residual(y, p["w_out"], x2, g1, seq)

        h2, e_idx, gates = norm_mod(x2, ffn_norm_g[l], 1.0 + sc2, sh2, seq, out_dtype=F32,
                                    router=router)
        pos, slot_tok, block_exp, block_first, n_used = _group_by_expert(e_idx)
        xs = gather_rows_cast(h2, slot_tok, n_used * (MOE_BLOCK // GATHER_ROWS), BF16)
        ys = expert_ffn(xs, block_exp, block_first, n_used, w_gate, w_up, w_down, l)
        sub = pos % MOE_BLOCK
        ys_row = (pos // MOE_BLOCK) * (MOE_BLOCK // 2) + sub % (MOE_BLOCK // 2)
        ys_half = (sub // (MOE_BLOCK // 2)).astype(F32)
        x2 = moe_combine(x2, g2, ys, ys_row.reshape(-1), gates.T, ys_half.T, seq)

    out = norm_mod(x2, final_norm_g, None, None, seq, out_dtype=F32)
    return out.reshape(batch, seq, d)
```

```python
import functools

import jax
import jax.numpy as jnp
from jax import lax
from jax.experimental import pallas as pl
from jax.experimental.pallas import tpu as pltpu

MLA_HEADS = 16
MLA_NOPE = 128
MLA_ROPE = 64
MLA_V = 128
Q_LORA = 1024
KV_LORA = 512
MLA_DK_PAD = 256
CONV_CH = 1024
CONV_W = 3
MOBA_HEADS = 8
MOBA_HD = 128
MOBA_BLOCK = 256
MOBA_TOPK = 3
ROPE_THETA = 10000.0
EPS = 1e-6
N_MOD = 6
N_EXPERTS = 16
N_GROUPS = 4
TOP_K = 2
MOE_BLOCK = 512
GATHER_ROWS = 256
FF_SPLIT = 2
UP_HEADS_PER_STEP = 4
ATTN_HEADS_PER_STEP = 4
ATTN_TILE = 512
ONES_ROWS = 16

V7X_VMEM_LIMIT_BYTES = 56 * 1024 * 1024
LANE = 128
SUBLANE = 8

F32 = jnp.float32
BF16 = jnp.bfloat16
PACKED = jnp.uint32
NEG_INF = float("-inf")


def _params(*sem):
    return pltpu.CompilerParams(dimension_semantics=sem, vmem_limit_bytes=V7X_VMEM_LIMIT_BYTES)


def _dot(a, b, precision=None):
    return jnp.dot(a, b, preferred_element_type=F32, precision=precision)


def _dot_nt(a, b):
    return lax.dot_general(a, b, (((1,), (1,)), ((), ())), preferred_element_type=F32)


def _tile(dim, want):
    return want if dim % want == 0 else dim


def _mod_kernel(c_ref, w0_ref, w1_ref, o_ref):
    @pl.when(pl.program_id(0) == 0)
    def _():
        o_ref[...] = jnp.zeros_like(o_ref)

    c = c_ref[...]
    a = (c * jax.nn.sigmoid(c)).astype(BF16)
    half = w0_ref.shape[1]
    o_ref[:, :half] += _dot(a, w0_ref[...].astype(BF16))
    o_ref[:, half:] += _dot(a, w1_ref[...].astype(BF16))


def mod_matmul(c_pad, w_mod):
    rows, d = c_pad.shape
    n = w_mod.shape[1]
    tk = _tile(d, LANE)
    return pl.pallas_call(
        _mod_kernel,
        grid=(d // tk,),
        in_specs=[pl.BlockSpec((rows, tk), lambda k: (0, k)),
                  pl.BlockSpec((tk, n // 2), lambda k: (k, 0)),
                  pl.BlockSpec((tk, n // 2), lambda k: (k, 1))],
        out_specs=pl.BlockSpec((rows, n), lambda k: (0, 0)),
        out_shape=jax.ShapeDtypeStruct((rows, n), F32),
        compiler_params=_params("arbitrary"),
        name="mod_matmul",
    )(c_pad, w_mod, w_mod)


def _rms(x, g):
    return x * lax.rsqrt(jnp.mean(x * x, axis=-1, keepdims=True) + EPS) * g


def _norm_mod_kernel(x_ref, g_ref, sc_ref, sh_ref, o_ref):
    y = _rms(x_ref[...], g_ref[...])
    o_ref[...] = (y * sc_ref[0] + sh_ref[0]).astype(o_ref.dtype)


def _norm_kernel(x_ref, g_ref, o_ref):
    o_ref[...] = _rms(x_ref[...], g_ref[...]).astype(o_ref.dtype)


def _top2_sum(a, b, c, d):
    hi1, lo1 = jnp.maximum(a, b), jnp.minimum(a, b)
    hi2, lo2 = jnp.maximum(c, d), jnp.minimum(c, d)
    return jnp.maximum(hi1, hi2) + jnp.maximum(jnp.minimum(hi1, hi2), jnp.maximum(lo1, lo2))


def _norm_router_kernel(x_ref, g_ref, sc_ref, sh_ref, whi_ref, wlo_ref, bias_ref,
                        o_ref, e_ref, gate_ref):
    h = _rms(x_ref[...], g_ref[...]) * sc_ref[0] + sh_ref[0]
    hi = h.astype(BF16)
    o_ref[...] = h.astype(o_ref.dtype)
    lo = (h - hi.astype(F32)).astype(BF16)
    whi, wlo = whi_ref[...], wlo_ref[...]
    logits = _dot_nt(whi, hi) + _dot_nt(wlo, hi) + _dot_nt(whi, lo)
    scores = jax.nn.sigmoid(logits)
    biased = scores + bias_ref[...]
    per_grp = N_EXPERTS // N_GROUPS
    b_rows = [biased[e:e + 1, :] for e in range(N_EXPERTS)]
    s_rows = [scores[e:e + 1, :] for e in range(N_EXPERTS)]
    grp_scores = [_top2_sum(*b_rows[g * per_grp:(g + 1) * per_grp]) for g in range(N_GROUPS)]
    best, grp = grp_scores[0], jnp.zeros(grp_scores[0].shape, jnp.int32)
    for g in range(1, N_GROUPS):
        better = grp_scores[g] > best
        grp = jnp.where(better, g, grp)
        best = jnp.where(better, grp_scores[g], best)
    b4, s4 = b_rows[:per_grp], s_rows[:per_grp]
    for g in range(1, N_GROUPS):
        in_g = grp == g
        b4 = [jnp.where(in_g, b_rows[g * per_grp + j], b4[j]) for j in range(per_grp)]
        s4 = [jnp.where(in_g, s_rows[g * per_grp + j], s4[j]) for j in range(per_grp)]
    v1, i1 = b4[0], jnp.zeros(grp.shape, jnp.int32)
    for j in range(1, per_grp):
        better = b4[j] > v1
        i1 = jnp.where(better, j, i1)
        v1 = jnp.where(better, b4[j], v1)
    v2, i2 = jnp.full(v1.shape, NEG_INF, F32), jnp.zeros(grp.shape, jnp.int32)
    for j in range(per_grp):
        better = (i1 != j) & (b4[j] > v2)
        i2 = jnp.where(better, j, i2)
        v2 = jnp.where(better, b4[j], v2)
    g1, g2 = s4[0], s4[0]
    for j in range(1, per_grp):
        g1 = jnp.where(i1 == j, s4[j], g1)
        g2 = jnp.where(i2 == j, s4[j], g2)
    total = g1 + g2
    e_ref[...] = jnp.concatenate([grp * per_grp + i1, grp * per_grp + i2], axis=0)
    gate_ref[...] = jnp.concatenate([g1 / total, g2 / total], axis=0)


def norm_mod(x2, g, sc1p, sh, seq, out_dtype=BF16, router=None):
    t, d = x2.shape
    tm = _tile(seq, 256)
    per_b = seq // tm
    row = pl.BlockSpec((tm, d), lambda i: (i, 0))
    vec = pl.BlockSpec((1, d), lambda i: (0, 0))
    mod = pl.BlockSpec((1, 1, d), lambda i: (i // per_b, 0, 0))
    g2 = g.reshape(1, d)
    if sc1p is None:
        return pl.pallas_call(
            _norm_kernel, grid=(t // tm,), in_specs=[row, vec], out_specs=row,
            out_shape=jax.ShapeDtypeStruct((t, d), out_dtype),
            compiler_params=_params("arbitrary"), name="final_norm")(x2, g2)
    if router is None:
        return pl.pallas_call(
            _norm_mod_kernel, grid=(t // tm,), in_specs=[row, vec, mod, mod], out_specs=row,
            out_shape=jax.ShapeDtypeStruct((t, d), out_dtype),
            compiler_params=_params("arbitrary"), name="norm_mod")(x2, g2, sc1p, sh)
    whi_t, wlo_t, bias = router
    ne = whi_t.shape[0]
    wspec = pl.BlockSpec((ne, d), lambda i: (0, 0))
    kspec = pl.BlockSpec((TOP_K, tm), lambda i: (0, i))
    return pl.pallas_call(
        _norm_router_kernel, grid=(t // tm,),
        in_specs=[row, vec, mod, mod, wspec, wspec, pl.BlockSpec((ne, 1), lambda i: (0, 0))],
        out_specs=[row, kspec, kspec],
        out_shape=[jax.ShapeDtypeStruct((t, d), out_dtype),
                   jax.ShapeDtypeStruct((TOP_K, t), jnp.int32),
                   jax.ShapeDtypeStruct((TOP_K, t), F32)],
        compiler_params=_params("arbitrary"), name="norm_router")(x2, g2, sc1p, sh, whi_t, wlo_t, bias)


def _mm_kernel(a_ref, w_ref, o_ref):
    o_ref[...] = _dot(a_ref[...], w_ref[...]).astype(o_ref.dtype)


def matmul(a, w, out_dtype, tm=1024, tn=1024):
    m, kd = a.shape
    n = w.shape[1]
    tm, tn = _tile(m, tm), _tile(n, tn)
    return pl.pallas_call(
        _mm_kernel,
        grid=(m // tm, n // tn),
        in_specs=[pl.BlockSpec((tm, kd), lambda i, j: (i, 0)),
                  pl.BlockSpec((kd, tn), lambda i, j: (0, j))],
        out_specs=pl.BlockSpec((tm, tn), lambda i, j: (i, j)),
        out_shape=jax.ShapeDtypeStruct((m, n), out_dtype),
        compiler_params=_params("parallel", "parallel"),
        name="matmul",
    )(a, w)


def _rope_pair(r2, cos_ref, sin_ref):
    return r2 * cos_ref[...] + pltpu.roll(r2, MLA_ROPE, 1) * sin_ref[...]


def _with_ones_rows(vt):
    return jnp.concatenate([vt, jnp.ones((ONES_ROWS, vt.shape[1]), vt.dtype)], axis=0)


def _qup_kernel(cq_ref, g_ref, w_ref, cos_ref, sin_ref, o_ref, an_ref, *, scale):
    @pl.when(pl.program_id(1) == 0)
    def _():
        an_ref[...] = _rms(cq_ref[...], g_ref[...]).astype(BF16)

    an = an_ref[...]
    for hh in range(w_ref.shape[0]):
        res = _dot(an, w_ref[hh])
        roped = _rope_pair(res[:, MLA_NOPE:], cos_ref, sin_ref)
        q = jnp.concatenate([res[:, :MLA_NOPE], roped], axis=1) * scale
        o_ref[hh, 0] = q.T.astype(o_ref.dtype)


def mla_q_up(proj_a, g, w_q, cos_p, sin_p, seq, scale):
    t = proj_a.shape[0]
    heads = w_q.shape[0]
    hb = UP_HEADS_PER_STEP
    tm = _tile(seq, ATTN_TILE)
    per_b = seq // tm
    return pl.pallas_call(
        functools.partial(_qup_kernel, scale=scale),
        grid=(t // tm, heads // hb),
        in_specs=[pl.BlockSpec((tm, Q_LORA), lambda i, h: (i, 0)),
                  pl.BlockSpec((1, Q_LORA), lambda i, h: (0, 0)),
                  pl.BlockSpec((hb, Q_LORA, MLA_DK_PAD), lambda i, h: (h, 0, 0)),
                  pl.BlockSpec((tm, LANE), lambda i, h: (i % per_b, 0)),
                  pl.BlockSpec((tm, LANE), lambda i, h: (i % per_b, 0))],
        out_specs=pl.BlockSpec((hb, 1, MLA_DK_PAD, tm), lambda i, h: (h, i, 0, 0)),
        out_shape=jax.ShapeDtypeStruct((heads, t // tm, MLA_DK_PAD, tm), BF16),
        scratch_shapes=[pltpu.VMEM((tm, Q_LORA), BF16)],
        compiler_params=_params("parallel", "arbitrary"),
        name="mla_q_up",
    )(proj_a, g.reshape(1, Q_LORA), w_q, cos_p, sin_p)


def _kvup_kernel(ckv_ref, g_ref, w_ref, kpe_ref, cos_ref, sin_ref, k_ref, vt_ref, an_ref):
    @pl.when(pl.program_id(1) == 0)
    def _():
        an_ref[...] = _rms(ckv_ref[...], g_ref[...]).astype(BF16)

    an = an_ref[...]
    k_rot = _rope_pair(kpe_ref[...], cos_ref, sin_ref)
    hw = MLA_NOPE + MLA_V
    for hh in range(k_ref.shape[0]):
        res = _dot(an, w_ref[:, hh * hw:(hh + 1) * hw])
        k_ref[hh] = jnp.concatenate([res[:, :MLA_NOPE], k_rot], axis=1).astype(k_ref.dtype)
        vt_ref[hh, 0] = _with_ones_rows(res[:, MLA_NOPE:].T).astype(vt_ref.dtype)


def mla_kv_up(proj_a, g, w_kv, cos_p, sin_p, seq):
    t = proj_a.shape[0]
    hw = MLA_NOPE + MLA_V
    heads = w_kv.shape[1] // hw
    hb = UP_HEADS_PER_STEP
    tm = _tile(seq, ATTN_TILE)
    per_b = seq // tm
    return pl.pallas_call(
        _kvup_kernel,
        grid=(t // tm, heads // hb),
        in_specs=[pl.BlockSpec((tm, KV_LORA), lambda i, h: (i, Q_LORA // KV_LORA)),
                  pl.BlockSpec((1, KV_LORA), lambda i, h: (0, 0)),
                  pl.BlockSpec((KV_LORA, hb * hw), lambda i, h: (0, h)),
                  pl.BlockSpec((tm, LANE), lambda i, h: (i, (Q_LORA + KV_LORA) // LANE)),
                  pl.BlockSpec((tm, LANE), lambda i, h: (i % per_b, 0)),
                  pl.BlockSpec((tm, LANE), lambda i, h: (i % per_b, 0))],
        out_specs=[pl.BlockSpec((hb, tm, MLA_DK_PAD), lambda i, h: (h, i, 0)),
                   pl.BlockSpec((hb, 1, MLA_V + ONES_ROWS, tm), lambda i, h: (h, i, 0, 0))],
        out_shape=[jax.ShapeDtypeStruct((heads, t, MLA_DK_PAD), BF16),
                   jax.ShapeDtypeStruct((heads, t // tm, MLA_V + ONES_ROWS, tm), BF16)],
        scratch_shapes=[pltpu.VMEM((tm, KV_LORA), BF16)],
        compiler_params=_params("parallel", "arbitrary"),
        name="mla_kv_up",
    )(proj_a, g.reshape(1, KV_LORA), w_kv, proj_a, cos_p, sin_p)


def _pv(vts, p):
    rows = p.shape[0] // len(vts)
    out = _dot(vts[0], p[:rows].astype(vts[0].dtype))
    for n in range(1, len(vts)):
        out = out + _dot(vts[n], p[n * rows:(n + 1) * rows].astype(vts[n].dtype))
    return out


def _softmax_first(s, vts, m_ref, acc_ref):
    m = jnp.max(s, axis=0, keepdims=True)
    m_ref[...] = m
    acc_ref[...] = _pv(vts, jnp.exp(s - m))


def _softmax_step(s, vts, m_ref, acc_ref):
    m_prev = m_ref[...]
    m_new = jnp.maximum(m_prev, jnp.max(s, axis=0, keepdims=True))
    alpha = jnp.exp(m_prev - m_new)
    acc_ref[...] = alpha * acc_ref[...] + _pv(vts, jnp.exp(s - m_new))
    m_ref[...] = m_new


def _softmax_finish(acc_ref, dv):
    acc = acc_ref[...]
    return (acc[:dv, :] / acc[dv:dv + 1, :]).T


def _causal_mask_t(s):
    key = lax.broadcasted_iota(jnp.int32, s.shape, 0)
    qry = lax.broadcasted_iota(jnp.int32, s.shape, 1)
    return jnp.where(key <= qry, s, NEG_INF)


def _mla_attn_kernel(qt_ref, k_ref, vt_ref, o_ref, m_ref, acc_ref, *, tq, dv):
    i = pl.program_id(2)
    n_h = qt_ref.shape[0]
    qts = [qt_ref[h, 0] for h in range(n_h)]

    def scores(h, j, n):
        off = pl.multiple_of(j * tq, tq)
        return _dot(k_ref[h, pl.ds(off, n * tq), :], qts[h])

    for h in range(n_h):
        _softmax_first(_causal_mask_t(scores(h, i, 1)), [vt_ref[h, i]], m_ref.at[h], acc_ref.at[h])

    @pl.when(i % 2 == 1)
    def _():
        for h in range(n_h):
            _softmax_step(scores(h, i - 1, 1), [vt_ref[h, i - 1]], m_ref.at[h], acc_ref.at[h])

    def body(c, carry):
        for h in range(n_h):
            _softmax_step(scores(h, 2 * c, 2), [vt_ref[h, 2 * c], vt_ref[h, 2 * c + 1]],
                          m_ref.at[h], acc_ref.at[h])
        return carry

    lax.fori_loop(0, i // 2, body, 0)
    for h in range(n_h):
        o_ref[:, h * dv:(h + 1) * dv] = _softmax_finish(acc_ref.at[h], dv).astype(o_ref.dtype)


def mla_attention(qt, k, vt, batch, seq, out_dtype):
    heads, _, dk, tq = qt.shape
    t = k.shape[1]
    dvx = vt.shape[2]
    dv = dvx - ONES_ROWS
    nq = seq // tq
    hb = ATTN_HEADS_PER_STEP
    return pl.pallas_call(
        functools.partial(_mla_attn_kernel, tq=tq, dv=dv),
        grid=(batch, heads // hb, nq),
        in_specs=[pl.BlockSpec((hb, 1, dk, tq), lambda b, h, i: (h, b * nq + i, 0, 0)),
                  pl.BlockSpec((hb, seq, dk), lambda b, h, i: (h, b, 0)),
                  pl.BlockSpec((hb, nq, dvx, tq), lambda b, h, i: (h, b, 0, 0))],
        out_specs=pl.BlockSpec((tq, hb * dv), lambda b, h, i: (b * nq + i, h)),
        out_shape=jax.ShapeDtypeStruct((t, heads * dv), out_dtype),
        scratch_shapes=[pltpu.VMEM((hb, 1, tq), F32), pltpu.VMEM((hb, dvx, tq), F32)],
        compiler_params=_params("parallel", "parallel", "arbitrary"),
        name="mla_attention",
    )(qt, k, vt)


def _conv_kernel(b_ref, c_ref, h_ref, w_ref, o_ref, carry_ref, *, per_b):
    i = pl.program_id(0)

    @pl.when(i % per_b == 0)
    def _():
        carry_ref[...] = jnp.zeros_like(carry_ref)

    tm = o_ref.shape[0]
    w0, w1, w2 = w_ref[0:1, :], w_ref[1:2, :], w_ref[2:3, :]
    u = c_ref[...] * h_ref[...]
    y = w0 * pltpu.roll(u, 2, 0) + w1 * pltpu.roll(u, 1, 0) + w2 * u
    o_ref[...] = (b_ref[...] * y).astype(o_ref.dtype)
    u8 = u[0:SUBLANE, :]
    tail = carry_ref[...]
    r8 = lax.broadcasted_iota(jnp.int32, u8.shape, 0)
    p1 = jnp.where(r8 < 1, pltpu.roll(tail, 1, 0), pltpu.roll(u8, 1, 0))
    p2 = jnp.where(r8 < 2, pltpu.roll(tail, 2, 0), pltpu.roll(u8, 2, 0))
    y8 = w0 * p2 + w1 * p1 + w2 * u8
    o_ref[0:SUBLANE, :] = (b_ref[0:SUBLANE, :] * y8).astype(o_ref.dtype)
    carry_ref[...] = u[tm - SUBLANE:tm, :]


def conv_mixer(bch, conv_w, seq, out_dtype):
    t = bch.shape[0]
    ch = conv_w.shape[1]
    tm = _tile(seq, 256)
    per_b = seq // tm
    return pl.pallas_call(
        functools.partial(_conv_kernel, per_b=per_b),
        grid=(t // tm,),
        in_specs=[pl.BlockSpec((tm, ch), lambda i: (i, 0)),
                  pl.BlockSpec((tm, ch), lambda i: (i, 1)),
                  pl.BlockSpec((tm, ch), lambda i: (i, 2)),
                  pl.BlockSpec((CONV_W, ch), lambda i: (0, 0))],
        out_specs=pl.BlockSpec((tm, ch), lambda i: (i, 0)),
        out_shape=jax.ShapeDtypeStruct((t, ch), out_dtype),
        scratch_shapes=[pltpu.VMEM((SUBLANE, ch), F32)],
        compiler_params=_params("arbitrary"),
        name="conv_mixer",
    )(bch, bch, bch, conv_w)


def _moba_prep_kernel(q_ref, k_ref, v_ref, cos_ref, sin_ref, qo_ref, ko_ref, vo_ref, km_ref):
    heads = ko_ref.shape[0]
    cos, sin = cos_ref[...], sin_ref[...]
    n_sub = km_ref.shape[0]
    means = [[] for _ in range(n_sub)]
    for h in range(heads):
        sl = slice(h * MOBA_HD, (h + 1) * MOBA_HD)
        qh = q_ref[:, sl]
        kh = k_ref[:, sl]
        qr = qh * cos + pltpu.roll(qh, MOBA_HD // 2, 1) * sin
        qo_ref[h, 0] = qr.T.astype(qo_ref.dtype)
        kr = kh * cos + pltpu.roll(kh, MOBA_HD // 2, 1) * sin
        ko_ref[h] = kr.astype(ko_ref.dtype)
        vo_ref[h, 0] = _with_ones_rows(v_ref[:, sl].T).astype(vo_ref.dtype)
        for s in range(n_sub):
            means[s].append(jnp.mean(kr[s * MOBA_BLOCK:(s + 1) * MOBA_BLOCK], axis=0, keepdims=True))
    for s in range(n_sub):
        km_ref[s] = jnp.concatenate(means[s], axis=0)


def moba_prep(qkv, cos_f, sin_s, seq):
    t = qkv.shape[0]
    width = qkv.shape[1] // 3
    heads = width // MOBA_HD
    tm = 2 * MOBA_BLOCK
    assert seq % tm == 0
    per_b = seq // tm
    nt = t // tm
    return pl.pallas_call(
        _moba_prep_kernel,
        grid=(nt,),
        in_specs=[pl.BlockSpec((tm, width), lambda i: (i, 0)),
                  pl.BlockSpec((tm, width), lambda i: (i, 1)),
                  pl.BlockSpec((tm, width), lambda i: (i, 2)),
                  pl.BlockSpec((tm, MOBA_HD), lambda i: (i % per_b, 0)),
                  pl.BlockSpec((tm, MOBA_HD), lambda i: (i % per_b, 0))],
        out_specs=[pl.BlockSpec((heads, 1, MOBA_HD, tm), lambda i: (0, i, 0, 0)),
                   pl.BlockSpec((heads, tm, MOBA_HD), lambda i: (0, i, 0)),
                   pl.BlockSpec((heads, 1, MOBA_HD + ONES_ROWS, tm), lambda i: (0, i, 0, 0)),
                   pl.BlockSpec((2, heads, MOBA_HD), lambda i: (i, 0, 0))],
        out_shape=[jax.ShapeDtypeStruct((heads, nt, MOBA_HD, tm), BF16),
                   jax.ShapeDtypeStruct((heads, t, MOBA_HD), BF16),
                   jax.ShapeDtypeStruct((heads, nt, MOBA_HD + ONES_ROWS, tm), BF16),
                   jax.ShapeDtypeStruct((2 * nt, heads, MOBA_HD), F32)],
        compiler_params=_params("arbitrary"),
        name="moba_prep",
    )(qkv, qkv, qkv, cos_f, sin_s)


def _moba_attn_kernel(qt_ref, k_ref, vt_ref, km_ref, o_ref, m_ref, acc_ref, sel_ref, *, scale, dv):
    i = pl.program_id(2)
    n_h = qt_ref.shape[0]
    n_blk = km_ref.shape[2]
    tq = qt_ref.shape[3]
    blk_w = MOBA_BLOCK
    blk = lax.broadcasted_iota(jnp.int32, (n_blk, tq), 0)
    own = 2 * i + (lax.broadcasted_iota(jnp.int32, (n_blk, tq), 1) >= blk_w).astype(jnp.int32)
    past = blk < own
    qts = []
    for h in range(n_h):
        qtf = qt_ref[h, 0].astype(F32)
        g = _dot(km_ref[0, h], qtf, precision=lax.Precision.HIGHEST)
        g = jnp.where(past, g, NEG_INF)
        sel = jnp.zeros(g.shape, F32)
        for _ in range(min(MOBA_TOPK, n_blk)):
            mx = jnp.max(g, axis=0, keepdims=True)
            first = jnp.min(jnp.where(g == mx, blk, n_blk), axis=0, keepdims=True)
            hit = blk == first
            sel = jnp.where(hit, 1.0, sel)
            g = jnp.where(hit, NEG_INF, g)
        sel_ref[h] = jnp.where(past, sel, 0.0)
        qts.append((qtf * scale).astype(k_ref.dtype))

    def scores(h, p, n):
        off = pl.multiple_of(p * tq, tq)
        return _dot(k_ref[h, pl.ds(off, n * tq), :], qts[h])

    def chosen(h, n):
        return sel_ref[h, pl.ds(n, 1), :] > 0.5

    def mask_past(h, s, first_blk):
        parts = [jnp.where(chosen(h, first_blk + n), s[n * blk_w:(n + 1) * blk_w], NEG_INF)
                 for n in range(s.shape[0] // blk_w)]
        return jnp.concatenate(parts, axis=0)

    key = lax.broadcasted_iota(jnp.int32, (blk_w, tq), 0)
    qry = lax.broadcasted_iota(jnp.int32, (blk_w, tq), 1)
    bot_ok = key <= qry - blk_w
    for h in range(n_h):
        s = scores(h, i, 1)
        chosen_lim = jnp.where(chosen(h, 2 * i), blk_w, -1)
        top_ok = key <= jnp.where(qry < blk_w, qry, chosen_lim)
        s = jnp.concatenate([jnp.where(top_ok, s[:blk_w], NEG_INF),
                             jnp.where(bot_ok, s[blk_w:], NEG_INF)], axis=0)
        _softmax_first(s, [vt_ref[h, i]], m_ref.at[h], acc_ref.at[h])

    @pl.when(i % 2 == 1)
    def _():
        for h in range(n_h):
            s = mask_past(h, scores(h, i - 1, 1), 2 * (i - 1))
            _softmax_step(s, [vt_ref[h, i - 1]], m_ref.at[h], acc_ref.at[h])

    def body(c, carry):
        for h in range(n_h):
            s = mask_past(h, scores(h, 2 * c, 2), 4 * c)
            _softmax_step(s, [vt_ref[h, 2 * c], vt_ref[h, 2 * c + 1]], m_ref.at[h], acc_ref.at[h])
        return carry

    lax.fori_loop(0, i // 2, body, 0)
    for h in range(n_h):
        o_ref[:, h * dv:(h + 1) * dv] = _softmax_finish(acc_ref.at[h], dv).astype(o_ref.dtype)


def moba_attention(qt, k, vt, k_mean, batch, seq, out_dtype):
    heads, _, hd, tq = qt.shape
    t = k.shape[1]
    dvx = vt.shape[2]
    dv = dvx - ONES_ROWS
    n_blk = seq // MOBA_BLOCK
    nq = seq // tq
    hb = ATTN_HEADS_PER_STEP
    return pl.pallas_call(
        functools.partial(_moba_attn_kernel, scale=hd ** -0.5, dv=dv),
        grid=(batch, heads // hb, nq),
        in_specs=[pl.BlockSpec((hb, 1, hd, tq), lambda b, h, i: (h, b * nq + i, 0, 0)),
                  pl.BlockSpec((hb, seq, hd), lambda b, h, i: (h, b, 0)),
                  pl.BlockSpec((hb, nq, dvx, tq), lambda b, h, i: (h, b, 0, 0)),
                  pl.BlockSpec((1, hb, n_blk, hd), lambda b, h, i: (b, h, 0, 0))],
        out_specs=pl.BlockSpec((tq, hb * dv), lambda b, h, i: (b * nq + i, h)),
        out_shape=jax.ShapeDtypeStruct((t, heads * dv), out_dtype),
        scratch_shapes=[pltpu.VMEM((hb, 1, tq), F32), pltpu.VMEM((hb, dvx, tq), F32),
                        pltpu.VMEM((hb, n_blk, tq), F32)],
        compiler_params=_params("parallel", "parallel", "arbitrary"),
        name="moba_attention",
    )(qt, k, vt, k_mean)


def _out_proj_kernel(a_ref, b_ref, c_ref, ga_ref, gb_ref, gc_ref, w_ref, x_ref, gate_ref, o_ref,
                     y_ref):
    @pl.when(pl.program_id(1) == 0)
    def _():
        wa, wb = a_ref.shape[1], b_ref.shape[1]
        y_ref[:, 0:wa] = _rms(a_ref[...].astype(F32), ga_ref[...]).astype(y_ref.dtype)
        y_ref[:, wa:wa + wb] = _rms(b_ref[...].astype(F32), gb_ref[...]).astype(y_ref.dtype)
        y_ref[:, wa + wb:] = _rms(c_ref[...].astype(F32), gc_ref[...]).astype(y_ref.dtype)

    o_ref[...] = x_ref[...] + gate_ref[0] * _dot(y_ref[...], w_ref[...])


def out_proj(y_a, y_b, y_c, g, w, x2, gate, seq, tm=512, tn=1024):
    t = y_a.shape[0]
    wa, wb, wc = y_a.shape[1], y_b.shape[1], y_c.shape[1]
    kd, n = w.shape
    tm, tn = _tile(seq, tm), _tile(n, tn)
    per_b = seq // tm
    ga, gb, gc = g[:wa].reshape(1, wa), g[wa:wa + wb].reshape(1, wb), g[wa + wb:].reshape(1, wc)
    rows = lambda width: pl.BlockSpec((tm, width), lambda i, j: (i, 0))
    vec = lambda width: pl.BlockSpec((1, width), lambda i, j: (0, 0))
    return pl.pallas_call(
        _out_proj_kernel,
        grid=(t // tm, n // tn),
        in_specs=[rows(wa), rows(wb), rows(wc), vec(wa), vec(wb), vec(wc),
                  pl.BlockSpec((kd, tn), lambda i, j: (0, j)),
                  pl.BlockSpec((tm, tn), lambda i, j: (i, j)),
                  pl.BlockSpec((1, 1, tn), lambda i, j: (i // per_b, 0, j))],
        out_specs=pl.BlockSpec((tm, tn), lambda i, j: (i, j)),
        out_shape=jax.ShapeDtypeStruct((t, n), F32),
        scratch_shapes=[pltpu.VMEM((tm, kd), BF16)],
        compiler_params=_params("parallel", "arbitrary"),
        name="out_proj",
    )(y_a, y_b, y_c, ga, gb, gc, w, x2, gate)


def _ffn_up_kernel(bexp_ref, first_ref, nused_ref, x_ref, wg_ref, wu_ref, a_ref, wg_bf, wu_bf):
    b = pl.program_id(1)
    used = b < nused_ref[0]

    @pl.when(used & (first_ref[b] == 1))
    def _():
        wg_bf[...] = wg_ref[0, 0].astype(BF16)
        wu_bf[...] = wu_ref[0, 0].astype(BF16)

    @pl.when(used)
    def _():
        x = x_ref[...]
        gate = _dot(x, wg_bf[...])
        up = _dot(x, wu_bf[...])
        a_ref[...] = (gate * jax.nn.sigmoid(gate) * up).astype(a_ref.dtype)

    @pl.when(jnp.logical_not(used))
    def _():
        a_ref[...] = jnp.zeros_like(a_ref)


def _pack_bf16_pair(lo, hi):
    lo_bits = lax.bitcast_convert_type(lo.astype(BF16).astype(F32), PACKED) >> 16
    hi_bits = lax.bitcast_convert_type(hi.astype(BF16).astype(F32), PACKED) & jnp.uint32(0xFFFF0000)
    return hi_bits | lo_bits


def _unpack_bf16_pair(packed):
    lo = lax.bitcast_convert_type(packed << 16, F32)
    hi = lax.bitcast_convert_type(packed & jnp.uint32(0xFFFF0000), F32)
    return lo, hi


def _ffn_down_kernel(bexp_ref, first_ref, nused_ref, a_ref, wd_ref, o_ref, wd_bf):
    b = pl.program_id(1)
    used = b < nused_ref[0]

    @pl.when(used & (first_ref[b] == 1))
    def _():
        wd_bf[...] = wd_ref[0, 0].astype(BF16)

    @pl.when(used)
    def _():
        y = _dot(a_ref[...], wd_bf[...])
        half = y.shape[0] // 2
        o_ref[...] = _pack_bf16_pair(y[:half], y[half:])

    @pl.when(jnp.logical_not(used))
    def _():
        o_ref[...] = jnp.zeros_like(o_ref)


def expert_ffn(xs, block_exp, block_first, n_used, w_gate, w_up, w_down, layer):
    n_slots, d = xs.shape
    ff = w_gate.shape[3]
    fh, dh = ff // FF_SPLIT, d // FF_SPLIT
    n_blocks = n_slots // MOE_BLOCK
    up_spec = pltpu.PrefetchScalarGridSpec(
        num_scalar_prefetch=3,
        grid=(FF_SPLIT, n_blocks),
        in_specs=[pl.BlockSpec((MOE_BLOCK, d), lambda j, b, be, fi, nu: (b, 0)),
                  pl.BlockSpec((1, 1, d, fh), lambda j, b, be, fi, nu: (layer, be[b], 0, j)),
                  pl.BlockSpec((1, 1, d, fh), lambda j, b, be, fi, nu: (layer, be[b], 0, j))],
        out_specs=pl.BlockSpec((MOE_BLOCK, fh), lambda j, b, be, fi, nu: (b, j)),
        scratch_shapes=[pltpu.VMEM((d, fh), BF16), pltpu.VMEM((d, fh), BF16)],
    )
    act = pl.pallas_call(
        _ffn_up_kernel,
        grid_spec=up_spec,
        out_shape=jax.ShapeDtypeStruct((n_slots, ff), BF16),
        compiler_params=_params("arbitrary", "arbitrary"),
        name="expert_ffn_up",
    )(block_exp, block_first, n_used, xs, w_gate, w_up)
    down_spec = pltpu.PrefetchScalarGridSpec(
        num_scalar_prefetch=3,
        grid=(FF_SPLIT, n_blocks),
        in_specs=[pl.BlockSpec((MOE_BLOCK, ff), lambda j, b, be, fi, nu: (b, 0)),
                  pl.BlockSpec((1, 1, ff, dh), lambda j, b, be, fi, nu: (layer, be[b], 0, j))],
        out_specs=pl.BlockSpec((MOE_BLOCK // 2, dh), lambda j, b, be, fi, nu: (b, j)),
        scratch_shapes=[pltpu.VMEM((ff, dh), BF16)],
    )
    return pl.pallas_call(
        _ffn_down_kernel,
        grid_spec=down_spec,
        out_shape=jax.ShapeDtypeStruct((n_slots // 2, d), PACKED),
        compiler_params=_params("arbitrary", "arbitrary"),
        name="expert_ffn_down",
    )(block_exp, block_first, n_used, act, w_down)


def _start_row_copies(idx_ref, base, n_rows, src_ref, dst_ref, sem):
    def issue(r, carry):
        row = idx_ref[base + r]
        pltpu.make_async_copy(src_ref.at[pl.ds(row, 1)], dst_ref.at[pl.ds(r, 1)], sem).start()
        return carry

    lax.fori_loop(0, n_rows, issue, 0, unroll=8)


def _wait_row_copies(n_rows, src_ref, dst_ref, sem):
    pltpu.make_async_copy(src_ref.at[pl.ds(0, n_rows)], dst_ref, sem).wait()


def _gather_cast_kernel(idx_ref, nused_ref, src_ref, o_ref, buf, sem):
    b = pl.program_id(0)
    rows = o_ref.shape[0]
    n_used = nused_ref[0]

    @pl.when((b == 0) & (n_used > 0))
    def _():
        _start_row_copies(idx_ref, 0, rows, src_ref, buf.at[0], sem.at[0])

    @pl.when(b + 1 < n_used)
    def _():
        nxt = (b + 1) % 2
        _start_row_copies(idx_ref, (b + 1) * rows, rows, src_ref, buf.at[nxt], sem.at[nxt])

    @pl.when(b < n_used)
    def _():
        cur = b % 2
        _wait_row_copies(rows, src_ref, buf.at[cur], sem.at[cur])
        o_ref[...] = buf[cur].astype(o_ref.dtype)

    @pl.when(b >= n_used)
    def _():
        o_ref[...] = jnp.zeros_like(o_ref)


def gather_rows_cast(src, idx, n_used_tiles, out_dtype):
    d = src.shape[1]
    m = idx.shape[0]
    rows = GATHER_ROWS
    grid_spec = pltpu.PrefetchScalarGridSpec(
        num_scalar_prefetch=2,
        grid=(m // rows,),
        in_specs=[pl.BlockSpec(memory_space=pl.ANY)],
        out_specs=pl.BlockSpec((rows, d), lambda b, ix, nu: (b, 0)),
        scratch_shapes=[pltpu.VMEM((2, rows, d), src.dtype), pltpu.SemaphoreType.DMA((2,))],
    )
    return pl.pallas_call(
        _gather_cast_kernel,
        grid_spec=grid_spec,
        out_shape=jax.ShapeDtypeStruct((m, d), out_dtype),
        compiler_params=_params("arbitrary"),
        name="gather_rows_cast",
    )(idx, n_used_tiles, src)


def _combine_kernel(pos_ref, x_ref, gate_ref, w_ref, half_ref, ys_ref, o_ref, ybuf, sem, *, n_tok):
    i = pl.program_id(0)
    n_tiles = pl.num_programs(0)
    tm = x_ref.shape[0]

    def start_tile(tile, slot):
        for kk in range(TOP_K):
            _start_row_copies(pos_ref, kk * n_tok + tile * tm, tm, ys_ref, ybuf.at[slot, kk],
                              sem.at[slot])

    @pl.when(i == 0)
    def _():
        start_tile(0, 0)

    @pl.when(i + 1 < n_tiles)
    def _():
        start_tile(i + 1, (i + 1) % 2)

    cur = i % 2
    for kk in range(TOP_K):
        _wait_row_copies(tm, ys_ref, ybuf.at[cur, kk], sem.at[cur])
    w = w_ref[...]
    upper = half_ref[...] > 0.5
    acc = None
    for kk in range(TOP_K):
        lo, hi = _unpack_bf16_pair(ybuf[cur, kk])
        term = w[:, kk:kk + 1] * jnp.where(upper[:, kk:kk + 1], hi, lo)
        acc = term if acc is None else acc + term
    o_ref[...] = x_ref[...] + gate_ref[0] * acc


def moe_combine(x2, gate, ys, pos, w_tok, half_tok, seq):
    t, d = x2.shape
    tm = _tile(seq, GATHER_ROWS)
    per_b = seq // tm
    grid_spec = pltpu.PrefetchScalarGridSpec(
        num_scalar_prefetch=1,
        grid=(t // tm,),
        in_specs=[pl.BlockSpec((tm, d), lambda i, ps: (i, 0)),
                  pl.BlockSpec((1, 1, d), lambda i, ps: (i // per_b, 0, 0)),
                  pl.BlockSpec((tm, TOP_K), lambda i, ps: (i, 0)),
                  pl.BlockSpec((tm, TOP_K), lambda i, ps: (i, 0)),
                  pl.BlockSpec(memory_space=pl.ANY)],
        out_specs=pl.BlockSpec((tm, d), lambda i, ps: (i, 0)),
        scratch_shapes=[pltpu.VMEM((2, TOP_K, tm, d), ys.dtype), pltpu.SemaphoreType.DMA((2,))],
    )
    return pl.pallas_call(
        functools.partial(_combine_kernel, n_tok=t),
        grid_spec=grid_spec,
        out_shape=jax.ShapeDtypeStruct((t, d), F32),
        compiler_params=_params("arbitrary"),
        name="moe_combine",
    )(pos, x2, gate, w_tok, half_tok, ys)


def _group_by_expert(e_idx):
    n_tok = e_idx.shape[1]
    n_asg = n_tok * TOP_K
    flat_e = e_idx.reshape(n_asg)
    onehot = (flat_e[:, None] == jnp.arange(N_EXPERTS)[None, :]).astype(jnp.int32)
    running = jnp.cumsum(onehot, axis=0)
    rank = jnp.sum(running * onehot, axis=1) - 1
    sizes = running[-1]
    padded = (sizes + MOE_BLOCK - 1) // MOE_BLOCK * MOE_BLOCK
    pad_end = jnp.cumsum(padded)
    pad_start = pad_end - padded
    dest = (jnp.sum(pad_start[None, :] * onehot, axis=1) + rank).astype(jnp.int32)
    n_blocks = -(-n_asg // MOE_BLOCK) + N_EXPERTS
    n_slots = n_blocks * MOE_BLOCK
    slot_tok = (jnp.arange(n_slots, dtype=jnp.int32) % n_tok).at[dest].set(
        jnp.arange(n_asg, dtype=jnp.int32) % n_tok, mode="promise_in_bounds")
    block_start = jnp.arange(n_blocks, dtype=jnp.int32) * MOE_BLOCK
    block_exp = jnp.minimum(jnp.sum(block_start[:, None] >= pad_end[None, :], axis=1),
                            N_EXPERTS - 1).astype(jnp.int32)
    n_used = (pad_end[-1] // MOE_BLOCK).astype(jnp.int32)
    block_exp = jnp.where(jnp.arange(n_blocks) < n_used, block_exp, block_exp[n_used - 1])
    prev_exp = jnp.concatenate([jnp.full((1,), -1, jnp.int32), block_exp[:-1]])
    block_first = (block_exp != prev_exp).astype(jnp.int32)
    return dest.reshape(TOP_K, n_tok), slot_tok, block_exp, block_first, n_used.reshape(1)


def _rot_cols(w):
    half = w.shape[-1] // 2
    return jnp.concatenate([-w[..., half:], w[..., :half]], axis=-1)


def _prep_layer(w_in, w_uq, w_ukv, w_out):
    a_end = Q_LORA + KV_LORA + MLA_ROPE
    b_end = a_end + 3 * CONV_CH
    w_kpe = w_in[:, Q_LORA + KV_LORA:a_end]
    w_a = jnp.concatenate([w_in[:, :a_end], _rot_cols(w_kpe)], axis=1).astype(BF16)
    w_b = w_in[:, a_end:b_end].astype(BF16)
    w_c = w_in[:, b_end:].astype(BF16)
    wq = w_uq.reshape(Q_LORA, MLA_HEADS, MLA_NOPE + MLA_ROPE)
    wq_rope = wq[..., MLA_NOPE:]
    wq = jnp.concatenate([wq, _rot_cols(wq_rope)], axis=-1).transpose(1, 0, 2).astype(BF16)
    return dict(w_a=w_a, w_b=w_b, w_c=w_c, w_q=wq, w_kv=w_ukv.astype(BF16),
                w_out=w_out.astype(BF16))


def _rope_tables(seq, dim):
    inv = 1.0 / (ROPE_THETA ** (jnp.arange(0, dim, 2, dtype=F32) / dim))
    ang = jnp.arange(seq, dtype=F32)[:, None] * inv[None, :]
    return jnp.cos(ang), jnp.sin(ang)


def kernel(x, c, w_mod, mod_table, mix_norm_g, w_in, q_norm_g, kv_norm_g, w_uq, w_ukv, conv_w,
           group_norm_g, w_out, ffn_norm_g, w_router, router_bias, w_gate, w_up, w_down,
           final_norm_g):
    batch, seq, d = x.shape
    depth = w_in.shape[0]
    t = batch * seq
    x2 = x.reshape(t, d)

    cos_a, sin_a = _rope_tables(seq, MLA_ROPE)
    zeros_a = jnp.zeros_like(cos_a)
    cos_p = jnp.concatenate([cos_a, cos_a, zeros_a, zeros_a], axis=1)
    sin_p = jnp.concatenate([sin_a, sin_a, zeros_a, zeros_a], axis=1)
    cos_b, sin_b = _rope_tables(seq, MOBA_HD)
    cos_f = jnp.concatenate([cos_b, cos_b], axis=1)
    sin_s = jnp.concatenate([-sin_b, sin_b], axis=1)

    c_pad = jnp.zeros((SUBLANE, d), F32).at[:batch].set(c)
    mod_shared = mod_matmul(c_pad, w_mod)[:batch].reshape(batch, N_MOD, d)

    wr_t = w_router.T
    wr_hi = wr_t.astype(BF16)
    wr_lo = (wr_t - wr_hi.astype(F32)).astype(BF16)
    router = (wr_hi, wr_lo, router_bias.astype(F32).reshape(N_EXPERTS, 1))
    mla_scale = (MLA_NOPE + MLA_ROPE) ** -0.5

    for l in range(depth):
        p = _prep_layer(w_in[l], w_uq[l], w_ukv[l], w_out[l])
        mod = mod_shared + mod_table[l][None]
        sh1, sc1, g1, sh2, sc2, g2 = [mod[:, i][:, None, :] for i in range(N_MOD)]

        h = norm_mod(x2, mix_norm_g[l], 1.0 + sc1, sh1, seq)
        proj_a = matmul(h, p["w_a"], F32, tm=512)
        bch = matmul(h, p["w_b"], F32)
        qkv = matmul(h, p["w_c"], F32)

        q_a = mla_q_up(proj_a, q_norm_g[l], p["w_q"], cos_p, sin_p, seq, mla_scale)
        k_a, v_a = mla_kv_up(proj_a, kv_norm_g[l], p["w_kv"], cos_p, sin_p, seq)
        y_a = mla_attention(q_a, k_a, v_a, batch, seq, BF16)

        y_b = conv_mixer(bch, conv_w[l], seq, F32)

        q_c, k_c, v_c, k_mean = moba_prep(qkv, cos_f, sin_s, seq)
        n_blk = seq // MOBA_BLOCK
        k_mean = k_mean.reshape(batch, n_blk, MOBA_HEADS, MOBA_HD).transpose(0, 2, 1, 3)
        y_c = moba_attention(q_c, k_c, v_c, k_mean, batch, seq, BF16)

        x2 = out_proj(y_a, y_b, y_c, group_norm_g[l], p["w_out"], x2, g1, seq)

        h2, e_idx, gates = norm_mod(x2, ffn_norm_g[l], 1.0 + sc2, sh2, seq, out_dtype=F32,
                                    router=router)
        pos, slot_tok, block_exp, block_first, n_used = _group_by_expert(e_idx)
        xs = gather_rows_cast(h2, slot_tok, n_used * (MOE_BLOCK // GATHER_ROWS), BF16)
        ys = expert_ffn(xs, block_exp, block_first, n_used, w_gate, w_up, w_down, l)
        sub = pos % MOE_BLOCK
        ys_row = (pos // MOE_BLOCK) * (MOE_BLOCK // 2) + sub % (MOE_BLOCK // 2)
        ys_half = (sub // (MOE_BLOCK // 2)).astype(F32)
        x2 = moe_combine(x2, g2, ys, ys_row.reshape(-1), gates.T, ys_half.T, seq)

    out = norm_mod(x2, final_norm_g, None, None, seq, out_dtype=F32)
    return out.reshape(batch, seq, d)
```

```python
import functools

import jax
import jax.numpy as jnp
from jax import lax
from jax.experimental import pallas as pl
from jax.experimental.pallas import tpu as pltpu

MLA_HEADS = 16
MLA_NOPE = 128
MLA_ROPE = 64
MLA_V = 128
Q_LORA = 1024
KV_LORA = 512
MLA_DK_PAD = 256
CONV_CH = 1024
CONV_W = 3
MOBA_HEADS = 8
MOBA_HD = 128
MOBA_BLOCK = 256
MOBA_TOPK = 3
ROPE_THETA = 10000.0
EPS = 1e-6
N_MOD = 6
N_EXPERTS = 16
N_GROUPS = 4
TOP_K = 2
MOE_BLOCK = 512
GATHER_ROWS = 256
FF_SPLIT = 2
UP_HEADS_PER_STEP = 4
ATTN_HEADS_PER_STEP = 4
ATTN_TILE = 512
ONES_ROWS = 16

V7X_VMEM_LIMIT_BYTES = 56 * 1024 * 1024
LANE = 128
SUBLANE = 8

F32 = jnp.float32
BF16 = jnp.bfloat16
PACKED = jnp.uint32
NEG_INF = float("-inf")


def _params(*sem):
    return pltpu.CompilerParams(dimension_semantics=sem, vmem_limit_bytes=V7X_VMEM_LIMIT_BYTES)


def _dot(a, b, precision=None):
    return jnp.dot(a, b, preferred_element_type=F32, precision=precision)


def _dot_nt(a, b):
    return lax.dot_general(a, b, (((1,), (1,)), ((), ())), preferred_element_type=F32)


def _tile(dim, want):
    return want if dim % want == 0 else dim


def _mod_kernel(c_ref, w0_ref, w1_ref, o_ref):
    @pl.when(pl.program_id(0) == 0)
    def _():
        o_ref[...] = jnp.zeros_like(o_ref)

    c = c_ref[...]
    a = (c * jax.nn.sigmoid(c)).astype(BF16)
    half = w0_ref.shape[1]
    o_ref[:, :half] += _dot(a, w0_ref[...].astype(BF16))
    o_ref[:, half:] += _dot(a, w1_ref[...].astype(BF16))


def mod_matmul(c_pad, w_mod):
    rows, d = c_pad.shape
    n = w_mod.shape[1]
    tk = _tile(d, LANE)
    return pl.pallas_call(
        _mod_kernel,
        grid=(d // tk,),
        in_specs=[pl.BlockSpec((rows, tk), lambda k: (0, k)),
                  pl.BlockSpec((tk, n // 2), lambda k: (k, 0)),
                  pl.BlockSpec((tk, n // 2), lambda k: (k, 1))],
        out_specs=pl.BlockSpec((rows, n), lambda k: (0, 0)),
        out_shape=jax.ShapeDtypeStruct((rows, n), F32),
        compiler_params=_params("arbitrary"),
        name="mod_matmul",
    )(c_pad, w_mod, w_mod)


def _rms(x, g):
    return x * lax.rsqrt(jnp.mean(x * x, axis=-1, keepdims=True) + EPS) * g


def _norm_mod_kernel(x_ref, g_ref, sc_ref, sh_ref, o_ref):
    y = _rms(x_ref[...], g_ref[...])
    o_ref[...] = (y * sc_ref[0] + sh_ref[0]).astype(o_ref.dtype)


def _top2_sum(a, b, c, d):
    hi1, lo1 = jnp.maximum(a, b), jnp.minimum(a, b)
    hi2, lo2 = jnp.maximum(c, d), jnp.minimum(c, d)
    return jnp.maximum(hi1, hi2) + jnp.maximum(jnp.minimum(hi1, hi2), jnp.maximum(lo1, lo2))


def _norm_router_kernel(x_ref, g_ref, sc_ref, sh_ref, whi_ref, wlo_ref, bias_ref,
                        o_ref, e_ref, gate_ref):
    h = _rms(x_ref[...], g_ref[...]) * sc_ref[0] + sh_ref[0]
    hi = h.astype(BF16)
    o_ref[...] = h.astype(o_ref.dtype)
    lo = (h - hi.astype(F32)).astype(BF16)
    whi, wlo = whi_ref[...], wlo_ref[...]
    logits = _dot_nt(whi, hi) + _dot_nt(wlo, hi) + _dot_nt(whi, lo)
    scores = jax.nn.sigmoid(logits)
    biased = scores + bias_ref[...]
    per_grp = N_EXPERTS // N_GROUPS
    b_rows = [biased[e:e + 1, :] for e in range(N_EXPERTS)]
    s_rows = [scores[e:e + 1, :] for e in range(N_EXPERTS)]
    grp_scores = [_top2_sum(*b_rows[g * per_grp:(g + 1) * per_grp]) for g in range(N_GROUPS)]
    best, grp = grp_scores[0], jnp.zeros(grp_scores[0].shape, jnp.int32)
    for g in range(1, N_GROUPS):
        better = grp_scores[g] > best
        grp = jnp.where(better, g, grp)
        best = jnp.where(better, grp_scores[g], best)
    b4, s4 = b_rows[:per_grp], s_rows[:per_grp]
    for g in range(1, N_GROUPS):
        in_g = grp == g
        b4 = [jnp.where(in_g, b_rows[g * per_grp + j], b4[j]) for j in range(per_grp)]
        s4 = [jnp.where(in_g, s_rows[g * per_grp + j], s4[j]) for j in range(per_grp)]
    v1, i1 = b4[0], jnp.zeros(grp.shape, jnp.int32)
    for j in range(1, per_grp):
        better = b4[j] > v1
        i1 = jnp.where(better, j, i1)
        v1 = jnp.where(better, b4[j], v1)
    v2, i2 = jnp.full(v1.shape, NEG_INF, F32), jnp.zeros(grp.shape, jnp.int32)
    for j in range(per_grp):
        better = (i1 != j) & (b4[j] > v2)
        i2 = jnp.where(better, j, i2)
        v2 = jnp.where(better, b4[j], v2)
    g1, g2 = s4[0], s4[0]
    for j in range(1, per_grp):
        g1 = jnp.where(i1 == j, s4[j], g1)
        g2 = jnp.where(i2 == j, s4[j], g2)
    total = g1 + g2
    e_ref[...] = jnp.concatenate([grp * per_grp + i1, grp * per_grp + i2], axis=0)
    gate_ref[...] = jnp.concatenate([g1 / total, g2 / total], axis=0)


def norm_mod(x2, g, sc1p, sh, seq, out_dtype=BF16, router=None):
    t, d = x2.shape
    tm = _tile(seq, 256)
    per_b = seq // tm
    row = pl.BlockSpec((tm, d), lambda i: (i, 0))
    vec = pl.BlockSpec((1, d), lambda i: (0, 0))
    mod = pl.BlockSpec((1, 1, d), lambda i: (i // per_b, 0, 0))
    g2 = g.reshape(1, d)
    if router is None:
        return pl.pallas_call(
            _norm_mod_kernel, grid=(t // tm,), in_specs=[row, vec, mod, mod], out_specs=row,
            out_shape=jax.ShapeDtypeStruct((t, d), out_dtype),
            compiler_params=_params("arbitrary"), name="norm_mod")(x2, g2, sc1p, sh)
    whi_t, wlo_t, bias = router
    ne = whi_t.shape[0]
    wspec = pl.BlockSpec((ne, d), lambda i: (0, 0))
    kspec = pl.BlockSpec((TOP_K, tm), lambda i: (0, i))
    return pl.pallas_call(
        _norm_router_kernel, grid=(t // tm,),
        in_specs=[row, vec, mod, mod, wspec, wspec, pl.BlockSpec((ne, 1), lambda i: (0, 0))],
        out_specs=[row, kspec, kspec],
        out_shape=[jax.ShapeDtypeStruct((t, d), out_dtype),
                   jax.ShapeDtypeStruct((TOP_K, t), jnp.int32),
                   jax.ShapeDtypeStruct((TOP_K, t), F32)],
        compiler_params=_params("arbitrary"), name="norm_router")(x2, g2, sc1p, sh, whi_t, wlo_t, bias)


def _mm_kernel(a_ref, w_ref, o_ref):
    o_ref[...] = _dot(a_ref[...], w_ref[...]).astype(o_ref.dtype)


def matmul(a, w, out_dtype, tm=1024, tn=1024):
    m, kd = a.shape
    n = w.shape[1]
    tm, tn = _tile(m, tm), _tile(n, tn)
    return pl.pallas_call(
        _mm_kernel,
        grid=(m // tm, n // tn),
        in_specs=[pl.BlockSpec((tm, kd), lambda i, j: (i, 0)),
                  pl.BlockSpec((kd, tn), lambda i, j: (0, j))],
        out_specs=pl.BlockSpec((tm, tn), lambda i, j: (i, j)),
        out_shape=jax.ShapeDtypeStruct((m, n), out_dtype),
        compiler_params=_params("parallel", "parallel"),
        name="matmul",
    )(a, w)


def _rope_pair(r2, cos_ref, sin_ref):
    return r2 * cos_ref[...] + pltpu.roll(r2, MLA_ROPE, 1) * sin_ref[...]


def _with_ones_rows(vt):
    return jnp.concatenate([vt, jnp.ones((ONES_ROWS, vt.shape[1]), vt.dtype)], axis=0)


def _qup_kernel(cq_ref, g_ref, w_ref, cos_ref, sin_ref, o_ref, an_ref, *, scale):
    @pl.when(pl.program_id(1) == 0)
    def _():
        an_ref[...] = _rms(cq_ref[...], g_ref[...]).astype(BF16)

    an = an_ref[...]
    for hh in range(w_ref.shape[0]):
        res = _dot(an, w_ref[hh])
        roped = _rope_pair(res[:, MLA_NOPE:], cos_ref, sin_ref)
        q = jnp.concatenate([res[:, :MLA_NOPE], roped], axis=1) * scale
        o_ref[hh, 0] = q.T.astype(o_ref.dtype)


def mla_q_up(proj_a, g, w_q, cos_p, sin_p, seq, scale):
    t = proj_a.shape[0]
    heads = w_q.shape[0]
    hb = UP_HEADS_PER_STEP
    tm = _tile(seq, ATTN_TILE)
    per_b = seq // tm
    return pl.pallas_call(
        functools.partial(_qup_kernel, scale=scale),
        grid=(t // tm, heads // hb),
        in_specs=[pl.BlockSpec((tm, Q_LORA), lambda i, h: (i, 0)),
                  pl.BlockSpec((1, Q_LORA), lambda i, h: (0, 0)),
                  pl.BlockSpec((hb, Q_LORA, MLA_DK_PAD), lambda i, h: (h, 0, 0)),
                  pl.BlockSpec((tm, LANE), lambda i, h: (i % per_b, 0)),
                  pl.BlockSpec((tm, LANE), lambda i, h: (i % per_b, 0))],
        out_specs=pl.BlockSpec((hb, 1, MLA_DK_PAD, tm), lambda i, h: (h, i, 0, 0)),
        out_shape=jax.ShapeDtypeStruct((heads, t // tm, MLA_DK_PAD, tm), BF16),
        scratch_shapes=[pltpu.VMEM((tm, Q_LORA), BF16)],
        compiler_params=_params("parallel", "arbitrary"),
        name="mla_q_up",
    )(proj_a, g.reshape(1, Q_LORA), w_q, cos_p, sin_p)


def _kvup_kernel(ckv_ref, g_ref, w_ref, kpe_ref, cos_ref, sin_ref, k_ref, vt_ref, an_ref):
    @pl.when(pl.program_id(1) == 0)
    def _():
        an_ref[...] = _rms(ckv_ref[...], g_ref[...]).astype(BF16)

    an = an_ref[...]
    k_rot = _rope_pair(kpe_ref[...], cos_ref, sin_ref)
    hw = MLA_NOPE + MLA_V
    for hh in range(k_ref.shape[0]):
        res = _dot(an, w_ref[:, hh * hw:(hh + 1) * hw])
        k_ref[hh] = jnp.concatenate([res[:, :MLA_NOPE], k_rot], axis=1).astype(k_ref.dtype)
        vt_ref[hh, 0] = _with_ones_rows(res[:, MLA_NOPE:].T).astype(vt_ref.dtype)


def mla_kv_up(proj_a, g, w_kv, cos_p, sin_p, seq):
    t = proj_a.shape[0]
    hw = MLA_NOPE + MLA_V
    heads = w_kv.shape[1] // hw
    hb = UP_HEADS_PER_STEP
    tm = _tile(seq, ATTN_TILE)
    per_b = seq // tm
    return pl.pallas_call(
        _kvup_kernel,
        grid=(t // tm, heads // hb),
        in_specs=[pl.BlockSpec((tm, KV_LORA), lambda i, h: (i, Q_LORA // KV_LORA)),
                  pl.BlockSpec((1, KV_LORA), lambda i, h: (0, 0)),
                  pl.BlockSpec((KV_LORA, hb * hw), lambda i, h: (0, h)),
                  pl.BlockSpec((tm, LANE), lambda i, h: (i, (Q_LORA + KV_LORA) // LANE)),
                  pl.BlockSpec((tm, LANE), lambda i, h: (i % per_b, 0)),
                  pl.BlockSpec((tm, LANE), lambda i, h: (i % per_b, 0))],
        out_specs=[pl.BlockSpec((hb, tm, MLA_DK_PAD), lambda i, h: (h, i, 0)),
                   pl.BlockSpec((hb, 1, MLA_V + ONES_ROWS, tm), lambda i, h: (h, i, 0, 0))],
        out_shape=[jax.ShapeDtypeStruct((heads, t, MLA_DK_PAD), BF16),
                   jax.ShapeDtypeStruct((heads, t // tm, MLA_V + ONES_ROWS, tm), BF16)],
        scratch_shapes=[pltpu.VMEM((tm, KV_LORA), BF16)],
        compiler_params=_params("parallel", "arbitrary"),
        name="mla_kv_up",
    )(proj_a, g.reshape(1, KV_LORA), w_kv, proj_a, cos_p, sin_p)


def _pv(vts, p):
    rows = p.shape[0] // len(vts)
    out = _dot(vts[0], p[:rows].astype(vts[0].dtype))
    for n in range(1, len(vts)):
        out = out + _dot(vts[n], p[n * rows:(n + 1) * rows].astype(vts[n].dtype))
    return out


def _softmax_first(s, vts, m_ref, acc_ref):
    m = jnp.max(s, axis=0, keepdims=True)
    m_ref[...] = m
    acc_ref[...] = _pv(vts, jnp.exp(s - m))


def _softmax_step(s, vts, m_ref, acc_ref):
    m_prev = m_ref[...]
    m_new = jnp.maximum(m_prev, jnp.max(s, axis=0, keepdims=True))
    alpha = jnp.exp(m_prev - m_new)
    acc_ref[...] = alpha * acc_ref[...] + _pv(vts, jnp.exp(s - m_new))
    m_ref[...] = m_new


def _softmax_finish(acc_ref, dv):
    acc = acc_ref[...]
    return (acc[:dv, :] / acc[dv:dv + 1, :]).T


def _causal_mask_t(s):
    key = lax.broadcasted_iota(jnp.int32, s.shape, 0)
    qry = lax.broadcasted_iota(jnp.int32, s.shape, 1)
    return jnp.where(key <= qry, s, NEG_INF)


def _mla_attn_kernel(qt_ref, k_ref, vt_ref, o_ref, m_ref, acc_ref, *, tq, dv):
    i = pl.program_id(2)
    n_h = qt_ref.shape[0]
    qts = [qt_ref[h, 0] for h in range(n_h)]

    def scores(h, j, n):
        off = pl.multiple_of(j * tq, tq)
        return _dot(k_ref[h, pl.ds(off, n * tq), :], qts[h])

    for h in range(n_h):
        _softmax_first(_causal_mask_t(scores(h, i, 1)), [vt_ref[h, i]], m_ref.at[h], acc_ref.at[h])

    @pl.when(i % 2 == 1)
    def _():
        for h in range(n_h):
            _softmax_step(scores(h, i - 1, 1), [vt_ref[h, i - 1]], m_ref.at[h], acc_ref.at[h])

    def body(c, carry):
        for h in range(n_h):
            _softmax_step(scores(h, 2 * c, 2), [vt_ref[h, 2 * c], vt_ref[h, 2 * c + 1]],
                          m_ref.at[h], acc_ref.at[h])
        return carry

    lax.fori_loop(0, i // 2, body, 0)
    for h in range(n_h):
        o_ref[:, h * dv:(h + 1) * dv] = _softmax_finish(acc_ref.at[h], dv).astype(o_ref.dtype)


def mla_attention(qt, k, vt, batch, seq, out_dtype):
    heads, _, dk, tq = qt.shape
    t = k.shape[1]
    dvx = vt.shape[2]
    dv = dvx - ONES_ROWS
    nq = seq // tq
    hb = ATTN_HEADS_PER_STEP
    return pl.pallas_call(
        functools.partial(_mla_attn_kernel, tq=tq, dv=dv),
        grid=(batch, heads // hb, nq),
        in_specs=[pl.BlockSpec((hb, 1, dk, tq), lambda b, h, i: (h, b * nq + i, 0, 0)),
                  pl.BlockSpec((hb, seq, dk), lambda b, h, i: (h, b, 0)),
                  pl.BlockSpec((hb, nq, dvx, tq), lambda b, h, i: (h, b, 0, 0))],
        out_specs=pl.BlockSpec((tq, hb * dv), lambda b, h, i: (b * nq + i, h)),
        out_shape=jax.ShapeDtypeStruct((t, heads * dv), out_dtype),
        scratch_shapes=[pltpu.VMEM((hb, 1, tq), F32), pltpu.VMEM((hb, dvx, tq), F32)],
        compiler_params=_params("parallel", "parallel", "arbitrary"),
        name="mla_attention",
    )(qt, k, vt)


def _conv_kernel(b_ref, c_ref, h_ref, w_ref, o_ref, carry_ref, *, per_b):
    i = pl.program_id(0)

    @pl.when(i % per_b == 0)
    def _():
        carry_ref[...] = jnp.zeros_like(carry_ref)

    tm = o_ref.shape[0]
    w0, w1, w2 = w_ref[0:1, :], w_ref[1:2, :], w_ref[2:3, :]
    u = c_ref[...] * h_ref[...]
    y = w0 * pltpu.roll(u, 2, 0) + w1 * pltpu.roll(u, 1, 0) + w2 * u
    o_ref[...] = (b_ref[...] * y).astype(o_ref.dtype)
    u8 = u[0:SUBLANE, :]
    tail = carry_ref[...]
    r8 = lax.broadcasted_iota(jnp.int32, u8.shape, 0)
    p1 = jnp.where(r8 < 1, pltpu.roll(tail, 1, 0), pltpu.roll(u8, 1, 0))
    p2 = jnp.where(r8 < 2, pltpu.roll(tail, 2, 0), pltpu.roll(u8, 2, 0))
    y8 = w0 * p2 + w1 * p1 + w2 * u8
    o_ref[0:SUBLANE, :] = (b_ref[0:SUBLANE, :] * y8).astype(o_ref.dtype)
    carry_ref[...] = u[tm - SUBLANE:tm, :]


def conv_mixer(bch, conv_w, seq, out_dtype):
    t = bch.shape[0]
    ch = conv_w.shape[1]
    tm = _tile(seq, 256)
    per_b = seq // tm
    return pl.pallas_call(
        functools.partial(_conv_kernel, per_b=per_b),
        grid=(t // tm,),
        in_specs=[pl.BlockSpec((tm, ch), lambda i: (i, 0)),
                  pl.BlockSpec((tm, ch), lambda i: (i, 1)),
                  pl.BlockSpec((tm, ch), lambda i: (i, 2)),
                  pl.BlockSpec((CONV_W, ch), lambda i: (0, 0))],
        out_specs=pl.BlockSpec((tm, ch), lambda i: (i, 0)),
        out_shape=jax.ShapeDtypeStruct((t, ch), out_dtype),
        scratch_shapes=[pltpu.VMEM((SUBLANE, ch), F32)],
        compiler_params=_params("arbitrary"),
        name="conv_mixer",
    )(bch, bch, bch, conv_w)


def _moba_prep_kernel(q_ref, k_ref, v_ref, cos_ref, sin_ref, qo_ref, ko_ref, vo_ref, km_ref):
    heads = ko_ref.shape[0]
    cos, sin = cos_ref[...], sin_ref[...]
    n_sub = km_ref.shape[0]
    means = [[] for _ in range(n_sub)]
    for h in range(heads):
        sl = slice(h * MOBA_HD, (h + 1) * MOBA_HD)
        qh = q_ref[:, sl]
        kh = k_ref[:, sl]
        qr = qh * cos + pltpu.roll(qh, MOBA_HD // 2, 1) * sin
        qo_ref[h, 0] = qr.T.astype(qo_ref.dtype)
        kr = kh * cos + pltpu.roll(kh, MOBA_HD // 2, 1) * sin
        ko_ref[h] = kr.astype(ko_ref.dtype)
        vo_ref[h, 0] = _with_ones_rows(v_ref[:, sl].T).astype(vo_ref.dtype)
        for s in range(n_sub):
            means[s].append(jnp.mean(kr[s * MOBA_BLOCK:(s + 1) * MOBA_BLOCK], axis=0, keepdims=True))
    for s in range(n_sub):
        km_ref[s] = jnp.concatenate(means[s], axis=0)


def moba_prep(qkv, cos_f, sin_s, seq):
    t = qkv.shape[0]
    width = qkv.shape[1] // 3
    heads = width // MOBA_HD
    tm = 2 * MOBA_BLOCK
    assert seq % tm == 0
    per_b = seq // tm
    nt = t // tm
    return pl.pallas_call(
        _moba_prep_kernel,
        grid=(nt,),
        in_specs=[pl.BlockSpec((tm, width), lambda i: (i, 0)),
                  pl.BlockSpec((tm, width), lambda i: (i, 1)),
                  pl.BlockSpec((tm, width), lambda i: (i, 2)),
                  pl.BlockSpec((tm, MOBA_HD), lambda i: (i % per_b, 0)),
                  pl.BlockSpec((tm, MOBA_HD), lambda i: (i % per_b, 0))],
        out_specs=[pl.BlockSpec((heads, 1, MOBA_HD, tm), lambda i: (0, i, 0, 0)),
                   pl.BlockSpec((heads, tm, MOBA_HD), lambda i: (0, i, 0)),
                   pl.BlockSpec((heads, 1, MOBA_HD + ONES_ROWS, tm), lambda i: (0, i, 0, 0)),
                   pl.BlockSpec((2, heads, MOBA_HD), lambda i: (i, 0, 0))],
        out_shape=[jax.ShapeDtypeStruct((heads, nt, MOBA_HD, tm), BF16),
                   jax.ShapeDtypeStruct((heads, t, MOBA_HD), BF16),
                   jax.ShapeDtypeStruct((heads, nt, MOBA_HD + ONES_ROWS, tm), BF16),
                   jax.ShapeDtypeStruct((2 * nt, heads, MOBA_HD), F32)],
        compiler_params=_params("arbitrary"),
        name="moba_prep",
    )(qkv, qkv, qkv, cos_f, sin_s)


def _moba_attn_kernel(qt_ref, k_ref, vt_ref, km_ref, o_ref, m_ref, acc_ref, sel_ref, *, scale, dv):
    i = pl.program_id(2)
    n_h = qt_ref.shape[0]
    n_blk = km_ref.shape[2]
    tq = qt_ref.shape[3]
    blk_w = MOBA_BLOCK
    blk = lax.broadcasted_iota(jnp.int32, (n_blk, tq), 0)
    own = 2 * i + (lax.broadcasted_iota(jnp.int32, (n_blk, tq), 1) >= blk_w).astype(jnp.int32)
    past = blk < own
    qts = []
    for h in range(n_h):
        qtf = qt_ref[h, 0].astype(F32)
        g = _dot(km_ref[0, h], qtf, precision=lax.Precision.HIGHEST)
        g = jnp.where(past, g, NEG_INF)
        sel = jnp.zeros(g.shape, F32)
        for _ in range(min(MOBA_TOPK, n_blk)):
            mx = jnp.max(g, axis=0, keepdims=True)
            first = jnp.min(jnp.where(g == mx, blk, n_blk), axis=0, keepdims=True)
            hit = blk == first
            sel = jnp.where(hit, 1.0, sel)
            g = jnp.where(hit, NEG_INF, g)
        sel_ref[h] = jnp.where(past, sel, 0.0)
        qts.append((qtf * scale).astype(k_ref.dtype))

    def scores(h, p, n):
        off = pl.multiple_of(p * tq, tq)
        return _dot(k_ref[h, pl.ds(off, n * tq), :], qts[h])

    def chosen(h, n):
        return sel_ref[h, pl.ds(n, 1), :] > 0.5

    def mask_past(h, s, first_blk):
        parts = [jnp.where(chosen(h, first_blk + n), s[n * blk_w:(n + 1) * blk_w], NEG_INF)
                 for n in range(s.shape[0] // blk_w)]
        return jnp.concatenate(parts, axis=0)

    key = lax.broadcasted_iota(jnp.int32, (blk_w, tq), 0)
    qry = lax.broadcasted_iota(jnp.int32, (blk_w, tq), 1)
    bot_ok = key <= qry - blk_w
    for h in range(n_h):
        s = scores(h, i, 1)
        chosen_lim = jnp.where(chosen(h, 2 * i), blk_w, -1)
        top_ok = key <= jnp.where(qry < blk_w, qry, chosen_lim)
        s = jnp.concatenate([jnp.where(top_ok, s[:blk_w], NEG_INF),
                             jnp.where(bot_ok, s[blk_w:], NEG_INF)], axis=0)
        _softmax_first(s, [vt_ref[h, i]], m_ref.at[h], acc_ref.at[h])

    @pl.when(i % 2 == 1)
    def _():
        for h in range(n_h):
            s = mask_past(h, scores(h, i - 1, 1), 2 * (i - 1))
            _softmax_step(s, [vt_ref[h, i - 1]], m_ref.at[h], acc_ref.at[h])

    def body(c, carry):
        for h in range(n_h):
            s = mask_past(h, scores(h, 2 * c, 2), 4 * c)
            _softmax_step(s, [vt_ref[h, 2 * c], vt_ref[h, 2 * c + 1]], m_ref.at[h], acc_ref.at[h])
        return carry

    lax.fori_loop(0, i // 2, body, 0)
    for h in range(n_h):
        o_ref[:, h * dv:(h + 1) * dv] = _softmax_finish(acc_ref.at[h], dv).astype(o_ref.dtype)


def moba_attention(qt, k, vt, k_mean, batch, seq, out_dtype):
    heads, _, hd, tq = qt.shape
    t = k.shape[1]
    dvx = vt.shape[2]
    dv = dvx - ONES_ROWS
    n_blk = seq // MOBA_BLOCK
    nq = seq // tq
    hb = ATTN_HEADS_PER_STEP
    return pl.pallas_call(
        functools.partial(_moba_attn_kernel, scale=hd ** -0.5, dv=dv),
        grid=(batch, heads // hb, nq),
        in_specs=[pl.BlockSpec((hb, 1, hd, tq), lambda b, h, i: (h, b * nq + i, 0, 0)),
                  pl.BlockSpec((hb, seq, hd), lambda b, h, i: (h, b, 0)),
                  pl.BlockSpec((hb, nq, dvx, tq), lambda b, h, i: (h, b, 0, 0)),
                  pl.BlockSpec((1, hb, n_blk, hd), lambda b, h, i: (b, h, 0, 0))],
        out_specs=pl.BlockSpec((tq, hb * dv), lambda b, h, i: (b * nq + i, h)),
        out_shape=jax.ShapeDtypeStruct((t, heads * dv), out_dtype),
        scratch_shapes=[pltpu.VMEM((hb, 1, tq), F32), pltpu.VMEM((hb, dvx, tq), F32),
                        pltpu.VMEM((hb, n_blk, tq), F32)],
        compiler_params=_params("parallel", "parallel", "arbitrary"),
        name="moba_attention",
    )(qt, k, vt, k_mean)


def _out_proj_kernel(a_ref, b_ref, c_ref, ga_ref, gb_ref, gc_ref, w_ref, x_ref, gate_ref, o_ref,
                     y_ref):
    @pl.when(pl.program_id(1) == 0)
    def _():
        wa, wb = a_ref.shape[1], b_ref.shape[1]
        y_ref[:, 0:wa] = _rms(a_ref[...].astype(F32), ga_ref[...]).astype(y_ref.dtype)
        y_ref[:, wa:wa + wb] = _rms(b_ref[...].astype(F32), gb_ref[...]).astype(y_ref.dtype)
        y_ref[:, wa + wb:] = _rms(c_ref[...].astype(F32), gc_ref[...]).astype(y_ref.dtype)

    o_ref[...] = x_ref[...] + gate_ref[0] * _dot(y_ref[...], w_ref[...])


def out_proj(y_a, y_b, y_c, g, w, x2, gate, seq, tm=512, tn=1024):
    t = y_a.shape[0]
    wa, wb, wc = y_a.shape[1], y_b.shape[1], y_c.shape[1]
    kd, n = w.shape
    tm, tn = _tile(seq, tm), _tile(n, tn)
    per_b = seq // tm
    ga, gb, gc = g[:wa].reshape(1, wa), g[wa:wa + wb].reshape(1, wb), g[wa + wb:].reshape(1, wc)
    rows = lambda width: pl.BlockSpec((tm, width), lambda i, j: (i, 0))
    vec = lambda width: pl.BlockSpec((1, width), lambda i, j: (0, 0))
    return pl.pallas_call(
        _out_proj_kernel,
        grid=(t // tm, n // tn),
        in_specs=[rows(wa), rows(wb), rows(wc), vec(wa), vec(wb), vec(wc),
                  pl.BlockSpec((kd, tn), lambda i, j: (0, j)),
                  pl.BlockSpec((tm, tn), lambda i, j: (i, j)),
                  pl.BlockSpec((1, 1, tn), lambda i, j: (i // per_b, 0, j))],
        out_specs=pl.BlockSpec((tm, tn), lambda i, j: (i, j)),
        out_shape=jax.ShapeDtypeStruct((t, n), F32),
        scratch_shapes=[pltpu.VMEM((tm, kd), BF16)],
        compiler_params=_params("parallel", "arbitrary"),
        name="out_proj",
    )(y_a, y_b, y_c, ga, gb, gc, w, x2, gate)


def _ffn_up_kernel(bexp_ref, first_ref, rows_ref, x_ref, wg_ref, wu_ref, a_ref, wg_bf, wu_bf):
    b = pl.program_id(1)
    n_rows = rows_ref[b]
    half = x_ref.shape[0] // 2

    @pl.when((n_rows > 0) & (first_ref[b] == 1))
    def _():
        wg_bf[...] = wg_ref[0, 0].astype(BF16)
        wu_bf[...] = wu_ref[0, 0].astype(BF16)

    def act(x):
        gate = _dot(x, wg_bf[...])
        up = _dot(x, wu_bf[...])
        return (gate * jax.nn.sigmoid(gate) * up).astype(a_ref.dtype)

    @pl.when(n_rows > half)
    def _():
        a_ref[...] = act(x_ref[...])

    @pl.when((n_rows > 0) & (n_rows <= half))
    def _():
        a_ref[:half] = act(x_ref[:half])
        a_ref[half:] = jnp.zeros((half, a_ref.shape[1]), a_ref.dtype)

    @pl.when(n_rows == 0)
    def _():
        a_ref[...] = jnp.zeros_like(a_ref)


def _pack_bf16_pair(lo, hi):
    lo_bits = lax.bitcast_convert_type(lo.astype(BF16).astype(F32), PACKED) >> 16
    hi_bits = lax.bitcast_convert_type(hi.astype(BF16).astype(F32), PACKED) & jnp.uint32(0xFFFF0000)
    return hi_bits | lo_bits


def _unpack_bf16_pair(packed):
    lo = lax.bitcast_convert_type(packed << 16, F32)
    hi = lax.bitcast_convert_type(packed & jnp.uint32(0xFFFF0000), F32)
    return lo, hi


def _ffn_down_kernel(bexp_ref, first_ref, rows_ref, a_ref, wd_ref, o_ref, wd_bf):
    b = pl.program_id(1)
    n_rows = rows_ref[b]
    half = a_ref.shape[0] // 2

    @pl.when((n_rows > 0) & (first_ref[b] == 1))
    def _():
        wd_bf[...] = wd_ref[0, 0].astype(BF16)

    @pl.when(n_rows > half)
    def _():
        y = _dot(a_ref[...], wd_bf[...])
        o_ref[...] = _pack_bf16_pair(y[:half], y[half:])

    @pl.when((n_rows > 0) & (n_rows <= half))
    def _():
        y = _dot(a_ref[:half], wd_bf[...])
        o_ref[...] = _pack_bf16_pair(y, jnp.zeros_like(y))

    @pl.when(n_rows == 0)
    def _():
        o_ref[...] = jnp.zeros_like(o_ref)


def expert_ffn(xs, block_exp, block_first, block_rows, w_gate, w_up, w_down, layer):
    n_slots, d = xs.shape
    ff = w_gate.shape[3]
    fh, dh = ff // FF_SPLIT, d // FF_SPLIT
    n_blocks = n_slots // MOE_BLOCK
    up_spec = pltpu.PrefetchScalarGridSpec(
        num_scalar_prefetch=3,
        grid=(FF_SPLIT, n_blocks),
        in_specs=[pl.BlockSpec((MOE_BLOCK, d), lambda j, b, be, fi, nu: (b, 0)),
                  pl.BlockSpec((1, 1, d, fh), lambda j, b, be, fi, nu: (layer, be[b], 0, j)),
                  pl.BlockSpec((1, 1, d, fh), lambda j, b, be, fi, nu: (layer, be[b], 0, j))],
        out_specs=pl.BlockSpec((MOE_BLOCK, fh), lambda j, b, be, fi, nu: (b, j)),
        scratch_shapes=[pltpu.VMEM((d, fh), BF16), pltpu.VMEM((d, fh), BF16)],
    )
    act = pl.pallas_call(
        _ffn_up_kernel,
        grid_spec=up_spec,
        out_shape=jax.ShapeDtypeStruct((n_slots, ff), BF16),
        compiler_params=_params("arbitrary", "arbitrary"),
        name="expert_ffn_up",
    )(block_exp, block_first, block_rows, xs, w_gate, w_up)
    down_spec = pltpu.PrefetchScalarGridSpec(
        num_scalar_prefetch=3,
        grid=(FF_SPLIT, n_blocks),
        in_specs=[pl.BlockSpec((MOE_BLOCK, ff), lambda j, b, be, fi, nu: (b, 0)),
                  pl.BlockSpec((1, 1, ff, dh), lambda j, b, be, fi, nu: (layer, be[b], 0, j))],
        out_specs=pl.BlockSpec((MOE_BLOCK // 2, dh), lambda j, b, be, fi, nu: (b, j)),
        scratch_shapes=[pltpu.VMEM((ff, dh), BF16)],
    )
    return pl.pallas_call(
        _ffn_down_kernel,
        grid_spec=down_spec,
        out_shape=jax.ShapeDtypeStruct((n_slots // 2, d), PACKED),
        compiler_params=_params("arbitrary", "arbitrary"),
        name="expert_ffn_down",
    )(block_exp, block_first, block_rows, act, w_down)


def _start_row_copies(idx_ref, base, n_rows, src_ref, dst_ref, sem):
    def issue(r, carry):
        row = idx_ref[base + r]
        pltpu.make_async_copy(src_ref.at[pl.ds(row, 1)], dst_ref.at[pl.ds(r, 1)], sem).start()
        return carry

    lax.fori_loop(0, n_rows, issue, 0, unroll=8)


def _wait_row_copies(n_rows, src_ref, dst_ref, sem):
    pltpu.make_async_copy(src_ref.at[pl.ds(0, n_rows)], dst_ref, sem).wait()


def _gather_cast_kernel(idx_ref, nused_ref, src_ref, o_ref, buf, sem):
    b = pl.program_id(0)
    rows = o_ref.shape[0]
    n_used = nused_ref[0]

    @pl.when((b == 0) & (n_used > 0))
    def _():
        _start_row_copies(idx_ref, 0, rows, src_ref, buf.at[0], sem.at[0])

    @pl.when(b + 1 < n_used)
    def _():
        nxt = (b + 1) % 2
        _start_row_copies(idx_ref, (b + 1) * rows, rows, src_ref, buf.at[nxt], sem.at[nxt])

    @pl.when(b < n_used)
    def _():
        cur = b % 2
        _wait_row_copies(rows, src_ref, buf.at[cur], sem.at[cur])
        o_ref[...] = buf[cur].astype(o_ref.dtype)

    @pl.when(b >= n_used)
    def _():
        o_ref[...] = jnp.zeros_like(o_ref)


def gather_rows_cast(src, idx, n_used_tiles, out_dtype):
    d = src.shape[1]
    m = idx.shape[0]
    rows = GATHER_ROWS
    grid_spec = pltpu.PrefetchScalarGridSpec(
        num_scalar_prefetch=2,
        grid=(m // rows,),
        in_specs=[pl.BlockSpec(memory_space=pl.ANY)],
        out_specs=pl.BlockSpec((rows, d), lambda b, ix, nu: (b, 0)),
        scratch_shapes=[pltpu.VMEM((2, rows, d), src.dtype), pltpu.SemaphoreType.DMA((2,))],
    )
    return pl.pallas_call(
        _gather_cast_kernel,
        grid_spec=grid_spec,
        out_shape=jax.ShapeDtypeStruct((m, d), out_dtype),
        compiler_params=_params("arbitrary"),
        name="gather_rows_cast",
    )(idx, n_used_tiles, src)


def _combine_kernel(pos_ref, x_ref, gate_ref, w_ref, half_ref, g_ref, sc_ref, sh_ref, ys_ref,
                    *rest, n_tok, final):
    if final:
        o_ref, ybuf, sem = rest
    else:
        o_ref, h_ref, ybuf, sem = rest
    i = pl.program_id(0)
    n_tiles = pl.num_programs(0)
    tm = x_ref.shape[0]

    def start_tile(tile, slot):
        for kk in range(TOP_K):
            _start_row_copies(pos_ref, kk * n_tok + tile * tm, tm, ys_ref, ybuf.at[slot, kk],
                              sem.at[slot])

    @pl.when(i == 0)
    def _():
        start_tile(0, 0)

    @pl.when(i + 1 < n_tiles)
    def _():
        start_tile(i + 1, (i + 1) % 2)

    cur = i % 2
    for kk in range(TOP_K):
        _wait_row_copies(tm, ys_ref, ybuf.at[cur, kk], sem.at[cur])
    w = w_ref[...]
    upper = half_ref[...] > 0.5
    acc = None
    for kk in range(TOP_K):
        lo, hi = _unpack_bf16_pair(ybuf[cur, kk])
        term = w[:, kk:kk + 1] * jnp.where(upper[:, kk:kk + 1], hi, lo)
        acc = term if acc is None else acc + term
    x_new = x_ref[...] + gate_ref[0] * acc
    if final:
        o_ref[...] = _rms(x_new, g_ref[...])
    else:
        o_ref[...] = x_new
        h_ref[...] = (_rms(x_new, g_ref[...]) * sc_ref[0] + sh_ref[0]).astype(h_ref.dtype)


def moe_combine(x2, gate, ys, pos, w_tok, half_tok, seq, norm_g, sc1p=None, sh=None):
    t, d = x2.shape
    tm = _tile(seq, GATHER_ROWS)
    per_b = seq // tm
    final = sc1p is None
    if final:
        sc1p = sh = jnp.zeros((x2.shape[0] // seq, 1, d), F32)
    row = pl.BlockSpec((tm, d), lambda i, ps: (i, 0))
    mod = pl.BlockSpec((1, 1, d), lambda i, ps: (i // per_b, 0, 0))
    tok = pl.BlockSpec((tm, TOP_K), lambda i, ps: (i, 0))
    grid_spec = pltpu.PrefetchScalarGridSpec(
        num_scalar_prefetch=1,
        grid=(t // tm,),
        in_specs=[row, mod, tok, tok, pl.BlockSpec((1, d), lambda i, ps: (0, 0)), mod, mod,
                  pl.BlockSpec(memory_space=pl.ANY)],
        out_specs=row if final else [row, row],
        scratch_shapes=[pltpu.VMEM((2, TOP_K, tm, d), ys.dtype), pltpu.SemaphoreType.DMA((2,))],
    )
    out_shape = jax.ShapeDtypeStruct((t, d), F32)
    return pl.pallas_call(
        functools.partial(_combine_kernel, n_tok=t, final=final),
        grid_spec=grid_spec,
        out_shape=out_shape if final else [out_shape, jax.ShapeDtypeStruct((t, d), BF16)],
        compiler_params=_params("arbitrary"),
        name="moe_combine",
    )(pos, x2, gate, w_tok, half_tok, norm_g.reshape(1, d), sc1p, sh, ys)


def _group_by_expert(e_idx):
    n_tok = e_idx.shape[1]
    n_asg = n_tok * TOP_K
    flat_e = e_idx.reshape(n_asg)
    onehot = (flat_e[:, None] == jnp.arange(N_EXPERTS)[None, :]).astype(jnp.int32)
    running = jnp.cumsum(onehot, axis=0)
    rank = jnp.sum(running * onehot, axis=1) - 1
    sizes = running[-1]
    padded = (sizes + MOE_BLOCK - 1) // MOE_BLOCK * MOE_BLOCK
    pad_end = jnp.cumsum(padded)
    pad_start = pad_end - padded
    dest = (jnp.sum(pad_start[None, :] * onehot, axis=1) + rank).astype(jnp.int32)
    n_blocks = -(-n_asg // MOE_BLOCK) + N_EXPERTS
    n_slots = n_blocks * MOE_BLOCK
    slot_tok = (jnp.arange(n_slots, dtype=jnp.int32) % n_tok).at[dest].set(
        jnp.arange(n_asg, dtype=jnp.int32) % n_tok, mode="promise_in_bounds")
    block_start = jnp.arange(n_blocks, dtype=jnp.int32) * MOE_BLOCK
    block_exp = jnp.minimum(jnp.sum(block_start[:, None] >= pad_end[None, :], axis=1),
                            N_EXPERTS - 1).astype(jnp.int32)
    n_used = (pad_end[-1] // MOE_BLOCK).astype(jnp.int32)
    block_exp = jnp.where(jnp.arange(n_blocks) < n_used, block_exp, block_exp[n_used - 1])
    prev_exp = jnp.concatenate([jnp.full((1,), -1, jnp.int32), block_exp[:-1]])
    block_first = (block_exp != prev_exp).astype(jnp.int32)
    block_rows = jnp.clip(pad_start[block_exp] + sizes[block_exp] - block_start, 0, MOE_BLOCK)
    block_rows = jnp.where(jnp.arange(n_blocks) < n_used, block_rows, 0).astype(jnp.int32)
    return (dest.reshape(TOP_K, n_tok), slot_tok, block_exp, block_first, block_rows,
            n_used.reshape(1))


def _rot_cols(w):
    half = w.shape[-1] // 2
    return jnp.concatenate([-w[..., half:], w[..., :half]], axis=-1)


def _prep_layer(w_in, w_uq, w_ukv, w_out):
    a_end = Q_LORA + KV_LORA + MLA_ROPE
    b_end = a_end + 3 * CONV_CH
    w_kpe = w_in[:, Q_LORA + KV_LORA:a_end]
    w_a = jnp.concatenate([w_in[:, :a_end], _rot_cols(w_kpe)], axis=1).astype(BF16)
    w_b = w_in[:, a_end:b_end].astype(BF16)
    w_c = w_in[:, b_end:].astype(BF16)
    wq = w_uq.reshape(Q_LORA, MLA_HEADS, MLA_NOPE + MLA_ROPE)
    wq_rope = wq[..., MLA_NOPE:]
    wq = jnp.concatenate([wq, _rot_cols(wq_rope)], axis=-1).transpose(1, 0, 2).astype(BF16)
    return dict(w_a=w_a, w_b=w_b, w_c=w_c, w_q=wq, w_kv=w_ukv.astype(BF16),
                w_out=w_out.astype(BF16))


def _rope_tables(seq, dim):
    inv = 1.0 / (ROPE_THETA ** (jnp.arange(0, dim, 2, dtype=F32) / dim))
    ang = jnp.arange(seq, dtype=F32)[:, None] * inv[None, :]
    return jnp.cos(ang), jnp.sin(ang)


def kernel(x, c, w_mod, mod_table, mix_norm_g, w_in, q_norm_g, kv_norm_g, w_uq, w_ukv, conv_w,
           group_norm_g, w_out, ffn_norm_g, w_router, router_bias, w_gate, w_up, w_down,
           final_norm_g):
    batch, seq, d = x.shape
    depth = w_in.shape[0]
    t = batch * seq
    x2 = x.reshape(t, d)

    cos_a, sin_a = _rope_tables(seq, MLA_ROPE)
    zeros_a = jnp.zeros_like(cos_a)
    cos_p = jnp.concatenate([cos_a, cos_a, zeros_a, zeros_a], axis=1)
    sin_p = jnp.concatenate([sin_a, sin_a, zeros_a, zeros_a], axis=1)
    cos_b, sin_b = _rope_tables(seq, MOBA_HD)
    cos_f = jnp.concatenate([cos_b, cos_b], axis=1)
    sin_s = jnp.concatenate([-sin_b, sin_b], axis=1)

    c_pad = jnp.zeros((SUBLANE, d), F32).at[:batch].set(c)
    mod_shared = mod_matmul(c_pad, w_mod)[:batch].reshape(batch, N_MOD, d)

    wr_t = w_router.T
    wr_hi = wr_t.astype(BF16)
    wr_lo = (wr_t - wr_hi.astype(F32)).astype(BF16)
    router = (wr_hi, wr_lo, router_bias.astype(F32).reshape(N_EXPERTS, 1))
    mla_scale = (MLA_NOPE + MLA_ROPE) ** -0.5

    mods = []
    for l in range(depth):
        mod = mod_shared + mod_table[l][None]
        mods.append([mod[:, i][:, None, :] for i in range(N_MOD)])

    h = norm_mod(x2, mix_norm_g[0], 1.0 + mods[0][1], mods[0][0], seq)
    for l in range(depth):
        p = _prep_layer(w_in[l], w_uq[l], w_ukv[l], w_out[l])
        sh1, sc1, g1, sh2, sc2, g2 = mods[l]

        proj_a = matmul(h, p["w_a"], F32, tm=512)
        bch = matmul(h, p["w_b"], F32)
        qkv = matmul(h, p["w_c"], F32)

        q_a = mla_q_up(proj_a, q_norm_g[l], p["w_q"], cos_p, sin_p, seq, mla_scale)
        k_a, v_a = mla_kv_up(proj_a, kv_norm_g[l], p["w_kv"], cos_p, sin_p, seq)
        y_a = mla_attention(q_a, k_a, v_a, batch, seq, BF16)

        y_b = conv_mixer(bch, conv_w[l], seq, F32)

        q_c, k_c, v_c, k_mean = moba_prep(qkv, cos_f, sin_s, seq)
        n_blk = seq // MOBA_BLOCK
        k_mean = k_mean.reshape(batch, n_blk, MOBA_HEADS, MOBA_HD).transpose(0, 2, 1, 3)
        y_c = moba_attention(q_c, k_c, v_c, k_mean, batch, seq, BF16)

        x2 = out_proj(y_a, y_b, y_c, group_norm_g[l], p["w_out"], x2, g1, seq)

        h2, e_idx, gates = norm_mod(x2, ffn_norm_g[l], 1.0 + sc2, sh2, seq, out_dtype=F32,
                                    router=router)
        pos, slot_tok, block_exp, block_first, block_rows, n_used = _group_by_expert(e_idx)
        xs = gather_rows_cast(h2, slot_tok, n_used * (MOE_BLOCK // GATHER_ROWS), BF16)
        ys = expert_ffn(xs, block_exp, block_first, block_rows, w_gate, w_up, w_down, l)
        sub = pos % MOE_BLOCK
        ys_row = ((pos // MOE_BLOCK) * (MOE_BLOCK // 2) + sub % (MOE_BLOCK // 2)).reshape(-1)
        ys_half = (sub // (MOE_BLOCK // 2)).astype(F32).T
        if l + 1 < depth:
            nxt = mods[l + 1]
            x2, h = moe_combine(x2, g2, ys, ys_row, gates.T, ys_half, seq, mix_norm_g[l + 1],
                                1.0 + nxt[1], nxt[0])
        else:
            out = moe_combine(x2, g2, ys, ys_row, gates.T, ys_half, seq, final_norm_g)

    return out.reshape(batch, seq, d)
```

```python
import functools

import jax
import jax.numpy as jnp
from jax import lax
from jax.experimental import pallas as pl
from jax.experimental.pallas import tpu as pltpu

MLA_HEADS = 16
MLA_NOPE = 128
MLA_ROPE = 64
MLA_V = 128
Q_LORA = 1024
KV_LORA = 512
MLA_DK_PAD = 256
CONV_CH = 1024
CONV_W = 3
MOBA_HEADS = 8
MOBA_HD = 128
MOBA_BLOCK = 256
MOBA_TOPK = 3
ROPE_THETA = 10000.0
EPS = 1e-6
N_MOD = 6
N_EXPERTS = 16
N_GROUPS = 4
TOP_K = 2
MOE_BLOCK = 512
GATHER_ROWS = 256
FF_SPLIT = 2
UP_HEADS_PER_STEP = 4
ATTN_HEADS_PER_STEP = 4
ATTN_TILE = 512
ONES_ROWS = 16

V7X_VMEM_LIMIT_BYTES = 56 * 1024 * 1024
LANE = 128
SUBLANE = 8

F32 = jnp.float32
BF16 = jnp.bfloat16
PACKED = jnp.uint32
NEG_INF = float("-inf")


def _params(*sem):
    return pltpu.CompilerParams(dimension_semantics=sem, vmem_limit_bytes=V7X_VMEM_LIMIT_BYTES)


def _dot(a, b, precision=None):
    return jnp.dot(a, b, preferred_element_type=F32, precision=precision)


def _dot_nt(a, b):
    return lax.dot_general(a, b, (((1,), (1,)), ((), ())), preferred_element_type=F32)


def _tile(dim, want):
    return want if dim % want == 0 else dim


def _mod_kernel(c_ref, w0_ref, w1_ref, o_ref):
    @pl.when(pl.program_id(0) == 0)
    def _():
        o_ref[...] = jnp.zeros_like(o_ref)

    c = c_ref[...]
    a = (c * jax.nn.sigmoid(c)).astype(BF16)
    half = w0_ref.shape[1]
    o_ref[:, :half] += _dot(a, w0_ref[...].astype(BF16))
    o_ref[:, half:] += _dot(a, w1_ref[...].astype(BF16))


def mod_matmul(c_pad, w_mod):
    rows, d = c_pad.shape
    n = w_mod.shape[1]
    tk = _tile(d, LANE)
    return pl.pallas_call(
        _mod_kernel,
        grid=(d // tk,),
        in_specs=[pl.BlockSpec((rows, tk), lambda k: (0, k)),
                  pl.BlockSpec((tk, n // 2), lambda k: (k, 0)),
                  pl.BlockSpec((tk, n // 2), lambda k: (k, 1))],
        out_specs=pl.BlockSpec((rows, n), lambda k: (0, 0)),
        out_shape=jax.ShapeDtypeStruct((rows, n), F32),
        compiler_params=_params("arbitrary"),
        name="mod_matmul",
    )(c_pad, w_mod, w_mod)


def _rms(x, g):
    return x * lax.rsqrt(jnp.mean(x * x, axis=-1, keepdims=True) + EPS) * g


def _norm_mod_kernel(x_ref, g_ref, sc_ref, sh_ref, o_ref):
    y = _rms(x_ref[...], g_ref[...])
    o_ref[...] = (y * sc_ref[0] + sh_ref[0]).astype(o_ref.dtype)


def _top2_sum(a, b, c, d):
    hi1, lo1 = jnp.maximum(a, b), jnp.minimum(a, b)
    hi2, lo2 = jnp.maximum(c, d), jnp.minimum(c, d)
    return jnp.maximum(hi1, hi2) + jnp.maximum(jnp.minimum(hi1, hi2), jnp.maximum(lo1, lo2))


def _norm_router_kernel(x_ref, g_ref, sc_ref, sh_ref, whi_ref, wlo_ref, bias_ref,
                        o_ref, e_ref, gate_ref):
    h = _rms(x_ref[...], g_ref[...]) * sc_ref[0] + sh_ref[0]
    hi = h.astype(BF16)
    o_ref[...] = h.astype(o_ref.dtype)
    lo = (h - hi.astype(F32)).astype(BF16)
    whi, wlo = whi_ref[...], wlo_ref[...]
    logits = _dot_nt(whi, hi) + _dot_nt(wlo, hi) + _dot_nt(whi, lo)
    scores = jax.nn.sigmoid(logits)
    biased = scores + bias_ref[...]
    per_grp = N_EXPERTS // N_GROUPS
    b_rows = [biased[e:e + 1, :] for e in range(N_EXPERTS)]
    s_rows = [scores[e:e + 1, :] for e in range(N_EXPERTS)]
    grp_scores = [_top2_sum(*b_rows[g * per_grp:(g + 1) * per_grp]) for g in range(N_GROUPS)]
    best, grp = grp_scores[0], jnp.zeros(grp_scores[0].shape, jnp.int32)
    for g in range(1, N_GROUPS):
        better = grp_scores[g] > best
        grp = jnp.where(better, g, grp)
        best = jnp.where(better, grp_scores[g], best)
    b4, s4 = b_rows[:per_grp], s_rows[:per_grp]
    for g in range(1, N_GROUPS):
        in_g = grp == g
        b4 = [jnp.where(in_g, b_rows[g * per_grp + j], b4[j]) for j in range(per_grp)]
        s4 = [jnp.where(in_g, s_rows[g * per_grp + j], s4[j]) for j in range(per_grp)]
    v1, i1 = b4[0], jnp.zeros(grp.shape, jnp.int32)
    for j in range(1, per_grp):
        better = b4[j] > v1
        i1 = jnp.where(better, j, i1)
        v1 = jnp.where(better, b4[j], v1)
    v2, i2 = jnp.full(v1.shape, NEG_INF, F32), jnp.zeros(grp.shape, jnp.int32)
    for j in range(per_grp):
        better = (i1 != j) & (b4[j] > v2)
        i2 = jnp.where(better, j, i2)
        v2 = jnp.where(better, b4[j], v2)
    g1, g2 = s4[0], s4[0]
    for j in range(1, per_grp):
        g1 = jnp.where(i1 == j, s4[j], g1)
        g2 = jnp.where(i2 == j, s4[j], g2)
    total = g1 + g2
    e_ref[...] = jnp.concatenate([grp * per_grp + i1, grp * per_grp + i2], axis=0)
    gate_ref[...] = jnp.concatenate([g1 / total, g2 / total], axis=0)


def norm_mod(x2, g, sc1p, sh, seq, out_dtype=BF16, router=None):
    t, d = x2.shape
    tm = _tile(seq, 256)
    per_b = seq // tm
    row = pl.BlockSpec((tm, d), lambda i: (i, 0))
    vec = pl.BlockSpec((1, d), lambda i: (0, 0))
    mod = pl.BlockSpec((1, 1, d), lambda i: (i // per_b, 0, 0))
    g2 = g.reshape(1, d)
    if router is None:
        return pl.pallas_call(
            _norm_mod_kernel, grid=(t // tm,), in_specs=[row, vec, mod, mod], out_specs=row,
            out_shape=jax.ShapeDtypeStruct((t, d), out_dtype),
            compiler_params=_params("arbitrary"), name="norm_mod")(x2, g2, sc1p, sh)
    whi_t, wlo_t, bias = router
    ne = whi_t.shape[0]
    wspec = pl.BlockSpec((ne, d), lambda i: (0, 0))
    kspec = pl.BlockSpec((TOP_K, tm), lambda i: (0, i))
    return pl.pallas_call(
        _norm_router_kernel, grid=(t // tm,),
        in_specs=[row, vec, mod, mod, wspec, wspec, pl.BlockSpec((ne, 1), lambda i: (0, 0))],
        out_specs=[row, kspec, kspec],
        out_shape=[jax.ShapeDtypeStruct((t, d), out_dtype),
                   jax.ShapeDtypeStruct((TOP_K, t), jnp.int32),
                   jax.ShapeDtypeStruct((TOP_K, t), F32)],
        compiler_params=_params("arbitrary"), name="norm_router")(x2, g2, sc1p, sh, whi_t, wlo_t, bias)


def _mm_kernel(a_ref, w_ref, o_ref):
    o_ref[...] = _dot(a_ref[...], w_ref[...]).astype(o_ref.dtype)


def matmul(a, w, out_dtype, tm=1024, tn=1024):
    m, kd = a.shape
    n = w.shape[1]
    tm, tn = _tile(m, tm), _tile(n, tn)
    return pl.pallas_call(
        _mm_kernel,
        grid=(m // tm, n // tn),
        in_specs=[pl.BlockSpec((tm, kd), lambda i, j: (i, 0)),
                  pl.BlockSpec((kd, tn), lambda i, j: (0, j))],
        out_specs=pl.BlockSpec((tm, tn), lambda i, j: (i, j)),
        out_shape=jax.ShapeDtypeStruct((m, n), out_dtype),
        compiler_params=_params("parallel", "parallel"),
        name="matmul",
    )(a, w)


def _rope_pair(r2, cos_ref, sin_ref):
    return r2 * cos_ref[...] + pltpu.roll(r2, MLA_ROPE, 1) * sin_ref[...]


def _with_ones_rows(vt):
    return jnp.concatenate([vt, jnp.ones((ONES_ROWS, vt.shape[1]), vt.dtype)], axis=0)


def _qup_kernel(cq_ref, g_ref, w_ref, cos_ref, sin_ref, o_ref, an_ref, *, scale):
    @pl.when(pl.program_id(1) == 0)
    def _():
        an_ref[...] = _rms(cq_ref[...], g_ref[...]).astype(BF16)

    an = an_ref[...]
    for hh in range(w_ref.shape[0]):
        res = _dot(an, w_ref[hh])
        roped = _rope_pair(res[:, MLA_NOPE:], cos_ref, sin_ref)
        q = jnp.concatenate([res[:, :MLA_NOPE], roped], axis=1) * scale
        o_ref[hh, 0] = q.T.astype(o_ref.dtype)


def mla_q_up(proj_a, g, w_q, cos_p, sin_p, seq, scale):
    t = proj_a.shape[0]
    heads = w_q.shape[0]
    hb = UP_HEADS_PER_STEP
    tm = _tile(seq, ATTN_TILE)
    per_b = seq // tm
    return pl.pallas_call(
        functools.partial(_qup_kernel, scale=scale),
        grid=(t // tm, heads // hb),
        in_specs=[pl.BlockSpec((tm, Q_LORA), lambda i, h: (i, 0)),
                  pl.BlockSpec((1, Q_LORA), lambda i, h: (0, 0)),
                  pl.BlockSpec((hb, Q_LORA, MLA_DK_PAD), lambda i, h: (h, 0, 0)),
                  pl.BlockSpec((tm, LANE), lambda i, h: (i % per_b, 0)),
                  pl.BlockSpec((tm, LANE), lambda i, h: (i % per_b, 0))],
        out_specs=pl.BlockSpec((hb, 1, MLA_DK_PAD, tm), lambda i, h: (h, i, 0, 0)),
        out_shape=jax.ShapeDtypeStruct((heads, t // tm, MLA_DK_PAD, tm), BF16),
        scratch_shapes=[pltpu.VMEM((tm, Q_LORA), BF16)],
        compiler_params=_params("parallel", "arbitrary"),
        name="mla_q_up",
    )(proj_a, g.reshape(1, Q_LORA), w_q, cos_p, sin_p)


def _kvup_kernel(ckv_ref, g_ref, w_ref, kpe_ref, cos_ref, sin_ref, k_ref, vt_ref, an_ref):
    @pl.when(pl.program_id(1) == 0)
    def _():
        an_ref[...] = _rms(ckv_ref[...], g_ref[...]).astype(BF16)

    an = an_ref[...]
    k_rot = _rope_pair(kpe_ref[...], cos_ref, sin_ref)
    hw = MLA_NOPE + MLA_V
    for hh in range(k_ref.shape[0]):
        res = _dot(an, w_ref[:, hh * hw:(hh + 1) * hw])
        k_ref[hh] = jnp.concatenate([res[:, :MLA_NOPE], k_rot], axis=1).astype(k_ref.dtype)
        vt_ref[hh, 0] = _with_ones_rows(res[:, MLA_NOPE:].T).astype(vt_ref.dtype)


def mla_kv_up(proj_a, g, w_kv, cos_p, sin_p, seq):
    t = proj_a.shape[0]
    hw = MLA_NOPE + MLA_V
    heads = w_kv.shape[1] // hw
    hb = UP_HEADS_PER_STEP
    tm = _tile(seq, ATTN_TILE)
    per_b = seq // tm
    return pl.pallas_call(
        _kvup_kernel,
        grid=(t // tm, heads // hb),
        in_specs=[pl.BlockSpec((tm, KV_LORA), lambda i, h: (i, Q_LORA // KV_LORA)),
                  pl.BlockSpec((1, KV_LORA), lambda i, h: (0, 0)),
                  pl.BlockSpec((KV_LORA, hb * hw), lambda i, h: (0, h)),
                  pl.BlockSpec((tm, LANE), lambda i, h: (i, (Q_LORA + KV_LORA) // LANE)),
                  pl.BlockSpec((tm, LANE), lambda i, h: (i % per_b, 0)),
                  pl.BlockSpec((tm, LANE), lambda i, h: (i % per_b, 0))],
        out_specs=[pl.BlockSpec((hb, tm, MLA_DK_PAD), lambda i, h: (h, i, 0)),
                   pl.BlockSpec((hb, 1, MLA_V + ONES_ROWS, tm), lambda i, h: (h, i, 0, 0))],
        out_shape=[jax.ShapeDtypeStruct((heads, t, MLA_DK_PAD), BF16),
                   jax.ShapeDtypeStruct((heads, t // tm, MLA_V + ONES_ROWS, tm), BF16)],
        scratch_shapes=[pltpu.VMEM((tm, KV_LORA), BF16)],
        compiler_params=_params("parallel", "arbitrary"),
        name="mla_kv_up",
    )(proj_a, g.reshape(1, KV_LORA), w_kv, proj_a, cos_p, sin_p)


def _pv(vts, p):
    rows = p.shape[0] // len(vts)
    out = _dot(vts[0], p[:rows].astype(vts[0].dtype))
    for n in range(1, len(vts)):
        out = out + _dot(vts[n], p[n * rows:(n + 1) * rows].astype(vts[n].dtype))
    return out


def _softmax_first(s, vts, m_ref, acc_ref):
    m = jnp.max(s, axis=0, keepdims=True)
    m_ref[...] = m
    acc_ref[...] = _pv(vts, jnp.exp(s - m))


def _softmax_step(s, vts, m_ref, acc_ref):
    m_prev = m_ref[...]
    m_new = jnp.maximum(m_prev, jnp.max(s, axis=0, keepdims=True))
    alpha = jnp.exp(m_prev - m_new)
    acc_ref[...] = alpha * acc_ref[...] + _pv(vts, jnp.exp(s - m_new))
    m_ref[...] = m_new


def _softmax_finish(acc_ref, dv):
    acc = acc_ref[...]
    return (acc[:dv, :] / acc[dv:dv + 1, :]).T


def _causal_mask_t(s):
    key = lax.broadcasted_iota(jnp.int32, s.shape, 0)
    qry = lax.broadcasted_iota(jnp.int32, s.shape, 1)
    return jnp.where(key <= qry, s, NEG_INF)


def _mla_attn_kernel(qt_ref, k_ref, vt_ref, o_ref, m_ref, acc_ref, *, tq, dv):
    i = pl.program_id(2)
    n_h = qt_ref.shape[0]
    qts = [qt_ref[h, 0] for h in range(n_h)]

    def scores(h, j, n):
        off = pl.multiple_of(j * tq, tq)
        return _dot(k_ref[h, pl.ds(off, n * tq), :], qts[h])

    for h in range(n_h):
        _softmax_first(_causal_mask_t(scores(h, i, 1)), [vt_ref[h, i]], m_ref.at[h], acc_ref.at[h])

    @pl.when(i % 2 == 1)
    def _():
        for h in range(n_h):
            _softmax_step(scores(h, i - 1, 1), [vt_ref[h, i - 1]], m_ref.at[h], acc_ref.at[h])

    def body(c, carry):
        for h in range(n_h):
            _softmax_step(scores(h, 2 * c, 2), [vt_ref[h, 2 * c], vt_ref[h, 2 * c + 1]],
                          m_ref.at[h], acc_ref.at[h])
        return carry

    lax.fori_loop(0, i // 2, body, 0)
    for h in range(n_h):
        o_ref[:, h * dv:(h + 1) * dv] = _softmax_finish(acc_ref.at[h], dv).astype(o_ref.dtype)


def mla_attention(qt, k, vt, batch, seq, out_dtype):
    heads, _, dk, tq = qt.shape
    t = k.shape[1]
    dvx = vt.shape[2]
    dv = dvx - ONES_ROWS
    nq = seq // tq
    hb = ATTN_HEADS_PER_STEP
    return pl.pallas_call(
        functools.partial(_mla_attn_kernel, tq=tq, dv=dv),
        grid=(batch, heads // hb, nq),
        in_specs=[pl.BlockSpec((hb, 1, dk, tq), lambda b, h, i: (h, b * nq + i, 0, 0)),
                  pl.BlockSpec((hb, seq, dk), lambda b, h, i: (h, b, 0)),
                  pl.BlockSpec((hb, nq, dvx, tq), lambda b, h, i: (h, b, 0, 0))],
        out_specs=pl.BlockSpec((tq, hb * dv), lambda b, h, i: (b * nq + i, h)),
        out_shape=jax.ShapeDtypeStruct((t, heads * dv), out_dtype),
        scratch_shapes=[pltpu.VMEM((hb, 1, tq), F32), pltpu.VMEM((hb, dvx, tq), F32)],
        compiler_params=_params("parallel", "parallel", "arbitrary"),
        name="mla_attention",
    )(qt, k, vt)


def _conv_kernel(b_ref, c_ref, h_ref, w_ref, o_ref, carry_ref, *, per_b):
    i = pl.program_id(0)

    @pl.when(i % per_b == 0)
    def _():
        carry_ref[...] = jnp.zeros_like(carry_ref)

    tm = o_ref.shape[0]
    w0, w1, w2 = w_ref[0:1, :], w_ref[1:2, :], w_ref[2:3, :]
    u = c_ref[...] * h_ref[...]
    y = w0 * pltpu.roll(u, 2, 0) + w1 * pltpu.roll(u, 1, 0) + w2 * u
    o_ref[...] = (b_ref[...] * y).astype(o_ref.dtype)
    u8 = u[0:SUBLANE, :]
    tail = carry_ref[...]
    r8 = lax.broadcasted_iota(jnp.int32, u8.shape, 0)
    p1 = jnp.where(r8 < 1, pltpu.roll(tail, 1, 0), pltpu.roll(u8, 1, 0))
    p2 = jnp.where(r8 < 2, pltpu.roll(tail, 2, 0), pltpu.roll(u8, 2, 0))
    y8 = w0 * p2 + w1 * p1 + w2 * u8
    o_ref[0:SUBLANE, :] = (b_ref[0:SUBLANE, :] * y8).astype(o_ref.dtype)
    carry_ref[...] = u[tm - SUBLANE:tm, :]


def conv_mixer(bch, conv_w, seq, out_dtype):
    t = bch.shape[0]
    ch = conv_w.shape[1]
    tm = _tile(seq, 256)
    per_b = seq // tm
    return pl.pallas_call(
        functools.partial(_conv_kernel, per_b=per_b),
        grid=(t // tm,),
        in_specs=[pl.BlockSpec((tm, ch), lambda i: (i, 0)),
                  pl.BlockSpec((tm, ch), lambda i: (i, 1)),
                  pl.BlockSpec((tm, ch), lambda i: (i, 2)),
                  pl.BlockSpec((CONV_W, ch), lambda i: (0, 0))],
        out_specs=pl.BlockSpec((tm, ch), lambda i: (i, 0)),
        out_shape=jax.ShapeDtypeStruct((t, ch), out_dtype),
        scratch_shapes=[pltpu.VMEM((SUBLANE, ch), F32)],
        compiler_params=_params("arbitrary"),
        name="conv_mixer",
    )(bch, bch, bch, conv_w)


def _moba_prep_kernel(q_ref, k_ref, v_ref, cos_ref, sin_ref, qo_ref, ko_ref, vo_ref, km_ref):
    heads = ko_ref.shape[0]
    cos, sin = cos_ref[...], sin_ref[...]
    n_sub = km_ref.shape[0]
    means = [[] for _ in range(n_sub)]
    for h in range(heads):
        sl = slice(h * MOBA_HD, (h + 1) * MOBA_HD)
        qh = q_ref[:, sl]
        kh = k_ref[:, sl]
        qr = qh * cos + pltpu.roll(qh, MOBA_HD // 2, 1) * sin
        qo_ref[h, 0] = qr.T.astype(qo_ref.dtype)
        kr = kh * cos + pltpu.roll(kh, MOBA_HD // 2, 1) * sin
        ko_ref[h] = kr.astype(ko_ref.dtype)
        vo_ref[h, 0] = _with_ones_rows(v_ref[:, sl].T).astype(vo_ref.dtype)
        for s in range(n_sub):
            means[s].append(jnp.mean(kr[s * MOBA_BLOCK:(s + 1) * MOBA_BLOCK], axis=0, keepdims=True))
    for s in range(n_sub):
        km_ref[s] = jnp.concatenate(means[s], axis=0)


def moba_prep(qkv, cos_f, sin_s, seq):
    t = qkv.shape[0]
    width = qkv.shape[1] // 3
    heads = width // MOBA_HD
    tm = 2 * MOBA_BLOCK
    assert seq % tm == 0
    per_b = seq // tm
    nt = t // tm
    return pl.pallas_call(
        _moba_prep_kernel,
        grid=(nt,),
        in_specs=[pl.BlockSpec((tm, width), lambda i: (i, 0)),
                  pl.BlockSpec((tm, width), lambda i: (i, 1)),
                  pl.BlockSpec((tm, width), lambda i: (i, 2)),
                  pl.BlockSpec((tm, MOBA_HD), lambda i: (i % per_b, 0)),
                  pl.BlockSpec((tm, MOBA_HD), lambda i: (i % per_b, 0))],
        out_specs=[pl.BlockSpec((heads, 1, MOBA_HD, tm), lambda i: (0, i, 0, 0)),
                   pl.BlockSpec((heads, tm, MOBA_HD), lambda i: (0, i, 0)),
                   pl.BlockSpec((heads, 1, MOBA_HD + ONES_ROWS, tm), lambda i: (0, i, 0, 0)),
                   pl.BlockSpec((2, heads, MOBA_HD), lambda i: (i, 0, 0))],
        out_shape=[jax.ShapeDtypeStruct((heads, nt, MOBA_HD, tm), BF16),
                   jax.ShapeDtypeStruct((heads, t, MOBA_HD), BF16),
                   jax.ShapeDtypeStruct((heads, nt, MOBA_HD + ONES_ROWS, tm), BF16),
                   jax.ShapeDtypeStruct((2 * nt, heads, MOBA_HD), F32)],
        compiler_params=_params("arbitrary"),
        name="moba_prep",
    )(qkv, qkv, qkv, cos_f, sin_s)


def _moba_attn_kernel(qt_ref, k_ref, vt_ref, km_ref, o_ref, m_ref, acc_ref, sel_ref, *, scale, dv):
    i = pl.program_id(2)
    n_h = qt_ref.shape[0]
    n_blk = km_ref.shape[2]
    tq = qt_ref.shape[3]
    blk_w = MOBA_BLOCK
    blk = lax.broadcasted_iota(jnp.int32, (n_blk, tq), 0)
    own = 2 * i + (lax.broadcasted_iota(jnp.int32, (n_blk, tq), 1) >= blk_w).astype(jnp.int32)
    past = blk < own
    qts = []
    for h in range(n_h):
        qtf = qt_ref[h, 0].astype(F32)
        g = _dot(km_ref[0, h], qtf, precision=lax.Precision.HIGHEST)
        g = jnp.where(past, g, NEG_INF)
        sel = jnp.zeros(g.shape, F32)
        for _ in range(min(MOBA_TOPK, n_blk)):
            mx = jnp.max(g, axis=0, keepdims=True)
            first = jnp.min(jnp.where(g == mx, blk, n_blk), axis=0, keepdims=True)
            hit = blk == first
            sel = jnp.where(hit, 1.0, sel)
            g = jnp.where(hit, NEG_INF, g)
        sel_ref[h] = jnp.where(past, sel, 0.0)
        qts.append((qtf * scale).astype(k_ref.dtype))

    def scores(h, p, n):
        off = pl.multiple_of(p * tq, tq)
        return _dot(k_ref[h, pl.ds(off, n * tq), :], qts[h])

    def chosen(h, n):
        return sel_ref[h, pl.ds(n, 1), :] > 0.5

    def mask_past(h, s, first_blk):
        parts = [jnp.where(chosen(h, first_blk + n), s[n * blk_w:(n + 1) * blk_w], NEG_INF)
                 for n in range(s.shape[0] // blk_w)]
        return jnp.concatenate(parts, axis=0)

    key = lax.broadcasted_iota(jnp.int32, (blk_w, tq), 0)
    qry = lax.broadcasted_iota(jnp.int32, (blk_w, tq), 1)
    bot_ok = key <= qry - blk_w
    for h in range(n_h):
        s = scores(h, i, 1)
        chosen_lim = jnp.where(chosen(h, 2 * i), blk_w, -1)
        top_ok = key <= jnp.where(qry < blk_w, qry, chosen_lim)
        s = jnp.concatenate([jnp.where(top_ok, s[:blk_w], NEG_INF),
                             jnp.where(bot_ok, s[blk_w:], NEG_INF)], axis=0)
        _softmax_first(s, [vt_ref[h, i]], m_ref.at[h], acc_ref.at[h])

    @pl.when(i % 2 == 1)
    def _():
        for h in range(n_h):
            s = mask_past(h, scores(h, i - 1, 1), 2 * (i - 1))
            _softmax_step(s, [vt_ref[h, i - 1]], m_ref.at[h], acc_ref.at[h])

    def body(c, carry):
        for h in range(n_h):
            s = mask_past(h, scores(h, 2 * c, 2), 4 * c)
            _softmax_step(s, [vt_ref[h, 2 * c], vt_ref[h, 2 * c + 1]], m_ref.at[h], acc_ref.at[h])
        return carry

    lax.fori_loop(0, i // 2, body, 0)
    for h in range(n_h):
        o_ref[:, h * dv:(h + 1) * dv] = _softmax_finish(acc_ref.at[h], dv).astype(o_ref.dtype)


def moba_attention(qt, k, vt, k_mean, batch, seq, out_dtype):
    heads, _, hd, tq = qt.shape
    t = k.shape[1]
    dvx = vt.shape[2]
    dv = dvx - ONES_ROWS
    n_blk = seq // MOBA_BLOCK
    nq = seq // tq
    hb = ATTN_HEADS_PER_STEP
    return pl.pallas_call(
        functools.partial(_moba_attn_kernel, scale=hd ** -0.5, dv=dv),
        grid=(batch, heads // hb, nq),
        in_specs=[pl.BlockSpec((hb, 1, hd, tq), lambda b, h, i: (h, b * nq + i, 0, 0)),
                  pl.BlockSpec((hb, seq, hd), lambda b, h, i: (h, b, 0)),
                  pl.BlockSpec((hb, nq, dvx, tq), lambda b, h, i: (h, b, 0, 0)),
                  pl.BlockSpec((1, hb, n_blk, hd), lambda b, h, i: (b, h, 0, 0))],
        out_specs=pl.BlockSpec((tq, hb * dv), lambda b, h, i: (b * nq + i, h)),
        out_shape=jax.ShapeDtypeStruct((t, heads * dv), out_dtype),
        scratch_shapes=[pltpu.VMEM((hb, 1, tq), F32), pltpu.VMEM((hb, dvx, tq), F32),
                        pltpu.VMEM((hb, n_blk, tq), F32)],
        compiler_params=_params("parallel", "parallel", "arbitrary"),
        name="moba_attention",
    )(qt, k, vt, k_mean)


def _out_proj_kernel(a_ref, b_ref, c_ref, ga_ref, gb_ref, gc_ref, w_ref, x_ref, gate_ref, o_ref,
                     y_ref):
    @pl.when(pl.program_id(1) == 0)
    def _():
        wa, wb = a_ref.shape[1], b_ref.shape[1]
        y_ref[:, 0:wa] = _rms(a_ref[...].astype(F32), ga_ref[...]).astype(y_ref.dtype)
        y_ref[:, wa:wa + wb] = _rms(b_ref[...].astype(F32), gb_ref[...]).astype(y_ref.dtype)
        y_ref[:, wa + wb:] = _rms(c_ref[...].astype(F32), gc_ref[...]).astype(y_ref.dtype)

    o_ref[...] = x_ref[...] + gate_ref[0] * _dot(y_ref[...], w_ref[...])


def out_proj(y_a, y_b, y_c, g, w, x2, gate, seq, tm=512, tn=1024):
    t = y_a.shape[0]
    wa, wb, wc = y_a.shape[1], y_b.shape[1], y_c.shape[1]
    kd, n = w.shape
    tm, tn = _tile(seq, tm), _tile(n, tn)
    per_b = seq // tm
    ga, gb, gc = g[:wa].reshape(1, wa), g[wa:wa + wb].reshape(1, wb), g[wa + wb:].reshape(1, wc)
    rows = lambda width: pl.BlockSpec((tm, width), lambda i, j: (i, 0))
    vec = lambda width: pl.BlockSpec((1, width), lambda i, j: (0, 0))
    return pl.pallas_call(
        _out_proj_kernel,
        grid=(t // tm, n // tn),
        in_specs=[rows(wa), rows(wb), rows(wc), vec(wa), vec(wb), vec(wc),
                  pl.BlockSpec((kd, tn), lambda i, j: (0, j)),
                  pl.BlockSpec((tm, tn), lambda i, j: (i, j)),
                  pl.BlockSpec((1, 1, tn), lambda i, j: (i // per_b, 0, j))],
        out_specs=pl.BlockSpec((tm, tn), lambda i, j: (i, j)),
        out_shape=jax.ShapeDtypeStruct((t, n), F32),
        scratch_shapes=[pltpu.VMEM((tm, kd), BF16)],
        compiler_params=_params("parallel", "arbitrary"),
        name="out_proj",
    )(y_a, y_b, y_c, ga, gb, gc, w, x2, gate)


def _expert_weight_copies(w_hbm, layer, expert, col, stage, sem):
    width = stage.shape[1]
    return pltpu.make_async_copy(w_hbm.at[layer, expert, :, pl.ds(col, width)], stage, sem)


def _ffn_up_kernel(bexp_ref, first_ref, rows_ref, next_ref, x_ref, wg_hbm, wu_hbm, a_ref,
                   wg_st, wu_st, wg_bf, wu_bf, sem, *, layer):
    j = pl.program_id(0)
    b = pl.program_id(1)
    n_rows = rows_ref[b]
    half = x_ref.shape[0] // 2
    col = pl.multiple_of(j * wg_bf.shape[1], wg_bf.shape[1])

    def copies(expert):
        return (_expert_weight_copies(wg_hbm, layer, expert, col, wg_st, sem.at[0]),
                _expert_weight_copies(wu_hbm, layer, expert, col, wu_st, sem.at[1]))

    @pl.when(b == 0)
    def _():
        for c in copies(bexp_ref[0]):
            c.start()

    @pl.when((n_rows > 0) & (first_ref[b] == 1))
    def _():
        for c in copies(bexp_ref[b]):
            c.wait()
        wg_bf[...] = wg_st[...].astype(BF16)
        wu_bf[...] = wu_st[...].astype(BF16)

        @pl.when(next_ref[b] >= 0)
        def _():
            for c in copies(next_ref[b]):
                c.start()

    def act(x):
        gate = _dot(x, wg_bf[...])
        up = _dot(x, wu_bf[...])
        return (gate * jax.nn.sigmoid(gate) * up).astype(a_ref.dtype)

    @pl.when(n_rows > half)
    def _():
        a_ref[...] = act(x_ref[...])

    @pl.when((n_rows > 0) & (n_rows <= half))
    def _():
        a_ref[:half] = act(x_ref[:half])
        a_ref[half:] = jnp.zeros((half, a_ref.shape[1]), a_ref.dtype)

    @pl.when(n_rows == 0)
    def _():
        a_ref[...] = jnp.zeros_like(a_ref)


def _pack_bf16_pair(lo, hi):
    lo_bits = lax.bitcast_convert_type(lo.astype(BF16).astype(F32), PACKED) >> 16
    hi_bits = lax.bitcast_convert_type(hi.astype(BF16).astype(F32), PACKED) & jnp.uint32(0xFFFF0000)
    return hi_bits | lo_bits


def _unpack_bf16_pair(packed):
    lo = lax.bitcast_convert_type(packed << 16, F32)
    hi = lax.bitcast_convert_type(packed & jnp.uint32(0xFFFF0000), F32)
    return lo, hi


def _ffn_down_kernel(bexp_ref, first_ref, rows_ref, next_ref, a_ref, wd_hbm, o_ref,
                     wd_st, wd_bf, sem, *, layer):
    j = pl.program_id(0)
    b = pl.program_id(1)
    n_rows = rows_ref[b]
    half = a_ref.shape[0] // 2
    col = pl.multiple_of(j * wd_bf.shape[1], wd_bf.shape[1])

    def copy(expert):
        return _expert_weight_copies(wd_hbm, layer, expert, col, wd_st, sem.at[0])

    @pl.when(b == 0)
    def _():
        copy(bexp_ref[0]).start()

    @pl.when((n_rows > 0) & (first_ref[b] == 1))
    def _():
        copy(bexp_ref[b]).wait()
        wd_bf[...] = wd_st[...].astype(BF16)

        @pl.when(next_ref[b] >= 0)
        def _():
            copy(next_ref[b]).start()

    @pl.when(n_rows > half)
    def _():
        y = _dot(a_ref[...], wd_bf[...])
        o_ref[...] = _pack_bf16_pair(y[:half], y[half:])

    @pl.when((n_rows > 0) & (n_rows <= half))
    def _():
        y = _dot(a_ref[:half], wd_bf[...])
        o_ref[...] = _pack_bf16_pair(y, jnp.zeros_like(y))

    @pl.when(n_rows == 0)
    def _():
        o_ref[...] = jnp.zeros_like(o_ref)


def expert_ffn(xs, block_exp, block_first, block_rows, block_next, w_gate, w_up, w_down, layer):
    n_slots, d = xs.shape
    ff = w_gate.shape[3]
    fh, dh = ff // FF_SPLIT, d // FF_SPLIT
    n_blocks = n_slots // MOE_BLOCK
    hbm = pl.BlockSpec(memory_space=pl.ANY)
    up_spec = pltpu.PrefetchScalarGridSpec(
        num_scalar_prefetch=4,
        grid=(FF_SPLIT, n_blocks),
        in_specs=[pl.BlockSpec((MOE_BLOCK, d), lambda j, b, *_: (b, 0)), hbm, hbm],
        out_specs=pl.BlockSpec((MOE_BLOCK, fh), lambda j, b, *_: (b, j)),
        scratch_shapes=[pltpu.VMEM((d, fh), F32), pltpu.VMEM((d, fh), F32),
                        pltpu.VMEM((d, fh), BF16), pltpu.VMEM((d, fh), BF16),
                        pltpu.SemaphoreType.DMA((2,))],
    )
    act = pl.pallas_call(
        functools.partial(_ffn_up_kernel, layer=layer),
        grid_spec=up_spec,
        out_shape=jax.ShapeDtypeStruct((n_slots, ff), BF16),
        compiler_params=_params("arbitrary", "arbitrary"),
        name="expert_ffn_up",
    )(block_exp, block_first, block_rows, block_next, xs, w_gate, w_up)
    down_spec = pltpu.PrefetchScalarGridSpec(
        num_scalar_prefetch=4,
        grid=(FF_SPLIT, n_blocks),
        in_specs=[pl.BlockSpec((MOE_BLOCK, ff), lambda j, b, *_: (b, 0)), hbm],
        out_specs=pl.BlockSpec((MOE_BLOCK // 2, dh), lambda j, b, *_: (b, j)),
        scratch_shapes=[pltpu.VMEM((ff, dh), F32), pltpu.VMEM((ff, dh), BF16),
                        pltpu.SemaphoreType.DMA((1,))],
    )
    return pl.pallas_call(
        functools.partial(_ffn_down_kernel, layer=layer),
        grid_spec=down_spec,
        out_shape=jax.ShapeDtypeStruct((n_slots // 2, d), PACKED),
        compiler_params=_params("arbitrary", "arbitrary"),
        name="expert_ffn_down",
    )(block_exp, block_first, block_rows, block_next, act, w_down)


def _start_row_copies(idx_ref, base, n_rows, src_ref, dst_ref, sem):
    def issue(r, carry):
        row = idx_ref[base + r]
        pltpu.make_async_copy(src_ref.at[pl.ds(row, 1)], dst_ref.at[pl.ds(r, 1)], sem).start()
        return carry

    lax.fori_loop(0, n_rows, issue, 0, unroll=8)


def _wait_row_copies(n_rows, src_ref, dst_ref, sem):
    pltpu.make_async_copy(src_ref.at[pl.ds(0, n_rows)], dst_ref, sem).wait()


def _gather_cast_kernel(idx_ref, nused_ref, src_ref, o_ref, buf, sem):
    b = pl.program_id(0)
    rows = o_ref.shape[0]
    n_used = nused_ref[0]

    @pl.when((b == 0) & (n_used > 0))
    def _():
        _start_row_copies(idx_ref, 0, rows, src_ref, buf.at[0], sem.at[0])

    @pl.when(b + 1 < n_used)
    def _():
        nxt = (b + 1) % 2
        _start_row_copies(idx_ref, (b + 1) * rows, rows, src_ref, buf.at[nxt], sem.at[nxt])

    @pl.when(b < n_used)
    def _():
        cur = b % 2
        _wait_row_copies(rows, src_ref, buf.at[cur], sem.at[cur])
        o_ref[...] = buf[cur].astype(o_ref.dtype)

    @pl.when(b >= n_used)
    def _():
        o_ref[...] = jnp.zeros_like(o_ref)


def gather_rows_cast(src, idx, n_used_tiles, out_dtype):
    d = src.shape[1]
    m = idx.shape[0]
    rows = GATHER_ROWS
    grid_spec = pltpu.PrefetchScalarGridSpec(
        num_scalar_prefetch=2,
        grid=(m // rows,),
        in_specs=[pl.BlockSpec(memory_space=pl.ANY)],
        out_specs=pl.BlockSpec((rows, d), lambda b, ix, nu: (b, 0)),
        scratch_shapes=[pltpu.VMEM((2, rows, d), src.dtype), pltpu.SemaphoreType.DMA((2,))],
    )
    return pl.pallas_call(
        _gather_cast_kernel,
        grid_spec=grid_spec,
        out_shape=jax.ShapeDtypeStruct((m, d), out_dtype),
        compiler_params=_params("arbitrary"),
        name="gather_rows_cast",
    )(idx, n_used_tiles, src)


def _combine_kernel(pos_ref, x_ref, gate_ref, w_ref, half_ref, g_ref, sc_ref, sh_ref, ys_ref,
                    *rest, n_tok, final):
    if final:
        o_ref, ybuf, sem = rest
    else:
        o_ref, h_ref, ybuf, sem = rest
    i = pl.program_id(0)
    n_tiles = pl.num_programs(0)
    tm = x_ref.shape[0]

    def start_tile(tile, slot):
        for kk in range(TOP_K):
            _start_row_copies(pos_ref, kk * n_tok + tile * tm, tm, ys_ref, ybuf.at[slot, kk],
                              sem.at[slot])

    @pl.when(i == 0)
    def _():
        start_tile(0, 0)

    @pl.when(i + 1 < n_tiles)
    def _():
        start_tile(i + 1, (i + 1) % 2)

    cur = i % 2
    for kk in range(TOP_K):
        _wait_row_copies(tm, ys_ref, ybuf.at[cur, kk], sem.at[cur])
    w = w_ref[...]
    upper = half_ref[...] > 0.5
    acc = None
    for kk in range(TOP_K):
        lo, hi = _unpack_bf16_pair(ybuf[cur, kk])
        term = w[:, kk:kk + 1] * jnp.where(upper[:, kk:kk + 1], hi, lo)
        acc = term if acc is None else acc + term
    x_new = x_ref[...] + gate_ref[0] * acc
    if final:
        o_ref[...] = _rms(x_new, g_ref[...])
    else:
        o_ref[...] = x_new
        h_ref[...] = (_rms(x_new, g_ref[...]) * sc_ref[0] + sh_ref[0]).astype(h_ref.dtype)


def moe_combine(x2, gate, ys, pos, w_tok, half_tok, seq, norm_g, sc1p=None, sh=None):
    t, d = x2.shape
    tm = _tile(seq, GATHER_ROWS)
    per_b = seq // tm
    final = sc1p is None
    if final:
        sc1p = sh = jnp.zeros((x2.shape[0] // seq, 1, d), F32)
    row = pl.BlockSpec((tm, d), lambda i, ps: (i, 0))
    mod = pl.BlockSpec((1, 1, d), lambda i, ps: (i // per_b, 0, 0))
    tok = pl.BlockSpec((tm, TOP_K), lambda i, ps: (i, 0))
    grid_spec = pltpu.PrefetchScalarGridSpec(
        num_scalar_prefetch=1,
        grid=(t // tm,),
        in_specs=[row, mod, tok, tok, pl.BlockSpec((1, d), lambda i, ps: (0, 0)), mod, mod,
                  pl.BlockSpec(memory_space=pl.ANY)],
        out_specs=row if final else [row, row],
        scratch_shapes=[pltpu.VMEM((2, TOP_K, tm, d), ys.dtype), pltpu.SemaphoreType.DMA((2,))],
    )
    out_shape = jax.ShapeDtypeStruct((t, d), F32)
    return pl.pallas_call(
        functools.partial(_combine_kernel, n_tok=t, final=final),
        grid_spec=grid_spec,
        out_shape=out_shape if final else [out_shape, jax.ShapeDtypeStruct((t, d), BF16)],
        compiler_params=_params("arbitrary"),
        name="moe_combine",
    )(pos, x2, gate, w_tok, half_tok, norm_g.reshape(1, d), sc1p, sh, ys)


def _group_by_expert(e_idx):
    n_tok = e_idx.shape[1]
    n_asg = n_tok * TOP_K
    flat_e = e_idx.reshape(n_asg)
    onehot = (flat_e[:, None] == jnp.arange(N_EXPERTS)[None, :]).astype(jnp.int32)
    running = jnp.cumsum(onehot, axis=0)
    rank = jnp.sum(running * onehot, axis=1) - 1
    sizes = running[-1]
    padded = (sizes + MOE_BLOCK - 1) // MOE_BLOCK * MOE_BLOCK
    pad_end = jnp.cumsum(padded)
    pad_start = pad_end - padded
    dest = (jnp.sum(pad_start[None, :] * onehot, axis=1) + rank).astype(jnp.int32)
    n_blocks = -(-n_asg // MOE_BLOCK) + N_EXPERTS
    n_slots = n_blocks * MOE_BLOCK
    slot_tok = (jnp.arange(n_slots, dtype=jnp.int32) % n_tok).at[dest].set(
        jnp.arange(n_asg, dtype=jnp.int32) % n_tok, mode="promise_in_bounds")
    block_start = jnp.arange(n_blocks, dtype=jnp.int32) * MOE_BLOCK
    block_exp = jnp.minimum(jnp.sum(block_start[:, None] >= pad_end[None, :], axis=1),
                            N_EXPERTS - 1).astype(jnp.int32)
    n_used = (pad_end[-1] // MOE_BLOCK).astype(jnp.int32)
    block_exp = jnp.where(jnp.arange(n_blocks) < n_used, block_exp, block_exp[n_used - 1])
    prev_exp = jnp.concatenate([jnp.full((1,), -1, jnp.int32), block_exp[:-1]])
    block_first = (block_exp != prev_exp).astype(jnp.int32)
    block_rows = jnp.clip(pad_start[block_exp] + sizes[block_exp] - block_start, 0, MOE_BLOCK)
    block_rows = jnp.where(jnp.arange(n_blocks) < n_used, block_rows, 0).astype(jnp.int32)
    blk = jnp.arange(n_blocks, dtype=jnp.int32)
    starts = jnp.where((block_first == 1) & (block_rows > 0), blk, n_blocks)
    later = lax.cummin(jnp.concatenate([starts[1:], jnp.full((1,), n_blocks, jnp.int32)]),
                       axis=0, reverse=True)
    block_next = jnp.where(later < n_blocks, block_exp[jnp.minimum(later, n_blocks - 1)],
                           -1).astype(jnp.int32)
    return (dest.reshape(TOP_K, n_tok), slot_tok, block_exp, block_first, block_rows, block_next,
            n_used.reshape(1))


def _rot_cols(w):
    half = w.shape[-1] // 2
    return jnp.concatenate([-w[..., half:], w[..., :half]], axis=-1)


def _prep_layer(w_in, w_uq, w_ukv, w_out):
    a_end = Q_LORA + KV_LORA + MLA_ROPE
    b_end = a_end + 3 * CONV_CH
    w_kpe = w_in[:, Q_LORA + KV_LORA:a_end]
    w_a = jnp.concatenate([w_in[:, :a_end], _rot_cols(w_kpe)], axis=1).astype(BF16)
    w_b = w_in[:, a_end:b_end].astype(BF16)
    w_c = w_in[:, b_end:].astype(BF16)
    wq = w_uq.reshape(Q_LORA, MLA_HEADS, MLA_NOPE + MLA_ROPE)
    wq_rope = wq[..., MLA_NOPE:]
    wq = jnp.concatenate([wq, _rot_cols(wq_rope)], axis=-1).transpose(1, 0, 2).astype(BF16)
    return dict(w_a=w_a, w_b=w_b, w_c=w_c, w_q=wq, w_kv=w_ukv.astype(BF16),
                w_out=w_out.astype(BF16))


def _rope_tables(seq, dim):
    inv = 1.0 / (ROPE_THETA ** (jnp.arange(0, dim, 2, dtype=F32) / dim))
    ang = jnp.arange(seq, dtype=F32)[:, None] * inv[None, :]
    return jnp.cos(ang), jnp.sin(ang)


def kernel(x, c, w_mod, mod_table, mix_norm_g, w_in, q_norm_g, kv_norm_g, w_uq, w_ukv, conv_w,
           group_norm_g, w_out, ffn_norm_g, w_router, router_bias, w_gate, w_up, w_down,
           final_norm_g):
    batch, seq, d = x.shape
    depth = w_in.shape[0]
    t = batch * seq
    x2 = x.reshape(t, d)

    cos_a, sin_a = _rope_tables(seq, MLA_ROPE)
    zeros_a = jnp.zeros_like(cos_a)
    cos_p = jnp.concatenate([cos_a, cos_a, zeros_a, zeros_a], axis=1)
    sin_p = jnp.concatenate([sin_a, sin_a, zeros_a, zeros_a], axis=1)
    cos_b, sin_b = _rope_tables(seq, MOBA_HD)
    cos_f = jnp.concatenate([cos_b, cos_b], axis=1)
    sin_s = jnp.concatenate([-sin_b, sin_b], axis=1)

    c_pad = jnp.zeros((SUBLANE, d), F32).at[:batch].set(c)
    mod_shared = mod_matmul(c_pad, w_mod)[:batch].reshape(batch, N_MOD, d)

    wr_t = w_router.T
    wr_hi = wr_t.astype(BF16)
    wr_lo = (wr_t - wr_hi.astype(F32)).astype(BF16)
    router = (wr_hi, wr_lo, router_bias.astype(F32).reshape(N_EXPERTS, 1))
    mla_scale = (MLA_NOPE + MLA_ROPE) ** -0.5

    mods = []
    for l in range(depth):
        mod = mod_shared + mod_table[l][None]
        mods.append([mod[:, i][:, None, :] for i in range(N_MOD)])

    h = norm_mod(x2, mix_norm_g[0], 1.0 + mods[0][1], mods[0][0], seq)
    for l in range(depth):
        p = _prep_layer(w_in[l], w_uq[l], w_ukv[l], w_out[l])
        sh1, sc1, g1, sh2, sc2, g2 = mods[l]

        proj_a = matmul(h, p["w_a"], F32, tm=512)
        bch = matmul(h, p["w_b"], F32)
        qkv = matmul(h, p["w_c"], F32)

        q_a = mla_q_up(proj_a, q_norm_g[l], p["w_q"], cos_p, sin_p, seq, mla_scale)
        k_a, v_a = mla_kv_up(proj_a, kv_norm_g[l], p["w_kv"], cos_p, sin_p, seq)
        y_a = mla_attention(q_a, k_a, v_a, batch, seq, BF16)

        y_b = conv_mixer(bch, conv_w[l], seq, F32)

        q_c, k_c, v_c, k_mean = moba_prep(qkv, cos_f, sin_s, seq)
        n_blk = seq // MOBA_BLOCK
        k_mean = k_mean.reshape(batch, n_blk, MOBA_HEADS, MOBA_HD).transpose(0, 2, 1, 3)
        y_c = moba_attention(q_c, k_c, v_c, k_mean, batch, seq, BF16)

        x2 = out_proj(y_a, y_b, y_c, group_norm_g[l], p["w_out"], x2, g1, seq)

        h2, e_idx, gates = norm_mod(x2, ffn_norm_g[l], 1.0 + sc2, sh2, seq, out_dtype=F32,
                                    router=router)
        pos, slot_tok, block_exp, block_first, block_rows, block_next, n_used = _group_by_expert(e_idx)
        xs = gather_rows_cast(h2, slot_tok, n_used * (MOE_BLOCK // GATHER_ROWS), BF16)
        ys = expert_ffn(xs, block_exp, block_first, block_rows, block_next, w_gate, w_up, w_down,
                        l)
        sub = pos % MOE_BLOCK
        ys_row = ((pos // MOE_BLOCK) * (MOE_BLOCK // 2) + sub % (MOE_BLOCK // 2)).reshape(-1)
        ys_half = (sub // (MOE_BLOCK // 2)).astype(F32).T
        if l + 1 < depth:
            nxt = mods[l + 1]
            x2, h = moe_combine(x2, g2, ys, ys_row, gates.T, ys_half, seq, mix_norm_g[l + 1],
                                1.0 + nxt[1], nxt[0])
        else:
            out = moe_combine(x2, g2, ys, ys_row, gates.T, ys_half, seq, final_norm_g)

    return out.reshape(batch, seq, d)
```

```python
import functools

import jax
import jax.numpy as jnp
from jax import lax
from jax.experimental import pallas as pl
from jax.experimental.pallas import tpu as pltpu

MLA_HEADS = 16
MLA_NOPE = 128
MLA_ROPE = 64
MLA_V = 128
Q_LORA = 1024
KV_LORA = 512
MLA_DK_PAD = 256
CONV_CH = 1024
CONV_W = 3
MOBA_HEADS = 8
MOBA_HD = 128
MOBA_BLOCK = 256
MOBA_TOPK = 3
ROPE_THETA = 10000.0
EPS = 1e-6
N_MOD = 6
N_EXPERTS = 16
N_GROUPS = 4
TOP_K = 2
MOE_BLOCK = 512
GATHER_ROWS = 256
FF_SPLIT = 2
UP_HEADS_PER_STEP = 4
ATTN_HEADS_PER_STEP = 4
ATTN_TILE = 512
ONES_ROWS = 16

V7X_VMEM_LIMIT_BYTES = 56 * 1024 * 1024
LANE = 128
SUBLANE = 8

F32 = jnp.float32
BF16 = jnp.bfloat16
PACKED = jnp.uint32
NEG_INF = float("-inf")


def _params(*sem):
    return pltpu.CompilerParams(dimension_semantics=sem, vmem_limit_bytes=V7X_VMEM_LIMIT_BYTES)


def _dot(a, b, precision=None):
    return jnp.dot(a, b, preferred_element_type=F32, precision=precision)


def _dot_nt(a, b):
    return lax.dot_general(a, b, (((1,), (1,)), ((), ())), preferred_element_type=F32)


def _tile(dim, want):
    return want if dim % want == 0 else dim


def _mod_kernel(c_ref, w0_ref, w1_ref, o_ref):
    @pl.when(pl.program_id(0) == 0)
    def _():
        o_ref[...] = jnp.zeros_like(o_ref)

    c = c_ref[...]
    a = (c * jax.nn.sigmoid(c)).astype(BF16)
    half = w0_ref.shape[1]
    o_ref[:, :half] += _dot(a, w0_ref[...].astype(BF16))
    o_ref[:, half:] += _dot(a, w1_ref[...].astype(BF16))


def mod_matmul(c_pad, w_mod):
    rows, d = c_pad.shape
    n = w_mod.shape[1]
    tk = _tile(d, LANE)
    return pl.pallas_call(
        _mod_kernel,
        grid=(d // tk,),
        in_specs=[pl.BlockSpec((rows, tk), lambda k: (0, k)),
                  pl.BlockSpec((tk, n // 2), lambda k: (k, 0)),
                  pl.BlockSpec((tk, n // 2), lambda k: (k, 1))],
        out_specs=pl.BlockSpec((rows, n), lambda k: (0, 0)),
        out_shape=jax.ShapeDtypeStruct((rows, n), F32),
        compiler_params=_params("arbitrary"),
        name="mod_matmul",
    )(c_pad, w_mod, w_mod)


def _rms(x, g):
    return x * lax.rsqrt(jnp.mean(x * x, axis=-1, keepdims=True) + EPS) * g


def _norm_mod_kernel(x_ref, g_ref, sc_ref, sh_ref, o_ref):
    y = _rms(x_ref[...], g_ref[...])
    o_ref[...] = (y * sc_ref[0] + sh_ref[0]).astype(o_ref.dtype)


def _top2_sum(a, b, c, d):
    hi1, lo1 = jnp.maximum(a, b), jnp.minimum(a, b)
    hi2, lo2 = jnp.maximum(c, d), jnp.minimum(c, d)
    return jnp.maximum(hi1, hi2) + jnp.maximum(jnp.minimum(hi1, hi2), jnp.maximum(lo1, lo2))


def _norm_router_kernel(x_ref, g_ref, sc_ref, sh_ref, whi_ref, wlo_ref, bias_ref,
                        o_ref, e_ref, gate_ref):
    h = _rms(x_ref[...], g_ref[...]) * sc_ref[0] + sh_ref[0]
    hi = h.astype(BF16)
    o_ref[...] = h.astype(o_ref.dtype)
    lo = (h - hi.astype(F32)).astype(BF16)
    whi, wlo = whi_ref[...], wlo_ref[...]
    logits = _dot_nt(whi, hi) + _dot_nt(wlo, hi) + _dot_nt(whi, lo)
    scores = jax.nn.sigmoid(logits)
    biased = scores + bias_ref[...]
    per_grp = N_EXPERTS // N_GROUPS
    b_rows = [biased[e:e + 1, :] for e in range(N_EXPERTS)]
    s_rows = [scores[e:e + 1, :] for e in range(N_EXPERTS)]
    grp_scores = [_top2_sum(*b_rows[g * per_grp:(g + 1) * per_grp]) for g in range(N_GROUPS)]
    best, grp = grp_scores[0], jnp.zeros(grp_scores[0].shape, jnp.int32)
    for g in range(1, N_GROUPS):
        better = grp_scores[g] > best
        grp = jnp.where(better, g, grp)
        best = jnp.where(better, grp_scores[g], best)
    b4, s4 = b_rows[:per_grp], s_rows[:per_grp]
    for g in range(1, N_GROUPS):
        in_g = grp == g
        b4 = [jnp.where(in_g, b_rows[g * per_grp + j], b4[j]) for j in range(per_grp)]
        s4 = [jnp.where(in_g, s_rows[g * per_grp + j], s4[j]) for j in range(per_grp)]
    v1, i1 = b4[0], jnp.zeros(grp.shape, jnp.int32)
    for j in range(1, per_grp):
        better = b4[j] > v1
        i1 = jnp.where(better, j, i1)
        v1 = jnp.where(better, b4[j], v1)
    v2, i2 = jnp.full(v1.shape, NEG_INF, F32), jnp.zeros(grp.shape, jnp.int32)
    for j in range(per_grp):
        better = (i1 != j) & (b4[j] > v2)
        i2 = jnp.where(better, j, i2)
        v2 = jnp.where(better, b4[j], v2)
    g1, g2 = s4[0], s4[0]
    for j in range(1, per_grp):
        g1 = jnp.where(i1 == j, s4[j], g1)
        g2 = jnp.where(i2 == j, s4[j], g2)
    total = g1 + g2
    e_ref[...] = jnp.concatenate([grp * per_grp + i1, grp * per_grp + i2], axis=0)
    gate_ref[...] = jnp.concatenate([g1 / total, g2 / total], axis=0)


def norm_mod(x2, g, sc1p, sh, seq, out_dtype=BF16, router=None):
    t, d = x2.shape
    tm = _tile(seq, 256)
    per_b = seq // tm
    row = pl.BlockSpec((tm, d), lambda i: (i, 0))
    vec = pl.BlockSpec((1, d), lambda i: (0, 0))
    mod = pl.BlockSpec((1, 1, d), lambda i: (i // per_b, 0, 0))
    g2 = g.reshape(1, d)
    if router is None:
        return pl.pallas_call(
            _norm_mod_kernel, grid=(t // tm,), in_specs=[row, vec, mod, mod], out_specs=row,
            out_shape=jax.ShapeDtypeStruct((t, d), out_dtype),
            compiler_params=_params("arbitrary"), name="norm_mod")(x2, g2, sc1p, sh)
    whi_t, wlo_t, bias = router
    ne = whi_t.shape[0]
    wspec = pl.BlockSpec((ne, d), lambda i: (0, 0))
    kspec = pl.BlockSpec((TOP_K, tm), lambda i: (0, i))
    return pl.pallas_call(
        _norm_router_kernel, grid=(t // tm,),
        in_specs=[row, vec, mod, mod, wspec, wspec, pl.BlockSpec((ne, 1), lambda i: (0, 0))],
        out_specs=[row, kspec, kspec],
        out_shape=[jax.ShapeDtypeStruct((t, d), out_dtype),
                   jax.ShapeDtypeStruct((TOP_K, t), jnp.int32),
                   jax.ShapeDtypeStruct((TOP_K, t), F32)],
        compiler_params=_params("arbitrary"), name="norm_router")(x2, g2, sc1p, sh, whi_t, wlo_t, bias)


def _mm_kernel(a_ref, w_ref, o_ref):
    o_ref[...] = _dot(a_ref[...], w_ref[0]).astype(o_ref.dtype)


def matmul(a, w_layers, layer, out_dtype, tm=1024, tn=1024):
    m, kd = a.shape
    n = w_layers.shape[2]
    tm, tn = _tile(m, tm), _tile(n, tn)
    return pl.pallas_call(
        _mm_kernel,
        grid=(m // tm, n // tn),
        in_specs=[pl.BlockSpec((tm, kd), lambda i, j: (i, 0)),
                  pl.BlockSpec((1, kd, tn), lambda i, j: (layer, 0, j))],
        out_specs=pl.BlockSpec((tm, tn), lambda i, j: (i, j)),
        out_shape=jax.ShapeDtypeStruct((m, n), out_dtype),
        compiler_params=_params("parallel", "parallel"),
        name="matmul",
    )(a, w_layers)


def _rope_pair(r2, cos_ref, sin_ref):
    return r2 * cos_ref[...] + pltpu.roll(r2, MLA_ROPE, 1) * sin_ref[...]


def _with_ones_rows(vt):
    return jnp.concatenate([vt, jnp.ones((ONES_ROWS, vt.shape[1]), vt.dtype)], axis=0)


def _qup_kernel(cq_ref, g_ref, w_ref, cos_ref, sin_ref, o_ref, an_ref, *, scale):
    @pl.when(pl.program_id(1) == 0)
    def _():
        an_ref[...] = _rms(cq_ref[...], g_ref[...]).astype(BF16)

    an = an_ref[...]
    for hh in range(w_ref.shape[1]):
        res = _dot(an, w_ref[0, hh])
        roped = _rope_pair(res[:, MLA_NOPE:], cos_ref, sin_ref)
        q = jnp.concatenate([res[:, :MLA_NOPE], roped], axis=1) * scale
        o_ref[hh, 0] = q.T.astype(o_ref.dtype)


def mla_q_up(proj_a, g, w_q, layer, cos_p, sin_p, seq, scale):
    t = proj_a.shape[0]
    heads = w_q.shape[1]
    hb = UP_HEADS_PER_STEP
    tm = _tile(seq, ATTN_TILE)
    per_b = seq // tm
    return pl.pallas_call(
        functools.partial(_qup_kernel, scale=scale),
        grid=(t // tm, heads // hb),
        in_specs=[pl.BlockSpec((tm, Q_LORA), lambda i, h: (i, 0)),
                  pl.BlockSpec((1, Q_LORA), lambda i, h: (0, 0)),
                  pl.BlockSpec((1, hb, Q_LORA, MLA_DK_PAD), lambda i, h: (layer, h, 0, 0)),
                  pl.BlockSpec((tm, LANE), lambda i, h: (i % per_b, 0)),
                  pl.BlockSpec((tm, LANE), lambda i, h: (i % per_b, 0))],
        out_specs=pl.BlockSpec((hb, 1, MLA_DK_PAD, tm), lambda i, h: (h, i, 0, 0)),
        out_shape=jax.ShapeDtypeStruct((heads, t // tm, MLA_DK_PAD, tm), BF16),
        scratch_shapes=[pltpu.VMEM((tm, Q_LORA), BF16)],
        compiler_params=_params("parallel", "arbitrary"),
        name="mla_q_up",
    )(proj_a, g.reshape(1, Q_LORA), w_q, cos_p, sin_p)


def _kvup_kernel(ckv_ref, g_ref, w_ref, kpe_ref, cos_ref, sin_ref, k_ref, vt_ref, an_ref):
    @pl.when(pl.program_id(1) == 0)
    def _():
        an_ref[...] = _rms(ckv_ref[...], g_ref[...]).astype(BF16)

    an = an_ref[...]
    k_rot = _rope_pair(kpe_ref[...], cos_ref, sin_ref)
    hw = MLA_NOPE + MLA_V
    for hh in range(k_ref.shape[0]):
        res = _dot(an, w_ref[0, :, hh * hw:(hh + 1) * hw])
        k_ref[hh] = jnp.concatenate([res[:, :MLA_NOPE], k_rot], axis=1).astype(k_ref.dtype)
        vt_ref[hh, 0] = _with_ones_rows(res[:, MLA_NOPE:].T).astype(vt_ref.dtype)


def mla_kv_up(proj_a, g, w_kv, layer, cos_p, sin_p, seq):
    t = proj_a.shape[0]
    hw = MLA_NOPE + MLA_V
    heads = w_kv.shape[2] // hw
    hb = UP_HEADS_PER_STEP
    tm = _tile(seq, ATTN_TILE)
    per_b = seq // tm
    return pl.pallas_call(
        _kvup_kernel,
        grid=(t // tm, heads // hb),
        in_specs=[pl.BlockSpec((tm, KV_LORA), lambda i, h: (i, Q_LORA // KV_LORA)),
                  pl.BlockSpec((1, KV_LORA), lambda i, h: (0, 0)),
                  pl.BlockSpec((1, KV_LORA, hb * hw), lambda i, h: (layer, 0, h)),
                  pl.BlockSpec((tm, LANE), lambda i, h: (i, (Q_LORA + KV_LORA) // LANE)),
                  pl.BlockSpec((tm, LANE), lambda i, h: (i % per_b, 0)),
                  pl.BlockSpec((tm, LANE), lambda i, h: (i % per_b, 0))],
        out_specs=[pl.BlockSpec((hb, tm, MLA_DK_PAD), lambda i, h: (h, i, 0)),
                   pl.BlockSpec((hb, 1, MLA_V + ONES_ROWS, tm), lambda i, h: (h, i, 0, 0))],
        out_shape=[jax.ShapeDtypeStruct((heads, t, MLA_DK_PAD), BF16),
                   jax.ShapeDtypeStruct((heads, t // tm, MLA_V + ONES_ROWS, tm), BF16)],
        scratch_shapes=[pltpu.VMEM((tm, KV_LORA), BF16)],
        compiler_params=_params("parallel", "arbitrary"),
        name="mla_kv_up",
    )(proj_a, g.reshape(1, KV_LORA), w_kv, proj_a, cos_p, sin_p)


def _pv(vts, p):
    rows = p.shape[0] // len(vts)
    out = _dot(vts[0], p[:rows].astype(vts[0].dtype))
    for n in range(1, len(vts)):
        out = out + _dot(vts[n], p[n * rows:(n + 1) * rows].astype(vts[n].dtype))
    return out


def _softmax_first(s, vts, m_ref, acc_ref):
    m = jnp.max(s, axis=0, keepdims=True)
    m_ref[...] = m
    acc_ref[...] = _pv(vts, jnp.exp(s - m))


def _softmax_step(s, vts, m_ref, acc_ref):
    m_prev = m_ref[...]
    m_new = jnp.maximum(m_prev, jnp.max(s, axis=0, keepdims=True))
    alpha = jnp.exp(m_prev - m_new)
    acc_ref[...] = alpha * acc_ref[...] + _pv(vts, jnp.exp(s - m_new))
    m_ref[...] = m_new


def _softmax_finish(acc_ref, dv):
    acc = acc_ref[...]
    return (acc[:dv, :] / acc[dv:dv + 1, :]).T


def _causal_mask_t(s):
    key = lax.broadcasted_iota(jnp.int32, s.shape, 0)
    qry = lax.broadcasted_iota(jnp.int32, s.shape, 1)
    return jnp.where(key <= qry, s, NEG_INF)


def _mla_attn_kernel(qt_ref, k_ref, vt_ref, o_ref, m_ref, acc_ref, *, tq, dv):
    i = pl.program_id(2)
    n_h = qt_ref.shape[0]
    qts = [qt_ref[h, 0] for h in range(n_h)]

    def scores(h, j, n):
        off = pl.multiple_of(j * tq, tq)
        return _dot(k_ref[h, pl.ds(off, n * tq), :], qts[h])

    for h in range(n_h):
        _softmax_first(_causal_mask_t(scores(h, i, 1)), [vt_ref[h, i]], m_ref.at[h], acc_ref.at[h])

    @pl.when(i % 2 == 1)
    def _():
        for h in range(n_h):
            _softmax_step(scores(h, i - 1, 1), [vt_ref[h, i - 1]], m_ref.at[h], acc_ref.at[h])

    def body(c, carry):
        for h in range(n_h):
            _softmax_step(scores(h, 2 * c, 2), [vt_ref[h, 2 * c], vt_ref[h, 2 * c + 1]],
                          m_ref.at[h], acc_ref.at[h])
        return carry

    lax.fori_loop(0, i // 2, body, 0)
    for h in range(n_h):
        o_ref[:, h * dv:(h + 1) * dv] = _softmax_finish(acc_ref.at[h], dv).astype(o_ref.dtype)


def mla_attention(qt, k, vt, batch, seq, out_dtype):
    heads, _, dk, tq = qt.shape
    t = k.shape[1]
    dvx = vt.shape[2]
    dv = dvx - ONES_ROWS
    nq = seq // tq
    hb = ATTN_HEADS_PER_STEP
    return pl.pallas_call(
        functools.partial(_mla_attn_kernel, tq=tq, dv=dv),
        grid=(batch, heads // hb, nq),
        in_specs=[pl.BlockSpec((hb, 1, dk, tq), lambda b, h, i: (h, b * nq + i, 0, 0)),
                  pl.BlockSpec((hb, seq, dk), lambda b, h, i: (h, b, 0)),
                  pl.BlockSpec((hb, nq, dvx, tq), lambda b, h, i: (h, b, 0, 0))],
        out_specs=pl.BlockSpec((tq, hb * dv), lambda b, h, i: (b * nq + i, h)),
        out_shape=jax.ShapeDtypeStruct((t, heads * dv), out_dtype),
        scratch_shapes=[pltpu.VMEM((hb, 1, tq), F32), pltpu.VMEM((hb, dvx, tq), F32)],
        compiler_params=_params("parallel", "parallel", "arbitrary"),
        name="mla_attention",
    )(qt, k, vt)


def _conv_kernel(b_ref, c_ref, h_ref, w_ref, o_ref, carry_ref, *, per_b):
    i = pl.program_id(0)

    @pl.when(i % per_b == 0)
    def _():
        carry_ref[...] = jnp.zeros_like(carry_ref)

    tm = o_ref.shape[0]
    w0, w1, w2 = w_ref[0:1, :], w_ref[1:2, :], w_ref[2:3, :]
    u = c_ref[...] * h_ref[...]
    y = w0 * pltpu.roll(u, 2, 0) + w1 * pltpu.roll(u, 1, 0) + w2 * u
    o_ref[...] = (b_ref[...] * y).astype(o_ref.dtype)
    u8 = u[0:SUBLANE, :]
    tail = carry_ref[...]
    r8 = lax.broadcasted_iota(jnp.int32, u8.shape, 0)
    p1 = jnp.where(r8 < 1, pltpu.roll(tail, 1, 0), pltpu.roll(u8, 1, 0))
    p2 = jnp.where(r8 < 2, pltpu.roll(tail, 2, 0), pltpu.roll(u8, 2, 0))
    y8 = w0 * p2 + w1 * p1 + w2 * u8
    o_ref[0:SUBLANE, :] = (b_ref[0:SUBLANE, :] * y8).astype(o_ref.dtype)
    carry_ref[...] = u[tm - SUBLANE:tm, :]


def conv_mixer(bch, conv_w, seq, out_dtype):
    t = bch.shape[0]
    ch = conv_w.shape[1]
    tm = _tile(seq, 256)
    per_b = seq // tm
    return pl.pallas_call(
        functools.partial(_conv_kernel, per_b=per_b),
        grid=(t // tm,),
        in_specs=[pl.BlockSpec((tm, ch), lambda i: (i, 0)),
                  pl.BlockSpec((tm, ch), lambda i: (i, 1)),
                  pl.BlockSpec((tm, ch), lambda i: (i, 2)),
                  pl.BlockSpec((CONV_W, ch), lambda i: (0, 0))],
        out_specs=pl.BlockSpec((tm, ch), lambda i: (i, 0)),
        out_shape=jax.ShapeDtypeStruct((t, ch), out_dtype),
        scratch_shapes=[pltpu.VMEM((SUBLANE, ch), F32)],
        compiler_params=_params("arbitrary"),
        name="conv_mixer",
    )(bch, bch, bch, conv_w)


def _moba_prep_kernel(q_ref, k_ref, v_ref, cos_ref, sin_ref, qo_ref, ko_ref, vo_ref, km_ref):
    heads = ko_ref.shape[0]
    cos, sin = cos_ref[...], sin_ref[...]
    n_sub = km_ref.shape[0]
    means = [[] for _ in range(n_sub)]
    for h in range(heads):
        sl = slice(h * MOBA_HD, (h + 1) * MOBA_HD)
        qh = q_ref[:, sl]
        kh = k_ref[:, sl]
        qr = qh * cos + pltpu.roll(qh, MOBA_HD // 2, 1) * sin
        qo_ref[h, 0] = qr.T.astype(qo_ref.dtype)
        kr = kh * cos + pltpu.roll(kh, MOBA_HD // 2, 1) * sin
        ko_ref[h] = kr.astype(ko_ref.dtype)
        vo_ref[h, 0] = _with_ones_rows(v_ref[:, sl].T).astype(vo_ref.dtype)
        for s in range(n_sub):
            means[s].append(jnp.mean(kr[s * MOBA_BLOCK:(s + 1) * MOBA_BLOCK], axis=0, keepdims=True))
    for s in range(n_sub):
        km_ref[s] = jnp.concatenate(means[s], axis=0)


def moba_prep(qkv, cos_f, sin_s, seq):
    t = qkv.shape[0]
    width = qkv.shape[1] // 3
    heads = width // MOBA_HD
    tm = 2 * MOBA_BLOCK
    assert seq % tm == 0
    per_b = seq // tm
    nt = t // tm
    return pl.pallas_call(
        _moba_prep_kernel,
        grid=(nt,),
        in_specs=[pl.BlockSpec((tm, width), lambda i: (i, 0)),
                  pl.BlockSpec((tm, width), lambda i: (i, 1)),
                  pl.BlockSpec((tm, width), lambda i: (i, 2)),
                  pl.BlockSpec((tm, MOBA_HD), lambda i: (i % per_b, 0)),
                  pl.BlockSpec((tm, MOBA_HD), lambda i: (i % per_b, 0))],
        out_specs=[pl.BlockSpec((heads, 1, MOBA_HD, tm), lambda i: (0, i, 0, 0)),
                   pl.BlockSpec((heads, tm, MOBA_HD), lambda i: (0, i, 0)),
                   pl.BlockSpec((heads, 1, MOBA_HD + ONES_ROWS, tm), lambda i: (0, i, 0, 0)),
                   pl.BlockSpec((2, heads, MOBA_HD), lambda i: (i, 0, 0))],
        out_shape=[jax.ShapeDtypeStruct((heads, nt, MOBA_HD, tm), BF16),
                   jax.ShapeDtypeStruct((heads, t, MOBA_HD), BF16),
                   jax.ShapeDtypeStruct((heads, nt, MOBA_HD + ONES_ROWS, tm), BF16),
                   jax.ShapeDtypeStruct((2 * nt, heads, MOBA_HD), F32)],
        compiler_params=_params("arbitrary"),
        name="moba_prep",
    )(qkv, qkv, qkv, cos_f, sin_s)


def _moba_attn_kernel(qt_ref, k_ref, vt_ref, km_ref, o_ref, m_ref, acc_ref, sel_ref, *, scale, dv):
    i = pl.program_id(2)
    n_h = qt_ref.shape[0]
    n_blk = km_ref.shape[2]
    tq = qt_ref.shape[3]
    blk_w = MOBA_BLOCK
    blk = lax.broadcasted_iota(jnp.int32, (n_blk, tq), 0)
    own = 2 * i + (lax.broadcasted_iota(jnp.int32, (n_blk, tq), 1) >= blk_w).astype(jnp.int32)
    past = blk < own
    qts = []
    for h in range(n_h):
        qtf = qt_ref[h, 0].astype(F32)
        g = _dot(km_ref[0, h], qtf, precision=lax.Precision.HIGHEST)
        g = jnp.where(past, g, NEG_INF)
        sel = jnp.zeros(g.shape, F32)
        for _ in range(min(MOBA_TOPK, n_blk)):
            mx = jnp.max(g, axis=0, keepdims=True)
            first = jnp.min(jnp.where(g == mx, blk, n_blk), axis=0, keepdims=True)
            hit = blk == first
            sel = jnp.where(hit, 1.0, sel)
            g = jnp.where(hit, NEG_INF, g)
        sel_ref[h] = jnp.where(past, sel, 0.0)
        qts.append((qtf * scale).astype(k_ref.dtype))

    def scores(h, p, n):
        off = pl.multiple_of(p * tq, tq)
        return _dot(k_ref[h, pl.ds(off, n * tq), :], qts[h])

    def chosen(h, n):
        return sel_ref[h, pl.ds(n, 1), :] > 0.5

    def mask_past(h, s, first_blk):
        parts = [jnp.where(chosen(h, first_blk + n), s[n * blk_w:(n + 1) * blk_w], NEG_INF)
                 for n in range(s.shape[0] // blk_w)]
        return jnp.concatenate(parts, axis=0)

    key = lax.broadcasted_iota(jnp.int32, (blk_w, tq), 0)
    qry = lax.broadcasted_iota(jnp.int32, (blk_w, tq), 1)
    bot_ok = key <= qry - blk_w
    for h in range(n_h):
        s = scores(h, i, 1)
        chosen_lim = jnp.where(chosen(h, 2 * i), blk_w, -1)
        top_ok = key <= jnp.where(qry < blk_w, qry, chosen_lim)
        s = jnp.concatenate([jnp.where(top_ok, s[:blk_w], NEG_INF),
                             jnp.where(bot_ok, s[blk_w:], NEG_INF)], axis=0)
        _softmax_first(s, [vt_ref[h, i]], m_ref.at[h], acc_ref.at[h])

    @pl.when(i % 2 == 1)
    def _():
        for h in range(n_h):
            s = mask_past(h, scores(h, i - 1, 1), 2 * (i - 1))
            _softmax_step(s, [vt_ref[h, i - 1]], m_ref.at[h], acc_ref.at[h])

    def body(c, carry):
        for h in range(n_h):
            s = mask_past(h, scores(h, 2 * c, 2), 4 * c)
            _softmax_step(s, [vt_ref[h, 2 * c], vt_ref[h, 2 * c + 1]], m_ref.at[h], acc_ref.at[h])
        return carry

    lax.fori_loop(0, i // 2, body, 0)
    for h in range(n_h):
        o_ref[:, h * dv:(h + 1) * dv] = _softmax_finish(acc_ref.at[h], dv).astype(o_ref.dtype)


def moba_attention(qt, k, vt, k_mean, batch, seq, out_dtype):
    heads, _, hd, tq = qt.shape
    t = k.shape[1]
    dvx = vt.shape[2]
    dv = dvx - ONES_ROWS
    n_blk = seq // MOBA_BLOCK
    nq = seq // tq
    hb = ATTN_HEADS_PER_STEP
    return pl.pallas_call(
        functools.partial(_moba_attn_kernel, scale=hd ** -0.5, dv=dv),
        grid=(batch, heads // hb, nq),
        in_specs=[pl.BlockSpec((hb, 1, hd, tq), lambda b, h, i: (h, b * nq + i, 0, 0)),
                  pl.BlockSpec((hb, seq, hd), lambda b, h, i: (h, b, 0)),
                  pl.BlockSpec((hb, nq, dvx, tq), lambda b, h, i: (h, b, 0, 0)),
                  pl.BlockSpec((1, hb, n_blk, hd), lambda b, h, i: (b, h, 0, 0))],
        out_specs=pl.BlockSpec((tq, hb * dv), lambda b, h, i: (b * nq + i, h)),
        out_shape=jax.ShapeDtypeStruct((t, heads * dv), out_dtype),
        scratch_shapes=[pltpu.VMEM((hb, 1, tq), F32), pltpu.VMEM((hb, dvx, tq), F32),
                        pltpu.VMEM((hb, n_blk, tq), F32)],
        compiler_params=_params("parallel", "parallel", "arbitrary"),
        name="moba_attention",
    )(qt, k, vt, k_mean)


def _out_proj_kernel(a_ref, b_ref, c_ref, ga_ref, gb_ref, gc_ref, w_ref, x_ref, gate_ref, o_ref,
                     y_ref):
    @pl.when(pl.program_id(1) == 0)
    def _():
        wa, wb = a_ref.shape[1], b_ref.shape[1]
        y_ref[:, 0:wa] = _rms(a_ref[...].astype(F32), ga_ref[...]).astype(y_ref.dtype)
        y_ref[:, wa:wa + wb] = _rms(b_ref[...].astype(F32), gb_ref[...]).astype(y_ref.dtype)
        y_ref[:, wa + wb:] = _rms(c_ref[...].astype(F32), gc_ref[...]).astype(y_ref.dtype)

    o_ref[...] = x_ref[...] + gate_ref[0] * _dot(y_ref[...], w_ref[0])


def out_proj(y_a, y_b, y_c, g, w_layers, layer, x2, gate, seq, tm=512, tn=1024):
    t = y_a.shape[0]
    wa, wb, wc = y_a.shape[1], y_b.shape[1], y_c.shape[1]
    _, kd, n = w_layers.shape
    tm, tn = _tile(seq, tm), _tile(n, tn)
    per_b = seq // tm
    ga, gb, gc = g[:wa].reshape(1, wa), g[wa:wa + wb].reshape(1, wb), g[wa + wb:].reshape(1, wc)
    rows = lambda width: pl.BlockSpec((tm, width), lambda i, j: (i, 0))
    vec = lambda width: pl.BlockSpec((1, width), lambda i, j: (0, 0))
    return pl.pallas_call(
        _out_proj_kernel,
        grid=(t // tm, n // tn),
        in_specs=[rows(wa), rows(wb), rows(wc), vec(wa), vec(wb), vec(wc),
                  pl.BlockSpec((1, kd, tn), lambda i, j: (layer, 0, j)),
                  pl.BlockSpec((tm, tn), lambda i, j: (i, j)),
                  pl.BlockSpec((1, 1, tn), lambda i, j: (i // per_b, 0, j))],
        out_specs=pl.BlockSpec((tm, tn), lambda i, j: (i, j)),
        out_shape=jax.ShapeDtypeStruct((t, n), F32),
        scratch_shapes=[pltpu.VMEM((tm, kd), BF16)],
        compiler_params=_params("parallel", "arbitrary"),
        name="out_proj",
    )(y_a, y_b, y_c, ga, gb, gc, w_layers, x2, gate)


def _expert_weight_copies(w_hbm, layer, expert, col, stage, sem):
    width = stage.shape[1]
    return pltpu.make_async_copy(w_hbm.at[layer, expert, :, pl.ds(col, width)], stage, sem)


def _ffn_up_kernel(bexp_ref, first_ref, rows_ref, next_ref, x_ref, wg_hbm, wu_hbm, a_ref,
                   wg_st, wu_st, wg_bf, wu_bf, sem, *, layer):
    j = pl.program_id(0)
    b = pl.program_id(1)
    n_rows = rows_ref[b]
    half = x_ref.shape[0] // 2
    col = pl.multiple_of(j * wg_bf.shape[1], wg_bf.shape[1])

    def copies(expert):
        return (_expert_weight_copies(wg_hbm, layer, expert, col, wg_st, sem.at[0]),
                _expert_weight_copies(wu_hbm, layer, expert, col, wu_st, sem.at[1]))

    @pl.when(b == 0)
    def _():
        for c in copies(bexp_ref[0]):
            c.start()

    @pl.when((n_rows > 0) & (first_ref[b] == 1))
    def _():
        for c in copies(bexp_ref[b]):
            c.wait()
        wg_bf[...] = wg_st[...].astype(BF16)
        wu_bf[...] = wu_st[...].astype(BF16)

        @pl.when(next_ref[b] >= 0)
        def _():
            for c in copies(next_ref[b]):
                c.start()

    def act(x):
        gate = _dot(x, wg_bf[...])
        up = _dot(x, wu_bf[...])
        return (gate * jax.nn.sigmoid(gate) * up).astype(a_ref.dtype)

    @pl.when(n_rows > half)
    def _():
        a_ref[...] = act(x_ref[...])

    @pl.when((n_rows > 0) & (n_rows <= half))
    def _():
        a_ref[:half] = act(x_ref[:half])
        a_ref[half:] = jnp.zeros((half, a_ref.shape[1]), a_ref.dtype)

    @pl.when(n_rows == 0)
    def _():
        a_ref[...] = jnp.zeros_like(a_ref)


def _pack_bf16_pair(lo, hi):
    lo_bits = lax.bitcast_convert_type(lo.astype(BF16).astype(F32), PACKED) >> 16
    hi_bits = lax.bitcast_convert_type(hi.astype(BF16).astype(F32), PACKED) & jnp.uint32(0xFFFF0000)
    return hi_bits | lo_bits


def _unpack_bf16_pair(packed):
    lo = lax.bitcast_convert_type(packed << 16, F32)
    hi = lax.bitcast_convert_type(packed & jnp.uint32(0xFFFF0000), F32)
    return lo, hi


def _ffn_down_kernel(bexp_ref, first_ref, rows_ref, next_ref, a_ref, wd_hbm, o_ref,
                     wd_st, wd_bf, sem, *, layer):
    j = pl.program_id(0)
    b = pl.program_id(1)
    n_rows = rows_ref[b]
    half = a_ref.shape[0] // 2
    col = pl.multiple_of(j * wd_bf.shape[1], wd_bf.shape[1])

    def copy(expert):
        return _expert_weight_copies(wd_hbm, layer, expert, col, wd_st, sem.at[0])

    @pl.when(b == 0)
    def _():
        copy(bexp_ref[0]).start()

    @pl.when((n_rows > 0) & (first_ref[b] == 1))
    def _():
        copy(bexp_ref[b]).wait()
        wd_bf[...] = wd_st[...].astype(BF16)

        @pl.when(next_ref[b] >= 0)
        def _():
            copy(next_ref[b]).start()

    @pl.when(n_rows > half)
    def _():
        y = _dot(a_ref[...], wd_bf[...])
        o_ref[...] = _pack_bf16_pair(y[:half], y[half:])

    @pl.when((n_rows > 0) & (n_rows <= half))
    def _():
        y = _dot(a_ref[:half], wd_bf[...])
        o_ref[...] = _pack_bf16_pair(y, jnp.zeros_like(y))

    @pl.when(n_rows == 0)
    def _():
        o_ref[...] = jnp.zeros_like(o_ref)


def expert_ffn(xs, block_exp, block_first, block_rows, block_next, w_gate, w_up, w_down, layer):
    n_slots, d = xs.shape
    ff = w_gate.shape[3]
    fh, dh = ff // FF_SPLIT, d // FF_SPLIT
    n_blocks = n_slots // MOE_BLOCK
    hbm = pl.BlockSpec(memory_space=pl.ANY)
    up_spec = pltpu.PrefetchScalarGridSpec(
        num_scalar_prefetch=4,
        grid=(FF_SPLIT, n_blocks),
        in_specs=[pl.BlockSpec((MOE_BLOCK, d), lambda j, b, *_: (b, 0)), hbm, hbm],
        out_specs=pl.BlockSpec((MOE_BLOCK, fh), lambda j, b, *_: (b, j)),
        scratch_shapes=[pltpu.VMEM((d, fh), F32), pltpu.VMEM((d, fh), F32),
                        pltpu.VMEM((d, fh), BF16), pltpu.VMEM((d, fh), BF16),
                        pltpu.SemaphoreType.DMA((2,))],
    )
    act = pl.pallas_call(
        functools.partial(_ffn_up_kernel, layer=layer),
        grid_spec=up_spec,
        out_shape=jax.ShapeDtypeStruct((n_slots, ff), BF16),
        compiler_params=_params("arbitrary", "arbitrary"),
        name="expert_ffn_up",
    )(block_exp, block_first, block_rows, block_next, xs, w_gate, w_up)
    down_spec = pltpu.PrefetchScalarGridSpec(
        num_scalar_prefetch=4,
        grid=(FF_SPLIT, n_blocks),
        in_specs=[pl.BlockSpec((MOE_BLOCK, ff), lambda j, b, *_: (b, 0)), hbm],
        out_specs=pl.BlockSpec((MOE_BLOCK // 2, dh), lambda j, b, *_: (b, j)),
        scratch_shapes=[pltpu.VMEM((ff, dh), F32), pltpu.VMEM((ff, dh), BF16),
                        pltpu.SemaphoreType.DMA((1,))],
    )
    return pl.pallas_call(
        functools.partial(_ffn_down_kernel, layer=layer),
        grid_spec=down_spec,
        out_shape=jax.ShapeDtypeStruct((n_slots // 2, d), PACKED),
        compiler_params=_params("arbitrary", "arbitrary"),
        name="expert_ffn_down",
    )(block_exp, block_first, block_rows, block_next, act, w_down)


def _start_row_copies(idx_ref, base, n_rows, src_ref, dst_ref, sem):
    def issue(r, carry):
        row = idx_ref[base + r]
        pltpu.make_async_copy(src_ref.at[pl.ds(row, 1)], dst_ref.at[pl.ds(r, 1)], sem).start()
        return carry

    lax.fori_loop(0, n_rows, issue, 0, unroll=8)


def _wait_row_copies(n_rows, src_ref, dst_ref, sem):
    pltpu.make_async_copy(src_ref.at[pl.ds(0, n_rows)], dst_ref, sem).wait()


def _gather_cast_kernel(idx_ref, nused_ref, src_ref, o_ref, buf, sem):
    b = pl.program_id(0)
    rows = o_ref.shape[0]
    n_used = nused_ref[0]

    @pl.when((b == 0) & (n_used > 0))
    def _():
        _start_row_copies(idx_ref, 0, rows, src_ref, buf.at[0], sem.at[0])

    @pl.when(b + 1 < n_used)
    def _():
        nxt = (b + 1) % 2
        _start_row_copies(idx_ref, (b + 1) * rows, rows, src_ref, buf.at[nxt], sem.at[nxt])

    @pl.when(b < n_used)
    def _():
        cur = b % 2
        _wait_row_copies(rows, src_ref, buf.at[cur], sem.at[cur])
        o_ref[...] = buf[cur].astype(o_ref.dtype)

    @pl.when(b >= n_used)
    def _():
        o_ref[...] = jnp.zeros_like(o_ref)


def gather_rows_cast(src, idx, n_used_tiles, out_dtype):
    d = src.shape[1]
    m = idx.shape[0]
    rows = GATHER_ROWS
    grid_spec = pltpu.PrefetchScalarGridSpec(
        num_scalar_prefetch=2,
        grid=(m // rows,),
        in_specs=[pl.BlockSpec(memory_space=pl.ANY)],
        out_specs=pl.BlockSpec((rows, d), lambda b, ix, nu: (b, 0)),
        scratch_shapes=[pltpu.VMEM((2, rows, d), src.dtype), pltpu.SemaphoreType.DMA((2,))],
    )
    return pl.pallas_call(
        _gather_cast_kernel,
        grid_spec=grid_spec,
        out_shape=jax.ShapeDtypeStruct((m, d), out_dtype),
        compiler_params=_params("arbitrary"),
        name="gather_rows_cast",
    )(idx, n_used_tiles, src)


def _combine_kernel(pos_ref, x_ref, gate_ref, w_ref, half_ref, g_ref, sc_ref, sh_ref, ys_ref,
                    *rest, n_tok, final):
    if final:
        o_ref, ybuf, sem = rest
    else:
        o_ref, h_ref, ybuf, sem = rest
    i = pl.program_id(0)
    n_tiles = pl.num_programs(0)
    tm = x_ref.shape[0]

    def start_tile(tile, slot):
        for kk in range(TOP_K):
            _start_row_copies(pos_ref, kk * n_tok + tile * tm, tm, ys_ref, ybuf.at[slot, kk],
                              sem.at[slot])

    @pl.when(i == 0)
    def _():
        start_tile(0, 0)

    @pl.when(i + 1 < n_tiles)
    def _():
        start_tile(i + 1, (i + 1) % 2)

    cur = i % 2
    for kk in range(TOP_K):
        _wait_row_copies(tm, ys_ref, ybuf.at[cur, kk], sem.at[cur])
    w = w_ref[...]
    upper = half_ref[...] > 0.5
    acc = None
    for kk in range(TOP_K):
        lo, hi = _unpack_bf16_pair(ybuf[cur, kk])
        term = w[:, kk:kk + 1] * jnp.where(upper[:, kk:kk + 1], hi, lo)
        acc = term if acc is None else acc + term
    x_new = x_ref[...] + gate_ref[0] * acc
    if final:
        o_ref[...] = _rms(x_new, g_ref[...])
    else:
        o_ref[...] = x_new
        h_ref[...] = (_rms(x_new, g_ref[...]) * sc_ref[0] + sh_ref[0]).astype(h_ref.dtype)


def moe_combine(x2, gate, ys, pos, w_tok, half_tok, seq, norm_g, sc1p=None, sh=None):
    t, d = x2.shape
    tm = _tile(seq, GATHER_ROWS)
    per_b = seq // tm
    final = sc1p is None
    if final:
        sc1p = sh = jnp.zeros((x2.shape[0] // seq, 1, d), F32)
    row = pl.BlockSpec((tm, d), lambda i, ps: (i, 0))
    mod = pl.BlockSpec((1, 1, d), lambda i, ps: (i // per_b, 0, 0))
    tok = pl.BlockSpec((tm, TOP_K), lambda i, ps: (i, 0))
    grid_spec = pltpu.PrefetchScalarGridSpec(
        num_scalar_prefetch=1,
        grid=(t // tm,),
        in_specs=[row, mod, tok, tok, pl.BlockSpec((1, d), lambda i, ps: (0, 0)), mod, mod,
                  pl.BlockSpec(memory_space=pl.ANY)],
        out_specs=row if final else [row, row],
        scratch_shapes=[pltpu.VMEM((2, TOP_K, tm, d), ys.dtype), pltpu.SemaphoreType.DMA((2,))],
    )
    out_shape = jax.ShapeDtypeStruct((t, d), F32)
    return pl.pallas_call(
        functools.partial(_combine_kernel, n_tok=t, final=final),
        grid_spec=grid_spec,
        out_shape=out_shape if final else [out_shape, jax.ShapeDtypeStruct((t, d), BF16)],
        compiler_params=_params("arbitrary"),
        name="moe_combine",
    )(pos, x2, gate, w_tok, half_tok, norm_g.reshape(1, d), sc1p, sh, ys)


def _group_by_expert(e_idx):
    n_tok = e_idx.shape[1]
    n_asg = n_tok * TOP_K
    flat_e = e_idx.reshape(n_asg)
    onehot = (flat_e[:, None] == jnp.arange(N_EXPERTS)[None, :]).astype(jnp.int32)
    running = jnp.cumsum(onehot, axis=0)
    rank = jnp.sum(running * onehot, axis=1) - 1
    sizes = running[-1]
    padded = (sizes + MOE_BLOCK - 1) // MOE_BLOCK * MOE_BLOCK
    pad_end = jnp.cumsum(padded)
    pad_start = pad_end - padded
    dest = (jnp.sum(pad_start[None, :] * onehot, axis=1) + rank).astype(jnp.int32)
    n_blocks = -(-n_asg // MOE_BLOCK) + N_EXPERTS
    n_slots = n_blocks * MOE_BLOCK
    slot_tok = (jnp.arange(n_slots, dtype=jnp.int32) % n_tok).at[dest].set(
        jnp.arange(n_asg, dtype=jnp.int32) % n_tok, mode="promise_in_bounds")
    block_start = jnp.arange(n_blocks, dtype=jnp.int32) * MOE_BLOCK
    block_exp = jnp.minimum(jnp.sum(block_start[:, None] >= pad_end[None, :], axis=1),
                            N_EXPERTS - 1).astype(jnp.int32)
    n_used = (pad_end[-1] // MOE_BLOCK).astype(jnp.int32)
    block_exp = jnp.where(jnp.arange(n_blocks) < n_used, block_exp, block_exp[n_used - 1])
    prev_exp = jnp.concatenate([jnp.full((1,), -1, jnp.int32), block_exp[:-1]])
    block_first = (block_exp != prev_exp).astype(jnp.int32)
    block_rows = jnp.clip(pad_start[block_exp] + sizes[block_exp] - block_start, 0, MOE_BLOCK)
    block_rows = jnp.where(jnp.arange(n_blocks) < n_used, block_rows, 0).astype(jnp.int32)
    blk = jnp.arange(n_blocks, dtype=jnp.int32)
    starts = jnp.where((block_first == 1) & (block_rows > 0), blk, n_blocks)
    later = lax.cummin(jnp.concatenate([starts[1:], jnp.full((1,), n_blocks, jnp.int32)]),
                       axis=0, reverse=True)
    block_next = jnp.where(later < n_blocks, block_exp[jnp.minimum(later, n_blocks - 1)],
                           -1).astype(jnp.int32)
    return (dest.reshape(TOP_K, n_tok), slot_tok, block_exp, block_first, block_rows, block_next,
            n_used.reshape(1))


def _rot_cols(w):
    half = w.shape[-1] // 2
    return jnp.concatenate([-w[..., half:], w[..., :half]], axis=-1)


def _prep_weights(w_in, w_uq, w_ukv, w_out):
    layers = w_in.shape[0]
    a_end = Q_LORA + KV_LORA + MLA_ROPE
    b_end = a_end + 3 * CONV_CH
    w_kpe = w_in[:, :, Q_LORA + KV_LORA:a_end]
    w_a = jnp.concatenate([w_in[:, :, :a_end], _rot_cols(w_kpe)], axis=2).astype(BF16)
    w_b = w_in[:, :, a_end:b_end].astype(BF16)
    w_c = w_in[:, :, b_end:].astype(BF16)
    wq = w_uq.reshape(layers, Q_LORA, MLA_HEADS, MLA_NOPE + MLA_ROPE)
    wq_rope = wq[..., MLA_NOPE:]
    wq = jnp.concatenate([wq, _rot_cols(wq_rope)], axis=-1).transpose(0, 2, 1, 3).astype(BF16)
    return dict(w_a=w_a, w_b=w_b, w_c=w_c, w_q=wq, w_kv=w_ukv.astype(BF16),
                w_out=w_out.astype(BF16))


def _rope_tables(seq, dim):
    inv = 1.0 / (ROPE_THETA ** (jnp.arange(0, dim, 2, dtype=F32) / dim))
    ang = jnp.arange(seq, dtype=F32)[:, None] * inv[None, :]
    return jnp.cos(ang), jnp.sin(ang)


def kernel(x, c, w_mod, mod_table, mix_norm_g, w_in, q_norm_g, kv_norm_g, w_uq, w_ukv, conv_w,
           group_norm_g, w_out, ffn_norm_g, w_router, router_bias, w_gate, w_up, w_down,
           final_norm_g):
    batch, seq, d = x.shape
    depth = w_in.shape[0]
    t = batch * seq
    x2 = x.reshape(t, d)

    cos_a, sin_a = _rope_tables(seq, MLA_ROPE)
    zeros_a = jnp.zeros_like(cos_a)
    cos_p = jnp.concatenate([cos_a, cos_a, zeros_a, zeros_a], axis=1)
    sin_p = jnp.concatenate([sin_a, sin_a, zeros_a, zeros_a], axis=1)
    cos_b, sin_b = _rope_tables(seq, MOBA_HD)
    cos_f = jnp.concatenate([cos_b, cos_b], axis=1)
    sin_s = jnp.concatenate([-sin_b, sin_b], axis=1)

    c_pad = jnp.zeros((SUBLANE, d), F32).at[:batch].set(c)
    mod_shared = mod_matmul(c_pad, w_mod)[:batch].reshape(batch, N_MOD, d)

    wr_t = w_router.T
    wr_hi = wr_t.astype(BF16)
    wr_lo = (wr_t - wr_hi.astype(F32)).astype(BF16)
    router = (wr_hi, wr_lo, router_bias.astype(F32).reshape(N_EXPERTS, 1))
    mla_scale = (MLA_NOPE + MLA_ROPE) ** -0.5

    mods = []
    for l in range(depth):
        mod = mod_shared + mod_table[l][None]
        mods.append([mod[:, i][:, None, :] for i in range(N_MOD)])

    p = _prep_weights(w_in, w_uq, w_ukv, w_out)
    h = norm_mod(x2, mix_norm_g[0], 1.0 + mods[0][1], mods[0][0], seq)
    for l in range(depth):
        sh1, sc1, g1, sh2, sc2, g2 = mods[l]

        proj_a = matmul(h, p["w_a"], l, F32, tm=512)
        bch = matmul(h, p["w_b"], l, F32)
        qkv = matmul(h, p["w_c"], l, F32)

        q_a = mla_q_up(proj_a, q_norm_g[l], p["w_q"], l, cos_p, sin_p, seq, mla_scale)
        k_a, v_a = mla_kv_up(proj_a, kv_norm_g[l], p["w_kv"], l, cos_p, sin_p, seq)
        y_a = mla_attention(q_a, k_a, v_a, batch, seq, BF16)

        y_b = conv_mixer(bch, conv_w[l], seq, F32)

        q_c, k_c, v_c, k_mean = moba_prep(qkv, cos_f, sin_s, seq)
        n_blk = seq // MOBA_BLOCK
        k_mean = k_mean.reshape(batch, n_blk, MOBA_HEADS, MOBA_HD).transpose(0, 2, 1, 3)
        y_c = moba_attention(q_c, k_c, v_c, k_mean, batch, seq, BF16)

        x2 = out_proj(y_a, y_b, y_c, group_norm_g[l], p["w_out"], l, x2, g1, seq)

        h2, e_idx, gates = norm_mod(x2, ffn_norm_g[l], 1.0 + sc2, sh2, seq, out_dtype=F32,
                                    router=router)
        pos, slot_tok, block_exp, block_first, block_rows, block_next, n_used = _group_by_expert(e_idx)
        xs = gather_rows_cast(h2, slot_tok, n_used * (MOE_BLOCK // GATHER_ROWS), BF16)
        ys = expert_ffn(xs, block_exp, block_first, block_rows, block_next, w_gate, w_up, w_down,
                        l)
        sub = pos % MOE_BLOCK
        ys_row = ((pos // MOE_BLOCK) * (MOE_BLOCK // 2) + sub % (MOE_BLOCK // 2)).reshape(-1)
        ys_half = (sub // (MOE_BLOCK // 2)).astype(F32).T
        if l + 1 < depth:
            nxt = mods[l + 1]
            x2, h = moe_combine(x2, g2, ys, ys_row, gates.T, ys_half, seq, mix_norm_g[l + 1],
                                1.0 + nxt[1], nxt[0])
        else:
            out = moe_combine(x2, g2, ys, ys_row, gates.T, ys_half, seq, final_norm_g)

    return out.reshape(batch, seq, d)
```

```python
import functools

import jax
import jax.numpy as jnp
from jax import lax
from jax.experimental import pallas as pl
from jax.experimental.pallas import tpu as pltpu

MLA_HEADS = 16
MLA_NOPE = 128
MLA_ROPE = 64
MLA_V = 128
Q_LORA = 1024
KV_LORA = 512
MLA_DK_PAD = 256
CONV_CH = 1024
CONV_W = 3
MOBA_HEADS = 8
MOBA_HD = 128
MOBA_BLOCK = 256
MOBA_TOPK = 3
ROPE_THETA = 10000.0
EPS = 1e-6
N_MOD = 6
N_EXPERTS = 16
N_GROUPS = 4
TOP_K = 2
MOE_BLOCK = 512
GATHER_ROWS = 256
FF_SPLIT = 2
UP_HEADS_PER_STEP = 4
ATTN_HEADS_PER_STEP = 4
ATTN_TILE = 512
ONES_ROWS = 16

V7X_VMEM_LIMIT_BYTES = 56 * 1024 * 1024
LANE = 128
SUBLANE = 8

F32 = jnp.float32
BF16 = jnp.bfloat16
PACKED = jnp.uint32
NEG_INF = float("-inf")


def _params(*sem):
    return pltpu.CompilerParams(dimension_semantics=sem, vmem_limit_bytes=V7X_VMEM_LIMIT_BYTES)


def _dot(a, b, precision=None):
    return jnp.dot(a, b, preferred_element_type=F32, precision=precision)


def _dot_nt(a, b):
    return lax.dot_general(a, b, (((1,), (1,)), ((), ())), preferred_element_type=F32)


def _tile(dim, want):
    return want if dim % want == 0 else dim


def _mod_kernel(c_ref, w0_ref, w1_ref, o_ref):
    @pl.when(pl.program_id(0) == 0)
    def _():
        o_ref[...] = jnp.zeros_like(o_ref)

    c = c_ref[...]
    a = (c * jax.nn.sigmoid(c)).astype(BF16)
    half = w0_ref.shape[1]
    o_ref[:, :half] += _dot(a, w0_ref[...].astype(BF16))
    o_ref[:, half:] += _dot(a, w1_ref[...].astype(BF16))


def mod_matmul(c_pad, w_mod):
    rows, d = c_pad.shape
    n = w_mod.shape[1]
    tk = _tile(d, LANE)
    return pl.pallas_call(
        _mod_kernel,
        grid=(d // tk,),
        in_specs=[pl.BlockSpec((rows, tk), lambda k: (0, k)),
                  pl.BlockSpec((tk, n // 2), lambda k: (k, 0)),
                  pl.BlockSpec((tk, n // 2), lambda k: (k, 1))],
        out_specs=pl.BlockSpec((rows, n), lambda k: (0, 0)),
        out_shape=jax.ShapeDtypeStruct((rows, n), F32),
        compiler_params=_params("arbitrary"),
        name="mod_matmul",
    )(c_pad, w_mod, w_mod)


def _rms(x, g):
    return x * lax.rsqrt(jnp.mean(x * x, axis=-1, keepdims=True) + EPS) * g


def _norm_mod_kernel(x_ref, g_ref, sc_ref, sh_ref, o_ref):
    y = _rms(x_ref[...], g_ref[...])
    o_ref[...] = (y * sc_ref[0] + sh_ref[0]).astype(o_ref.dtype)


def _top2_sum(a, b, c, d):
    hi1, lo1 = jnp.maximum(a, b), jnp.minimum(a, b)
    hi2, lo2 = jnp.maximum(c, d), jnp.minimum(c, d)
    return jnp.maximum(hi1, hi2) + jnp.maximum(jnp.minimum(hi1, hi2), jnp.maximum(lo1, lo2))


def _norm_router_kernel(x_ref, g_ref, sc_ref, sh_ref, whi_ref, wlo_ref, bias_ref,
                        o_ref, e_ref, gate_ref):
    h = _rms(x_ref[...], g_ref[...]) * sc_ref[0] + sh_ref[0]
    hi = h.astype(BF16)
    o_ref[...] = h.astype(o_ref.dtype)
    lo = (h - hi.astype(F32)).astype(BF16)
    whi, wlo = whi_ref[...], wlo_ref[...]
    logits = _dot_nt(whi, hi) + _dot_nt(wlo, hi) + _dot_nt(whi, lo)
    scores = jax.nn.sigmoid(logits)
    biased = scores + bias_ref[...]
    per_grp = N_EXPERTS // N_GROUPS
    b_rows = [biased[e:e + 1, :] for e in range(N_EXPERTS)]
    s_rows = [scores[e:e + 1, :] for e in range(N_EXPERTS)]
    grp_scores = [_top2_sum(*b_rows[g * per_grp:(g + 1) * per_grp]) for g in range(N_GROUPS)]
    best, grp = grp_scores[0], jnp.zeros(grp_scores[0].shape, jnp.int32)
    for g in range(1, N_GROUPS):
        better = grp_scores[g] > best
        grp = jnp.where(better, g, grp)
        best = jnp.where(better, grp_scores[g], best)
    b4, s4 = b_rows[:per_grp], s_rows[:per_grp]
    for g in range(1, N_GROUPS):
        in_g = grp == g
        b4 = [jnp.where(in_g, b_rows[g * per_grp + j], b4[j]) for j in range(per_grp)]
        s4 = [jnp.where(in_g, s_rows[g * per_grp + j], s4[j]) for j in range(per_grp)]
    v1, i1 = b4[0], jnp.zeros(grp.shape, jnp.int32)
    for j in range(1, per_grp):
        better = b4[j] > v1
        i1 = jnp.where(better, j, i1)
        v1 = jnp.where(better, b4[j], v1)
    v2, i2 = jnp.full(v1.shape, NEG_INF, F32), jnp.zeros(grp.shape, jnp.int32)
    for j in range(per_grp):
        better = (i1 != j) & (b4[j] > v2)
        i2 = jnp.where(better, j, i2)
        v2 = jnp.where(better, b4[j], v2)
    g1, g2 = s4[0], s4[0]
    for j in range(1, per_grp):
        g1 = jnp.where(i1 == j, s4[j], g1)
        g2 = jnp.where(i2 == j, s4[j], g2)
    total = g1 + g2
    e_ref[...] = jnp.concatenate([grp * per_grp + i1, grp * per_grp + i2], axis=0)
    gate_ref[...] = jnp.concatenate([g1 / total, g2 / total], axis=0)


def norm_mod(x2, g, sc1p, sh, seq, out_dtype=BF16, router=None):
    t, d = x2.shape
    tm = _tile(seq, 256)
    per_b = seq // tm
    row = pl.BlockSpec((tm, d), lambda i: (i, 0))
    vec = pl.BlockSpec((1, d), lambda i: (0, 0))
    mod = pl.BlockSpec((1, 1, d), lambda i: (i // per_b, 0, 0))
    g2 = g.reshape(1, d)
    if router is None:
        return pl.pallas_call(
            _norm_mod_kernel, grid=(t // tm,), in_specs=[row, vec, mod, mod], out_specs=row,
            out_shape=jax.ShapeDtypeStruct((t, d), out_dtype),
            compiler_params=_params("arbitrary"), name="norm_mod")(x2, g2, sc1p, sh)
    whi_t, wlo_t, bias = router
    ne = whi_t.shape[0]
    wspec = pl.BlockSpec((ne, d), lambda i: (0, 0))
    kspec = pl.BlockSpec((TOP_K, tm), lambda i: (0, i))
    return pl.pallas_call(
        _norm_router_kernel, grid=(t // tm,),
        in_specs=[row, vec, mod, mod, wspec, wspec, pl.BlockSpec((ne, 1), lambda i: (0, 0))],
        out_specs=[row, kspec, kspec],
        out_shape=[jax.ShapeDtypeStruct((t, d), out_dtype),
                   jax.ShapeDtypeStruct((TOP_K, t), jnp.int32),
                   jax.ShapeDtypeStruct((TOP_K, t), F32)],
        compiler_params=_params("arbitrary"), name="norm_router")(x2, g2, sc1p, sh, whi_t, wlo_t, bias)


def _mm_kernel(a_ref, w_ref, o_ref):
    o_ref[...] = _dot(a_ref[...], w_ref[0]).astype(o_ref.dtype)


def matmul(a, w_layers, layer, out_dtype, tm=1024, tn=1024):
    m, kd = a.shape
    n = w_layers.shape[2]
    tm, tn = _tile(m, tm), _tile(n, tn)
    return pl.pallas_call(
        _mm_kernel,
        grid=(m // tm, n // tn),
        in_specs=[pl.BlockSpec((tm, kd), lambda i, j: (i, 0)),
                  pl.BlockSpec((1, kd, tn), lambda i, j: (layer, 0, j))],
        out_specs=pl.BlockSpec((tm, tn), lambda i, j: (i, j)),
        out_shape=jax.ShapeDtypeStruct((m, n), out_dtype),
        compiler_params=_params("parallel", "parallel"),
        name="matmul",
    )(a, w_layers)


def _rope_pair(r2, cos_ref, sin_ref):
    return r2 * cos_ref[...] + pltpu.roll(r2, MLA_ROPE, 1) * sin_ref[...]


def _with_ones_rows(vt):
    return jnp.concatenate([vt, jnp.ones((ONES_ROWS, vt.shape[1]), vt.dtype)], axis=0)


def _qup_kernel(cq_ref, g_ref, w_ref, cos_ref, sin_ref, o_ref, an_ref, *, scale):
    @pl.when(pl.program_id(1) == 0)
    def _():
        an_ref[...] = _rms(cq_ref[...], g_ref[...]).astype(BF16)

    an = an_ref[...]
    for hh in range(w_ref.shape[1]):
        res = _dot(an, w_ref[0, hh])
        roped = _rope_pair(res[:, MLA_NOPE:], cos_ref, sin_ref)
        q = jnp.concatenate([res[:, :MLA_NOPE], roped], axis=1) * scale
        o_ref[hh, 0] = q.T.astype(o_ref.dtype)


def mla_q_up(proj_a, g, w_q, layer, cos_p, sin_p, seq, scale):
    t = proj_a.shape[0]
    heads = w_q.shape[1]
    hb = UP_HEADS_PER_STEP
    tm = _tile(seq, ATTN_TILE)
    per_b = seq // tm
    return pl.pallas_call(
        functools.partial(_qup_kernel, scale=scale),
        grid=(t // tm, heads // hb),
        in_specs=[pl.BlockSpec((tm, Q_LORA), lambda i, h: (i, 0)),
                  pl.BlockSpec((1, Q_LORA), lambda i, h: (0, 0)),
                  pl.BlockSpec((1, hb, Q_LORA, MLA_DK_PAD), lambda i, h: (layer, h, 0, 0)),
                  pl.BlockSpec((tm, LANE), lambda i, h: (i % per_b, 0)),
                  pl.BlockSpec((tm, LANE), lambda i, h: (i % per_b, 0))],
        out_specs=pl.BlockSpec((hb, 1, MLA_DK_PAD, tm), lambda i, h: (h, i, 0, 0)),
        out_shape=jax.ShapeDtypeStruct((heads, t // tm, MLA_DK_PAD, tm), BF16),
        scratch_shapes=[pltpu.VMEM((tm, Q_LORA), BF16)],
        compiler_params=_params("parallel", "arbitrary"),
        name="mla_q_up",
    )(proj_a, g.reshape(1, Q_LORA), w_q, cos_p, sin_p)


def _kvup_kernel(ckv_ref, g_ref, w_ref, kpe_ref, cos_ref, sin_ref, k_ref, vt_ref, an_ref):
    @pl.when(pl.program_id(1) == 0)
    def _():
        an_ref[...] = _rms(ckv_ref[...], g_ref[...]).astype(BF16)

    an = an_ref[...]
    k_rot = _rope_pair(kpe_ref[...], cos_ref, sin_ref)
    hw = MLA_NOPE + MLA_V
    for hh in range(k_ref.shape[0]):
        res = _dot(an, w_ref[0, :, hh * hw:(hh + 1) * hw])
        k_ref[hh] = jnp.concatenate([res[:, :MLA_NOPE], k_rot], axis=1).astype(k_ref.dtype)
        vt_ref[hh, 0] = _with_ones_rows(res[:, MLA_NOPE:].T).astype(vt_ref.dtype)


def mla_kv_up(proj_a, g, w_kv, layer, cos_p, sin_p, seq):
    t = proj_a.shape[0]
    hw = MLA_NOPE + MLA_V
    heads = w_kv.shape[2] // hw
    hb = UP_HEADS_PER_STEP
    tm = _tile(seq, ATTN_TILE)
    per_b = seq // tm
    return pl.pallas_call(
        _kvup_kernel,
        grid=(t // tm, heads // hb),
        in_specs=[pl.BlockSpec((tm, KV_LORA), lambda i, h: (i, Q_LORA // KV_LORA)),
                  pl.BlockSpec((1, KV_LORA), lambda i, h: (0, 0)),
                  pl.BlockSpec((1, KV_LORA, hb * hw), lambda i, h: (layer, 0, h)),
                  pl.BlockSpec((tm, LANE), lambda i, h: (i, (Q_LORA + KV_LORA) // LANE)),
                  pl.BlockSpec((tm, LANE), lambda i, h: (i % per_b, 0)),
                  pl.BlockSpec((tm, LANE), lambda i, h: (i % per_b, 0))],
        out_specs=[pl.BlockSpec((hb, tm, MLA_DK_PAD), lambda i, h: (h, i, 0)),
                   pl.BlockSpec((hb, 1, MLA_V + ONES_ROWS, tm), lambda i, h: (h, i, 0, 0))],
        out_shape=[jax.ShapeDtypeStruct((heads, t, MLA_DK_PAD), BF16),
                   jax.ShapeDtypeStruct((heads, t // tm, MLA_V + ONES_ROWS, tm), BF16)],
        scratch_shapes=[pltpu.VMEM((tm, KV_LORA), BF16)],
        compiler_params=_params("parallel", "arbitrary"),
        name="mla_kv_up",
    )(proj_a, g.reshape(1, KV_LORA), w_kv, proj_a, cos_p, sin_p)


def _pv(vts, p):
    rows = p.shape[0] // len(vts)
    out = _dot(vts[0], p[:rows].astype(vts[0].dtype))
    for n in range(1, len(vts)):
        out = out + _dot(vts[n], p[n * rows:(n + 1) * rows].astype(vts[n].dtype))
    return out


def _softmax_first(s, vts, m_ref, acc_ref):
    m = jnp.max(s, axis=0, keepdims=True)
    m_ref[...] = m
    acc_ref[...] = _pv(vts, jnp.exp(s - m))


def _softmax_step(s, vts, m_ref, acc_ref):
    m_prev = m_ref[...]
    m_new = jnp.maximum(m_prev, jnp.max(s, axis=0, keepdims=True))
    alpha = jnp.exp(m_prev - m_new)
    acc_ref[...] = alpha * acc_ref[...] + _pv(vts, jnp.exp(s - m_new))
    m_ref[...] = m_new


def _softmax_finish(acc_ref, dv):
    acc = acc_ref[...]
    return (acc[:dv, :] / acc[dv:dv + 1, :]).T


def _causal_mask_t(s):
    key = lax.broadcasted_iota(jnp.int32, s.shape, 0)
    qry = lax.broadcasted_iota(jnp.int32, s.shape, 1)
    return jnp.where(key <= qry, s, NEG_INF)


def _mla_attn_kernel(qt_ref, k_ref, vt_ref, o_ref, m_ref, acc_ref, *, tq, dv):
    i = pl.program_id(2)
    n_h = qt_ref.shape[0]
    qts = [qt_ref[h, 0] for h in range(n_h)]

    def scores(h, j, n):
        off = pl.multiple_of(j * tq, tq)
        return _dot(k_ref[h, pl.ds(off, n * tq), :], qts[h])

    for h in range(n_h):
        _softmax_first(_causal_mask_t(scores(h, i, 1)), [vt_ref[h, i]], m_ref.at[h], acc_ref.at[h])

    @pl.when(i % 2 == 1)
    def _():
        for h in range(n_h):
            _softmax_step(scores(h, i - 1, 1), [vt_ref[h, i - 1]], m_ref.at[h], acc_ref.at[h])

    def body(c, carry):
        for h in range(n_h):
            _softmax_step(scores(h, 2 * c, 2), [vt_ref[h, 2 * c], vt_ref[h, 2 * c + 1]],
                          m_ref.at[h], acc_ref.at[h])
        return carry

    lax.fori_loop(0, i // 2, body, 0)
    for h in range(n_h):
        o_ref[:, h * dv:(h + 1) * dv] = _softmax_finish(acc_ref.at[h], dv).astype(o_ref.dtype)


def mla_attention(qt, k, vt, batch, seq, out_dtype):
    heads, _, dk, tq = qt.shape
    t = k.shape[1]
    dvx = vt.shape[2]
    dv = dvx - ONES_ROWS
    nq = seq // tq
    hb = ATTN_HEADS_PER_STEP
    return pl.pallas_call(
        functools.partial(_mla_attn_kernel, tq=tq, dv=dv),
        grid=(batch, heads // hb, nq),
        in_specs=[pl.BlockSpec((hb, 1, dk, tq), lambda b, h, i: (h, b * nq + i, 0, 0)),
                  pl.BlockSpec((hb, seq, dk), lambda b, h, i: (h, b, 0)),
                  pl.BlockSpec((hb, nq, dvx, tq), lambda b, h, i: (h, b, 0, 0))],
        out_specs=pl.BlockSpec((tq, hb * dv), lambda b, h, i: (b * nq + i, h)),
        out_shape=jax.ShapeDtypeStruct((t, heads * dv), out_dtype),
        scratch_shapes=[pltpu.VMEM((hb, 1, tq), F32), pltpu.VMEM((hb, dvx, tq), F32)],
        compiler_params=_params("parallel", "parallel", "arbitrary"),
        name="mla_attention",
    )(qt, k, vt)


def _conv_kernel(b_ref, c_ref, h_ref, w_ref, o_ref, carry_ref, *, per_b):
    i = pl.program_id(0)

    @pl.when(i % per_b == 0)
    def _():
        carry_ref[...] = jnp.zeros_like(carry_ref)

    tm = o_ref.shape[0]
    w0, w1, w2 = w_ref[0:1, :], w_ref[1:2, :], w_ref[2:3, :]
    u = c_ref[...] * h_ref[...]
    y = w0 * pltpu.roll(u, 2, 0) + w1 * pltpu.roll(u, 1, 0) + w2 * u
    o_ref[...] = (b_ref[...] * y).astype(o_ref.dtype)
    u8 = u[0:SUBLANE, :]
    tail = carry_ref[...]
    r8 = lax.broadcasted_iota(jnp.int32, u8.shape, 0)
    p1 = jnp.where(r8 < 1, pltpu.roll(tail, 1, 0), pltpu.roll(u8, 1, 0))
    p2 = jnp.where(r8 < 2, pltpu.roll(tail, 2, 0), pltpu.roll(u8, 2, 0))
    y8 = w0 * p2 + w1 * p1 + w2 * u8
    o_ref[0:SUBLANE, :] = (b_ref[0:SUBLANE, :] * y8).astype(o_ref.dtype)
    carry_ref[...] = u[tm - SUBLANE:tm, :]


def conv_mixer(bch, conv_w, seq, out_dtype):
    t = bch.shape[0]
    ch = conv_w.shape[1]
    tm = _tile(seq, 256)
    per_b = seq // tm
    return pl.pallas_call(
        functools.partial(_conv_kernel, per_b=per_b),
        grid=(t // tm,),
        in_specs=[pl.BlockSpec((tm, ch), lambda i: (i, 0)),
                  pl.BlockSpec((tm, ch), lambda i: (i, 1)),
                  pl.BlockSpec((tm, ch), lambda i: (i, 2)),
                  pl.BlockSpec((CONV_W, ch), lambda i: (0, 0))],
        out_specs=pl.BlockSpec((tm, ch), lambda i: (i, 0)),
        out_shape=jax.ShapeDtypeStruct((t, ch), out_dtype),
        scratch_shapes=[pltpu.VMEM((SUBLANE, ch), F32)],
        compiler_params=_params("arbitrary"),
        name="conv_mixer",
    )(bch, bch, bch, conv_w)


def _moba_prep_kernel(q_ref, k_ref, v_ref, cos_ref, sin_ref, qo_ref, ko_ref, vo_ref, km_ref):
    heads = ko_ref.shape[0]
    cos, sin = cos_ref[...], sin_ref[...]
    n_sub = km_ref.shape[0]
    means = [[] for _ in range(n_sub)]
    for h in range(heads):
        sl = slice(h * MOBA_HD, (h + 1) * MOBA_HD)
        qh = q_ref[:, sl]
        kh = k_ref[:, sl]
        qr = qh * cos + pltpu.roll(qh, MOBA_HD // 2, 1) * sin
        qo_ref[h, 0] = qr.T.astype(qo_ref.dtype)
        kr = kh * cos + pltpu.roll(kh, MOBA_HD // 2, 1) * sin
        ko_ref[h] = kr.astype(ko_ref.dtype)
        vo_ref[h, 0] = _with_ones_rows(v_ref[:, sl].T).astype(vo_ref.dtype)
        for s in range(n_sub):
            means[s].append(jnp.mean(kr[s * MOBA_BLOCK:(s + 1) * MOBA_BLOCK], axis=0, keepdims=True))
    for s in range(n_sub):
        km_ref[s] = jnp.concatenate(means[s], axis=0)


def moba_prep(qkv, cos_f, sin_s, seq):
    t = qkv.shape[0]
    width = qkv.shape[1] // 3
    heads = width // MOBA_HD
    tm = 2 * MOBA_BLOCK
    assert seq % tm == 0
    per_b = seq // tm
    nt = t // tm
    return pl.pallas_call(
        _moba_prep_kernel,
        grid=(nt,),
        in_specs=[pl.BlockSpec((tm, width), lambda i: (i, 0)),
                  pl.BlockSpec((tm, width), lambda i: (i, 1)),
                  pl.BlockSpec((tm, width), lambda i: (i, 2)),
                  pl.BlockSpec((tm, MOBA_HD), lambda i: (i % per_b, 0)),
                  pl.BlockSpec((tm, MOBA_HD), lambda i: (i % per_b, 0))],
        out_specs=[pl.BlockSpec((heads, 1, MOBA_HD, tm), lambda i: (0, i, 0, 0)),
                   pl.BlockSpec((heads, tm, MOBA_HD), lambda i: (0, i, 0)),
                   pl.BlockSpec((heads, 1, MOBA_HD + ONES_ROWS, tm), lambda i: (0, i, 0, 0)),
                   pl.BlockSpec((2, heads, MOBA_HD), lambda i: (i, 0, 0))],
        out_shape=[jax.ShapeDtypeStruct((heads, nt, MOBA_HD, tm), BF16),
                   jax.ShapeDtypeStruct((heads, t, MOBA_HD), BF16),
                   jax.ShapeDtypeStruct((heads, nt, MOBA_HD + ONES_ROWS, tm), BF16),
                   jax.ShapeDtypeStruct((2 * nt, heads, MOBA_HD), F32)],
        compiler_params=_params("arbitrary"),
        name="moba_prep",
    )(qkv, qkv, qkv, cos_f, sin_s)


def _moba_attn_kernel(qt_ref, k_ref, vt_ref, km_ref, o_ref, m_ref, acc_ref, sel_ref, *, scale, dv):
    i = pl.program_id(2)
    n_h = qt_ref.shape[0]
    n_blk = km_ref.shape[2]
    tq = qt_ref.shape[3]
    blk_w = MOBA_BLOCK
    blk = lax.broadcasted_iota(jnp.int32, (n_blk, tq), 0)
    own = 2 * i + (lax.broadcasted_iota(jnp.int32, (n_blk, tq), 1) >= blk_w).astype(jnp.int32)
    past = blk < own
    qts = []
    for h in range(n_h):
        qtf = qt_ref[h, 0].astype(F32)
        g = _dot(km_ref[0, h], qtf, precision=lax.Precision.HIGHEST)
        g = jnp.where(past, g, NEG_INF)
        sel = jnp.zeros(g.shape, F32)
        for _ in range(min(MOBA_TOPK, n_blk)):
            mx = jnp.max(g, axis=0, keepdims=True)
            first = jnp.min(jnp.where(g == mx, blk, n_blk), axis=0, keepdims=True)
            hit = blk == first
            sel = jnp.where(hit, 1.0, sel)
            g = jnp.where(hit, NEG_INF, g)
        sel_ref[h] = jnp.where(past, sel, 0.0)
        qts.append((qtf * scale).astype(k_ref.dtype))

    def scores(h, p, n):
        off = pl.multiple_of(p * tq, tq)
        return _dot(k_ref[h, pl.ds(off, n * tq), :], qts[h])

    def chosen(h, n):
        return sel_ref[h, pl.ds(n, 1), :] > 0.5

    def mask_past(h, s, first_blk):
        parts = [jnp.where(chosen(h, first_blk + n), s[n * blk_w:(n + 1) * blk_w], NEG_INF)
                 for n in range(s.shape[0] // blk_w)]
        return jnp.concatenate(parts, axis=0)

    key = lax.broadcasted_iota(jnp.int32, (blk_w, tq), 0)
    qry = lax.broadcasted_iota(jnp.int32, (blk_w, tq), 1)
    bot_ok = key <= qry - blk_w
    for h in range(n_h):
        s = scores(h, i, 1)
        chosen_lim = jnp.where(chosen(h, 2 * i), blk_w, -1)
        top_ok = key <= jnp.where(qry < blk_w, qry, chosen_lim)
        s = jnp.concatenate([jnp.where(top_ok, s[:blk_w], NEG_INF),
                             jnp.where(bot_ok, s[blk_w:], NEG_INF)], axis=0)
        _softmax_first(s, [vt_ref[h, i]], m_ref.at[h], acc_ref.at[h])

    @pl.when(i % 2 == 1)
    def _():
        for h in range(n_h):
            s = mask_past(h, scores(h, i - 1, 1), 2 * (i - 1))
            _softmax_step(s, [vt_ref[h, i - 1]], m_ref.at[h], acc_ref.at[h])

    def body(c, carry):
        for h in range(n_h):
            s = mask_past(h, scores(h, 2 * c, 2), 4 * c)
            _softmax_step(s, [vt_ref[h, 2 * c], vt_ref[h, 2 * c + 1]], m_ref.at[h], acc_ref.at[h])
        return carry

    lax.fori_loop(0, i // 2, body, 0)
    for h in range(n_h):
        o_ref[:, h * dv:(h + 1) * dv] = _softmax_finish(acc_ref.at[h], dv).astype(o_ref.dtype)


def moba_attention(qt, k, vt, k_mean, batch, seq, out_dtype):
    heads, _, hd, tq = qt.shape
    t = k.shape[1]
    dvx = vt.shape[2]
    dv = dvx - ONES_ROWS
    n_blk = seq // MOBA_BLOCK
    nq = seq // tq
    hb = ATTN_HEADS_PER_STEP
    return pl.pallas_call(
        functools.partial(_moba_attn_kernel, scale=hd ** -0.5, dv=dv),
        grid=(batch, heads // hb, nq),
        in_specs=[pl.BlockSpec((hb, 1, hd, tq), lambda b, h, i: (h, b * nq + i, 0, 0)),
                  pl.BlockSpec((hb, seq, hd), lambda b, h, i: (h, b, 0)),
                  pl.BlockSpec((hb, nq, dvx, tq), lambda b, h, i: (h, b, 0, 0)),
                  pl.BlockSpec((1, hb, n_blk, hd), lambda b, h, i: (b, h, 0, 0))],
        out_specs=pl.BlockSpec((tq, hb * dv), lambda b, h, i: (b * nq + i, h)),
        out_shape=jax.ShapeDtypeStruct((t, heads * dv), out_dtype),
        scratch_shapes=[pltpu.VMEM((hb, 1, tq), F32), pltpu.VMEM((hb, dvx, tq), F32),
                        pltpu.VMEM((hb, n_blk, tq), F32)],
        compiler_params=_params("parallel", "parallel", "arbitrary"),
        name="moba_attention",
    )(qt, k, vt, k_mean)


def _out_proj_kernel(a_ref, b_ref, c_ref, ga_ref, gb_ref, gc_ref, w_ref, x_ref, gate_ref, o_ref,
                     y_ref):
    @pl.when(pl.program_id(1) == 0)
    def _():
        wa, wb = a_ref.shape[1], b_ref.shape[1]
        y_ref[:, 0:wa] = _rms(a_ref[...].astype(F32), ga_ref[...]).astype(y_ref.dtype)
        y_ref[:, wa:wa + wb] = _rms(b_ref[...].astype(F32), gb_ref[...]).astype(y_ref.dtype)
        y_ref[:, wa + wb:] = _rms(c_ref[...].astype(F32), gc_ref[...]).astype(y_ref.dtype)

    o_ref[...] = x_ref[...] + gate_ref[0] * _dot(y_ref[...], w_ref[0])


def out_proj(y_a, y_b, y_c, g, w_layers, layer, x2, gate, seq, tm=512, tn=1024):
    t = y_a.shape[0]
    wa, wb, wc = y_a.shape[1], y_b.shape[1], y_c.shape[1]
    _, kd, n = w_layers.shape
    tm, tn = _tile(seq, tm), _tile(n, tn)
    per_b = seq // tm
    ga, gb, gc = g[:wa].reshape(1, wa), g[wa:wa + wb].reshape(1, wb), g[wa + wb:].reshape(1, wc)
    rows = lambda width: pl.BlockSpec((tm, width), lambda i, j: (i, 0))
    vec = lambda width: pl.BlockSpec((1, width), lambda i, j: (0, 0))
    return pl.pallas_call(
        _out_proj_kernel,
        grid=(t // tm, n // tn),
        in_specs=[rows(wa), rows(wb), rows(wc), vec(wa), vec(wb), vec(wc),
                  pl.BlockSpec((1, kd, tn), lambda i, j: (layer, 0, j)),
                  pl.BlockSpec((tm, tn), lambda i, j: (i, j)),
                  pl.BlockSpec((1, 1, tn), lambda i, j: (i // per_b, 0, j))],
        out_specs=pl.BlockSpec((tm, tn), lambda i, j: (i, j)),
        out_shape=jax.ShapeDtypeStruct((t, n), F32),
        scratch_shapes=[pltpu.VMEM((tm, kd), BF16)],
        compiler_params=_params("parallel", "arbitrary"),
        name="out_proj",
    )(y_a, y_b, y_c, ga, gb, gc, w_layers, x2, gate)


def _expert_weight_copies(w_hbm, layer, expert, col, stage, sem):
    width = stage.shape[1]
    return pltpu.make_async_copy(w_hbm.at[layer, expert, :, pl.ds(col, width)], stage, sem)


def _ffn_up_kernel(bexp_ref, first_ref, rows_ref, next_ref, x_ref, wg_hbm, wu_hbm, a_ref,
                   wg_st, wu_st, wg_bf, wu_bf, sem, *, layer):
    j = pl.program_id(0)
    b = pl.program_id(1)
    n_rows = rows_ref[b]
    half = x_ref.shape[0] // 2
    col = pl.multiple_of(j * wg_bf.shape[1], wg_bf.shape[1])

    def copies(expert):
        return (_expert_weight_copies(wg_hbm, layer, expert, col, wg_st, sem.at[0]),
                _expert_weight_copies(wu_hbm, layer, expert, col, wu_st, sem.at[1]))

    @pl.when(b == 0)
    def _():
        for c in copies(bexp_ref[0]):
            c.start()

    @pl.when((n_rows > 0) & (first_ref[b] == 1))
    def _():
        for c in copies(bexp_ref[b]):
            c.wait()
        wg_bf[...] = wg_st[...].astype(BF16)
        wu_bf[...] = wu_st[...].astype(BF16)

        @pl.when(next_ref[b] >= 0)
        def _():
            for c in copies(next_ref[b]):
                c.start()

    def act(x):
        gate = _dot(x, wg_bf[...])
        up = _dot(x, wu_bf[...])
        return (gate * jax.nn.sigmoid(gate) * up).astype(a_ref.dtype)

    @pl.when(n_rows > half)
    def _():
        a_ref[...] = act(x_ref[...])

    @pl.when((n_rows > 0) & (n_rows <= half))
    def _():
        a_ref[:half] = act(x_ref[:half])
        a_ref[half:] = jnp.zeros((half, a_ref.shape[1]), a_ref.dtype)

    @pl.when(n_rows == 0)
    def _():
        a_ref[...] = jnp.zeros_like(a_ref)


def _pack_bf16_pair(lo, hi):
    lo_bits = lax.bitcast_convert_type(lo.astype(BF16).astype(F32), PACKED) >> 16
    hi_bits = lax.bitcast_convert_type(hi.astype(BF16).astype(F32), PACKED) & jnp.uint32(0xFFFF0000)
    return hi_bits | lo_bits


def _unpack_bf16_pair(packed):
    lo = lax.bitcast_convert_type(packed << 16, F32)
    hi = lax.bitcast_convert_type(packed & jnp.uint32(0xFFFF0000), F32)
    return lo, hi


def _ffn_down_kernel(bexp_ref, first_ref, rows_ref, next_ref, a_ref, wd_hbm, o_ref,
                     wd_st, wd_bf, sem, *, layer):
    j = pl.program_id(0)
    b = pl.program_id(1)
    n_rows = rows_ref[b]
    half = a_ref.shape[0] // 2
    col = pl.multiple_of(j * wd_bf.shape[1], wd_bf.shape[1])

    def copy(expert):
        return _expert_weight_copies(wd_hbm, layer, expert, col, wd_st, sem.at[0])

    @pl.when(b == 0)
    def _():
        copy(bexp_ref[0]).start()

    @pl.when((n_rows > 0) & (first_ref[b] == 1))
    def _():
        copy(bexp_ref[b]).wait()
        wd_bf[...] = wd_st[...].astype(BF16)

        @pl.when(next_ref[b] >= 0)
        def _():
            copy(next_ref[b]).start()

    @pl.when(n_rows > half)
    def _():
        y = _dot(a_ref[...], wd_bf[...])
        o_ref[...] = _pack_bf16_pair(y[:half], y[half:])

    @pl.when((n_rows > 0) & (n_rows <= half))
    def _():
        y = _dot(a_ref[:half], wd_bf[...])
        o_ref[...] = _pack_bf16_pair(y, jnp.zeros_like(y))

    @pl.when(n_rows == 0)
    def _():
        o_ref[...] = jnp.zeros_like(o_ref)


def expert_ffn(xs, block_exp, block_first, block_rows, block_next, w_gate, w_up, w_down, layer):
    n_slots, d = xs.shape
    ff = w_gate.shape[3]
    fh, dh = ff // FF_SPLIT, d // FF_SPLIT
    n_blocks = n_slots // MOE_BLOCK
    hbm = pl.BlockSpec(memory_space=pl.ANY)
    up_spec = pltpu.PrefetchScalarGridSpec(
        num_scalar_prefetch=4,
        grid=(FF_SPLIT, n_blocks),
        in_specs=[pl.BlockSpec((MOE_BLOCK, d), lambda j, b, *_: (b, 0)), hbm, hbm],
        out_specs=pl.BlockSpec((MOE_BLOCK, fh), lambda j, b, *_: (b, j)),
        scratch_shapes=[pltpu.VMEM((d, fh), F32), pltpu.VMEM((d, fh), F32),
                        pltpu.VMEM((d, fh), BF16), pltpu.VMEM((d, fh), BF16),
                        pltpu.SemaphoreType.DMA((2,))],
    )
    act = pl.pallas_call(
        functools.partial(_ffn_up_kernel, layer=layer),
        grid_spec=up_spec,
        out_shape=jax.ShapeDtypeStruct((n_slots, ff), BF16),
        compiler_params=_params("arbitrary", "arbitrary"),
        name="expert_ffn_up",
    )(block_exp, block_first, block_rows, block_next, xs, w_gate, w_up)
    down_spec = pltpu.PrefetchScalarGridSpec(
        num_scalar_prefetch=4,
        grid=(FF_SPLIT, n_blocks),
        in_specs=[pl.BlockSpec((MOE_BLOCK, ff), lambda j, b, *_: (b, 0)), hbm],
        out_specs=pl.BlockSpec((MOE_BLOCK // 2, dh), lambda j, b, *_: (b, j)),
        scratch_shapes=[pltpu.VMEM((ff, dh), F32), pltpu.VMEM((ff, dh), BF16),
                        pltpu.SemaphoreType.DMA((1,))],
    )
    return pl.pallas_call(
        functools.partial(_ffn_down_kernel, layer=layer),
        grid_spec=down_spec,
        out_shape=jax.ShapeDtypeStruct((n_slots // 2, d), PACKED),
        compiler_params=_params("arbitrary", "arbitrary"),
        name="expert_ffn_down",
    )(block_exp, block_first, block_rows, block_next, act, w_down)


def _start_row_copies(idx_ref, base, n_rows, src_ref, dst_ref, sem):
    def issue(r, carry):
        row = idx_ref[base + r]
        pltpu.make_async_copy(src_ref.at[pl.ds(row, 1)], dst_ref.at[pl.ds(r, 1)], sem).start()
        return carry

    lax.fori_loop(0, n_rows, issue, 0, unroll=8)


def _wait_row_copies(n_rows, src_ref, dst_ref, sem):
    pltpu.make_async_copy(src_ref.at[pl.ds(0, n_rows)], dst_ref, sem).wait()


def _gather_cast_kernel(idx_ref, nused_ref, src_ref, o_ref, buf, sem):
    b = pl.program_id(0)
    rows = o_ref.shape[0]
    n_used = nused_ref[0]

    @pl.when((b == 0) & (n_used > 0))
    def _():
        _start_row_copies(idx_ref, 0, rows, src_ref, buf.at[0], sem.at[0])

    @pl.when(b + 1 < n_used)
    def _():
        nxt = (b + 1) % 2
        _start_row_copies(idx_ref, (b + 1) * rows, rows, src_ref, buf.at[nxt], sem.at[nxt])

    @pl.when(b < n_used)
    def _():
        cur = b % 2
        _wait_row_copies(rows, src_ref, buf.at[cur], sem.at[cur])
        o_ref[...] = buf[cur].astype(o_ref.dtype)

    @pl.when(b >= n_used)
    def _():
        o_ref[...] = jnp.zeros_like(o_ref)


def gather_rows_cast(src, idx, n_used_tiles, out_dtype):
    d = src.shape[1]
    m = idx.shape[0]
    rows = GATHER_ROWS
    grid_spec = pltpu.PrefetchScalarGridSpec(
        num_scalar_prefetch=2,
        grid=(m // rows,),
        in_specs=[pl.BlockSpec(memory_space=pl.ANY)],
        out_specs=pl.BlockSpec((rows, d), lambda b, ix, nu: (b, 0)),
        scratch_shapes=[pltpu.VMEM((2, rows, d), src.dtype), pltpu.SemaphoreType.DMA((2,))],
    )
    return pl.pallas_call(
        _gather_cast_kernel,
        grid_spec=grid_spec,
        out_shape=jax.ShapeDtypeStruct((m, d), out_dtype),
        compiler_params=_params("arbitrary"),
        name="gather_rows_cast",
    )(idx, n_used_tiles, src)


def _combine_kernel(pos_ref, x_ref, gate_ref, w_ref, half_ref, g_ref, sc_ref, sh_ref, ys_ref,
                    *rest, n_tok, final):
    if final:
        o_ref, ybuf, sem = rest
    else:
        o_ref, h_ref, ybuf, sem = rest
    i = pl.program_id(0)
    n_tiles = pl.num_programs(0)
    tm = x_ref.shape[0]

    def start_tile(tile, slot):
        for kk in range(TOP_K):
            _start_row_copies(pos_ref, kk * n_tok + tile * tm, tm, ys_ref, ybuf.at[slot, kk],
                              sem.at[slot])

    @pl.when(i == 0)
    def _():
        start_tile(0, 0)

    @pl.when(i + 1 < n_tiles)
    def _():
        start_tile(i + 1, (i + 1) % 2)

    cur = i % 2
    for kk in range(TOP_K):
        _wait_row_copies(tm, ys_ref, ybuf.at[cur, kk], sem.at[cur])
    w = w_ref[...]
    upper = half_ref[...] > 0.5
    acc = None
    for kk in range(TOP_K):
        lo, hi = _unpack_bf16_pair(ybuf[cur, kk])
        term = w[:, kk:kk + 1] * jnp.where(upper[:, kk:kk + 1], hi, lo)
        acc = term if acc is None else acc + term
    x_new = x_ref[...] + gate_ref[0] * acc
    if final:
        o_ref[...] = _rms(x_new, g_ref[...])
    else:
        o_ref[...] = x_new
        h_ref[...] = (_rms(x_new, g_ref[...]) * sc_ref[0] + sh_ref[0]).astype(h_ref.dtype)


def moe_combine(x2, gate, ys, pos, w_tok, half_tok, seq, norm_g, sc1p=None, sh=None):
    t, d = x2.shape
    tm = _tile(seq, GATHER_ROWS)
    per_b = seq // tm
    final = sc1p is None
    if final:
        sc1p = sh = jnp.zeros((x2.shape[0] // seq, 1, d), F32)
    row = pl.BlockSpec((tm, d), lambda i, ps: (i, 0))
    mod = pl.BlockSpec((1, 1, d), lambda i, ps: (i // per_b, 0, 0))
    tok = pl.BlockSpec((tm, TOP_K), lambda i, ps: (i, 0))
    grid_spec = pltpu.PrefetchScalarGridSpec(
        num_scalar_prefetch=1,
        grid=(t // tm,),
        in_specs=[row, mod, tok, tok, pl.BlockSpec((1, d), lambda i, ps: (0, 0)), mod, mod,
                  pl.BlockSpec(memory_space=pl.ANY)],
        out_specs=row if final else [row, row],
        scratch_shapes=[pltpu.VMEM((2, TOP_K, tm, d), ys.dtype), pltpu.SemaphoreType.DMA((2,))],
    )
    out_shape = jax.ShapeDtypeStruct((t, d), F32)
    return pl.pallas_call(
        functools.partial(_combine_kernel, n_tok=t, final=final),
        grid_spec=grid_spec,
        out_shape=out_shape if final else [out_shape, jax.ShapeDtypeStruct((t, d), BF16)],
        compiler_params=_params("arbitrary"),
        name="moe_combine",
    )(pos, x2, gate, w_tok, half_tok, norm_g.reshape(1, d), sc1p, sh, ys)


def _group_by_expert(e_idx):
    n_tok = e_idx.shape[1]
    n_asg = n_tok * TOP_K
    flat_e = e_idx.reshape(n_asg)
    onehot = (flat_e[:, None] == jnp.arange(N_EXPERTS)[None, :]).astype(jnp.int32)
    running = jnp.cumsum(onehot, axis=0)
    rank = jnp.sum(running * onehot, axis=1) - 1
    sizes = running[-1]
    padded = (sizes + MOE_BLOCK - 1) // MOE_BLOCK * MOE_BLOCK
    pad_end = jnp.cumsum(padded)
    pad_start = pad_end - padded
    dest = (jnp.sum(pad_start[None, :] * onehot, axis=1) + rank).astype(jnp.int32)
    n_blocks = -(-n_asg // MOE_BLOCK) + N_EXPERTS
    n_slots = n_blocks * MOE_BLOCK
    slot_tok = (jnp.arange(n_slots, dtype=jnp.int32) % n_tok).at[dest].set(
        jnp.arange(n_asg, dtype=jnp.int32) % n_tok, mode="promise_in_bounds")
    block_start = jnp.arange(n_blocks, dtype=jnp.int32) * MOE_BLOCK
    block_exp = jnp.minimum(jnp.sum(block_start[:, None] >= pad_end[None, :], axis=1),
                            N_EXPERTS - 1).astype(jnp.int32)
    n_used = (pad_end[-1] // MOE_BLOCK).astype(jnp.int32)
    block_exp = jnp.where(jnp.arange(n_blocks) < n_used, block_exp, block_exp[n_used - 1])
    prev_exp = jnp.concatenate([jnp.full((1,), -1, jnp.int32), block_exp[:-1]])
    block_first = (block_exp != prev_exp).astype(jnp.int32)
    block_rows = jnp.clip(pad_start[block_exp] + sizes[block_exp] - block_start, 0, MOE_BLOCK)
    block_rows = jnp.where(jnp.arange(n_blocks) < n_used, block_rows, 0).astype(jnp.int32)
    blk = jnp.arange(n_blocks, dtype=jnp.int32)
    starts = jnp.where((block_first == 1) & (block_rows > 0), blk, n_blocks)
    later = lax.cummin(jnp.concatenate([starts[1:], jnp.full((1,), n_blocks, jnp.int32)]),
                       axis=0, reverse=True)
    block_next = jnp.where(later < n_blocks, block_exp[jnp.minimum(later, n_blocks - 1)],
                           -1).astype(jnp.int32)
    return (dest.reshape(TOP_K, n_tok), slot_tok, block_exp, block_first, block_rows, block_next,
            n_used.reshape(1))


def _rot_cols(w):
    half = w.shape[-1] // 2
    return jnp.concatenate([-w[..., half:], w[..., :half]], axis=-1)


def _split_in_weights_kernel(w_ref, a_ref, b_ref, c_ref):
    w = w_ref[0]
    kpe_at = Q_LORA + KV_LORA
    a_end = kpe_at + MLA_ROPE
    b_end = a_end + 3 * CONV_CH
    win = w[:, kpe_at:kpe_at + LANE]
    lane = lax.broadcasted_iota(jnp.int32, win.shape, 1)
    quarter = MLA_ROPE // 2
    rot = jnp.where(lane < MLA_ROPE + quarter, -pltpu.roll(win, quarter, 1),
                    pltpu.roll(win, LANE - MLA_ROPE + quarter, 1))
    tail = jnp.where(lane < MLA_ROPE, win, rot)
    a_ref[0] = jnp.concatenate([w[:, :kpe_at], tail], axis=1).astype(a_ref.dtype)
    b_ref[0] = w[:, a_end:b_end].astype(b_ref.dtype)
    c_ref[0] = w[:, b_end:].astype(c_ref.dtype)


def split_in_weights(w_in):
    layers, d, n_in = w_in.shape
    a_w = Q_LORA + KV_LORA + LANE
    b_w = 3 * CONV_CH
    c_w = n_in - (Q_LORA + KV_LORA + MLA_ROPE) - b_w
    tr = _tile(d, 256)
    spec = lambda width: pl.BlockSpec((1, tr, width), lambda l, i: (l, i, 0))
    return pl.pallas_call(
        _split_in_weights_kernel,
        grid=(layers, d // tr),
        in_specs=[spec(n_in)],
        out_specs=[spec(a_w), spec(b_w), spec(c_w)],
        out_shape=[jax.ShapeDtypeStruct((layers, d, width), BF16) for width in (a_w, b_w, c_w)],
        compiler_params=_params("parallel", "parallel"),
        name="split_in_weights",
    )(w_in)


def _prep_weights(w_in, w_uq, w_ukv, w_out):
    layers = w_in.shape[0]
    w_a, w_b, w_c = split_in_weights(w_in)
    wq = w_uq.reshape(layers, Q_LORA, MLA_HEADS, MLA_NOPE + MLA_ROPE)
    wq_rope = wq[..., MLA_NOPE:]
    wq = jnp.concatenate([wq, _rot_cols(wq_rope)], axis=-1).transpose(0, 2, 1, 3).astype(BF16)
    return dict(w_a=w_a, w_b=w_b, w_c=w_c, w_q=wq, w_kv=w_ukv.astype(BF16),
                w_out=w_out.astype(BF16))


def _rope_tables(seq, dim):
    inv = 1.0 / (ROPE_THETA ** (jnp.arange(0, dim, 2, dtype=F32) / dim))
    ang = jnp.arange(seq, dtype=F32)[:, None] * inv[None, :]
    return jnp.cos(ang), jnp.sin(ang)


def kernel(x, c, w_mod, mod_table, mix_norm_g, w_in, q_norm_g, kv_norm_g, w_uq, w_ukv, conv_w,
           group_norm_g, w_out, ffn_norm_g, w_router, router_bias, w_gate, w_up, w_down,
           final_norm_g):
    batch, seq, d = x.shape
    depth = w_in.shape[0]
    t = batch * seq
    x2 = x.reshape(t, d)

    cos_a, sin_a = _rope_tables(seq, MLA_ROPE)
    zeros_a = jnp.zeros_like(cos_a)
    cos_p = jnp.concatenate([cos_a, cos_a, zeros_a, zeros_a], axis=1)
    sin_p = jnp.concatenate([sin_a, sin_a, zeros_a, zeros_a], axis=1)
    cos_b, sin_b = _rope_tables(seq, MOBA_HD)
    cos_f = jnp.concatenate([cos_b, cos_b], axis=1)
    sin_s = jnp.concatenate([-sin_b, sin_b], axis=1)

    c_pad = jnp.zeros((SUBLANE, d), F32).at[:batch].set(c)
    mod_shared = mod_matmul(c_pad, w_mod)[:batch].reshape(batch, N_MOD, d)

    wr_t = w_router.T
    wr_hi = wr_t.astype(BF16)
    wr_lo = (wr_t - wr_hi.astype(F32)).astype(BF16)
    router = (wr_hi, wr_lo, router_bias.astype(F32).reshape(N_EXPERTS, 1))
    mla_scale = (MLA_NOPE + MLA_ROPE) ** -0.5

    mods = []
    for l in range(depth):
        mod = mod_shared + mod_table[l][None]
        mods.append([mod[:, i][:, None, :] for i in range(N_MOD)])

    p = _prep_weights(w_in, w_uq, w_ukv, w_out)
    h = norm_mod(x2, mix_norm_g[0], 1.0 + mods[0][1], mods[0][0], seq)
    for l in range(depth):
        sh1, sc1, g1, sh2, sc2, g2 = mods[l]

        proj_a = matmul(h, p["w_a"], l, F32, tm=512)
        bch = matmul(h, p["w_b"], l, F32)
        qkv = matmul(h, p["w_c"], l, F32)

        q_a = mla_q_up(proj_a, q_norm_g[l], p["w_q"], l, cos_p, sin_p, seq, mla_scale)
        k_a, v_a = mla_kv_up(proj_a, kv_norm_g[l], p["w_kv"], l, cos_p, sin_p, seq)
        y_a = mla_attention(q_a, k_a, v_a, batch, seq, BF16)

        y_b = conv_mixer(bch, conv_w[l], seq, F32)

        q_c, k_c, v_c, k_mean = moba_prep(qkv, cos_f, sin_s, seq)
        n_blk = seq // MOBA_BLOCK
        k_mean = k_mean.reshape(batch, n_blk, MOBA_HEADS, MOBA_HD).transpose(0, 2, 1, 3)
        y_c = moba_attention(q_c, k_c, v_c, k_mean, batch, seq, BF16)

        x2 = out_proj(y_a, y_b, y_c, group_norm_g[l], p["w_out"], l, x2, g1, seq)

        h2, e_idx, gates = norm_mod(x2, ffn_norm_g[l], 1.0 + sc2, sh2, seq, out_dtype=F32,
                                    router=router)
        pos, slot_tok, block_exp, block_first, block_rows, block_next, n_used = _group_by_expert(e_idx)
        xs = gather_rows_cast(h2, slot_tok, n_used * (MOE_BLOCK // GATHER_ROWS), BF16)
        ys = expert_ffn(xs, block_exp, block_first, block_rows, block_next, w_gate, w_up, w_down,
                        l)
        sub = pos % MOE_BLOCK
        ys_row = ((pos // MOE_BLOCK) * (MOE_BLOCK // 2) + sub % (MOE_BLOCK // 2)).reshape(-1)
        ys_half = (sub // (MOE_BLOCK // 2)).astype(F32).T
        if l + 1 < depth:
            nxt = mods[l + 1]
            x2, h = moe_combine(x2, g2, ys, ys_row, gates.T, ys_half, seq, mix_norm_g[l + 1],
                                1.0 + nxt[1], nxt[0])
        else:
            out = moe_combine(x2, g2, ys, ys_row, gates.T, ys_half, seq, final_norm_g)

    return out.reshape(batch, seq, d)
```

```python
import functools

import jax
import jax.numpy as jnp
from jax import lax
from jax.experimental import pallas as pl
from jax.experimental.pallas import tpu as pltpu

MLA_HEADS = 16
MLA_NOPE = 128
MLA_ROPE = 64
MLA_V = 128
Q_LORA = 1024
KV_LORA = 512
MLA_DK_PAD = 256
CONV_CH = 1024
CONV_W = 3
MOBA_HEADS = 8
MOBA_HD = 128
MOBA_BLOCK = 256
MOBA_TOPK = 3
ROPE_THETA = 10000.0
EPS = 1e-6
N_MOD = 6
N_EXPERTS = 16
N_GROUPS = 4
TOP_K = 2
MOE_BLOCK = 512
GATHER_ROWS = 256
FF_SPLIT = 2
UP_HEADS_PER_STEP = 4
ATTN_HEADS_PER_STEP = 4
ATTN_TILE = 512
ONES_ROWS = 16

V7X_VMEM_LIMIT_BYTES = 56 * 1024 * 1024
LANE = 128
SUBLANE = 8

F32 = jnp.float32
BF16 = jnp.bfloat16
PACKED = jnp.uint32
NEG_INF = float("-inf")


def _params(*sem):
    return pltpu.CompilerParams(dimension_semantics=sem, vmem_limit_bytes=V7X_VMEM_LIMIT_BYTES)


def _dot(a, b, precision=None):
    return jnp.dot(a, b, preferred_element_type=F32, precision=precision)


def _dot_nt(a, b):
    return lax.dot_general(a, b, (((1,), (1,)), ((), ())), preferred_element_type=F32)


def _tile(dim, want):
    return want if dim % want == 0 else dim


def _mod_kernel(c_ref, w0_ref, w1_ref, o_ref):
    @pl.when(pl.program_id(0) == 0)
    def _():
        o_ref[...] = jnp.zeros_like(o_ref)

    c = c_ref[...]
    a = (c * jax.nn.sigmoid(c)).astype(BF16)
    half = w0_ref.shape[1]
    o_ref[:, :half] += _dot(a, w0_ref[...].astype(BF16))
    o_ref[:, half:] += _dot(a, w1_ref[...].astype(BF16))


def mod_matmul(c_pad, w_mod):
    rows, d = c_pad.shape
    n = w_mod.shape[1]
    tk = _tile(d, LANE)
    return pl.pallas_call(
        _mod_kernel,
        grid=(d // tk,),
        in_specs=[pl.BlockSpec((rows, tk), lambda k: (0, k)),
                  pl.BlockSpec((tk, n // 2), lambda k: (k, 0)),
                  pl.BlockSpec((tk, n // 2), lambda k: (k, 1))],
        out_specs=pl.BlockSpec((rows, n), lambda k: (0, 0)),
        out_shape=jax.ShapeDtypeStruct((rows, n), F32),
        compiler_params=_params("arbitrary"),
        name="mod_matmul",
    )(c_pad, w_mod, w_mod)


def _rms(x, g):
    return x * lax.rsqrt(jnp.mean(x * x, axis=-1, keepdims=True) + EPS) * g


def _norm_mod_kernel(x_ref, g_ref, sc_ref, sh_ref, o_ref):
    y = _rms(x_ref[...], g_ref[...])
    o_ref[...] = (y * sc_ref[0] + sh_ref[0]).astype(o_ref.dtype)


def _top2_sum(a, b, c, d):
    hi1, lo1 = jnp.maximum(a, b), jnp.minimum(a, b)
    hi2, lo2 = jnp.maximum(c, d), jnp.minimum(c, d)
    return jnp.maximum(hi1, hi2) + jnp.maximum(jnp.minimum(hi1, hi2), jnp.maximum(lo1, lo2))


def _norm_router_kernel(x_ref, g_ref, sc_ref, sh_ref, whi_ref, wlo_ref, bias_ref,
                        o_ref, e_ref, gate_ref):
    h = _rms(x_ref[...], g_ref[...]) * sc_ref[0] + sh_ref[0]
    hi = h.astype(BF16)
    o_ref[...] = h.astype(o_ref.dtype)
    lo = (h - hi.astype(F32)).astype(BF16)
    whi, wlo = whi_ref[...], wlo_ref[...]
    logits = _dot_nt(whi, hi) + _dot_nt(wlo, hi) + _dot_nt(whi, lo)
    scores = jax.nn.sigmoid(logits)
    biased = scores + bias_ref[...]
    per_grp = N_EXPERTS // N_GROUPS
    b_rows = [biased[e:e + 1, :] for e in range(N_EXPERTS)]
    s_rows = [scores[e:e + 1, :] for e in range(N_EXPERTS)]
    grp_scores = [_top2_sum(*b_rows[g * per_grp:(g + 1) * per_grp]) for g in range(N_GROUPS)]
    best, grp = grp_scores[0], jnp.zeros(grp_scores[0].shape, jnp.int32)
    for g in range(1, N_GROUPS):
        better = grp_scores[g] > best
        grp = jnp.where(better, g, grp)
        best = jnp.where(better, grp_scores[g], best)
    b4, s4 = b_rows[:per_grp], s_rows[:per_grp]
    for g in range(1, N_GROUPS):
        in_g = grp == g
        b4 = [jnp.where(in_g, b_rows[g * per_grp + j], b4[j]) for j in range(per_grp)]
        s4 = [jnp.where(in_g, s_rows[g * per_grp + j], s4[j]) for j in range(per_grp)]
    v1, i1 = b4[0], jnp.zeros(grp.shape, jnp.int32)
    for j in range(1, per_grp):
        better = b4[j] > v1
        i1 = jnp.where(better, j, i1)
        v1 = jnp.where(better, b4[j], v1)
    v2, i2 = jnp.full(v1.shape, NEG_INF, F32), jnp.zeros(grp.shape, jnp.int32)
    for j in range(per_grp):
        better = (i1 != j) & (b4[j] > v2)
        i2 = jnp.where(better, j, i2)
        v2 = jnp.where(better, b4[j], v2)
    g1, g2 = s4[0], s4[0]
    for j in range(1, per_grp):
        g1 = jnp.where(i1 == j, s4[j], g1)
        g2 = jnp.where(i2 == j, s4[j], g2)
    total = g1 + g2
    e_ref[...] = jnp.concatenate([grp * per_grp + i1, grp * per_grp + i2], axis=0)
    gate_ref[...] = jnp.concatenate([g1 / total, g2 / total], axis=0)


def norm_mod(x2, g, sc1p, sh, seq, out_dtype=BF16, router=None):
    t, d = x2.shape
    tm = _tile(seq, 256)
    per_b = seq // tm
    row = pl.BlockSpec((tm, d), lambda i: (i, 0))
    vec = pl.BlockSpec((1, d), lambda i: (0, 0))
    mod = pl.BlockSpec((1, 1, d), lambda i: (i // per_b, 0, 0))
    g2 = g.reshape(1, d)
    if router is None:
        return pl.pallas_call(
            _norm_mod_kernel, grid=(t // tm,), in_specs=[row, vec, mod, mod], out_specs=row,
            out_shape=jax.ShapeDtypeStruct((t, d), out_dtype),
            compiler_params=_params("arbitrary"), name="norm_mod")(x2, g2, sc1p, sh)
    whi_t, wlo_t, bias = router
    ne = whi_t.shape[0]
    wspec = pl.BlockSpec((ne, d), lambda i: (0, 0))
    kspec = pl.BlockSpec((TOP_K, tm), lambda i: (0, i))
    return pl.pallas_call(
        _norm_router_kernel, grid=(t // tm,),
        in_specs=[row, vec, mod, mod, wspec, wspec, pl.BlockSpec((ne, 1), lambda i: (0, 0))],
        out_specs=[row, kspec, kspec],
        out_shape=[jax.ShapeDtypeStruct((t, d), out_dtype),
                   jax.ShapeDtypeStruct((TOP_K, t), jnp.int32),
                   jax.ShapeDtypeStruct((TOP_K, t), F32)],
        compiler_params=_params("arbitrary"), name="norm_router")(x2, g2, sc1p, sh, whi_t, wlo_t, bias)


def _mm_kernel(a_ref, wt_ref, o_ref):
    o_ref[...] = _dot_nt(a_ref[...], wt_ref[0]).astype(o_ref.dtype)


def matmul(a, wt_layers, layer, out_dtype, tm=1024, tn=1024):
    m, kd = a.shape
    n = wt_layers.shape[1]
    tm, tn = _tile(m, tm), _tile(n, tn)
    return pl.pallas_call(
        _mm_kernel,
        grid=(m // tm, n // tn),
        in_specs=[pl.BlockSpec((tm, kd), lambda i, j: (i, 0)),
                  pl.BlockSpec((1, tn, kd), lambda i, j: (layer, j, 0))],
        out_specs=pl.BlockSpec((tm, tn), lambda i, j: (i, j)),
        out_shape=jax.ShapeDtypeStruct((m, n), out_dtype),
        compiler_params=_params("parallel", "parallel"),
        name="matmul",
    )(a, wt_layers)


def _rope_pair(r2, cos_ref, sin_ref):
    return r2 * cos_ref[...] + pltpu.roll(r2, MLA_ROPE, 1) * sin_ref[...]


def _with_ones_rows(vt):
    return jnp.concatenate([vt, jnp.ones((ONES_ROWS, vt.shape[1]), vt.dtype)], axis=0)


def _qup_kernel(cq_ref, g_ref, w_ref, cos_ref, sin_ref, o_ref, an_ref, *, scale):
    @pl.when(pl.program_id(1) == 0)
    def _():
        an_ref[...] = _rms(cq_ref[...], g_ref[...]).astype(BF16)

    an = an_ref[...]
    for hh in range(w_ref.shape[1]):
        res = _dot(an, w_ref[0, hh])
        roped = _rope_pair(res[:, MLA_NOPE:], cos_ref, sin_ref)
        q = jnp.concatenate([res[:, :MLA_NOPE], roped], axis=1) * scale
        o_ref[hh, 0] = q.T.astype(o_ref.dtype)


def mla_q_up(proj_a, g, w_q, layer, cos_p, sin_p, seq, scale):
    t = proj_a.shape[0]
    heads = w_q.shape[1]
    hb = UP_HEADS_PER_STEP
    tm = _tile(seq, ATTN_TILE)
    per_b = seq // tm
    return pl.pallas_call(
        functools.partial(_qup_kernel, scale=scale),
        grid=(t // tm, heads // hb),
        in_specs=[pl.BlockSpec((tm, Q_LORA), lambda i, h: (i, 0)),
                  pl.BlockSpec((1, Q_LORA), lambda i, h: (0, 0)),
                  pl.BlockSpec((1, hb, Q_LORA, MLA_DK_PAD), lambda i, h: (layer, h, 0, 0)),
                  pl.BlockSpec((tm, LANE), lambda i, h: (i % per_b, 0)),
                  pl.BlockSpec((tm, LANE), lambda i, h: (i % per_b, 0))],
        out_specs=pl.BlockSpec((hb, 1, MLA_DK_PAD, tm), lambda i, h: (h, i, 0, 0)),
        out_shape=jax.ShapeDtypeStruct((heads, t // tm, MLA_DK_PAD, tm), BF16),
        scratch_shapes=[pltpu.VMEM((tm, Q_LORA), BF16)],
        compiler_params=_params("parallel", "arbitrary"),
        name="mla_q_up",
    )(proj_a, g.reshape(1, Q_LORA), w_q, cos_p, sin_p)


def _kvup_kernel(ckv_ref, g_ref, w_ref, kpe_ref, cos_ref, sin_ref, k_ref, vt_ref, an_ref):
    @pl.when(pl.program_id(1) == 0)
    def _():
        an_ref[...] = _rms(ckv_ref[...], g_ref[...]).astype(BF16)

    an = an_ref[...]
    k_rot = _rope_pair(kpe_ref[...], cos_ref, sin_ref)
    hw = MLA_NOPE + MLA_V
    for hh in range(k_ref.shape[0]):
        res = _dot(an, w_ref[0, :, hh * hw:(hh + 1) * hw])
        k_ref[hh] = jnp.concatenate([res[:, :MLA_NOPE], k_rot], axis=1).astype(k_ref.dtype)
        vt_ref[hh, 0] = _with_ones_rows(res[:, MLA_NOPE:].T).astype(vt_ref.dtype)


def mla_kv_up(proj_a, g, w_kv, layer, cos_p, sin_p, seq):
    t = proj_a.shape[0]
    hw = MLA_NOPE + MLA_V
    heads = w_kv.shape[2] // hw
    hb = UP_HEADS_PER_STEP
    tm = _tile(seq, ATTN_TILE)
    per_b = seq // tm
    return pl.pallas_call(
        _kvup_kernel,
        grid=(t // tm, heads // hb),
        in_specs=[pl.BlockSpec((tm, KV_LORA), lambda i, h: (i, Q_LORA // KV_LORA)),
                  pl.BlockSpec((1, KV_LORA), lambda i, h: (0, 0)),
                  pl.BlockSpec((1, KV_LORA, hb * hw), lambda i, h: (layer, 0, h)),
                  pl.BlockSpec((tm, LANE), lambda i, h: (i, (Q_LORA + KV_LORA) // LANE)),
                  pl.BlockSpec((tm, LANE), lambda i, h: (i % per_b, 0)),
                  pl.BlockSpec((tm, LANE), lambda i, h: (i % per_b, 0))],
        out_specs=[pl.BlockSpec((hb, tm, MLA_DK_PAD), lambda i, h: (h, i, 0)),
                   pl.BlockSpec((hb, 1, MLA_V + ONES_ROWS, tm), lambda i, h: (h, i, 0, 0))],
        out_shape=[jax.ShapeDtypeStruct((heads, t, MLA_DK_PAD), BF16),
                   jax.ShapeDtypeStruct((heads, t // tm, MLA_V + ONES_ROWS, tm), BF16)],
        scratch_shapes=[pltpu.VMEM((tm, KV_LORA), BF16)],
        compiler_params=_params("parallel", "arbitrary"),
        name="mla_kv_up",
    )(proj_a, g.reshape(1, KV_LORA), w_kv, proj_a, cos_p, sin_p)


def _pv(vts, p):
    rows = p.shape[0] // len(vts)
    out = _dot(vts[0], p[:rows].astype(vts[0].dtype))
    for n in range(1, len(vts)):
        out = out + _dot(vts[n], p[n * rows:(n + 1) * rows].astype(vts[n].dtype))
    return out


def _softmax_first(s, vts, m_ref, acc_ref):
    m = jnp.max(s, axis=0, keepdims=True)
    m_ref[...] = m
    acc_ref[...] = _pv(vts, jnp.exp(s - m))


def _softmax_step(s, vts, m_ref, acc_ref):
    m_prev = m_ref[...]
    m_new = jnp.maximum(m_prev, jnp.max(s, axis=0, keepdims=True))
    alpha = jnp.exp(m_prev - m_new)
    acc_ref[...] = alpha * acc_ref[...] + _pv(vts, jnp.exp(s - m_new))
    m_ref[...] = m_new


def _softmax_finish(acc_ref, dv):
    acc = acc_ref[...]
    return (acc[:dv, :] / acc[dv:dv + 1, :]).T


def _causal_mask_t(s):
    key = lax.broadcasted_iota(jnp.int32, s.shape, 0)
    qry = lax.broadcasted_iota(jnp.int32, s.shape, 1)
    return jnp.where(key <= qry, s, NEG_INF)


def _mla_attn_kernel(qt_ref, k_ref, vt_ref, o_ref, m_ref, acc_ref, *, tq, dv):
    i = pl.program_id(2)
    n_h = qt_ref.shape[0]
    qts = [qt_ref[h, 0] for h in range(n_h)]

    def scores(h, j, n):
        off = pl.multiple_of(j * tq, tq)
        return _dot(k_ref[h, pl.ds(off, n * tq), :], qts[h])

    for h in range(n_h):
        _softmax_first(_causal_mask_t(scores(h, i, 1)), [vt_ref[h, i]], m_ref.at[h], acc_ref.at[h])

    @pl.when(i % 2 == 1)
    def _():
        for h in range(n_h):
            _softmax_step(scores(h, i - 1, 1), [vt_ref[h, i - 1]], m_ref.at[h], acc_ref.at[h])

    def body(c, carry):
        for h in range(n_h):
            _softmax_step(scores(h, 2 * c, 2), [vt_ref[h, 2 * c], vt_ref[h, 2 * c + 1]],
                          m_ref.at[h], acc_ref.at[h])
        return carry

    lax.fori_loop(0, i // 2, body, 0)
    for h in range(n_h):
        o_ref[:, h * dv:(h + 1) * dv] = _softmax_finish(acc_ref.at[h], dv).astype(o_ref.dtype)


def mla_attention(qt, k, vt, batch, seq, out_dtype):
    heads, _, dk, tq = qt.shape
    t = k.shape[1]
    dvx = vt.shape[2]
    dv = dvx - ONES_ROWS
    nq = seq // tq
    hb = ATTN_HEADS_PER_STEP
    return pl.pallas_call(
        functools.partial(_mla_attn_kernel, tq=tq, dv=dv),
        grid=(batch, heads // hb, nq),
        in_specs=[pl.BlockSpec((hb, 1, dk, tq), lambda b, h, i: (h, b * nq + i, 0, 0)),
                  pl.BlockSpec((hb, seq, dk), lambda b, h, i: (h, b, 0)),
                  pl.BlockSpec((hb, nq, dvx, tq), lambda b, h, i: (h, b, 0, 0))],
        out_specs=pl.BlockSpec((tq, hb * dv), lambda b, h, i: (b * nq + i, h)),
        out_shape=jax.ShapeDtypeStruct((t, heads * dv), out_dtype),
        scratch_shapes=[pltpu.VMEM((hb, 1, tq), F32), pltpu.VMEM((hb, dvx, tq), F32)],
        compiler_params=_params("parallel", "parallel", "arbitrary"),
        name="mla_attention",
    )(qt, k, vt)


def _conv_kernel(b_ref, c_ref, h_ref, w_ref, o_ref, carry_ref, *, per_b):
    i = pl.program_id(0)

    @pl.when(i % per_b == 0)
    def _():
        carry_ref[...] = jnp.zeros_like(carry_ref)

    tm = o_ref.shape[0]
    w0, w1, w2 = w_ref[0:1, :], w_ref[1:2, :], w_ref[2:3, :]
    u = c_ref[...] * h_ref[...]
    y = w0 * pltpu.roll(u, 2, 0) + w1 * pltpu.roll(u, 1, 0) + w2 * u
    o_ref[...] = (b_ref[...] * y).astype(o_ref.dtype)
    u8 = u[0:SUBLANE, :]
    tail = carry_ref[...]
    r8 = lax.broadcasted_iota(jnp.int32, u8.shape, 0)
    p1 = jnp.where(r8 < 1, pltpu.roll(tail, 1, 0), pltpu.roll(u8, 1, 0))
    p2 = jnp.where(r8 < 2, pltpu.roll(tail, 2, 0), pltpu.roll(u8, 2, 0))
    y8 = w0 * p2 + w1 * p1 + w2 * u8
    o_ref[0:SUBLANE, :] = (b_ref[0:SUBLANE, :] * y8).astype(o_ref.dtype)
    carry_ref[...] = u[tm - SUBLANE:tm, :]


def conv_mixer(bch, conv_w, seq, out_dtype):
    t = bch.shape[0]
    ch = conv_w.shape[1]
    tm = _tile(seq, 256)
    per_b = seq // tm
    return pl.pallas_call(
        functools.partial(_conv_kernel, per_b=per_b),
        grid=(t // tm,),
        in_specs=[pl.BlockSpec((tm, ch), lambda i: (i, 0)),
                  pl.BlockSpec((tm, ch), lambda i: (i, 1)),
                  pl.BlockSpec((tm, ch), lambda i: (i, 2)),
                  pl.BlockSpec((CONV_W, ch), lambda i: (0, 0))],
        out_specs=pl.BlockSpec((tm, ch), lambda i: (i, 0)),
        out_shape=jax.ShapeDtypeStruct((t, ch), out_dtype),
        scratch_shapes=[pltpu.VMEM((SUBLANE, ch), F32)],
        compiler_params=_params("arbitrary"),
        name="conv_mixer",
    )(bch, bch, bch, conv_w)


def _moba_prep_kernel(q_ref, k_ref, v_ref, cos_ref, sin_ref, qo_ref, ko_ref, vo_ref, km_ref):
    heads = ko_ref.shape[0]
    cos, sin = cos_ref[...], sin_ref[...]
    n_sub = km_ref.shape[0]
    means = [[] for _ in range(n_sub)]
    for h in range(heads):
        sl = slice(h * MOBA_HD, (h + 1) * MOBA_HD)
        qh = q_ref[:, sl]
        kh = k_ref[:, sl]
        qr = qh * cos + pltpu.roll(qh, MOBA_HD // 2, 1) * sin
        qo_ref[h, 0] = qr.T.astype(qo_ref.dtype)
        kr = kh * cos + pltpu.roll(kh, MOBA_HD // 2, 1) * sin
        ko_ref[h] = kr.astype(ko_ref.dtype)
        vo_ref[h, 0] = _with_ones_rows(v_ref[:, sl].T).astype(vo_ref.dtype)
        for s in range(n_sub):
            means[s].append(jnp.mean(kr[s * MOBA_BLOCK:(s + 1) * MOBA_BLOCK], axis=0, keepdims=True))
    for s in range(n_sub):
        km_ref[s] = jnp.concatenate(means[s], axis=0)


def moba_prep(qkv, cos_f, sin_s, seq):
    t = qkv.shape[0]
    width = qkv.shape[1] // 3
    heads = width // MOBA_HD
    tm = 2 * MOBA_BLOCK
    assert seq % tm == 0
    per_b = seq // tm
    nt = t // tm
    return pl.pallas_call(
        _moba_prep_kernel,
        grid=(nt,),
        in_specs=[pl.BlockSpec((tm, width), lambda i: (i, 0)),
                  pl.BlockSpec((tm, width), lambda i: (i, 1)),
                  pl.BlockSpec((tm, width), lambda i: (i, 2)),
                  pl.BlockSpec((tm, MOBA_HD), lambda i: (i % per_b, 0)),
                  pl.BlockSpec((tm, MOBA_HD), lambda i: (i % per_b, 0))],
        out_specs=[pl.BlockSpec((heads, 1, MOBA_HD, tm), lambda i: (0, i, 0, 0)),
                   pl.BlockSpec((heads, tm, MOBA_HD), lambda i: (0, i, 0)),
                   pl.BlockSpec((heads, 1, MOBA_HD + ONES_ROWS, tm), lambda i: (0, i, 0, 0)),
                   pl.BlockSpec((2, heads, MOBA_HD), lambda i: (i, 0, 0))],
        out_shape=[jax.ShapeDtypeStruct((heads, nt, MOBA_HD, tm), BF16),
                   jax.ShapeDtypeStruct((heads, t, MOBA_HD), BF16),
                   jax.ShapeDtypeStruct((heads, nt, MOBA_HD + ONES_ROWS, tm), BF16),
                   jax.ShapeDtypeStruct((2 * nt, heads, MOBA_HD), F32)],
        compiler_params=_params("arbitrary"),
        name="moba_prep",
    )(qkv, qkv, qkv, cos_f, sin_s)


def _moba_attn_kernel(qt_ref, k_ref, vt_ref, km_ref, o_ref, m_ref, acc_ref, sel_ref, *, scale, dv):
    i = pl.program_id(2)
    n_h = qt_ref.shape[0]
    n_blk = km_ref.shape[2]
    tq = qt_ref.shape[3]
    blk_w = MOBA_BLOCK
    blk = lax.broadcasted_iota(jnp.int32, (n_blk, tq), 0)
    own = 2 * i + (lax.broadcasted_iota(jnp.int32, (n_blk, tq), 1) >= blk_w).astype(jnp.int32)
    past = blk < own
    qts = []
    for h in range(n_h):
        qtf = qt_ref[h, 0].astype(F32)
        g = _dot(km_ref[0, h], qtf, precision=lax.Precision.HIGHEST)
        g = jnp.where(past, g, NEG_INF)
        sel = jnp.zeros(g.shape, F32)
        for _ in range(min(MOBA_TOPK, n_blk)):
            mx = jnp.max(g, axis=0, keepdims=True)
            first = jnp.min(jnp.where(g == mx, blk, n_blk), axis=0, keepdims=True)
            hit = blk == first
            sel = jnp.where(hit, 1.0, sel)
            g = jnp.where(hit, NEG_INF, g)
        sel_ref[h] = jnp.where(past, sel, 0.0)
        qts.append((qtf * scale).astype(k_ref.dtype))

    def scores(h, p, n):
        off = pl.multiple_of(p * tq, tq)
        return _dot(k_ref[h, pl.ds(off, n * tq), :], qts[h])

    def chosen(h, n):
        return sel_ref[h, pl.ds(n, 1), :] > 0.5

    def mask_past(h, s, first_blk):
        parts = [jnp.where(chosen(h, first_blk + n), s[n * blk_w:(n + 1) * blk_w], NEG_INF)
                 for n in range(s.shape[0] // blk_w)]
        return jnp.concatenate(parts, axis=0)

    key = lax.broadcasted_iota(jnp.int32, (blk_w, tq), 0)
    qry = lax.broadcasted_iota(jnp.int32, (blk_w, tq), 1)
    bot_ok = key <= qry - blk_w
    for h in range(n_h):
        s = scores(h, i, 1)
        chosen_lim = jnp.where(chosen(h, 2 * i), blk_w, -1)
        top_ok = key <= jnp.where(qry < blk_w, qry, chosen_lim)
        s = jnp.concatenate([jnp.where(top_ok, s[:blk_w], NEG_INF),
                             jnp.where(bot_ok, s[blk_w:], NEG_INF)], axis=0)
        _softmax_first(s, [vt_ref[h, i]], m_ref.at[h], acc_ref.at[h])

    @pl.when(i % 2 == 1)
    def _():
        for h in range(n_h):
            s = mask_past(h, scores(h, i - 1, 1), 2 * (i - 1))
            _softmax_step(s, [vt_ref[h, i - 1]], m_ref.at[h], acc_ref.at[h])

    def body(c, carry):
        for h in range(n_h):
            s = mask_past(h, scores(h, 2 * c, 2), 4 * c)
            _softmax_step(s, [vt_ref[h, 2 * c], vt_ref[h, 2 * c + 1]], m_ref.at[h], acc_ref.at[h])
        return carry

    lax.fori_loop(0, i // 2, body, 0)
    for h in range(n_h):
        o_ref[:, h * dv:(h + 1) * dv] = _softmax_finish(acc_ref.at[h], dv).astype(o_ref.dtype)


def moba_attention(qt, k, vt, k_mean, batch, seq, out_dtype):
    heads, _, hd, tq = qt.shape
    t = k.shape[1]
    dvx = vt.shape[2]
    dv = dvx - ONES_ROWS
    n_blk = seq // MOBA_BLOCK
    nq = seq // tq
    hb = ATTN_HEADS_PER_STEP
    return pl.pallas_call(
        functools.partial(_moba_attn_kernel, scale=hd ** -0.5, dv=dv),
        grid=(batch, heads // hb, nq),
        in_specs=[pl.BlockSpec((hb, 1, hd, tq), lambda b, h, i: (h, b * nq + i, 0, 0)),
                  pl.BlockSpec((hb, seq, hd), lambda b, h, i: (h, b, 0)),
                  pl.BlockSpec((hb, nq, dvx, tq), lambda b, h, i: (h, b, 0, 0)),
                  pl.BlockSpec((1, hb, n_blk, hd), lambda b, h, i: (b, h, 0, 0))],
        out_specs=pl.BlockSpec((tq, hb * dv), lambda b, h, i: (b * nq + i, h)),
        out_shape=jax.ShapeDtypeStruct((t, heads * dv), out_dtype),
        scratch_shapes=[pltpu.VMEM((hb, 1, tq), F32), pltpu.VMEM((hb, dvx, tq), F32),
                        pltpu.VMEM((hb, n_blk, tq), F32)],
        compiler_params=_params("parallel", "parallel", "arbitrary"),
        name="moba_attention",
    )(qt, k, vt, k_mean)


def _out_proj_kernel(a_ref, b_ref, c_ref, ga_ref, gb_ref, gc_ref, w_ref, x_ref, gate_ref, o_ref,
                     y_ref):
    @pl.when(pl.program_id(1) == 0)
    def _():
        wa, wb = a_ref.shape[1], b_ref.shape[1]
        y_ref[:, 0:wa] = _rms(a_ref[...].astype(F32), ga_ref[...]).astype(y_ref.dtype)
        y_ref[:, wa:wa + wb] = _rms(b_ref[...].astype(F32), gb_ref[...]).astype(y_ref.dtype)
        y_ref[:, wa + wb:] = _rms(c_ref[...].astype(F32), gc_ref[...]).astype(y_ref.dtype)

    o_ref[...] = x_ref[...] + gate_ref[0] * _dot(y_ref[...], w_ref[0])


def out_proj(y_a, y_b, y_c, g, w_layers, layer, x2, gate, seq, tm=512, tn=1024):
    t = y_a.shape[0]
    wa, wb, wc = y_a.shape[1], y_b.shape[1], y_c.shape[1]
    _, kd, n = w_layers.shape
    tm, tn = _tile(seq, tm), _tile(n, tn)
    per_b = seq // tm
    ga, gb, gc = g[:wa].reshape(1, wa), g[wa:wa + wb].reshape(1, wb), g[wa + wb:].reshape(1, wc)
    rows = lambda width: pl.BlockSpec((tm, width), lambda i, j: (i, 0))
    vec = lambda width: pl.BlockSpec((1, width), lambda i, j: (0, 0))
    return pl.pallas_call(
        _out_proj_kernel,
        grid=(t // tm, n // tn),
        in_specs=[rows(wa), rows(wb), rows(wc), vec(wa), vec(wb), vec(wc),
                  pl.BlockSpec((1, kd, tn), lambda i, j: (layer, 0, j)),
                  pl.BlockSpec((tm, tn), lambda i, j: (i, j)),
                  pl.BlockSpec((1, 1, tn), lambda i, j: (i // per_b, 0, j))],
        out_specs=pl.BlockSpec((tm, tn), lambda i, j: (i, j)),
        out_shape=jax.ShapeDtypeStruct((t, n), F32),
        scratch_shapes=[pltpu.VMEM((tm, kd), BF16)],
        compiler_params=_params("parallel", "arbitrary"),
        name="out_proj",
    )(y_a, y_b, y_c, ga, gb, gc, w_layers, x2, gate)


def _expert_weight_copies(w_hbm, layer, expert, col, stage, sem):
    width = stage.shape[1]
    return pltpu.make_async_copy(w_hbm.at[layer, expert, :, pl.ds(col, width)], stage, sem)


def _ffn_up_kernel(bexp_ref, first_ref, rows_ref, next_ref, x_ref, wg_hbm, wu_hbm, a_ref,
                   wg_st, wu_st, wg_bf, wu_bf, sem, *, layer):
    j = pl.program_id(0)
    b = pl.program_id(1)
    n_rows = rows_ref[b]
    half = x_ref.shape[0] // 2
    col = pl.multiple_of(j * wg_bf.shape[1], wg_bf.shape[1])

    def copies(expert):
        return (_expert_weight_copies(wg_hbm, layer, expert, col, wg_st, sem.at[0]),
                _expert_weight_copies(wu_hbm, layer, expert, col, wu_st, sem.at[1]))

    @pl.when(b == 0)
    def _():
        for c in copies(bexp_ref[0]):
            c.start()

    @pl.when((n_rows > 0) & (first_ref[b] == 1))
    def _():
        for c in copies(bexp_ref[b]):
            c.wait()
        wg_bf[...] = wg_st[...].astype(BF16)
        wu_bf[...] = wu_st[...].astype(BF16)

        @pl.when(next_ref[b] >= 0)
        def _():
            for c in copies(next_ref[b]):
                c.start()

    def act(x):
        gate = _dot(x, wg_bf[...])
        up = _dot(x, wu_bf[...])
        return (gate * jax.nn.sigmoid(gate) * up).astype(a_ref.dtype)

    @pl.when(n_rows > half)
    def _():
        a_ref[...] = act(x_ref[...])

    @pl.when((n_rows > 0) & (n_rows <= half))
    def _():
        a_ref[:half] = act(x_ref[:half])
        a_ref[half:] = jnp.zeros((half, a_ref.shape[1]), a_ref.dtype)

    @pl.when(n_rows == 0)
    def _():
        a_ref[...] = jnp.zeros_like(a_ref)


def _pack_bf16_pair(lo, hi):
    lo_bits = lax.bitcast_convert_type(lo.astype(BF16).astype(F32), PACKED) >> 16
    hi_bits = lax.bitcast_convert_type(hi.astype(BF16).astype(F32), PACKED) & jnp.uint32(0xFFFF0000)
    return hi_bits | lo_bits


def _unpack_bf16_pair(packed):
    lo = lax.bitcast_convert_type(packed << 16, F32)
    hi = lax.bitcast_convert_type(packed & jnp.uint32(0xFFFF0000), F32)
    return lo, hi


def _ffn_down_kernel(bexp_ref, first_ref, rows_ref, next_ref, a_ref, wd_hbm, o_ref,
                     wd_st, wd_bf, sem, *, layer):
    j = pl.program_id(0)
    b = pl.program_id(1)
    n_rows = rows_ref[b]
    half = a_ref.shape[0] // 2
    col = pl.multiple_of(j * wd_bf.shape[1], wd_bf.shape[1])

    def copy(expert):
        return _expert_weight_copies(wd_hbm, layer, expert, col, wd_st, sem.at[0])

    @pl.when(b == 0)
    def _():
        copy(bexp_ref[0]).start()

    @pl.when((n_rows > 0) & (first_ref[b] == 1))
    def _():
        copy(bexp_ref[b]).wait()
        wd_bf[...] = wd_st[...].astype(BF16)

        @pl.when(next_ref[b] >= 0)
        def _():
            copy(next_ref[b]).start()

    @pl.when(n_rows > half)
    def _():
        y = _dot(a_ref[...], wd_bf[...])
        o_ref[...] = _pack_bf16_pair(y[:half], y[half:])

    @pl.when((n_rows > 0) & (n_rows <= half))
    def _():
        y = _dot(a_ref[:half], wd_bf[...])
        o_ref[...] = _pack_bf16_pair(y, jnp.zeros_like(y))

    @pl.when(n_rows == 0)
    def _():
        o_ref[...] = jnp.zeros_like(o_ref)


def expert_ffn(xs, block_exp, block_first, block_rows, block_next, w_gate, w_up, w_down, layer):
    n_slots, d = xs.shape
    ff = w_gate.shape[3]
    fh, dh = ff // FF_SPLIT, d // FF_SPLIT
    n_blocks = n_slots // MOE_BLOCK
    hbm = pl.BlockSpec(memory_space=pl.ANY)
    up_spec = pltpu.PrefetchScalarGridSpec(
        num_scalar_prefetch=4,
        grid=(FF_SPLIT, n_blocks),
        in_specs=[pl.BlockSpec((MOE_BLOCK, d), lambda j, b, *_: (b, 0)), hbm, hbm],
        out_specs=pl.BlockSpec((MOE_BLOCK, fh), lambda j, b, *_: (b, j)),
        scratch_shapes=[pltpu.VMEM((d, fh), F32), pltpu.VMEM((d, fh), F32),
                        pltpu.VMEM((d, fh), BF16), pltpu.VMEM((d, fh), BF16),
                        pltpu.SemaphoreType.DMA((2,))],
    )
    act = pl.pallas_call(
        functools.partial(_ffn_up_kernel, layer=layer),
        grid_spec=up_spec,
        out_shape=jax.ShapeDtypeStruct((n_slots, ff), BF16),
        compiler_params=_params("arbitrary", "arbitrary"),
        name="expert_ffn_up",
    )(block_exp, block_first, block_rows, block_next, xs, w_gate, w_up)
    down_spec = pltpu.PrefetchScalarGridSpec(
        num_scalar_prefetch=4,
        grid=(FF_SPLIT, n_blocks),
        in_specs=[pl.BlockSpec((MOE_BLOCK, ff), lambda j, b, *_: (b, 0)), hbm],
        out_specs=pl.BlockSpec((MOE_BLOCK // 2, dh), lambda j, b, *_: (b, j)),
        scratch_shapes=[pltpu.VMEM((ff, dh), F32), pltpu.VMEM((ff, dh), BF16),
                        pltpu.SemaphoreType.DMA((1,))],
    )
    return pl.pallas_call(
        functools.partial(_ffn_down_kernel, layer=layer),
        grid_spec=down_spec,
        out_shape=jax.ShapeDtypeStruct((n_slots // 2, d), PACKED),
        compiler_params=_params("arbitrary", "arbitrary"),
        name="expert_ffn_down",
    )(block_exp, block_first, block_rows, block_next, act, w_down)


def _start_row_copies(idx_ref, base, n_rows, src_ref, dst_ref, sem):
    def issue(r, carry):
        row = idx_ref[base + r]
        pltpu.make_async_copy(src_ref.at[pl.ds(row, 1)], dst_ref.at[pl.ds(r, 1)], sem).start()
        return carry

    lax.fori_loop(0, n_rows, issue, 0, unroll=8)


def _wait_row_copies(n_rows, src_ref, dst_ref, sem):
    pltpu.make_async_copy(src_ref.at[pl.ds(0, n_rows)], dst_ref, sem).wait()


def _gather_cast_kernel(idx_ref, nused_ref, src_ref, o_ref, buf, sem):
    b = pl.program_id(0)
    rows = o_ref.shape[0]
    n_used = nused_ref[0]

    @pl.when((b == 0) & (n_used > 0))
    def _():
        _start_row_copies(idx_ref, 0, rows, src_ref, buf.at[0], sem.at[0])

    @pl.when(b + 1 < n_used)
    def _():
        nxt = (b + 1) % 2
        _start_row_copies(idx_ref, (b + 1) * rows, rows, src_ref, buf.at[nxt], sem.at[nxt])

    @pl.when(b < n_used)
    def _():
        cur = b % 2
        _wait_row_copies(rows, src_ref, buf.at[cur], sem.at[cur])
        o_ref[...] = buf[cur].astype(o_ref.dtype)

    @pl.when(b >= n_used)
    def _():
        o_ref[...] = jnp.zeros_like(o_ref)


def gather_rows_cast(src, idx, n_used_tiles, out_dtype):
    d = src.shape[1]
    m = idx.shape[0]
    rows = GATHER_ROWS
    grid_spec = pltpu.PrefetchScalarGridSpec(
        num_scalar_prefetch=2,
        grid=(m // rows,),
        in_specs=[pl.BlockSpec(memory_space=pl.ANY)],
        out_specs=pl.BlockSpec((rows, d), lambda b, ix, nu: (b, 0)),
        scratch_shapes=[pltpu.VMEM((2, rows, d), src.dtype), pltpu.SemaphoreType.DMA((2,))],
    )
    return pl.pallas_call(
        _gather_cast_kernel,
        grid_spec=grid_spec,
        out_shape=jax.ShapeDtypeStruct((m, d), out_dtype),
        compiler_params=_params("arbitrary"),
        name="gather_rows_cast",
    )(idx, n_used_tiles, src)


def _combine_kernel(pos_ref, x_ref, gate_ref, w_ref, half_ref, g_ref, sc_ref, sh_ref, ys_ref,
                    *rest, n_tok, final):
    if final:
        o_ref, ybuf, sem = rest
    else:
        o_ref, h_ref, ybuf, sem = rest
    i = pl.program_id(0)
    n_tiles = pl.num_programs(0)
    tm = x_ref.shape[0]

    def start_tile(tile, slot):
        for kk in range(TOP_K):
            _start_row_copies(pos_ref, kk * n_tok + tile * tm, tm, ys_ref, ybuf.at[slot, kk],
                              sem.at[slot])

    @pl.when(i == 0)
    def _():
        start_tile(0, 0)

    @pl.when(i + 1 < n_tiles)
    def _():
        start_tile(i + 1, (i + 1) % 2)

    cur = i % 2
    for kk in range(TOP_K):
        _wait_row_copies(tm, ys_ref, ybuf.at[cur, kk], sem.at[cur])
    w = w_ref[...]
    upper = half_ref[...] > 0.5
    acc = None
    for kk in range(TOP_K):
        lo, hi = _unpack_bf16_pair(ybuf[cur, kk])
        term = w[:, kk:kk + 1] * jnp.where(upper[:, kk:kk + 1], hi, lo)
        acc = term if acc is None else acc + term
    x_new = x_ref[...] + gate_ref[0] * acc
    if final:
        o_ref[...] = _rms(x_new, g_ref[...])
    else:
        o_ref[...] = x_new
        h_ref[...] = (_rms(x_new, g_ref[...]) * sc_ref[0] + sh_ref[0]).astype(h_ref.dtype)


def moe_combine(x2, gate, ys, pos, w_tok, half_tok, seq, norm_g, sc1p=None, sh=None):
    t, d = x2.shape
    tm = _tile(seq, GATHER_ROWS)
    per_b = seq // tm
    final = sc1p is None
    if final:
        sc1p = sh = jnp.zeros((x2.shape[0] // seq, 1, d), F32)
    row = pl.BlockSpec((tm, d), lambda i, ps: (i, 0))
    mod = pl.BlockSpec((1, 1, d), lambda i, ps: (i // per_b, 0, 0))
    tok = pl.BlockSpec((tm, TOP_K), lambda i, ps: (i, 0))
    grid_spec = pltpu.PrefetchScalarGridSpec(
        num_scalar_prefetch=1,
        grid=(t // tm,),
        in_specs=[row, mod, tok, tok, pl.BlockSpec((1, d), lambda i, ps: (0, 0)), mod, mod,
                  pl.BlockSpec(memory_space=pl.ANY)],
        out_specs=row if final else [row, row],
        scratch_shapes=[pltpu.VMEM((2, TOP_K, tm, d), ys.dtype), pltpu.SemaphoreType.DMA((2,))],
    )
    out_shape = jax.ShapeDtypeStruct((t, d), F32)
    return pl.pallas_call(
        functools.partial(_combine_kernel, n_tok=t, final=final),
        grid_spec=grid_spec,
        out_shape=out_shape if final else [out_shape, jax.ShapeDtypeStruct((t, d), BF16)],
        compiler_params=_params("arbitrary"),
        name="moe_combine",
    )(pos, x2, gate, w_tok, half_tok, norm_g.reshape(1, d), sc1p, sh, ys)


def _group_by_expert(e_idx):
    n_tok = e_idx.shape[1]
    n_asg = n_tok * TOP_K
    flat_e = e_idx.reshape(n_asg)
    onehot = (flat_e[:, None] == jnp.arange(N_EXPERTS)[None, :]).astype(jnp.int32)
    running = jnp.cumsum(onehot, axis=0)
    rank = jnp.sum(running * onehot, axis=1) - 1
    sizes = running[-1]
    padded = (sizes + MOE_BLOCK - 1) // MOE_BLOCK * MOE_BLOCK
    pad_end = jnp.cumsum(padded)
    pad_start = pad_end - padded
    dest = (jnp.sum(pad_start[None, :] * onehot, axis=1) + rank).astype(jnp.int32)
    n_blocks = -(-n_asg // MOE_BLOCK) + N_EXPERTS
    n_slots = n_blocks * MOE_BLOCK
    slot_tok = (jnp.arange(n_slots, dtype=jnp.int32) % n_tok).at[dest].set(
        jnp.arange(n_asg, dtype=jnp.int32) % n_tok, mode="promise_in_bounds")
    block_start = jnp.arange(n_blocks, dtype=jnp.int32) * MOE_BLOCK
    block_exp = jnp.minimum(jnp.sum(block_start[:, None] >= pad_end[None, :], axis=1),
                            N_EXPERTS - 1).astype(jnp.int32)
    n_used = (pad_end[-1] // MOE_BLOCK).astype(jnp.int32)
    block_exp = jnp.where(jnp.arange(n_blocks) < n_used, block_exp, block_exp[n_used - 1])
    prev_exp = jnp.concatenate([jnp.full((1,), -1, jnp.int32), block_exp[:-1]])
    block_first = (block_exp != prev_exp).astype(jnp.int32)
    block_rows = jnp.clip(pad_start[block_exp] + sizes[block_exp] - block_start, 0, MOE_BLOCK)
    block_rows = jnp.where(jnp.arange(n_blocks) < n_used, block_rows, 0).astype(jnp.int32)
    blk = jnp.arange(n_blocks, dtype=jnp.int32)
    starts = jnp.where((block_first == 1) & (block_rows > 0), blk, n_blocks)
    later = lax.cummin(jnp.concatenate([starts[1:], jnp.full((1,), n_blocks, jnp.int32)]),
                       axis=0, reverse=True)
    block_next = jnp.where(later < n_blocks, block_exp[jnp.minimum(later, n_blocks - 1)],
                           -1).astype(jnp.int32)
    return (dest.reshape(TOP_K, n_tok), slot_tok, block_exp, block_first, block_rows, block_next,
            n_used.reshape(1))


def _rot_cols(w):
    half = w.shape[-1] // 2
    return jnp.concatenate([-w[..., half:], w[..., :half]], axis=-1)


IN_ROW_BLOCK = 64


def _split_in_weights_kernel(wt_ref, a_ref, b_ref, c_ref, *, a_blocks, b_blocks, c_blocks):
    g = pl.program_id(1)
    w = wt_ref[0]

    @pl.when(g < a_blocks)
    def _():
        a_ref[0] = w.astype(a_ref.dtype)

    @pl.when((g >= a_blocks) & (g < a_blocks + b_blocks))
    def _():
        b_ref[0] = w.astype(b_ref.dtype)

    @pl.when((g >= a_blocks + b_blocks) & (g < a_blocks + b_blocks + c_blocks))
    def _():
        c_ref[0] = w.astype(c_ref.dtype)

    @pl.when(g == a_blocks + b_blocks + c_blocks)
    def _():
        half = w.shape[0] // 2
        a_ref[0] = jnp.concatenate([-w[half:], w[:half]], axis=0).astype(a_ref.dtype)


def split_in_weights(w_in):
    layers, d, n_in = w_in.shape
    wt = jnp.swapaxes(w_in, 1, 2)
    rb = IN_ROW_BLOCK
    assert MLA_ROPE == rb
    a_blocks = (Q_LORA + KV_LORA + MLA_ROPE) // rb
    b_blocks = 3 * CONV_CH // rb
    c_blocks = n_in // rb - a_blocks - b_blocks
    last = a_blocks + b_blocks + c_blocks

    def src(l, g):
        return (l, jnp.where(g == last, a_blocks - 1, g), 0)

    def dst_a(l, g):
        return (l, jnp.where(g == last, a_blocks, jnp.minimum(g, a_blocks - 1)), 0)

    def dst_b(l, g):
        return (l, jnp.clip(g - a_blocks, 0, b_blocks - 1), 0)

    def dst_c(l, g):
        return (l, jnp.clip(g - a_blocks - b_blocks, 0, c_blocks - 1), 0)

    blk = (1, rb, d)
    return pl.pallas_call(
        functools.partial(_split_in_weights_kernel, a_blocks=a_blocks, b_blocks=b_blocks,
                          c_blocks=c_blocks),
        grid=(layers, last + 1),
        in_specs=[pl.BlockSpec(blk, src)],
        out_specs=[pl.BlockSpec(blk, dst_a), pl.BlockSpec(blk, dst_b), pl.BlockSpec(blk, dst_c)],
        out_shape=[jax.ShapeDtypeStruct((layers, nb * rb, d), BF16)
                   for nb in (a_blocks + 1, b_blocks, c_blocks)],
        compiler_params=_params("arbitrary", "arbitrary"),
        name="split_in_weights",
    )(wt)


def _prep_weights(w_in, w_uq, w_ukv, w_out):
    layers = w_in.shape[0]
    w_a, w_b, w_c = split_in_weights(w_in)
    wq = w_uq.reshape(layers, Q_LORA, MLA_HEADS, MLA_NOPE + MLA_ROPE)
    wq_rope = wq[..., MLA_NOPE:]
    wq = jnp.concatenate([wq, _rot_cols(wq_rope)], axis=-1).transpose(0, 2, 1, 3).astype(BF16)
    return dict(w_a=w_a, w_b=w_b, w_c=w_c, w_q=wq, w_kv=w_ukv.astype(BF16),
                w_out=w_out.astype(BF16))


def _rope_tables(seq, dim):
    inv = 1.0 / (ROPE_THETA ** (jnp.arange(0, dim, 2, dtype=F32) / dim))
    ang = jnp.arange(seq, dtype=F32)[:, None] * inv[None, :]
    return jnp.cos(ang), jnp.sin(ang)


def kernel(x, c, w_mod, mod_table, mix_norm_g, w_in, q_norm_g, kv_norm_g, w_uq, w_ukv, conv_w,
           group_norm_g, w_out, ffn_norm_g, w_router, router_bias, w_gate, w_up, w_down,
           final_norm_g):
    batch, seq, d = x.shape
    depth = w_in.shape[0]
    t = batch * seq
    x2 = x.reshape(t, d)

    cos_a, sin_a = _rope_tables(seq, MLA_ROPE)
    zeros_a = jnp.zeros_like(cos_a)
    cos_p = jnp.concatenate([cos_a, cos_a, zeros_a, zeros_a], axis=1)
    sin_p = jnp.concatenate([sin_a, sin_a, zeros_a, zeros_a], axis=1)
    cos_b, sin_b = _rope_tables(seq, MOBA_HD)
    cos_f = jnp.concatenate([cos_b, cos_b], axis=1)
    sin_s = jnp.concatenate([-sin_b, sin_b], axis=1)

    c_pad = jnp.zeros((SUBLANE, d), F32).at[:batch].set(c)
    mod_shared = mod_matmul(c_pad, w_mod)[:batch].reshape(batch, N_MOD, d)

    wr_t = w_router.T
    wr_hi = wr_t.astype(BF16)
    wr_lo = (wr_t - wr_hi.astype(F32)).astype(BF16)
    router = (wr_hi, wr_lo, router_bias.astype(F32).reshape(N_EXPERTS, 1))
    mla_scale = (MLA_NOPE + MLA_ROPE) ** -0.5

    mods = []
    for l in range(depth):
        mod = mod_shared + mod_table[l][None]
        mods.append([mod[:, i][:, None, :] for i in range(N_MOD)])

    p = _prep_weights(w_in, w_uq, w_ukv, w_out)
    h = norm_mod(x2, mix_norm_g[0], 1.0 + mods[0][1], mods[0][0], seq)
    for l in range(depth):
        sh1, sc1, g1, sh2, sc2, g2 = mods[l]

        proj_a = matmul(h, p["w_a"], l, F32, tm=512)
        bch = matmul(h, p["w_b"], l, F32)
        qkv = matmul(h, p["w_c"], l, F32)

        q_a = mla_q_up(proj_a, q_norm_g[l], p["w_q"], l, cos_p, sin_p, seq, mla_scale)
        k_a, v_a = mla_kv_up(proj_a, kv_norm_g[l], p["w_kv"], l, cos_p, sin_p, seq)
        y_a = mla_attention(q_a, k_a, v_a, batch, seq, BF16)

        y_b = conv_mixer(bch, conv_w[l], seq, F32)

        q_c, k_c, v_c, k_mean = moba_prep(qkv, cos_f, sin_s, seq)
        n_blk = seq // MOBA_BLOCK
        k_mean = k_mean.reshape(batch, n_blk, MOBA_HEADS, MOBA_HD).transpose(0, 2, 1, 3)
        y_c = moba_attention(q_c, k_c, v_c, k_mean, batch, seq, BF16)

        x2 = out_proj(y_a, y_b, y_c, group_norm_g[l], p["w_out"], l, x2, g1, seq)

        h2, e_idx, gates = norm_mod(x2, ffn_norm_g[l], 1.0 + sc2, sh2, seq, out_dtype=F32,
                                    router=router)
        pos, slot_tok, block_exp, block_first, block_rows, block_next, n_used = _group_by_expert(e_idx)
        xs = gather_rows_cast(h2, slot_tok, n_used * (MOE_BLOCK // GATHER_ROWS), BF16)
        ys = expert_ffn(xs, block_exp, block_first, block_rows, block_next, w_gate, w_up, w_down,
                        l)
        sub = pos % MOE_BLOCK
        ys_row = ((pos // MOE_BLOCK) * (MOE_BLOCK // 2) + sub % (MOE_BLOCK // 2)).reshape(-1)
        ys_half = (sub // (MOE_BLOCK // 2)).astype(F32).T
        if l + 1 < depth:
            nxt = mods[l + 1]
            x2, h = moe_combine(x2, g2, ys, ys_row, gates.T, ys_half, seq, mix_norm_g[l + 1],
                                1.0 + nxt[1], nxt[0])
        else:
            out = moe_combine(x2, g2, ys, ys_row, gates.T, ys_half, seq, final_norm_g)

    return out.reshape(batch, seq, d)
```

```python
import functools

import jax
import jax.numpy as jnp
from jax import lax
from jax.experimental import pallas as pl
from jax.experimental.pallas import tpu as pltpu

MLA_HEADS = 16
MLA_NOPE = 128
MLA_ROPE = 64
MLA_V = 128
Q_LORA = 1024
KV_LORA = 512
MLA_DK_PAD = 256
CONV_CH = 1024
CONV_W = 3
MOBA_HEADS = 8
MOBA_HD = 128
MOBA_BLOCK = 256
MOBA_TOPK = 3
ROPE_THETA = 10000.0
EPS = 1e-6
N_MOD = 6
N_EXPERTS = 16
N_GROUPS = 4
TOP_K = 2
MOE_BLOCK = 512
GATHER_ROWS = 256
FF_SPLIT = 2
UP_HEADS_PER_STEP = 4
ATTN_HEADS_PER_STEP = 4
ATTN_TILE = 512
ONES_ROWS = 16

V7X_VMEM_LIMIT_BYTES = 56 * 1024 * 1024
LANE = 128
SUBLANE = 8

F32 = jnp.float32
BF16 = jnp.bfloat16
PACKED = jnp.uint32
NEG_INF = float("-inf")


def _params(*sem):
    return pltpu.CompilerParams(dimension_semantics=sem, vmem_limit_bytes=V7X_VMEM_LIMIT_BYTES)


def _dot(a, b, precision=None):
    return jnp.dot(a, b, preferred_element_type=F32, precision=precision)


def _dot_nt(a, b):
    return lax.dot_general(a, b, (((1,), (1,)), ((), ())), preferred_element_type=F32)


def _tile(dim, want):
    return want if dim % want == 0 else dim


def _mod_kernel(c_ref, w0_ref, w1_ref, o_ref):
    @pl.when(pl.program_id(0) == 0)
    def _():
        o_ref[...] = jnp.zeros_like(o_ref)

    c = c_ref[...]
    a = (c * jax.nn.sigmoid(c)).astype(BF16)
    half = w0_ref.shape[1]
    o_ref[:, :half] += _dot(a, w0_ref[...].astype(BF16))
    o_ref[:, half:] += _dot(a, w1_ref[...].astype(BF16))


def mod_matmul(c_pad, w_mod):
    rows, d = c_pad.shape
    n = w_mod.shape[1]
    tk = _tile(d, LANE)
    return pl.pallas_call(
        _mod_kernel,
        grid=(d // tk,),
        in_specs=[pl.BlockSpec((rows, tk), lambda k: (0, k)),
                  pl.BlockSpec((tk, n // 2), lambda k: (k, 0)),
                  pl.BlockSpec((tk, n // 2), lambda k: (k, 1))],
        out_specs=pl.BlockSpec((rows, n), lambda k: (0, 0)),
        out_shape=jax.ShapeDtypeStruct((rows, n), F32),
        compiler_params=_params("arbitrary"),
        name="mod_matmul",
    )(c_pad, w_mod, w_mod)


def _rms(x, g):
    return x * lax.rsqrt(jnp.mean(x * x, axis=-1, keepdims=True) + EPS) * g


def _norm_mod_kernel(x_ref, g_ref, sc_ref, sh_ref, o_ref):
    y = _rms(x_ref[...], g_ref[...])
    o_ref[...] = (y * sc_ref[0] + sh_ref[0]).astype(o_ref.dtype)


def _top2_sum(a, b, c, d):
    hi1, lo1 = jnp.maximum(a, b), jnp.minimum(a, b)
    hi2, lo2 = jnp.maximum(c, d), jnp.minimum(c, d)
    return jnp.maximum(hi1, hi2) + jnp.maximum(jnp.minimum(hi1, hi2), jnp.maximum(lo1, lo2))


def _norm_router_kernel(x_ref, g_ref, sc_ref, sh_ref, whi_ref, wlo_ref, bias_ref,
                        o_ref, e_ref, gate_ref):
    h = _rms(x_ref[...], g_ref[...]) * sc_ref[0] + sh_ref[0]
    hi = h.astype(BF16)
    o_ref[...] = h.astype(o_ref.dtype)
    lo = (h - hi.astype(F32)).astype(BF16)
    whl = jnp.concatenate([whi_ref[...], wlo_ref[...]], axis=0)
    both = _dot_nt(whl, hi)
    n_e = whi_ref.shape[0]
    logits = both[:n_e] + both[n_e:] + _dot_nt(whi_ref[...], lo)
    scores = jax.nn.sigmoid(logits)
    biased = scores + bias_ref[...]
    per_grp = N_EXPERTS // N_GROUPS
    b_rows = [biased[e:e + 1, :] for e in range(N_EXPERTS)]
    s_rows = [scores[e:e + 1, :] for e in range(N_EXPERTS)]
    grp_scores = [_top2_sum(*b_rows[g * per_grp:(g + 1) * per_grp]) for g in range(N_GROUPS)]
    best, grp = grp_scores[0], jnp.zeros(grp_scores[0].shape, jnp.int32)
    for g in range(1, N_GROUPS):
        better = grp_scores[g] > best
        grp = jnp.where(better, g, grp)
        best = jnp.where(better, grp_scores[g], best)
    b4, s4 = b_rows[:per_grp], s_rows[:per_grp]
    for g in range(1, N_GROUPS):
        in_g = grp == g
        b4 = [jnp.where(in_g, b_rows[g * per_grp + j], b4[j]) for j in range(per_grp)]
        s4 = [jnp.where(in_g, s_rows[g * per_grp + j], s4[j]) for j in range(per_grp)]
    v1, i1 = b4[0], jnp.zeros(grp.shape, jnp.int32)
    for j in range(1, per_grp):
        better = b4[j] > v1
        i1 = jnp.where(better, j, i1)
        v1 = jnp.where(better, b4[j], v1)
    v2, i2 = jnp.full(v1.shape, NEG_INF, F32), jnp.zeros(grp.shape, jnp.int32)
    for j in range(per_grp):
        better = (i1 != j) & (b4[j] > v2)
        i2 = jnp.where(better, j, i2)
        v2 = jnp.where(better, b4[j], v2)
    g1, g2 = s4[0], s4[0]
    for j in range(1, per_grp):
        g1 = jnp.where(i1 == j, s4[j], g1)
        g2 = jnp.where(i2 == j, s4[j], g2)
    total = g1 + g2
    e_ref[...] = jnp.concatenate([grp * per_grp + i1, grp * per_grp + i2], axis=0)
    gate_ref[...] = jnp.concatenate([g1 / total, g2 / total], axis=0)


def norm_mod(x2, g, sc1p, sh, seq, out_dtype=BF16, router=None):
    t, d = x2.shape
    tm = _tile(seq, 256)
    per_b = seq // tm
    row = pl.BlockSpec((tm, d), lambda i: (i, 0))
    vec = pl.BlockSpec((1, d), lambda i: (0, 0))
    mod = pl.BlockSpec((1, 1, d), lambda i: (i // per_b, 0, 0))
    g2 = g.reshape(1, d)
    if router is None:
        return pl.pallas_call(
            _norm_mod_kernel, grid=(t // tm,), in_specs=[row, vec, mod, mod], out_specs=row,
            out_shape=jax.ShapeDtypeStruct((t, d), out_dtype),
            compiler_params=_params("arbitrary"), name="norm_mod")(x2, g2, sc1p, sh)
    whi_t, wlo_t, bias = router
    ne = whi_t.shape[0]
    wspec = pl.BlockSpec((ne, d), lambda i: (0, 0))
    kspec = pl.BlockSpec((TOP_K, tm), lambda i: (0, i))
    return pl.pallas_call(
        _norm_router_kernel, grid=(t // tm,),
        in_specs=[row, vec, mod, mod, wspec, wspec, pl.BlockSpec((ne, 1), lambda i: (0, 0))],
        out_specs=[row, kspec, kspec],
        out_shape=[jax.ShapeDtypeStruct((t, d), out_dtype),
                   jax.ShapeDtypeStruct((TOP_K, t), jnp.int32),
                   jax.ShapeDtypeStruct((TOP_K, t), F32)],
        compiler_params=_params("arbitrary"), name="norm_router")(x2, g2, sc1p, sh, whi_t, wlo_t, bias)


def _mm_kernel(a_ref, wt_ref, o_ref):
    o_ref[...] = _dot_nt(a_ref[...], wt_ref[0]).astype(o_ref.dtype)


def matmul(a, wt_layers, layer, out_dtype, tm=1024, tn=1024):
    m, kd = a.shape
    n = wt_layers.shape[1]
    tm, tn = _tile(m, tm), _tile(n, tn)
    return pl.pallas_call(
        _mm_kernel,
        grid=(m // tm, n // tn),
        in_specs=[pl.BlockSpec((tm, kd), lambda i, j: (i, 0)),
                  pl.BlockSpec((1, tn, kd), lambda i, j: (layer, j, 0))],
        out_specs=pl.BlockSpec((tm, tn), lambda i, j: (i, j)),
        out_shape=jax.ShapeDtypeStruct((m, n), out_dtype),
        compiler_params=_params("parallel", "parallel"),
        name="matmul",
    )(a, wt_layers)


def _rope_pair(r2, cos_ref, sin_ref):
    return r2 * cos_ref[...] + pltpu.roll(r2, MLA_ROPE, 1) * sin_ref[...]


def _with_ones_rows(vt):
    return jnp.concatenate([vt, jnp.ones((ONES_ROWS, vt.shape[1]), vt.dtype)], axis=0)


def _qup_kernel(cq_ref, g_ref, w_ref, cos_ref, sin_ref, o_ref, an_ref, *, scale):
    @pl.when(pl.program_id(1) == 0)
    def _():
        an_ref[...] = _rms(cq_ref[...], g_ref[...]).astype(BF16)

    an = an_ref[...]
    for hh in range(w_ref.shape[1]):
        res = _dot(an, w_ref[0, hh])
        roped = _rope_pair(res[:, MLA_NOPE:], cos_ref, sin_ref)
        q = jnp.concatenate([res[:, :MLA_NOPE], roped], axis=1) * scale
        o_ref[hh, 0] = q.T.astype(o_ref.dtype)


def mla_q_up(proj_a, g, w_q, layer, cos_p, sin_p, seq, scale):
    t = proj_a.shape[0]
    heads = w_q.shape[1]
    hb = UP_HEADS_PER_STEP
    tm = _tile(seq, ATTN_TILE)
    per_b = seq // tm
    return pl.pallas_call(
        functools.partial(_qup_kernel, scale=scale),
        grid=(t // tm, heads // hb),
        in_specs=[pl.BlockSpec((tm, Q_LORA), lambda i, h: (i, 0)),
                  pl.BlockSpec((1, Q_LORA), lambda i, h: (0, 0)),
                  pl.BlockSpec((1, hb, Q_LORA, MLA_DK_PAD), lambda i, h: (layer, h, 0, 0)),
                  pl.BlockSpec((tm, LANE), lambda i, h: (i % per_b, 0)),
                  pl.BlockSpec((tm, LANE), lambda i, h: (i % per_b, 0))],
        out_specs=pl.BlockSpec((hb, 1, MLA_DK_PAD, tm), lambda i, h: (h, i, 0, 0)),
        out_shape=jax.ShapeDtypeStruct((heads, t // tm, MLA_DK_PAD, tm), BF16),
        scratch_shapes=[pltpu.VMEM((tm, Q_LORA), BF16)],
        compiler_params=_params("parallel", "arbitrary"),
        name="mla_q_up",
    )(proj_a, g.reshape(1, Q_LORA), w_q, cos_p, sin_p)


def _kvup_kernel(ckv_ref, g_ref, w_ref, kpe_ref, cos_ref, sin_ref, k_ref, vt_ref, an_ref):
    @pl.when(pl.program_id(1) == 0)
    def _():
        an_ref[...] = _rms(ckv_ref[...], g_ref[...]).astype(BF16)

    an = an_ref[...]
    k_rot = _rope_pair(kpe_ref[...], cos_ref, sin_ref)
    hw = MLA_NOPE + MLA_V
    for hh in range(k_ref.shape[0]):
        res = _dot(an, w_ref[0, :, hh * hw:(hh + 1) * hw])
        k_ref[hh] = jnp.concatenate([res[:, :MLA_NOPE], k_rot], axis=1).astype(k_ref.dtype)
        vt_ref[hh, 0] = _with_ones_rows(res[:, MLA_NOPE:].T).astype(vt_ref.dtype)


def mla_kv_up(proj_a, g, w_kv, layer, cos_p, sin_p, seq):
    t = proj_a.shape[0]
    hw = MLA_NOPE + MLA_V
    heads = w_kv.shape[2] // hw
    hb = UP_HEADS_PER_STEP
    tm = _tile(seq, ATTN_TILE)
    per_b = seq // tm
    return pl.pallas_call(
        _kvup_kernel,
        grid=(t // tm, heads // hb),
        in_specs=[pl.BlockSpec((tm, KV_LORA), lambda i, h: (i, Q_LORA // KV_LORA)),
                  pl.BlockSpec((1, KV_LORA), lambda i, h: (0, 0)),
                  pl.BlockSpec((1, KV_LORA, hb * hw), lambda i, h: (layer, 0, h)),
                  pl.BlockSpec((tm, LANE), lambda i, h: (i, (Q_LORA + KV_LORA) // LANE)),
                  pl.BlockSpec((tm, LANE), lambda i, h: (i % per_b, 0)),
                  pl.BlockSpec((tm, LANE), lambda i, h: (i % per_b, 0))],
        out_specs=[pl.BlockSpec((hb, tm, MLA_DK_PAD), lambda i, h: (h, i, 0)),
                   pl.BlockSpec((hb, 1, MLA_V + ONES_ROWS, tm), lambda i, h: (h, i, 0, 0))],
        out_shape=[jax.ShapeDtypeStruct((heads, t, MLA_DK_PAD), BF16),
                   jax.ShapeDtypeStruct((heads, t // tm, MLA_V + ONES_ROWS, tm), BF16)],
        scratch_shapes=[pltpu.VMEM((tm, KV_LORA), BF16)],
        compiler_params=_params("parallel", "arbitrary"),
        name="mla_kv_up",
    )(proj_a, g.reshape(1, KV_LORA), w_kv, proj_a, cos_p, sin_p)


def _pv(vts, p):
    rows = p.shape[0] // len(vts)
    out = _dot(vts[0], p[:rows].astype(vts[0].dtype))
    for n in range(1, len(vts)):
        out = out + _dot(vts[n], p[n * rows:(n + 1) * rows].astype(vts[n].dtype))
    return out


def _softmax_first(s, vts, m_ref, acc_ref):
    m = jnp.max(s, axis=0, keepdims=True)
    m_ref[...] = m
    acc_ref[...] = _pv(vts, jnp.exp(s - m))


def _softmax_step(s, vts, m_ref, acc_ref):
    m_prev = m_ref[...]
    m_new = jnp.maximum(m_prev, jnp.max(s, axis=0, keepdims=True))
    alpha = jnp.exp(m_prev - m_new)
    acc_ref[...] = alpha * acc_ref[...] + _pv(vts, jnp.exp(s - m_new))
    m_ref[...] = m_new


def _softmax_finish(acc_ref, dv):
    acc = acc_ref[...]
    return (acc[:dv, :] / acc[dv:dv + 1, :]).T


def _causal_mask_t(s):
    key = lax.broadcasted_iota(jnp.int32, s.shape, 0)
    qry = lax.broadcasted_iota(jnp.int32, s.shape, 1)
    return jnp.where(key <= qry, s, NEG_INF)


def _mla_attn_kernel(qt_ref, k_ref, vt_ref, o_ref, m_ref, acc_ref, *, tq, dv):
    i = pl.program_id(2)
    n_h = qt_ref.shape[0]
    qts = [qt_ref[h, 0] for h in range(n_h)]

    def scores(h, j, n):
        off = pl.multiple_of(j * tq, tq)
        return _dot(k_ref[h, pl.ds(off, n * tq), :], qts[h])

    for h in range(n_h):
        _softmax_first(_causal_mask_t(scores(h, i, 1)), [vt_ref[h, i]], m_ref.at[h], acc_ref.at[h])

    @pl.when(i % 2 == 1)
    def _():
        for h in range(n_h):
            _softmax_step(scores(h, i - 1, 1), [vt_ref[h, i - 1]], m_ref.at[h], acc_ref.at[h])

    def body(c, carry):
        for h in range(n_h):
            _softmax_step(scores(h, 2 * c, 2), [vt_ref[h, 2 * c], vt_ref[h, 2 * c + 1]],
                          m_ref.at[h], acc_ref.at[h])
        return carry

    lax.fori_loop(0, i // 2, body, 0)
    for h in range(n_h):
        o_ref[:, h * dv:(h + 1) * dv] = _softmax_finish(acc_ref.at[h], dv).astype(o_ref.dtype)


def mla_attention(qt, k, vt, batch, seq, out_dtype):
    heads, _, dk, tq = qt.shape
    t = k.shape[1]
    dvx = vt.shape[2]
    dv = dvx - ONES_ROWS
    nq = seq // tq
    hb = ATTN_HEADS_PER_STEP
    return pl.pallas_call(
        functools.partial(_mla_attn_kernel, tq=tq, dv=dv),
        grid=(batch, heads // hb, nq),
        in_specs=[pl.BlockSpec((hb, 1, dk, tq), lambda b, h, i: (h, b * nq + i, 0, 0)),
                  pl.BlockSpec((hb, seq, dk), lambda b, h, i: (h, b, 0)),
                  pl.BlockSpec((hb, nq, dvx, tq), lambda b, h, i: (h, b, 0, 0))],
        out_specs=pl.BlockSpec((tq, hb * dv), lambda b, h, i: (b * nq + i, h)),
        out_shape=jax.ShapeDtypeStruct((t, heads * dv), out_dtype),
        scratch_shapes=[pltpu.VMEM((hb, 1, tq), F32), pltpu.VMEM((hb, dvx, tq), F32)],
        compiler_params=_params("parallel", "parallel", "arbitrary"),
        name="mla_attention",
    )(qt, k, vt)


def _conv_kernel(b_ref, c_ref, h_ref, w_ref, o_ref, carry_ref, *, per_b):
    i = pl.program_id(0)

    @pl.when(i % per_b == 0)
    def _():
        carry_ref[...] = jnp.zeros_like(carry_ref)

    tm = o_ref.shape[0]
    w0, w1, w2 = w_ref[0:1, :], w_ref[1:2, :], w_ref[2:3, :]
    u = c_ref[...] * h_ref[...]
    y = w0 * pltpu.roll(u, 2, 0) + w1 * pltpu.roll(u, 1, 0) + w2 * u
    o_ref[...] = (b_ref[...] * y).astype(o_ref.dtype)
    u8 = u[0:SUBLANE, :]
    tail = carry_ref[...]
    r8 = lax.broadcasted_iota(jnp.int32, u8.shape, 0)
    p1 = jnp.where(r8 < 1, pltpu.roll(tail, 1, 0), pltpu.roll(u8, 1, 0))
    p2 = jnp.where(r8 < 2, pltpu.roll(tail, 2, 0), pltpu.roll(u8, 2, 0))
    y8 = w0 * p2 + w1 * p1 + w2 * u8
    o_ref[0:SUBLANE, :] = (b_ref[0:SUBLANE, :] * y8).astype(o_ref.dtype)
    carry_ref[...] = u[tm - SUBLANE:tm, :]


def conv_mixer(bch, conv_w, seq, out_dtype):
    t = bch.shape[0]
    ch = conv_w.shape[1]
    tm = _tile(seq, 256)
    per_b = seq // tm
    return pl.pallas_call(
        functools.partial(_conv_kernel, per_b=per_b),
        grid=(t // tm,),
        in_specs=[pl.BlockSpec((tm, ch), lambda i: (i, 0)),
                  pl.BlockSpec((tm, ch), lambda i: (i, 1)),
                  pl.BlockSpec((tm, ch), lambda i: (i, 2)),
                  pl.BlockSpec((CONV_W, ch), lambda i: (0, 0))],
        out_specs=pl.BlockSpec((tm, ch), lambda i: (i, 0)),
        out_shape=jax.ShapeDtypeStruct((t, ch), out_dtype),
        scratch_shapes=[pltpu.VMEM((SUBLANE, ch), F32)],
        compiler_params=_params("arbitrary"),
        name="conv_mixer",
    )(bch, bch, bch, conv_w)


def _moba_prep_kernel(q_ref, k_ref, v_ref, cos_ref, sin_ref, qo_ref, ko_ref, vo_ref, km_ref):
    heads = ko_ref.shape[0]
    cos, sin = cos_ref[...], sin_ref[...]
    n_sub = km_ref.shape[0]
    means = [[] for _ in range(n_sub)]
    for h in range(heads):
        sl = slice(h * MOBA_HD, (h + 1) * MOBA_HD)
        qh = q_ref[:, sl]
        kh = k_ref[:, sl]
        qr = qh * cos + pltpu.roll(qh, MOBA_HD // 2, 1) * sin
        qo_ref[h, 0] = qr.T.astype(qo_ref.dtype)
        kr = kh * cos + pltpu.roll(kh, MOBA_HD // 2, 1) * sin
        ko_ref[h] = kr.astype(ko_ref.dtype)
        vo_ref[h, 0] = _with_ones_rows(v_ref[:, sl].T).astype(vo_ref.dtype)
        for s in range(n_sub):
            means[s].append(jnp.mean(kr[s * MOBA_BLOCK:(s + 1) * MOBA_BLOCK], axis=0, keepdims=True))
    for s in range(n_sub):
        km_ref[s] = jnp.concatenate(means[s], axis=0)


def moba_prep(qkv, cos_f, sin_s, seq):
    t = qkv.shape[0]
    width = qkv.shape[1] // 3
    heads = width // MOBA_HD
    tm = 2 * MOBA_BLOCK
    assert seq % tm == 0
    per_b = seq // tm
    nt = t // tm
    return pl.pallas_call(
        _moba_prep_kernel,
        grid=(nt,),
        in_specs=[pl.BlockSpec((tm, width), lambda i: (i, 0)),
                  pl.BlockSpec((tm, width), lambda i: (i, 1)),
                  pl.BlockSpec((tm, width), lambda i: (i, 2)),
                  pl.BlockSpec((tm, MOBA_HD), lambda i: (i % per_b, 0)),
                  pl.BlockSpec((tm, MOBA_HD), lambda i: (i % per_b, 0))],
        out_specs=[pl.BlockSpec((heads, 1, MOBA_HD, tm), lambda i: (0, i, 0, 0)),
                   pl.BlockSpec((heads, tm, MOBA_HD), lambda i: (0, i, 0)),
                   pl.BlockSpec((heads, 1, MOBA_HD + ONES_ROWS, tm), lambda i: (0, i, 0, 0)),
                   pl.BlockSpec((2, heads, MOBA_HD), lambda i: (i, 0, 0))],
        out_shape=[jax.ShapeDtypeStruct((heads, nt, MOBA_HD, tm), BF16),
                   jax.ShapeDtypeStruct((heads, t, MOBA_HD), BF16),
                   jax.ShapeDtypeStruct((heads, nt, MOBA_HD + ONES_ROWS, tm), BF16),
                   jax.ShapeDtypeStruct((2 * nt, heads, MOBA_HD), F32)],
        compiler_params=_params("arbitrary"),
        name="moba_prep",
    )(qkv, qkv, qkv, cos_f, sin_s)


def _moba_attn_kernel(qt_ref, k_ref, vt_ref, km_ref, o_ref, m_ref, acc_ref, sel_ref, *, scale, dv):
    i = pl.program_id(2)
    n_h = qt_ref.shape[0]
    n_blk = km_ref.shape[2]
    tq = qt_ref.shape[3]
    blk_w = MOBA_BLOCK
    blk = lax.broadcasted_iota(jnp.int32, (n_blk, tq), 0)
    own = 2 * i + (lax.broadcasted_iota(jnp.int32, (n_blk, tq), 1) >= blk_w).astype(jnp.int32)
    past = blk < own
    qts = []
    for h in range(n_h):
        qtf = qt_ref[h, 0].astype(F32)
        g = _dot(km_ref[0, h], qtf, precision=lax.Precision.HIGHEST)
        g = jnp.where(past, g, NEG_INF)
        sel = jnp.zeros(g.shape, F32)
        for _ in range(min(MOBA_TOPK, n_blk)):
            mx = jnp.max(g, axis=0, keepdims=True)
            first = jnp.min(jnp.where(g == mx, blk, n_blk), axis=0, keepdims=True)
            hit = blk == first
            sel = jnp.where(hit, 1.0, sel)
            g = jnp.where(hit, NEG_INF, g)
        sel_ref[h] = jnp.where(past, sel, 0.0)
        qts.append((qtf * scale).astype(k_ref.dtype))

    def scores(h, p, n):
        off = pl.multiple_of(p * tq, tq)
        return _dot(k_ref[h, pl.ds(off, n * tq), :], qts[h])

    def chosen(h, n):
        return sel_ref[h, pl.ds(n, 1), :] > 0.5

    def mask_past(h, s, first_blk):
        parts = [jnp.where(chosen(h, first_blk + n), s[n * blk_w:(n + 1) * blk_w], NEG_INF)
                 for n in range(s.shape[0] // blk_w)]
        return jnp.concatenate(parts, axis=0)

    key = lax.broadcasted_iota(jnp.int32, (blk_w, tq), 0)
    qry = lax.broadcasted_iota(jnp.int32, (blk_w, tq), 1)
    bot_ok = key <= qry - blk_w
    for h in range(n_h):
        s = scores(h, i, 1)
        chosen_lim = jnp.where(chosen(h, 2 * i), blk_w, -1)
        top_ok = key <= jnp.where(qry < blk_w, qry, chosen_lim)
        s = jnp.concatenate([jnp.where(top_ok, s[:blk_w], NEG_INF),
                             jnp.where(bot_ok, s[blk_w:], NEG_INF)], axis=0)
        _softmax_first(s, [vt_ref[h, i]], m_ref.at[h], acc_ref.at[h])

    @pl.when(i % 2 == 1)
    def _():
        for h in range(n_h):
            s = mask_past(h, scores(h, i - 1, 1), 2 * (i - 1))
            _softmax_step(s, [vt_ref[h, i - 1]], m_ref.at[h], acc_ref.at[h])

    def body(c, carry):
        for h in range(n_h):
            s = mask_past(h, scores(h, 2 * c, 2), 4 * c)
            _softmax_step(s, [vt_ref[h, 2 * c], vt_ref[h, 2 * c + 1]], m_ref.at[h], acc_ref.at[h])
        return carry

    lax.fori_loop(0, i // 2, body, 0)
    for h in range(n_h):
        o_ref[:, h * dv:(h + 1) * dv] = _softmax_finish(acc_ref.at[h], dv).astype(o_ref.dtype)


def moba_attention(qt, k, vt, k_mean, batch, seq, out_dtype):
    heads, _, hd, tq = qt.shape
    t = k.shape[1]
    dvx = vt.shape[2]
    dv = dvx - ONES_ROWS
    n_blk = seq // MOBA_BLOCK
    nq = seq // tq
    hb = ATTN_HEADS_PER_STEP
    return pl.pallas_call(
        functools.partial(_moba_attn_kernel, scale=hd ** -0.5, dv=dv),
        grid=(batch, heads // hb, nq),
        in_specs=[pl.BlockSpec((hb, 1, hd, tq), lambda b, h, i: (h, b * nq + i, 0, 0)),
                  pl.BlockSpec((hb, seq, hd), lambda b, h, i: (h, b, 0)),
                  pl.BlockSpec((hb, nq, dvx, tq), lambda b, h, i: (h, b, 0, 0)),
                  pl.BlockSpec((1, hb, n_blk, hd), lambda b, h, i: (b, h, 0, 0))],
        out_specs=pl.BlockSpec((tq, hb * dv), lambda b, h, i: (b * nq + i, h)),
        out_shape=jax.ShapeDtypeStruct((t, heads * dv), out_dtype),
        scratch_shapes=[pltpu.VMEM((hb, 1, tq), F32), pltpu.VMEM((hb, dvx, tq), F32),
                        pltpu.VMEM((hb, n_blk, tq), F32)],
        compiler_params=_params("parallel", "parallel", "arbitrary"),
        name="moba_attention",
    )(qt, k, vt, k_mean)


def _out_proj_kernel(a_ref, b_ref, c_ref, ga_ref, gb_ref, gc_ref, w_ref, x_ref, gate_ref, o_ref,
                     y_ref):
    @pl.when(pl.program_id(1) == 0)
    def _():
        wa, wb = a_ref.shape[1], b_ref.shape[1]
        y_ref[:, 0:wa] = _rms(a_ref[...].astype(F32), ga_ref[...]).astype(y_ref.dtype)
        y_ref[:, wa:wa + wb] = _rms(b_ref[...].astype(F32), gb_ref[...]).astype(y_ref.dtype)
        y_ref[:, wa + wb:] = _rms(c_ref[...].astype(F32), gc_ref[...]).astype(y_ref.dtype)

    o_ref[...] = x_ref[...] + gate_ref[0] * _dot(y_ref[...], w_ref[0])


def out_proj(y_a, y_b, y_c, g, w_layers, layer, x2, gate, seq, tm=1024, tn=512):
    t = y_a.shape[0]
    wa, wb, wc = y_a.shape[1], y_b.shape[1], y_c.shape[1]
    _, kd, n = w_layers.shape
    tm, tn = _tile(seq, tm), _tile(n, tn)
    per_b = seq // tm
    ga, gb, gc = g[:wa].reshape(1, wa), g[wa:wa + wb].reshape(1, wb), g[wa + wb:].reshape(1, wc)
    rows = lambda width: pl.BlockSpec((tm, width), lambda i, j: (i, 0))
    vec = lambda width: pl.BlockSpec((1, width), lambda i, j: (0, 0))
    return pl.pallas_call(
        _out_proj_kernel,
        grid=(t // tm, n // tn),
        in_specs=[rows(wa), rows(wb), rows(wc), vec(wa), vec(wb), vec(wc),
                  pl.BlockSpec((1, kd, tn), lambda i, j: (layer, 0, j)),
                  pl.BlockSpec((tm, tn), lambda i, j: (i, j)),
                  pl.BlockSpec((1, 1, tn), lambda i, j: (i // per_b, 0, j))],
        out_specs=pl.BlockSpec((tm, tn), lambda i, j: (i, j)),
        out_shape=jax.ShapeDtypeStruct((t, n), F32),
        scratch_shapes=[pltpu.VMEM((tm, kd), BF16)],
        compiler_params=_params("parallel", "arbitrary"),
        name="out_proj",
    )(y_a, y_b, y_c, ga, gb, gc, w_layers, x2, gate)


def _expert_weight_copies(w_hbm, layer, expert, col, stage, sem):
    width = stage.shape[1]
    return pltpu.make_async_copy(w_hbm.at[layer, expert, :, pl.ds(col, width)], stage, sem)


def _ffn_up_kernel(bexp_ref, first_ref, rows_ref, next_ref, x_ref, wg_hbm, wu_hbm, a_ref,
                   wg_st, wu_st, wg_bf, wu_bf, sem, *, layer):
    j = pl.program_id(0)
    b = pl.program_id(1)
    n_rows = rows_ref[b]
    half = x_ref.shape[0] // 2
    col = pl.multiple_of(j * wg_bf.shape[1], wg_bf.shape[1])

    def copies(expert):
        return (_expert_weight_copies(wg_hbm, layer, expert, col, wg_st, sem.at[0]),
                _expert_weight_copies(wu_hbm, layer, expert, col, wu_st, sem.at[1]))

    @pl.when(b == 0)
    def _():
        for c in copies(bexp_ref[0]):
            c.start()

    @pl.when((n_rows > 0) & (first_ref[b] == 1))
    def _():
        for c in copies(bexp_ref[b]):
            c.wait()
        wg_bf[...] = wg_st[...].astype(BF16)
        wu_bf[...] = wu_st[...].astype(BF16)

        @pl.when(next_ref[b] >= 0)
        def _():
            for c in copies(next_ref[b]):
                c.start()

    def act(x):
        gate = _dot(x, wg_bf[...])
        up = _dot(x, wu_bf[...])
        return (gate * jax.nn.sigmoid(gate) * up).astype(a_ref.dtype)

    @pl.when(n_rows > half)
    def _():
        a_ref[...] = act(x_ref[...])

    @pl.when((n_rows > 0) & (n_rows <= half))
    def _():
        a_ref[:half] = act(x_ref[:half])
        a_ref[half:] = jnp.zeros((half, a_ref.shape[1]), a_ref.dtype)

    @pl.when(n_rows == 0)
    def _():
        a_ref[...] = jnp.zeros_like(a_ref)


def _pack_bf16_pair(lo, hi):
    lo_bits = lax.bitcast_convert_type(lo.astype(BF16).astype(F32), PACKED) >> 16
    hi_bits = lax.bitcast_convert_type(hi.astype(BF16).astype(F32), PACKED) & jnp.uint32(0xFFFF0000)
    return hi_bits | lo_bits


def _unpack_bf16_pair(packed):
    lo = lax.bitcast_convert_type(packed << 16, F32)
    hi = lax.bitcast_convert_type(packed & jnp.uint32(0xFFFF0000), F32)
    return lo, hi


def _ffn_down_kernel(bexp_ref, first_ref, rows_ref, next_ref, a_ref, wd_hbm, o_ref,
                     wd_st, wd_bf, sem, *, layer):
    j = pl.program_id(0)
    b = pl.program_id(1)
    n_rows = rows_ref[b]
    half = a_ref.shape[0] // 2
    col = pl.multiple_of(j * wd_bf.shape[1], wd_bf.shape[1])

    def copy(expert):
        return _expert_weight_copies(wd_hbm, layer, expert, col, wd_st, sem.at[0])

    @pl.when(b == 0)
    def _():
        copy(bexp_ref[0]).start()

    @pl.when((n_rows > 0) & (first_ref[b] == 1))
    def _():
        copy(bexp_ref[b]).wait()
        wd_bf[...] = wd_st[...].astype(BF16)

        @pl.when(next_ref[b] >= 0)
        def _():
            copy(next_ref[b]).start()

    @pl.when(n_rows > half)
    def _():
        y = _dot(a_ref[...], wd_bf[...])
        o_ref[...] = _pack_bf16_pair(y[:half], y[half:])

    @pl.when((n_rows > 0) & (n_rows <= half))
    def _():
        y = _dot(a_ref[:half], wd_bf[...])
        o_ref[...] = _pack_bf16_pair(y, jnp.zeros_like(y))

    @pl.when(n_rows == 0)
    def _():
        o_ref[...] = jnp.zeros_like(o_ref)


def expert_ffn(xs, block_exp, block_first, block_rows, block_next, w_gate, w_up, w_down, layer):
    n_slots, d = xs.shape
    ff = w_gate.shape[3]
    fh, dh = ff // FF_SPLIT, d // FF_SPLIT
    n_blocks = n_slots // MOE_BLOCK
    hbm = pl.BlockSpec(memory_space=pl.ANY)
    up_spec = pltpu.PrefetchScalarGridSpec(
        num_scalar_prefetch=4,
        grid=(FF_SPLIT, n_blocks),
        in_specs=[pl.BlockSpec((MOE_BLOCK, d), lambda j, b, *_: (b, 0)), hbm, hbm],
        out_specs=pl.BlockSpec((MOE_BLOCK, fh), lambda j, b, *_: (b, j)),
        scratch_shapes=[pltpu.VMEM((d, fh), F32), pltpu.VMEM((d, fh), F32),
                        pltpu.VMEM((d, fh), BF16), pltpu.VMEM((d, fh), BF16),
                        pltpu.SemaphoreType.DMA((2,))],
    )
    act = pl.pallas_call(
        functools.partial(_ffn_up_kernel, layer=layer),
        grid_spec=up_spec,
        out_shape=jax.ShapeDtypeStruct((n_slots, ff), BF16),
        compiler_params=_params("arbitrary", "arbitrary"),
        name="expert_ffn_up",
    )(block_exp, block_first, block_rows, block_next, xs, w_gate, w_up)
    down_spec = pltpu.PrefetchScalarGridSpec(
        num_scalar_prefetch=4,
        grid=(FF_SPLIT, n_blocks),
        in_specs=[pl.BlockSpec((MOE_BLOCK, ff), lambda j, b, *_: (b, 0)), hbm],
        out_specs=pl.BlockSpec((MOE_BLOCK // 2, dh), lambda j, b, *_: (b, j)),
        scratch_shapes=[pltpu.VMEM((ff, dh), F32), pltpu.VMEM((ff, dh), BF16),
                        pltpu.SemaphoreType.DMA((1,))],
    )
    return pl.pallas_call(
        functools.partial(_ffn_down_kernel, layer=layer),
        grid_spec=down_spec,
        out_shape=jax.ShapeDtypeStruct((n_slots // 2, d), PACKED),
        compiler_params=_params("arbitrary", "arbitrary"),
        name="expert_ffn_down",
    )(block_exp, block_first, block_rows, block_next, act, w_down)


def _start_row_copies(idx_ref, base, n_rows, src_ref, dst_ref, sem):
    def issue(r, carry):
        row = idx_ref[base + r]
        pltpu.make_async_copy(src_ref.at[pl.ds(row, 1)], dst_ref.at[pl.ds(r, 1)], sem).start()
        return carry

    lax.fori_loop(0, n_rows, issue, 0, unroll=8)


def _wait_row_copies(n_rows, src_ref, dst_ref, sem):
    pltpu.make_async_copy(src_ref.at[pl.ds(0, n_rows)], dst_ref, sem).wait()


def _gather_cast_kernel(idx_ref, nused_ref, src_ref, o_ref, buf, sem):
    b = pl.program_id(0)
    rows = o_ref.shape[0]
    n_used = nused_ref[0]

    @pl.when((b == 0) & (n_used > 0))
    def _():
        _start_row_copies(idx_ref, 0, rows, src_ref, buf.at[0], sem.at[0])

    @pl.when(b + 1 < n_used)
    def _():
        nxt = (b + 1) % 2
        _start_row_copies(idx_ref, (b + 1) * rows, rows, src_ref, buf.at[nxt], sem.at[nxt])

    @pl.when(b < n_used)
    def _():
        cur = b % 2
        _wait_row_copies(rows, src_ref, buf.at[cur], sem.at[cur])
        o_ref[...] = buf[cur].astype(o_ref.dtype)

    @pl.when(b >= n_used)
    def _():
        o_ref[...] = jnp.zeros_like(o_ref)


def gather_rows_cast(src, idx, n_used_tiles, out_dtype):
    d = src.shape[1]
    m = idx.shape[0]
    rows = GATHER_ROWS
    grid_spec = pltpu.PrefetchScalarGridSpec(
        num_scalar_prefetch=2,
        grid=(m // rows,),
        in_specs=[pl.BlockSpec(memory_space=pl.ANY)],
        out_specs=pl.BlockSpec((rows, d), lambda b, ix, nu: (b, 0)),
        scratch_shapes=[pltpu.VMEM((2, rows, d), src.dtype), pltpu.SemaphoreType.DMA((2,))],
    )
    return pl.pallas_call(
        _gather_cast_kernel,
        grid_spec=grid_spec,
        out_shape=jax.ShapeDtypeStruct((m, d), out_dtype),
        compiler_params=_params("arbitrary"),
        name="gather_rows_cast",
    )(idx, n_used_tiles, src)


def _combine_kernel(pos_ref, x_ref, gate_ref, w_ref, half_ref, g_ref, sc_ref, sh_ref, ys_ref,
                    *rest, n_tok, final):
    if final:
        o_ref, ybuf, sem = rest
    else:
        o_ref, h_ref, ybuf, sem = rest
    i = pl.program_id(0)
    n_tiles = pl.num_programs(0)
    tm = x_ref.shape[0]

    def start_tile(tile, slot):
        for kk in range(TOP_K):
            _start_row_copies(pos_ref, kk * n_tok + tile * tm, tm, ys_ref, ybuf.at[slot, kk],
                              sem.at[slot])

    @pl.when(i == 0)
    def _():
        start_tile(0, 0)

    @pl.when(i + 1 < n_tiles)
    def _():
        start_tile(i + 1, (i + 1) % 2)

    cur = i % 2
    for kk in range(TOP_K):
        _wait_row_copies(tm, ys_ref, ybuf.at[cur, kk], sem.at[cur])
    w = w_ref[...]
    upper = half_ref[...] > 0.5
    acc = None
    for kk in range(TOP_K):
        lo, hi = _unpack_bf16_pair(ybuf[cur, kk])
        term = w[:, kk:kk + 1] * jnp.where(upper[:, kk:kk + 1], hi, lo)
        acc = term if acc is None else acc + term
    x_new = x_ref[...] + gate_ref[0] * acc
    if final:
        o_ref[...] = _rms(x_new, g_ref[...])
    else:
        o_ref[...] = x_new
        h_ref[...] = (_rms(x_new, g_ref[...]) * sc_ref[0] + sh_ref[0]).astype(h_ref.dtype)


def moe_combine(x2, gate, ys, pos, w_tok, half_tok, seq, norm_g, sc1p=None, sh=None):
    t, d = x2.shape
    tm = _tile(seq, GATHER_ROWS)
    per_b = seq // tm
    final = sc1p is None
    if final:
        sc1p = sh = jnp.zeros((x2.shape[0] // seq, 1, d), F32)
    row = pl.BlockSpec((tm, d), lambda i, ps: (i, 0))
    mod = pl.BlockSpec((1, 1, d), lambda i, ps: (i // per_b, 0, 0))
    tok = pl.BlockSpec((tm, TOP_K), lambda i, ps: (i, 0))
    grid_spec = pltpu.PrefetchScalarGridSpec(
        num_scalar_prefetch=1,
        grid=(t // tm,),
        in_specs=[row, mod, tok, tok, pl.BlockSpec((1, d), lambda i, ps: (0, 0)), mod, mod,
                  pl.BlockSpec(memory_space=pl.ANY)],
        out_specs=row if final else [row, row],
        scratch_shapes=[pltpu.VMEM((2, TOP_K, tm, d), ys.dtype), pltpu.SemaphoreType.DMA((2,))],
    )
    out_shape = jax.ShapeDtypeStruct((t, d), F32)
    return pl.pallas_call(
        functools.partial(_combine_kernel, n_tok=t, final=final),
        grid_spec=grid_spec,
        out_shape=out_shape if final else [out_shape, jax.ShapeDtypeStruct((t, d), BF16)],
        compiler_params=_params("arbitrary"),
        name="moe_combine",
    )(pos, x2, gate, w_tok, half_tok, norm_g.reshape(1, d), sc1p, sh, ys)


def _group_by_expert(e_idx):
    n_tok = e_idx.shape[1]
    n_asg = n_tok * TOP_K
    flat_e = e_idx.reshape(n_asg)
    onehot = (flat_e[:, None] == jnp.arange(N_EXPERTS)[None, :]).astype(jnp.int32)
    running = jnp.cumsum(onehot, axis=0)
    rank = jnp.sum(running * onehot, axis=1) - 1
    sizes = running[-1]
    padded = (sizes + MOE_BLOCK - 1) // MOE_BLOCK * MOE_BLOCK
    pad_end = jnp.cumsum(padded)
    pad_start = pad_end - padded
    dest = (jnp.sum(pad_start[None, :] * onehot, axis=1) + rank).astype(jnp.int32)
    n_blocks = -(-n_asg // MOE_BLOCK) + N_EXPERTS
    n_slots = n_blocks * MOE_BLOCK
    slot_tok = (jnp.arange(n_slots, dtype=jnp.int32) % n_tok).at[dest].set(
        jnp.arange(n_asg, dtype=jnp.int32) % n_tok, mode="promise_in_bounds")
    block_start = jnp.arange(n_blocks, dtype=jnp.int32) * MOE_BLOCK
    block_exp = jnp.minimum(jnp.sum(block_start[:, None] >= pad_end[None, :], axis=1),
                            N_EXPERTS - 1).astype(jnp.int32)
    n_used = (pad_end[-1] // MOE_BLOCK).astype(jnp.int32)
    block_exp = jnp.where(jnp.arange(n_blocks) < n_used, block_exp, block_exp[n_used - 1])
    prev_exp = jnp.concatenate([jnp.full((1,), -1, jnp.int32), block_exp[:-1]])
    block_first = (block_exp != prev_exp).astype(jnp.int32)
    block_rows = jnp.clip(pad_start[block_exp] + sizes[block_exp] - block_start, 0, MOE_BLOCK)
    block_rows = jnp.where(jnp.arange(n_blocks) < n_used, block_rows, 0).astype(jnp.int32)
    blk = jnp.arange(n_blocks, dtype=jnp.int32)
    starts = jnp.where((block_first == 1) & (block_rows > 0), blk, n_blocks)
    later = lax.cummin(jnp.concatenate([starts[1:], jnp.full((1,), n_blocks, jnp.int32)]),
                       axis=0, reverse=True)
    block_next = jnp.where(later < n_blocks, block_exp[jnp.minimum(later, n_blocks - 1)],
                           -1).astype(jnp.int32)
    return (dest.reshape(TOP_K, n_tok), slot_tok, block_exp, block_first, block_rows, block_next,
            n_used.reshape(1))


def _rot_cols(w):
    half = w.shape[-1] // 2
    return jnp.concatenate([-w[..., half:], w[..., :half]], axis=-1)


IN_ROW_BLOCK = 64


def _split_in_weights_kernel(wt_ref, a_ref, b_ref, c_ref, *, a_blocks, b_blocks, c_blocks):
    g = pl.program_id(1)
    w = wt_ref[0]

    @pl.when(g < a_blocks)
    def _():
        a_ref[0] = w.astype(a_ref.dtype)

    @pl.when((g >= a_blocks) & (g < a_blocks + b_blocks))
    def _():
        b_ref[0] = w.astype(b_ref.dtype)

    @pl.when((g >= a_blocks + b_blocks) & (g < a_blocks + b_blocks + c_blocks))
    def _():
        c_ref[0] = w.astype(c_ref.dtype)

    @pl.when(g == a_blocks + b_blocks + c_blocks)
    def _():
        half = w.shape[0] // 2
        a_ref[0] = jnp.concatenate([-w[half:], w[:half]], axis=0).astype(a_ref.dtype)


def split_in_weights(w_in):
    layers, d, n_in = w_in.shape
    wt = jnp.swapaxes(w_in, 1, 2)
    rb = IN_ROW_BLOCK
    assert MLA_ROPE == rb
    a_blocks = (Q_LORA + KV_LORA + MLA_ROPE) // rb
    b_blocks = 3 * CONV_CH // rb
    c_blocks = n_in // rb - a_blocks - b_blocks
    last = a_blocks + b_blocks + c_blocks

    def src(l, g):
        return (l, jnp.where(g == last, a_blocks - 1, g), 0)

    def dst_a(l, g):
        return (l, jnp.where(g == last, a_blocks, jnp.minimum(g, a_blocks - 1)), 0)

    def dst_b(l, g):
        return (l, jnp.clip(g - a_blocks, 0, b_blocks - 1), 0)

    def dst_c(l, g):
        return (l, jnp.clip(g - a_blocks - b_blocks, 0, c_blocks - 1), 0)

    blk = (1, rb, d)
    return pl.pallas_call(
        functools.partial(_split_in_weights_kernel, a_blocks=a_blocks, b_blocks=b_blocks,
                          c_blocks=c_blocks),
        grid=(layers, last + 1),
        in_specs=[pl.BlockSpec(blk, src)],
        out_specs=[pl.BlockSpec(blk, dst_a), pl.BlockSpec(blk, dst_b), pl.BlockSpec(blk, dst_c)],
        out_shape=[jax.ShapeDtypeStruct((layers, nb * rb, d), BF16)
                   for nb in (a_blocks + 1, b_blocks, c_blocks)],
        compiler_params=_params("arbitrary", "arbitrary"),
        name="split_in_weights",
    )(wt)


def _prep_weights(w_in, w_uq, w_ukv, w_out):
    layers = w_in.shape[0]
    w_a, w_b, w_c = split_in_weights(w_in)
    wq = w_uq.reshape(layers, Q_LORA, MLA_HEADS, MLA_NOPE + MLA_ROPE)
    wq_rope = wq[..., MLA_NOPE:]
    wq = jnp.concatenate([wq, _rot_cols(wq_rope)], axis=-1).transpose(0, 2, 1, 3).astype(BF16)
    return dict(w_a=w_a, w_b=w_b, w_c=w_c, w_q=wq, w_kv=w_ukv.astype(BF16),
                w_out=w_out.astype(BF16))


def _rope_tables(seq, dim):
    inv = 1.0 / (ROPE_THETA ** (jnp.arange(0, dim, 2, dtype=F32) / dim))
    ang = jnp.arange(seq, dtype=F32)[:, None] * inv[None, :]
    return jnp.cos(ang), jnp.sin(ang)


def kernel(x, c, w_mod, mod_table, mix_norm_g, w_in, q_norm_g, kv_norm_g, w_uq, w_ukv, conv_w,
           group_norm_g, w_out, ffn_norm_g, w_router, router_bias, w_gate, w_up, w_down,
           final_norm_g):
    batch, seq, d = x.shape
    depth = w_in.shape[0]
    t = batch * seq
    x2 = x.reshape(t, d)

    cos_a, sin_a = _rope_tables(seq, MLA_ROPE)
    zeros_a = jnp.zeros_like(cos_a)
    cos_p = jnp.concatenate([cos_a, cos_a, zeros_a, zeros_a], axis=1)
    sin_p = jnp.concatenate([sin_a, sin_a, zeros_a, zeros_a], axis=1)
    cos_b, sin_b = _rope_tables(seq, MOBA_HD)
    cos_f = jnp.concatenate([cos_b, cos_b], axis=1)
    sin_s = jnp.concatenate([-sin_b, sin_b], axis=1)

    c_pad = jnp.zeros((SUBLANE, d), F32).at[:batch].set(c)
    mod_shared = mod_matmul(c_pad, w_mod)[:batch].reshape(batch, N_MOD, d)

    wr_t = w_router.T
    wr_hi = wr_t.astype(BF16)
    wr_lo = (wr_t - wr_hi.astype(F32)).astype(BF16)
    router = (wr_hi, wr_lo, router_bias.astype(F32).reshape(N_EXPERTS, 1))
    mla_scale = (MLA_NOPE + MLA_ROPE) ** -0.5

    mods = []
    for l in range(depth):
        mod = mod_shared + mod_table[l][None]
        mods.append([mod[:, i][:, None, :] for i in range(N_MOD)])

    p = _prep_weights(w_in, w_uq, w_ukv, w_out)
    h = norm_mod(x2, mix_norm_g[0], 1.0 + mods[0][1], mods[0][0], seq)
    for l in range(depth):
        sh1, sc1, g1, sh2, sc2, g2 = mods[l]

        proj_a = matmul(h, p["w_a"], l, F32, tm=512)
        bch = matmul(h, p["w_b"], l, F32)
        qkv = matmul(h, p["w_c"], l, F32)

        q_a = mla_q_up(proj_a, q_norm_g[l], p["w_q"], l, cos_p, sin_p, seq, mla_scale)
        k_a, v_a = mla_kv_up(proj_a, kv_norm_g[l], p["w_kv"], l, cos_p, sin_p, seq)
        y_a = mla_attention(q_a, k_a, v_a, batch, seq, BF16)

        y_b = conv_mixer(bch, conv_w[l], seq, F32)

        q_c, k_c, v_c, k_mean = moba_prep(qkv, cos_f, sin_s, seq)
        n_blk = seq // MOBA_BLOCK
        k_mean = k_mean.reshape(batch, n_blk, MOBA_HEADS, MOBA_HD).transpose(0, 2, 1, 3)
        y_c = moba_attention(q_c, k_c, v_c, k_mean, batch, seq, BF16)

        x2 = out_proj(y_a, y_b, y_c, group_norm_g[l], p["w_out"], l, x2, g1, seq)

        h2, e_idx, gates = norm_mod(x2, ffn_norm_g[l], 1.0 + sc2, sh2, seq, out_dtype=F32,
                                    router=router)
        pos, slot_tok, block_exp, block_first, block_rows, block_next, n_used = _group_by_expert(e_idx)
        xs = gather_rows_cast(h2, slot_tok, n_used * (MOE_BLOCK // GATHER_ROWS), BF16)
        ys = expert_ffn(xs, block_exp, block_first, block_rows, block_next, w_gate, w_up, w_down,
                        l)
        sub = pos % MOE_BLOCK
        ys_row = ((pos // MOE_BLOCK) * (MOE_BLOCK // 2) + sub % (MOE_BLOCK // 2)).reshape(-1)
        ys_half = (sub // (MOE_BLOCK // 2)).astype(F32).T
        if l + 1 < depth:
            nxt = mods[l + 1]
            x2, h = moe_combine(x2, g2, ys, ys_row, gates.T, ys_half, seq, mix_norm_g[l + 1],
                                1.0 + nxt[1], nxt[0])
        else:
            out = moe_combine(x2, g2, ys, ys_row, gates.T, ys_half, seq, final_norm_g)

    return out.reshape(batch, seq, d)
```

```python
import functools

import jax
import jax.numpy as jnp
from jax import lax
from jax.experimental import pallas as pl
from jax.experimental.pallas import tpu as pltpu

MLA_HEADS = 16
MLA_NOPE = 128
MLA_ROPE = 64
MLA_V = 128
Q_LORA = 1024
KV_LORA = 512
MLA_DK_PAD = 256
CONV_CH = 1024
CONV_W = 3
MOBA_HEADS = 8
MOBA_HD = 128
MOBA_BLOCK = 256
MOBA_TOPK = 3
ROPE_THETA = 10000.0
EPS = 1e-6
N_MOD = 6
N_EXPERTS = 16
N_GROUPS = 4
TOP_K = 2
MOE_BLOCK = 512
GATHER_ROWS = 256
FF_SPLIT = 2
UP_HEADS_PER_STEP = 4
ATTN_HEADS_PER_STEP = 4
ATTN_TILE = 512
ONES_ROWS = 16

V7X_VMEM_LIMIT_BYTES = 56 * 1024 * 1024
LANE = 128
SUBLANE = 8

F32 = jnp.float32
BF16 = jnp.bfloat16
PACKED = jnp.uint32
NEG_INF = float("-inf")


def _params(*sem):
    return pltpu.CompilerParams(dimension_semantics=sem, vmem_limit_bytes=V7X_VMEM_LIMIT_BYTES)


def _dot(a, b, precision=None):
    return jnp.dot(a, b, preferred_element_type=F32, precision=precision)


def _dot_nt(a, b):
    return lax.dot_general(a, b, (((1,), (1,)), ((), ())), preferred_element_type=F32)


def _tile(dim, want):
    return want if dim % want == 0 else dim


def _mod_kernel(c_ref, w0_ref, w1_ref, o_ref):
    @pl.when(pl.program_id(0) == 0)
    def _():
        o_ref[...] = jnp.zeros_like(o_ref)

    c = c_ref[...]
    a = (c * jax.nn.sigmoid(c)).astype(BF16)
    half = w0_ref.shape[1]
    o_ref[:, :half] += _dot(a, w0_ref[...].astype(BF16))
    o_ref[:, half:] += _dot(a, w1_ref[...].astype(BF16))


def mod_matmul(c_pad, w_mod):
    rows, d = c_pad.shape
    n = w_mod.shape[1]
    tk = _tile(d, LANE)
    return pl.pallas_call(
        _mod_kernel,
        grid=(d // tk,),
        in_specs=[pl.BlockSpec((rows, tk), lambda k: (0, k)),
                  pl.BlockSpec((tk, n // 2), lambda k: (k, 0)),
                  pl.BlockSpec((tk, n // 2), lambda k: (k, 1))],
        out_specs=pl.BlockSpec((rows, n), lambda k: (0, 0)),
        out_shape=jax.ShapeDtypeStruct((rows, n), F32),
        compiler_params=_params("arbitrary"),
        name="mod_matmul",
    )(c_pad, w_mod, w_mod)


def _rms(x, g):
    return x * lax.rsqrt(jnp.mean(x * x, axis=-1, keepdims=True) + EPS) * g


def _pack_bf16_pair(lo, hi):
    lo_bits = lax.bitcast_convert_type(lo.astype(BF16).astype(F32), PACKED) >> 16
    hi_bits = lax.bitcast_convert_type(hi.astype(BF16).astype(F32), PACKED) & jnp.uint32(0xFFFF0000)
    return hi_bits | lo_bits


def _unpack_bf16_pair(packed):
    lo = lax.bitcast_convert_type(packed << 16, F32)
    hi = lax.bitcast_convert_type(packed & jnp.uint32(0xFFFF0000), F32)
    return lo, hi


def _norm_mod_kernel(x_ref, g_ref, sc_ref, sh_ref, o_ref):
    y = _rms(x_ref[...], g_ref[...])
    o_ref[...] = (y * sc_ref[0] + sh_ref[0]).astype(o_ref.dtype)


def _top2_sum(a, b, c, d):
    hi1, lo1 = jnp.maximum(a, b), jnp.minimum(a, b)
    hi2, lo2 = jnp.maximum(c, d), jnp.minimum(c, d)
    return jnp.maximum(hi1, hi2) + jnp.maximum(jnp.minimum(hi1, hi2), jnp.maximum(lo1, lo2))


def _norm_router_kernel(x_ref, g_ref, sc_ref, sh_ref, whi_ref, wlo_ref, bias_ref,
                        o_ref, e_ref, gate_ref):
    h = _rms(x_ref[...], g_ref[...]) * sc_ref[0] + sh_ref[0]
    hi = h.astype(BF16)
    o_ref[...] = _pack_bf16_pair(h[:, :h.shape[1] // 2], h[:, h.shape[1] // 2:])
    lo = (h - hi.astype(F32)).astype(BF16)
    whl = jnp.concatenate([whi_ref[...], wlo_ref[...]], axis=0)
    both = _dot_nt(whl, hi)
    n_e = whi_ref.shape[0]
    logits = both[:n_e] + both[n_e:] + _dot_nt(whi_ref[...], lo)
    scores = jax.nn.sigmoid(logits)
    biased = scores + bias_ref[...]
    per_grp = N_EXPERTS // N_GROUPS
    b_rows = [biased[e:e + 1, :] for e in range(N_EXPERTS)]
    s_rows = [scores[e:e + 1, :] for e in range(N_EXPERTS)]
    grp_scores = [_top2_sum(*b_rows[g * per_grp:(g + 1) * per_grp]) for g in range(N_GROUPS)]
    best, grp = grp_scores[0], jnp.zeros(grp_scores[0].shape, jnp.int32)
    for g in range(1, N_GROUPS):
        better = grp_scores[g] > best
        grp = jnp.where(better, g, grp)
        best = jnp.where(better, grp_scores[g], best)
    b4, s4 = b_rows[:per_grp], s_rows[:per_grp]
    for g in range(1, N_GROUPS):
        in_g = grp == g
        b4 = [jnp.where(in_g, b_rows[g * per_grp + j], b4[j]) for j in range(per_grp)]
        s4 = [jnp.where(in_g, s_rows[g * per_grp + j], s4[j]) for j in range(per_grp)]
    v1, i1 = b4[0], jnp.zeros(grp.shape, jnp.int32)
    for j in range(1, per_grp):
        better = b4[j] > v1
        i1 = jnp.where(better, j, i1)
        v1 = jnp.where(better, b4[j], v1)
    v2, i2 = jnp.full(v1.shape, NEG_INF, F32), jnp.zeros(grp.shape, jnp.int32)
    for j in range(per_grp):
        better = (i1 != j) & (b4[j] > v2)
        i2 = jnp.where(better, j, i2)
        v2 = jnp.where(better, b4[j], v2)
    g1, g2 = s4[0], s4[0]
    for j in range(1, per_grp):
        g1 = jnp.where(i1 == j, s4[j], g1)
        g2 = jnp.where(i2 == j, s4[j], g2)
    total = g1 + g2
    e_ref[...] = jnp.concatenate([grp * per_grp + i1, grp * per_grp + i2], axis=0)
    gate_ref[...] = jnp.concatenate([g1 / total, g2 / total], axis=0)


def norm_mod(x2, g, sc1p, sh, seq, out_dtype=BF16, router=None):
    t, d = x2.shape
    tm = _tile(seq, 256)
    per_b = seq // tm
    row = pl.BlockSpec((tm, d), lambda i: (i, 0))
    vec = pl.BlockSpec((1, d), lambda i: (0, 0))
    mod = pl.BlockSpec((1, 1, d), lambda i: (i // per_b, 0, 0))
    g2 = g.reshape(1, d)
    if router is None:
        return pl.pallas_call(
            _norm_mod_kernel, grid=(t // tm,), in_specs=[row, vec, mod, mod], out_specs=row,
            out_shape=jax.ShapeDtypeStruct((t, d), out_dtype),
            compiler_params=_params("arbitrary"), name="norm_mod")(x2, g2, sc1p, sh)
    whi_t, wlo_t, bias = router
    ne = whi_t.shape[0]
    wspec = pl.BlockSpec((ne, d), lambda i: (0, 0))
    kspec = pl.BlockSpec((TOP_K, tm), lambda i: (0, i))
    return pl.pallas_call(
        _norm_router_kernel, grid=(t // tm,),
        in_specs=[row, vec, mod, mod, wspec, wspec, pl.BlockSpec((ne, 1), lambda i: (0, 0))],
        out_specs=[pl.BlockSpec((tm, d // 2), lambda i: (i, 0)), kspec, kspec],
        out_shape=[jax.ShapeDtypeStruct((t, d // 2), PACKED),
                   jax.ShapeDtypeStruct((TOP_K, t), jnp.int32),
                   jax.ShapeDtypeStruct((TOP_K, t), F32)],
        compiler_params=_params("arbitrary"), name="norm_router")(x2, g2, sc1p, sh, whi_t, wlo_t, bias)


def _mm_kernel(a_ref, wt_ref, o_ref):
    o_ref[...] = _dot_nt(a_ref[...], wt_ref[0]).astype(o_ref.dtype)


def matmul(a, wt_layers, layer, out_dtype, tm=1024, tn=1024):
    m, kd = a.shape
    n = wt_layers.shape[1]
    tm, tn = _tile(m, tm), _tile(n, tn)
    return pl.pallas_call(
        _mm_kernel,
        grid=(m // tm, n // tn),
        in_specs=[pl.BlockSpec((tm, kd), lambda i, j: (i, 0)),
                  pl.BlockSpec((1, tn, kd), lambda i, j: (layer, j, 0))],
        out_specs=pl.BlockSpec((tm, tn), lambda i, j: (i, j)),
        out_shape=jax.ShapeDtypeStruct((m, n), out_dtype),
        compiler_params=_params("parallel", "parallel"),
        name="matmul",
    )(a, wt_layers)


def _rope_pair(r2, cos_ref, sin_ref):
    return r2 * cos_ref[...] + pltpu.roll(r2, MLA_ROPE, 1) * sin_ref[...]


def _with_ones_rows(vt):
    return jnp.concatenate([vt, jnp.ones((ONES_ROWS, vt.shape[1]), vt.dtype)], axis=0)


def _qup_kernel(cq_ref, g_ref, w_ref, cos_ref, sin_ref, o_ref, an_ref, *, scale):
    @pl.when(pl.program_id(1) == 0)
    def _():
        an_ref[...] = _rms(cq_ref[...], g_ref[...]).astype(BF16)

    an = an_ref[...]
    for hh in range(w_ref.shape[1]):
        res = _dot(an, w_ref[0, hh])
        roped = _rope_pair(res[:, MLA_NOPE:], cos_ref, sin_ref)
        q = jnp.concatenate([res[:, :MLA_NOPE], roped], axis=1) * scale
        o_ref[hh, 0] = q.T.astype(o_ref.dtype)


def mla_q_up(proj_a, g, w_q, layer, cos_p, sin_p, seq, scale):
    t = proj_a.shape[0]
    heads = w_q.shape[1]
    hb = UP_HEADS_PER_STEP
    tm = _tile(seq, ATTN_TILE)
    per_b = seq // tm
    return pl.pallas_call(
        functools.partial(_qup_kernel, scale=scale),
        grid=(t // tm, heads // hb),
        in_specs=[pl.BlockSpec((tm, Q_LORA), lambda i, h: (i, 0)),
                  pl.BlockSpec((1, Q_LORA), lambda i, h: (0, 0)),
                  pl.BlockSpec((1, hb, Q_LORA, MLA_DK_PAD), lambda i, h: (layer, h, 0, 0)),
                  pl.BlockSpec((tm, LANE), lambda i, h: (i % per_b, 0)),
                  pl.BlockSpec((tm, LANE), lambda i, h: (i % per_b, 0))],
        out_specs=pl.BlockSpec((hb, 1, MLA_DK_PAD, tm), lambda i, h: (h, i, 0, 0)),
        out_shape=jax.ShapeDtypeStruct((heads, t // tm, MLA_DK_PAD, tm), BF16),
        scratch_shapes=[pltpu.VMEM((tm, Q_LORA), BF16)],
        compiler_params=_params("parallel", "arbitrary"),
        name="mla_q_up",
    )(proj_a, g.reshape(1, Q_LORA), w_q, cos_p, sin_p)


def _kvup_kernel(ckv_ref, g_ref, w_ref, kpe_ref, cos_ref, sin_ref, k_ref, vt_ref, an_ref):
    @pl.when(pl.program_id(1) == 0)
    def _():
        an_ref[...] = _rms(ckv_ref[...], g_ref[...]).astype(BF16)

    an = an_ref[...]
    k_rot = _rope_pair(kpe_ref[...], cos_ref, sin_ref)
    hw = MLA_NOPE + MLA_V
    for hh in range(k_ref.shape[0]):
        res = _dot(an, w_ref[0, :, hh * hw:(hh + 1) * hw])
        k_ref[hh] = jnp.concatenate([res[:, :MLA_NOPE], k_rot], axis=1).astype(k_ref.dtype)
        vt_ref[hh, 0] = _with_ones_rows(res[:, MLA_NOPE:].T).astype(vt_ref.dtype)


def mla_kv_up(proj_a, g, w_kv, layer, cos_p, sin_p, seq):
    t = proj_a.shape[0]
    hw = MLA_NOPE + MLA_V
    heads = w_kv.shape[2] // hw
    hb = UP_HEADS_PER_STEP
    tm = _tile(seq, ATTN_TILE)
    per_b = seq // tm
    return pl.pallas_call(
        _kvup_kernel,
        grid=(t // tm, heads // hb),
        in_specs=[pl.BlockSpec((tm, KV_LORA), lambda i, h: (i, Q_LORA // KV_LORA)),
                  pl.BlockSpec((1, KV_LORA), lambda i, h: (0, 0)),
                  pl.BlockSpec((1, KV_LORA, hb * hw), lambda i, h: (layer, 0, h)),
                  pl.BlockSpec((tm, LANE), lambda i, h: (i, (Q_LORA + KV_LORA) // LANE)),
                  pl.BlockSpec((tm, LANE), lambda i, h: (i % per_b, 0)),
                  pl.BlockSpec((tm, LANE), lambda i, h: (i % per_b, 0))],
        out_specs=[pl.BlockSpec((hb, tm, MLA_DK_PAD), lambda i, h: (h, i, 0)),
                   pl.BlockSpec((hb, 1, MLA_V + ONES_ROWS, tm), lambda i, h: (h, i, 0, 0))],
        out_shape=[jax.ShapeDtypeStruct((heads, t, MLA_DK_PAD), BF16),
                   jax.ShapeDtypeStruct((heads, t // tm, MLA_V + ONES_ROWS, tm), BF16)],
        scratch_shapes=[pltpu.VMEM((tm, KV_LORA), BF16)],
        compiler_params=_params("parallel", "arbitrary"),
        name="mla_kv_up",
    )(proj_a, g.reshape(1, KV_LORA), w_kv, proj_a, cos_p, sin_p)


def _pv(vts, p):
    rows = p.shape[0] // len(vts)
    out = _dot(vts[0], p[:rows].astype(vts[0].dtype))
    for n in range(1, len(vts)):
        out = out + _dot(vts[n], p[n * rows:(n + 1) * rows].astype(vts[n].dtype))
    return out


def _softmax_first(s, vts, m_ref, acc_ref):
    m = jnp.max(s, axis=0, keepdims=True)
    m_ref[...] = m
    acc_ref[...] = _pv(vts, jnp.exp(s - m))


def _softmax_step(s, vts, m_ref, acc_ref):
    m_prev = m_ref[...]
    m_new = jnp.maximum(m_prev, jnp.max(s, axis=0, keepdims=True))
    alpha = jnp.exp(m_prev - m_new)
    acc_ref[...] = alpha * acc_ref[...] + _pv(vts, jnp.exp(s - m_new))
    m_ref[...] = m_new


def _softmax_finish(acc_ref, dv):
    acc = acc_ref[...]
    return (acc[:dv, :] / acc[dv:dv + 1, :]).T


def _causal_mask_t(s):
    key = lax.broadcasted_iota(jnp.int32, s.shape, 0)
    qry = lax.broadcasted_iota(jnp.int32, s.shape, 1)
    return jnp.where(key <= qry, s, NEG_INF)


def _mla_attn_kernel(qt_ref, k_ref, vt_ref, o_ref, m_ref, acc_ref, *, tq, dv):
    i = pl.program_id(2)
    n_h = qt_ref.shape[0]
    qts = [qt_ref[h, 0] for h in range(n_h)]

    def scores(h, j, n):
        off = pl.multiple_of(j * tq, tq)
        return _dot(k_ref[h, pl.ds(off, n * tq), :], qts[h])

    for h in range(n_h):
        _softmax_first(_causal_mask_t(scores(h, i, 1)), [vt_ref[h, i]], m_ref.at[h], acc_ref.at[h])

    @pl.when(i % 2 == 1)
    def _():
        for h in range(n_h):
            _softmax_step(scores(h, i - 1, 1), [vt_ref[h, i - 1]], m_ref.at[h], acc_ref.at[h])

    def body(c, carry):
        for h in range(n_h):
            _softmax_step(scores(h, 2 * c, 2), [vt_ref[h, 2 * c], vt_ref[h, 2 * c + 1]],
                          m_ref.at[h], acc_ref.at[h])
        return carry

    lax.fori_loop(0, i // 2, body, 0)
    for h in range(n_h):
        o_ref[:, h * dv:(h + 1) * dv] = _softmax_finish(acc_ref.at[h], dv).astype(o_ref.dtype)


def mla_attention(qt, k, vt, batch, seq, out_dtype):
    heads, _, dk, tq = qt.shape
    t = k.shape[1]
    dvx = vt.shape[2]
    dv = dvx - ONES_ROWS
    nq = seq // tq
    hb = ATTN_HEADS_PER_STEP
    return pl.pallas_call(
        functools.partial(_mla_attn_kernel, tq=tq, dv=dv),
        grid=(batch, heads // hb, nq),
        in_specs=[pl.BlockSpec((hb, 1, dk, tq), lambda b, h, i: (h, b * nq + i, 0, 0)),
                  pl.BlockSpec((hb, seq, dk), lambda b, h, i: (h, b, 0)),
                  pl.BlockSpec((hb, nq, dvx, tq), lambda b, h, i: (h, b, 0, 0))],
        out_specs=pl.BlockSpec((tq, hb * dv), lambda b, h, i: (b * nq + i, h)),
        out_shape=jax.ShapeDtypeStruct((t, heads * dv), out_dtype),
        scratch_shapes=[pltpu.VMEM((hb, 1, tq), F32), pltpu.VMEM((hb, dvx, tq), F32)],
        compiler_params=_params("parallel", "parallel", "arbitrary"),
        name="mla_attention",
    )(qt, k, vt)


def _conv_kernel(b_ref, c_ref, h_ref, w_ref, o_ref, carry_ref, *, per_b):
    i = pl.program_id(0)

    @pl.when(i % per_b == 0)
    def _():
        carry_ref[...] = jnp.zeros_like(carry_ref)

    tm = o_ref.shape[0]
    w0, w1, w2 = w_ref[0:1, :], w_ref[1:2, :], w_ref[2:3, :]
    u = c_ref[...] * h_ref[...]
    y = w0 * pltpu.roll(u, 2, 0) + w1 * pltpu.roll(u, 1, 0) + w2 * u
    o_ref[...] = (b_ref[...] * y).astype(o_ref.dtype)
    u8 = u[0:SUBLANE, :]
    tail = carry_ref[...]
    r8 = lax.broadcasted_iota(jnp.int32, u8.shape, 0)
    p1 = jnp.where(r8 < 1, pltpu.roll(tail, 1, 0), pltpu.roll(u8, 1, 0))
    p2 = jnp.where(r8 < 2, pltpu.roll(tail, 2, 0), pltpu.roll(u8, 2, 0))
    y8 = w0 * p2 + w1 * p1 + w2 * u8
    o_ref[0:SUBLANE, :] = (b_ref[0:SUBLANE, :] * y8).astype(o_ref.dtype)
    carry_ref[...] = u[tm - SUBLANE:tm, :]


def conv_mixer(bch, conv_w, seq, out_dtype):
    t = bch.shape[0]
    ch = conv_w.shape[1]
    tm = _tile(seq, 256)
    per_b = seq // tm
    return pl.pallas_call(
        functools.partial(_conv_kernel, per_b=per_b),
        grid=(t // tm,),
        in_specs=[pl.BlockSpec((tm, ch), lambda i: (i, 0)),
                  pl.BlockSpec((tm, ch), lambda i: (i, 1)),
                  pl.BlockSpec((tm, ch), lambda i: (i, 2)),
                  pl.BlockSpec((CONV_W, ch), lambda i: (0, 0))],
        out_specs=pl.BlockSpec((tm, ch), lambda i: (i, 0)),
        out_shape=jax.ShapeDtypeStruct((t, ch), out_dtype),
        scratch_shapes=[pltpu.VMEM((SUBLANE, ch), F32)],
        compiler_params=_params("arbitrary"),
        name="conv_mixer",
    )(bch, bch, bch, conv_w)


def _moba_prep_kernel(q_ref, k_ref, v_ref, cos_ref, sin_ref, qo_ref, ko_ref, vo_ref, km_ref):
    heads = ko_ref.shape[0]
    cos, sin = cos_ref[...], sin_ref[...]
    n_sub = km_ref.shape[0]
    means = [[] for _ in range(n_sub)]
    for h in range(heads):
        sl = slice(h * MOBA_HD, (h + 1) * MOBA_HD)
        qh = q_ref[:, sl]
        kh = k_ref[:, sl]
        qr = qh * cos + pltpu.roll(qh, MOBA_HD // 2, 1) * sin
        qo_ref[h, 0] = qr.T.astype(qo_ref.dtype)
        kr = kh * cos + pltpu.roll(kh, MOBA_HD // 2, 1) * sin
        ko_ref[h] = kr.astype(ko_ref.dtype)
        vo_ref[h, 0] = _with_ones_rows(v_ref[:, sl].T).astype(vo_ref.dtype)
        for s in range(n_sub):
            means[s].append(jnp.mean(kr[s * MOBA_BLOCK:(s + 1) * MOBA_BLOCK], axis=0, keepdims=True))
    for s in range(n_sub):
        km_ref[s] = jnp.concatenate(means[s], axis=0)


def moba_prep(qkv, cos_f, sin_s, seq):
    t = qkv.shape[0]
    width = qkv.shape[1] // 3
    heads = width // MOBA_HD
    tm = 2 * MOBA_BLOCK
    assert seq % tm == 0
    per_b = seq // tm
    nt = t // tm
    return pl.pallas_call(
        _moba_prep_kernel,
        grid=(nt,),
        in_specs=[pl.BlockSpec((tm, width), lambda i: (i, 0)),
                  pl.BlockSpec((tm, width), lambda i: (i, 1)),
                  pl.BlockSpec((tm, width), lambda i: (i, 2)),
                  pl.BlockSpec((tm, MOBA_HD), lambda i: (i % per_b, 0)),
                  pl.BlockSpec((tm, MOBA_HD), lambda i: (i % per_b, 0))],
        out_specs=[pl.BlockSpec((heads, 1, MOBA_HD, tm), lambda i: (0, i, 0, 0)),
                   pl.BlockSpec((heads, tm, MOBA_HD), lambda i: (0, i, 0)),
                   pl.BlockSpec((heads, 1, MOBA_HD + ONES_ROWS, tm), lambda i: (0, i, 0, 0)),
                   pl.BlockSpec((2, heads, MOBA_HD), lambda i: (i, 0, 0))],
        out_shape=[jax.ShapeDtypeStruct((heads, nt, MOBA_HD, tm), BF16),
                   jax.ShapeDtypeStruct((heads, t, MOBA_HD), BF16),
                   jax.ShapeDtypeStruct((heads, nt, MOBA_HD + ONES_ROWS, tm), BF16),
                   jax.ShapeDtypeStruct((2 * nt, heads, MOBA_HD), F32)],
        compiler_params=_params("arbitrary"),
        name="moba_prep",
    )(qkv, qkv, qkv, cos_f, sin_s)


def _moba_attn_kernel(qt_ref, k_ref, vt_ref, km_ref, o_ref, m_ref, acc_ref, sel_ref, *, scale, dv):
    i = pl.program_id(2)
    n_h = qt_ref.shape[0]
    n_blk = km_ref.shape[2]
    tq = qt_ref.shape[3]
    blk_w = MOBA_BLOCK
    blk = lax.broadcasted_iota(jnp.int32, (n_blk, tq), 0)
    own = 2 * i + (lax.broadcasted_iota(jnp.int32, (n_blk, tq), 1) >= blk_w).astype(jnp.int32)
    past = blk < own
    qts = []
    for h in range(n_h):
        qtf = qt_ref[h, 0].astype(F32)
        g = _dot(km_ref[0, h], qtf, precision=lax.Precision.HIGHEST)
        g = jnp.where(past, g, NEG_INF)
        sel = jnp.zeros(g.shape, F32)
        for _ in range(min(MOBA_TOPK, n_blk)):
            mx = jnp.max(g, axis=0, keepdims=True)
            first = jnp.min(jnp.where(g == mx, blk, n_blk), axis=0, keepdims=True)
            hit = blk == first
            sel = jnp.where(hit, 1.0, sel)
            g = jnp.where(hit, NEG_INF, g)
        sel_ref[h] = jnp.where(past, sel, 0.0)
        qts.append((qtf * scale).astype(k_ref.dtype))

    def scores(h, p, n):
        off = pl.multiple_of(p * tq, tq)
        return _dot(k_ref[h, pl.ds(off, n * tq), :], qts[h])

    def chosen(h, n):
        return sel_ref[h, pl.ds(n, 1), :] > 0.5

    def mask_past(h, s, first_blk):
        parts = [jnp.where(chosen(h, first_blk + n), s[n * blk_w:(n + 1) * blk_w], NEG_INF)
                 for n in range(s.shape[0] // blk_w)]
        return jnp.concatenate(parts, axis=0)

    key = lax.broadcasted_iota(jnp.int32, (blk_w, tq), 0)
    qry = lax.broadcasted_iota(jnp.int32, (blk_w, tq), 1)
    bot_ok = key <= qry - blk_w
    for h in range(n_h):
        s = scores(h, i, 1)
        chosen_lim = jnp.where(chosen(h, 2 * i), blk_w, -1)
        top_ok = key <= jnp.where(qry < blk_w, qry, chosen_lim)
        s = jnp.concatenate([jnp.where(top_ok, s[:blk_w], NEG_INF),
                             jnp.where(bot_ok, s[blk_w:], NEG_INF)], axis=0)
        _softmax_first(s, [vt_ref[h, i]], m_ref.at[h], acc_ref.at[h])

    @pl.when(i % 2 == 1)
    def _():
        for h in range(n_h):
            s = mask_past(h, scores(h, i - 1, 1), 2 * (i - 1))
            _softmax_step(s, [vt_ref[h, i - 1]], m_ref.at[h], acc_ref.at[h])

    def body(c, carry):
        for h in range(n_h):
            s = mask_past(h, scores(h, 2 * c, 2), 4 * c)
            _softmax_step(s, [vt_ref[h, 2 * c], vt_ref[h, 2 * c + 1]], m_ref.at[h], acc_ref.at[h])
        return carry

    lax.fori_loop(0, i // 2, body, 0)
    for h in range(n_h):
        o_ref[:, h * dv:(h + 1) * dv] = _softmax_finish(acc_ref.at[h], dv).astype(o_ref.dtype)


def moba_attention(qt, k, vt, k_mean, batch, seq, out_dtype):
    heads, _, hd, tq = qt.shape
    t = k.shape[1]
    dvx = vt.shape[2]
    dv = dvx - ONES_ROWS
    n_blk = seq // MOBA_BLOCK
    nq = seq // tq
    hb = ATTN_HEADS_PER_STEP
    return pl.pallas_call(
        functools.partial(_moba_attn_kernel, scale=hd ** -0.5, dv=dv),
        grid=(batch, heads // hb, nq),
        in_specs=[pl.BlockSpec((hb, 1, hd, tq), lambda b, h, i: (h, b * nq + i, 0, 0)),
                  pl.BlockSpec((hb, seq, hd), lambda b, h, i: (h, b, 0)),
                  pl.BlockSpec((hb, nq, dvx, tq), lambda b, h, i: (h, b, 0, 0)),
                  pl.BlockSpec((1, hb, n_blk, hd), lambda b, h, i: (b, h, 0, 0))],
        out_specs=pl.BlockSpec((tq, hb * dv), lambda b, h, i: (b * nq + i, h)),
        out_shape=jax.ShapeDtypeStruct((t, heads * dv), out_dtype),
        scratch_shapes=[pltpu.VMEM((hb, 1, tq), F32), pltpu.VMEM((hb, dvx, tq), F32),
                        pltpu.VMEM((hb, n_blk, tq), F32)],
        compiler_params=_params("parallel", "parallel", "arbitrary"),
        name="moba_attention",
    )(qt, k, vt, k_mean)


def _out_proj_kernel(a_ref, b_ref, c_ref, ga_ref, gb_ref, gc_ref, w_ref, x_ref, gate_ref, o_ref,
                     y_ref):
    @pl.when(pl.program_id(1) == 0)
    def _():
        wa, wb = a_ref.shape[1], b_ref.shape[1]
        y_ref[:, 0:wa] = _rms(a_ref[...].astype(F32), ga_ref[...]).astype(y_ref.dtype)
        y_ref[:, wa:wa + wb] = _rms(b_ref[...].astype(F32), gb_ref[...]).astype(y_ref.dtype)
        y_ref[:, wa + wb:] = _rms(c_ref[...].astype(F32), gc_ref[...]).astype(y_ref.dtype)

    o_ref[...] = x_ref[...] + gate_ref[0] * _dot(y_ref[...], w_ref[0])


def out_proj(y_a, y_b, y_c, g, w_layers, layer, x2, gate, seq, tm=1024, tn=512):
    t = y_a.shape[0]
    wa, wb, wc = y_a.shape[1], y_b.shape[1], y_c.shape[1]
    _, kd, n = w_layers.shape
    tm, tn = _tile(seq, tm), _tile(n, tn)
    per_b = seq // tm
    ga, gb, gc = g[:wa].reshape(1, wa), g[wa:wa + wb].reshape(1, wb), g[wa + wb:].reshape(1, wc)
    rows = lambda width: pl.BlockSpec((tm, width), lambda i, j: (i, 0))
    vec = lambda width: pl.BlockSpec((1, width), lambda i, j: (0, 0))
    return pl.pallas_call(
        _out_proj_kernel,
        grid=(t // tm, n // tn),
        in_specs=[rows(wa), rows(wb), rows(wc), vec(wa), vec(wb), vec(wc),
                  pl.BlockSpec((1, kd, tn), lambda i, j: (layer, 0, j)),
                  pl.BlockSpec((tm, tn), lambda i, j: (i, j)),
                  pl.BlockSpec((1, 1, tn), lambda i, j: (i // per_b, 0, j))],
        out_specs=pl.BlockSpec((tm, tn), lambda i, j: (i, j)),
        out_shape=jax.ShapeDtypeStruct((t, n), F32),
        scratch_shapes=[pltpu.VMEM((tm, kd), BF16)],
        compiler_params=_params("parallel", "arbitrary"),
        name="out_proj",
    )(y_a, y_b, y_c, ga, gb, gc, w_layers, x2, gate)


def _expert_weight_copies(w_hbm, layer, expert, col, stage, sem):
    width = stage.shape[1]
    return pltpu.make_async_copy(w_hbm.at[layer, expert, :, pl.ds(col, width)], stage, sem)


def _ffn_up_kernel(bexp_ref, first_ref, rows_ref, next_ref, x_ref, wg_hbm, wu_hbm, a_ref,
                   wg_st, wu_st, wg_bf, wu_bf, sem, *, layer):
    j = pl.program_id(0)
    b = pl.program_id(1)
    n_rows = rows_ref[b]
    half = x_ref.shape[0] // 2
    col = pl.multiple_of(j * wg_bf.shape[1], wg_bf.shape[1])

    def copies(expert):
        return (_expert_weight_copies(wg_hbm, layer, expert, col, wg_st, sem.at[0]),
                _expert_weight_copies(wu_hbm, layer, expert, col, wu_st, sem.at[1]))

    @pl.when(b == 0)
    def _():
        for c in copies(bexp_ref[0]):
            c.start()

    @pl.when((n_rows > 0) & (first_ref[b] == 1))
    def _():
        for c in copies(bexp_ref[b]):
            c.wait()
        wg_bf[...] = wg_st[...].astype(BF16)
        wu_bf[...] = wu_st[...].astype(BF16)

        @pl.when(next_ref[b] >= 0)
        def _():
            for c in copies(next_ref[b]):
                c.start()

    def act(x):
        gate = _dot(x, wg_bf[...])
        up = _dot(x, wu_bf[...])
        return (gate * jax.nn.sigmoid(gate) * up).astype(a_ref.dtype)

    @pl.when(n_rows > half)
    def _():
        a_ref[...] = act(x_ref[...])

    @pl.when((n_rows > 0) & (n_rows <= half))
    def _():
        a_ref[:half] = act(x_ref[:half])
        a_ref[half:] = jnp.zeros((half, a_ref.shape[1]), a_ref.dtype)

    @pl.when(n_rows == 0)
    def _():
        a_ref[...] = jnp.zeros_like(a_ref)


def _ffn_down_kernel(bexp_ref, first_ref, rows_ref, next_ref, a_ref, wd_hbm, o_ref,
                     wd_st, wd_bf, sem, *, layer):
    j = pl.program_id(0)
    b = pl.program_id(1)
    n_rows = rows_ref[b]
    half = a_ref.shape[0] // 2
    col = pl.multiple_of(j * wd_bf.shape[1], wd_bf.shape[1])

    def copy(expert):
        return _expert_weight_copies(wd_hbm, layer, expert, col, wd_st, sem.at[0])

    @pl.when(b == 0)
    def _():
        copy(bexp_ref[0]).start()

    @pl.when((n_rows > 0) & (first_ref[b] == 1))
    def _():
        copy(bexp_ref[b]).wait()
        wd_bf[...] = wd_st[...].astype(BF16)

        @pl.when(next_ref[b] >= 0)
        def _():
            copy(next_ref[b]).start()

    def packed(y):
        mid = y.shape[1] // 2
        return _pack_bf16_pair(y[:, :mid], y[:, mid:])

    @pl.when(n_rows > half)
    def _():
        o_ref[...] = packed(_dot(a_ref[...], wd_bf[...]))

    @pl.when((n_rows > 0) & (n_rows <= half))
    def _():
        o_ref[:half] = packed(_dot(a_ref[:half], wd_bf[...]))
        o_ref[half:] = jnp.zeros((half, o_ref.shape[1]), o_ref.dtype)

    @pl.when(n_rows == 0)
    def _():
        o_ref[...] = jnp.zeros_like(o_ref)


def expert_ffn(xs, block_exp, block_first, block_rows, block_next, w_gate, w_up, w_down, layer):
    n_slots, d = xs.shape
    ff = w_gate.shape[3]
    fh, dh = ff // FF_SPLIT, d // FF_SPLIT
    n_blocks = n_slots // MOE_BLOCK
    hbm = pl.BlockSpec(memory_space=pl.ANY)
    up_spec = pltpu.PrefetchScalarGridSpec(
        num_scalar_prefetch=4,
        grid=(FF_SPLIT, n_blocks),
        in_specs=[pl.BlockSpec((MOE_BLOCK, d), lambda j, b, *_: (b, 0)), hbm, hbm],
        out_specs=pl.BlockSpec((MOE_BLOCK, fh), lambda j, b, *_: (b, j)),
        scratch_shapes=[pltpu.VMEM((d, fh), F32), pltpu.VMEM((d, fh), F32),
                        pltpu.VMEM((d, fh), BF16), pltpu.VMEM((d, fh), BF16),
                        pltpu.SemaphoreType.DMA((2,))],
    )
    act = pl.pallas_call(
        functools.partial(_ffn_up_kernel, layer=layer),
        grid_spec=up_spec,
        out_shape=jax.ShapeDtypeStruct((n_slots, ff), BF16),
        compiler_params=_params("arbitrary", "arbitrary"),
        name="expert_ffn_up",
    )(block_exp, block_first, block_rows, block_next, xs, w_gate, w_up)
    down_spec = pltpu.PrefetchScalarGridSpec(
        num_scalar_prefetch=4,
        grid=(FF_SPLIT, n_blocks),
        in_specs=[pl.BlockSpec((MOE_BLOCK, ff), lambda j, b, *_: (b, 0)), hbm],
        out_specs=pl.BlockSpec((MOE_BLOCK, dh // 2), lambda j, b, *_: (b, j)),
        scratch_shapes=[pltpu.VMEM((ff, dh), F32), pltpu.VMEM((ff, dh), BF16),
                        pltpu.SemaphoreType.DMA((1,))],
    )
    return pl.pallas_call(
        functools.partial(_ffn_down_kernel, layer=layer),
        grid_spec=down_spec,
        out_shape=jax.ShapeDtypeStruct((n_slots, d // 2), PACKED),
        compiler_params=_params("arbitrary", "arbitrary"),
        name="expert_ffn_down",
    )(block_exp, block_first, block_rows, block_next, act, w_down)


def _start_row_copies(idx_ref, base, n_rows, src_ref, dst_ref, sem):
    def issue(r, carry):
        row = idx_ref[base + r]
        pltpu.make_async_copy(src_ref.at[pl.ds(row, 1)], dst_ref.at[pl.ds(r, 1)], sem).start()
        return carry

    lax.fori_loop(0, n_rows, issue, 0, unroll=8)


def _wait_row_copies(n_rows, src_ref, dst_ref, sem):
    pltpu.make_async_copy(src_ref.at[pl.ds(0, n_rows)], dst_ref, sem).wait()


def _gather_cast_kernel(idx_ref, nused_ref, src_ref, o_ref, buf, sem):
    b = pl.program_id(0)
    rows = o_ref.shape[0]
    n_used = nused_ref[0]

    @pl.when((b == 0) & (n_used > 0))
    def _():
        _start_row_copies(idx_ref, 0, rows, src_ref, buf.at[0], sem.at[0])

    @pl.when(b + 1 < n_used)
    def _():
        nxt = (b + 1) % 2
        _start_row_copies(idx_ref, (b + 1) * rows, rows, src_ref, buf.at[nxt], sem.at[nxt])

    @pl.when(b < n_used)
    def _():
        cur = b % 2
        _wait_row_copies(rows, src_ref, buf.at[cur], sem.at[cur])
        lo, hi = _unpack_bf16_pair(buf[cur])
        half = lo.shape[1]
        o_ref[:, :half] = lo.astype(o_ref.dtype)
        o_ref[:, half:] = hi.astype(o_ref.dtype)

    @pl.when(b >= n_used)
    def _():
        o_ref[...] = jnp.zeros_like(o_ref)


def gather_rows_cast(src, idx, n_used_tiles, out_dtype):
    dp = src.shape[1]
    d = 2 * dp
    m = idx.shape[0]
    rows = GATHER_ROWS
    grid_spec = pltpu.PrefetchScalarGridSpec(
        num_scalar_prefetch=2,
        grid=(m // rows,),
        in_specs=[pl.BlockSpec(memory_space=pl.ANY)],
        out_specs=pl.BlockSpec((rows, d), lambda b, ix, nu: (b, 0)),
        scratch_shapes=[pltpu.VMEM((2, rows, dp), src.dtype), pltpu.SemaphoreType.DMA((2,))],
    )
    return pl.pallas_call(
        _gather_cast_kernel,
        grid_spec=grid_spec,
        out_shape=jax.ShapeDtypeStruct((m, d), out_dtype),
        compiler_params=_params("arbitrary"),
        name="gather_rows_cast",
    )(idx, n_used_tiles, src)


def _combine_kernel(pos_ref, x_ref, gate_ref, w_ref, g_ref, sc_ref, sh_ref, ys_ref,
                    *rest, n_tok, n_windows, final):
    if final:
        o_ref, ybuf, sem = rest
    else:
        o_ref, h_ref, ybuf, sem = rest
    i = pl.program_id(0)
    n_tiles = pl.num_programs(0)
    tm = x_ref.shape[0]

    def start_tile(tile, slot):
        for kk in range(TOP_K):
            _start_row_copies(pos_ref, kk * n_tok + tile * tm, tm, ys_ref, ybuf.at[slot, kk],
                              sem.at[slot])

    @pl.when(i == 0)
    def _():
        start_tile(0, 0)

    @pl.when(i + 1 < n_tiles)
    def _():
        start_tile(i + 1, (i + 1) % 2)

    cur = i % 2
    for kk in range(TOP_K):
        _wait_row_copies(tm, ys_ref, ybuf.at[cur, kk], sem.at[cur])
    w = w_ref[...]
    acc = None
    for kk in range(TOP_K):
        lo, hi = _unpack_bf16_pair(ybuf[cur, kk])
        ww = lo.shape[1] // n_windows
        parts = []
        for n in range(n_windows):
            parts += [lo[:, n * ww:(n + 1) * ww], hi[:, n * ww:(n + 1) * ww]]
        term = w[:, kk:kk + 1] * jnp.concatenate(parts, axis=1)
        acc = term if acc is None else acc + term
    x_new = x_ref[...] + gate_ref[0] * acc
    if final:
        o_ref[...] = _rms(x_new, g_ref[...])
    else:
        o_ref[...] = x_new
        h_ref[...] = (_rms(x_new, g_ref[...]) * sc_ref[0] + sh_ref[0]).astype(h_ref.dtype)


def moe_combine(x2, gate, ys, pos, w_tok, seq, norm_g, sc1p=None, sh=None):
    t, d = x2.shape
    tm = _tile(seq, GATHER_ROWS)
    per_b = seq // tm
    final = sc1p is None
    if final:
        sc1p = sh = jnp.zeros((x2.shape[0] // seq, 1, d), F32)
    row = pl.BlockSpec((tm, d), lambda i, ps: (i, 0))
    mod = pl.BlockSpec((1, 1, d), lambda i, ps: (i // per_b, 0, 0))
    tok = pl.BlockSpec((tm, TOP_K), lambda i, ps: (i, 0))
    grid_spec = pltpu.PrefetchScalarGridSpec(
        num_scalar_prefetch=1,
        grid=(t // tm,),
        in_specs=[row, mod, tok, pl.BlockSpec((1, d), lambda i, ps: (0, 0)), mod, mod,
                  pl.BlockSpec(memory_space=pl.ANY)],
        out_specs=row if final else [row, row],
        scratch_shapes=[pltpu.VMEM((2, TOP_K, tm, d // 2), ys.dtype),
                        pltpu.SemaphoreType.DMA((2,))],
    )
    out_shape = jax.ShapeDtypeStruct((t, d), F32)
    return pl.pallas_call(
        functools.partial(_combine_kernel, n_tok=t, n_windows=FF_SPLIT, final=final),
        grid_spec=grid_spec,
        out_shape=out_shape if final else [out_shape, jax.ShapeDtypeStruct((t, d), BF16)],
        compiler_params=_params("arbitrary"),
        name="moe_combine",
    )(pos, x2, gate, w_tok, norm_g.reshape(1, d), sc1p, sh, ys)


def _group_by_expert(e_idx):
    n_tok = e_idx.shape[1]
    n_asg = n_tok * TOP_K
    flat_e = e_idx.reshape(n_asg)
    onehot = (flat_e[:, None] == jnp.arange(N_EXPERTS)[None, :]).astype(jnp.int32)
    running = jnp.cumsum(onehot, axis=0)
    rank = jnp.sum(running * onehot, axis=1) - 1
    sizes = running[-1]
    padded = (sizes + MOE_BLOCK - 1) // MOE_BLOCK * MOE_BLOCK
    pad_end = jnp.cumsum(padded)
    pad_start = pad_end - padded
    dest = (jnp.sum(pad_start[None, :] * onehot, axis=1) + rank).astype(jnp.int32)
    n_blocks = -(-n_asg // MOE_BLOCK) + N_EXPERTS
    n_slots = n_blocks * MOE_BLOCK
    slot_tok = (jnp.arange(n_slots, dtype=jnp.int32) % n_tok).at[dest].set(
        jnp.arange(n_asg, dtype=jnp.int32) % n_tok, mode="promise_in_bounds")
    block_start = jnp.arange(n_blocks, dtype=jnp.int32) * MOE_BLOCK
    block_exp = jnp.minimum(jnp.sum(block_start[:, None] >= pad_end[None, :], axis=1),
                            N_EXPERTS - 1).astype(jnp.int32)
    n_used = (pad_end[-1] // MOE_BLOCK).astype(jnp.int32)
    block_exp = jnp.where(jnp.arange(n_blocks) < n_used, block_exp, block_exp[n_used - 1])
    prev_exp = jnp.concatenate([jnp.full((1,), -1, jnp.int32), block_exp[:-1]])
    block_first = (block_exp != prev_exp).astype(jnp.int32)
    block_rows = jnp.clip(pad_start[block_exp] + sizes[block_exp] - block_start, 0, MOE_BLOCK)
    block_rows = jnp.where(jnp.arange(n_blocks) < n_used, block_rows, 0).astype(jnp.int32)
    blk = jnp.arange(n_blocks, dtype=jnp.int32)
    starts = jnp.where((block_first == 1) & (block_rows > 0), blk, n_blocks)
    later = lax.cummin(jnp.concatenate([starts[1:], jnp.full((1,), n_blocks, jnp.int32)]),
                       axis=0, reverse=True)
    block_next = jnp.where(later < n_blocks, block_exp[jnp.minimum(later, n_blocks - 1)],
                           -1).astype(jnp.int32)
    return (dest.reshape(TOP_K, n_tok), slot_tok, block_exp, block_first, block_rows, block_next,
            n_used.reshape(1))


def _rot_cols(w):
    half = w.shape[-1] // 2
    return jnp.concatenate([-w[..., half:], w[..., :half]], axis=-1)


IN_ROW_BLOCK = 64


def _split_in_weights_kernel(wt_ref, a_ref, b_ref, c_ref, *, a_blocks, b_blocks, c_blocks):
    g = pl.program_id(1)
    w = wt_ref[0]

    @pl.when(g < a_blocks)
    def _():
        a_ref[0] = w.astype(a_ref.dtype)

    @pl.when((g >= a_blocks) & (g < a_blocks + b_blocks))
    def _():
        b_ref[0] = w.astype(b_ref.dtype)

    @pl.when((g >= a_blocks + b_blocks) & (g < a_blocks + b_blocks + c_blocks))
    def _():
        c_ref[0] = w.astype(c_ref.dtype)

    @pl.when(g == a_blocks + b_blocks + c_blocks)
    def _():
        half = w.shape[0] // 2
        a_ref[0] = jnp.concatenate([-w[half:], w[:half]], axis=0).astype(a_ref.dtype)


def split_in_weights(w_in):
    layers, d, n_in = w_in.shape
    wt = jnp.swapaxes(w_in, 1, 2)
    rb = IN_ROW_BLOCK
    assert MLA_ROPE == rb
    a_blocks = (Q_LORA + KV_LORA + MLA_ROPE) // rb
    b_blocks = 3 * CONV_CH // rb
    c_blocks = n_in // rb - a_blocks - b_blocks
    last = a_blocks + b_blocks + c_blocks

    def src(l, g):
        return (l, jnp.where(g == last, a_blocks - 1, g), 0)

    def dst_a(l, g):
        return (l, jnp.where(g == last, a_blocks, jnp.minimum(g, a_blocks - 1)), 0)

    def dst_b(l, g):
        return (l, jnp.clip(g - a_blocks, 0, b_blocks - 1), 0)

    def dst_c(l, g):
        return (l, jnp.clip(g - a_blocks - b_blocks, 0, c_blocks - 1), 0)

    blk = (1, rb, d)
    return pl.pallas_call(
        functools.partial(_split_in_weights_kernel, a_blocks=a_blocks, b_blocks=b_blocks,
                          c_blocks=c_blocks),
        grid=(layers, last + 1),
        in_specs=[pl.BlockSpec(blk, src)],
        out_specs=[pl.BlockSpec(blk, dst_a), pl.BlockSpec(blk, dst_b), pl.BlockSpec(blk, dst_c)],
        out_shape=[jax.ShapeDtypeStruct((layers, nb * rb, d), BF16)
                   for nb in (a_blocks + 1, b_blocks, c_blocks)],
        compiler_params=_params("arbitrary", "arbitrary"),
        name="split_in_weights",
    )(wt)


def _prep_weights(w_in, w_uq, w_ukv, w_out):
    layers = w_in.shape[0]
    w_a, w_b, w_c = split_in_weights(w_in)
    wq = w_uq.reshape(layers, Q_LORA, MLA_HEADS, MLA_NOPE + MLA_ROPE)
    wq_rope = wq[..., MLA_NOPE:]
    wq = jnp.concatenate([wq, _rot_cols(wq_rope)], axis=-1).transpose(0, 2, 1, 3).astype(BF16)
    return dict(w_a=w_a, w_b=w_b, w_c=w_c, w_q=wq, w_kv=w_ukv.astype(BF16),
                w_out=w_out.astype(BF16))


def _rope_tables(seq, dim):
    inv = 1.0 / (ROPE_THETA ** (jnp.arange(0, dim, 2, dtype=F32) / dim))
    ang = jnp.arange(seq, dtype=F32)[:, None] * inv[None, :]
    return jnp.cos(ang), jnp.sin(ang)


def kernel(x, c, w_mod, mod_table, mix_norm_g, w_in, q_norm_g, kv_norm_g, w_uq, w_ukv, conv_w,
           group_norm_g, w_out, ffn_norm_g, w_router, router_bias, w_gate, w_up, w_down,
           final_norm_g):
    batch, seq, d = x.shape
    depth = w_in.shape[0]
    t = batch * seq
    x2 = x.reshape(t, d)

    cos_a, sin_a = _rope_tables(seq, MLA_ROPE)
    zeros_a = jnp.zeros_like(cos_a)
    cos_p = jnp.concatenate([cos_a, cos_a, zeros_a, zeros_a], axis=1)
    sin_p = jnp.concatenate([sin_a, sin_a, zeros_a, zeros_a], axis=1)
    cos_b, sin_b = _rope_tables(seq, MOBA_HD)
    cos_f = jnp.concatenate([cos_b, cos_b], axis=1)
    sin_s = jnp.concatenate([-sin_b, sin_b], axis=1)

    c_pad = jnp.zeros((SUBLANE, d), F32).at[:batch].set(c)
    mod_shared = mod_matmul(c_pad, w_mod)[:batch].reshape(batch, N_MOD, d)

    wr_t = w_router.T
    wr_hi = wr_t.astype(BF16)
    wr_lo = (wr_t - wr_hi.astype(F32)).astype(BF16)
    router = (wr_hi, wr_lo, router_bias.astype(F32).reshape(N_EXPERTS, 1))
    mla_scale = (MLA_NOPE + MLA_ROPE) ** -0.5

    mods = []
    for l in range(depth):
        mod = mod_shared + mod_table[l][None]
        mods.append([mod[:, i][:, None, :] for i in range(N_MOD)])

    p = _prep_weights(w_in, w_uq, w_ukv, w_out)
    h = norm_mod(x2, mix_norm_g[0], 1.0 + mods[0][1], mods[0][0], seq)
    for l in range(depth):
        sh1, sc1, g1, sh2, sc2, g2 = mods[l]

        proj_a = matmul(h, p["w_a"], l, F32, tm=512)
        bch = matmul(h, p["w_b"], l, F32)
        qkv = matmul(h, p["w_c"], l, F32)

        q_a = mla_q_up(proj_a, q_norm_g[l], p["w_q"], l, cos_p, sin_p, seq, mla_scale)
        k_a, v_a = mla_kv_up(proj_a, kv_norm_g[l], p["w_kv"], l, cos_p, sin_p, seq)
        y_a = mla_attention(q_a, k_a, v_a, batch, seq, BF16)

        y_b = conv_mixer(bch, conv_w[l], seq, F32)

        q_c, k_c, v_c, k_mean = moba_prep(qkv, cos_f, sin_s, seq)
        n_blk = seq // MOBA_BLOCK
        k_mean = k_mean.reshape(batch, n_blk, MOBA_HEADS, MOBA_HD).transpose(0, 2, 1, 3)
        y_c = moba_attention(q_c, k_c, v_c, k_mean, batch, seq, BF16)

        x2 = out_proj(y_a, y_b, y_c, group_norm_g[l], p["w_out"], l, x2, g1, seq)

        h2, e_idx, gates = norm_mod(x2, ffn_norm_g[l], 1.0 + sc2, sh2, seq, router=router)
        pos, slot_tok, block_exp, block_first, block_rows, block_next, n_used = _group_by_expert(e_idx)
        xs = gather_rows_cast(h2, slot_tok, n_used * (MOE_BLOCK // GATHER_ROWS), BF16)
        ys = expert_ffn(xs, block_exp, block_first, block_rows, block_next, w_gate, w_up, w_down,
                        l)
        if l + 1 < depth:
            nxt = mods[l + 1]
            x2, h = moe_combine(x2, g2, ys, pos.reshape(-1), gates.T, seq, mix_norm_g[l + 1],
                                1.0 + nxt[1], nxt[0])
        else:
            out = moe_combine(x2, g2, ys, pos.reshape(-1), gates.T, seq, final_norm_g)

    return out.reshape(batch, seq, d)
```

```python
import functools

import jax
import jax.numpy as jnp
from jax import lax
from jax.experimental import pallas as pl
from jax.experimental.pallas import tpu as pltpu

MLA_HEADS = 16
MLA_NOPE = 128
MLA_ROPE = 64
MLA_V = 128
Q_LORA = 1024
KV_LORA = 512
MLA_DK_PAD = 256
CONV_CH = 1024
CONV_W = 3
MOBA_HEADS = 8
MOBA_HD = 128
MOBA_BLOCK = 256
MOBA_TOPK = 3
ROPE_THETA = 10000.0
EPS = 1e-6
N_MOD = 6
N_EXPERTS = 16
N_GROUPS = 4
TOP_K = 2
MOE_BLOCK = 512
GATHER_ROWS = 256
FF_SPLIT = 2
UP_HEADS_PER_STEP = 4
ATTN_HEADS_PER_STEP = 4
ATTN_TILE = 512
ONES_ROWS = 16

V7X_VMEM_LIMIT_BYTES = 56 * 1024 * 1024
LANE = 128
SUBLANE = 8

F32 = jnp.float32
BF16 = jnp.bfloat16
PACKED = jnp.uint32
NEG_INF = float("-inf")


def _params(*sem):
    return pltpu.CompilerParams(dimension_semantics=sem, vmem_limit_bytes=V7X_VMEM_LIMIT_BYTES)


def _dot(a, b, precision=None):
    return jnp.dot(a, b, preferred_element_type=F32, precision=precision)


def _dot_nt(a, b):
    return lax.dot_general(a, b, (((1,), (1,)), ((), ())), preferred_element_type=F32)


def _tile(dim, want):
    return want if dim % want == 0 else dim


def _mod_kernel(c_ref, w0_ref, w1_ref, o_ref):
    @pl.when(pl.program_id(0) == 0)
    def _():
        o_ref[...] = jnp.zeros_like(o_ref)

    c = c_ref[...]
    a = (c * jax.nn.sigmoid(c)).astype(BF16)
    half = w0_ref.shape[1]
    o_ref[:, :half] += _dot(a, w0_ref[...].astype(BF16))
    o_ref[:, half:] += _dot(a, w1_ref[...].astype(BF16))


def mod_matmul(c_pad, w_mod):
    rows, d = c_pad.shape
    n = w_mod.shape[1]
    tk = _tile(d, LANE)
    return pl.pallas_call(
        _mod_kernel,
        grid=(d // tk,),
        in_specs=[pl.BlockSpec((rows, tk), lambda k: (0, k)),
                  pl.BlockSpec((tk, n // 2), lambda k: (k, 0)),
                  pl.BlockSpec((tk, n // 2), lambda k: (k, 1))],
        out_specs=pl.BlockSpec((rows, n), lambda k: (0, 0)),
        out_shape=jax.ShapeDtypeStruct((rows, n), F32),
        compiler_params=_params("arbitrary"),
        name="mod_matmul",
    )(c_pad, w_mod, w_mod)


def _rms(x, g):
    return x * lax.rsqrt(jnp.mean(x * x, axis=-1, keepdims=True) + EPS) * g


def _pack_bf16_pair(lo, hi):
    lo_bits = lax.bitcast_convert_type(lo.astype(BF16).astype(F32), PACKED) >> 16
    hi_bits = lax.bitcast_convert_type(hi.astype(BF16).astype(F32), PACKED) & jnp.uint32(0xFFFF0000)
    return hi_bits | lo_bits


def _unpack_bf16_pair(packed):
    lo = lax.bitcast_convert_type(packed << 16, F32)
    hi = lax.bitcast_convert_type(packed & jnp.uint32(0xFFFF0000), F32)
    return lo, hi


def _norm_mod_kernel(x_ref, g_ref, sc_ref, sh_ref, o_ref):
    y = _rms(x_ref[...], g_ref[...])
    o_ref[...] = (y * sc_ref[0] + sh_ref[0]).astype(o_ref.dtype)


def _top2_sum(a, b, c, d):
    hi1, lo1 = jnp.maximum(a, b), jnp.minimum(a, b)
    hi2, lo2 = jnp.maximum(c, d), jnp.minimum(c, d)
    return jnp.maximum(hi1, hi2) + jnp.maximum(jnp.minimum(hi1, hi2), jnp.maximum(lo1, lo2))


def _norm_router_kernel(x_ref, g_ref, sc_ref, sh_ref, whi_ref, wlo_ref, bias_ref,
                        o_ref, e_ref, gate_ref):
    h = _rms(x_ref[...], g_ref[...]) * sc_ref[0] + sh_ref[0]
    hi = h.astype(BF16)
    o_ref[...] = _pack_bf16_pair(h[:, :h.shape[1] // 2], h[:, h.shape[1] // 2:])
    lo = (h - hi.astype(F32)).astype(BF16)
    whl = jnp.concatenate([whi_ref[...], wlo_ref[...]], axis=0)
    both = _dot_nt(whl, hi)
    n_e = whi_ref.shape[0]
    logits = both[:n_e] + both[n_e:] + _dot_nt(whi_ref[...], lo)
    scores = jax.nn.sigmoid(logits)
    biased = scores + bias_ref[...]
    per_grp = N_EXPERTS // N_GROUPS
    b_rows = [biased[e:e + 1, :] for e in range(N_EXPERTS)]
    s_rows = [scores[e:e + 1, :] for e in range(N_EXPERTS)]
    grp_scores = [_top2_sum(*b_rows[g * per_grp:(g + 1) * per_grp]) for g in range(N_GROUPS)]
    best, grp = grp_scores[0], jnp.zeros(grp_scores[0].shape, jnp.int32)
    for g in range(1, N_GROUPS):
        better = grp_scores[g] > best
        grp = jnp.where(better, g, grp)
        best = jnp.where(better, grp_scores[g], best)
    b4, s4 = b_rows[:per_grp], s_rows[:per_grp]
    for g in range(1, N_GROUPS):
        in_g = grp == g
        b4 = [jnp.where(in_g, b_rows[g * per_grp + j], b4[j]) for j in range(per_grp)]
        s4 = [jnp.where(in_g, s_rows[g * per_grp + j], s4[j]) for j in range(per_grp)]
    v1, i1 = b4[0], jnp.zeros(grp.shape, jnp.int32)
    for j in range(1, per_grp):
        better = b4[j] > v1
        i1 = jnp.where(better, j, i1)
        v1 = jnp.where(better, b4[j], v1)
    v2, i2 = jnp.full(v1.shape, NEG_INF, F32), jnp.zeros(grp.shape, jnp.int32)
    for j in range(per_grp):
        better = (i1 != j) & (b4[j] > v2)
        i2 = jnp.where(better, j, i2)
        v2 = jnp.where(better, b4[j], v2)
    g1, g2 = s4[0], s4[0]
    for j in range(1, per_grp):
        g1 = jnp.where(i1 == j, s4[j], g1)
        g2 = jnp.where(i2 == j, s4[j], g2)
    total = g1 + g2
    e_ref[...] = jnp.concatenate([grp * per_grp + i1, grp * per_grp + i2], axis=0)
    gate_ref[...] = jnp.concatenate([g1 / total, g2 / total], axis=0)


def norm_mod(x2, g, sc1p, sh, seq, out_dtype=BF16, router=None):
    t, d = x2.shape
    tm = _tile(seq, 256)
    per_b = seq // tm
    row = pl.BlockSpec((tm, d), lambda i: (i, 0))
    vec = pl.BlockSpec((1, d), lambda i: (0, 0))
    mod = pl.BlockSpec((1, 1, d), lambda i: (i // per_b, 0, 0))
    g2 = g.reshape(1, d)
    if router is None:
        return pl.pallas_call(
            _norm_mod_kernel, grid=(t // tm,), in_specs=[row, vec, mod, mod], out_specs=row,
            out_shape=jax.ShapeDtypeStruct((t, d), out_dtype),
            compiler_params=_params("arbitrary"), name="norm_mod")(x2, g2, sc1p, sh)
    whi_t, wlo_t, bias = router
    ne = whi_t.shape[0]
    wspec = pl.BlockSpec((ne, d), lambda i: (0, 0))
    kspec = pl.BlockSpec((TOP_K, tm), lambda i: (0, i))
    return pl.pallas_call(
        _norm_router_kernel, grid=(t // tm,),
        in_specs=[row, vec, mod, mod, wspec, wspec, pl.BlockSpec((ne, 1), lambda i: (0, 0))],
        out_specs=[pl.BlockSpec((tm, d // 2), lambda i: (i, 0)), kspec, kspec],
        out_shape=[jax.ShapeDtypeStruct((t, d // 2), PACKED),
                   jax.ShapeDtypeStruct((TOP_K, t), jnp.int32),
                   jax.ShapeDtypeStruct((TOP_K, t), F32)],
        compiler_params=_params("arbitrary"), name="norm_router")(x2, g2, sc1p, sh, whi_t, wlo_t, bias)


def _mm_kernel(a_ref, wt_ref, o_ref):
    o_ref[...] = _dot_nt(a_ref[...], wt_ref[0]).astype(o_ref.dtype)


def matmul(a, wt_layers, layer, out_dtype, tm=1024, tn=1024):
    m, kd = a.shape
    n = wt_layers.shape[1]
    tm, tn = _tile(m, tm), _tile(n, tn)
    return pl.pallas_call(
        _mm_kernel,
        grid=(m // tm, n // tn),
        in_specs=[pl.BlockSpec((tm, kd), lambda i, j: (i, 0)),
                  pl.BlockSpec((1, tn, kd), lambda i, j: (layer, j, 0))],
        out_specs=pl.BlockSpec((tm, tn), lambda i, j: (i, j)),
        out_shape=jax.ShapeDtypeStruct((m, n), out_dtype),
        compiler_params=_params("parallel", "parallel"),
        name="matmul",
    )(a, wt_layers)


def _rope_pair(r2, cos_ref, sin_ref):
    return r2 * cos_ref[...] + pltpu.roll(r2, MLA_ROPE, 1) * sin_ref[...]


def _with_ones_rows(vt):
    return jnp.concatenate([vt, jnp.ones((ONES_ROWS, vt.shape[1]), vt.dtype)], axis=0)


def _qup_kernel(cq_ref, g_ref, wt_ref, cos_ref, sin_ref, o_ref, ant_ref, *, scale):
    @pl.when(pl.program_id(1) == 0)
    def _():
        ant_ref[...] = _rms(cq_ref[...], g_ref[...]).T.astype(BF16)

    ant = ant_ref[...]
    cos, sin = cos_ref[...], sin_ref[...]
    rope_end = MLA_NOPE + MLA_ROPE
    for hh in range(wt_ref.shape[1]):
        res = _dot(wt_ref[0, hh], ant)
        roped = res[MLA_NOPE:rope_end] * cos + res[rope_end:] * sin
        qt = jnp.concatenate([res[:MLA_NOPE], roped, jnp.zeros_like(roped)], axis=0) * scale
        o_ref[hh, 0] = qt.astype(o_ref.dtype)


def mla_q_up(proj_a, g, wt_q, layer, cos_t, sin_t, seq, scale):
    t = proj_a.shape[0]
    heads = wt_q.shape[1]
    hb = UP_HEADS_PER_STEP
    tm = _tile(seq, ATTN_TILE)
    per_b = seq // tm
    return pl.pallas_call(
        functools.partial(_qup_kernel, scale=scale),
        grid=(t // tm, heads // hb),
        in_specs=[pl.BlockSpec((tm, Q_LORA), lambda i, h: (i, 0)),
                  pl.BlockSpec((1, Q_LORA), lambda i, h: (0, 0)),
                  pl.BlockSpec((1, hb, MLA_DK_PAD, Q_LORA), lambda i, h: (layer, h, 0, 0)),
                  pl.BlockSpec((MLA_ROPE, tm), lambda i, h: (0, i % per_b)),
                  pl.BlockSpec((MLA_ROPE, tm), lambda i, h: (0, i % per_b))],
        out_specs=pl.BlockSpec((hb, 1, MLA_DK_PAD, tm), lambda i, h: (h, i, 0, 0)),
        out_shape=jax.ShapeDtypeStruct((heads, t // tm, MLA_DK_PAD, tm), BF16),
        scratch_shapes=[pltpu.VMEM((Q_LORA, tm), BF16)],
        compiler_params=_params("parallel", "arbitrary"),
        name="mla_q_up",
    )(proj_a, g.reshape(1, Q_LORA), wt_q, cos_t, sin_t)


def _kvup_kernel(ckv_ref, g_ref, w_ref, kpe_ref, cos_ref, sin_ref, k_ref, vt_ref, an_ref):
    @pl.when(pl.program_id(1) == 0)
    def _():
        an_ref[...] = _rms(ckv_ref[...], g_ref[...]).astype(BF16)

    an = an_ref[...]
    k_rot = _rope_pair(kpe_ref[...], cos_ref, sin_ref)
    hw = MLA_NOPE + MLA_V
    for hh in range(k_ref.shape[0]):
        res = _dot(an, w_ref[0, :, hh * hw:(hh + 1) * hw])
        k_ref[hh] = jnp.concatenate([res[:, :MLA_NOPE], k_rot], axis=1).astype(k_ref.dtype)
        vt_ref[hh, 0] = _with_ones_rows(res[:, MLA_NOPE:].T).astype(vt_ref.dtype)


def mla_kv_up(proj_a, g, w_kv, layer, cos_p, sin_p, seq):
    t = proj_a.shape[0]
    hw = MLA_NOPE + MLA_V
    heads = w_kv.shape[2] // hw
    hb = UP_HEADS_PER_STEP
    tm = _tile(seq, ATTN_TILE)
    per_b = seq // tm
    return pl.pallas_call(
        _kvup_kernel,
        grid=(t // tm, heads // hb),
        in_specs=[pl.BlockSpec((tm, KV_LORA), lambda i, h: (i, Q_LORA // KV_LORA)),
                  pl.BlockSpec((1, KV_LORA), lambda i, h: (0, 0)),
                  pl.BlockSpec((1, KV_LORA, hb * hw), lambda i, h: (layer, 0, h)),
                  pl.BlockSpec((tm, LANE), lambda i, h: (i, (Q_LORA + KV_LORA) // LANE)),
                  pl.BlockSpec((tm, LANE), lambda i, h: (i % per_b, 0)),
                  pl.BlockSpec((tm, LANE), lambda i, h: (i % per_b, 0))],
        out_specs=[pl.BlockSpec((hb, tm, MLA_DK_PAD), lambda i, h: (h, i, 0)),
                   pl.BlockSpec((hb, 1, MLA_V + ONES_ROWS, tm), lambda i, h: (h, i, 0, 0))],
        out_shape=[jax.ShapeDtypeStruct((heads, t, MLA_DK_PAD), BF16),
                   jax.ShapeDtypeStruct((heads, t // tm, MLA_V + ONES_ROWS, tm), BF16)],
        scratch_shapes=[pltpu.VMEM((tm, KV_LORA), BF16)],
        compiler_params=_params("parallel", "arbitrary"),
        name="mla_kv_up",
    )(proj_a, g.reshape(1, KV_LORA), w_kv, proj_a, cos_p, sin_p)


def _pv(vts, p):
    rows = p.shape[0] // len(vts)
    out = _dot(vts[0], p[:rows].astype(vts[0].dtype))
    for n in range(1, len(vts)):
        out = out + _dot(vts[n], p[n * rows:(n + 1) * rows].astype(vts[n].dtype))
    return out


def _softmax_first(s, vts, m_ref, acc_ref):
    m = jnp.max(s, axis=0, keepdims=True)
    m_ref[...] = m
    acc_ref[...] = _pv(vts, jnp.exp(s - m))


def _softmax_step(s, vts, m_ref, acc_ref):
    m_prev = m_ref[...]
    m_new = jnp.maximum(m_prev, jnp.max(s, axis=0, keepdims=True))
    alpha = jnp.exp(m_prev - m_new)
    acc_ref[...] = alpha * acc_ref[...] + _pv(vts, jnp.exp(s - m_new))
    m_ref[...] = m_new


def _softmax_finish(acc_ref, dv):
    acc = acc_ref[...]
    return (acc[:dv, :] / acc[dv:dv + 1, :]).T


def _causal_mask_t(s):
    key = lax.broadcasted_iota(jnp.int32, s.shape, 0)
    qry = lax.broadcasted_iota(jnp.int32, s.shape, 1)
    return jnp.where(key <= qry, s, NEG_INF)


def _mla_attn_kernel(qt_ref, k_ref, vt_ref, o_ref, m_ref, acc_ref, *, tq, dv):
    i = pl.program_id(2)
    n_h = qt_ref.shape[0]
    qts = [qt_ref[h, 0] for h in range(n_h)]

    def scores(h, j, n):
        off = pl.multiple_of(j * tq, tq)
        return _dot(k_ref[h, pl.ds(off, n * tq), :], qts[h])

    for h in range(n_h):
        _softmax_first(_causal_mask_t(scores(h, i, 1)), [vt_ref[h, i]], m_ref.at[h], acc_ref.at[h])

    @pl.when(i % 2 == 1)
    def _():
        for h in range(n_h):
            _softmax_step(scores(h, i - 1, 1), [vt_ref[h, i - 1]], m_ref.at[h], acc_ref.at[h])

    def body(c, carry):
        for h in range(n_h):
            _softmax_step(scores(h, 2 * c, 2), [vt_ref[h, 2 * c], vt_ref[h, 2 * c + 1]],
                          m_ref.at[h], acc_ref.at[h])
        return carry

    lax.fori_loop(0, i // 2, body, 0)
    for h in range(n_h):
        o_ref[:, h * dv:(h + 1) * dv] = _softmax_finish(acc_ref.at[h], dv).astype(o_ref.dtype)


def mla_attention(qt, k, vt, batch, seq, out_dtype):
    heads, _, dk, tq = qt.shape
    t = k.shape[1]
    dvx = vt.shape[2]
    dv = dvx - ONES_ROWS
    nq = seq // tq
    hb = ATTN_HEADS_PER_STEP
    return pl.pallas_call(
        functools.partial(_mla_attn_kernel, tq=tq, dv=dv),
        grid=(batch, heads // hb, nq),
        in_specs=[pl.BlockSpec((hb, 1, dk, tq), lambda b, h, i: (h, b * nq + i, 0, 0)),
                  pl.BlockSpec((hb, seq, dk), lambda b, h, i: (h, b, 0)),
                  pl.BlockSpec((hb, nq, dvx, tq), lambda b, h, i: (h, b, 0, 0))],
        out_specs=pl.BlockSpec((tq, hb * dv), lambda b, h, i: (b * nq + i, h)),
        out_shape=jax.ShapeDtypeStruct((t, heads * dv), out_dtype),
        scratch_shapes=[pltpu.VMEM((hb, 1, tq), F32), pltpu.VMEM((hb, dvx, tq), F32)],
        compiler_params=_params("parallel", "parallel", "arbitrary"),
        name="mla_attention",
    )(qt, k, vt)


def _conv_kernel(b_ref, c_ref, h_ref, w_ref, o_ref, carry_ref, *, per_b):
    i = pl.program_id(0)

    @pl.when(i % per_b == 0)
    def _():
        carry_ref[...] = jnp.zeros_like(carry_ref)

    tm = o_ref.shape[0]
    w0, w1, w2 = w_ref[0:1, :], w_ref[1:2, :], w_ref[2:3, :]
    u = c_ref[...] * h_ref[...]
    y = w0 * pltpu.roll(u, 2, 0) + w1 * pltpu.roll(u, 1, 0) + w2 * u
    o_ref[...] = (b_ref[...] * y).astype(o_ref.dtype)
    u8 = u[0:SUBLANE, :]
    tail = carry_ref[...]
    r8 = lax.broadcasted_iota(jnp.int32, u8.shape, 0)
    p1 = jnp.where(r8 < 1, pltpu.roll(tail, 1, 0), pltpu.roll(u8, 1, 0))
    p2 = jnp.where(r8 < 2, pltpu.roll(tail, 2, 0), pltpu.roll(u8, 2, 0))
    y8 = w0 * p2 + w1 * p1 + w2 * u8
    o_ref[0:SUBLANE, :] = (b_ref[0:SUBLANE, :] * y8).astype(o_ref.dtype)
    carry_ref[...] = u[tm - SUBLANE:tm, :]


def conv_mixer(bch, conv_w, seq, out_dtype):
    t = bch.shape[0]
    ch = conv_w.shape[1]
    tm = _tile(seq, 256)
    per_b = seq // tm
    return pl.pallas_call(
        functools.partial(_conv_kernel, per_b=per_b),
        grid=(t // tm,),
        in_specs=[pl.BlockSpec((tm, ch), lambda i: (i, 0)),
                  pl.BlockSpec((tm, ch), lambda i: (i, 1)),
                  pl.BlockSpec((tm, ch), lambda i: (i, 2)),
                  pl.BlockSpec((CONV_W, ch), lambda i: (0, 0))],
        out_specs=pl.BlockSpec((tm, ch), lambda i: (i, 0)),
        out_shape=jax.ShapeDtypeStruct((t, ch), out_dtype),
        scratch_shapes=[pltpu.VMEM((SUBLANE, ch), F32)],
        compiler_params=_params("arbitrary"),
        name="conv_mixer",
    )(bch, bch, bch, conv_w)


def _moba_prep_kernel(q_ref, k_ref, v_ref, cos_ref, sin_ref, qo_ref, ko_ref, vo_ref, km_ref):
    heads = ko_ref.shape[0]
    cos, sin = cos_ref[...], sin_ref[...]
    n_sub = km_ref.shape[0]
    means = [[] for _ in range(n_sub)]
    for h in range(heads):
        sl = slice(h * MOBA_HD, (h + 1) * MOBA_HD)
        qh = q_ref[:, sl]
        kh = k_ref[:, sl]
        qr = qh * cos + pltpu.roll(qh, MOBA_HD // 2, 1) * sin
        qo_ref[h, 0] = qr.T.astype(qo_ref.dtype)
        kr = kh * cos + pltpu.roll(kh, MOBA_HD // 2, 1) * sin
        ko_ref[h] = kr.astype(ko_ref.dtype)
        vo_ref[h, 0] = _with_ones_rows(v_ref[:, sl].T).astype(vo_ref.dtype)
        for s in range(n_sub):
            means[s].append(jnp.mean(kr[s * MOBA_BLOCK:(s + 1) * MOBA_BLOCK], axis=0, keepdims=True))
    for s in range(n_sub):
        km_ref[s] = jnp.concatenate(means[s], axis=0)


def moba_prep(qkv, cos_f, sin_s, seq):
    t = qkv.shape[0]
    width = qkv.shape[1] // 3
    heads = width // MOBA_HD
    tm = 2 * MOBA_BLOCK
    assert seq % tm == 0
    per_b = seq // tm
    nt = t // tm
    return pl.pallas_call(
        _moba_prep_kernel,
        grid=(nt,),
        in_specs=[pl.BlockSpec((tm, width), lambda i: (i, 0)),
                  pl.BlockSpec((tm, width), lambda i: (i, 1)),
                  pl.BlockSpec((tm, width), lambda i: (i, 2)),
                  pl.BlockSpec((tm, MOBA_HD), lambda i: (i % per_b, 0)),
                  pl.BlockSpec((tm, MOBA_HD), lambda i: (i % per_b, 0))],
        out_specs=[pl.BlockSpec((heads, 1, MOBA_HD, tm), lambda i: (0, i, 0, 0)),
                   pl.BlockSpec((heads, tm, MOBA_HD), lambda i: (0, i, 0)),
                   pl.BlockSpec((heads, 1, MOBA_HD + ONES_ROWS, tm), lambda i: (0, i, 0, 0)),
                   pl.BlockSpec((2, heads, MOBA_HD), lambda i: (i, 0, 0))],
        out_shape=[jax.ShapeDtypeStruct((heads, nt, MOBA_HD, tm), BF16),
                   jax.ShapeDtypeStruct((heads, t, MOBA_HD), BF16),
                   jax.ShapeDtypeStruct((heads, nt, MOBA_HD + ONES_ROWS, tm), BF16),
                   jax.ShapeDtypeStruct((2 * nt, heads, MOBA_HD), F32)],
        compiler_params=_params("arbitrary"),
        name="moba_prep",
    )(qkv, qkv, qkv, cos_f, sin_s)


def _moba_attn_kernel(qt_ref, k_ref, vt_ref, km_ref, o_ref, m_ref, acc_ref, sel_ref, *, scale, dv):
    i = pl.program_id(2)
    n_h = qt_ref.shape[0]
    n_blk = km_ref.shape[2]
    tq = qt_ref.shape[3]
    blk_w = MOBA_BLOCK
    blk = lax.broadcasted_iota(jnp.int32, (n_blk, tq), 0)
    own = 2 * i + (lax.broadcasted_iota(jnp.int32, (n_blk, tq), 1) >= blk_w).astype(jnp.int32)
    past = blk < own
    qts = []
    for h in range(n_h):
        qtf = qt_ref[h, 0].astype(F32)
        g = _dot(km_ref[0, h], qtf, precision=lax.Precision.HIGHEST)
        g = jnp.where(past, g, NEG_INF)
        sel = jnp.zeros(g.shape, F32)
        for _ in range(min(MOBA_TOPK, n_blk)):
            mx = jnp.max(g, axis=0, keepdims=True)
            first = jnp.min(jnp.where(g == mx, blk, n_blk), axis=0, keepdims=True)
            hit = blk == first
            sel = jnp.where(hit, 1.0, sel)
            g = jnp.where(hit, NEG_INF, g)
        sel_ref[h] = jnp.where(past, sel, 0.0)
        qts.append((qtf * scale).astype(k_ref.dtype))

    def scores(h, p, n):
        off = pl.multiple_of(p * tq, tq)
        return _dot(k_ref[h, pl.ds(off, n * tq), :], qts[h])

    def chosen(h, n):
        return sel_ref[h, pl.ds(n, 1), :] > 0.5

    def mask_past(h, s, first_blk):
        parts = [jnp.where(chosen(h, first_blk + n), s[n * blk_w:(n + 1) * blk_w], NEG_INF)
                 for n in range(s.shape[0] // blk_w)]
        return jnp.concatenate(parts, axis=0)

    key = lax.broadcasted_iota(jnp.int32, (blk_w, tq), 0)
    qry = lax.broadcasted_iota(jnp.int32, (blk_w, tq), 1)
    bot_ok = key <= qry - blk_w
    for h in range(n_h):
        s = scores(h, i, 1)
        chosen_lim = jnp.where(chosen(h, 2 * i), blk_w, -1)
        top_ok = key <= jnp.where(qry < blk_w, qry, chosen_lim)
        s = jnp.concatenate([jnp.where(top_ok, s[:blk_w], NEG_INF),
                             jnp.where(bot_ok, s[blk_w:], NEG_INF)], axis=0)
        _softmax_first(s, [vt_ref[h, i]], m_ref.at[h], acc_ref.at[h])

    @pl.when(i % 2 == 1)
    def _():
        for h in range(n_h):
            s = mask_past(h, scores(h, i - 1, 1), 2 * (i - 1))
            _softmax_step(s, [vt_ref[h, i - 1]], m_ref.at[h], acc_ref.at[h])

    def body(c, carry):
        for h in range(n_h):
            s = mask_past(h, scores(h, 2 * c, 2), 4 * c)
            _softmax_step(s, [vt_ref[h, 2 * c], vt_ref[h, 2 * c + 1]], m_ref.at[h], acc_ref.at[h])
        return carry

    lax.fori_loop(0, i // 2, body, 0)
    for h in range(n_h):
        o_ref[:, h * dv:(h + 1) * dv] = _softmax_finish(acc_ref.at[h], dv).astype(o_ref.dtype)


def moba_attention(qt, k, vt, k_mean, batch, seq, out_dtype):
    heads, _, hd, tq = qt.shape
    t = k.shape[1]
    dvx = vt.shape[2]
    dv = dvx - ONES_ROWS
    n_blk = seq // MOBA_BLOCK
    nq = seq // tq
    hb = ATTN_HEADS_PER_STEP
    return pl.pallas_call(
        functools.partial(_moba_attn_kernel, scale=hd ** -0.5, dv=dv),
        grid=(batch, heads // hb, nq),
        in_specs=[pl.BlockSpec((hb, 1, hd, tq), lambda b, h, i: (h, b * nq + i, 0, 0)),
                  pl.BlockSpec((hb, seq, hd), lambda b, h, i: (h, b, 0)),
                  pl.BlockSpec((hb, nq, dvx, tq), lambda b, h, i: (h, b, 0, 0)),
                  pl.BlockSpec((1, hb, n_blk, hd), lambda b, h, i: (b, h, 0, 0))],
        out_specs=pl.BlockSpec((tq, hb * dv), lambda b, h, i: (b * nq + i, h)),
        out_shape=jax.ShapeDtypeStruct((t, heads * dv), out_dtype),
        scratch_shapes=[pltpu.VMEM((hb, 1, tq), F32), pltpu.VMEM((hb, dvx, tq), F32),
                        pltpu.VMEM((hb, n_blk, tq), F32)],
        compiler_params=_params("parallel", "parallel", "arbitrary"),
        name="moba_attention",
    )(qt, k, vt, k_mean)


def _out_proj_kernel(a_ref, b_ref, c_ref, ga_ref, gb_ref, gc_ref, w_ref, x_ref, gate_ref, o_ref,
                     y_ref):
    @pl.when(pl.program_id(1) == 0)
    def _():
        wa, wb = a_ref.shape[1], b_ref.shape[1]
        y_ref[:, 0:wa] = _rms(a_ref[...].astype(F32), ga_ref[...]).astype(y_ref.dtype)
        y_ref[:, wa:wa + wb] = _rms(b_ref[...].astype(F32), gb_ref[...]).astype(y_ref.dtype)
        y_ref[:, wa + wb:] = _rms(c_ref[...].astype(F32), gc_ref[...]).astype(y_ref.dtype)

    o_ref[...] = x_ref[...] + gate_ref[0] * _dot(y_ref[...], w_ref[0])


def out_proj(y_a, y_b, y_c, g, w_layers, layer, x2, gate, seq, tm=1024, tn=512):
    t = y_a.shape[0]
    wa, wb, wc = y_a.shape[1], y_b.shape[1], y_c.shape[1]
    _, kd, n = w_layers.shape
    tm, tn = _tile(seq, tm), _tile(n, tn)
    per_b = seq // tm
    ga, gb, gc = g[:wa].reshape(1, wa), g[wa:wa + wb].reshape(1, wb), g[wa + wb:].reshape(1, wc)
    rows = lambda width: pl.BlockSpec((tm, width), lambda i, j: (i, 0))
    vec = lambda width: pl.BlockSpec((1, width), lambda i, j: (0, 0))
    return pl.pallas_call(
        _out_proj_kernel,
        grid=(t // tm, n // tn),
        in_specs=[rows(wa), rows(wb), rows(wc), vec(wa), vec(wb), vec(wc),
                  pl.BlockSpec((1, kd, tn), lambda i, j: (layer, 0, j)),
                  pl.BlockSpec((tm, tn), lambda i, j: (i, j)),
                  pl.BlockSpec((1, 1, tn), lambda i, j: (i // per_b, 0, j))],
        out_specs=pl.BlockSpec((tm, tn), lambda i, j: (i, j)),
        out_shape=jax.ShapeDtypeStruct((t, n), F32),
        scratch_shapes=[pltpu.VMEM((tm, kd), BF16)],
        compiler_params=_params("parallel", "arbitrary"),
        name="out_proj",
    )(y_a, y_b, y_c, ga, gb, gc, w_layers, x2, gate)


def _expert_weight_copies(w_hbm, layer, expert, col, stage, sem):
    width = stage.shape[1]
    return pltpu.make_async_copy(w_hbm.at[layer, expert, :, pl.ds(col, width)], stage, sem)


def _ffn_up_kernel(bexp_ref, first_ref, rows_ref, next_ref, x_ref, wg_hbm, wu_hbm, a_ref,
                   wg_st, wu_st, wg_bf, wu_bf, sem, *, layer):
    j = pl.program_id(0)
    b = pl.program_id(1)
    n_rows = rows_ref[b]
    half = x_ref.shape[0] // 2
    col = pl.multiple_of(j * wg_bf.shape[1], wg_bf.shape[1])

    def copies(expert):
        return (_expert_weight_copies(wg_hbm, layer, expert, col, wg_st, sem.at[0]),
                _expert_weight_copies(wu_hbm, layer, expert, col, wu_st, sem.at[1]))

    @pl.when(b == 0)
    def _():
        for c in copies(bexp_ref[0]):
            c.start()

    @pl.when((n_rows > 0) & (first_ref[b] == 1))
    def _():
        for c in copies(bexp_ref[b]):
            c.wait()
        wg_bf[...] = wg_st[...].astype(BF16)
        wu_bf[...] = wu_st[...].astype(BF16)

        @pl.when(next_ref[b] >= 0)
        def _():
            for c in copies(next_ref[b]):
                c.start()

    def act(x):
        gate = _dot(x, wg_bf[...])
        up = _dot(x, wu_bf[...])
        return (gate * jax.nn.sigmoid(gate) * up).astype(a_ref.dtype)

    @pl.when(n_rows > half)
    def _():
        a_ref[...] = act(x_ref[...])

    @pl.when((n_rows > 0) & (n_rows <= half))
    def _():
        a_ref[:half] = act(x_ref[:half])
        a_ref[half:] = jnp.zeros((half, a_ref.shape[1]), a_ref.dtype)

    @pl.when(n_rows == 0)
    def _():
        a_ref[...] = jnp.zeros_like(a_ref)


def _ffn_down_kernel(bexp_ref, first_ref, rows_ref, next_ref, a_ref, wd_hbm, o_ref,
                     wd_st, wd_bf, sem, *, layer):
    j = pl.program_id(0)
    b = pl.program_id(1)
    n_rows = rows_ref[b]
    half = a_ref.shape[0] // 2
    col = pl.multiple_of(j * wd_bf.shape[1], wd_bf.shape[1])

    def copy(expert):
        return _expert_weight_copies(wd_hbm, layer, expert, col, wd_st, sem.at[0])

    @pl.when(b == 0)
    def _():
        copy(bexp_ref[0]).start()

    @pl.when((n_rows > 0) & (first_ref[b] == 1))
    def _():
        copy(bexp_ref[b]).wait()
        wd_bf[...] = wd_st[...].astype(BF16)

        @pl.when(next_ref[b] >= 0)
        def _():
            copy(next_ref[b]).start()

    def packed(y):
        mid = y.shape[1] // 2
        return _pack_bf16_pair(y[:, :mid], y[:, mid:])

    @pl.when(n_rows > half)
    def _():
        o_ref[...] = packed(_dot(a_ref[...], wd_bf[...]))

    @pl.when((n_rows > 0) & (n_rows <= half))
    def _():
        o_ref[:half] = packed(_dot(a_ref[:half], wd_bf[...]))
        o_ref[half:] = jnp.zeros((half, o_ref.shape[1]), o_ref.dtype)

    @pl.when(n_rows == 0)
    def _():
        o_ref[...] = jnp.zeros_like(o_ref)


def expert_ffn(xs, block_exp, block_first, block_rows, block_next, w_gate, w_up, w_down, layer):
    n_slots, d = xs.shape
    ff = w_gate.shape[3]
    fh, dh = ff // FF_SPLIT, d // FF_SPLIT
    n_blocks = n_slots // MOE_BLOCK
    hbm = pl.BlockSpec(memory_space=pl.ANY)
    up_spec = pltpu.PrefetchScalarGridSpec(
        num_scalar_prefetch=4,
        grid=(FF_SPLIT, n_blocks),
        in_specs=[pl.BlockSpec((MOE_BLOCK, d), lambda j, b, *_: (b, 0)), hbm, hbm],
        out_specs=pl.BlockSpec((MOE_BLOCK, fh), lambda j, b, *_: (b, j)),
        scratch_shapes=[pltpu.VMEM((d, fh), F32), pltpu.VMEM((d, fh), F32),
                        pltpu.VMEM((d, fh), BF16), pltpu.VMEM((d, fh), BF16),
                        pltpu.SemaphoreType.DMA((2,))],
    )
    act = pl.pallas_call(
        functools.partial(_ffn_up_kernel, layer=layer),
        grid_spec=up_spec,
        out_shape=jax.ShapeDtypeStruct((n_slots, ff), BF16),
        compiler_params=_params("arbitrary", "arbitrary"),
        name="expert_ffn_up",
    )(block_exp, block_first, block_rows, block_next, xs, w_gate, w_up)
    down_spec = pltpu.PrefetchScalarGridSpec(
        num_scalar_prefetch=4,
        grid=(FF_SPLIT, n_blocks),
        in_specs=[pl.BlockSpec((MOE_BLOCK, ff), lambda j, b, *_: (b, 0)), hbm],
        out_specs=pl.BlockSpec((MOE_BLOCK, dh // 2), lambda j, b, *_: (b, j)),
        scratch_shapes=[pltpu.VMEM((ff, dh), F32), pltpu.VMEM((ff, dh), BF16),
                        pltpu.SemaphoreType.DMA((1,))],
    )
    return pl.pallas_call(
        functools.partial(_ffn_down_kernel, layer=layer),
        grid_spec=down_spec,
        out_shape=jax.ShapeDtypeStruct((n_slots, d // 2), PACKED),
        compiler_params=_params("arbitrary", "arbitrary"),
        name="expert_ffn_down",
    )(block_exp, block_first, block_rows, block_next, act, w_down)


def _start_row_copies(idx_ref, base, n_rows, src_ref, dst_ref, sem):
    def issue(r, carry):
        row = idx_ref[base + r]
        pltpu.make_async_copy(src_ref.at[pl.ds(row, 1)], dst_ref.at[pl.ds(r, 1)], sem).start()
        return carry

    lax.fori_loop(0, n_rows, issue, 0, unroll=8)


def _wait_row_copies(n_rows, src_ref, dst_ref, sem):
    pltpu.make_async_copy(src_ref.at[pl.ds(0, n_rows)], dst_ref, sem).wait()


def _gather_cast_kernel(idx_ref, nused_ref, src_ref, o_ref, buf, sem):
    b = pl.program_id(0)
    rows = o_ref.shape[0]
    n_used = nused_ref[0]

    @pl.when((b == 0) & (n_used > 0))
    def _():
        _start_row_copies(idx_ref, 0, rows, src_ref, buf.at[0], sem.at[0])

    @pl.when(b + 1 < n_used)
    def _():
        nxt = (b + 1) % 2
        _start_row_copies(idx_ref, (b + 1) * rows, rows, src_ref, buf.at[nxt], sem.at[nxt])

    @pl.when(b < n_used)
    def _():
        cur = b % 2
        _wait_row_copies(rows, src_ref, buf.at[cur], sem.at[cur])
        lo, hi = _unpack_bf16_pair(buf[cur])
        half = lo.shape[1]
        o_ref[:, :half] = lo.astype(o_ref.dtype)
        o_ref[:, half:] = hi.astype(o_ref.dtype)

    @pl.when(b >= n_used)
    def _():
        o_ref[...] = jnp.zeros_like(o_ref)


def gather_rows_cast(src, idx, n_used_tiles, out_dtype):
    dp = src.shape[1]
    d = 2 * dp
    m = idx.shape[0]
    rows = GATHER_ROWS
    grid_spec = pltpu.PrefetchScalarGridSpec(
        num_scalar_prefetch=2,
        grid=(m // rows,),
        in_specs=[pl.BlockSpec(memory_space=pl.ANY)],
        out_specs=pl.BlockSpec((rows, d), lambda b, ix, nu: (b, 0)),
        scratch_shapes=[pltpu.VMEM((2, rows, dp), src.dtype), pltpu.SemaphoreType.DMA((2,))],
    )
    return pl.pallas_call(
        _gather_cast_kernel,
        grid_spec=grid_spec,
        out_shape=jax.ShapeDtypeStruct((m, d), out_dtype),
        compiler_params=_params("arbitrary"),
        name="gather_rows_cast",
    )(idx, n_used_tiles, src)


def _combine_kernel(pos_ref, x_ref, gate_ref, w_ref, g_ref, sc_ref, sh_ref, ys_ref,
                    *rest, n_tok, n_windows, final):
    if final:
        o_ref, ybuf, sem = rest
    else:
        o_ref, h_ref, ybuf, sem = rest
    i = pl.program_id(0)
    n_tiles = pl.num_programs(0)
    tm = x_ref.shape[0]

    def start_tile(tile, slot):
        for kk in range(TOP_K):
            _start_row_copies(pos_ref, kk * n_tok + tile * tm, tm, ys_ref, ybuf.at[slot, kk],
                              sem.at[slot])

    @pl.when(i == 0)
    def _():
        start_tile(0, 0)

    @pl.when(i + 1 < n_tiles)
    def _():
        start_tile(i + 1, (i + 1) % 2)

    cur = i % 2
    for kk in range(TOP_K):
        _wait_row_copies(tm, ys_ref, ybuf.at[cur, kk], sem.at[cur])
    w = w_ref[...]
    acc = None
    for kk in range(TOP_K):
        lo, hi = _unpack_bf16_pair(ybuf[cur, kk])
        ww = lo.shape[1] // n_windows
        parts = []
        for n in range(n_windows):
            parts += [lo[:, n * ww:(n + 1) * ww], hi[:, n * ww:(n + 1) * ww]]
        term = w[:, kk:kk + 1] * jnp.concatenate(parts, axis=1)
        acc = term if acc is None else acc + term
    x_new = x_ref[...] + gate_ref[0] * acc
    if final:
        o_ref[...] = _rms(x_new, g_ref[...])
    else:
        o_ref[...] = x_new
        h_ref[...] = (_rms(x_new, g_ref[...]) * sc_ref[0] + sh_ref[0]).astype(h_ref.dtype)


def moe_combine(x2, gate, ys, pos, w_tok, seq, norm_g, sc1p=None, sh=None):
    t, d = x2.shape
    tm = _tile(seq, GATHER_ROWS)
    per_b = seq // tm
    final = sc1p is None
    if final:
        sc1p = sh = jnp.zeros((x2.shape[0] // seq, 1, d), F32)
    row = pl.BlockSpec((tm, d), lambda i, ps: (i, 0))
    mod = pl.BlockSpec((1, 1, d), lambda i, ps: (i // per_b, 0, 0))
    tok = pl.BlockSpec((tm, TOP_K), lambda i, ps: (i, 0))
    grid_spec = pltpu.PrefetchScalarGridSpec(
        num_scalar_prefetch=1,
        grid=(t // tm,),
        in_specs=[row, mod, tok, pl.BlockSpec((1, d), lambda i, ps: (0, 0)), mod, mod,
                  pl.BlockSpec(memory_space=pl.ANY)],
        out_specs=row if final else [row, row],
        scratch_shapes=[pltpu.VMEM((2, TOP_K, tm, d // 2), ys.dtype),
                        pltpu.SemaphoreType.DMA((2,))],
    )
    out_shape = jax.ShapeDtypeStruct((t, d), F32)
    return pl.pallas_call(
        functools.partial(_combine_kernel, n_tok=t, n_windows=FF_SPLIT, final=final),
        grid_spec=grid_spec,
        out_shape=out_shape if final else [out_shape, jax.ShapeDtypeStruct((t, d), BF16)],
        compiler_params=_params("arbitrary"),
        name="moe_combine",
    )(pos, x2, gate, w_tok, norm_g.reshape(1, d), sc1p, sh, ys)


def _group_by_expert(e_idx):
    n_tok = e_idx.shape[1]
    n_asg = n_tok * TOP_K
    flat_e = e_idx.reshape(n_asg)
    onehot = (flat_e[:, None] == jnp.arange(N_EXPERTS)[None, :]).astype(jnp.int32)
    running = jnp.cumsum(onehot, axis=0)
    rank = jnp.sum(running * onehot, axis=1) - 1
    sizes = running[-1]
    padded = (sizes + MOE_BLOCK - 1) // MOE_BLOCK * MOE_BLOCK
    pad_end = jnp.cumsum(padded)
    pad_start = pad_end - padded
    dest = (jnp.sum(pad_start[None, :] * onehot, axis=1) + rank).astype(jnp.int32)
    n_blocks = -(-n_asg // MOE_BLOCK) + N_EXPERTS
    n_slots = n_blocks * MOE_BLOCK
    slot_tok = (jnp.arange(n_slots, dtype=jnp.int32) % n_tok).at[dest].set(
        jnp.arange(n_asg, dtype=jnp.int32) % n_tok, mode="promise_in_bounds")
    block_start = jnp.arange(n_blocks, dtype=jnp.int32) * MOE_BLOCK
    block_exp = jnp.minimum(jnp.sum(block_start[:, None] >= pad_end[None, :], axis=1),
                            N_EXPERTS - 1).astype(jnp.int32)
    n_used = (pad_end[-1] // MOE_BLOCK).astype(jnp.int32)
    block_exp = jnp.where(jnp.arange(n_blocks) < n_used, block_exp, block_exp[n_used - 1])
    prev_exp = jnp.concatenate([jnp.full((1,), -1, jnp.int32), block_exp[:-1]])
    block_first = (block_exp != prev_exp).astype(jnp.int32)
    block_rows = jnp.clip(pad_start[block_exp] + sizes[block_exp] - block_start, 0, MOE_BLOCK)
    block_rows = jnp.where(jnp.arange(n_blocks) < n_used, block_rows, 0).astype(jnp.int32)
    blk = jnp.arange(n_blocks, dtype=jnp.int32)
    starts = jnp.where((block_first == 1) & (block_rows > 0), blk, n_blocks)
    later = lax.cummin(jnp.concatenate([starts[1:], jnp.full((1,), n_blocks, jnp.int32)]),
                       axis=0, reverse=True)
    block_next = jnp.where(later < n_blocks, block_exp[jnp.minimum(later, n_blocks - 1)],
                           -1).astype(jnp.int32)
    return (dest.reshape(TOP_K, n_tok), slot_tok, block_exp, block_first, block_rows, block_next,
            n_used.reshape(1))


def _rot_cols(w):
    half = w.shape[-1] // 2
    return jnp.concatenate([-w[..., half:], w[..., :half]], axis=-1)


IN_ROW_BLOCK = 64


def _split_in_weights_kernel(wt_ref, a_ref, b_ref, c_ref, *, a_blocks, b_blocks, c_blocks):
    g = pl.program_id(1)
    w = wt_ref[0]

    @pl.when(g < a_blocks)
    def _():
        a_ref[0] = w.astype(a_ref.dtype)

    @pl.when((g >= a_blocks) & (g < a_blocks + b_blocks))
    def _():
        b_ref[0] = w.astype(b_ref.dtype)

    @pl.when((g >= a_blocks + b_blocks) & (g < a_blocks + b_blocks + c_blocks))
    def _():
        c_ref[0] = w.astype(c_ref.dtype)

    @pl.when(g == a_blocks + b_blocks + c_blocks)
    def _():
        half = w.shape[0] // 2
        a_ref[0] = jnp.concatenate([-w[half:], w[:half]], axis=0).astype(a_ref.dtype)


def split_in_weights(w_in):
    layers, d, n_in = w_in.shape
    wt = jnp.swapaxes(w_in, 1, 2)
    rb = IN_ROW_BLOCK
    assert MLA_ROPE == rb
    a_blocks = (Q_LORA + KV_LORA + MLA_ROPE) // rb
    b_blocks = 3 * CONV_CH // rb
    c_blocks = n_in // rb - a_blocks - b_blocks
    last = a_blocks + b_blocks + c_blocks

    def src(l, g):
        return (l, jnp.where(g == last, a_blocks - 1, g), 0)

    def dst_a(l, g):
        return (l, jnp.where(g == last, a_blocks, jnp.minimum(g, a_blocks - 1)), 0)

    def dst_b(l, g):
        return (l, jnp.clip(g - a_blocks, 0, b_blocks - 1), 0)

    def dst_c(l, g):
        return (l, jnp.clip(g - a_blocks - b_blocks, 0, c_blocks - 1), 0)

    blk = (1, rb, d)
    return pl.pallas_call(
        functools.partial(_split_in_weights_kernel, a_blocks=a_blocks, b_blocks=b_blocks,
                          c_blocks=c_blocks),
        grid=(layers, last + 1),
        in_specs=[pl.BlockSpec(blk, src)],
        out_specs=[pl.BlockSpec(blk, dst_a), pl.BlockSpec(blk, dst_b), pl.BlockSpec(blk, dst_c)],
        out_shape=[jax.ShapeDtypeStruct((layers, nb * rb, d), BF16)
                   for nb in (a_blocks + 1, b_blocks, c_blocks)],
        compiler_params=_params("arbitrary", "arbitrary"),
        name="split_in_weights",
    )(wt)


def _prep_weights(w_in, w_uq, w_ukv, w_out):
    layers = w_in.shape[0]
    w_a, w_b, w_c = split_in_weights(w_in)
    wq = w_uq.reshape(layers, Q_LORA, MLA_HEADS, MLA_NOPE + MLA_ROPE)
    wq_rope = wq[..., MLA_NOPE:]
    wq = jnp.concatenate([wq, _rot_cols(wq_rope)], axis=-1).transpose(0, 2, 3, 1).astype(BF16)
    return dict(w_a=w_a, w_b=w_b, w_c=w_c, w_q=wq, w_kv=w_ukv.astype(BF16),
                w_out=w_out.astype(BF16))


def _rope_tables(seq, dim):
    inv = 1.0 / (ROPE_THETA ** (jnp.arange(0, dim, 2, dtype=F32) / dim))
    ang = jnp.arange(seq, dtype=F32)[:, None] * inv[None, :]
    return jnp.cos(ang), jnp.sin(ang)


def kernel(x, c, w_mod, mod_table, mix_norm_g, w_in, q_norm_g, kv_norm_g, w_uq, w_ukv, conv_w,
           group_norm_g, w_out, ffn_norm_g, w_router, router_bias, w_gate, w_up, w_down,
           final_norm_g):
    batch, seq, d = x.shape
    depth = w_in.shape[0]
    t = batch * seq
    x2 = x.reshape(t, d)

    cos_a, sin_a = _rope_tables(seq, MLA_ROPE)
    zeros_a = jnp.zeros_like(cos_a)
    cos_p = jnp.concatenate([cos_a, cos_a, zeros_a, zeros_a], axis=1)
    cos_t = jnp.concatenate([cos_a, cos_a], axis=1).T
    sin_t = jnp.concatenate([sin_a, sin_a], axis=1).T
    sin_p = jnp.concatenate([sin_a, sin_a, zeros_a, zeros_a], axis=1)
    cos_b, sin_b = _rope_tables(seq, MOBA_HD)
    cos_f = jnp.concatenate([cos_b, cos_b], axis=1)
    sin_s = jnp.concatenate([-sin_b, sin_b], axis=1)

    c_pad = jnp.zeros((SUBLANE, d), F32).at[:batch].set(c)
    mod_shared = mod_matmul(c_pad, w_mod)[:batch].reshape(batch, N_MOD, d)

    wr_t = w_router.T
    wr_hi = wr_t.astype(BF16)
    wr_lo = (wr_t - wr_hi.astype(F32)).astype(BF16)
    router = (wr_hi, wr_lo, router_bias.astype(F32).reshape(N_EXPERTS, 1))
    mla_scale = (MLA_NOPE + MLA_ROPE) ** -0.5

    mods = []
    for l in range(depth):
        mod = mod_shared + mod_table[l][None]
        mods.append([mod[:, i][:, None, :] for i in range(N_MOD)])

    p = _prep_weights(w_in, w_uq, w_ukv, w_out)
    h = norm_mod(x2, mix_norm_g[0], 1.0 + mods[0][1], mods[0][0], seq)
    for l in range(depth):
        sh1, sc1, g1, sh2, sc2, g2 = mods[l]

        proj_a = matmul(h, p["w_a"], l, F32, tm=512)
        bch = matmul(h, p["w_b"], l, F32)
        qkv = matmul(h, p["w_c"], l, F32)

        q_a = mla_q_up(proj_a, q_norm_g[l], p["w_q"], l, cos_t, sin_t, seq, mla_scale)
        k_a, v_a = mla_kv_up(proj_a, kv_norm_g[l], p["w_kv"], l, cos_p, sin_p, seq)
        y_a = mla_attention(q_a, k_a, v_a, batch, seq, BF16)

        y_b = conv_mixer(bch, conv_w[l], seq, F32)

        q_c, k_c, v_c, k_mean = moba_prep(qkv, cos_f, sin_s, seq)
        n_blk = seq // MOBA_BLOCK
        k_mean = k_mean.reshape(batch, n_blk, MOBA_HEADS, MOBA_HD).transpose(0, 2, 1, 3)
        y_c = moba_attention(q_c, k_c, v_c, k_mean, batch, seq, BF16)

        x2 = out_proj(y_a, y_b, y_c, group_norm_g[l], p["w_out"], l, x2, g1, seq)

        h2, e_idx, gates = norm_mod(x2, ffn_norm_g[l], 1.0 + sc2, sh2, seq, router=router)
        pos, slot_tok, block_exp, block_first, block_rows, block_next, n_used = _group_by_expert(e_idx)
        xs = gather_rows_cast(h2, slot_tok, n_used * (MOE_BLOCK // GATHER_ROWS), BF16)
        ys = expert_ffn(xs, block_exp, block_first, block_rows, block_next, w_gate, w_up, w_down,
                        l)
        if l + 1 < depth:
            nxt = mods[l + 1]
            x2, h = moe_combine(x2, g2, ys, pos.reshape(-1), gates.T, seq, mix_norm_g[l + 1],
                                1.0 + nxt[1], nxt[0])
        else:
            out = moe_combine(x2, g2, ys, pos.reshape(-1), gates.T, seq, final_norm_g)

    return out.reshape(batch, seq, d)
```

```python
import functools

import jax
import jax.numpy as jnp
from jax import lax
from jax.experimental import pallas as pl
from jax.experimental.pallas import tpu as pltpu

MLA_HEADS = 16
MLA_NOPE = 128
MLA_ROPE = 64
MLA_V = 128
Q_LORA = 1024
KV_LORA = 512
MLA_DK_PAD = 256
CONV_CH = 1024
CONV_W = 3
MOBA_HEADS = 8
MOBA_HD = 128
MOBA_BLOCK = 256
MOBA_TOPK = 3
ROPE_THETA = 10000.0
EPS = 1e-6
N_MOD = 6
N_EXPERTS = 16
N_GROUPS = 4
TOP_K = 2
MOE_BLOCK = 512
GATHER_ROWS = 256
FF_SPLIT = 2
UP_HEADS_PER_STEP = 8
ATTN_HEADS_PER_STEP = 4
ATTN_TILE = 512
ONES_ROWS = 16

V7X_VMEM_LIMIT_BYTES = 56 * 1024 * 1024
LANE = 128
SUBLANE = 8

F32 = jnp.float32
BF16 = jnp.bfloat16
PACKED = jnp.uint32
NEG_INF = float("-inf")


def _params(*sem):
    return pltpu.CompilerParams(dimension_semantics=sem, vmem_limit_bytes=V7X_VMEM_LIMIT_BYTES)


def _dot(a, b, precision=None):
    return jnp.dot(a, b, preferred_element_type=F32, precision=precision)


def _dot_nt(a, b):
    return lax.dot_general(a, b, (((1,), (1,)), ((), ())), preferred_element_type=F32)


def _tile(dim, want):
    return want if dim % want == 0 else dim


def _mod_kernel(c_ref, w0_ref, w1_ref, o_ref):
    @pl.when(pl.program_id(0) == 0)
    def _():
        o_ref[...] = jnp.zeros_like(o_ref)

    c = c_ref[...]
    a = (c * jax.nn.sigmoid(c)).astype(BF16)
    half = w0_ref.shape[1]
    o_ref[:, :half] += _dot(a, w0_ref[...].astype(BF16))
    o_ref[:, half:] += _dot(a, w1_ref[...].astype(BF16))


def mod_matmul(c_pad, w_mod):
    rows, d = c_pad.shape
    n = w_mod.shape[1]
    tk = _tile(d, LANE)
    return pl.pallas_call(
        _mod_kernel,
        grid=(d // tk,),
        in_specs=[pl.BlockSpec((rows, tk), lambda k: (0, k)),
                  pl.BlockSpec((tk, n // 2), lambda k: (k, 0)),
                  pl.BlockSpec((tk, n // 2), lambda k: (k, 1))],
        out_specs=pl.BlockSpec((rows, n), lambda k: (0, 0)),
        out_shape=jax.ShapeDtypeStruct((rows, n), F32),
        compiler_params=_params("arbitrary"),
        name="mod_matmul",
    )(c_pad, w_mod, w_mod)


def _rms(x, g):
    return x * lax.rsqrt(jnp.mean(x * x, axis=-1, keepdims=True) + EPS) * g


def _pack_bf16_pair(lo, hi):
    lo_bits = lax.bitcast_convert_type(lo.astype(BF16).astype(F32), PACKED) >> 16
    hi_bits = lax.bitcast_convert_type(hi.astype(BF16).astype(F32), PACKED) & jnp.uint32(0xFFFF0000)
    return hi_bits | lo_bits


def _unpack_bf16_pair(packed):
    lo = lax.bitcast_convert_type(packed << 16, F32)
    hi = lax.bitcast_convert_type(packed & jnp.uint32(0xFFFF0000), F32)
    return lo, hi


def _norm_mod_kernel(x_ref, g_ref, sc_ref, sh_ref, o_ref):
    y = _rms(x_ref[...], g_ref[...])
    o_ref[...] = (y * sc_ref[0] + sh_ref[0]).astype(o_ref.dtype)


def _top2_sum(a, b, c, d):
    hi1, lo1 = jnp.maximum(a, b), jnp.minimum(a, b)
    hi2, lo2 = jnp.maximum(c, d), jnp.minimum(c, d)
    return jnp.maximum(hi1, hi2) + jnp.maximum(jnp.minimum(hi1, hi2), jnp.maximum(lo1, lo2))


def _norm_router_kernel(x_ref, g_ref, sc_ref, sh_ref, whi_ref, wlo_ref, bias_ref,
                        o_ref, e_ref, gate_ref):
    h = _rms(x_ref[...], g_ref[...]) * sc_ref[0] + sh_ref[0]
    hi = h.astype(BF16)
    o_ref[...] = _pack_bf16_pair(h[:, :h.shape[1] // 2], h[:, h.shape[1] // 2:])
    lo = (h - hi.astype(F32)).astype(BF16)
    whl = jnp.concatenate([whi_ref[...], wlo_ref[...]], axis=0)
    both = _dot_nt(whl, hi)
    n_e = whi_ref.shape[0]
    logits = both[:n_e] + both[n_e:] + _dot_nt(whi_ref[...], lo)
    scores = jax.nn.sigmoid(logits)
    biased = scores + bias_ref[...]
    per_grp = N_EXPERTS // N_GROUPS
    b_rows = [biased[e:e + 1, :] for e in range(N_EXPERTS)]
    s_rows = [scores[e:e + 1, :] for e in range(N_EXPERTS)]
    grp_scores = [_top2_sum(*b_rows[g * per_grp:(g + 1) * per_grp]) for g in range(N_GROUPS)]
    best, grp = grp_scores[0], jnp.zeros(grp_scores[0].shape, jnp.int32)
    for g in range(1, N_GROUPS):
        better = grp_scores[g] > best
        grp = jnp.where(better, g, grp)
        best = jnp.where(better, grp_scores[g], best)
    b4, s4 = b_rows[:per_grp], s_rows[:per_grp]
    for g in range(1, N_GROUPS):
        in_g = grp == g
        b4 = [jnp.where(in_g, b_rows[g * per_grp + j], b4[j]) for j in range(per_grp)]
        s4 = [jnp.where(in_g, s_rows[g * per_grp + j], s4[j]) for j in range(per_grp)]
    v1, i1 = b4[0], jnp.zeros(grp.shape, jnp.int32)
    for j in range(1, per_grp):
        better = b4[j] > v1
        i1 = jnp.where(better, j, i1)
        v1 = jnp.where(better, b4[j], v1)
    v2, i2 = jnp.full(v1.shape, NEG_INF, F32), jnp.zeros(grp.shape, jnp.int32)
    for j in range(per_grp):
        better = (i1 != j) & (b4[j] > v2)
        i2 = jnp.where(better, j, i2)
        v2 = jnp.where(better, b4[j], v2)
    g1, g2 = s4[0], s4[0]
    for j in range(1, per_grp):
        g1 = jnp.where(i1 == j, s4[j], g1)
        g2 = jnp.where(i2 == j, s4[j], g2)
    total = g1 + g2
    e_ref[...] = jnp.concatenate([grp * per_grp + i1, grp * per_grp + i2], axis=0)
    gate_ref[...] = jnp.concatenate([g1 / total, g2 / total], axis=0)


def norm_mod(x2, g, sc1p, sh, seq, out_dtype=BF16, router=None):
    t, d = x2.shape
    tm = _tile(seq, 256)
    per_b = seq // tm
    row = pl.BlockSpec((tm, d), lambda i: (i, 0))
    vec = pl.BlockSpec((1, d), lambda i: (0, 0))
    mod = pl.BlockSpec((1, 1, d), lambda i: (i // per_b, 0, 0))
    g2 = g.reshape(1, d)
    if router is None:
        return pl.pallas_call(
            _norm_mod_kernel, grid=(t // tm,), in_specs=[row, vec, mod, mod], out_specs=row,
            out_shape=jax.ShapeDtypeStruct((t, d), out_dtype),
            compiler_params=_params("arbitrary"), name="norm_mod")(x2, g2, sc1p, sh)
    whi_t, wlo_t, bias = router
    ne = whi_t.shape[0]
    wspec = pl.BlockSpec((ne, d), lambda i: (0, 0))
    kspec = pl.BlockSpec((TOP_K, tm), lambda i: (0, i))
    return pl.pallas_call(
        _norm_router_kernel, grid=(t // tm,),
        in_specs=[row, vec, mod, mod, wspec, wspec, pl.BlockSpec((ne, 1), lambda i: (0, 0))],
        out_specs=[pl.BlockSpec((tm, d // 2), lambda i: (i, 0)), kspec, kspec],
        out_shape=[jax.ShapeDtypeStruct((t, d // 2), PACKED),
                   jax.ShapeDtypeStruct((TOP_K, t), jnp.int32),
                   jax.ShapeDtypeStruct((TOP_K, t), F32)],
        compiler_params=_params("arbitrary"), name="norm_router")(x2, g2, sc1p, sh, whi_t, wlo_t, bias)


def _mm_kernel(a_ref, wt_ref, o_ref):
    o_ref[...] = _dot_nt(a_ref[...], wt_ref[0]).astype(o_ref.dtype)


def matmul(a, wt_layers, layer, out_dtype, tm=1024, tn=1024):
    m, kd = a.shape
    n = wt_layers.shape[1]
    tm, tn = _tile(m, tm), _tile(n, tn)
    return pl.pallas_call(
        _mm_kernel,
        grid=(m // tm, n // tn),
        in_specs=[pl.BlockSpec((tm, kd), lambda i, j: (i, 0)),
                  pl.BlockSpec((1, tn, kd), lambda i, j: (layer, j, 0))],
        out_specs=pl.BlockSpec((tm, tn), lambda i, j: (i, j)),
        out_shape=jax.ShapeDtypeStruct((m, n), out_dtype),
        compiler_params=_params("parallel", "parallel"),
        name="matmul",
    )(a, wt_layers)


def _rope_pair(r2, cos_ref, sin_ref):
    return r2 * cos_ref[...] + pltpu.roll(r2, MLA_ROPE, 1) * sin_ref[...]


def _with_ones_rows(vt):
    return jnp.concatenate([vt, jnp.ones((ONES_ROWS, vt.shape[1]), vt.dtype)], axis=0)


def _qup_kernel(cq_ref, g_ref, wt_ref, cos_ref, sin_ref, o_ref, ant_ref, *, scale):
    @pl.when(pl.program_id(1) == 0)
    def _():
        ant_ref[...] = _rms(cq_ref[...].astype(F32), g_ref[...]).T.astype(BF16)

    ant = ant_ref[...]
    cos, sin = cos_ref[...], sin_ref[...]
    rope_end = MLA_NOPE + MLA_ROPE
    for hh in range(wt_ref.shape[1]):
        res = _dot(wt_ref[0, hh], ant)
        roped = res[MLA_NOPE:rope_end] * cos + res[rope_end:] * sin
        qt = jnp.concatenate([res[:MLA_NOPE], roped, jnp.zeros_like(roped)], axis=0) * scale
        o_ref[hh, 0] = qt.astype(o_ref.dtype)


def mla_q_up(proj_a, g, wt_q, layer, cos_t, sin_t, seq, scale):
    t = proj_a.shape[0]
    heads = wt_q.shape[1]
    hb = UP_HEADS_PER_STEP
    tm = _tile(seq, ATTN_TILE)
    per_b = seq // tm
    return pl.pallas_call(
        functools.partial(_qup_kernel, scale=scale),
        grid=(t // tm, heads // hb),
        in_specs=[pl.BlockSpec((tm, Q_LORA), lambda i, h: (i, 0)),
                  pl.BlockSpec((1, Q_LORA), lambda i, h: (0, 0)),
                  pl.BlockSpec((1, hb, MLA_DK_PAD, Q_LORA), lambda i, h: (layer, h, 0, 0)),
                  pl.BlockSpec((MLA_ROPE, tm), lambda i, h: (0, i % per_b)),
                  pl.BlockSpec((MLA_ROPE, tm), lambda i, h: (0, i % per_b))],
        out_specs=pl.BlockSpec((hb, 1, MLA_DK_PAD, tm), lambda i, h: (h, i, 0, 0)),
        out_shape=jax.ShapeDtypeStruct((heads, t // tm, MLA_DK_PAD, tm), BF16),
        scratch_shapes=[pltpu.VMEM((Q_LORA, tm), BF16)],
        compiler_params=_params("parallel", "arbitrary"),
        name="mla_q_up",
    )(proj_a, g.reshape(1, Q_LORA), wt_q, cos_t, sin_t)


def _kvup_kernel(ckv_ref, g_ref, w_ref, kpe_ref, cos_ref, sin_ref, k_ref, vt_ref, an_ref):
    @pl.when(pl.program_id(1) == 0)
    def _():
        an_ref[...] = _rms(ckv_ref[...].astype(F32), g_ref[...]).astype(BF16)

    an = an_ref[...]
    k_rot = _rope_pair(kpe_ref[...].astype(F32), cos_ref, sin_ref)
    hw = MLA_NOPE + MLA_V
    for hh in range(k_ref.shape[0]):
        res = _dot(an, w_ref[0, :, hh * hw:(hh + 1) * hw])
        k_ref[hh] = jnp.concatenate([res[:, :MLA_NOPE], k_rot], axis=1).astype(k_ref.dtype)
        vt_ref[hh, 0] = _with_ones_rows(res[:, MLA_NOPE:].T).astype(vt_ref.dtype)


def mla_kv_up(proj_a, g, w_kv, layer, cos_p, sin_p, seq):
    t = proj_a.shape[0]
    hw = MLA_NOPE + MLA_V
    heads = w_kv.shape[2] // hw
    hb = UP_HEADS_PER_STEP
    tm = _tile(seq, ATTN_TILE)
    per_b = seq // tm
    return pl.pallas_call(
        _kvup_kernel,
        grid=(t // tm, heads // hb),
        in_specs=[pl.BlockSpec((tm, KV_LORA), lambda i, h: (i, Q_LORA // KV_LORA)),
                  pl.BlockSpec((1, KV_LORA), lambda i, h: (0, 0)),
                  pl.BlockSpec((1, KV_LORA, hb * hw), lambda i, h: (layer, 0, h)),
                  pl.BlockSpec((tm, LANE), lambda i, h: (i, (Q_LORA + KV_LORA) // LANE)),
                  pl.BlockSpec((tm, LANE), lambda i, h: (i % per_b, 0)),
                  pl.BlockSpec((tm, LANE), lambda i, h: (i % per_b, 0))],
        out_specs=[pl.BlockSpec((hb, tm, MLA_DK_PAD), lambda i, h: (h, i, 0)),
                   pl.BlockSpec((hb, 1, MLA_V + ONES_ROWS, tm), lambda i, h: (h, i, 0, 0))],
        out_shape=[jax.ShapeDtypeStruct((heads, t, MLA_DK_PAD), BF16),
                   jax.ShapeDtypeStruct((heads, t // tm, MLA_V + ONES_ROWS, tm), BF16)],
        scratch_shapes=[pltpu.VMEM((tm, KV_LORA), BF16)],
        compiler_params=_params("parallel", "arbitrary"),
        name="mla_kv_up",
    )(proj_a, g.reshape(1, KV_LORA), w_kv, proj_a, cos_p, sin_p)


def _pv(vts, p):
    rows = p.shape[0] // len(vts)
    out = _dot(vts[0], p[:rows].astype(vts[0].dtype))
    for n in range(1, len(vts)):
        out = out + _dot(vts[n], p[n * rows:(n + 1) * rows].astype(vts[n].dtype))
    return out


def _softmax_first(s, vts, m_ref, acc_ref):
    m = jnp.max(s, axis=0, keepdims=True)
    m_ref[...] = m
    acc_ref[...] = _pv(vts, jnp.exp(s - m))


def _softmax_step(s, vts, m_ref, acc_ref):
    m_prev = m_ref[...]
    m_new = jnp.maximum(m_prev, jnp.max(s, axis=0, keepdims=True))
    alpha = jnp.exp(m_prev - m_new)
    acc_ref[...] = alpha * acc_ref[...] + _pv(vts, jnp.exp(s - m_new))
    m_ref[...] = m_new


def _softmax_finish(acc_ref, dv):
    acc = acc_ref[...]
    return (acc[:dv, :] / acc[dv:dv + 1, :]).T


def _causal_mask_t(s):
    key = lax.broadcasted_iota(jnp.int32, s.shape, 0)
    qry = lax.broadcasted_iota(jnp.int32, s.shape, 1)
    return jnp.where(key <= qry, s, NEG_INF)


def _mla_attn_kernel(qt_ref, k_ref, vt_ref, o_ref, m_ref, acc_ref, *, tq, dv):
    i = pl.program_id(2)
    n_h = qt_ref.shape[0]
    qts = [qt_ref[h, 0] for h in range(n_h)]

    def scores(h, j, n):
        off = pl.multiple_of(j * tq, tq)
        return _dot(k_ref[h, pl.ds(off, n * tq), :], qts[h])

    for h in range(n_h):
        _softmax_first(_causal_mask_t(scores(h, i, 1)), [vt_ref[h, i]], m_ref.at[h], acc_ref.at[h])

    @pl.when(i % 2 == 1)
    def _():
        for h in range(n_h):
            _softmax_step(scores(h, i - 1, 1), [vt_ref[h, i - 1]], m_ref.at[h], acc_ref.at[h])

    def body(c, carry):
        for h in range(n_h):
            _softmax_step(scores(h, 2 * c, 2), [vt_ref[h, 2 * c], vt_ref[h, 2 * c + 1]],
                          m_ref.at[h], acc_ref.at[h])
        return carry

    lax.fori_loop(0, i // 2, body, 0)
    for h in range(n_h):
        o_ref[:, h * dv:(h + 1) * dv] = _softmax_finish(acc_ref.at[h], dv).astype(o_ref.dtype)


def mla_attention(qt, k, vt, batch, seq, out_dtype):
    heads, _, dk, tq = qt.shape
    t = k.shape[1]
    dvx = vt.shape[2]
    dv = dvx - ONES_ROWS
    nq = seq // tq
    hb = ATTN_HEADS_PER_STEP
    return pl.pallas_call(
        functools.partial(_mla_attn_kernel, tq=tq, dv=dv),
        grid=(batch, heads // hb, nq),
        in_specs=[pl.BlockSpec((hb, 1, dk, tq), lambda b, h, i: (h, b * nq + i, 0, 0)),
                  pl.BlockSpec((hb, seq, dk), lambda b, h, i: (h, b, 0)),
                  pl.BlockSpec((hb, nq, dvx, tq), lambda b, h, i: (h, b, 0, 0))],
        out_specs=pl.BlockSpec((tq, hb * dv), lambda b, h, i: (b * nq + i, h)),
        out_shape=jax.ShapeDtypeStruct((t, heads * dv), out_dtype),
        scratch_shapes=[pltpu.VMEM((hb, 1, tq), F32), pltpu.VMEM((hb, dvx, tq), F32)],
        compiler_params=_params("parallel", "parallel", "arbitrary"),
        name="mla_attention",
    )(qt, k, vt)


def _conv_kernel(b_ref, c_ref, h_ref, w_ref, o_ref, carry_ref, *, per_b):
    i = pl.program_id(0)

    @pl.when(i % per_b == 0)
    def _():
        carry_ref[...] = jnp.zeros_like(carry_ref)

    tm = o_ref.shape[0]
    w0, w1, w2 = w_ref[0:1, :], w_ref[1:2, :], w_ref[2:3, :]
    u = c_ref[...].astype(F32) * h_ref[...].astype(F32)
    y = w0 * pltpu.roll(u, 2, 0) + w1 * pltpu.roll(u, 1, 0) + w2 * u
    o_ref[...] = (b_ref[...].astype(F32) * y).astype(o_ref.dtype)
    u8 = u[0:SUBLANE, :]
    tail = carry_ref[...]
    r8 = lax.broadcasted_iota(jnp.int32, u8.shape, 0)
    p1 = jnp.where(r8 < 1, pltpu.roll(tail, 1, 0), pltpu.roll(u8, 1, 0))
    p2 = jnp.where(r8 < 2, pltpu.roll(tail, 2, 0), pltpu.roll(u8, 2, 0))
    y8 = w0 * p2 + w1 * p1 + w2 * u8
    o_ref[0:SUBLANE, :] = (b_ref[0:SUBLANE, :].astype(F32) * y8).astype(o_ref.dtype)
    carry_ref[...] = u[tm - SUBLANE:tm, :]


def conv_mixer(bch, conv_w, seq, out_dtype):
    t = bch.shape[0]
    ch = conv_w.shape[1]
    tm = _tile(seq, 256)
    per_b = seq // tm
    return pl.pallas_call(
        functools.partial(_conv_kernel, per_b=per_b),
        grid=(t // tm,),
        in_specs=[pl.BlockSpec((tm, ch), lambda i: (i, 0)),
                  pl.BlockSpec((tm, ch), lambda i: (i, 1)),
                  pl.BlockSpec((tm, ch), lambda i: (i, 2)),
                  pl.BlockSpec((CONV_W, ch), lambda i: (0, 0))],
        out_specs=pl.BlockSpec((tm, ch), lambda i: (i, 0)),
        out_shape=jax.ShapeDtypeStruct((t, ch), out_dtype),
        scratch_shapes=[pltpu.VMEM((SUBLANE, ch), F32)],
        compiler_params=_params("arbitrary"),
        name="conv_mixer",
    )(bch, bch, bch, conv_w)


def _moba_prep_kernel(q_ref, k_ref, v_ref, cos_ref, sin_ref, qo_ref, ko_ref, vo_ref, km_ref):
    heads = ko_ref.shape[0]
    cos, sin = cos_ref[...], sin_ref[...]
    n_sub = km_ref.shape[0]
    means = [[] for _ in range(n_sub)]
    for h in range(heads):
        sl = slice(h * MOBA_HD, (h + 1) * MOBA_HD)
        qh = q_ref[:, sl].astype(F32)
        kh = k_ref[:, sl].astype(F32)
        qr = qh * cos + pltpu.roll(qh, MOBA_HD // 2, 1) * sin
        qo_ref[h, 0] = qr.T.astype(qo_ref.dtype)
        kr = kh * cos + pltpu.roll(kh, MOBA_HD // 2, 1) * sin
        ko_ref[h] = kr.astype(ko_ref.dtype)
        vo_ref[h, 0] = _with_ones_rows(v_ref[:, sl].astype(F32).T).astype(vo_ref.dtype)
        for s in range(n_sub):
            means[s].append(jnp.mean(kr[s * MOBA_BLOCK:(s + 1) * MOBA_BLOCK], axis=0, keepdims=True))
    for s in range(n_sub):
        km_ref[s] = jnp.concatenate(means[s], axis=0)


def moba_prep(qkv, cos_f, sin_s, seq):
    t = qkv.shape[0]
    width = qkv.shape[1] // 3
    heads = width // MOBA_HD
    tm = 2 * MOBA_BLOCK
    assert seq % tm == 0
    per_b = seq // tm
    nt = t // tm
    return pl.pallas_call(
        _moba_prep_kernel,
        grid=(nt,),
        in_specs=[pl.BlockSpec((tm, width), lambda i: (i, 0)),
                  pl.BlockSpec((tm, width), lambda i: (i, 1)),
                  pl.BlockSpec((tm, width), lambda i: (i, 2)),
                  pl.BlockSpec((tm, MOBA_HD), lambda i: (i % per_b, 0)),
                  pl.BlockSpec((tm, MOBA_HD), lambda i: (i % per_b, 0))],
        out_specs=[pl.BlockSpec((heads, 1, MOBA_HD, tm), lambda i: (0, i, 0, 0)),
                   pl.BlockSpec((heads, tm, MOBA_HD), lambda i: (0, i, 0)),
                   pl.BlockSpec((heads, 1, MOBA_HD + ONES_ROWS, tm), lambda i: (0, i, 0, 0)),
                   pl.BlockSpec((2, heads, MOBA_HD), lambda i: (i, 0, 0))],
        out_shape=[jax.ShapeDtypeStruct((heads, nt, MOBA_HD, tm), BF16),
                   jax.ShapeDtypeStruct((heads, t, MOBA_HD), BF16),
                   jax.ShapeDtypeStruct((heads, nt, MOBA_HD + ONES_ROWS, tm), BF16),
                   jax.ShapeDtypeStruct((2 * nt, heads, MOBA_HD), F32)],
        compiler_params=_params("arbitrary"),
        name="moba_prep",
    )(qkv, qkv, qkv, cos_f, sin_s)


def _moba_attn_kernel(qt_ref, k_ref, vt_ref, km_ref, o_ref, m_ref, acc_ref, sel_ref, *, scale, dv):
    i = pl.program_id(2)
    n_h = qt_ref.shape[0]
    n_blk = km_ref.shape[2]
    tq = qt_ref.shape[3]
    blk_w = MOBA_BLOCK
    blk = lax.broadcasted_iota(jnp.int32, (n_blk, tq), 0)
    own = 2 * i + (lax.broadcasted_iota(jnp.int32, (n_blk, tq), 1) >= blk_w).astype(jnp.int32)
    past = blk < own
    qts = []
    for h in range(n_h):
        qtf = qt_ref[h, 0].astype(F32)
        g = _dot(km_ref[0, h], qtf, precision=lax.Precision.HIGHEST)
        g = jnp.where(past, g, NEG_INF)
        sel = jnp.zeros(g.shape, F32)
        for _ in range(min(MOBA_TOPK, n_blk)):
            mx = jnp.max(g, axis=0, keepdims=True)
            first = jnp.min(jnp.where(g == mx, blk, n_blk), axis=0, keepdims=True)
            hit = blk == first
            sel = jnp.where(hit, 1.0, sel)
            g = jnp.where(hit, NEG_INF, g)
        sel_ref[h] = jnp.where(past, sel, 0.0)
        qts.append((qtf * scale).astype(k_ref.dtype))

    def scores(h, p, n):
        off = pl.multiple_of(p * tq, tq)
        return _dot(k_ref[h, pl.ds(off, n * tq), :], qts[h])

    def chosen(h, n):
        return sel_ref[h, pl.ds(n, 1), :] > 0.5

    def mask_past(h, s, first_blk):
        parts = [jnp.where(chosen(h, first_blk + n), s[n * blk_w:(n + 1) * blk_w], NEG_INF)
                 for n in range(s.shape[0] // blk_w)]
        return jnp.concatenate(parts, axis=0)

    key = lax.broadcasted_iota(jnp.int32, (blk_w, tq), 0)
    qry = lax.broadcasted_iota(jnp.int32, (blk_w, tq), 1)
    bot_ok = key <= qry - blk_w
    for h in range(n_h):
        s = scores(h, i, 1)
        chosen_lim = jnp.where(chosen(h, 2 * i), blk_w, -1)
        top_ok = key <= jnp.where(qry < blk_w, qry, chosen_lim)
        s = jnp.concatenate([jnp.where(top_ok, s[:blk_w], NEG_INF),
                             jnp.where(bot_ok, s[blk_w:], NEG_INF)], axis=0)
        _softmax_first(s, [vt_ref[h, i]], m_ref.at[h], acc_ref.at[h])

    @pl.when(i % 2 == 1)
    def _():
        for h in range(n_h):
            s = mask_past(h, scores(h, i - 1, 1), 2 * (i - 1))
            _softmax_step(s, [vt_ref[h, i - 1]], m_ref.at[h], acc_ref.at[h])

    def body(c, carry):
        for h in range(n_h):
            s = mask_past(h, scores(h, 2 * c, 2), 4 * c)
            _softmax_step(s, [vt_ref[h, 2 * c], vt_ref[h, 2 * c + 1]], m_ref.at[h], acc_ref.at[h])
        return carry

    lax.fori_loop(0, i // 2, body, 0)
    for h in range(n_h):
        o_ref[:, h * dv:(h + 1) * dv] = _softmax_finish(acc_ref.at[h], dv).astype(o_ref.dtype)


def moba_attention(qt, k, vt, k_mean, batch, seq, out_dtype):
    heads, _, hd, tq = qt.shape
    t = k.shape[1]
    dvx = vt.shape[2]
    dv = dvx - ONES_ROWS
    n_blk = seq // MOBA_BLOCK
    nq = seq // tq
    hb = ATTN_HEADS_PER_STEP
    return pl.pallas_call(
        functools.partial(_moba_attn_kernel, scale=hd ** -0.5, dv=dv),
        grid=(batch, heads // hb, nq),
        in_specs=[pl.BlockSpec((hb, 1, hd, tq), lambda b, h, i: (h, b * nq + i, 0, 0)),
                  pl.BlockSpec((hb, seq, hd), lambda b, h, i: (h, b, 0)),
                  pl.BlockSpec((hb, nq, dvx, tq), lambda b, h, i: (h, b, 0, 0)),
                  pl.BlockSpec((1, hb, n_blk, hd), lambda b, h, i: (b, h, 0, 0))],
        out_specs=pl.BlockSpec((tq, hb * dv), lambda b, h, i: (b * nq + i, h)),
        out_shape=jax.ShapeDtypeStruct((t, heads * dv), out_dtype),
        scratch_shapes=[pltpu.VMEM((hb, 1, tq), F32), pltpu.VMEM((hb, dvx, tq), F32),
                        pltpu.VMEM((hb, n_blk, tq), F32)],
        compiler_params=_params("parallel", "parallel", "arbitrary"),
        name="moba_attention",
    )(qt, k, vt, k_mean)


def _out_proj_kernel(a_ref, b_ref, c_ref, ga_ref, gb_ref, gc_ref, w_ref, x_ref, gate_ref, o_ref,
                     y_ref):
    @pl.when(pl.program_id(1) == 0)
    def _():
        wa, wb = a_ref.shape[1], b_ref.shape[1]
        y_ref[:, 0:wa] = _rms(a_ref[...].astype(F32), ga_ref[...]).astype(y_ref.dtype)
        y_ref[:, wa:wa + wb] = _rms(b_ref[...].astype(F32), gb_ref[...]).astype(y_ref.dtype)
        y_ref[:, wa + wb:] = _rms(c_ref[...].astype(F32), gc_ref[...]).astype(y_ref.dtype)

    o_ref[...] = x_ref[...] + gate_ref[0] * _dot(y_ref[...], w_ref[0])


def out_proj(y_a, y_b, y_c, g, w_layers, layer, x2, gate, seq, tm=1024, tn=512):
    t = y_a.shape[0]
    wa, wb, wc = y_a.shape[1], y_b.shape[1], y_c.shape[1]
    _, kd, n = w_layers.shape
    tm, tn = _tile(seq, tm), _tile(n, tn)
    per_b = seq // tm
    ga, gb, gc = g[:wa].reshape(1, wa), g[wa:wa + wb].reshape(1, wb), g[wa + wb:].reshape(1, wc)
    rows = lambda width: pl.BlockSpec((tm, width), lambda i, j: (i, 0))
    vec = lambda width: pl.BlockSpec((1, width), lambda i, j: (0, 0))
    return pl.pallas_call(
        _out_proj_kernel,
        grid=(t // tm, n // tn),
        in_specs=[rows(wa), rows(wb), rows(wc), vec(wa), vec(wb), vec(wc),
                  pl.BlockSpec((1, kd, tn), lambda i, j: (layer, 0, j)),
                  pl.BlockSpec((tm, tn), lambda i, j: (i, j)),
                  pl.BlockSpec((1, 1, tn), lambda i, j: (i // per_b, 0, j))],
        out_specs=pl.BlockSpec((tm, tn), lambda i, j: (i, j)),
        out_shape=jax.ShapeDtypeStruct((t, n), F32),
        scratch_shapes=[pltpu.VMEM((tm, kd), BF16)],
        compiler_params=_params("parallel", "arbitrary"),
        name="out_proj",
    )(y_a, y_b, y_c, ga, gb, gc, w_layers, x2, gate)


def _expert_weight_copies(w_hbm, layer, expert, col, stage, sem):
    width = stage.shape[1]
    return pltpu.make_async_copy(w_hbm.at[layer, expert, :, pl.ds(col, width)], stage, sem)


def _ffn_up_kernel(bexp_ref, first_ref, rows_ref, next_ref, x_ref, wg_hbm, wu_hbm, a_ref,
                   wg_st, wu_st, wg_bf, wu_bf, sem, *, layer):
    j = pl.program_id(0)
    b = pl.program_id(1)
    n_rows = rows_ref[b]
    half = x_ref.shape[0] // 2
    col = pl.multiple_of(j * wg_bf.shape[1], wg_bf.shape[1])

    def copies(expert):
        return (_expert_weight_copies(wg_hbm, layer, expert, col, wg_st, sem.at[0]),
                _expert_weight_copies(wu_hbm, layer, expert, col, wu_st, sem.at[1]))

    @pl.when(b == 0)
    def _():
        for c in copies(bexp_ref[0]):
            c.start()

    @pl.when((n_rows > 0) & (first_ref[b] == 1))
    def _():
        for c in copies(bexp_ref[b]):
            c.wait()
        wg_bf[...] = wg_st[...].astype(BF16)
        wu_bf[...] = wu_st[...].astype(BF16)

        @pl.when(next_ref[b] >= 0)
        def _():
            for c in copies(next_ref[b]):
                c.start()

    def act(x):
        gate = _dot(x, wg_bf[...])
        up = _dot(x, wu_bf[...])
        return (gate * jax.nn.sigmoid(gate) * up).astype(a_ref.dtype)

    @pl.when(n_rows > half)
    def _():
        a_ref[...] = act(x_ref[...])

    @pl.when((n_rows > 0) & (n_rows <= half))
    def _():
        a_ref[:half] = act(x_ref[:half])
        a_ref[half:] = jnp.zeros((half, a_ref.shape[1]), a_ref.dtype)

    @pl.when(n_rows == 0)
    def _():
        a_ref[...] = jnp.zeros_like(a_ref)


def _ffn_down_kernel(bexp_ref, first_ref, rows_ref, next_ref, a_ref, wd_hbm, o_ref,
                     wd_st, wd_bf, sem, *, layer):
    j = pl.program_id(0)
    b = pl.program_id(1)
    n_rows = rows_ref[b]
    half = a_ref.shape[0] // 2
    col = pl.multiple_of(j * wd_bf.shape[1], wd_bf.shape[1])

    def copy(expert):
        return _expert_weight_copies(wd_hbm, layer, expert, col, wd_st, sem.at[0])

    @pl.when(b == 0)
    def _():
        copy(bexp_ref[0]).start()

    @pl.when((n_rows > 0) & (first_ref[b] == 1))
    def _():
        copy(bexp_ref[b]).wait()
        wd_bf[...] = wd_st[...].astype(BF16)

        @pl.when(next_ref[b] >= 0)
        def _():
            copy(next_ref[b]).start()

    def packed(y):
        mid = y.shape[1] // 2
        return _pack_bf16_pair(y[:, :mid], y[:, mid:])

    @pl.when(n_rows > half)
    def _():
        o_ref[...] = packed(_dot(a_ref[...], wd_bf[...]))

    @pl.when((n_rows > 0) & (n_rows <= half))
    def _():
        o_ref[:half] = packed(_dot(a_ref[:half], wd_bf[...]))
        o_ref[half:] = jnp.zeros((half, o_ref.shape[1]), o_ref.dtype)

    @pl.when(n_rows == 0)
    def _():
        o_ref[...] = jnp.zeros_like(o_ref)


def expert_ffn(xs, block_exp, block_first, block_rows, block_next, w_gate, w_up, w_down, layer):
    n_slots, d = xs.shape
    ff = w_gate.shape[3]
    fh, dh = ff // FF_SPLIT, d // FF_SPLIT
    n_blocks = n_slots // MOE_BLOCK
    hbm = pl.BlockSpec(memory_space=pl.ANY)
    up_spec = pltpu.PrefetchScalarGridSpec(
        num_scalar_prefetch=4,
        grid=(FF_SPLIT, n_blocks),
        in_specs=[pl.BlockSpec((MOE_BLOCK, d), lambda j, b, *_: (b, 0)), hbm, hbm],
        out_specs=pl.BlockSpec((MOE_BLOCK, fh), lambda j, b, *_: (b, j)),
        scratch_shapes=[pltpu.VMEM((d, fh), F32), pltpu.VMEM((d, fh), F32),
                        pltpu.VMEM((d, fh), BF16), pltpu.VMEM((d, fh), BF16),
                        pltpu.SemaphoreType.DMA((2,))],
    )
    act = pl.pallas_call(
        functools.partial(_ffn_up_kernel, layer=layer),
        grid_spec=up_spec,
        out_shape=jax.ShapeDtypeStruct((n_slots, ff), BF16),
        compiler_params=_params("arbitrary", "arbitrary"),
        name="expert_ffn_up",
    )(block_exp, block_first, block_rows, block_next, xs, w_gate, w_up)
    down_spec = pltpu.PrefetchScalarGridSpec(
        num_scalar_prefetch=4,
        grid=(FF_SPLIT, n_blocks),
        in_specs=[pl.BlockSpec((MOE_BLOCK, ff), lambda j, b, *_: (b, 0)), hbm],
        out_specs=pl.BlockSpec((MOE_BLOCK, dh // 2), lambda j, b, *_: (b, j)),
        scratch_shapes=[pltpu.VMEM((ff, dh), F32), pltpu.VMEM((ff, dh), BF16),
                        pltpu.SemaphoreType.DMA((1,))],
    )
    return pl.pallas_call(
        functools.partial(_ffn_down_kernel, layer=layer),
        grid_spec=down_spec,
        out_shape=jax.ShapeDtypeStruct((n_slots, d // 2), PACKED),
        compiler_params=_params("arbitrary", "arbitrary"),
        name="expert_ffn_down",
    )(block_exp, block_first, block_rows, block_next, act, w_down)


def _start_row_copies(idx_ref, base, n_rows, src_ref, dst_ref, sem):
    def issue(r, carry):
        row = idx_ref[base + r]
        pltpu.make_async_copy(src_ref.at[pl.ds(row, 1)], dst_ref.at[pl.ds(r, 1)], sem).start()
        return carry

    lax.fori_loop(0, n_rows, issue, 0, unroll=8)


def _wait_row_copies(n_rows, src_ref, dst_ref, sem):
    pltpu.make_async_copy(src_ref.at[pl.ds(0, n_rows)], dst_ref, sem).wait()


def _gather_cast_kernel(idx_ref, nused_ref, src_ref, o_ref, buf, sem):
    b = pl.program_id(0)
    rows = o_ref.shape[0]
    n_used = nused_ref[0]

    @pl.when((b == 0) & (n_used > 0))
    def _():
        _start_row_copies(idx_ref, 0, rows, src_ref, buf.at[0], sem.at[0])

    @pl.when(b + 1 < n_used)
    def _():
        nxt = (b + 1) % 2
        _start_row_copies(idx_ref, (b + 1) * rows, rows, src_ref, buf.at[nxt], sem.at[nxt])

    @pl.when(b < n_used)
    def _():
        cur = b % 2
        _wait_row_copies(rows, src_ref, buf.at[cur], sem.at[cur])
        lo, hi = _unpack_bf16_pair(buf[cur])
        half = lo.shape[1]
        o_ref[:, :half] = lo.astype(o_ref.dtype)
        o_ref[:, half:] = hi.astype(o_ref.dtype)

    @pl.when(b >= n_used)
    def _():
        o_ref[...] = jnp.zeros_like(o_ref)


def gather_rows_cast(src, idx, n_used_tiles, out_dtype):
    dp = src.shape[1]
    d = 2 * dp
    m = idx.shape[0]
    rows = GATHER_ROWS
    grid_spec = pltpu.PrefetchScalarGridSpec(
        num_scalar_prefetch=2,
        grid=(m // rows,),
        in_specs=[pl.BlockSpec(memory_space=pl.ANY)],
        out_specs=pl.BlockSpec((rows, d), lambda b, ix, nu: (b, 0)),
        scratch_shapes=[pltpu.VMEM((2, rows, dp), src.dtype), pltpu.SemaphoreType.DMA((2,))],
    )
    return pl.pallas_call(
        _gather_cast_kernel,
        grid_spec=grid_spec,
        out_shape=jax.ShapeDtypeStruct((m, d), out_dtype),
        compiler_params=_params("arbitrary"),
        name="gather_rows_cast",
    )(idx, n_used_tiles, src)


def _combine_kernel(pos_ref, x_ref, gate_ref, w_ref, g_ref, sc_ref, sh_ref, ys_ref,
                    *rest, n_tok, n_windows, final):
    if final:
        o_ref, ybuf, sem = rest
    else:
        o_ref, h_ref, ybuf, sem = rest
    i = pl.program_id(0)
    n_tiles = pl.num_programs(0)
    tm = x_ref.shape[0]

    def start_tile(tile, slot):
        for kk in range(TOP_K):
            _start_row_copies(pos_ref, kk * n_tok + tile * tm, tm, ys_ref, ybuf.at[slot, kk],
                              sem.at[slot])

    @pl.when(i == 0)
    def _():
        start_tile(0, 0)

    @pl.when(i + 1 < n_tiles)
    def _():
        start_tile(i + 1, (i + 1) % 2)

    cur = i % 2
    for kk in range(TOP_K):
        _wait_row_copies(tm, ys_ref, ybuf.at[cur, kk], sem.at[cur])
    w = w_ref[...]
    acc = None
    for kk in range(TOP_K):
        lo, hi = _unpack_bf16_pair(ybuf[cur, kk])
        ww = lo.shape[1] // n_windows
        parts = []
        for n in range(n_windows):
            parts += [lo[:, n * ww:(n + 1) * ww], hi[:, n * ww:(n + 1) * ww]]
        term = w[:, kk:kk + 1] * jnp.concatenate(parts, axis=1)
        acc = term if acc is None else acc + term
    x_new = x_ref[...] + gate_ref[0] * acc
    if final:
        o_ref[...] = _rms(x_new, g_ref[...])
    else:
        o_ref[...] = x_new
        h_ref[...] = (_rms(x_new, g_ref[...]) * sc_ref[0] + sh_ref[0]).astype(h_ref.dtype)


def moe_combine(x2, gate, ys, pos, w_tok, seq, norm_g, sc1p=None, sh=None):
    t, d = x2.shape
    tm = _tile(seq, GATHER_ROWS)
    per_b = seq // tm
    final = sc1p is None
    if final:
        sc1p = sh = jnp.zeros((x2.shape[0] // seq, 1, d), F32)
    row = pl.BlockSpec((tm, d), lambda i, ps: (i, 0))
    mod = pl.BlockSpec((1, 1, d), lambda i, ps: (i // per_b, 0, 0))
    tok = pl.BlockSpec((tm, TOP_K), lambda i, ps: (i, 0))
    grid_spec = pltpu.PrefetchScalarGridSpec(
        num_scalar_prefetch=1,
        grid=(t // tm,),
        in_specs=[row, mod, tok, pl.BlockSpec((1, d), lambda i, ps: (0, 0)), mod, mod,
                  pl.BlockSpec(memory_space=pl.ANY)],
        out_specs=row if final else [row, row],
        scratch_shapes=[pltpu.VMEM((2, TOP_K, tm, d // 2), ys.dtype),
                        pltpu.SemaphoreType.DMA((2,))],
    )
    out_shape = jax.ShapeDtypeStruct((t, d), F32)
    return pl.pallas_call(
        functools.partial(_combine_kernel, n_tok=t, n_windows=FF_SPLIT, final=final),
        grid_spec=grid_spec,
        out_shape=out_shape if final else [out_shape, jax.ShapeDtypeStruct((t, d), BF16)],
        compiler_params=_params("arbitrary"),
        name="moe_combine",
    )(pos, x2, gate, w_tok, norm_g.reshape(1, d), sc1p, sh, ys)


def _group_by_expert(e_idx):
    n_tok = e_idx.shape[1]
    n_asg = n_tok * TOP_K
    flat_e = e_idx.reshape(n_asg)
    onehot = (flat_e[:, None] == jnp.arange(N_EXPERTS)[None, :]).astype(jnp.int32)
    running = jnp.cumsum(onehot, axis=0)
    rank = jnp.sum(running * onehot, axis=1) - 1
    sizes = running[-1]
    padded = (sizes + MOE_BLOCK - 1) // MOE_BLOCK * MOE_BLOCK
    pad_end = jnp.cumsum(padded)
    pad_start = pad_end - padded
    dest = (jnp.sum(pad_start[None, :] * onehot, axis=1) + rank).astype(jnp.int32)
    n_blocks = -(-n_asg // MOE_BLOCK) + N_EXPERTS
    n_slots = n_blocks * MOE_BLOCK
    slot_tok = (jnp.arange(n_slots, dtype=jnp.int32) % n_tok).at[dest].set(
        jnp.arange(n_asg, dtype=jnp.int32) % n_tok, mode="promise_in_bounds")
    block_start = jnp.arange(n_blocks, dtype=jnp.int32) * MOE_BLOCK
    block_exp = jnp.minimum(jnp.sum(block_start[:, None] >= pad_end[None, :], axis=1),
                            N_EXPERTS - 1).astype(jnp.int32)
    n_used = (pad_end[-1] // MOE_BLOCK).astype(jnp.int32)
    block_exp = jnp.where(jnp.arange(n_blocks) < n_used, block_exp, block_exp[n_used - 1])
    prev_exp = jnp.concatenate([jnp.full((1,), -1, jnp.int32), block_exp[:-1]])
    block_first = (block_exp != prev_exp).astype(jnp.int32)
    block_rows = jnp.clip(pad_start[block_exp] + sizes[block_exp] - block_start, 0, MOE_BLOCK)
    block_rows = jnp.where(jnp.arange(n_blocks) < n_used, block_rows, 0).astype(jnp.int32)
    blk = jnp.arange(n_blocks, dtype=jnp.int32)
    starts = jnp.where((block_first == 1) & (block_rows > 0), blk, n_blocks)
    later = lax.cummin(jnp.concatenate([starts[1:], jnp.full((1,), n_blocks, jnp.int32)]),
                       axis=0, reverse=True)
    block_next = jnp.where(later < n_blocks, block_exp[jnp.minimum(later, n_blocks - 1)],
                           -1).astype(jnp.int32)
    return (dest.reshape(TOP_K, n_tok), slot_tok, block_exp, block_first, block_rows, block_next,
            n_used.reshape(1))


def _rot_cols(w):
    half = w.shape[-1] // 2
    return jnp.concatenate([-w[..., half:], w[..., :half]], axis=-1)


IN_ROW_BLOCK = 64


def _split_in_weights_kernel(wt_ref, a_ref, b_ref, c_ref, *, a_blocks, b_blocks, c_blocks):
    g = pl.program_id(1)
    w = wt_ref[0]

    @pl.when(g < a_blocks)
    def _():
        a_ref[0] = w.astype(a_ref.dtype)

    @pl.when((g >= a_blocks) & (g < a_blocks + b_blocks))
    def _():
        b_ref[0] = w.astype(b_ref.dtype)

    @pl.when((g >= a_blocks + b_blocks) & (g < a_blocks + b_blocks + c_blocks))
    def _():
        c_ref[0] = w.astype(c_ref.dtype)

    @pl.when(g == a_blocks + b_blocks + c_blocks)
    def _():
        half = w.shape[0] // 2
        a_ref[0] = jnp.concatenate([-w[half:], w[:half]], axis=0).astype(a_ref.dtype)


def split_in_weights(w_in):
    layers, d, n_in = w_in.shape
    wt = jnp.swapaxes(w_in, 1, 2)
    rb = IN_ROW_BLOCK
    assert MLA_ROPE == rb
    a_blocks = (Q_LORA + KV_LORA + MLA_ROPE) // rb
    b_blocks = 3 * CONV_CH // rb
    c_blocks = n_in // rb - a_blocks - b_blocks
    last = a_blocks + b_blocks + c_blocks

    def src(l, g):
        return (l, jnp.where(g == last, a_blocks - 1, g), 0)

    def dst_a(l, g):
        return (l, jnp.where(g == last, a_blocks, jnp.minimum(g, a_blocks - 1)), 0)

    def dst_b(l, g):
        return (l, jnp.clip(g - a_blocks, 0, b_blocks - 1), 0)

    def dst_c(l, g):
        return (l, jnp.clip(g - a_blocks - b_blocks, 0, c_blocks - 1), 0)

    blk = (1, rb, d)
    return pl.pallas_call(
        functools.partial(_split_in_weights_kernel, a_blocks=a_blocks, b_blocks=b_blocks,
                          c_blocks=c_blocks),
        grid=(layers, last + 1),
        in_specs=[pl.BlockSpec(blk, src)],
        out_specs=[pl.BlockSpec(blk, dst_a), pl.BlockSpec(blk, dst_b), pl.BlockSpec(blk, dst_c)],
        out_shape=[jax.ShapeDtypeStruct((layers, nb * rb, d), BF16)
                   for nb in (a_blocks + 1, b_blocks, c_blocks)],
        compiler_params=_params("arbitrary", "arbitrary"),
        name="split_in_weights",
    )(wt)


def _prep_weights(w_in, w_uq, w_ukv, w_out):
    layers = w_in.shape[0]
    w_a, w_b, w_c = split_in_weights(w_in)
    wq = w_uq.reshape(layers, Q_LORA, MLA_HEADS, MLA_NOPE + MLA_ROPE)
    wq_rope = wq[..., MLA_NOPE:]
    wq = jnp.concatenate([wq, _rot_cols(wq_rope)], axis=-1).transpose(0, 2, 3, 1).astype(BF16)
    return dict(w_a=w_a, w_b=w_b, w_c=w_c, w_q=wq, w_kv=w_ukv.astype(BF16),
                w_out=w_out.astype(BF16))


def _rope_tables(seq, dim):
    inv = 1.0 / (ROPE_THETA ** (jnp.arange(0, dim, 2, dtype=F32) / dim))
    ang = jnp.arange(seq, dtype=F32)[:, None] * inv[None, :]
    return jnp.cos(ang), jnp.sin(ang)


def kernel(x, c, w_mod, mod_table, mix_norm_g, w_in, q_norm_g, kv_norm_g, w_uq, w_ukv, conv_w,
           group_norm_g, w_out, ffn_norm_g, w_router, router_bias, w_gate, w_up, w_down,
           final_norm_g):
    batch, seq, d = x.shape
    depth = w_in.shape[0]
    t = batch * seq
    x2 = x.reshape(t, d)

    cos_a, sin_a = _rope_tables(seq, MLA_ROPE)
    zeros_a = jnp.zeros_like(cos_a)
    cos_p = jnp.concatenate([cos_a, cos_a, zeros_a, zeros_a], axis=1)
    cos_t = jnp.concatenate([cos_a, cos_a], axis=1).T
    sin_t = jnp.concatenate([sin_a, sin_a], axis=1).T
    sin_p = jnp.concatenate([sin_a, sin_a, zeros_a, zeros_a], axis=1)
    cos_b, sin_b = _rope_tables(seq, MOBA_HD)
    cos_f = jnp.concatenate([cos_b, cos_b], axis=1)
    sin_s = jnp.concatenate([-sin_b, sin_b], axis=1)

    c_pad = jnp.zeros((SUBLANE, d), F32).at[:batch].set(c)
    mod_shared = mod_matmul(c_pad, w_mod)[:batch].reshape(batch, N_MOD, d)

    wr_t = w_router.T
    wr_hi = wr_t.astype(BF16)
    wr_lo = (wr_t - wr_hi.astype(F32)).astype(BF16)
    router = (wr_hi, wr_lo, router_bias.astype(F32).reshape(N_EXPERTS, 1))
    mla_scale = (MLA_NOPE + MLA_ROPE) ** -0.5

    mods = []
    for l in range(depth):
        mod = mod_shared + mod_table[l][None]
        mods.append([mod[:, i][:, None, :] for i in range(N_MOD)])

    p = _prep_weights(w_in, w_uq, w_ukv, w_out)
    h = norm_mod(x2, mix_norm_g[0], 1.0 + mods[0][1], mods[0][0], seq)
    for l in range(depth):
        sh1, sc1, g1, sh2, sc2, g2 = mods[l]

        proj_a = matmul(h, p["w_a"], l, BF16, tm=512)
        bch = matmul(h, p["w_b"], l, BF16)
        qkv = matmul(h, p["w_c"], l, BF16)

        q_a = mla_q_up(proj_a, q_norm_g[l], p["w_q"], l, cos_t, sin_t, seq, mla_scale)
        k_a, v_a = mla_kv_up(proj_a, kv_norm_g[l], p["w_kv"], l, cos_p, sin_p, seq)
        y_a = mla_attention(q_a, k_a, v_a, batch, seq, BF16)

        y_b = conv_mixer(bch, conv_w[l], seq, F32)

        q_c, k_c, v_c, k_mean = moba_prep(qkv, cos_f, sin_s, seq)
        n_blk = seq // MOBA_BLOCK
        k_mean = k_mean.reshape(batch, n_blk, MOBA_HEADS, MOBA_HD).transpose(0, 2, 1, 3)
        y_c = moba_attention(q_c, k_c, v_c, k_mean, batch, seq, BF16)

        x2 = out_proj(y_a, y_b, y_c, group_norm_g[l], p["w_out"], l, x2, g1, seq)

        h2, e_idx, gates = norm_mod(x2, ffn_norm_g[l], 1.0 + sc2, sh2, seq, router=router)
        pos, slot_tok, block_exp, block_first, block_rows, block_next, n_used = _group_by_expert(e_idx)
        xs = gather_rows_cast(h2, slot_tok, n_used * (MOE_BLOCK // GATHER_ROWS), BF16)
        ys = expert_ffn(xs, block_exp, block_first, block_rows, block_next, w_gate, w_up, w_down,
                        l)
        if l + 1 < depth:
            nxt = mods[l + 1]
            x2, h = moe_combine(x2, g2, ys, pos.reshape(-1), gates.T, seq, mix_norm_g[l + 1],
                                1.0 + nxt[1], nxt[0])
        else:
            out = moe_combine(x2, g2, ys, pos.reshape(-1), gates.T, seq, final_norm_g)

    return out.reshape(batch, seq, d)
```

```python
import functools

import jax
import jax.numpy as jnp
from jax import lax
from jax.experimental import pallas as pl
from jax.experimental.pallas import tpu as pltpu

MLA_HEADS = 16
MLA_NOPE = 128
MLA_ROPE = 64
MLA_V = 128
Q_LORA = 1024
KV_LORA = 512
MLA_DK_PAD = 256
CONV_CH = 1024
CONV_W = 3
MOBA_HEADS = 8
MOBA_HD = 128
MOBA_BLOCK = 256
MOBA_TOPK = 3
ROPE_THETA = 10000.0
EPS = 1e-6
N_MOD = 6
N_EXPERTS = 16
N_GROUPS = 4
TOP_K = 2
MOE_BLOCK = 512
GATHER_ROWS = 256
FF_SPLIT = 2
UP_HEADS_PER_STEP = 8
ATTN_HEADS_PER_STEP = 4
ATTN_TILE = 512
ONES_ROWS = 16

V7X_VMEM_LIMIT_BYTES = 56 * 1024 * 1024
LANE = 128
SUBLANE = 8

F32 = jnp.float32
BF16 = jnp.bfloat16
PACKED = jnp.uint32
NEG_INF = float("-inf")


def _params(*sem):
    return pltpu.CompilerParams(dimension_semantics=sem, vmem_limit_bytes=V7X_VMEM_LIMIT_BYTES)


def _dot(a, b, precision=None):
    return jnp.dot(a, b, preferred_element_type=F32, precision=precision)


def _dot_nt(a, b):
    return lax.dot_general(a, b, (((1,), (1,)), ((), ())), preferred_element_type=F32)


def _tile(dim, want):
    return want if dim % want == 0 else dim


def _mod_kernel(c_ref, w0_ref, w1_ref, o_ref):
    @pl.when(pl.program_id(0) == 0)
    def _():
        o_ref[...] = jnp.zeros_like(o_ref)

    c = c_ref[...]
    a = (c * jax.nn.sigmoid(c)).astype(BF16)
    half = w0_ref.shape[1]
    o_ref[:, :half] += _dot(a, w0_ref[...].astype(BF16))
    o_ref[:, half:] += _dot(a, w1_ref[...].astype(BF16))


def mod_matmul(c_pad, w_mod):
    rows, d = c_pad.shape
    n = w_mod.shape[1]
    tk = _tile(d, LANE)
    return pl.pallas_call(
        _mod_kernel,
        grid=(d // tk,),
        in_specs=[pl.BlockSpec((rows, tk), lambda k: (0, k)),
                  pl.BlockSpec((tk, n // 2), lambda k: (k, 0)),
                  pl.BlockSpec((tk, n // 2), lambda k: (k, 1))],
        out_specs=pl.BlockSpec((rows, n), lambda k: (0, 0)),
        out_shape=jax.ShapeDtypeStruct((rows, n), F32),
        compiler_params=_params("arbitrary"),
        name="mod_matmul",
    )(c_pad, w_mod, w_mod)


def _rms(x, g):
    return x * lax.rsqrt(jnp.mean(x * x, axis=-1, keepdims=True) + EPS) * g


def _pack_bf16_pair(lo, hi):
    lo_bits = lax.bitcast_convert_type(lo.astype(BF16).astype(F32), PACKED) >> 16
    hi_bits = lax.bitcast_convert_type(hi.astype(BF16).astype(F32), PACKED) & jnp.uint32(0xFFFF0000)
    return hi_bits | lo_bits


def _unpack_bf16_pair(packed):
    lo = lax.bitcast_convert_type(packed << 16, F32)
    hi = lax.bitcast_convert_type(packed & jnp.uint32(0xFFFF0000), F32)
    return lo, hi


def _norm_mod_kernel(x_ref, g_ref, sc_ref, sh_ref, o_ref):
    y = _rms(x_ref[...], g_ref[...])
    o_ref[...] = (y * sc_ref[0] + sh_ref[0]).astype(o_ref.dtype)


def _top2_sum(a, b, c, d):
    hi1, lo1 = jnp.maximum(a, b), jnp.minimum(a, b)
    hi2, lo2 = jnp.maximum(c, d), jnp.minimum(c, d)
    return jnp.maximum(hi1, hi2) + jnp.maximum(jnp.minimum(hi1, hi2), jnp.maximum(lo1, lo2))


def _norm_router_kernel(x_ref, g_ref, sc_ref, sh_ref, whi_ref, wlo_ref, bias_ref,
                        o_ref, e_ref, gate_ref):
    h = _rms(x_ref[...], g_ref[...]) * sc_ref[0] + sh_ref[0]
    hi = h.astype(BF16)
    o_ref[...] = _pack_bf16_pair(h[:, :h.shape[1] // 2], h[:, h.shape[1] // 2:])
    lo = (h - hi.astype(F32)).astype(BF16)
    whl = jnp.concatenate([whi_ref[...], wlo_ref[...]], axis=0)
    both = _dot_nt(whl, hi)
    n_e = whi_ref.shape[0]
    logits = both[:n_e] + both[n_e:] + _dot_nt(whi_ref[...], lo)
    scores = jax.nn.sigmoid(logits)
    biased = scores + bias_ref[...]
    per_grp = N_EXPERTS // N_GROUPS
    b_rows = [biased[e:e + 1, :] for e in range(N_EXPERTS)]
    s_rows = [scores[e:e + 1, :] for e in range(N_EXPERTS)]
    grp_scores = [_top2_sum(*b_rows[g * per_grp:(g + 1) * per_grp]) for g in range(N_GROUPS)]
    best, grp = grp_scores[0], jnp.zeros(grp_scores[0].shape, jnp.int32)
    for g in range(1, N_GROUPS):
        better = grp_scores[g] > best
        grp = jnp.where(better, g, grp)
        best = jnp.where(better, grp_scores[g], best)
    b4, s4 = b_rows[:per_grp], s_rows[:per_grp]
    for g in range(1, N_GROUPS):
        in_g = grp == g
        b4 = [jnp.where(in_g, b_rows[g * per_grp + j], b4[j]) for j in range(per_grp)]
        s4 = [jnp.where(in_g, s_rows[g * per_grp + j], s4[j]) for j in range(per_grp)]
    v1, i1 = b4[0], jnp.zeros(grp.shape, jnp.int32)
    for j in range(1, per_grp):
        better = b4[j] > v1
        i1 = jnp.where(better, j, i1)
        v1 = jnp.where(better, b4[j], v1)
    v2, i2 = jnp.full(v1.shape, NEG_INF, F32), jnp.zeros(grp.shape, jnp.int32)
    for j in range(per_grp):
        better = (i1 != j) & (b4[j] > v2)
        i2 = jnp.where(better, j, i2)
        v2 = jnp.where(better, b4[j], v2)
    g1, g2 = s4[0], s4[0]
    for j in range(1, per_grp):
        g1 = jnp.where(i1 == j, s4[j], g1)
        g2 = jnp.where(i2 == j, s4[j], g2)
    total = g1 + g2
    e_ref[...] = jnp.concatenate([grp * per_grp + i1, grp * per_grp + i2], axis=0)
    gate_ref[...] = jnp.concatenate([g1 / total, g2 / total], axis=0)


def norm_mod(x2, g, sc1p, sh, seq, out_dtype=BF16, router=None):
    t, d = x2.shape
    tm = _tile(seq, 256)
    per_b = seq // tm
    row = pl.BlockSpec((tm, d), lambda i: (i, 0))
    vec = pl.BlockSpec((1, d), lambda i: (0, 0))
    mod = pl.BlockSpec((1, 1, d), lambda i: (i // per_b, 0, 0))
    g2 = g.reshape(1, d)
    if router is None:
        return pl.pallas_call(
            _norm_mod_kernel, grid=(t // tm,), in_specs=[row, vec, mod, mod], out_specs=row,
            out_shape=jax.ShapeDtypeStruct((t, d), out_dtype),
            compiler_params=_params("arbitrary"), name="norm_mod")(x2, g2, sc1p, sh)
    whi_t, wlo_t, bias = router
    ne = whi_t.shape[0]
    wspec = pl.BlockSpec((ne, d), lambda i: (0, 0))
    kspec = pl.BlockSpec((TOP_K, tm), lambda i: (0, i))
    return pl.pallas_call(
        _norm_router_kernel, grid=(t // tm,),
        in_specs=[row, vec, mod, mod, wspec, wspec, pl.BlockSpec((ne, 1), lambda i: (0, 0))],
        out_specs=[pl.BlockSpec((tm, d // 2), lambda i: (i, 0)), kspec, kspec],
        out_shape=[jax.ShapeDtypeStruct((t, d // 2), PACKED),
                   jax.ShapeDtypeStruct((TOP_K, t), jnp.int32),
                   jax.ShapeDtypeStruct((TOP_K, t), F32)],
        compiler_params=_params("arbitrary"), name="norm_router")(x2, g2, sc1p, sh, whi_t, wlo_t, bias)


def _mm_kernel(a_ref, wt_ref, o_ref):
    o_ref[...] = _dot_nt(a_ref[...], wt_ref[0]).astype(o_ref.dtype)


def matmul(a, wt_layers, layer, out_dtype, tm=1024, tn=1024):
    m, kd = a.shape
    n = wt_layers.shape[1]
    tm, tn = _tile(m, tm), _tile(n, tn)
    return pl.pallas_call(
        _mm_kernel,
        grid=(m // tm, n // tn),
        in_specs=[pl.BlockSpec((tm, kd), lambda i, j: (i, 0)),
                  pl.BlockSpec((1, tn, kd), lambda i, j: (layer, j, 0))],
        out_specs=pl.BlockSpec((tm, tn), lambda i, j: (i, j)),
        out_shape=jax.ShapeDtypeStruct((m, n), out_dtype),
        compiler_params=_params("parallel", "parallel"),
        name="matmul",
    )(a, wt_layers)


def _rope_pair(r2, cos_ref, sin_ref):
    return r2 * cos_ref[...] + pltpu.roll(r2, MLA_ROPE, 1) * sin_ref[...]


def _with_ones_rows(vt):
    return jnp.concatenate([vt, jnp.ones((ONES_ROWS, vt.shape[1]), vt.dtype)], axis=0)


def _qup_kernel(cq_ref, g_ref, wt_ref, cos_ref, sin_ref, o_ref, ant_ref, *, scale):
    @pl.when(pl.program_id(1) == 0)
    def _():
        ant_ref[...] = _rms(cq_ref[...].astype(F32), g_ref[...]).T.astype(BF16)

    ant = ant_ref[...]
    cos, sin = cos_ref[...], sin_ref[...]
    rope_end = MLA_NOPE + MLA_ROPE
    for hh in range(wt_ref.shape[1]):
        res = _dot(wt_ref[0, hh], ant)
        roped = res[MLA_NOPE:rope_end] * cos + res[rope_end:] * sin
        qt = jnp.concatenate([res[:MLA_NOPE], roped, jnp.zeros_like(roped)], axis=0) * scale
        o_ref[hh, 0] = qt.astype(o_ref.dtype)


def mla_q_up(proj_a, g, wt_q, layer, cos_t, sin_t, seq, scale):
    t = proj_a.shape[0]
    heads = wt_q.shape[1]
    hb = UP_HEADS_PER_STEP
    tm = _tile(seq, ATTN_TILE)
    per_b = seq // tm
    return pl.pallas_call(
        functools.partial(_qup_kernel, scale=scale),
        grid=(t // tm, heads // hb),
        in_specs=[pl.BlockSpec((tm, Q_LORA), lambda i, h: (i, 0)),
                  pl.BlockSpec((1, Q_LORA), lambda i, h: (0, 0)),
                  pl.BlockSpec((1, hb, MLA_DK_PAD, Q_LORA), lambda i, h: (layer, h, 0, 0)),
                  pl.BlockSpec((MLA_ROPE, tm), lambda i, h: (0, i % per_b)),
                  pl.BlockSpec((MLA_ROPE, tm), lambda i, h: (0, i % per_b))],
        out_specs=pl.BlockSpec((hb, 1, MLA_DK_PAD, tm), lambda i, h: (h, i, 0, 0)),
        out_shape=jax.ShapeDtypeStruct((heads, t // tm, MLA_DK_PAD, tm), BF16),
        scratch_shapes=[pltpu.VMEM((Q_LORA, tm), BF16)],
        compiler_params=_params("parallel", "arbitrary"),
        name="mla_q_up",
    )(proj_a, g.reshape(1, Q_LORA), wt_q, cos_t, sin_t)


def _kvup_kernel(ckv_ref, g_ref, w_ref, kpe_ref, cos_ref, sin_ref, k_ref, vt_ref, an_ref):
    @pl.when(pl.program_id(1) == 0)
    def _():
        an_ref[...] = _rms(ckv_ref[...].astype(F32), g_ref[...]).astype(BF16)

    an = an_ref[...]
    k_rot = _rope_pair(kpe_ref[...].astype(F32), cos_ref, sin_ref)
    hw = MLA_NOPE + MLA_V
    for hh in range(k_ref.shape[0]):
        res = _dot(an, w_ref[0, :, hh * hw:(hh + 1) * hw])
        k_ref[hh] = jnp.concatenate([res[:, :MLA_NOPE], k_rot], axis=1).astype(k_ref.dtype)
        vt_ref[hh, 0] = _with_ones_rows(res[:, MLA_NOPE:].T).astype(vt_ref.dtype)


def mla_kv_up(proj_a, g, w_kv, layer, cos_p, sin_p, seq):
    t = proj_a.shape[0]
    hw = MLA_NOPE + MLA_V
    heads = w_kv.shape[2] // hw
    hb = UP_HEADS_PER_STEP
    tm = _tile(seq, ATTN_TILE)
    per_b = seq // tm
    return pl.pallas_call(
        _kvup_kernel,
        grid=(t // tm, heads // hb),
        in_specs=[pl.BlockSpec((tm, KV_LORA), lambda i, h: (i, Q_LORA // KV_LORA)),
                  pl.BlockSpec((1, KV_LORA), lambda i, h: (0, 0)),
                  pl.BlockSpec((1, KV_LORA, hb * hw), lambda i, h: (layer, 0, h)),
                  pl.BlockSpec((tm, LANE), lambda i, h: (i, (Q_LORA + KV_LORA) // LANE)),
                  pl.BlockSpec((tm, LANE), lambda i, h: (i % per_b, 0)),
                  pl.BlockSpec((tm, LANE), lambda i, h: (i % per_b, 0))],
        out_specs=[pl.BlockSpec((hb, tm, MLA_DK_PAD), lambda i, h: (h, i, 0)),
                   pl.BlockSpec((hb, 1, MLA_V + ONES_ROWS, tm), lambda i, h: (h, i, 0, 0))],
        out_shape=[jax.ShapeDtypeStruct((heads, t, MLA_DK_PAD), BF16),
                   jax.ShapeDtypeStruct((heads, t // tm, MLA_V + ONES_ROWS, tm), BF16)],
        scratch_shapes=[pltpu.VMEM((tm, KV_LORA), BF16)],
        compiler_params=_params("parallel", "arbitrary"),
        name="mla_kv_up",
    )(proj_a, g.reshape(1, KV_LORA), w_kv, proj_a, cos_p, sin_p)


def _pv(vts, p):
    rows = p.shape[0] // len(vts)
    out = _dot(vts[0], p[:rows].astype(vts[0].dtype))
    for n in range(1, len(vts)):
        out = out + _dot(vts[n], p[n * rows:(n + 1) * rows].astype(vts[n].dtype))
    return out


def _softmax_first(s, vts, m_ref, acc_ref):
    m = jnp.max(s, axis=0, keepdims=True)
    m_ref[...] = m
    acc_ref[...] = _pv(vts, jnp.exp(s - m))


def _softmax_step(s, vts, m_ref, acc_ref):
    m_prev = m_ref[...]
    m_new = jnp.maximum(m_prev, jnp.max(s, axis=0, keepdims=True))
    alpha = jnp.exp(m_prev - m_new)
    acc_ref[...] = alpha * acc_ref[...] + _pv(vts, jnp.exp(s - m_new))
    m_ref[...] = m_new


def _softmax_finish(acc_ref, dv):
    acc = acc_ref[...]
    return (acc[:dv, :] / acc[dv:dv + 1, :]).T


def _causal_mask_t(s):
    key = lax.broadcasted_iota(jnp.int32, s.shape, 0)
    qry = lax.broadcasted_iota(jnp.int32, s.shape, 1)
    return jnp.where(key <= qry, s, NEG_INF)


def _mla_attn_kernel(qt_ref, k_ref, vt_ref, o_ref, m_ref, acc_ref, *, tq, dv):
    i = pl.program_id(2)
    n_h = qt_ref.shape[0]
    qts = [qt_ref[h, 0] for h in range(n_h)]

    def scores(h, j, n):
        off = pl.multiple_of(j * tq, tq)
        return _dot(k_ref[h, pl.ds(off, n * tq), :], qts[h])

    def first(h, n):
        j0 = i - (n - 1)
        s = scores(h, j0, n)
        diag = _causal_mask_t(s[(n - 1) * tq:])
        s = diag if n == 1 else jnp.concatenate([s[:tq], diag], axis=0)
        _softmax_first(s, [vt_ref[h, j0 + m] for m in range(n)], m_ref.at[h], acc_ref.at[h])

    @pl.when(i % 2 == 0)
    def _():
        for h in range(n_h):
            first(h, 1)

    @pl.when(i % 2 == 1)
    def _():
        for h in range(n_h):
            first(h, 2)

    def body(c, carry):
        for h in range(n_h):
            _softmax_step(scores(h, 2 * c, 2), [vt_ref[h, 2 * c], vt_ref[h, 2 * c + 1]],
                          m_ref.at[h], acc_ref.at[h])
        return carry

    lax.fori_loop(0, i // 2, body, 0)
    for h in range(n_h):
        o_ref[:, h * dv:(h + 1) * dv] = _softmax_finish(acc_ref.at[h], dv).astype(o_ref.dtype)


def mla_attention(qt, k, vt, batch, seq, out_dtype):
    heads, _, dk, tq = qt.shape
    t = k.shape[1]
    dvx = vt.shape[2]
    dv = dvx - ONES_ROWS
    nq = seq // tq
    hb = ATTN_HEADS_PER_STEP
    return pl.pallas_call(
        functools.partial(_mla_attn_kernel, tq=tq, dv=dv),
        grid=(batch, heads // hb, nq),
        in_specs=[pl.BlockSpec((hb, 1, dk, tq), lambda b, h, i: (h, b * nq + i, 0, 0)),
                  pl.BlockSpec((hb, seq, dk), lambda b, h, i: (h, b, 0)),
                  pl.BlockSpec((hb, nq, dvx, tq), lambda b, h, i: (h, b, 0, 0))],
        out_specs=pl.BlockSpec((tq, hb * dv), lambda b, h, i: (b * nq + i, h)),
        out_shape=jax.ShapeDtypeStruct((t, heads * dv), out_dtype),
        scratch_shapes=[pltpu.VMEM((hb, 1, tq), F32), pltpu.VMEM((hb, dvx, tq), F32)],
        compiler_params=_params("parallel", "parallel", "arbitrary"),
        name="mla_attention",
    )(qt, k, vt)


def _conv_kernel(b_ref, c_ref, h_ref, w_ref, o_ref, carry_ref, *, per_b):
    i = pl.program_id(0)

    @pl.when(i % per_b == 0)
    def _():
        carry_ref[...] = jnp.zeros_like(carry_ref)

    tm = o_ref.shape[0]
    w0, w1, w2 = w_ref[0:1, :], w_ref[1:2, :], w_ref[2:3, :]
    u = c_ref[...].astype(F32) * h_ref[...].astype(F32)
    y = w0 * pltpu.roll(u, 2, 0) + w1 * pltpu.roll(u, 1, 0) + w2 * u
    o_ref[...] = (b_ref[...].astype(F32) * y).astype(o_ref.dtype)
    u8 = u[0:SUBLANE, :]
    tail = carry_ref[...]
    r8 = lax.broadcasted_iota(jnp.int32, u8.shape, 0)
    p1 = jnp.where(r8 < 1, pltpu.roll(tail, 1, 0), pltpu.roll(u8, 1, 0))
    p2 = jnp.where(r8 < 2, pltpu.roll(tail, 2, 0), pltpu.roll(u8, 2, 0))
    y8 = w0 * p2 + w1 * p1 + w2 * u8
    o_ref[0:SUBLANE, :] = (b_ref[0:SUBLANE, :].astype(F32) * y8).astype(o_ref.dtype)
    carry_ref[...] = u[tm - SUBLANE:tm, :]


def conv_mixer(bch, conv_w, seq, out_dtype):
    t = bch.shape[0]
    ch = conv_w.shape[1]
    tm = _tile(seq, 256)
    per_b = seq // tm
    return pl.pallas_call(
        functools.partial(_conv_kernel, per_b=per_b),
        grid=(t // tm,),
        in_specs=[pl.BlockSpec((tm, ch), lambda i: (i, 0)),
                  pl.BlockSpec((tm, ch), lambda i: (i, 1)),
                  pl.BlockSpec((tm, ch), lambda i: (i, 2)),
                  pl.BlockSpec((CONV_W, ch), lambda i: (0, 0))],
        out_specs=pl.BlockSpec((tm, ch), lambda i: (i, 0)),
        out_shape=jax.ShapeDtypeStruct((t, ch), out_dtype),
        scratch_shapes=[pltpu.VMEM((SUBLANE, ch), F32)],
        compiler_params=_params("arbitrary"),
        name="conv_mixer",
    )(bch, bch, bch, conv_w)


def _moba_prep_kernel(q_ref, k_ref, v_ref, cos_ref, sin_ref, qo_ref, ko_ref, vo_ref, km_ref):
    heads = ko_ref.shape[0]
    cos, sin = cos_ref[...], sin_ref[...]
    n_sub = km_ref.shape[0]
    means = [[] for _ in range(n_sub)]
    for h in range(heads):
        sl = slice(h * MOBA_HD, (h + 1) * MOBA_HD)
        qh = q_ref[:, sl].astype(F32)
        kh = k_ref[:, sl].astype(F32)
        qr = qh * cos + pltpu.roll(qh, MOBA_HD // 2, 1) * sin
        qo_ref[h, 0] = qr.T.astype(qo_ref.dtype)
        kr = kh * cos + pltpu.roll(kh, MOBA_HD // 2, 1) * sin
        ko_ref[h] = kr.astype(ko_ref.dtype)
        vo_ref[h, 0] = _with_ones_rows(v_ref[:, sl].astype(F32).T).astype(vo_ref.dtype)
        for s in range(n_sub):
            means[s].append(jnp.mean(kr[s * MOBA_BLOCK:(s + 1) * MOBA_BLOCK], axis=0, keepdims=True))
    for s in range(n_sub):
        km_ref[s] = jnp.concatenate(means[s], axis=0)


def moba_prep(qkv, cos_f, sin_s, seq):
    t = qkv.shape[0]
    width = qkv.shape[1] // 3
    heads = width // MOBA_HD
    tm = 2 * MOBA_BLOCK
    assert seq % tm == 0
    per_b = seq // tm
    nt = t // tm
    return pl.pallas_call(
        _moba_prep_kernel,
        grid=(nt,),
        in_specs=[pl.BlockSpec((tm, width), lambda i: (i, 0)),
                  pl.BlockSpec((tm, width), lambda i: (i, 1)),
                  pl.BlockSpec((tm, width), lambda i: (i, 2)),
                  pl.BlockSpec((tm, MOBA_HD), lambda i: (i % per_b, 0)),
                  pl.BlockSpec((tm, MOBA_HD), lambda i: (i % per_b, 0))],
        out_specs=[pl.BlockSpec((heads, 1, MOBA_HD, tm), lambda i: (0, i, 0, 0)),
                   pl.BlockSpec((heads, tm, MOBA_HD), lambda i: (0, i, 0)),
                   pl.BlockSpec((heads, 1, MOBA_HD + ONES_ROWS, tm), lambda i: (0, i, 0, 0)),
                   pl.BlockSpec((2, heads, MOBA_HD), lambda i: (i, 0, 0))],
        out_shape=[jax.ShapeDtypeStruct((heads, nt, MOBA_HD, tm), BF16),
                   jax.ShapeDtypeStruct((heads, t, MOBA_HD), BF16),
                   jax.ShapeDtypeStruct((heads, nt, MOBA_HD + ONES_ROWS, tm), BF16),
                   jax.ShapeDtypeStruct((2 * nt, heads, MOBA_HD), F32)],
        compiler_params=_params("arbitrary"),
        name="moba_prep",
    )(qkv, qkv, qkv, cos_f, sin_s)


def _moba_attn_kernel(qt_ref, k_ref, vt_ref, km_ref, o_ref, m_ref, acc_ref, sel_ref, *, scale, dv):
    i = pl.program_id(2)
    n_h = qt_ref.shape[0]
    n_blk = km_ref.shape[2]
    tq = qt_ref.shape[3]
    blk_w = MOBA_BLOCK
    blk = lax.broadcasted_iota(jnp.int32, (n_blk, tq), 0)
    own = 2 * i + (lax.broadcasted_iota(jnp.int32, (n_blk, tq), 1) >= blk_w).astype(jnp.int32)
    past = blk < own
    qts = []
    for h in range(n_h):
        qtf = qt_ref[h, 0].astype(F32)
        g = _dot(km_ref[0, h], qtf, precision=lax.Precision.HIGHEST)
        g = jnp.where(past, g, NEG_INF)
        sel = jnp.zeros(g.shape, F32)
        for _ in range(min(MOBA_TOPK, n_blk)):
            mx = jnp.max(g, axis=0, keepdims=True)
            first = jnp.min(jnp.where(g == mx, blk, n_blk), axis=0, keepdims=True)
            hit = blk == first
            sel = jnp.where(hit, 1.0, sel)
            g = jnp.where(hit, NEG_INF, g)
        sel_ref[h] = jnp.where(past, sel, 0.0)
        qts.append((qtf * scale).astype(k_ref.dtype))

    def scores(h, p, n):
        off = pl.multiple_of(p * tq, tq)
        return _dot(k_ref[h, pl.ds(off, n * tq), :], qts[h])

    def chosen(h, n):
        return sel_ref[h, pl.ds(n, 1), :] > 0.5

    def mask_past(h, s, first_blk):
        parts = [jnp.where(chosen(h, first_blk + n), s[n * blk_w:(n + 1) * blk_w], NEG_INF)
                 for n in range(s.shape[0] // blk_w)]
        return jnp.concatenate(parts, axis=0)

    key = lax.broadcasted_iota(jnp.int32, (blk_w, tq), 0)
    qry = lax.broadcasted_iota(jnp.int32, (blk_w, tq), 1)
    bot_ok = key <= qry - blk_w
    def first(h, n):
        p0 = i - (n - 1)
        s = scores(h, p0, n)
        own = s[(n - 1) * tq:]
        chosen_lim = jnp.where(chosen(h, 2 * i), blk_w, -1)
        top_ok = key <= jnp.where(qry < blk_w, qry, chosen_lim)
        parts = [jnp.where(top_ok, own[:blk_w], NEG_INF), jnp.where(bot_ok, own[blk_w:], NEG_INF)]
        if n == 2:
            parts = [mask_past(h, s[:tq], 2 * p0)] + parts
        _softmax_first(jnp.concatenate(parts, axis=0), [vt_ref[h, p0 + m] for m in range(n)],
                       m_ref.at[h], acc_ref.at[h])

    @pl.when(i % 2 == 0)
    def _():
        for h in range(n_h):
            first(h, 1)

    @pl.when(i % 2 == 1)
    def _():
        for h in range(n_h):
            first(h, 2)

    def body(c, carry):
        for h in range(n_h):
            s = mask_past(h, scores(h, 2 * c, 2), 4 * c)
            _softmax_step(s, [vt_ref[h, 2 * c], vt_ref[h, 2 * c + 1]], m_ref.at[h], acc_ref.at[h])
        return carry

    lax.fori_loop(0, i // 2, body, 0)
    for h in range(n_h):
        o_ref[:, h * dv:(h + 1) * dv] = _softmax_finish(acc_ref.at[h], dv).astype(o_ref.dtype)


def moba_attention(qt, k, vt, k_mean, batch, seq, out_dtype):
    heads, _, hd, tq = qt.shape
    t = k.shape[1]
    dvx = vt.shape[2]
    dv = dvx - ONES_ROWS
    n_blk = seq // MOBA_BLOCK
    nq = seq // tq
    hb = ATTN_HEADS_PER_STEP
    return pl.pallas_call(
        functools.partial(_moba_attn_kernel, scale=hd ** -0.5, dv=dv),
        grid=(batch, heads // hb, nq),
        in_specs=[pl.BlockSpec((hb, 1, hd, tq), lambda b, h, i: (h, b * nq + i, 0, 0)),
                  pl.BlockSpec((hb, seq, hd), lambda b, h, i: (h, b, 0)),
                  pl.BlockSpec((hb, nq, dvx, tq), lambda b, h, i: (h, b, 0, 0)),
                  pl.BlockSpec((1, hb, n_blk, hd), lambda b, h, i: (b, h, 0, 0))],
        out_specs=pl.BlockSpec((tq, hb * dv), lambda b, h, i: (b * nq + i, h)),
        out_shape=jax.ShapeDtypeStruct((t, heads * dv), out_dtype),
        scratch_shapes=[pltpu.VMEM((hb, 1, tq), F32), pltpu.VMEM((hb, dvx, tq), F32),
                        pltpu.VMEM((hb, n_blk, tq), F32)],
        compiler_params=_params("parallel", "parallel", "arbitrary"),
        name="moba_attention",
    )(qt, k, vt, k_mean)


def _out_proj_kernel(a_ref, b_ref, c_ref, ga_ref, gb_ref, gc_ref, w_ref, x_ref, gate_ref, o_ref,
                     y_ref):
    @pl.when(pl.program_id(1) == 0)
    def _():
        wa, wb = a_ref.shape[1], b_ref.shape[1]
        y_ref[:, 0:wa] = _rms(a_ref[...].astype(F32), ga_ref[...]).astype(y_ref.dtype)
        y_ref[:, wa:wa + wb] = _rms(b_ref[...].astype(F32), gb_ref[...]).astype(y_ref.dtype)
        y_ref[:, wa + wb:] = _rms(c_ref[...].astype(F32), gc_ref[...]).astype(y_ref.dtype)

    o_ref[...] = x_ref[...] + gate_ref[0] * _dot(y_ref[...], w_ref[0])


def out_proj(y_a, y_b, y_c, g, w_layers, layer, x2, gate, seq, tm=1024, tn=512):
    t = y_a.shape[0]
    wa, wb, wc = y_a.shape[1], y_b.shape[1], y_c.shape[1]
    _, kd, n = w_layers.shape
    tm, tn = _tile(seq, tm), _tile(n, tn)
    per_b = seq // tm
    ga, gb, gc = g[:wa].reshape(1, wa), g[wa:wa + wb].reshape(1, wb), g[wa + wb:].reshape(1, wc)
    rows = lambda width: pl.BlockSpec((tm, width), lambda i, j: (i, 0))
    vec = lambda width: pl.BlockSpec((1, width), lambda i, j: (0, 0))
    return pl.pallas_call(
        _out_proj_kernel,
        grid=(t // tm, n // tn),
        in_specs=[rows(wa), rows(wb), rows(wc), vec(wa), vec(wb), vec(wc),
                  pl.BlockSpec((1, kd, tn), lambda i, j: (layer, 0, j)),
                  pl.BlockSpec((tm, tn), lambda i, j: (i, j)),
                  pl.BlockSpec((1, 1, tn), lambda i, j: (i // per_b, 0, j))],
        out_specs=pl.BlockSpec((tm, tn), lambda i, j: (i, j)),
        out_shape=jax.ShapeDtypeStruct((t, n), F32),
        scratch_shapes=[pltpu.VMEM((tm, kd), BF16)],
        compiler_params=_params("parallel", "arbitrary"),
        name="out_proj",
    )(y_a, y_b, y_c, ga, gb, gc, w_layers, x2, gate)


def _expert_weight_copies(w_hbm, layer, expert, col, stage, sem):
    width = stage.shape[1]
    return pltpu.make_async_copy(w_hbm.at[layer, expert, :, pl.ds(col, width)], stage, sem)


def _ffn_up_kernel(bexp_ref, first_ref, rows_ref, next_ref, x_ref, wg_hbm, wu_hbm, a_ref,
                   wg_st, wu_st, wg_bf, wu_bf, sem, *, layer):
    j = pl.program_id(0)
    b = pl.program_id(1)
    n_rows = rows_ref[b]
    half = x_ref.shape[0] // 2
    col = pl.multiple_of(j * wg_bf.shape[1], wg_bf.shape[1])

    def copies(expert):
        return (_expert_weight_copies(wg_hbm, layer, expert, col, wg_st, sem.at[0]),
                _expert_weight_copies(wu_hbm, layer, expert, col, wu_st, sem.at[1]))

    @pl.when(b == 0)
    def _():
        for c in copies(bexp_ref[0]):
            c.start()

    @pl.when((n_rows > 0) & (first_ref[b] == 1))
    def _():
        for c in copies(bexp_ref[b]):
            c.wait()
        wg_bf[...] = wg_st[...].astype(BF16)
        wu_bf[...] = wu_st[...].astype(BF16)

        @pl.when(next_ref[b] >= 0)
        def _():
            for c in copies(next_ref[b]):
                c.start()

    def act(x):
        gate = _dot(x, wg_bf[...])
        up = _dot(x, wu_bf[...])
        return (gate * jax.nn.sigmoid(gate) * up).astype(a_ref.dtype)

    @pl.when(n_rows > half)
    def _():
        a_ref[...] = act(x_ref[...])

    @pl.when((n_rows > 0) & (n_rows <= half))
    def _():
        a_ref[:half] = act(x_ref[:half])
        a_ref[half:] = jnp.zeros((half, a_ref.shape[1]), a_ref.dtype)

    @pl.when(n_rows == 0)
    def _():
        a_ref[...] = jnp.zeros_like(a_ref)


def _ffn_down_kernel(bexp_ref, first_ref, rows_ref, next_ref, a_ref, wd_hbm, o_ref,
                     wd_st, wd_bf, sem, *, layer):
    j = pl.program_id(0)
    b = pl.program_id(1)
    n_rows = rows_ref[b]
    half = a_ref.shape[0] // 2
    col = pl.multiple_of(j * wd_bf.shape[1], wd_bf.shape[1])

    def copy(expert):
        return _expert_weight_copies(wd_hbm, layer, expert, col, wd_st, sem.at[0])

    @pl.when(b == 0)
    def _():
        copy(bexp_ref[0]).start()

    @pl.when((n_rows > 0) & (first_ref[b] == 1))
    def _():
        copy(bexp_ref[b]).wait()
        wd_bf[...] = wd_st[...].astype(BF16)

        @pl.when(next_ref[b] >= 0)
        def _():
            copy(next_ref[b]).start()

    def packed(y):
        mid = y.shape[1] // 2
        return _pack_bf16_pair(y[:, :mid], y[:, mid:])

    @pl.when(n_rows > half)
    def _():
        o_ref[...] = packed(_dot(a_ref[...], wd_bf[...]))

    @pl.when((n_rows > 0) & (n_rows <= half))
    def _():
        o_ref[:half] = packed(_dot(a_ref[:half], wd_bf[...]))
        o_ref[half:] = jnp.zeros((half, o_ref.shape[1]), o_ref.dtype)

    @pl.when(n_rows == 0)
    def _():
        o_ref[...] = jnp.zeros_like(o_ref)


def expert_ffn(xs, block_exp, block_first, block_rows, block_next, w_gate, w_up, w_down, layer):
    n_slots, d = xs.shape
    ff = w_gate.shape[3]
    fh, dh = ff // FF_SPLIT, d // FF_SPLIT
    n_blocks = n_slots // MOE_BLOCK
    hbm = pl.BlockSpec(memory_space=pl.ANY)
    up_spec = pltpu.PrefetchScalarGridSpec(
        num_scalar_prefetch=4,
        grid=(FF_SPLIT, n_blocks),
        in_specs=[pl.BlockSpec((MOE_BLOCK, d), lambda j, b, *_: (b, 0)), hbm, hbm],
        out_specs=pl.BlockSpec((MOE_BLOCK, fh), lambda j, b, *_: (b, j)),
        scratch_shapes=[pltpu.VMEM((d, fh), F32), pltpu.VMEM((d, fh), F32),
                        pltpu.VMEM((d, fh), BF16), pltpu.VMEM((d, fh), BF16),
                        pltpu.SemaphoreType.DMA((2,))],
    )
    act = pl.pallas_call(
        functools.partial(_ffn_up_kernel, layer=layer),
        grid_spec=up_spec,
        out_shape=jax.ShapeDtypeStruct((n_slots, ff), BF16),
        compiler_params=_params("arbitrary", "arbitrary"),
        name="expert_ffn_up",
    )(block_exp, block_first, block_rows, block_next, xs, w_gate, w_up)
    down_spec = pltpu.PrefetchScalarGridSpec(
        num_scalar_prefetch=4,
        grid=(FF_SPLIT, n_blocks),
        in_specs=[pl.BlockSpec((MOE_BLOCK, ff), lambda j, b, *_: (b, 0)), hbm],
        out_specs=pl.BlockSpec((MOE_BLOCK, dh // 2), lambda j, b, *_: (b, j)),
        scratch_shapes=[pltpu.VMEM((ff, dh), F32), pltpu.VMEM((ff, dh), BF16),
                        pltpu.SemaphoreType.DMA((1,))],
    )
    return pl.pallas_call(
        functools.partial(_ffn_down_kernel, layer=layer),
        grid_spec=down_spec,
        out_shape=jax.ShapeDtypeStruct((n_slots, d // 2), PACKED),
        compiler_params=_params("arbitrary", "arbitrary"),
        name="expert_ffn_down",
    )(block_exp, block_first, block_rows, block_next, act, w_down)


def _start_row_copies(idx_ref, base, n_rows, src_ref, dst_ref, sem):
    def issue(r, carry):
        row = idx_ref[base + r]
        pltpu.make_async_copy(src_ref.at[pl.ds(row, 1)], dst_ref.at[pl.ds(r, 1)], sem).start()
        return carry

    lax.fori_loop(0, n_rows, issue, 0, unroll=8)


def _wait_row_copies(n_rows, src_ref, dst_ref, sem):
    pltpu.make_async_copy(src_ref.at[pl.ds(0, n_rows)], dst_ref, sem).wait()


def _gather_cast_kernel(idx_ref, nused_ref, src_ref, o_ref, buf, sem):
    b = pl.program_id(0)
    rows = o_ref.shape[0]
    n_used = nused_ref[0]

    @pl.when((b == 0) & (n_used > 0))
    def _():
        _start_row_copies(idx_ref, 0, rows, src_ref, buf.at[0], sem.at[0])

    @pl.when(b + 1 < n_used)
    def _():
        nxt = (b + 1) % 2
        _start_row_copies(idx_ref, (b + 1) * rows, rows, src_ref, buf.at[nxt], sem.at[nxt])

    @pl.when(b < n_used)
    def _():
        cur = b % 2
        _wait_row_copies(rows, src_ref, buf.at[cur], sem.at[cur])
        lo, hi = _unpack_bf16_pair(buf[cur])
        half = lo.shape[1]
        o_ref[:, :half] = lo.astype(o_ref.dtype)
        o_ref[:, half:] = hi.astype(o_ref.dtype)

    @pl.when(b >= n_used)
    def _():
        o_ref[...] = jnp.zeros_like(o_ref)


def gather_rows_cast(src, idx, n_used_tiles, out_dtype):
    dp = src.shape[1]
    d = 2 * dp
    m = idx.shape[0]
    rows = GATHER_ROWS
    grid_spec = pltpu.PrefetchScalarGridSpec(
        num_scalar_prefetch=2,
        grid=(m // rows,),
        in_specs=[pl.BlockSpec(memory_space=pl.ANY)],
        out_specs=pl.BlockSpec((rows, d), lambda b, ix, nu: (b, 0)),
        scratch_shapes=[pltpu.VMEM((2, rows, dp), src.dtype), pltpu.SemaphoreType.DMA((2,))],
    )
    return pl.pallas_call(
        _gather_cast_kernel,
        grid_spec=grid_spec,
        out_shape=jax.ShapeDtypeStruct((m, d), out_dtype),
        compiler_params=_params("arbitrary"),
        name="gather_rows_cast",
    )(idx, n_used_tiles, src)


def _combine_kernel(pos_ref, x_ref, gate_ref, w_ref, g_ref, sc_ref, sh_ref, ys_ref,
                    *rest, n_tok, n_windows, final):
    if final:
        o_ref, ybuf, sem = rest
    else:
        o_ref, h_ref, ybuf, sem = rest
    i = pl.program_id(0)
    n_tiles = pl.num_programs(0)
    tm = x_ref.shape[0]

    def start_tile(tile, slot):
        for kk in range(TOP_K):
            _start_row_copies(pos_ref, kk * n_tok + tile * tm, tm, ys_ref, ybuf.at[slot, kk],
                              sem.at[slot])

    @pl.when(i == 0)
    def _():
        start_tile(0, 0)

    @pl.when(i + 1 < n_tiles)
    def _():
        start_tile(i + 1, (i + 1) % 2)

    cur = i % 2
    for kk in range(TOP_K):
        _wait_row_copies(tm, ys_ref, ybuf.at[cur, kk], sem.at[cur])
    w = w_ref[...]
    acc = None
    for kk in range(TOP_K):
        lo, hi = _unpack_bf16_pair(ybuf[cur, kk])
        ww = lo.shape[1] // n_windows
        parts = []
        for n in range(n_windows):
            parts += [lo[:, n * ww:(n + 1) * ww], hi[:, n * ww:(n + 1) * ww]]
        term = w[:, kk:kk + 1] * jnp.concatenate(parts, axis=1)
        acc = term if acc is None else acc + term
    x_new = x_ref[...] + gate_ref[0] * acc
    if final:
        o_ref[...] = _rms(x_new, g_ref[...])
    else:
        o_ref[...] = x_new
        h_ref[...] = (_rms(x_new, g_ref[...]) * sc_ref[0] + sh_ref[0]).astype(h_ref.dtype)


def moe_combine(x2, gate, ys, pos, w_tok, seq, norm_g, sc1p=None, sh=None):
    t, d = x2.shape
    tm = _tile(seq, GATHER_ROWS)
    per_b = seq // tm
    final = sc1p is None
    if final:
        sc1p = sh = jnp.zeros((x2.shape[0] // seq, 1, d), F32)
    row = pl.BlockSpec((tm, d), lambda i, ps: (i, 0))
    mod = pl.BlockSpec((1, 1, d), lambda i, ps: (i // per_b, 0, 0))
    tok = pl.BlockSpec((tm, TOP_K), lambda i, ps: (i, 0))
    grid_spec = pltpu.PrefetchScalarGridSpec(
        num_scalar_prefetch=1,
        grid=(t // tm,),
        in_specs=[row, mod, tok, pl.BlockSpec((1, d), lambda i, ps: (0, 0)), mod, mod,
                  pl.BlockSpec(memory_space=pl.ANY)],
        out_specs=row if final else [row, row],
        scratch_shapes=[pltpu.VMEM((2, TOP_K, tm, d // 2), ys.dtype),
                        pltpu.SemaphoreType.DMA((2,))],
    )
    out_shape = jax.ShapeDtypeStruct((t, d), F32)
    return pl.pallas_call(
        functools.partial(_combine_kernel, n_tok=t, n_windows=FF_SPLIT, final=final),
        grid_spec=grid_spec,
        out_shape=out_shape if final else [out_shape, jax.ShapeDtypeStruct((t, d), BF16)],
        compiler_params=_params("arbitrary"),
        name="moe_combine",
    )(pos, x2, gate, w_tok, norm_g.reshape(1, d), sc1p, sh, ys)


def _group_by_expert(e_idx):
    n_tok = e_idx.shape[1]
    n_asg = n_tok * TOP_K
    flat_e = e_idx.reshape(n_asg)
    onehot = (flat_e[:, None] == jnp.arange(N_EXPERTS)[None, :]).astype(jnp.int32)
    running = jnp.cumsum(onehot, axis=0)
    rank = jnp.sum(running * onehot, axis=1) - 1
    sizes = running[-1]
    padded = (sizes + MOE_BLOCK - 1) // MOE_BLOCK * MOE_BLOCK
    pad_end = jnp.cumsum(padded)
    pad_start = pad_end - padded
    dest = (jnp.sum(pad_start[None, :] * onehot, axis=1) + rank).astype(jnp.int32)
    n_blocks = -(-n_asg // MOE_BLOCK) + N_EXPERTS
    n_slots = n_blocks * MOE_BLOCK
    slot_tok = (jnp.arange(n_slots, dtype=jnp.int32) % n_tok).at[dest].set(
        jnp.arange(n_asg, dtype=jnp.int32) % n_tok, mode="promise_in_bounds")
    block_start = jnp.arange(n_blocks, dtype=jnp.int32) * MOE_BLOCK
    block_exp = jnp.minimum(jnp.sum(block_start[:, None] >= pad_end[None, :], axis=1),
                            N_EXPERTS - 1).astype(jnp.int32)
    n_used = (pad_end[-1] // MOE_BLOCK).astype(jnp.int32)
    block_exp = jnp.where(jnp.arange(n_blocks) < n_used, block_exp, block_exp[n_used - 1])
    prev_exp = jnp.concatenate([jnp.full((1,), -1, jnp.int32), block_exp[:-1]])
    block_first = (block_exp != prev_exp).astype(jnp.int32)
    block_rows = jnp.clip(pad_start[block_exp] + sizes[block_exp] - block_start, 0, MOE_BLOCK)
    block_rows = jnp.where(jnp.arange(n_blocks) < n_used, block_rows, 0).astype(jnp.int32)
    blk = jnp.arange(n_blocks, dtype=jnp.int32)
    starts = jnp.where((block_first == 1) & (block_rows > 0), blk, n_blocks)
    later = lax.cummin(jnp.concatenate([starts[1:], jnp.full((1,), n_blocks, jnp.int32)]),
                       axis=0, reverse=True)
    block_next = jnp.where(later < n_blocks, block_exp[jnp.minimum(later, n_blocks - 1)],
                           -1).astype(jnp.int32)
    return (dest.reshape(TOP_K, n_tok), slot_tok, block_exp, block_first, block_rows, block_next,
            n_used.reshape(1))


def _rot_cols(w):
    half = w.shape[-1] // 2
    return jnp.concatenate([-w[..., half:], w[..., :half]], axis=-1)


IN_ROW_BLOCK = 64


def _split_in_weights_kernel(wt_ref, a_ref, b_ref, c_ref, *, a_blocks, b_blocks, c_blocks):
    g = pl.program_id(1)
    w = wt_ref[0]

    @pl.when(g < a_blocks)
    def _():
        a_ref[0] = w.astype(a_ref.dtype)

    @pl.when((g >= a_blocks) & (g < a_blocks + b_blocks))
    def _():
        b_ref[0] = w.astype(b_ref.dtype)

    @pl.when((g >= a_blocks + b_blocks) & (g < a_blocks + b_blocks + c_blocks))
    def _():
        c_ref[0] = w.astype(c_ref.dtype)

    @pl.when(g == a_blocks + b_blocks + c_blocks)
    def _():
        half = w.shape[0] // 2
        a_ref[0] = jnp.concatenate([-w[half:], w[:half]], axis=0).astype(a_ref.dtype)


def split_in_weights(w_in):
    layers, d, n_in = w_in.shape
    wt = jnp.swapaxes(w_in, 1, 2)
    rb = IN_ROW_BLOCK
    assert MLA_ROPE == rb
    a_blocks = (Q_LORA + KV_LORA + MLA_ROPE) // rb
    b_blocks = 3 * CONV_CH // rb
    c_blocks = n_in // rb - a_blocks - b_blocks
    last = a_blocks + b_blocks + c_blocks

    def src(l, g):
        return (l, jnp.where(g == last, a_blocks - 1, g), 0)

    def dst_a(l, g):
        return (l, jnp.where(g == last, a_blocks, jnp.minimum(g, a_blocks - 1)), 0)

    def dst_b(l, g):
        return (l, jnp.clip(g - a_blocks, 0, b_blocks - 1), 0)

    def dst_c(l, g):
        return (l, jnp.clip(g - a_blocks - b_blocks, 0, c_blocks - 1), 0)

    blk = (1, rb, d)
    return pl.pallas_call(
        functools.partial(_split_in_weights_kernel, a_blocks=a_blocks, b_blocks=b_blocks,
                          c_blocks=c_blocks),
        grid=(layers, last + 1),
        in_specs=[pl.BlockSpec(blk, src)],
        out_specs=[pl.BlockSpec(blk, dst_a), pl.BlockSpec(blk, dst_b), pl.BlockSpec(blk, dst_c)],
        out_shape=[jax.ShapeDtypeStruct((layers, nb * rb, d), BF16)
                   for nb in (a_blocks + 1, b_blocks, c_blocks)],
        compiler_params=_params("arbitrary", "arbitrary"),
        name="split_in_weights",
    )(wt)


def _prep_weights(w_in, w_uq, w_ukv, w_out):
    layers = w_in.shape[0]
    w_a, w_b, w_c = split_in_weights(w_in)
    wq = w_uq.reshape(layers, Q_LORA, MLA_HEADS, MLA_NOPE + MLA_ROPE)
    wq_rope = wq[..., MLA_NOPE:]
    wq = jnp.concatenate([wq, _rot_cols(wq_rope)], axis=-1).transpose(0, 2, 3, 1).astype(BF16)
    return dict(w_a=w_a, w_b=w_b, w_c=w_c, w_q=wq, w_kv=w_ukv.astype(BF16),
                w_out=w_out.astype(BF16))


def _rope_tables(seq, dim):
    inv = 1.0 / (ROPE_THETA ** (jnp.arange(0, dim, 2, dtype=F32) / dim))
    ang = jnp.arange(seq, dtype=F32)[:, None] * inv[None, :]
    return jnp.cos(ang), jnp.sin(ang)


def kernel(x, c, w_mod, mod_table, mix_norm_g, w_in, q_norm_g, kv_norm_g, w_uq, w_ukv, conv_w,
           group_norm_g, w_out, ffn_norm_g, w_router, router_bias, w_gate, w_up, w_down,
           final_norm_g):
    batch, seq, d = x.shape
    depth = w_in.shape[0]
    t = batch * seq
    x2 = x.reshape(t, d)

    cos_a, sin_a = _rope_tables(seq, MLA_ROPE)
    zeros_a = jnp.zeros_like(cos_a)
    cos_p = jnp.concatenate([cos_a, cos_a, zeros_a, zeros_a], axis=1)
    cos_t = jnp.concatenate([cos_a, cos_a], axis=1).T
    sin_t = jnp.concatenate([sin_a, sin_a], axis=1).T
    sin_p = jnp.concatenate([sin_a, sin_a, zeros_a, zeros_a], axis=1)
    cos_b, sin_b = _rope_tables(seq, MOBA_HD)
    cos_f = jnp.concatenate([cos_b, cos_b], axis=1)
    sin_s = jnp.concatenate([-sin_b, sin_b], axis=1)

    c_pad = jnp.zeros((SUBLANE, d), F32).at[:batch].set(c)
    mod_shared = mod_matmul(c_pad, w_mod)[:batch].reshape(batch, N_MOD, d)

    wr_t = w_router.T
    wr_hi = wr_t.astype(BF16)
    wr_lo = (wr_t - wr_hi.astype(F32)).astype(BF16)
    router = (wr_hi, wr_lo, router_bias.astype(F32).reshape(N_EXPERTS, 1))
    mla_scale = (MLA_NOPE + MLA_ROPE) ** -0.5

    mods = []
    for l in range(depth):
        mod = mod_shared + mod_table[l][None]
        mods.append([mod[:, i][:, None, :] for i in range(N_MOD)])

    p = _prep_weights(w_in, w_uq, w_ukv, w_out)
    h = norm_mod(x2, mix_norm_g[0], 1.0 + mods[0][1], mods[0][0], seq)
    for l in range(depth):
        sh1, sc1, g1, sh2, sc2, g2 = mods[l]

        proj_a = matmul(h, p["w_a"], l, BF16, tm=512)
        bch = matmul(h, p["w_b"], l, BF16)
        qkv = matmul(h, p["w_c"], l, BF16)

        q_a = mla_q_up(proj_a, q_norm_g[l], p["w_q"], l, cos_t, sin_t, seq, mla_scale)
        k_a, v_a = mla_kv_up(proj_a, kv_norm_g[l], p["w_kv"], l, cos_p, sin_p, seq)
        y_a = mla_attention(q_a, k_a, v_a, batch, seq, BF16)

        y_b = conv_mixer(bch, conv_w[l], seq, F32)

        q_c, k_c, v_c, k_mean = moba_prep(qkv, cos_f, sin_s, seq)
        n_blk = seq // MOBA_BLOCK
        k_mean = k_mean.reshape(batch, n_blk, MOBA_HEADS, MOBA_HD).transpose(0, 2, 1, 3)
        y_c = moba_attention(q_c, k_c, v_c, k_mean, batch, seq, BF16)

        x2 = out_proj(y_a, y_b, y_c, group_norm_g[l], p["w_out"], l, x2, g1, seq)

        h2, e_idx, gates = norm_mod(x2, ffn_norm_g[l], 1.0 + sc2, sh2, seq, router=router)
        pos, slot_tok, block_exp, block_first, block_rows, block_next, n_used = _group_by_expert(e_idx)
        xs = gather_rows_cast(h2, slot_tok, n_used * (MOE_BLOCK // GATHER_ROWS), BF16)
        ys = expert_ffn(xs, block_exp, block_first, block_rows, block_next, w_gate, w_up, w_down,
                        l)
        if l + 1 < depth:
            nxt = mods[l + 1]
            x2, h = moe_combine(x2, g2, ys, pos.reshape(-1), gates.T, seq, mix_norm_g[l + 1],
                                1.0 + nxt[1], nxt[0])
        else:
            out = moe_combine(x2, g2, ys, pos.reshape(-1), gates.T, seq, final_norm_g)

    return out.reshape(batch, seq, d)
```

```python
import functools

import jax
import jax.numpy as jnp
from jax import lax
from jax.experimental import pallas as pl
from jax.experimental.pallas import tpu as pltpu

MLA_HEADS = 16
MLA_NOPE = 128
MLA_ROPE = 64
MLA_V = 128
Q_LORA = 1024
KV_LORA = 512
MLA_DK_PAD = 256
CONV_CH = 1024
CONV_W = 3
MOBA_HEADS = 8
MOBA_HD = 128
MOBA_BLOCK = 256
MOBA_TOPK = 3
ROPE_THETA = 10000.0
EPS = 1e-6
N_MOD = 6
N_EXPERTS = 16
N_GROUPS = 4
TOP_K = 2
MOE_BLOCK = 512
GATHER_ROWS = 256
FF_SPLIT = 2
UP_HEADS_PER_STEP = 8
ATTN_HEADS_PER_STEP = 4
ATTN_TILE = 512
ONES_ROWS = 16

V7X_VMEM_LIMIT_BYTES = 56 * 1024 * 1024
LANE = 128
SUBLANE = 8

F32 = jnp.float32
BF16 = jnp.bfloat16
PACKED = jnp.uint32
NEG_INF = float("-inf")


def _params(*sem):
    return pltpu.CompilerParams(dimension_semantics=sem, vmem_limit_bytes=V7X_VMEM_LIMIT_BYTES)


def _dot(a, b, precision=None):
    return jnp.dot(a, b, preferred_element_type=F32, precision=precision)


def _dot_nt(a, b):
    return lax.dot_general(a, b, (((1,), (1,)), ((), ())), preferred_element_type=F32)


def _tile(dim, want):
    return want if dim % want == 0 else dim


def _mod_kernel(c_ref, w0_ref, w1_ref, o_ref):
    @pl.when(pl.program_id(0) == 0)
    def _():
        o_ref[...] = jnp.zeros_like(o_ref)

    c = c_ref[...]
    a = (c * jax.nn.sigmoid(c)).astype(BF16)
    half = w0_ref.shape[1]
    o_ref[:, :half] += _dot(a, w0_ref[...].astype(BF16))
    o_ref[:, half:] += _dot(a, w1_ref[...].astype(BF16))


def mod_matmul(c_pad, w_mod):
    rows, d = c_pad.shape
    n = w_mod.shape[1]
    tk = _tile(d, LANE)
    return pl.pallas_call(
        _mod_kernel,
        grid=(d // tk,),
        in_specs=[pl.BlockSpec((rows, tk), lambda k: (0, k)),
                  pl.BlockSpec((tk, n // 2), lambda k: (k, 0)),
                  pl.BlockSpec((tk, n // 2), lambda k: (k, 1))],
        out_specs=pl.BlockSpec((rows, n), lambda k: (0, 0)),
        out_shape=jax.ShapeDtypeStruct((rows, n), F32),
        compiler_params=_params("arbitrary"),
        name="mod_matmul",
    )(c_pad, w_mod, w_mod)


def _rms(x, g):
    return x * lax.rsqrt(jnp.mean(x * x, axis=-1, keepdims=True) + EPS) * g


def _pack_bf16_pair(lo, hi):
    lo_bits = lax.bitcast_convert_type(lo.astype(BF16).astype(F32), PACKED) >> 16
    hi_bits = lax.bitcast_convert_type(hi.astype(BF16).astype(F32), PACKED) & jnp.uint32(0xFFFF0000)
    return hi_bits | lo_bits


def _unpack_bf16_pair(packed):
    lo = lax.bitcast_convert_type(packed << 16, F32)
    hi = lax.bitcast_convert_type(packed & jnp.uint32(0xFFFF0000), F32)
    return lo, hi


def _norm_mod_kernel(x_ref, g_ref, sc_ref, sh_ref, o_ref):
    y = _rms(x_ref[...], g_ref[...])
    o_ref[...] = (y * sc_ref[0] + sh_ref[0]).astype(o_ref.dtype)


def _top2_sum(a, b, c, d):
    hi1, lo1 = jnp.maximum(a, b), jnp.minimum(a, b)
    hi2, lo2 = jnp.maximum(c, d), jnp.minimum(c, d)
    return jnp.maximum(hi1, hi2) + jnp.maximum(jnp.minimum(hi1, hi2), jnp.maximum(lo1, lo2))


def _norm_router_kernel(x_ref, g_ref, sc_ref, sh_ref, whi_ref, wlo_ref, bias_ref,
                        o_ref, e_ref, gate_ref):
    h = _rms(x_ref[...], g_ref[...]) * sc_ref[0] + sh_ref[0]
    hi = h.astype(BF16)
    o_ref[...] = _pack_bf16_pair(h[:, :h.shape[1] // 2], h[:, h.shape[1] // 2:])
    lo = (h - hi.astype(F32)).astype(BF16)
    whl = jnp.concatenate([whi_ref[...], wlo_ref[...]], axis=0)
    both = _dot_nt(whl, hi)
    n_e = whi_ref.shape[0]
    logits = both[:n_e] + both[n_e:] + _dot_nt(whi_ref[...], lo)
    scores = jax.nn.sigmoid(logits)
    biased = scores + bias_ref[...]
    per_grp = N_EXPERTS // N_GROUPS
    b_rows = [biased[e:e + 1, :] for e in range(N_EXPERTS)]
    s_rows = [scores[e:e + 1, :] for e in range(N_EXPERTS)]
    grp_scores = [_top2_sum(*b_rows[g * per_grp:(g + 1) * per_grp]) for g in range(N_GROUPS)]
    best, grp = grp_scores[0], jnp.zeros(grp_scores[0].shape, jnp.int32)
    for g in range(1, N_GROUPS):
        better = grp_scores[g] > best
        grp = jnp.where(better, g, grp)
        best = jnp.where(better, grp_scores[g], best)
    b4, s4 = b_rows[:per_grp], s_rows[:per_grp]
    for g in range(1, N_GROUPS):
        in_g = grp == g
        b4 = [jnp.where(in_g, b_rows[g * per_grp + j], b4[j]) for j in range(per_grp)]
        s4 = [jnp.where(in_g, s_rows[g * per_grp + j], s4[j]) for j in range(per_grp)]
    v1, i1 = b4[0], jnp.zeros(grp.shape, jnp.int32)
    for j in range(1, per_grp):
        better = b4[j] > v1
        i1 = jnp.where(better, j, i1)
        v1 = jnp.where(better, b4[j], v1)
    v2, i2 = jnp.full(v1.shape, NEG_INF, F32), jnp.zeros(grp.shape, jnp.int32)
    for j in range(per_grp):
        better = (i1 != j) & (b4[j] > v2)
        i2 = jnp.where(better, j, i2)
        v2 = jnp.where(better, b4[j], v2)
    g1, g2 = s4[0], s4[0]
    for j in range(1, per_grp):
        g1 = jnp.where(i1 == j, s4[j], g1)
        g2 = jnp.where(i2 == j, s4[j], g2)
    total = g1 + g2
    e_ref[...] = jnp.concatenate([grp * per_grp + i1, grp * per_grp + i2], axis=0)
    gate_ref[...] = jnp.concatenate([g1 / total, g2 / total], axis=0)


def norm_mod(x2, g, sc1p, sh, seq, out_dtype=BF16, router=None):
    t, d = x2.shape
    tm = _tile(seq, 256)
    per_b = seq // tm
    row = pl.BlockSpec((tm, d), lambda i: (i, 0))
    vec = pl.BlockSpec((1, d), lambda i: (0, 0))
    mod = pl.BlockSpec((1, 1, d), lambda i: (i // per_b, 0, 0))
    g2 = g.reshape(1, d)
    if router is None:
        return pl.pallas_call(
            _norm_mod_kernel, grid=(t // tm,), in_specs=[row, vec, mod, mod], out_specs=row,
            out_shape=jax.ShapeDtypeStruct((t, d), out_dtype),
            compiler_params=_params("arbitrary"), name="norm_mod")(x2, g2, sc1p, sh)
    whi_t, wlo_t, bias = router
    ne = whi_t.shape[0]
    wspec = pl.BlockSpec((ne, d), lambda i: (0, 0))
    kspec = pl.BlockSpec((TOP_K, tm), lambda i: (0, i))
    return pl.pallas_call(
        _norm_router_kernel, grid=(t // tm,),
        in_specs=[row, vec, mod, mod, wspec, wspec, pl.BlockSpec((ne, 1), lambda i: (0, 0))],
        out_specs=[pl.BlockSpec((tm, d // 2), lambda i: (i, 0)), kspec, kspec],
        out_shape=[jax.ShapeDtypeStruct((t, d // 2), PACKED),
                   jax.ShapeDtypeStruct((TOP_K, t), jnp.int32),
                   jax.ShapeDtypeStruct((TOP_K, t), F32)],
        compiler_params=_params("arbitrary"), name="norm_router")(x2, g2, sc1p, sh, whi_t, wlo_t, bias)


def _mm_kernel(a_ref, wt_ref, o_ref):
    o_ref[...] = _dot_nt(a_ref[...], wt_ref[0]).astype(o_ref.dtype)


def matmul(a, wt_layers, layer, out_dtype, tm=1024, tn=1024):
    m, kd = a.shape
    n = wt_layers.shape[1]
    tm, tn = _tile(m, tm), _tile(n, tn)
    return pl.pallas_call(
        _mm_kernel,
        grid=(m // tm, n // tn),
        in_specs=[pl.BlockSpec((tm, kd), lambda i, j: (i, 0)),
                  pl.BlockSpec((1, tn, kd), lambda i, j: (layer, j, 0))],
        out_specs=pl.BlockSpec((tm, tn), lambda i, j: (i, j)),
        out_shape=jax.ShapeDtypeStruct((m, n), out_dtype),
        compiler_params=_params("parallel", "parallel"),
        name="matmul",
    )(a, wt_layers)


def _rope_pair(r2, cos_ref, sin_ref):
    return r2 * cos_ref[...] + pltpu.roll(r2, MLA_ROPE, 1) * sin_ref[...]


def _with_ones_rows(vt):
    return jnp.concatenate([vt, jnp.ones((ONES_ROWS, vt.shape[1]), vt.dtype)], axis=0)


def _qup_kernel(cq_ref, g_ref, wt_ref, cos_ref, sin_ref, o_ref, ant_ref, *, scale):
    @pl.when(pl.program_id(1) == 0)
    def _():
        ant_ref[...] = _rms(cq_ref[...].astype(F32), g_ref[...]).T.astype(BF16)

    ant = ant_ref[...]
    cos, sin = cos_ref[...], sin_ref[...]
    rope_end = MLA_NOPE + MLA_ROPE
    for hh in range(wt_ref.shape[1]):
        res = _dot(wt_ref[0, hh], ant)
        roped = res[MLA_NOPE:rope_end] * cos + res[rope_end:] * sin
        qt = jnp.concatenate([res[:MLA_NOPE], roped, jnp.zeros_like(roped)], axis=0) * scale
        o_ref[hh, 0] = qt.astype(o_ref.dtype)


def mla_q_up(proj_a, g, wt_q, layer, cos_t, sin_t, seq, scale):
    t = proj_a.shape[0]
    heads = wt_q.shape[1]
    hb = UP_HEADS_PER_STEP
    tm = _tile(seq, ATTN_TILE)
    per_b = seq // tm
    return pl.pallas_call(
        functools.partial(_qup_kernel, scale=scale),
        grid=(t // tm, heads // hb),
        in_specs=[pl.BlockSpec((tm, Q_LORA), lambda i, h: (i, 0)),
                  pl.BlockSpec((1, Q_LORA), lambda i, h: (0, 0)),
                  pl.BlockSpec((1, hb, MLA_DK_PAD, Q_LORA), lambda i, h: (layer, h, 0, 0)),
                  pl.BlockSpec((MLA_ROPE, tm), lambda i, h: (0, i % per_b)),
                  pl.BlockSpec((MLA_ROPE, tm), lambda i, h: (0, i % per_b))],
        out_specs=pl.BlockSpec((hb, 1, MLA_DK_PAD, tm), lambda i, h: (h, i, 0, 0)),
        out_shape=jax.ShapeDtypeStruct((heads, t // tm, MLA_DK_PAD, tm), BF16),
        scratch_shapes=[pltpu.VMEM((Q_LORA, tm), BF16)],
        compiler_params=_params("parallel", "arbitrary"),
        name="mla_q_up",
    )(proj_a, g.reshape(1, Q_LORA), wt_q, cos_t, sin_t)


def _kvup_kernel(ckv_ref, g_ref, w_ref, kpe_ref, cos_ref, sin_ref, k_ref, vt_ref, an_ref):
    @pl.when(pl.program_id(1) == 0)
    def _():
        an_ref[...] = _rms(ckv_ref[...].astype(F32), g_ref[...]).astype(BF16)

    an = an_ref[...]
    k_rot = _rope_pair(kpe_ref[...].astype(F32), cos_ref, sin_ref)
    hw = MLA_NOPE + MLA_V
    for hh in range(k_ref.shape[0]):
        res = _dot(an, w_ref[0, :, hh * hw:(hh + 1) * hw])
        k_ref[hh] = jnp.concatenate([res[:, :MLA_NOPE], k_rot], axis=1).astype(k_ref.dtype)
        vt_ref[hh, 0] = _with_ones_rows(res[:, MLA_NOPE:].T).astype(vt_ref.dtype)


def mla_kv_up(proj_a, g, w_kv, layer, cos_p, sin_p, seq):
    t = proj_a.shape[0]
    hw = MLA_NOPE + MLA_V
    heads = w_kv.shape[2] // hw
    hb = UP_HEADS_PER_STEP
    tm = _tile(seq, ATTN_TILE)
    per_b = seq // tm
    return pl.pallas_call(
        _kvup_kernel,
        grid=(t // tm, heads // hb),
        in_specs=[pl.BlockSpec((tm, KV_LORA), lambda i, h: (i, Q_LORA // KV_LORA)),
                  pl.BlockSpec((1, KV_LORA), lambda i, h: (0, 0)),
                  pl.BlockSpec((1, KV_LORA, hb * hw), lambda i, h: (layer, 0, h)),
                  pl.BlockSpec((tm, LANE), lambda i, h: (i, (Q_LORA + KV_LORA) // LANE)),
                  pl.BlockSpec((tm, LANE), lambda i, h: (i % per_b, 0)),
                  pl.BlockSpec((tm, LANE), lambda i, h: (i % per_b, 0))],
        out_specs=[pl.BlockSpec((hb, tm, MLA_DK_PAD), lambda i, h: (h, i, 0)),
                   pl.BlockSpec((hb, 1, MLA_V + ONES_ROWS, tm), lambda i, h: (h, i, 0, 0))],
        out_shape=[jax.ShapeDtypeStruct((heads, t, MLA_DK_PAD), BF16),
                   jax.ShapeDtypeStruct((heads, t // tm, MLA_V + ONES_ROWS, tm), BF16)],
        scratch_shapes=[pltpu.VMEM((tm, KV_LORA), BF16)],
        compiler_params=_params("parallel", "arbitrary"),
        name="mla_kv_up",
    )(proj_a, g.reshape(1, KV_LORA), w_kv, proj_a, cos_p, sin_p)


def _pv(vts, p):
    rows = p.shape[0] // len(vts)
    out = _dot(vts[0], p[:rows].astype(vts[0].dtype))
    for n in range(1, len(vts)):
        out = out + _dot(vts[n], p[n * rows:(n + 1) * rows].astype(vts[n].dtype))
    return out


def _softmax_first(s, vts, m_ref, acc_ref):
    m = jnp.max(s, axis=0, keepdims=True)
    m_ref[...] = m
    acc_ref[...] = _pv(vts, jnp.exp(s - m))


def _softmax_step(s, vts, m_ref, acc_ref):
    m_prev = m_ref[...]
    m_new = jnp.maximum(m_prev, jnp.max(s, axis=0, keepdims=True))
    alpha = jnp.exp(m_prev - m_new)
    acc_ref[...] = alpha * acc_ref[...] + _pv(vts, jnp.exp(s - m_new))
    m_ref[...] = m_new


def _softmax_finish(acc_ref, dv):
    acc = acc_ref[...]
    return (acc[:dv, :] / acc[dv:dv + 1, :]).T


def _causal_mask_t(s):
    key = lax.broadcasted_iota(jnp.int32, s.shape, 0)
    qry = lax.broadcasted_iota(jnp.int32, s.shape, 1)
    return jnp.where(key <= qry, s, NEG_INF)


def _mla_attn_kernel(qt_ref, k_ref, vt_ref, o_ref, m_ref, acc_ref, *, tq, dv):
    i = pl.program_id(2)
    n_h = qt_ref.shape[0]
    qts = [qt_ref[h, 0] for h in range(n_h)]

    def scores(h, j, n):
        off = pl.multiple_of(j * tq, tq)
        return _dot(k_ref[h, pl.ds(off, n * tq), :], qts[h])

    def first(h, n):
        j0 = i - (n - 1)
        s = scores(h, j0, n)
        diag = _causal_mask_t(s[(n - 1) * tq:])
        s = diag if n == 1 else jnp.concatenate([s[:tq], diag], axis=0)
        _softmax_first(s, [vt_ref[h, j0 + m] for m in range(n)], m_ref.at[h], acc_ref.at[h])

    @pl.when(i % 2 == 0)
    def _():
        for h in range(n_h):
            first(h, 1)

    @pl.when(i % 2 == 1)
    def _():
        for h in range(n_h):
            first(h, 2)

    def body(c, carry):
        for h in range(n_h):
            _softmax_step(scores(h, 2 * c, 2), [vt_ref[h, 2 * c], vt_ref[h, 2 * c + 1]],
                          m_ref.at[h], acc_ref.at[h])
        return carry

    lax.fori_loop(0, i // 2, body, 0)
    for h in range(n_h):
        o_ref[:, h * dv:(h + 1) * dv] = _softmax_finish(acc_ref.at[h], dv).astype(o_ref.dtype)


def mla_attention(qt, k, vt, batch, seq, out_dtype):
    heads, _, dk, tq = qt.shape
    t = k.shape[1]
    dvx = vt.shape[2]
    dv = dvx - ONES_ROWS
    nq = seq // tq
    hb = ATTN_HEADS_PER_STEP
    return pl.pallas_call(
        functools.partial(_mla_attn_kernel, tq=tq, dv=dv),
        grid=(batch, heads // hb, nq),
        in_specs=[pl.BlockSpec((hb, 1, dk, tq), lambda b, h, i: (h, b * nq + i, 0, 0)),
                  pl.BlockSpec((hb, seq, dk), lambda b, h, i: (h, b, 0)),
                  pl.BlockSpec((hb, nq, dvx, tq), lambda b, h, i: (h, b, 0, 0))],
        out_specs=pl.BlockSpec((tq, hb * dv), lambda b, h, i: (b * nq + i, h)),
        out_shape=jax.ShapeDtypeStruct((t, heads * dv), out_dtype),
        scratch_shapes=[pltpu.VMEM((hb, 1, tq), F32), pltpu.VMEM((hb, dvx, tq), F32)],
        compiler_params=_params("parallel", "parallel", "arbitrary"),
        name="mla_attention",
    )(qt, k, vt)


def _conv_kernel(b_ref, c_ref, h_ref, w_ref, o_ref, carry_ref, *, per_b):
    i = pl.program_id(0)

    @pl.when(i % per_b == 0)
    def _():
        carry_ref[...] = jnp.zeros_like(carry_ref)

    tm = o_ref.shape[0]
    w0, w1, w2 = w_ref[0:1, :], w_ref[1:2, :], w_ref[2:3, :]
    u = c_ref[...].astype(F32) * h_ref[...].astype(F32)
    y = w0 * pltpu.roll(u, 2, 0) + w1 * pltpu.roll(u, 1, 0) + w2 * u
    o_ref[...] = (b_ref[...].astype(F32) * y).astype(o_ref.dtype)
    u8 = u[0:SUBLANE, :]
    tail = carry_ref[...]
    r8 = lax.broadcasted_iota(jnp.int32, u8.shape, 0)
    p1 = jnp.where(r8 < 1, pltpu.roll(tail, 1, 0), pltpu.roll(u8, 1, 0))
    p2 = jnp.where(r8 < 2, pltpu.roll(tail, 2, 0), pltpu.roll(u8, 2, 0))
    y8 = w0 * p2 + w1 * p1 + w2 * u8
    o_ref[0:SUBLANE, :] = (b_ref[0:SUBLANE, :].astype(F32) * y8).astype(o_ref.dtype)
    carry_ref[...] = u[tm - SUBLANE:tm, :]


def conv_mixer(bch, conv_w, seq, out_dtype):
    t = bch.shape[0]
    ch = conv_w.shape[1]
    tm = _tile(seq, 256)
    per_b = seq // tm
    return pl.pallas_call(
        functools.partial(_conv_kernel, per_b=per_b),
        grid=(t // tm,),
        in_specs=[pl.BlockSpec((tm, ch), lambda i: (i, 0)),
                  pl.BlockSpec((tm, ch), lambda i: (i, 1)),
                  pl.BlockSpec((tm, ch), lambda i: (i, 2)),
                  pl.BlockSpec((CONV_W, ch), lambda i: (0, 0))],
        out_specs=pl.BlockSpec((tm, ch), lambda i: (i, 0)),
        out_shape=jax.ShapeDtypeStruct((t, ch), out_dtype),
        scratch_shapes=[pltpu.VMEM((SUBLANE, ch), F32)],
        compiler_params=_params("arbitrary"),
        name="conv_mixer",
    )(bch, bch, bch, conv_w)


def _moba_prep_kernel(q_ref, k_ref, v_ref, cos_ref, sin_ref, qo_ref, ko_ref, vo_ref, km_ref):
    heads = ko_ref.shape[0]
    cos, sin = cos_ref[...], sin_ref[...]
    n_sub = km_ref.shape[0]
    means = [[] for _ in range(n_sub)]
    for h in range(heads):
        sl = slice(h * MOBA_HD, (h + 1) * MOBA_HD)
        qh = q_ref[:, sl].astype(F32)
        kh = k_ref[:, sl].astype(F32)
        qr = qh * cos + pltpu.roll(qh, MOBA_HD // 2, 1) * sin
        qo_ref[h, 0] = qr.T.astype(qo_ref.dtype)
        kr = kh * cos + pltpu.roll(kh, MOBA_HD // 2, 1) * sin
        ko_ref[h] = kr.astype(ko_ref.dtype)
        vo_ref[h, 0] = _with_ones_rows(v_ref[:, sl].astype(F32).T).astype(vo_ref.dtype)
        for s in range(n_sub):
            means[s].append(jnp.mean(kr[s * MOBA_BLOCK:(s + 1) * MOBA_BLOCK], axis=0, keepdims=True))
    for s in range(n_sub):
        km_ref[s] = jnp.concatenate(means[s], axis=0)


def moba_prep(qkv, cos_f, sin_s, seq):
    t = qkv.shape[0]
    width = qkv.shape[1] // 3
    heads = width // MOBA_HD
    tm = 2 * MOBA_BLOCK
    assert seq % tm == 0
    per_b = seq // tm
    nt = t // tm
    return pl.pallas_call(
        _moba_prep_kernel,
        grid=(nt,),
        in_specs=[pl.BlockSpec((tm, width), lambda i: (i, 0)),
                  pl.BlockSpec((tm, width), lambda i: (i, 1)),
                  pl.BlockSpec((tm, width), lambda i: (i, 2)),
                  pl.BlockSpec((tm, MOBA_HD), lambda i: (i % per_b, 0)),
                  pl.BlockSpec((tm, MOBA_HD), lambda i: (i % per_b, 0))],
        out_specs=[pl.BlockSpec((heads, 1, MOBA_HD, tm), lambda i: (0, i, 0, 0)),
                   pl.BlockSpec((heads, tm, MOBA_HD), lambda i: (0, i, 0)),
                   pl.BlockSpec((heads, 1, MOBA_HD + ONES_ROWS, tm), lambda i: (0, i, 0, 0)),
                   pl.BlockSpec((2, heads, MOBA_HD), lambda i: (i, 0, 0))],
        out_shape=[jax.ShapeDtypeStruct((heads, nt, MOBA_HD, tm), BF16),
                   jax.ShapeDtypeStruct((heads, t, MOBA_HD), BF16),
                   jax.ShapeDtypeStruct((heads, nt, MOBA_HD + ONES_ROWS, tm), BF16),
                   jax.ShapeDtypeStruct((2 * nt, heads, MOBA_HD), F32)],
        compiler_params=_params("arbitrary"),
        name="moba_prep",
    )(qkv, qkv, qkv, cos_f, sin_s)


def _moba_attn_kernel(qt_ref, k_ref, vt_ref, km_ref, o_ref, m_ref, acc_ref, sel_ref, *, scale, dv):
    i = pl.program_id(2)
    n_h = qt_ref.shape[0]
    n_blk = km_ref.shape[2]
    tq = qt_ref.shape[3]
    blk_w = MOBA_BLOCK
    blk = lax.broadcasted_iota(jnp.int32, (n_blk, tq), 0)
    own = 2 * i + (lax.broadcasted_iota(jnp.int32, (n_blk, tq), 1) >= blk_w).astype(jnp.int32)
    past = blk < own
    qts = []
    for h in range(n_h):
        qtf = qt_ref[h, 0].astype(F32)
        g = _dot(km_ref[0, h], qtf, precision=lax.Precision.HIGHEST)
        g = jnp.where(past, g, NEG_INF)
        sel = jnp.zeros(g.shape, F32)
        for _ in range(min(MOBA_TOPK, n_blk)):
            mx = jnp.max(g, axis=0, keepdims=True)
            first = jnp.min(jnp.where(g == mx, blk, n_blk), axis=0, keepdims=True)
            hit = blk == first
            sel = jnp.where(hit, 1.0, sel)
            g = jnp.where(hit, NEG_INF, g)
        sel_ref[h] = jnp.where(past, sel, 0.0)
        qts.append((qtf * scale).astype(k_ref.dtype))

    def scores(h, p, n):
        off = pl.multiple_of(p * tq, tq)
        return _dot(k_ref[h, pl.ds(off, n * tq), :], qts[h])

    def chosen(h, n):
        return sel_ref[h, pl.ds(n, 1), :] > 0.5

    def mask_past(h, s, first_blk):
        parts = [jnp.where(chosen(h, first_blk + n), s[n * blk_w:(n + 1) * blk_w], NEG_INF)
                 for n in range(s.shape[0] // blk_w)]
        return jnp.concatenate(parts, axis=0)

    key = lax.broadcasted_iota(jnp.int32, (blk_w, tq), 0)
    qry = lax.broadcasted_iota(jnp.int32, (blk_w, tq), 1)
    bot_ok = key <= qry - blk_w
    def first(h, n):
        p0 = i - (n - 1)
        s = scores(h, p0, n)
        own = s[(n - 1) * tq:]
        chosen_lim = jnp.where(chosen(h, 2 * i), blk_w, -1)
        top_ok = key <= jnp.where(qry < blk_w, qry, chosen_lim)
        parts = [jnp.where(top_ok, own[:blk_w], NEG_INF), jnp.where(bot_ok, own[blk_w:], NEG_INF)]
        if n == 2:
            parts = [mask_past(h, s[:tq], 2 * p0)] + parts
        _softmax_first(jnp.concatenate(parts, axis=0), [vt_ref[h, p0 + m] for m in range(n)],
                       m_ref.at[h], acc_ref.at[h])

    @pl.when(i % 2 == 0)
    def _():
        for h in range(n_h):
            first(h, 1)

    @pl.when(i % 2 == 1)
    def _():
        for h in range(n_h):
            first(h, 2)

    def body(c, carry):
        for h in range(n_h):
            s = mask_past(h, scores(h, 2 * c, 2), 4 * c)
            _softmax_step(s, [vt_ref[h, 2 * c], vt_ref[h, 2 * c + 1]], m_ref.at[h], acc_ref.at[h])
        return carry

    lax.fori_loop(0, i // 2, body, 0)
    for h in range(n_h):
        o_ref[:, h * dv:(h + 1) * dv] = _softmax_finish(acc_ref.at[h], dv).astype(o_ref.dtype)


def moba_attention(qt, k, vt, k_mean, batch, seq, out_dtype):
    heads, _, hd, tq = qt.shape
    t = k.shape[1]
    dvx = vt.shape[2]
    dv = dvx - ONES_ROWS
    n_blk = seq // MOBA_BLOCK
    nq = seq // tq
    hb = ATTN_HEADS_PER_STEP
    return pl.pallas_call(
        functools.partial(_moba_attn_kernel, scale=hd ** -0.5, dv=dv),
        grid=(batch, heads // hb, nq),
        in_specs=[pl.BlockSpec((hb, 1, hd, tq), lambda b, h, i: (h, b * nq + i, 0, 0)),
                  pl.BlockSpec((hb, seq, hd), lambda b, h, i: (h, b, 0)),
                  pl.BlockSpec((hb, nq, dvx, tq), lambda b, h, i: (h, b, 0, 0)),
                  pl.BlockSpec((1, hb, n_blk, hd), lambda b, h, i: (b, h, 0, 0))],
        out_specs=pl.BlockSpec((tq, hb * dv), lambda b, h, i: (b * nq + i, h)),
        out_shape=jax.ShapeDtypeStruct((t, heads * dv), out_dtype),
        scratch_shapes=[pltpu.VMEM((hb, 1, tq), F32), pltpu.VMEM((hb, dvx, tq), F32),
                        pltpu.VMEM((hb, n_blk, tq), F32)],
        compiler_params=_params("parallel", "parallel", "arbitrary"),
        name="moba_attention",
    )(qt, k, vt, k_mean)


def _out_proj_kernel(a_ref, b_ref, c_ref, ga_ref, gb_ref, gc_ref, w_ref, x_ref, gate_ref, o_ref,
                     y_ref):
    @pl.when(pl.program_id(1) == 0)
    def _():
        wa, wb = a_ref.shape[1], b_ref.shape[1]
        y_ref[:, 0:wa] = _rms(a_ref[...].astype(F32), ga_ref[...]).astype(y_ref.dtype)
        y_ref[:, wa:wa + wb] = _rms(b_ref[...].astype(F32), gb_ref[...]).astype(y_ref.dtype)
        y_ref[:, wa + wb:] = _rms(c_ref[...].astype(F32), gc_ref[...]).astype(y_ref.dtype)

    o_ref[...] = x_ref[...] + gate_ref[0] * _dot(y_ref[...], w_ref[0])


def out_proj(y_a, y_b, y_c, g, w_layers, layer, x2, gate, seq, tm=1024, tn=512):
    t = y_a.shape[0]
    wa, wb, wc = y_a.shape[1], y_b.shape[1], y_c.shape[1]
    _, kd, n = w_layers.shape
    tm, tn = _tile(seq, tm), _tile(n, tn)
    per_b = seq // tm
    ga, gb, gc = g[:wa].reshape(1, wa), g[wa:wa + wb].reshape(1, wb), g[wa + wb:].reshape(1, wc)
    rows = lambda width: pl.BlockSpec((tm, width), lambda i, j: (i, 0))
    vec = lambda width: pl.BlockSpec((1, width), lambda i, j: (0, 0))
    return pl.pallas_call(
        _out_proj_kernel,
        grid=(t // tm, n // tn),
        in_specs=[rows(wa), rows(wb), rows(wc), vec(wa), vec(wb), vec(wc),
                  pl.BlockSpec((1, kd, tn), lambda i, j: (layer, 0, j)),
                  pl.BlockSpec((tm, tn), lambda i, j: (i, j)),
                  pl.BlockSpec((1, 1, tn), lambda i, j: (i // per_b, 0, j))],
        out_specs=pl.BlockSpec((tm, tn), lambda i, j: (i, j)),
        out_shape=jax.ShapeDtypeStruct((t, n), F32),
        scratch_shapes=[pltpu.VMEM((tm, kd), BF16)],
        compiler_params=_params("parallel", "arbitrary"),
        name="out_proj",
    )(y_a, y_b, y_c, ga, gb, gc, w_layers, x2, gate)


def _expert_weight_copies(w_hbm, layer, expert, col, stage, sem):
    width = stage.shape[1]
    return pltpu.make_async_copy(w_hbm.at[layer, expert, :, pl.ds(col, width)], stage, sem)


def _ffn_up_kernel(bexp_ref, first_ref, rows_ref, next_ref, x_ref, wg_hbm, wu_hbm, a_ref,
                   wg_st, wu_st, wg_bf, wu_bf, sem, *, layer):
    j = pl.program_id(0)
    b = pl.program_id(1)
    n_rows = rows_ref[b]
    half = x_ref.shape[0] // 2
    col = pl.multiple_of(j * wg_bf.shape[1], wg_bf.shape[1])

    def copies(expert):
        return (_expert_weight_copies(wg_hbm, layer, expert, col, wg_st, sem.at[0]),
                _expert_weight_copies(wu_hbm, layer, expert, col, wu_st, sem.at[1]))

    @pl.when(b == 0)
    def _():
        for c in copies(bexp_ref[0]):
            c.start()

    @pl.when((n_rows > 0) & (first_ref[b] == 1))
    def _():
        for c in copies(bexp_ref[b]):
            c.wait()
        wg_bf[...] = wg_st[...].astype(BF16)
        wu_bf[...] = wu_st[...].astype(BF16)

        @pl.when(next_ref[b] >= 0)
        def _():
            for c in copies(next_ref[b]):
                c.start()

    def act(x):
        gate = _dot(x, wg_bf[...])
        up = _dot(x, wu_bf[...])
        return (gate * jax.nn.sigmoid(gate) * up).astype(a_ref.dtype)

    @pl.when(n_rows > half)
    def _():
        a_ref[...] = act(x_ref[...])

    @pl.when((n_rows > 0) & (n_rows <= half))
    def _():
        a_ref[:half] = act(x_ref[:half])
        a_ref[half:] = jnp.zeros((half, a_ref.shape[1]), a_ref.dtype)

    @pl.when(n_rows == 0)
    def _():
        a_ref[...] = jnp.zeros_like(a_ref)


def _ffn_down_kernel(bexp_ref, first_ref, rows_ref, next_ref, a_ref, wd_hbm, o_ref,
                     wd_st, wd_bf, sem, *, layer):
    j = pl.program_id(0)
    b = pl.program_id(1)
    n_rows = rows_ref[b]
    half = a_ref.shape[0] // 2
    col = pl.multiple_of(j * wd_bf.shape[1], wd_bf.shape[1])

    def copy(expert):
        return _expert_weight_copies(wd_hbm, layer, expert, col, wd_st, sem.at[0])

    @pl.when(b == 0)
    def _():
        copy(bexp_ref[0]).start()

    @pl.when((n_rows > 0) & (first_ref[b] == 1))
    def _():
        copy(bexp_ref[b]).wait()
        wd_bf[...] = wd_st[...].astype(BF16)

        @pl.when(next_ref[b] >= 0)
        def _():
            copy(next_ref[b]).start()

    def packed(y):
        mid = y.shape[1] // 2
        return _pack_bf16_pair(y[:, :mid], y[:, mid:])

    @pl.when(n_rows > half)
    def _():
        o_ref[...] = packed(_dot(a_ref[...], wd_bf[...]))

    @pl.when((n_rows > 0) & (n_rows <= half))
    def _():
        o_ref[:half] = packed(_dot(a_ref[:half], wd_bf[...]))
        o_ref[half:] = jnp.zeros((half, o_ref.shape[1]), o_ref.dtype)

    @pl.when(n_rows == 0)
    def _():
        o_ref[...] = jnp.zeros_like(o_ref)


def expert_ffn(xs, block_exp, block_first, block_rows, block_next, w_gate, w_up, w_down, layer):
    n_slots, d = xs.shape
    ff = w_gate.shape[3]
    fh, dh = ff // FF_SPLIT, d // FF_SPLIT
    n_blocks = n_slots // MOE_BLOCK
    hbm = pl.BlockSpec(memory_space=pl.ANY)
    up_spec = pltpu.PrefetchScalarGridSpec(
        num_scalar_prefetch=4,
        grid=(FF_SPLIT, n_blocks),
        in_specs=[pl.BlockSpec((MOE_BLOCK, d), lambda j, b, *_: (b, 0)), hbm, hbm],
        out_specs=pl.BlockSpec((MOE_BLOCK, fh), lambda j, b, *_: (b, j)),
        scratch_shapes=[pltpu.VMEM((d, fh), F32), pltpu.VMEM((d, fh), F32),
                        pltpu.VMEM((d, fh), BF16), pltpu.VMEM((d, fh), BF16),
                        pltpu.SemaphoreType.DMA((2,))],
    )
    act = pl.pallas_call(
        functools.partial(_ffn_up_kernel, layer=layer),
        grid_spec=up_spec,
        out_shape=jax.ShapeDtypeStruct((n_slots, ff), BF16),
        compiler_params=_params("arbitrary", "arbitrary"),
        name="expert_ffn_up",
    )(block_exp, block_first, block_rows, block_next, xs, w_gate, w_up)
    down_spec = pltpu.PrefetchScalarGridSpec(
        num_scalar_prefetch=4,
        grid=(FF_SPLIT, n_blocks),
        in_specs=[pl.BlockSpec((MOE_BLOCK, ff), lambda j, b, *_: (b, 0)), hbm],
        out_specs=pl.BlockSpec((MOE_BLOCK, dh // 2), lambda j, b, *_: (b, j)),
        scratch_shapes=[pltpu.VMEM((ff, dh), F32), pltpu.VMEM((ff, dh), BF16),
                        pltpu.SemaphoreType.DMA((1,))],
    )
    return pl.pallas_call(
        functools.partial(_ffn_down_kernel, layer=layer),
        grid_spec=down_spec,
        out_shape=jax.ShapeDtypeStruct((n_slots, d // 2), PACKED),
        compiler_params=_params("arbitrary", "arbitrary"),
        name="expert_ffn_down",
    )(block_exp, block_first, block_rows, block_next, act, w_down)


def _start_row_copies(idx_ref, base, n_rows, src_ref, dst_ref, sem):
    def issue(pair, carry):
        for prio in range(2):
            r = 2 * pair + prio
            row = idx_ref[base + r]
            pltpu.make_async_copy(src_ref.at[pl.ds(row, 1)], dst_ref.at[pl.ds(r, 1)],
                                  sem).start(priority=prio)
        return carry

    lax.fori_loop(0, n_rows // 2, issue, 0, unroll=4)


def _wait_row_copies(n_rows, src_ref, dst_ref, sem):
    pltpu.make_async_copy(src_ref.at[pl.ds(0, n_rows)], dst_ref, sem).wait()


def _gather_cast_kernel(idx_ref, nused_ref, src_ref, o_ref, buf, sem):
    b = pl.program_id(0)
    rows = o_ref.shape[0]
    n_used = nused_ref[0]

    @pl.when((b == 0) & (n_used > 0))
    def _():
        _start_row_copies(idx_ref, 0, rows, src_ref, buf.at[0], sem.at[0])

    @pl.when(b + 1 < n_used)
    def _():
        nxt = (b + 1) % 2
        _start_row_copies(idx_ref, (b + 1) * rows, rows, src_ref, buf.at[nxt], sem.at[nxt])

    @pl.when(b < n_used)
    def _():
        cur = b % 2
        _wait_row_copies(rows, src_ref, buf.at[cur], sem.at[cur])
        lo, hi = _unpack_bf16_pair(buf[cur])
        half = lo.shape[1]
        o_ref[:, :half] = lo.astype(o_ref.dtype)
        o_ref[:, half:] = hi.astype(o_ref.dtype)

    @pl.when(b >= n_used)
    def _():
        o_ref[...] = jnp.zeros_like(o_ref)


def gather_rows_cast(src, idx, n_used_tiles, out_dtype):
    dp = src.shape[1]
    d = 2 * dp
    m = idx.shape[0]
    rows = GATHER_ROWS
    grid_spec = pltpu.PrefetchScalarGridSpec(
        num_scalar_prefetch=2,
        grid=(m // rows,),
        in_specs=[pl.BlockSpec(memory_space=pl.ANY)],
        out_specs=pl.BlockSpec((rows, d), lambda b, ix, nu: (b, 0)),
        scratch_shapes=[pltpu.VMEM((2, rows, dp), src.dtype), pltpu.SemaphoreType.DMA((2,))],
    )
    return pl.pallas_call(
        _gather_cast_kernel,
        grid_spec=grid_spec,
        out_shape=jax.ShapeDtypeStruct((m, d), out_dtype),
        compiler_params=_params("arbitrary"),
        name="gather_rows_cast",
    )(idx, n_used_tiles, src)


def _combine_kernel(pos_ref, x_ref, gate_ref, w_ref, g_ref, sc_ref, sh_ref, ys_ref,
                    *rest, n_tok, n_windows, final):
    if final:
        o_ref, ybuf, sem = rest
    else:
        o_ref, h_ref, ybuf, sem = rest
    i = pl.program_id(0)
    n_tiles = pl.num_programs(0)
    tm = x_ref.shape[0]

    def start_tile(tile, slot):
        for kk in range(TOP_K):
            _start_row_copies(pos_ref, kk * n_tok + tile * tm, tm, ys_ref, ybuf.at[slot, kk],
                              sem.at[slot])

    @pl.when(i == 0)
    def _():
        start_tile(0, 0)

    @pl.when(i + 1 < n_tiles)
    def _():
        start_tile(i + 1, (i + 1) % 2)

    cur = i % 2
    for kk in range(TOP_K):
        _wait_row_copies(tm, ys_ref, ybuf.at[cur, kk], sem.at[cur])
    w = w_ref[...]
    acc = None
    for kk in range(TOP_K):
        lo, hi = _unpack_bf16_pair(ybuf[cur, kk])
        ww = lo.shape[1] // n_windows
        parts = []
        for n in range(n_windows):
            parts += [lo[:, n * ww:(n + 1) * ww], hi[:, n * ww:(n + 1) * ww]]
        term = w[:, kk:kk + 1] * jnp.concatenate(parts, axis=1)
        acc = term if acc is None else acc + term
    x_new = x_ref[...] + gate_ref[0] * acc
    if final:
        o_ref[...] = _rms(x_new, g_ref[...])
    else:
        o_ref[...] = x_new
        h_ref[...] = (_rms(x_new, g_ref[...]) * sc_ref[0] + sh_ref[0]).astype(h_ref.dtype)


def moe_combine(x2, gate, ys, pos, w_tok, seq, norm_g, sc1p=None, sh=None):
    t, d = x2.shape
    tm = _tile(seq, GATHER_ROWS)
    per_b = seq // tm
    final = sc1p is None
    if final:
        sc1p = sh = jnp.zeros((x2.shape[0] // seq, 1, d), F32)
    row = pl.BlockSpec((tm, d), lambda i, ps: (i, 0))
    mod = pl.BlockSpec((1, 1, d), lambda i, ps: (i // per_b, 0, 0))
    tok = pl.BlockSpec((tm, TOP_K), lambda i, ps: (i, 0))
    grid_spec = pltpu.PrefetchScalarGridSpec(
        num_scalar_prefetch=1,
        grid=(t // tm,),
        in_specs=[row, mod, tok, pl.BlockSpec((1, d), lambda i, ps: (0, 0)), mod, mod,
                  pl.BlockSpec(memory_space=pl.ANY)],
        out_specs=row if final else [row, row],
        scratch_shapes=[pltpu.VMEM((2, TOP_K, tm, d // 2), ys.dtype),
                        pltpu.SemaphoreType.DMA((2,))],
    )
    out_shape = jax.ShapeDtypeStruct((t, d), F32)
    return pl.pallas_call(
        functools.partial(_combine_kernel, n_tok=t, n_windows=FF_SPLIT, final=final),
        grid_spec=grid_spec,
        out_shape=out_shape if final else [out_shape, jax.ShapeDtypeStruct((t, d), BF16)],
        compiler_params=_params("arbitrary"),
        name="moe_combine",
    )(pos, x2, gate, w_tok, norm_g.reshape(1, d), sc1p, sh, ys)


def _group_by_expert(e_idx):
    n_tok = e_idx.shape[1]
    n_asg = n_tok * TOP_K
    flat_e = e_idx.reshape(n_asg)
    onehot = (flat_e[:, None] == jnp.arange(N_EXPERTS)[None, :]).astype(jnp.int32)
    running = jnp.cumsum(onehot, axis=0)
    rank = jnp.sum(running * onehot, axis=1) - 1
    sizes = running[-1]
    padded = (sizes + MOE_BLOCK - 1) // MOE_BLOCK * MOE_BLOCK
    pad_end = jnp.cumsum(padded)
    pad_start = pad_end - padded
    dest = (jnp.sum(pad_start[None, :] * onehot, axis=1) + rank).astype(jnp.int32)
    n_blocks = -(-n_asg // MOE_BLOCK) + N_EXPERTS
    n_slots = n_blocks * MOE_BLOCK
    slot_tok = (jnp.arange(n_slots, dtype=jnp.int32) % n_tok).at[dest].set(
        jnp.arange(n_asg, dtype=jnp.int32) % n_tok, mode="promise_in_bounds")
    block_start = jnp.arange(n_blocks, dtype=jnp.int32) * MOE_BLOCK
    block_exp = jnp.minimum(jnp.sum(block_start[:, None] >= pad_end[None, :], axis=1),
                            N_EXPERTS - 1).astype(jnp.int32)
    n_used = (pad_end[-1] // MOE_BLOCK).astype(jnp.int32)
    block_exp = jnp.where(jnp.arange(n_blocks) < n_used, block_exp, block_exp[n_used - 1])
    prev_exp = jnp.concatenate([jnp.full((1,), -1, jnp.int32), block_exp[:-1]])
    block_first = (block_exp != prev_exp).astype(jnp.int32)
    block_rows = jnp.clip(pad_start[block_exp] + sizes[block_exp] - block_start, 0, MOE_BLOCK)
    block_rows = jnp.where(jnp.arange(n_blocks) < n_used, block_rows, 0).astype(jnp.int32)
    blk = jnp.arange(n_blocks, dtype=jnp.int32)
    starts = jnp.where((block_first == 1) & (block_rows > 0), blk, n_blocks)
    later = lax.cummin(jnp.concatenate([starts[1:], jnp.full((1,), n_blocks, jnp.int32)]),
                       axis=0, reverse=True)
    block_next = jnp.where(later < n_blocks, block_exp[jnp.minimum(later, n_blocks - 1)],
                           -1).astype(jnp.int32)
    return (dest.reshape(TOP_K, n_tok), slot_tok, block_exp, block_first, block_rows, block_next,
            n_used.reshape(1))


def _rot_cols(w):
    half = w.shape[-1] // 2
    return jnp.concatenate([-w[..., half:], w[..., :half]], axis=-1)


IN_ROW_BLOCK = 64


def _split_in_weights_kernel(wt_ref, a_ref, b_ref, c_ref, *, a_blocks, b_blocks, c_blocks):
    g = pl.program_id(1)
    w = wt_ref[0]

    @pl.when(g < a_blocks)
    def _():
        a_ref[0] = w.astype(a_ref.dtype)

    @pl.when((g >= a_blocks) & (g < a_blocks + b_blocks))
    def _():
        b_ref[0] = w.astype(b_ref.dtype)

    @pl.when((g >= a_blocks + b_blocks) & (g < a_blocks + b_blocks + c_blocks))
    def _():
        c_ref[0] = w.astype(c_ref.dtype)

    @pl.when(g == a_blocks + b_blocks + c_blocks)
    def _():
        half = w.shape[0] // 2
        a_ref[0] = jnp.concatenate([-w[half:], w[:half]], axis=0).astype(a_ref.dtype)


def split_in_weights(w_in):
    layers, d, n_in = w_in.shape
    wt = jnp.swapaxes(w_in, 1, 2)
    rb = IN_ROW_BLOCK
    assert MLA_ROPE == rb
    a_blocks = (Q_LORA + KV_LORA + MLA_ROPE) // rb
    b_blocks = 3 * CONV_CH // rb
    c_blocks = n_in // rb - a_blocks - b_blocks
    last = a_blocks + b_blocks + c_blocks

    def src(l, g):
        return (l, jnp.where(g == last, a_blocks - 1, g), 0)

    def dst_a(l, g):
        return (l, jnp.where(g == last, a_blocks, jnp.minimum(g, a_blocks - 1)), 0)

    def dst_b(l, g):
        return (l, jnp.clip(g - a_blocks, 0, b_blocks - 1), 0)

    def dst_c(l, g):
        return (l, jnp.clip(g - a_blocks - b_blocks, 0, c_blocks - 1), 0)

    blk = (1, rb, d)
    return pl.pallas_call(
        functools.partial(_split_in_weights_kernel, a_blocks=a_blocks, b_blocks=b_blocks,
                          c_blocks=c_blocks),
        grid=(layers, last + 1),
        in_specs=[pl.BlockSpec(blk, src)],
        out_specs=[pl.BlockSpec(blk, dst_a), pl.BlockSpec(blk, dst_b), pl.BlockSpec(blk, dst_c)],
        out_shape=[jax.ShapeDtypeStruct((layers, nb * rb, d), BF16)
                   for nb in (a_blocks + 1, b_blocks, c_blocks)],
        compiler_params=_params("arbitrary", "arbitrary"),
        name="split_in_weights",
    )(wt)


def _prep_weights(w_in, w_uq, w_ukv, w_out):
    layers = w_in.shape[0]
    w_a, w_b, w_c = split_in_weights(w_in)
    wq = w_uq.reshape(layers, Q_LORA, MLA_HEADS, MLA_NOPE + MLA_ROPE)
    wq_rope = wq[..., MLA_NOPE:]
    wq = jnp.concatenate([wq, _rot_cols(wq_rope)], axis=-1).transpose(0, 2, 3, 1).astype(BF16)
    return dict(w_a=w_a, w_b=w_b, w_c=w_c, w_q=wq, w_kv=w_ukv.astype(BF16),
                w_out=w_out.astype(BF16))


def _rope_tables(seq, dim):
    inv = 1.0 / (ROPE_THETA ** (jnp.arange(0, dim, 2, dtype=F32) / dim))
    ang = jnp.arange(seq, dtype=F32)[:, None] * inv[None, :]
    return jnp.cos(ang), jnp.sin(ang)


def kernel(x, c, w_mod, mod_table, mix_norm_g, w_in, q_norm_g, kv_norm_g, w_uq, w_ukv, conv_w,
           group_norm_g, w_out, ffn_norm_g, w_router, router_bias, w_gate, w_up, w_down,
           final_norm_g):
    batch, seq, d = x.shape
    depth = w_in.shape[0]
    t = batch * seq
    x2 = x.reshape(t, d)

    cos_a, sin_a = _rope_tables(seq, MLA_ROPE)
    zeros_a = jnp.zeros_like(cos_a)
    cos_p = jnp.concatenate([cos_a, cos_a, zeros_a, zeros_a], axis=1)
    cos_t = jnp.concatenate([cos_a, cos_a], axis=1).T
    sin_t = jnp.concatenate([sin_a, sin_a], axis=1).T
    sin_p = jnp.concatenate([sin_a, sin_a, zeros_a, zeros_a], axis=1)
    cos_b, sin_b = _rope_tables(seq, MOBA_HD)
    cos_f = jnp.concatenate([cos_b, cos_b], axis=1)
    sin_s = jnp.concatenate([-sin_b, sin_b], axis=1)

    c_pad = jnp.zeros((SUBLANE, d), F32).at[:batch].set(c)
    mod_shared = mod_matmul(c_pad, w_mod)[:batch].reshape(batch, N_MOD, d)

    wr_t = w_router.T
    wr_hi = wr_t.astype(BF16)
    wr_lo = (wr_t - wr_hi.astype(F32)).astype(BF16)
    router = (wr_hi, wr_lo, router_bias.astype(F32).reshape(N_EXPERTS, 1))
    mla_scale = (MLA_NOPE + MLA_ROPE) ** -0.5

    mods = []
    for l in range(depth):
        mod = mod_shared + mod_table[l][None]
        mods.append([mod[:, i][:, None, :] for i in range(N_MOD)])

    p = _prep_weights(w_in, w_uq, w_ukv, w_out)
    h = norm_mod(x2, mix_norm_g[0], 1.0 + mods[0][1], mods[0][0], seq)
    for l in range(depth):
        sh1, sc1, g1, sh2, sc2, g2 = mods[l]

        proj_a = matmul(h, p["w_a"], l, BF16, tm=512)
        bch = matmul(h, p["w_b"], l, BF16)
        qkv = matmul(h, p["w_c"], l, BF16)

        q_a = mla_q_up(proj_a, q_norm_g[l], p["w_q"], l, cos_t, sin_t, seq, mla_scale)
        k_a, v_a = mla_kv_up(proj_a, kv_norm_g[l], p["w_kv"], l, cos_p, sin_p, seq)
        y_a = mla_attention(q_a, k_a, v_a, batch, seq, BF16)

        y_b = conv_mixer(bch, conv_w[l], seq, F32)

        q_c, k_c, v_c, k_mean = moba_prep(qkv, cos_f, sin_s, seq)
        n_blk = seq // MOBA_BLOCK
        k_mean = k_mean.reshape(batch, n_blk, MOBA_HEADS, MOBA_HD).transpose(0, 2, 1, 3)
        y_c = moba_attention(q_c, k_c, v_c, k_mean, batch, seq, BF16)

        x2 = out_proj(y_a, y_b, y_c, group_norm_g[l], p["w_out"], l, x2, g1, seq)

        h2, e_idx, gates = norm_mod(x2, ffn_norm_g[l], 1.0 + sc2, sh2, seq, router=router)
        pos, slot_tok, block_exp, block_first, block_rows, block_next, n_used = _group_by_expert(e_idx)
        xs = gather_rows_cast(h2, slot_tok, n_used * (MOE_BLOCK // GATHER_ROWS), BF16)
        ys = expert_ffn(xs, block_exp, block_first, block_rows, block_next, w_gate, w_up, w_down,
                        l)
        if l + 1 < depth:
            nxt = mods[l + 1]
            x2, h = moe_combine(x2, g2, ys, pos.reshape(-1), gates.T, seq, mix_norm_g[l + 1],
                                1.0 + nxt[1], nxt[0])
        else:
            out = moe_combine(x2, g2, ys, pos.reshape(-1), gates.T, seq, final_norm_g)

    return out.reshape(batch, seq, d)
```
